```python
import math
import jax, jax.numpy as jnp
from jax import lax
import numpy as np

D_MODEL = 1024
BATCH = 8
SEQ = 16384
DEPTH = 1

CHUNK = 64

D_MIX = D_MODEL
ATTN_HEADS = 8
HEAD_DIM = 64
D_ATTN = ATTN_HEADS * HEAD_DIM
D_CONV = D_MIX - D_ATTN
CONV_WIDTH = 3
Q_BLOCK = 128
D_IN_PROJ = 3 * D_ATTN + ATTN_HEADS + 3 * D_CONV

D_FF = 2816
PLE_DIM = 256
LN_EPS = 1e-5
RMS_EPS = 1e-6
NEG_INF = -1e30

DEEPNORM_ALPHA = (2.0 * DEPTH) ** 0.25
DEEPNORM_BETA = (8.0 * DEPTH) ** -0.25

kernel_name = "hybrid_fox_shortconv_macaron_deepnorm"


def _layer_norm(x, g, b):
    xf = x.astype(jnp.float32)
    mu = jnp.mean(xf, axis=-1, keepdims=True)
    xc = xf - mu
    var = jnp.mean(xc * xc, axis=-1, keepdims=True)
    y = xc * lax.rsqrt(var + LN_EPS) * g.astype(jnp.float32) + b.astype(jnp.float32)
    return y.astype(x.dtype)


def _rms_norm(x, g):
    xf = x.astype(jnp.float32)
    ms = jnp.mean(xf * xf, axis=-1, keepdims=True)
    return (xf * lax.rsqrt(ms + RMS_EPS) * g.astype(jnp.float32)).astype(x.dtype)


def _swiglu(x, w_in, w_out):
    gu = x @ w_in
    gate, up = jnp.split(gu, 2, axis=-1)
    return (jax.nn.silu(gate) * up) @ w_out


def _forgetting_attention(q, k, v, log_f):
    b, s, h, dh = q.shape
    nb = s // Q_BLOCK
    scale = 1.0 / math.sqrt(dh)
    c = jnp.cumsum(log_f, axis=1).transpose(0, 2, 1)
    qh = q.transpose(0, 2, 1, 3)
    kh = k.transpose(0, 2, 1, 3)
    vh = v.transpose(0, 2, 1, 3)
    q_blocks = qh.reshape(b, h, nb, Q_BLOCK, dh).transpose(2, 0, 1, 3, 4)
    c_blocks = c.reshape(b, h, nb, Q_BLOCK).transpose(2, 0, 1, 3)
    key_pos = jnp.arange(s)

    def one_block(args):
        qb, cb, blk = args
        q_pos = blk * Q_BLOCK + jnp.arange(Q_BLOCK)
        logits = jnp.einsum('bhqd,bhkd->bhqk', qb, kh,
                            preferred_element_type=jnp.float32) * scale
        logits = logits + cb[..., None] - c[:, :, None, :]
        logits = jnp.where(q_pos[:, None] >= key_pos[None, :], logits, NEG_INF)
        probs = jax.nn.softmax(logits, axis=-1)
        return jnp.einsum('bhqk,bhkd->bhqd', probs.astype(vh.dtype), vh)

    out = lax.map(one_block, (q_blocks, c_blocks, jnp.arange(nb)))
    return out.transpose(1, 0, 3, 2, 4).reshape(b, s, h * dh)


def _short_gated_conv(gate_b, gate_c, h_in, conv_w):
    u = gate_c * h_in
    y = lax.conv_general_dilated(
        u, conv_w[:, None, :].astype(u.dtype), window_strides=(1,),
        padding=[(CONV_WIDTH - 1, 0)],
        dimension_numbers=('NWC', 'WIO', 'NWC'),
        feature_group_count=D_CONV)
    return gate_b * y


def _hybrid_mixer(x, w_mix_in, b_forget, conv_w, g_attn, g_conv, w_mix_out):
    b, s, _ = x.shape
    proj = x @ w_mix_in
    o = 0
    q = proj[..., o:o + D_ATTN]; o += D_ATTN
    k = proj[..., o:o + D_ATTN]; o += D_ATTN
    v = proj[..., o:o + D_ATTN]; o += D_ATTN
    f_logit = proj[..., o:o + ATTN_HEADS]; o += ATTN_HEADS
    gate_b = proj[..., o:o + D_CONV]; o += D_CONV
    gate_c = proj[..., o:o + D_CONV]; o += D_CONV
    h_in = proj[..., o:o + D_CONV]

    log_f = jax.nn.log_sigmoid((f_logit + b_forget).astype(jnp.float32))
    attn = _forgetting_attention(
        q.reshape(b, s, ATTN_HEADS, HEAD_DIM),
        k.reshape(b, s, ATTN_HEADS, HEAD_DIM),
        v.reshape(b, s, ATTN_HEADS, HEAD_DIM),
        log_f)
    conv = _short_gated_conv(gate_b, gate_c, h_in, conv_w)

    merged = jnp.concatenate([_rms_norm(attn, g_attn), _rms_norm(conv, g_conv)], axis=-1)
    return merged @ w_mix_out


def _fwd_setup_inputs(seed: int = 0) -> dict:
    key = jax.random.key(seed)
    ks = jax.random.split(key, 26)
    L, D, F = DEPTH, D_MODEL, D_FF
    f32 = jnp.float32

    def nrm(k, shape, scale):
        return jax.random.normal(k, shape, f32) * scale

    def gain(k):
        return 1.0 + 0.02 * jax.random.normal(k, (L, D), f32)

    def bias(k, n):
        return 0.02 * jax.random.normal(k, (L, n), f32)

    b_forget = (jnp.linspace(1.0, 5.0, ATTN_HEADS, dtype=f32)[None, :]
                + 0.1 * jax.random.normal(ks[6], (L, ATTN_HEADS), f32))

    return {
        "x": nrm(ks[0], (BATCH, SEQ, D), 1.0),
        "p": nrm(ks[1], (DEPTH, BATCH, SEQ, PLE_DIM), 1.0),
        "ffn1_w_in": nrm(ks[2], (L, D, 2 * F), D ** -0.5),
        "ffn1_w_out": nrm(ks[3], (L, F, D), F ** -0.5 * DEEPNORM_BETA),
        "ln1_g": gain(ks[4]), "ln1_b": bias(ks[5], D),
        "w_mix_in": nrm(ks[7], (L, D, D_IN_PROJ), D ** -0.5),
        "b_forget": b_forget,
        "conv_w": nrm(ks[8], (L, CONV_WIDTH, D_CONV), CONV_WIDTH ** -0.5),
        "g_attn": 1.0 + 0.02 * jax.random.normal(ks[9], (L, D_ATTN), f32),
        "g_conv": 1.0 + 0.02 * jax.random.normal(ks[10], (L, D_CONV), f32),
        "w_mix_out": nrm(ks[11], (L, D_MIX, D), D_MIX ** -0.5 * DEEPNORM_BETA),
        "ln2_g": gain(ks[12]), "ln2_b": bias(ks[13], D),
        "ffn2_w_in": nrm(ks[14], (L, D, 2 * F), D ** -0.5),
        "ffn2_w_out": nrm(ks[15], (L, F, D), F ** -0.5 * DEEPNORM_BETA),
        "ln3_g": gain(ks[16]), "ln3_b": bias(ks[17], D),
        "w_ple": nrm(ks[18], (L, PLE_DIM, D), PLE_DIM ** -0.5 * DEEPNORM_BETA),
        "w_ple_gate": nrm(ks[19], (L, D, D), D ** -0.5),
        "b_ple_gate": bias(ks[20], D),
        "ln4_g": gain(ks[21]), "ln4_b": bias(ks[22], D),
    }


def _fwd_reference(x, p, ffn1_w_in, ffn1_w_out, ln1_g, ln1_b, w_mix_in, b_forget, conv_w,
              g_attn, g_conv, w_mix_out, ln2_g, ln2_b, ffn2_w_in, ffn2_w_out, ln3_g, ln3_b,
              w_ple, w_ple_gate, b_ple_gate, ln4_g, ln4_b):
    a = DEEPNORM_ALPHA
    for i in range(DEPTH):
        x = _layer_norm(a * x + 0.5 * _swiglu(x, ffn1_w_in[i], ffn1_w_out[i]), ln1_g[i], ln1_b[i])
        mix = _hybrid_mixer(x, w_mix_in[i], b_forget[i], conv_w[i], g_attn[i], g_conv[i], w_mix_out[i])
        x = _layer_norm(a * x + mix, ln2_g[i], ln2_b[i])
        x = _layer_norm(a * x + 0.5 * _swiglu(x, ffn2_w_in[i], ffn2_w_out[i]), ln3_g[i], ln3_b[i])
        gate = jax.nn.sigmoid(x @ w_ple_gate[i] + b_ple_gate[i])
        x = _layer_norm(a * x + gate * (p[i] @ w_ple[i]), ln4_g[i], ln4_b[i])
    return x


import jax as _jax
import jax.numpy as _jnp

TWIN_FORMAT = 'train_step'
FWD_PARAMS = ['x', 'p', 'ffn1_w_in', 'ffn1_w_out', 'ln1_g', 'ln1_b', 'w_mix_in', 'b_forget', 'conv_w', 'g_attn', 'g_conv', 'w_mix_out', 'ln2_g', 'ln2_b', 'ffn2_w_in', 'ffn2_w_out', 'ln3_g', 'ln3_b', 'w_ple', 'w_ple_gate', 'b_ple_gate', 'ln4_g', 'ln4_b']
TWIN_WEIGHTS = ['ffn1_w_in', 'ffn1_w_out', 'ln1_g', 'ln1_b', 'w_mix_in', 'b_forget', 'conv_w', 'g_attn', 'g_conv', 'w_mix_out', 'ln2_g', 'ln2_b', 'ffn2_w_in', 'ffn2_w_out', 'ln3_g', 'ln3_b', 'w_ple', 'w_ple_gate', 'b_ple_gate', 'ln4_g', 'ln4_b']
TWIN_DIFF_INPUT = 'x'
TWIN_INPUTS = ['x', 'p', 'ffn1_w_in', 'ffn1_w_out', 'ln1_g', 'ln1_b', 'w_mix_in', 'b_forget', 'conv_w', 'g_attn', 'g_conv', 'w_mix_out', 'ln2_g', 'ln2_b', 'ffn2_w_in', 'ffn2_w_out', 'ln3_g', 'ln3_b', 'w_ple', 'w_ple_gate', 'b_ple_gate', 'ln4_g', 'ln4_b', 'loss_target', 'm_ffn1_w_in', 'm_ffn1_w_out', 'm_ln1_g', 'm_ln1_b', 'm_w_mix_in', 'm_b_forget', 'm_conv_w', 'm_g_attn', 'm_g_conv', 'm_w_mix_out', 'm_ln2_g', 'm_ln2_b', 'm_ffn2_w_in', 'm_ffn2_w_out', 'm_ln3_g', 'm_ln3_b', 'm_w_ple', 'm_w_ple_gate', 'm_b_ple_gate', 'm_ln4_g', 'm_ln4_b', 'v_ffn1_w_in', 'v_ffn1_w_out', 'v_ln1_g', 'v_ln1_b', 'v_w_mix_in', 'v_b_forget', 'v_conv_w', 'v_g_attn', 'v_g_conv', 'v_w_mix_out', 'v_ln2_g', 'v_ln2_b', 'v_ffn2_w_in', 'v_ffn2_w_out', 'v_ln3_g', 'v_ln3_b', 'v_w_ple', 'v_w_ple_gate', 'v_b_ple_gate', 'v_ln4_g', 'v_ln4_b']
TWIN_OUTPUTS = ['loss', 'grad_x', 'grad_ffn1_w_in', 'grad_ffn1_w_out', 'grad_ln1_g', 'grad_ln1_b', 'grad_w_mix_in', 'grad_b_forget', 'grad_conv_w', 'grad_g_attn', 'grad_g_conv', 'grad_w_mix_out', 'grad_ln2_g', 'grad_ln2_b', 'grad_ffn2_w_in', 'grad_ffn2_w_out', 'grad_ln3_g', 'grad_ln3_b', 'grad_w_ple', 'grad_w_ple_gate', 'grad_b_ple_gate', 'grad_ln4_g', 'grad_ln4_b', 'delta_ffn1_w_in', 'delta_ffn1_w_out', 'delta_ln1_g', 'delta_ln1_b', 'delta_w_mix_in', 'delta_b_forget', 'delta_conv_w', 'delta_g_attn', 'delta_g_conv', 'delta_w_mix_out', 'delta_ln2_g', 'delta_ln2_b', 'delta_ffn2_w_in', 'delta_ffn2_w_out', 'delta_ln3_g', 'delta_ln3_b', 'delta_w_ple', 'delta_w_ple_gate', 'delta_b_ple_gate', 'delta_ln4_g', 'delta_ln4_b', 'new_m_ffn1_w_in', 'new_m_ffn1_w_out', 'new_m_ln1_g', 'new_m_ln1_b', 'new_m_w_mix_in', 'new_m_b_forget', 'new_m_conv_w', 'new_m_g_attn', 'new_m_g_conv', 'new_m_w_mix_out', 'new_m_ln2_g', 'new_m_ln2_b', 'new_m_ffn2_w_in', 'new_m_ffn2_w_out', 'new_m_ln3_g', 'new_m_ln3_b', 'new_m_w_ple', 'new_m_w_ple_gate', 'new_m_b_ple_gate', 'new_m_ln4_g', 'new_m_ln4_b', 'new_v_ffn1_w_in', 'new_v_ffn1_w_out', 'new_v_ln1_g', 'new_v_ln1_b', 'new_v_w_mix_in', 'new_v_b_forget', 'new_v_conv_w', 'new_v_g_attn', 'new_v_g_conv', 'new_v_w_mix_out', 'new_v_ln2_g', 'new_v_ln2_b', 'new_v_ffn2_w_in', 'new_v_ffn2_w_out', 'new_v_ln3_g', 'new_v_ln3_b', 'new_v_w_ple', 'new_v_w_ple_gate', 'new_v_b_ple_gate', 'new_v_ln4_g', 'new_v_ln4_b']
TWIN_LEAF_KINDS = {'loss': 'loss', 'grad_x': 'grad_x', 'grad_ffn1_w_in': 'grad_w', 'grad_ffn1_w_out': 'grad_w', 'grad_ln1_g': 'grad_w', 'grad_ln1_b': 'grad_w', 'grad_w_mix_in': 'grad_w', 'grad_b_forget': 'grad_w', 'grad_conv_w': 'grad_w', 'grad_g_attn': 'grad_w', 'grad_g_conv': 'grad_w', 'grad_w_mix_out': 'grad_w', 'grad_ln2_g': 'grad_w', 'grad_ln2_b': 'grad_w', 'grad_ffn2_w_in': 'grad_w', 'grad_ffn2_w_out': 'grad_w', 'grad_ln3_g': 'grad_w', 'grad_ln3_b': 'grad_w', 'grad_w_ple': 'grad_w', 'grad_w_ple_gate': 'grad_w', 'grad_b_ple_gate': 'grad_w', 'grad_ln4_g': 'grad_w', 'grad_ln4_b': 'grad_w', 'delta_ffn1_w_in': 'delta_w', 'delta_ffn1_w_out': 'delta_w', 'delta_ln1_g': 'delta_w', 'delta_ln1_b': 'delta_w', 'delta_w_mix_in': 'delta_w', 'delta_b_forget': 'delta_w', 'delta_conv_w': 'delta_w', 'delta_g_attn': 'delta_w', 'delta_g_conv': 'delta_w', 'delta_w_mix_out': 'delta_w', 'delta_ln2_g': 'delta_w', 'delta_ln2_b': 'delta_w', 'delta_ffn2_w_in': 'delta_w', 'delta_ffn2_w_out': 'delta_w', 'delta_ln3_g': 'delta_w', 'delta_ln3_b': 'delta_w', 'delta_w_ple': 'delta_w', 'delta_w_ple_gate': 'delta_w', 'delta_b_ple_gate': 'delta_w', 'delta_ln4_g': 'delta_w', 'delta_ln4_b': 'delta_w', 'new_m_ffn1_w_in': 'new_m', 'new_m_ffn1_w_out': 'new_m', 'new_m_ln1_g': 'new_m', 'new_m_ln1_b': 'new_m', 'new_m_w_mix_in': 'new_m', 'new_m_b_forget': 'new_m', 'new_m_conv_w': 'new_m', 'new_m_g_attn': 'new_m', 'new_m_g_conv': 'new_m', 'new_m_w_mix_out': 'new_m', 'new_m_ln2_g': 'new_m', 'new_m_ln2_b': 'new_m', 'new_m_ffn2_w_in': 'new_m', 'new_m_ffn2_w_out': 'new_m', 'new_m_ln3_g': 'new_m', 'new_m_ln3_b': 'new_m', 'new_m_w_ple': 'new_m', 'new_m_w_ple_gate': 'new_m', 'new_m_b_ple_gate': 'new_m', 'new_m_ln4_g': 'new_m', 'new_m_ln4_b': 'new_m', 'new_v_ffn1_w_in': 'new_v', 'new_v_ffn1_w_out': 'new_v', 'new_v_ln1_g': 'new_v', 'new_v_ln1_b': 'new_v', 'new_v_w_mix_in': 'new_v', 'new_v_b_forget': 'new_v', 'new_v_conv_w': 'new_v', 'new_v_g_attn': 'new_v', 'new_v_g_conv': 'new_v', 'new_v_w_mix_out': 'new_v', 'new_v_ln2_g': 'new_v', 'new_v_ln2_b': 'new_v', 'new_v_ffn2_w_in': 'new_v', 'new_v_ffn2_w_out': 'new_v', 'new_v_ln3_g': 'new_v', 'new_v_ln3_b': 'new_v', 'new_v_w_ple': 'new_v', 'new_v_w_ple_gate': 'new_v', 'new_v_b_ple_gate': 'new_v', 'new_v_ln4_g': 'new_v', 'new_v_ln4_b': 'new_v'}


def _forward(args):
    return _fwd_reference(*[args[k] for k in FWD_PARAMS])


def _output_shape():
    def fwd():
        inp = _fwd_setup_inputs(0)
        return _fwd_reference(*[inp[k] for k in FWD_PARAMS])
    out = _jax.eval_shape(fwd)
    return out.shape, out.dtype

N_MICROBATCH = 1
ADAM_LR = 0.001
ADAM_B1 = 0.9
ADAM_B2 = 0.999
ADAM_EPS = 1e-08
ADAM_WD = 0.01
ADAM_STEP = 10
PER_EXAMPLE_BATCH_AXIS = {'x': 0, 'p': 1, 'loss_target': 0}
SHARED_INPUTS = []
_WEIGHT_DTYPES = {'ffn1_w_in': _jnp.float32, 'ffn1_w_out': _jnp.float32, 'ln1_g': _jnp.float32, 'ln1_b': _jnp.float32, 'w_mix_in': _jnp.float32, 'b_forget': _jnp.float32, 'conv_w': _jnp.float32, 'g_attn': _jnp.float32, 'g_conv': _jnp.float32, 'w_mix_out': _jnp.float32, 'ln2_g': _jnp.float32, 'ln2_b': _jnp.float32, 'ffn2_w_in': _jnp.float32, 'ffn2_w_out': _jnp.float32, 'ln3_g': _jnp.float32, 'ln3_b': _jnp.float32, 'w_ple': _jnp.float32, 'w_ple_gate': _jnp.float32, 'b_ple_gate': _jnp.float32, 'ln4_g': _jnp.float32, 'ln4_b': _jnp.float32}
MOMENT_SCALE = {'ffn1_w_in': 3.645454e-02, 'ffn1_w_out': 9.995411e-02, 'ln1_g': 2.885860e+00, 'ln1_b': 2.354399e+00, 'w_mix_in': 1.475389e-01, 'b_forget': 1.286403e+00, 'conv_w': 1.669603e-01, 'g_attn': 1.538582e-01, 'g_conv': 1.541437e-01, 'w_mix_out': 2.657394e-01, 'ln2_g': 3.902594e+00, 'ln2_b': 1.754031e+00, 'ffn2_w_in': 3.140958e-02, 'ffn2_w_out': 8.615414e-02, 'ln3_g': 4.031785e+00, 'ln3_b': 1.787934e+00, 'w_ple': 1.562593e-01, 'w_ple_gate': 3.621818e-02, 'b_ple_gate': 7.103664e-02, 'ln4_g': 1.283402e+02, 'ln4_b': 4.023571e+00}


def _to_microbatches(a, axis):
    t = _jnp.moveaxis(a, axis, 0)
    t = t.reshape((N_MICROBATCH, t.shape[0] // N_MICROBATCH) + t.shape[1:])
    return _jnp.moveaxis(t, 1, axis + 1)


def setup_inputs(seed: int = 0) -> dict:
    inp = _fwd_setup_inputs(seed)
    key = _jax.random.fold_in(_jax.random.key(seed), 7919)
    shape, _ = _output_shape()
    out = dict(inp)
    out["loss_target"] = _jax.random.normal(_jax.random.fold_in(key, 0), shape, _jnp.float32)
    for i, name in enumerate(TWIN_WEIGHTS):
        w = inp[name].astype(_jnp.float32)
        if MOMENT_SCALE is None:
            s = _jnp.sqrt(_jnp.mean(_jnp.square(w)) + 1e-30)
        else:
            s = MOMENT_SCALE[name]
        km, kv = _jax.random.split(_jax.random.fold_in(key, i + 1))
        out[name] = w
        out["m_" + name] = s * _jax.random.normal(km, w.shape, _jnp.float32)
        out["v_" + name] = (s * s) * _jax.random.uniform(kv, w.shape, _jnp.float32, 0.5, 1.5)
    if N_MICROBATCH > 1:
        for name, axis in PER_EXAMPLE_BATCH_AXIS.items():
            out[name] = _to_microbatches(out[name], axis)
    return {'x': out['x'], 'p': out['p'], 'ffn1_w_in': out['ffn1_w_in'], 'ffn1_w_out': out['ffn1_w_out'], 'ln1_g': out['ln1_g'], 'ln1_b': out['ln1_b'], 'w_mix_in': out['w_mix_in'], 'b_forget': out['b_forget'], 'conv_w': out['conv_w'], 'g_attn': out['g_attn'], 'g_conv': out['g_conv'], 'w_mix_out': out['w_mix_out'], 'ln2_g': out['ln2_g'], 'ln2_b': out['ln2_b'], 'ffn2_w_in': out['ffn2_w_in'], 'ffn2_w_out': out['ffn2_w_out'], 'ln3_g': out['ln3_g'], 'ln3_b': out['ln3_b'], 'w_ple': out['w_ple'], 'w_ple_gate': out['w_ple_gate'], 'b_ple_gate': out['b_ple_gate'], 'ln4_g': out['ln4_g'], 'ln4_b': out['ln4_b'], 'loss_target': out['loss_target'], 'm_ffn1_w_in': out['m_ffn1_w_in'], 'm_ffn1_w_out': out['m_ffn1_w_out'], 'm_ln1_g': out['m_ln1_g'], 'm_ln1_b': out['m_ln1_b'], 'm_w_mix_in': out['m_w_mix_in'], 'm_b_forget': out['m_b_forget'], 'm_conv_w': out['m_conv_w'], 'm_g_attn': out['m_g_attn'], 'm_g_conv': out['m_g_conv'], 'm_w_mix_out': out['m_w_mix_out'], 'm_ln2_g': out['m_ln2_g'], 'm_ln2_b': out['m_ln2_b'], 'm_ffn2_w_in': out['m_ffn2_w_in'], 'm_ffn2_w_out': out['m_ffn2_w_out'], 'm_ln3_g': out['m_ln3_g'], 'm_ln3_b': out['m_ln3_b'], 'm_w_ple': out['m_w_ple'], 'm_w_ple_gate': out['m_w_ple_gate'], 'm_b_ple_gate': out['m_b_ple_gate'], 'm_ln4_g': out['m_ln4_g'], 'm_ln4_b': out['m_ln4_b'], 'v_ffn1_w_in': out['v_ffn1_w_in'], 'v_ffn1_w_out': out['v_ffn1_w_out'], 'v_ln1_g': out['v_ln1_g'], 'v_ln1_b': out['v_ln1_b'], 'v_w_mix_in': out['v_w_mix_in'], 'v_b_forget': out['v_b_forget'], 'v_conv_w': out['v_conv_w'], 'v_g_attn': out['v_g_attn'], 'v_g_conv': out['v_g_conv'], 'v_w_mix_out': out['v_w_mix_out'], 'v_ln2_g': out['v_ln2_g'], 'v_ln2_b': out['v_ln2_b'], 'v_ffn2_w_in': out['v_ffn2_w_in'], 'v_ffn2_w_out': out['v_ffn2_w_out'], 'v_ln3_g': out['v_ln3_g'], 'v_ln3_b': out['v_ln3_b'], 'v_w_ple': out['v_w_ple'], 'v_w_ple_gate': out['v_w_ple_gate'], 'v_b_ple_gate': out['v_b_ple_gate'], 'v_ln4_g': out['v_ln4_g'], 'v_ln4_b': out['v_ln4_b']}


def _loss(weights, diff, rest, loss_target):
    with _jax.named_scope("forward"):
        args = {**rest, TWIN_DIFF_INPUT: diff, **{k: w.astype(_WEIGHT_DTYPES[k]) for k, w in weights.items()}}
        y = _forward(args)
    with _jax.named_scope("loss_head"):
        err = _jnp.square(y.astype(_jnp.float32) - loss_target)
        return 0.5 * _jnp.sum(_jnp.mean(err, axis=-1)) if err.ndim else 0.5 * err


def _adamw(w, g, m, v):
    m = ADAM_B1 * m + (1.0 - ADAM_B1) * g
    v = ADAM_B2 * v + (1.0 - ADAM_B2) * _jnp.square(g)
    m_hat = m / (1.0 - ADAM_B1 ** ADAM_STEP)
    v_hat = v / (1.0 - ADAM_B2 ** ADAM_STEP)
    delta = -ADAM_LR * (m_hat / (_jnp.sqrt(v_hat) + ADAM_EPS) + ADAM_WD * w)
    return delta, m, v


def reference(x, p, ffn1_w_in, ffn1_w_out, ln1_g, ln1_b, w_mix_in, b_forget, conv_w, g_attn, g_conv, w_mix_out, ln2_g, ln2_b, ffn2_w_in, ffn2_w_out, ln3_g, ln3_b, w_ple, w_ple_gate, b_ple_gate, ln4_g, ln4_b, loss_target, m_ffn1_w_in, m_ffn1_w_out, m_ln1_g, m_ln1_b, m_w_mix_in, m_b_forget, m_conv_w, m_g_attn, m_g_conv, m_w_mix_out, m_ln2_g, m_ln2_b, m_ffn2_w_in, m_ffn2_w_out, m_ln3_g, m_ln3_b, m_w_ple, m_w_ple_gate, m_b_ple_gate, m_ln4_g, m_ln4_b, v_ffn1_w_in, v_ffn1_w_out, v_ln1_g, v_ln1_b, v_w_mix_in, v_b_forget, v_conv_w, v_g_attn, v_g_conv, v_w_mix_out, v_ln2_g, v_ln2_b, v_ffn2_w_in, v_ffn2_w_out, v_ln3_g, v_ln3_b, v_w_ple, v_w_ple_gate, v_b_ple_gate, v_ln4_g, v_ln4_b):
    given = dict(x=x, p=p, ffn1_w_in=ffn1_w_in, ffn1_w_out=ffn1_w_out, ln1_g=ln1_g, ln1_b=ln1_b, w_mix_in=w_mix_in, b_forget=b_forget, conv_w=conv_w, g_attn=g_attn, g_conv=g_conv, w_mix_out=w_mix_out, ln2_g=ln2_g, ln2_b=ln2_b, ffn2_w_in=ffn2_w_in, ffn2_w_out=ffn2_w_out, ln3_g=ln3_g, ln3_b=ln3_b, w_ple=w_ple, w_ple_gate=w_ple_gate, b_ple_gate=b_ple_gate, ln4_g=ln4_g, ln4_b=ln4_b, loss_target=loss_target, m_ffn1_w_in=m_ffn1_w_in, m_ffn1_w_out=m_ffn1_w_out, m_ln1_g=m_ln1_g, m_ln1_b=m_ln1_b, m_w_mix_in=m_w_mix_in, m_b_forget=m_b_forget, m_conv_w=m_conv_w, m_g_attn=m_g_attn, m_g_conv=m_g_conv, m_w_mix_out=m_w_mix_out, m_ln2_g=m_ln2_g, m_ln2_b=m_ln2_b, m_ffn2_w_in=m_ffn2_w_in, m_ffn2_w_out=m_ffn2_w_out, m_ln3_g=m_ln3_g, m_ln3_b=m_ln3_b, m_w_ple=m_w_ple, m_w_ple_gate=m_w_ple_gate, m_b_ple_gate=m_b_ple_gate, m_ln4_g=m_ln4_g, m_ln4_b=m_ln4_b, v_ffn1_w_in=v_ffn1_w_in, v_ffn1_w_out=v_ffn1_w_out, v_ln1_g=v_ln1_g, v_ln1_b=v_ln1_b, v_w_mix_in=v_w_mix_in, v_b_forget=v_b_forget, v_conv_w=v_conv_w, v_g_attn=v_g_attn, v_g_conv=v_g_conv, v_w_mix_out=v_w_mix_out, v_ln2_g=v_ln2_g, v_ln2_b=v_ln2_b, v_ffn2_w_in=v_ffn2_w_in, v_ffn2_w_out=v_ffn2_w_out, v_ln3_g=v_ln3_g, v_ln3_b=v_ln3_b, v_w_ple=v_w_ple, v_w_ple_gate=v_w_ple_gate, v_b_ple_gate=v_b_ple_gate, v_ln4_g=v_ln4_g, v_ln4_b=v_ln4_b)
    weights = {n: given[n] for n in TWIN_WEIGHTS}
    shared = {n: given[n] for n in SHARED_INPUTS}
    per_example = {n: given[n] for n in ['x', 'p']}
    grad_fn = _jax.value_and_grad(_loss, argnums=(0, 1))

    def one_microbatch(ex, loss_target):
        ex = dict(ex)
        diff = ex.pop(TWIN_DIFF_INPUT)
        return grad_fn(weights, diff, {**shared, **ex}, loss_target)

    if N_MICROBATCH == 1:
        loss, (grad_w, grad_x) = one_microbatch(per_example, given["loss_target"])
    else:
        def body(carry, xs):
            loss_sum, grad_sum = carry
            l_k, (gw_k, gx_k) = one_microbatch(xs[0], xs[1])
            with _jax.named_scope("update"):
                return (loss_sum + l_k, _jax.tree.map(_jnp.add, grad_sum, gw_k)), gx_k

        init = (_jnp.zeros((), _jnp.float32), _jax.tree.map(_jnp.zeros_like, weights))
        (loss, grad_w), grad_x = _jax.lax.scan(body, init, (per_example, given["loss_target"]))
    with _jax.named_scope("update"):
        delta_w, new_m, new_v = {}, {}, {}
        for n in TWIN_WEIGHTS:
            delta_w[n], new_m[n], new_v[n] = _adamw(weights[n], grad_w[n], given["m_" + n], given["v_" + n])
    return (loss, grad_x, *[grad_w[n] for n in TWIN_WEIGHTS], *[delta_w[n] for n in TWIN_WEIGHTS],
            *[new_m[n] for n in TWIN_WEIGHTS], *[new_v[n] for n in TWIN_WEIGHTS])
```

```python
import functools
import math

import jax
import jax.numpy as jnp
from jax import lax
from jax.experimental import pallas as pl
from jax.experimental.pallas import tpu as pltpu

F32 = jnp.float32
BF16 = jnp.bfloat16

D_MODEL = 1024
D_FF = 2816
N_HEADS = 8
HEAD_DIM = 64
D_ATTN = N_HEADS * HEAD_DIM
D_CONV = 512
PLE_DIM = 256
N_FLOG = 128
ALPHA = 2.0 ** 0.25
LN_EPS = 1e-5
RMS_EPS = 1e-6
NEG_INF = -1e30
Q_SCALE = 1.0 / math.sqrt(HEAD_DIM)

ADAM_LR = 0.001
ADAM_B1 = 0.9
ADAM_B2 = 0.999
ADAM_EPS = 1e-08
ADAM_WD = 0.01
ADAM_STEP = 10

V7X_VMEM_BYTES = 64 << 20
VMEM_LIMIT = V7X_VMEM_BYTES - (8 << 20)
LANE = 128
FF_CHUNK = 256
N_CHIPS = 4
MESH = pl.DeviceIdType.MESH


def _cp(n_axes):
    return pltpu.CompilerParams(dimension_semantics=("arbitrary",) * n_axes, vmem_limit_bytes=VMEM_LIMIT)


def _resident(shape):
    n = len(shape)
    return pl.BlockSpec(shape, lambda *_: (0,) * n, pipeline_mode=pl.Buffered(1))


def _nn(a, b):
    return jnp.dot(a, b, preferred_element_type=F32)


def _nt(a, b):
    return lax.dot_general(a, b, (((1,), (1,)), ((), ())), preferred_element_type=F32)


def _tn(a, b):
    return lax.dot_general(a, b, (((0,), (0,)), ((), ())), preferred_element_type=F32)


def _ln_stats(r):
    mu = jnp.mean(r, axis=-1, keepdims=True)
    xc = r - mu
    var = jnp.mean(xc * xc, axis=-1, keepdims=True)
    rstd = lax.rsqrt(var + LN_EPS)
    return xc * rstd, rstd


def _ln_bwd(dy, xhat, rstd, g):
    dxh = dy * g
    m1 = jnp.mean(dxh, axis=-1, keepdims=True)
    m2 = jnp.mean(dxh * xhat, axis=-1, keepdims=True)
    return rstd * (dxh - m1 - xhat * m2)


def _sigmoid(z):
    return 1.0 / (1.0 + jnp.exp(-z))


def _rowsum(a):
    return jnp.sum(a, axis=0, keepdims=True)


def _tile(total, want):
    if total <= want:
        return total
    for t in range(want - want % 8, 0, -8):
        if total % t == 0:
            return t
    raise ValueError((total, want))


def _ffn_fwd(x, w_in, w_out, lg, lb, name):
    T = x.shape[0]
    tm = _tile(T, 512)
    nf = D_FF // FF_CHUNK

    def body(x_ref, wi_ref, wo_ref, lg_ref, lb_ref, xo_ref, r_ref, g_ref, u_ref, h_ref):
        xf = x_ref[...]
        xb = xf.astype(BF16)
        acc = jnp.zeros((tm, D_MODEL), F32)
        for j in range(nf):
            c0 = j * FF_CHUNK
            g = _nn(xb, wi_ref[:, c0:c0 + FF_CHUNK])
            u = _nn(xb, wi_ref[:, D_FF + c0:D_FF + c0 + FF_CHUNK])
            hb = (g * _sigmoid(g) * u).astype(BF16)
            g_ref[:, c0:c0 + FF_CHUNK] = g.astype(BF16)
            u_ref[:, c0:c0 + FF_CHUNK] = u.astype(BF16)
            h_ref[:, c0:c0 + FF_CHUNK] = hb
            acc = acc + _nn(hb, wo_ref[c0:c0 + FF_CHUNK, :])
        r = ALPHA * xf + 0.5 * acc
        r_ref[...] = r
        xhat, _ = _ln_stats(r)
        xo_ref[...] = xhat * lg_ref[...] + lb_ref[...]

    row = lambda n: pl.BlockSpec((tm, n), lambda i: (i, 0))
    return pl.pallas_call(
        body, name=name, grid=(T // tm,),
        in_specs=[row(D_MODEL), _resident((D_MODEL, 2 * D_FF)), _resident((D_FF, D_MODEL)),
                  _resident((1, D_MODEL)), _resident((1, D_MODEL))],
        out_specs=[row(D_MODEL), row(D_MODEL), row(D_FF), row(D_FF), row(D_FF)],
        out_shape=[jax.ShapeDtypeStruct((T, D_MODEL), F32), jax.ShapeDtypeStruct((T, D_MODEL), F32),
                   jax.ShapeDtypeStruct((T, D_FF), BF16), jax.ShapeDtypeStruct((T, D_FF), BF16),
                   jax.ShapeDtypeStruct((T, D_FF), BF16)],
        compiler_params=_cp(1),
    )(x, w_in, w_out, lg, lb)


def _ffn_bwd(dxo, r, g, u, w_in, w_out, lg, name):
    T = r.shape[0]
    tm = _tile(T, 256)
    nf = D_FF // FF_CHUNK

    def body(dxo_ref, r_ref, g_ref, u_ref, wi_ref, wo_ref, lg_ref, dx_ref, dgu_ref, df_ref, dlg_ref, dlb_ref):
        i = pl.program_id(0)
        dy = dxo_ref[...]
        xhat, rstd = _ln_stats(r_ref[...])
        dr = _ln_bwd(dy, xhat, rstd, lg_ref[...])

        @pl.when(i == 0)
        def _():
            dlg_ref[...] = jnp.zeros_like(dlg_ref)
            dlb_ref[...] = jnp.zeros_like(dlb_ref)

        dlg_ref[...] += _rowsum(dy * xhat)
        dlb_ref[...] += _rowsum(dy)
        dfb = (0.5 * dr).astype(BF16)
        df_ref[...] = dfb
        acc = jnp.zeros((tm, D_MODEL), F32)
        for j in range(nf):
            c0 = j * FF_CHUNK
            dh = _nt(dfb, wo_ref[c0:c0 + FF_CHUNK, :])
            gg = g_ref[:, c0:c0 + FF_CHUNK].astype(F32)
            uu = u_ref[:, c0:c0 + FF_CHUNK].astype(F32)
            s = _sigmoid(gg)
            dgb = (dh * uu * s * (1.0 + gg * (1.0 - s))).astype(BF16)
            dub = (dh * gg * s).astype(BF16)
            dgu_ref[:, c0:c0 + FF_CHUNK] = dgb
            dgu_ref[:, D_FF + c0:D_FF + c0 + FF_CHUNK] = dub
            acc = acc + _nt(dgb, wi_ref[:, c0:c0 + FF_CHUNK]) + _nt(dub, wi_ref[:, D_FF + c0:D_FF + c0 + FF_CHUNK])
        dx_ref[...] = ALPHA * dr + acc

    row = lambda n: pl.BlockSpec((tm, n), lambda i: (i, 0))
    return pl.pallas_call(
        body, name=name, grid=(T // tm,),
        in_specs=[row(D_MODEL), row(D_MODEL), row(D_FF), row(D_FF), _resident((D_MODEL, 2 * D_FF)),
                  _resident((D_FF, D_MODEL)), _resident((1, D_MODEL))],
        out_specs=[row(D_MODEL), row(2 * D_FF), row(D_MODEL), _resident((1, D_MODEL)), _resident((1, D_MODEL))],
        out_shape=[jax.ShapeDtypeStruct((T, D_MODEL), F32), jax.ShapeDtypeStruct((T, 2 * D_FF), BF16),
                   jax.ShapeDtypeStruct((T, D_MODEL), BF16), jax.ShapeDtypeStruct((1, D_MODEL), F32),
                   jax.ShapeDtypeStruct((1, D_MODEL), F32)],
        compiler_params=_cp(1),
    )(dxo, r, g, u, w_in, w_out, lg)


def _matmul_tn(a, b, name, tn=None):
    T, K = a.shape
    N = b.shape[1]
    tt = _tile(T, 512)
    if tn is None:
        tn = N
        while K * tn * 4 > (6 << 20) and tn % 256 == 0:
            tn //= 2
    assert N % tn == 0

    def body(a_ref, b_ref, o_ref):
        @pl.when(pl.program_id(1) == 0)
        def _():
            o_ref[...] = jnp.zeros_like(o_ref)

        o_ref[...] += _tn(a_ref[...].astype(BF16), b_ref[...].astype(BF16))

    return pl.pallas_call(
        body, name=name, grid=(N // tn, T // tt),
        in_specs=[pl.BlockSpec((tt, K), lambda n, t: (t, 0)), pl.BlockSpec((tt, tn), lambda n, t: (t, n))],
        out_specs=pl.BlockSpec((K, tn), lambda n, t: (0, n)),
        out_shape=jax.ShapeDtypeStruct((K, N), F32),
        compiler_params=_cp(2),
    )(a, b)


def _matmul_nn(x, w, scale, out_dtype, name):
    T, K = x.shape
    N = w.shape[1]
    tm = _tile(T, 512)

    def body(x_ref, w_ref, s_ref, o_ref):
        o_ref[...] = (_nn(x_ref[...].astype(BF16), w_ref[...]) * s_ref[...]).astype(out_dtype)

    return pl.pallas_call(
        body, name=name, grid=(T // tm,),
        in_specs=[pl.BlockSpec((tm, K), lambda i: (i, 0)), _resident((K, N)), _resident((1, N))],
        out_specs=pl.BlockSpec((tm, N), lambda i: (i, 0)),
        out_shape=jax.ShapeDtypeStruct((T, N), out_dtype),
        compiler_params=_cp(1),
    )(x, w, scale)


def _log_sigmoid(z):
    return jnp.minimum(z, 0.0) - jnp.log1p(jnp.exp(-jnp.abs(z)))


def _tri(n, lower):
    r = lax.broadcasted_iota(jnp.int32, (n, n), 0)
    c = lax.broadcasted_iota(jnp.int32, (n, n), 1)
    return jnp.where((c <= r) if lower else (c >= r), 1.0, 0.0).astype(F32)


def _f32dot(a, b):
    return jnp.dot(a, b, preferred_element_type=F32, precision=lax.Precision.HIGHEST)


def _forget_cumsum(flog, col, bf):
    T = flog.shape[0]
    bt = _tile(T, 512)

    def body(f_ref, b_ref, c_ref, carry):
        @pl.when(pl.program_id(0) == 0)
        def _():
            carry[...] = jnp.zeros_like(carry)

        lf = _log_sigmoid(f_ref[...] + b_ref[...])
        c = _f32dot(_tri(bt, True), lf) + carry[...]
        c_ref[...] = c
        carry[...] = c[bt - 1:bt, :]

    return pl.pallas_call(
        body, name="forget_cumsum", grid=(T // bt,),
        in_specs=[pl.BlockSpec((bt, N_FLOG), lambda i: (i, col)), _resident((1, N_FLOG))],
        out_specs=pl.BlockSpec((bt, N_FLOG), lambda i: (i, 0)),
        out_shape=jax.ShapeDtypeStruct((T, N_FLOG), F32),
        scratch_shapes=[pltpu.VMEM((1, N_FLOG), F32)],
        compiler_params=_cp(1),
    )(flog, bf)


def _forget_bwd(dc, dcq, flog, col, bf):
    T = dc.shape[0]
    bt = _tile(T, 512)
    nb = T // bt

    def body(dc_ref, dcq_ref, f_ref, b_ref, dz_ref, db_ref, carry):
        @pl.when(pl.program_id(0) == 0)
        def _():
            carry[...] = jnp.zeros_like(carry)
            db_ref[...] = jnp.zeros_like(db_ref)

        dlf = _f32dot(_tri(bt, False), dc_ref[...] + dcq_ref[...]) + carry[...]
        carry[...] = dlf[0:1, :]
        z = f_ref[...] + b_ref[...]
        dz = dlf * _sigmoid(-z)
        dz_ref[...] = dz.astype(BF16)
        db_ref[...] += _rowsum(dz)

    return pl.pallas_call(
        body, name="forget_bwd", grid=(nb,),
        in_specs=[pl.BlockSpec((bt, N_FLOG), lambda i: (nb - 1 - i, 0)),
                  pl.BlockSpec((bt, N_FLOG), lambda i: (nb - 1 - i, 0)),
                  pl.BlockSpec((bt, N_FLOG), lambda i: (nb - 1 - i, col)), _resident((1, N_FLOG))],
        out_specs=[pl.BlockSpec((bt, N_FLOG), lambda i: (nb - 1 - i, 0)), _resident((1, N_FLOG))],
        out_shape=[jax.ShapeDtypeStruct((T, N_FLOG), BF16), jax.ShapeDtypeStruct((1, N_FLOG), F32)],
        scratch_shapes=[pltpu.VMEM((1, N_FLOG), F32)],
        compiler_params=_cp(1),
    )(dc, dcq, flog, bf)


def _head_masks():
    lane = lax.broadcasted_iota(jnp.int32, (1, LANE), 1)
    return lane < HEAD_DIM


def _split_heads(x2, is_a):
    zero = jnp.zeros_like(x2)
    return jnp.where(is_a, x2, zero), jnp.where(is_a, zero, x2)


def _attn_fwd(qkv, cb, crow):
    T = qkv.shape[0]
    tq = _tile(T, 512)
    tk = tq
    nq = T // tq
    npair = N_HEADS // 2

    def body(q_ref, k_ref, v_ref, cb_ref, cr_ref, o_ref, al_ref, m_s, l_s, acc_s):
        i = pl.program_id(1)
        is_a = _head_masks()
        qs = _split_heads(q_ref[...], is_a)
        ct = (cb_ref[:, 0:1], cb_ref[:, HEAD_DIM:HEAD_DIM + 1])
        m_s[...] = jnp.full_like(m_s, NEG_INF)
        l_s[...] = jnp.zeros_like(l_s)
        acc_s[...] = jnp.zeros_like(acc_s)

        def step(kk, masked):
            k0 = pl.multiple_of(kk * tk, tk)
            k2 = k_ref[pl.ds(k0, tk), :]
            v2 = v_ref[pl.ds(k0, tk), :]
            for h in range(2):
                cs = cr_ref[0, h:h + 1, pl.ds(k0, tk)]
                z = _nt(qs[h], k2) + (ct[h] - cs)
                if masked:
                    rr = lax.broadcasted_iota(jnp.int32, (tq, tk), 0)
                    cc = lax.broadcasted_iota(jnp.int32, (tq, tk), 1)
                    z = jnp.where(rr >= cc, z, NEG_INF)
                m_old = m_s[h]
                m_new = jnp.maximum(m_old, jnp.max(z, axis=-1, keepdims=True))
                p = jnp.exp(z - m_new)
                a = jnp.exp(m_old - m_new)
                l_s[h] = a * l_s[h] + jnp.sum(p, axis=-1, keepdims=True)
                acc_s[h] = a * acc_s[h] + _nn(p.astype(BF16), v2)
                m_s[h] = m_new

        def loop_body(kk, carry):
            step(kk, False)
            return carry

        lax.fori_loop(0, i, loop_body, 0)
        step(i, True)
        outs, als = [], []
        for h in range(2):
            l = l_s[h]
            outs.append(acc_s[h] * (1.0 / l))
            als.append(ct[h] - (m_s[h] + jnp.log(l)))
        o_ref[...] = jnp.where(is_a, outs[0], outs[1])
        al_ref[...] = jnp.where(is_a, als[0], als[1])

    return pl.pallas_call(
        body, name="attn_fwd", grid=(npair, nq),
        in_specs=[pl.BlockSpec((tq, LANE), lambda j, i: (i, j)),
                  pl.BlockSpec((T, LANE), lambda j, i: (0, npair + j)),
                  pl.BlockSpec((T, LANE), lambda j, i: (0, 2 * npair + j)),
                  pl.BlockSpec((tq, LANE), lambda j, i: (i, j)),
                  pl.BlockSpec((1, 8, T), lambda j, i: (j, 0, 0))],
        out_specs=[pl.BlockSpec((tq, LANE), lambda j, i: (i, j)), pl.BlockSpec((tq, LANE), lambda j, i: (i, j))],
        out_shape=[jax.ShapeDtypeStruct((T, D_ATTN), F32), jax.ShapeDtypeStruct((T, D_ATTN), F32)],
        scratch_shapes=[pltpu.VMEM((2, tq, 1), F32), pltpu.VMEM((2, tq, 1), F32), pltpu.VMEM((2, tq, LANE), F32)],
        compiler_params=_cp(2),
    )(qkv, qkv, qkv, cb, crow)


def _attn_bwd(qkv, dob, cb, alrow, dlrow):
    T = qkv.shape[0]
    tq = _tile(T, 512)
    tk = tq
    nq = T // tq
    npair = N_HEADS // 2

    def body(q_ref, k_ref, v_ref, do_ref, cb_ref, al_ref, dl_ref, dq_ref, dk_ref, dv_ref, dc_ref, dcq_ref,
             dk_s, dv_s, dc_s):
        kj = pl.program_id(1)
        is_a = _head_masks()
        ks = _split_heads(k_ref[...], is_a)
        vs = _split_heads(v_ref[...], is_a)
        cs = (cb_ref[:, 0:1], cb_ref[:, HEAD_DIM:HEAD_DIM + 1])

        @pl.when(kj == 0)
        def _():
            dq_ref[...] = jnp.zeros_like(dq_ref)
            dcq_ref[...] = jnp.zeros_like(dcq_ref)

        dk_s[...] = jnp.zeros_like(dk_s)
        dv_s[...] = jnp.zeros_like(dv_s)
        dc_s[...] = jnp.zeros_like(dc_s)

        def step(qi, masked):
            q0 = pl.multiple_of(qi * tq, tq)
            q2 = q_ref[pl.ds(q0, tq), :]
            do2 = do_ref[pl.ds(q0, tq), :]
            for h in range(2):
                alr = al_ref[0, h:h + 1, pl.ds(q0, tq)]
                dlr = dl_ref[0, h:h + 1, pl.ds(q0, tq)]
                zt = _nt(ks[h], q2) + (alr - cs[h])
                if masked:
                    rr = lax.broadcasted_iota(jnp.int32, (tk, tq), 0)
                    cc = lax.broadcasted_iota(jnp.int32, (tk, tq), 1)
                    zt = jnp.where(cc >= rr, zt, NEG_INF)
                pt = jnp.exp(zt)
                dst = pt * (_nt(vs[h], do2) - dlr)
                pb = pt.astype(BF16)
                dsb = dst.astype(BF16)
                dv_s[h] += _nn(pb, do2)
                dk_s[h] += _nn(dsb, q2)
                dc_s[h] += jnp.sum(dst, axis=-1, keepdims=True)
                dcq_ref[0, h:h + 1, pl.ds(q0, tq)] += jnp.sum(dst, axis=0, keepdims=True)
                dq_ref[pl.ds(q0, tq), :] += _tn(dsb, ks[h])

        step(kj, True)

        def loop_body(qi, carry):
            step(qi, False)
            return carry

        lax.fori_loop(kj + 1, nq, loop_body, 0)
        dk_ref[...] = jnp.where(is_a, dk_s[0], dk_s[1])
        dv_ref[...] = jnp.where(is_a, dv_s[0], dv_s[1])
        dc_ref[...] = jnp.where(is_a, -dc_s[0], -dc_s[1])

        @pl.when(kj == nq - 1)
        def _():
            dq_ref[...] = dq_ref[...] * Q_SCALE

    full = lambda col: pl.BlockSpec((T, LANE), lambda j, kj: (0, col(j)), pipeline_mode=pl.Buffered(1))
    tile = lambda col: pl.BlockSpec((tk, LANE), lambda j, kj: (kj, col(j)))
    rowl = pl.BlockSpec((1, 8, T), lambda j, kj: (j, 0, 0))
    return pl.pallas_call(
        body, name="attn_bwd", grid=(npair, nq),
        in_specs=[full(lambda j: j), tile(lambda j: npair + j), tile(lambda j: 2 * npair + j), full(lambda j: j),
                  tile(lambda j: j), rowl, rowl],
        out_specs=[pl.BlockSpec((T, LANE), lambda j, kj: (0, j)), tile(lambda j: j), tile(lambda j: j),
                   tile(lambda j: j), rowl],
        out_shape=[jax.ShapeDtypeStruct((T, D_ATTN), F32)] * 4 + [jax.ShapeDtypeStruct((npair, 8, T), F32)],
        scratch_shapes=[pltpu.VMEM((2, tk, LANE), F32), pltpu.VMEM((2, tk, LANE), F32), pltpu.VMEM((2, tk, 1), F32)],
        compiler_params=_cp(2),
    )(qkv, qkv, qkv, dob, cb, alrow, dlrow)


HALO = 8


def _shift_rows(cur, other, k, tm, down):
    row = lax.broadcasted_iota(jnp.int32, (tm, 1), 0)
    reps = tm // HALO
    if down:
        rolled = pltpu.roll(cur, k, 0)
        fill = jnp.tile(pltpu.roll(other, k, 0), (reps, 1))
        return jnp.where(row < k, fill, rolled)
    rolled = pltpu.roll(cur, tm - k, 0)
    fill = jnp.tile(pltpu.roll(other, HALO - k, 0), (reps, 1))
    return jnp.where(row >= tm - k, fill, rolled)


def _conv_fwd(c, hh, c_prev, hh_prev, w_ref, first, tm):
    u = c * hh
    u_prev = jnp.where(first, 0.0, c_prev * hh_prev)
    u1 = _shift_rows(u, u_prev, 1, tm, True)
    u2 = _shift_rows(u, u_prev, 2, tm, True)
    y = w_ref[0:1, :] * u2 + w_ref[1:2, :] * u1 + w_ref[2:3, :] * u
    return u, u1, u2, y


def _rms(x, g):
    rs = lax.rsqrt(jnp.mean(x * x, axis=-1, keepdims=True) + RMS_EPS)
    return x * rs * g, rs


def _mixer_tail_fwd(o, bchf, conv_w, g_attn, g_conv, w_mo, x1, lg, lb):
    T = o.shape[0]
    tm = _tile(T, 512)
    hb = tm // HALO

    def body(o_ref, b_ref, c_ref, h_ref, cp_ref, hp_ref, w_ref, ga_ref, gc_ref, wmo_ref, x1_ref, lg_ref, lb_ref,
             x2_ref, r2_ref, mg_ref):
        first = pl.program_id(0) == 0
        _, _, _, y = _conv_fwd(c_ref[...], h_ref[...], cp_ref[...], hp_ref[...], w_ref, first, tm)
        na, _ = _rms(o_ref[...], ga_ref[...])
        nc, _ = _rms(b_ref[...] * y, gc_ref[...])
        nab = na.astype(BF16)
        ncb = nc.astype(BF16)
        mg_ref[:, 0:D_ATTN] = nab
        mg_ref[:, D_ATTN:] = ncb
        r2 = ALPHA * x1_ref[...] + _nn(nab, wmo_ref[0:D_ATTN, :]) + _nn(ncb, wmo_ref[D_ATTN:, :])
        r2_ref[...] = r2
        xhat, _ = _ln_stats(r2)
        x2_ref[...] = xhat * lg_ref[...] + lb_ref[...]

    row = lambda n, col=0: pl.BlockSpec((tm, n), lambda i: (i, col))
    prev = lambda col: pl.BlockSpec((HALO, D_CONV), lambda i: (jnp.maximum(i * hb - 1, 0), col))
    return pl.pallas_call(
        body, name="mixer_tail_fwd", grid=(T // tm,),
        in_specs=[row(D_ATTN), row(D_CONV, 0), row(D_CONV, 1), row(D_CONV, 2), prev(1), prev(2),
                  _resident((3, D_CONV)), _resident((1, D_ATTN)), _resident((1, D_CONV)),
                  _resident((D_MODEL, D_MODEL)), row(D_MODEL), _resident((1, D_MODEL)), _resident((1, D_MODEL))],
        out_specs=[row(D_MODEL), row(D_MODEL), row(D_MODEL)],
        out_shape=[jax.ShapeDtypeStruct((T, D_MODEL), F32), jax.ShapeDtypeStruct((T, D_MODEL), F32),
                   jax.ShapeDtypeStruct((T, D_MODEL), BF16)],
        compiler_params=_cp(1),
    )(o, bchf, bchf, bchf, bchf, bchf, conv_w, g_attn, g_conv, w_mo, x1, lg, lb)


def _head_sum_matrix():
    r = lax.broadcasted_iota(jnp.int32, (D_ATTN, D_ATTN), 0) // HEAD_DIM
    c = lax.broadcasted_iota(jnp.int32, (D_ATTN, D_ATTN), 1) // HEAD_DIM
    return jnp.where(r == c, 1.0, 0.0).astype(F32)


def _mixer_tail_bwd(dx2, r2, lg, w_mo, o, bchf, conv_w, g_attn, g_conv):
    T = o.shape[0]
    tm = _tile(T, 256)
    hb = tm // HALO

    def body(dx2_ref, r2_ref, lg_ref, wmo_ref, o_ref, b_ref, c_ref, h_ref, cp_ref, hp_ref, w_ref, ga_ref, gc_ref,
             dx1_ref, dr_ref, do_ref, dl_ref, dco_ref, dlg_ref, dlb_ref, dga_ref, dgc_ref):
        i = pl.program_id(0)

        @pl.when(i == 0)
        def _():
            for ref in (dlg_ref, dlb_ref, dga_ref, dgc_ref):
                ref[...] = jnp.zeros_like(ref)

        dy = dx2_ref[...]
        xhat, rstd = _ln_stats(r2_ref[...])
        dr = _ln_bwd(dy, xhat, rstd, lg_ref[...])
        dlg_ref[...] += _rowsum(dy * xhat)
        dlb_ref[...] += _rowsum(dy)
        dx1_ref[...] = ALPHA * dr
        drb = dr.astype(BF16)
        dr_ref[...] = drb
        dna = _nt(drb, wmo_ref[0:D_ATTN, :])
        dnc = _nt(drb, wmo_ref[D_ATTN:, :])

        def rms_bwd(x, g, dn):
            rs = lax.rsqrt(jnp.mean(x * x, axis=-1, keepdims=True) + RMS_EPS)
            dng = dn * g
            dx = rs * dng - x * (rs * rs * rs) * jnp.mean(dng * x, axis=-1, keepdims=True)
            return dx, _rowsum(dn * x * rs)

        oo = o_ref[...]
        do, dga = rms_bwd(oo, ga_ref[...], dna)
        dga_ref[...] += dga
        do_ref[...] = do.astype(BF16)
        dl_ref[...] = _f32dot(do * oo, _head_sum_matrix())
        _, _, _, y = _conv_fwd(c_ref[...], h_ref[...], cp_ref[...], hp_ref[...], w_ref, i == 0, tm)
        dco, dgc = rms_bwd(b_ref[...] * y, gc_ref[...], dnc)
        dgc_ref[...] += dgc
        dco_ref[...] = dco

    row = lambda n, col=0: pl.BlockSpec((tm, n), lambda i: (i, col))
    prev = lambda col: pl.BlockSpec((HALO, D_CONV), lambda i: (jnp.maximum(i * hb - 1, 0), col))
    vec = lambda n: _resident((1, n))
    return pl.pallas_call(
        body, name="mixer_tail_bwd", grid=(T // tm,),
        in_specs=[row(D_MODEL), row(D_MODEL), vec(D_MODEL), _resident((D_MODEL, D_MODEL)), row(D_ATTN),
                  row(D_CONV, 0), row(D_CONV, 1), row(D_CONV, 2), prev(1), prev(2), _resident((3, D_CONV)),
                  vec(D_ATTN), vec(D_CONV)],
        out_specs=[row(D_MODEL), row(D_MODEL), row(D_ATTN), row(D_ATTN), row(D_CONV),
                   vec(D_MODEL), vec(D_MODEL), vec(D_ATTN), vec(D_CONV)],
        out_shape=[jax.ShapeDtypeStruct((T, D_MODEL), F32), jax.ShapeDtypeStruct((T, D_MODEL), BF16),
                   jax.ShapeDtypeStruct((T, D_ATTN), BF16), jax.ShapeDtypeStruct((T, D_ATTN), F32),
                   jax.ShapeDtypeStruct((T, D_CONV), F32), jax.ShapeDtypeStruct((1, D_MODEL), F32),
                   jax.ShapeDtypeStruct((1, D_MODEL), F32), jax.ShapeDtypeStruct((1, D_ATTN), F32),
                   jax.ShapeDtypeStruct((1, D_CONV), F32)],
        compiler_params=_cp(1),
    )(dx2, r2, lg, w_mo, o, bchf, bchf, bchf, bchf, bchf, conv_w, g_attn, g_conv)


def _conv_bwd(dco, bchf, conv_w):
    T = dco.shape[0]
    tm = _tile(T, 512)
    hb = tm // HALO
    nt = T // tm

    def body(dco_ref, dcon_ref, b_ref, bn_ref, c_ref, h_ref, cp_ref, hp_ref, w_ref, dbch_ref, dw_ref):
        i = pl.program_id(0)

        @pl.when(i == 0)
        def _():
            dw_ref[...] = jnp.zeros_like(dw_ref)

        cc = c_ref[...]
        hh = h_ref[...]
        u, u1, u2, y = _conv_fwd(cc, hh, cp_ref[...], hp_ref[...], w_ref, i == 0, tm)
        dco = dco_ref[...]
        bb = b_ref[...]
        dyc = dco * bb
        dy_next = jnp.where(i == nt - 1, 0.0, dcon_ref[...] * bn_ref[...])
        d1 = _shift_rows(dyc, dy_next, 1, tm, False)
        d2 = _shift_rows(dyc, dy_next, 2, tm, False)
        du = w_ref[2:3, :] * dyc + w_ref[1:2, :] * d1 + w_ref[0:1, :] * d2
        dbch_ref[:, 0:D_CONV] = (dco * y).astype(BF16)
        dbch_ref[:, D_CONV:2 * D_CONV] = (du * hh).astype(BF16)
        dbch_ref[:, 2 * D_CONV:] = (du * cc).astype(BF16)
        dw_ref[0:1, :] += _rowsum(dyc * u2)
        dw_ref[1:2, :] += _rowsum(dyc * u1)
        dw_ref[2:3, :] += _rowsum(dyc * u)

    row = lambda n, col=0: pl.BlockSpec((tm, n), lambda i: (i, col))
    prev = lambda col: pl.BlockSpec((HALO, D_CONV), lambda i: (jnp.maximum(i * hb - 1, 0), col))
    nxt = lambda col: pl.BlockSpec((HALO, D_CONV), lambda i: (jnp.minimum((i + 1) * hb, T // HALO - 1), col))
    return pl.pallas_call(
        body, name="conv_bwd", grid=(nt,),
        in_specs=[row(D_CONV), nxt(0), row(D_CONV, 0), nxt(0), row(D_CONV, 1), row(D_CONV, 2), prev(1), prev(2),
                  _resident((3, D_CONV))],
        out_specs=[row(3 * D_CONV), _resident((8, D_CONV))],
        out_shape=[jax.ShapeDtypeStruct((T, 3 * D_CONV), BF16), jax.ShapeDtypeStruct((8, D_CONV), F32)],
        compiler_params=_cp(1),
    )(dco, dco, bchf, bchf, bchf, bchf, bchf, bchf, conv_w)


def _mixer_in_bwd(dx1a, dqkv, dbch, dfl, w_qkv, w_bch, w_f):
    T = dx1a.shape[0]
    tm = _tile(T, 512)

    def body(a_ref, dq_ref, db_ref, df_ref, wq_ref, wb_ref, wf_ref, o_ref):
        o_ref[...] = (a_ref[...] + _nt(dq_ref[...], wq_ref[...]) + _nt(db_ref[...], wb_ref[...])
                      + _nt(df_ref[...], wf_ref[...]))

    row = lambda n: pl.BlockSpec((tm, n), lambda i: (i, 0))
    return pl.pallas_call(
        body, name="mixer_in_bwd", grid=(T // tm,),
        in_specs=[row(D_MODEL), row(3 * D_ATTN), row(3 * D_CONV), row(N_FLOG),
                  _resident((D_MODEL, 3 * D_ATTN)), _resident((D_MODEL, 3 * D_CONV)), _resident((D_MODEL, N_FLOG))],
        out_specs=row(D_MODEL),
        out_shape=jax.ShapeDtypeStruct((T, D_MODEL), F32),
        compiler_params=_cp(1),
    )(dx1a, dqkv, dbch, dfl, w_qkv, w_bch, w_f)


def _ple_loss(x3, p, tgt, w_g, w_p, b_g, lg, lb):
    T = x3.shape[0]
    tm = _tile(T, 512)

    def body(x_ref, p_ref, t_ref, wg_ref, wp_ref, bg_ref, lg_ref, lb_ref,
             dx_ref, de_ref, dz_ref, loss_ref, dlg_ref, dlb_ref, dbg_ref):
        @pl.when(pl.program_id(0) == 0)
        def _():
            for ref in (loss_ref, dlg_ref, dlb_ref, dbg_ref):
                ref[...] = jnp.zeros_like(ref)

        xf = x_ref[...]
        gate = _sigmoid(_nn(xf.astype(BF16), wg_ref[...]) + bg_ref[...])
        e = _nn(p_ref[...].astype(BF16), wp_ref[...])
        xhat, rstd = _ln_stats(ALPHA * xf + gate * e)
        err = xhat * lg_ref[...] + lb_ref[...] - t_ref[...]
        sq = jnp.sum(_rowsum(err * err), axis=-1, keepdims=True)
        loss_ref[...] += jnp.broadcast_to(sq * (0.5 / D_MODEL), loss_ref.shape)
        dy = err * (1.0 / D_MODEL)
        dr = _ln_bwd(dy, xhat, rstd, lg_ref[...])
        dlg_ref[...] += _rowsum(dy * xhat)
        dlb_ref[...] += _rowsum(dy)
        de_ref[...] = (dr * gate).astype(BF16)
        dz = dr * e * gate * (1.0 - gate)
        dbg_ref[...] += _rowsum(dz)
        dzb = dz.astype(BF16)
        dz_ref[...] = dzb
        dx_ref[...] = ALPHA * dr + _nt(dzb, wg_ref[...])

    row = lambda n: pl.BlockSpec((tm, n), lambda i: (i, 0))
    vec = lambda n: _resident((1, n))
    return pl.pallas_call(
        body, name="ple_loss", grid=(T // tm,),
        in_specs=[row(D_MODEL), row(PLE_DIM), row(D_MODEL), _resident((D_MODEL, D_MODEL)),
                  _resident((PLE_DIM, D_MODEL)), vec(D_MODEL), vec(D_MODEL), vec(D_MODEL)],
        out_specs=[row(D_MODEL), row(D_MODEL), row(D_MODEL), vec(LANE), vec(D_MODEL), vec(D_MODEL), vec(D_MODEL)],
        out_shape=[jax.ShapeDtypeStruct((T, D_MODEL), F32), jax.ShapeDtypeStruct((T, D_MODEL), BF16),
                   jax.ShapeDtypeStruct((T, D_MODEL), BF16), jax.ShapeDtypeStruct((1, LANE), F32),
                   jax.ShapeDtypeStruct((1, D_MODEL), F32), jax.ShapeDtypeStruct((1, D_MODEL), F32),
                   jax.ShapeDtypeStruct((1, D_MODEL), F32)],
        compiler_params=_cp(1),
    )(x3, p, tgt, w_g, w_p, b_g, lg, lb)


def _lane_layout(v8):
    return jnp.repeat(v8, HEAD_DIM, axis=1)


def _row_layout(v8):
    t = v8.shape[0]
    return jnp.pad(v8.T.reshape(N_HEADS // 2, 2, t), ((0, 0), (0, 6), (0, 0)))


def _from_lane_layout(vl):
    return vl[:, ::HEAD_DIM]


def _from_row_layout(vr):
    return vr[:, :2, :].reshape(N_HEADS, -1).T


def _local_step(x, p, tgt, w):
    bf = lambda a: a.astype(BF16)
    w1i, w1o, w2i, w2o = bf(w["ffn1_w_in"]), bf(w["ffn1_w_out"]), bf(w["ffn2_w_in"]), bf(w["ffn2_w_out"])
    wmi = w["w_mix_in"]
    o_f = 3 * D_ATTN
    o_b = o_f + N_HEADS
    w_qkv = bf(wmi[:, :o_f])
    w_f = bf(jnp.pad(wmi[:, o_f:o_b], ((0, 0), (0, N_FLOG - N_HEADS))))
    w_bch = bf(wmi[:, o_b:])
    w_bchf = jnp.concatenate([w_bch, w_f], axis=1)
    w_mo, w_g, w_p = bf(w["w_mix_out"]), bf(w["w_ple_gate"]), bf(w["w_ple"])
    b_f = jnp.pad(w["b_forget"], ((0, 0), (0, N_FLOG - N_HEADS)))

    x1, r1, g1, u1, h1 = _ffn_fwd(x, w1i, w1o, w["ln1_g"], w["ln1_b"], "ffn1_fwd")
    q_scale = jnp.concatenate([jnp.full((1, D_ATTN), Q_SCALE, F32), jnp.ones((1, 2 * D_ATTN), F32)], axis=1)
    qkv = _matmul_nn(x1, w_qkv, q_scale, BF16, "proj_qkv")
    bchf = _matmul_nn(x1, w_bchf, jnp.ones((1, 3 * D_CONV + N_FLOG), F32), F32, "proj_bchf")
    fcol = 3 * D_CONV // N_FLOG
    c = _forget_cumsum(bchf, fcol, b_f)
    c8 = c[:, :N_HEADS]
    cb = _lane_layout(c8)
    o, al = _attn_fwd(qkv, cb, _row_layout(c8))
    x2, r2, merged = _mixer_tail_fwd(o, bchf, w["conv_w"], w["g_attn"], w["g_conv"], w_mo, x1, w["ln2_g"], w["ln2_b"])
    x3, r3, g2, u2, h2 = _ffn_fwd(x2, w2i, w2o, w["ln3_g"], w["ln3_b"], "ffn2_fwd")

    grads = {}
    dx3, de, dz, loss, grads["ln4_g"], grads["ln4_b"], grads["b_ple_gate"] = _ple_loss(
        x3, p, tgt, w_g, w_p, w["b_ple_gate"], w["ln4_g"], w["ln4_b"])
    grads["w_ple"] = _matmul_tn(p, de, "dw_ple")
    grads["w_ple_gate"] = _matmul_tn(x3, dz, "dw_ple_gate")

    dx2, dgu2, df2, grads["ln3_g"], grads["ln3_b"] = _ffn_bwd(dx3, r3, g2, u2, w2i, w2o, w["ln3_g"], "ffn2_bwd")
    grads["ffn2_w_in"] = _matmul_tn(x2, dgu2, "dw_ffn2_in")
    grads["ffn2_w_out"] = _matmul_tn(h2, df2, "dw_ffn2_out")

    (dx1a, dr2, dob, delta, dco, grads["ln2_g"], grads["ln2_b"], grads["g_attn"], grads["g_conv"]) = _mixer_tail_bwd(
        dx2, r2, w["ln2_g"], w_mo, o, bchf, w["conv_w"], w["g_attn"], w["g_conv"])
    grads["w_mix_out"] = _matmul_tn(merged, dr2, "dw_mix_out")
    dbch, dcw = _conv_bwd(dco, bchf, w["conv_w"])
    grads["conv_w"] = dcw[:3]
    dq, dk, dv, dck, dcq = _attn_bwd(qkv, dob, cb, _row_layout(_from_lane_layout(al)),
                                     _row_layout(_from_lane_layout(delta)))
    pad_heads = lambda v8: jnp.pad(v8, ((0, 0), (0, N_FLOG - N_HEADS)))
    dfl, dbf = _forget_bwd(pad_heads(_from_lane_layout(dck)), pad_heads(_from_row_layout(dcq)), bchf, fcol, b_f)
    grads["b_forget"] = dbf[:, :N_HEADS]
    dqkv = jnp.concatenate([bf(dq), bf(dk), bf(dv)], axis=1)
    dx1 = _mixer_in_bwd(dx1a, dqkv, dbch, dfl, w_qkv, w_bch, w_f)
    grads["w_mix_in"] = jnp.concatenate(
        [_matmul_tn(x1, dqkv, "dw_qkv"), _matmul_tn(x1, dfl, "dw_flog")[:, :N_HEADS], _matmul_tn(x1, dbch, "dw_bch")],
        axis=1)

    dx0, dgu1, df1, grads["ln1_g"], grads["ln1_b"] = _ffn_bwd(dx1, r1, g1, u1, w1i, w1o, w["ln1_g"], "ffn1_bwd")
    grads["ffn1_w_in"] = _matmul_tn(x, dgu1, "dw_ffn1_in")
    grads["ffn1_w_out"] = _matmul_tn(h1, df1, "dw_ffn1_out")
    return loss, dx0, grads


WEIGHTS = ["ffn1_w_in", "ffn1_w_out", "ln1_g", "ln1_b", "w_mix_in", "b_forget", "conv_w", "g_attn", "g_conv",
           "w_mix_out", "ln2_g", "ln2_b", "ffn2_w_in", "ffn2_w_out", "ln3_g", "ln3_b", "w_ple", "w_ple_gate",
           "b_ple_gate", "ln4_g", "ln4_b"]
LAYOUT = {
    "ffn1_w_in": ((D_MODEL, 2 * D_FF), 1), "ffn1_w_out": ((D_FF, D_MODEL), 0),
    "w_mix_in": ((D_MODEL, 3 * D_ATTN + N_HEADS + 3 * D_CONV), 1), "conv_w": ((3, D_CONV), 1),
    "w_mix_out": ((D_MODEL, D_MODEL), 0), "ffn2_w_in": ((D_MODEL, 2 * D_FF), 1), "ffn2_w_out": ((D_FF, D_MODEL), 0),
    "w_ple": ((PLE_DIM, D_MODEL), 1), "w_ple_gate": ((D_MODEL, D_MODEL), 0),
    "ln1_g": ((1, D_MODEL), None), "ln1_b": ((1, D_MODEL), None), "b_forget": ((1, N_HEADS), None),
    "g_attn": ((1, D_ATTN), None), "g_conv": ((1, D_CONV), None), "ln2_g": ((1, D_MODEL), None),
    "ln2_b": ((1, D_MODEL), None), "ln3_g": ((1, D_MODEL), None), "ln3_b": ((1, D_MODEL), None),
    "b_ple_gate": ((1, D_MODEL), None), "ln4_g": ((1, D_MODEL), None), "ln4_b": ((1, D_MODEL), None),
}
ROW = 1024
ROW_ALIGN = 64


def _shard_shape(name):
    shape, axis = LAYOUT[name]
    if axis is None:
        return shape
    return tuple(s // N_CHIPS if a == axis else s for a, s in enumerate(shape))


def _piece_len(name):
    n = math.prod(_shard_shape(name))
    return -(-n // ROW) * ROW


PAYLOAD_LEN = sum(_piece_len(n) for n in WEIGHTS)
PAYLOAD_ROWS = -(-PAYLOAD_LEN // (ROW * ROW_ALIGN)) * ROW_ALIGN


def _pack(pieces):
    lead = pieces[WEIGHTS[0]].shape[:-len(_shard_shape(WEIGHTS[0]))]
    flat = []
    for n in WEIGHTS:
        a = pieces[n].reshape(lead + (-1,))
        flat.append(jnp.pad(a, [(0, 0)] * len(lead) + [(0, _piece_len(n) - a.shape[-1])]))
    tail = PAYLOAD_ROWS * ROW - PAYLOAD_LEN
    if tail:
        flat.append(jnp.zeros(lead + (tail,), flat[0].dtype))
    return jnp.concatenate(flat, axis=-1).reshape(lead + (PAYLOAD_ROWS, ROW))


def _unpack(payload):
    lead = payload.shape[:-2]
    flat = payload.reshape(lead + (-1,))
    out, off = {}, 0
    for n in WEIGHTS:
        ss = _shard_shape(n)
        out[n] = flat[..., off:off + math.prod(ss)].reshape(lead + ss)
        off += _piece_len(n)
    return out


def _split_chips(name, full):
    shape, axis = LAYOUT[name]
    if axis is None:
        return jnp.broadcast_to(full[None], (N_CHIPS,) + shape)
    if axis == 0:
        return full.reshape((N_CHIPS, shape[0] // N_CHIPS) + shape[1:])
    return jnp.moveaxis(full.reshape(shape[:1] + (N_CHIPS, shape[1] // N_CHIPS)), 1, 0)


def _join_chips(name, parts):
    shape, axis = LAYOUT[name]
    if axis is None:
        return parts[0]
    if axis == 0:
        return parts.reshape(shape)
    return jnp.moveaxis(parts, 0, 1).reshape(shape)


def _place():
    x, y, c = lax.axis_index("x"), lax.axis_index("y"), lax.axis_index("c")
    others = [(1 - x, y), (x, 1 - y), (1 - x, 1 - y)]
    return x, y, c, others


def _all_gather_weights(wp):
    half = wp.shape[1:]

    def body(w_ref, out_ref, send_sems, recv_sems, local_sem):
        x, y, c, others = _place()
        s = 2 * x + y
        mine = pltpu.make_async_copy(w_ref, out_ref.at[s], local_sem)
        mine.start()

        def copy(k, chip, half_idx, to):
            slot = out_ref.at[2 * chip[0] + chip[1], half_idx]
            return pltpu.make_async_remote_copy(src_ref=slot, dst_ref=slot, send_sem=send_sems.at[k],
                                                recv_sem=recv_sems.at[k], device_id=to, device_id_type=MESH)

        first = []
        for j, chip in enumerate(others):
            slot = out_ref.at[s, c]
            first.append(pltpu.make_async_remote_copy(src_ref=w_ref.at[c], dst_ref=slot, send_sem=send_sems.at[j],
                                                      recv_sem=recv_sems.at[j], device_id=(*chip, c),
                                                      device_id_type=MESH))
        for cp in first:
            cp.start()
        passed = [copy(3 + j, chip, c, (x, y, 1 - c)) for j, chip in enumerate(others)]
        for j, chip in enumerate(others):
            copy(j, chip, c, (x, y, c)).wait_recv()
            passed[j].start()
        for j, chip in enumerate(others):
            copy(3 + j, chip, 1 - c, (x, y, c)).wait_recv()
        for cp in first + passed:
            cp.wait_send()
        mine.wait()

    any_spec = pl.BlockSpec(memory_space=pl.ANY)
    return pl.pallas_call(
        body, name="all_gather_weights",
        out_shape=jax.ShapeDtypeStruct((N_CHIPS, 2) + half, wp.dtype),
        in_specs=[any_spec], out_specs=any_spec,
        scratch_shapes=[pltpu.SemaphoreType.DMA((6,)), pltpu.SemaphoreType.DMA((6,)), pltpu.SemaphoreType.DMA],
    )(wp)


def _swap_halves(g):
    rh = g.shape[2]

    def body(g_ref, out_ref, send_sem, recv_sem):
        x, y, c, _ = _place()
        cp = pltpu.make_async_remote_copy(src_ref=g_ref.at[:, 1 - c], dst_ref=out_ref, send_sem=send_sem,
                                          recv_sem=recv_sem, device_id=(x, y, 1 - c), device_id_type=MESH)
        cp.start()
        cp.wait()

    any_spec = pl.BlockSpec(memory_space=pl.ANY)
    return pl.pallas_call(
        body, name="grad_swap_halves",
        out_shape=jax.ShapeDtypeStruct((N_CHIPS, rh, ROW), g.dtype),
        in_specs=[any_spec], out_specs=any_spec,
        scratch_shapes=[pltpu.SemaphoreType.DMA, pltpu.SemaphoreType.DMA],
    )(g)


def _exchange_chips(pp):
    def body(p_ref, out_ref, send_sems, recv_sems, local_sem):
        x, y, c, others = _place()
        s = 2 * x + y
        mine = pltpu.make_async_copy(p_ref.at[s], out_ref.at[s], local_sem)
        mine.start()
        sends = []
        for j, chip in enumerate(others):
            sends.append(pltpu.make_async_remote_copy(
                src_ref=p_ref.at[2 * chip[0] + chip[1]], dst_ref=out_ref.at[s], send_sem=send_sems.at[j],
                recv_sem=recv_sems.at[j], device_id=(*chip, c), device_id_type=MESH))
        for cp in sends:
            cp.start()
        for j, chip in enumerate(others):
            slot = out_ref.at[2 * chip[0] + chip[1]]
            pltpu.make_async_remote_copy(src_ref=slot, dst_ref=slot, send_sem=send_sems.at[j],
                                         recv_sem=recv_sems.at[j], device_id=(x, y, c),
                                         device_id_type=MESH).wait_recv()
        for cp in sends:
            cp.wait_send()
        mine.wait()

    any_spec = pl.BlockSpec(memory_space=pl.ANY)
    return pl.pallas_call(
        body, name="grad_exchange_chips",
        out_shape=jax.ShapeDtypeStruct(pp.shape, pp.dtype),
        in_specs=[any_spec], out_specs=any_spec,
        scratch_shapes=[pltpu.SemaphoreType.DMA((3,)), pltpu.SemaphoreType.DMA((3,)), pltpu.SemaphoreType.DMA],
    )(pp)


def _share_half(r):
    def body(r_ref, out_ref, send_sem, recv_sem, local_sem):
        x, y, c, _ = _place()
        mine = pltpu.make_async_copy(r_ref, out_ref.at[c], local_sem)
        mine.start()
        cp = pltpu.make_async_remote_copy(src_ref=r_ref, dst_ref=out_ref.at[c], send_sem=send_sem, recv_sem=recv_sem,
                                          device_id=(x, y, 1 - c), device_id_type=MESH)
        cp.start()
        slot = out_ref.at[1 - c]
        pltpu.make_async_remote_copy(src_ref=slot, dst_ref=slot, send_sem=send_sem, recv_sem=recv_sem,
                                     device_id=(x, y, c), device_id_type=MESH).wait_recv()
        cp.wait_send()
        mine.wait()

    any_spec = pl.BlockSpec(memory_space=pl.ANY)
    return pl.pallas_call(
        body, name="grad_share_half",
        out_shape=jax.ShapeDtypeStruct((2,) + r.shape, r.dtype),
        in_specs=[any_spec], out_specs=any_spec,
        scratch_shapes=[pltpu.SemaphoreType.DMA, pltpu.SemaphoreType.DMA, pltpu.SemaphoreType.DMA],
    )(r)


def _add_my_half(g, sib, c_idx):
    rh = g.shape[2]
    tr = _tile(rh, 512)

    def body(c_ref, g_ref, s_ref, o_ref):
        o_ref[...] = g_ref[...] + s_ref[...]

    return pl.pallas_call(
        body, name="grad_add_halves",
        grid_spec=pltpu.PrefetchScalarGridSpec(
            num_scalar_prefetch=1, grid=(N_CHIPS, rh // tr),
            in_specs=[pl.BlockSpec((None, None, tr, ROW), lambda s, i, c: (s, c[0], i, 0)),
                      pl.BlockSpec((None, tr, ROW), lambda s, i, c: (s, i, 0))],
            out_specs=pl.BlockSpec((None, tr, ROW), lambda s, i, c: (s, i, 0))),
        out_shape=jax.ShapeDtypeStruct((N_CHIPS, rh, ROW), g.dtype),
        compiler_params=_cp(2),
    )(c_idx, g, sib)


def _sum_chips(parts):
    rh = parts.shape[1]
    tr = _tile(rh, 512)

    def body(p_ref, o_ref):
        o_ref[...] = ((p_ref[0] + p_ref[1]) + p_ref[2]) + p_ref[3]

    return pl.pallas_call(
        body, name="grad_sum_chips", grid=(rh // tr,),
        in_specs=[pl.BlockSpec((N_CHIPS, tr, ROW), lambda i: (0, i, 0))],
        out_specs=pl.BlockSpec((tr, ROW), lambda i: (i, 0)),
        out_shape=jax.ShapeDtypeStruct((rh, ROW), parts.dtype),
        compiler_params=_cp(1),
    )(parts)


def _adamw(w, g, m, v):
    rows = w.shape[0]
    tr = _tile(rows, 512)
    c1 = 1.0 - ADAM_B1 ** ADAM_STEP
    c2 = 1.0 - ADAM_B2 ** ADAM_STEP

    def body(w_ref, g_ref, m_ref, v_ref, d_ref, mo_ref, vo_ref):
        g = g_ref[...]
        m = ADAM_B1 * m_ref[...] + (1.0 - ADAM_B1) * g
        v = ADAM_B2 * v_ref[...] + (1.0 - ADAM_B2) * (g * g)
        mo_ref[...] = m
        vo_ref[...] = v
        d_ref[...] = -ADAM_LR * ((m / c1) / (jnp.sqrt(v / c2) + ADAM_EPS) + ADAM_WD * w_ref[...])

    spec = pl.BlockSpec((tr, ROW), lambda i: (i, 0))
    return pl.pallas_call(
        body, name="adamw", grid=(rows // tr,),
        in_specs=[spec] * 4, out_specs=[spec] * 3,
        out_shape=[jax.ShapeDtypeStruct(w.shape, F32)] * 3,
        compiler_params=_cp(1),
    )(w, g, m, v)


def kernel(x, p, ffn1_w_in, ffn1_w_out, ln1_g, ln1_b, w_mix_in, b_forget, conv_w, g_attn, g_conv, w_mix_out, ln2_g, ln2_b, ffn2_w_in, ffn2_w_out, ln3_g, ln3_b, w_ple, w_ple_gate, b_ple_gate, ln4_g, ln4_b, loss_target, m_ffn1_w_in, m_ffn1_w_out, m_ln1_g, m_ln1_b, m_w_mix_in, m_b_forget, m_conv_w, m_g_attn, m_g_conv, m_w_mix_out, m_ln2_g, m_ln2_b, m_ffn2_w_in, m_ffn2_w_out, m_ln3_g, m_ln3_b, m_w_ple, m_w_ple_gate, m_b_ple_gate, m_ln4_g, m_ln4_b, v_ffn1_w_in, v_ffn1_w_out, v_ln1_g, v_ln1_b, v_w_mix_in, v_b_forget, v_conv_w, v_g_attn, v_g_conv, v_w_mix_out, v_ln2_g, v_ln2_b, v_ffn2_w_in, v_ffn2_w_out, v_ln3_g, v_ln3_b, v_w_ple, v_w_ple_gate, v_b_ple_gate, v_ln4_g, v_ln4_b):
    args = dict(locals())
    shard = {n: args[n][0] if LAYOUT[n][1] is not None else args[n] for n in WEIGHTS}
    m_shard = {n: args["m_" + n][0] if LAYOUT[n][1] is not None else args["m_" + n] for n in WEIGHTS}
    v_shard = {n: args["v_" + n][0] if LAYOUT[n][1] is not None else args["v_" + n] for n in WEIGHTS}
    rh = PAYLOAD_ROWS // 2

    w_pay = _pack(shard)
    gathered = _all_gather_weights(w_pay.astype(BF16).reshape(2, rh, ROW))
    parts = _unpack(gathered.reshape(N_CHIPS, PAYLOAD_ROWS, ROW))
    full = {n: _join_chips(n, parts[n]) for n in WEIGHTS}
    full.update({n: shard[n] for n in WEIGHTS if LAYOUT[n][1] is None})

    loss_acc, grad_x, grads = _local_step(x[0], p[0, 0], loss_target[0], full)
    loss = lax.psum(loss_acc[0, 0], ("x", "y", "c"))

    g_pay = _pack({n: _split_chips(n, grads[n]) for n in WEIGHTS}).reshape(N_CHIPS, 2, rh, ROW)
    c_idx = lax.axis_index("c").astype(jnp.int32).reshape(1)
    chip_sums = _add_my_half(g_pay, _swap_halves(g_pay), c_idx)
    reduced = _share_half(_sum_chips(_exchange_chips(chip_sums))).reshape(PAYLOAD_ROWS, ROW)

    delta, new_m, new_v = _adamw(w_pay, reduced, _pack(m_shard), _pack(v_shard))

    def outs(payload):
        u = _unpack(payload)
        return [u[n][None] if LAYOUT[n][1] is not None else u[n] for n in WEIGHTS]

    return (loss, grad_x[None], *outs(reduced), *outs(delta), *outs(new_m), *outs(new_v))
```

```python
import functools
import math

import jax
import jax.numpy as jnp
from jax import lax
from jax.experimental import pallas as pl
from jax.experimental.pallas import tpu as pltpu

F32 = jnp.float32
BF16 = jnp.bfloat16

D_MODEL = 1024
D_FF = 2816
N_HEADS = 8
HEAD_DIM = 64
D_ATTN = N_HEADS * HEAD_DIM
D_CONV = 512
PLE_DIM = 256
N_FLOG = 128
ALPHA = 2.0 ** 0.25
LN_EPS = 1e-5
RMS_EPS = 1e-6
NEG_INF = -1e30
Q_SCALE = 1.0 / math.sqrt(HEAD_DIM)

ADAM_LR = 0.001
ADAM_B1 = 0.9
ADAM_B2 = 0.999
ADAM_EPS = 1e-08
ADAM_WD = 0.01
ADAM_STEP = 10

V7X_VMEM_BYTES = 64 << 20
VMEM_LIMIT = V7X_VMEM_BYTES - (8 << 20)
LANE = 128
FF_CHUNK = 256
N_CHIPS = 4
MESH = pl.DeviceIdType.MESH


def _cp(n_axes):
    return pltpu.CompilerParams(dimension_semantics=("arbitrary",) * n_axes, vmem_limit_bytes=VMEM_LIMIT)


def _resident(shape):
    n = len(shape)
    return pl.BlockSpec(shape, lambda *_: (0,) * n, pipeline_mode=pl.Buffered(1))


def _nn(a, b):
    return jnp.dot(a, b, preferred_element_type=F32)


def _nt(a, b):
    return lax.dot_general(a, b, (((1,), (1,)), ((), ())), preferred_element_type=F32)


def _tn(a, b):
    return lax.dot_general(a, b, (((0,), (0,)), ((), ())), preferred_element_type=F32)


def _ln_stats(r):
    mu = jnp.mean(r, axis=-1, keepdims=True)
    xc = r - mu
    var = jnp.mean(xc * xc, axis=-1, keepdims=True)
    rstd = lax.rsqrt(var + LN_EPS)
    return xc * rstd, rstd


def _ln_bwd(dy, xhat, rstd, g):
    dxh = dy * g
    m1 = jnp.mean(dxh, axis=-1, keepdims=True)
    m2 = jnp.mean(dxh * xhat, axis=-1, keepdims=True)
    return rstd * (dxh - m1 - xhat * m2)


def _sigmoid(z):
    return 1.0 / (1.0 + jnp.exp(-z))


def _rowsum(a):
    return jnp.sum(a, axis=0, keepdims=True)


def _tile(total, want):
    if total <= want:
        return total
    for t in range(want - want % 8, 0, -8):
        if total % t == 0:
            return t
    raise ValueError((total, want))


def _ffn_fwd(x, w_in, w_out, lg, lb, name):
    T = x.shape[0]
    tm = _tile(T, 512)
    nf = D_FF // FF_CHUNK

    def body(x_ref, wi_ref, wo_ref, lg_ref, lb_ref, xo_ref, r_ref, g_ref, u_ref, h_ref):
        xf = x_ref[...]
        xb = xf.astype(BF16)
        acc = jnp.zeros((tm, D_MODEL), F32)
        for j in range(nf):
            c0 = j * FF_CHUNK
            g = _nn(xb, wi_ref[:, c0:c0 + FF_CHUNK])
            u = _nn(xb, wi_ref[:, D_FF + c0:D_FF + c0 + FF_CHUNK])
            hb = (g * _sigmoid(g) * u).astype(BF16)
            g_ref[:, c0:c0 + FF_CHUNK] = g.astype(BF16)
            u_ref[:, c0:c0 + FF_CHUNK] = u.astype(BF16)
            h_ref[:, c0:c0 + FF_CHUNK] = hb
            acc = acc + _nn(hb, wo_ref[c0:c0 + FF_CHUNK, :])
        r = ALPHA * xf + 0.5 * acc
        r_ref[...] = r
        xhat, _ = _ln_stats(r)
        xo_ref[...] = xhat * lg_ref[...] + lb_ref[...]

    row = lambda n: pl.BlockSpec((tm, n), lambda i: (i, 0))
    return pl.pallas_call(
        body, name=name, grid=(T // tm,),
        in_specs=[row(D_MODEL), _resident((D_MODEL, 2 * D_FF)), _resident((D_FF, D_MODEL)),
                  _resident((1, D_MODEL)), _resident((1, D_MODEL))],
        out_specs=[row(D_MODEL), row(D_MODEL), row(D_FF), row(D_FF), row(D_FF)],
        out_shape=[jax.ShapeDtypeStruct((T, D_MODEL), F32), jax.ShapeDtypeStruct((T, D_MODEL), F32),
                   jax.ShapeDtypeStruct((T, D_FF), BF16), jax.ShapeDtypeStruct((T, D_FF), BF16),
                   jax.ShapeDtypeStruct((T, D_FF), BF16)],
        compiler_params=_cp(1),
    )(x, w_in, w_out, lg, lb)


def _ffn_bwd(dxo, r, g, u, w_in, w_out, lg, name):
    T = r.shape[0]
    tm = _tile(T, 256)
    nf = D_FF // FF_CHUNK

    def body(dxo_ref, r_ref, g_ref, u_ref, wi_ref, wo_ref, lg_ref, dx_ref, dgu_ref, df_ref, dlg_ref, dlb_ref):
        i = pl.program_id(0)
        dy = dxo_ref[...]
        xhat, rstd = _ln_stats(r_ref[...])
        dr = _ln_bwd(dy, xhat, rstd, lg_ref[...])

        @pl.when(i == 0)
        def _():
            dlg_ref[...] = jnp.zeros_like(dlg_ref)
            dlb_ref[...] = jnp.zeros_like(dlb_ref)

        dlg_ref[...] += _rowsum(dy * xhat)
        dlb_ref[...] += _rowsum(dy)
        dfb = (0.5 * dr).astype(BF16)
        df_ref[...] = dfb
        acc = jnp.zeros((tm, D_MODEL), F32)
        for j in range(nf):
            c0 = j * FF_CHUNK
            dh = _nt(dfb, wo_ref[c0:c0 + FF_CHUNK, :])
            gg = g_ref[:, c0:c0 + FF_CHUNK].astype(F32)
            uu = u_ref[:, c0:c0 + FF_CHUNK].astype(F32)
            s = _sigmoid(gg)
            dgb = (dh * uu * s * (1.0 + gg * (1.0 - s))).astype(BF16)
            dub = (dh * gg * s).astype(BF16)
            dgu_ref[:, c0:c0 + FF_CHUNK] = dgb
            dgu_ref[:, D_FF + c0:D_FF + c0 + FF_CHUNK] = dub
            acc = acc + _nt(dgb, wi_ref[:, c0:c0 + FF_CHUNK]) + _nt(dub, wi_ref[:, D_FF + c0:D_FF + c0 + FF_CHUNK])
        dx_ref[...] = ALPHA * dr + acc

    row = lambda n: pl.BlockSpec((tm, n), lambda i: (i, 0))
    return pl.pallas_call(
        body, name=name, grid=(T // tm,),
        in_specs=[row(D_MODEL), row(D_MODEL), row(D_FF), row(D_FF), _resident((D_MODEL, 2 * D_FF)),
                  _resident((D_FF, D_MODEL)), _resident((1, D_MODEL))],
        out_specs=[row(D_MODEL), row(2 * D_FF), row(D_MODEL), _resident((1, D_MODEL)), _resident((1, D_MODEL))],
        out_shape=[jax.ShapeDtypeStruct((T, D_MODEL), F32), jax.ShapeDtypeStruct((T, 2 * D_FF), BF16),
                   jax.ShapeDtypeStruct((T, D_MODEL), BF16), jax.ShapeDtypeStruct((1, D_MODEL), F32),
                   jax.ShapeDtypeStruct((1, D_MODEL), F32)],
        compiler_params=_cp(1),
    )(dxo, r, g, u, w_in, w_out, lg)


def _matmul_tn(a, b, name, tn=None):
    T, K = a.shape
    N = b.shape[1]
    tt = _tile(T, 512)
    if tn is None:
        tn = N
        while K * tn * 4 > (6 << 20) and tn % 256 == 0:
            tn //= 2
    assert N % tn == 0

    def body(a_ref, b_ref, o_ref):
        @pl.when(pl.program_id(1) == 0)
        def _():
            o_ref[...] = jnp.zeros_like(o_ref)

        o_ref[...] += _tn(a_ref[...].astype(BF16), b_ref[...].astype(BF16))

    return pl.pallas_call(
        body, name=name, grid=(N // tn, T // tt),
        in_specs=[pl.BlockSpec((tt, K), lambda n, t: (t, 0)), pl.BlockSpec((tt, tn), lambda n, t: (t, n))],
        out_specs=pl.BlockSpec((K, tn), lambda n, t: (0, n)),
        out_shape=jax.ShapeDtypeStruct((K, N), F32),
        compiler_params=_cp(2),
    )(a, b)


def _matmul_nn(x, w, scale, out_dtype, name):
    T, K = x.shape
    N = w.shape[1]
    tm = _tile(T, 512)

    def body(x_ref, w_ref, s_ref, o_ref):
        o_ref[...] = (_nn(x_ref[...].astype(BF16), w_ref[...]) * s_ref[...]).astype(out_dtype)

    return pl.pallas_call(
        body, name=name, grid=(T // tm,),
        in_specs=[pl.BlockSpec((tm, K), lambda i: (i, 0)), _resident((K, N)), _resident((1, N))],
        out_specs=pl.BlockSpec((tm, N), lambda i: (i, 0)),
        out_shape=jax.ShapeDtypeStruct((T, N), out_dtype),
        compiler_params=_cp(1),
    )(x, w, scale)


def _log_sigmoid(z):
    return jnp.minimum(z, 0.0) - jnp.log1p(jnp.exp(-jnp.abs(z)))


def _tri(n, lower):
    r = lax.broadcasted_iota(jnp.int32, (n, n), 0)
    c = lax.broadcasted_iota(jnp.int32, (n, n), 1)
    return jnp.where((c <= r) if lower else (c >= r), 1.0, 0.0).astype(F32)


def _f32dot(a, b):
    return jnp.dot(a, b, preferred_element_type=F32, precision=lax.Precision.HIGHEST)


def _forget_cumsum(flog, col, bf):
    T = flog.shape[0]
    bt = _tile(T, 512)

    def body(f_ref, b_ref, c_ref, carry):
        @pl.when(pl.program_id(0) == 0)
        def _():
            carry[...] = jnp.zeros_like(carry)

        lf = _log_sigmoid(f_ref[...] + b_ref[...])
        c = _f32dot(_tri(bt, True), lf) + carry[...]
        c_ref[...] = c
        carry[...] = c[bt - 1:bt, :]

    return pl.pallas_call(
        body, name="forget_cumsum", grid=(T // bt,),
        in_specs=[pl.BlockSpec((bt, N_FLOG), lambda i: (i, col)), _resident((1, N_FLOG))],
        out_specs=pl.BlockSpec((bt, N_FLOG), lambda i: (i, 0)),
        out_shape=jax.ShapeDtypeStruct((T, N_FLOG), F32),
        scratch_shapes=[pltpu.VMEM((1, N_FLOG), F32)],
        compiler_params=_cp(1),
    )(flog, bf)


def _forget_bwd(dc, dcq, flog, col, bf):
    T = dc.shape[0]
    bt = _tile(T, 512)
    nb = T // bt

    def body(dc_ref, dcq_ref, f_ref, b_ref, dz_ref, db_ref, carry):
        @pl.when(pl.program_id(0) == 0)
        def _():
            carry[...] = jnp.zeros_like(carry)
            db_ref[...] = jnp.zeros_like(db_ref)

        dlf = _f32dot(_tri(bt, False), dc_ref[...] + dcq_ref[...]) + carry[...]
        carry[...] = dlf[0:1, :]
        z = f_ref[...] + b_ref[...]
        dz = dlf * _sigmoid(-z)
        dz_ref[...] = dz.astype(BF16)
        db_ref[...] += _rowsum(dz)

    return pl.pallas_call(
        body, name="forget_bwd", grid=(nb,),
        in_specs=[pl.BlockSpec((bt, N_FLOG), lambda i: (nb - 1 - i, 0)),
                  pl.BlockSpec((bt, N_FLOG), lambda i: (nb - 1 - i, 0)),
                  pl.BlockSpec((bt, N_FLOG), lambda i: (nb - 1 - i, col)), _resident((1, N_FLOG))],
        out_specs=[pl.BlockSpec((bt, N_FLOG), lambda i: (nb - 1 - i, 0)), _resident((1, N_FLOG))],
        out_shape=[jax.ShapeDtypeStruct((T, N_FLOG), BF16), jax.ShapeDtypeStruct((1, N_FLOG), F32)],
        scratch_shapes=[pltpu.VMEM((1, N_FLOG), F32)],
        compiler_params=_cp(1),
    )(dc, dcq, flog, bf)


def _head_masks():
    lane = lax.broadcasted_iota(jnp.int32, (1, LANE), 1)
    return lane < HEAD_DIM


def _split_heads(x2, is_a):
    zero = jnp.zeros_like(x2)
    return jnp.where(is_a, x2, zero), jnp.where(is_a, zero, x2)


def _attn_fwd(qkv, vt, cb, crow):
    T = qkv.shape[0]
    tq = _tile(T, 512)
    tk = tq
    nq = T // tq
    npair = N_HEADS // 2

    def body(q_ref, k_ref, vt_ref, cb_ref, cr_ref, o_ref, al_ref, m_s, l_s, acc_s):
        i = pl.program_id(1)
        qs = _split_heads(q_ref[...], _head_masks())
        ct = (cr_ref[0, 0:1, :], cr_ref[0, 1:2, :])
        m_s[...] = jnp.full_like(m_s, NEG_INF)
        l_s[...] = jnp.zeros_like(l_s)
        acc_s[...] = jnp.zeros_like(acc_s)

        def step(kk, masked):
            k0 = pl.multiple_of(kk * tk, tk)
            k2 = k_ref[pl.ds(k0, tk), :]
            v2t = vt_ref[:, pl.ds(k0, tk)]
            csb = cb_ref[pl.ds(k0, tk), :]
            for h in range(2):
                cs = csb[:, h * HEAD_DIM:h * HEAD_DIM + 1]
                zt = _nt(k2, qs[h]) + (ct[h] - cs)
                if masked:
                    rr = lax.broadcasted_iota(jnp.int32, (tk, tq), 0)
                    cc = lax.broadcasted_iota(jnp.int32, (tk, tq), 1)
                    zt = jnp.where(cc >= rr, zt, NEG_INF)
                m_old = m_s[h]
                m_new = jnp.maximum(m_old, jnp.max(zt, axis=0, keepdims=True))
                p = jnp.exp(zt - m_new)
                a = jnp.exp(m_old - m_new)
                l_s[h] = a * l_s[h] + jnp.sum(p, axis=0, keepdims=True)
                acc_s[h] = a * acc_s[h] + _nn(v2t, p.astype(BF16))
                m_s[h] = m_new

        def loop_body(kk, carry):
            step(kk, False)
            return carry

        lax.fori_loop(0, i, loop_body, 0)
        step(i, True)
        outs = []
        for h in range(2):
            l = l_s[h]
            outs.append(acc_s[h] * (1.0 / l))
            al_ref[0, h:h + 1, :] = ct[h] - (m_s[h] + jnp.log(l))
        al_ref[0, 2:8, :] = jnp.zeros((6, tq), F32)
        dim = lax.broadcasted_iota(jnp.int32, (LANE, 1), 0)
        o_ref[...] = jnp.where(dim < HEAD_DIM, outs[0], outs[1]).T

    rowl = pl.BlockSpec((1, 8, tq), lambda j, i: (j, 0, i))
    return pl.pallas_call(
        body, name="attn_fwd", grid=(npair, nq),
        in_specs=[pl.BlockSpec((tq, LANE), lambda j, i: (i, j)),
                  pl.BlockSpec((T, LANE), lambda j, i: (0, npair + j), pipeline_mode=pl.Buffered(1)),
                  pl.BlockSpec((LANE, T), lambda j, i: (j, 0), pipeline_mode=pl.Buffered(1)),
                  pl.BlockSpec((T, LANE), lambda j, i: (0, j), pipeline_mode=pl.Buffered(1)),
                  rowl],
        out_specs=[pl.BlockSpec((tq, LANE), lambda j, i: (i, j)), rowl],
        out_shape=[jax.ShapeDtypeStruct((T, D_ATTN), F32), jax.ShapeDtypeStruct((npair, 8, T), F32)],
        scratch_shapes=[pltpu.VMEM((2, 1, tq), F32), pltpu.VMEM((2, 1, tq), F32), pltpu.VMEM((2, LANE, tq), F32)],
        compiler_params=_cp(2),
    )(qkv, qkv, vt, cb, crow)


def _attn_bwd(qkv, dob, cb, alrow, dlrow):
    T = qkv.shape[0]
    tq = _tile(T, 512)
    tk = tq
    nq = T // tq
    npair = N_HEADS // 2

    def body(q_ref, k_ref, v_ref, do_ref, cb_ref, al_ref, dl_ref, dq_ref, dk_ref, dv_ref, dc_ref, dcq_ref,
             dk_s, dv_s, dc_s):
        kj = pl.program_id(1)
        is_a = _head_masks()
        ks = _split_heads(k_ref[...], is_a)
        vs = _split_heads(v_ref[...], is_a)
        cs = (cb_ref[:, 0:1], cb_ref[:, HEAD_DIM:HEAD_DIM + 1])

        @pl.when(kj == 0)
        def _():
            dq_ref[...] = jnp.zeros_like(dq_ref)
            dcq_ref[...] = jnp.zeros_like(dcq_ref)

        dk_s[...] = jnp.zeros_like(dk_s)
        dv_s[...] = jnp.zeros_like(dv_s)
        dc_s[...] = jnp.zeros_like(dc_s)

        def step(qi, masked):
            q0 = pl.multiple_of(qi * tq, tq)
            q2 = q_ref[pl.ds(q0, tq), :]
            do2 = do_ref[pl.ds(q0, tq), :]
            for h in range(2):
                alr = al_ref[0, h:h + 1, pl.ds(q0, tq)]
                dlr = dl_ref[0, h:h + 1, pl.ds(q0, tq)]
                zt = _nt(ks[h], q2) + (alr - cs[h])
                if masked:
                    rr = lax.broadcasted_iota(jnp.int32, (tk, tq), 0)
                    cc = lax.broadcasted_iota(jnp.int32, (tk, tq), 1)
                    zt = jnp.where(cc >= rr, zt, NEG_INF)
                pt = jnp.exp(zt)
                dst = pt * (_nt(vs[h], do2) - dlr)
                pb = pt.astype(BF16)
                dsb = dst.astype(BF16)
                dv_s[h] += _nn(pb, do2)
                dk_s[h] += _nn(dsb, q2)
                dc_s[h] += jnp.sum(dst, axis=-1, keepdims=True)
                dcq_ref[0, h:h + 1, pl.ds(q0, tq)] += jnp.sum(dst, axis=0, keepdims=True)
                dq_ref[pl.ds(q0, tq), :] += _tn(dsb, ks[h])

        step(kj, True)

        def loop_body(qi, carry):
            step(qi, False)
            return carry

        lax.fori_loop(kj + 1, nq, loop_body, 0)
        dk_ref[...] = jnp.where(is_a, dk_s[0], dk_s[1])
        dv_ref[...] = jnp.where(is_a, dv_s[0], dv_s[1])
        dc_ref[...] = jnp.where(is_a, -dc_s[0], -dc_s[1])

        @pl.when(kj == nq - 1)
        def _():
            dq_ref[...] = dq_ref[...] * Q_SCALE

    full = lambda col: pl.BlockSpec((T, LANE), lambda j, kj: (0, col(j)), pipeline_mode=pl.Buffered(1))
    tile = lambda col: pl.BlockSpec((tk, LANE), lambda j, kj: (kj, col(j)))
    rowl = pl.BlockSpec((1, 8, T), lambda j, kj: (j, 0, 0))
    return pl.pallas_call(
        body, name="attn_bwd", grid=(npair, nq),
        in_specs=[full(lambda j: j), tile(lambda j: npair + j), tile(lambda j: 2 * npair + j), full(lambda j: j),
                  tile(lambda j: j), rowl, rowl],
        out_specs=[pl.BlockSpec((T, LANE), lambda j, kj: (0, j)), tile(lambda j: j), tile(lambda j: j),
                   tile(lambda j: j), rowl],
        out_shape=[jax.ShapeDtypeStruct((T, D_ATTN), F32)] * 4 + [jax.ShapeDtypeStruct((npair, 8, T), F32)],
        scratch_shapes=[pltpu.VMEM((2, tk, LANE), F32), pltpu.VMEM((2, tk, LANE), F32), pltpu.VMEM((2, tk, 1), F32)],
        compiler_params=_cp(2),
    )(qkv, qkv, qkv, dob, cb, alrow, dlrow)


HALO = 8


def _shift_rows(cur, other, k, tm, down):
    row = lax.broadcasted_iota(jnp.int32, (tm, 1), 0)
    reps = tm // HALO
    if down:
        rolled = pltpu.roll(cur, k, 0)
        fill = jnp.tile(pltpu.roll(other, k, 0), (reps, 1))
        return jnp.where(row < k, fill, rolled)
    rolled = pltpu.roll(cur, tm - k, 0)
    fill = jnp.tile(pltpu.roll(other, HALO - k, 0), (reps, 1))
    return jnp.where(row >= tm - k, fill, rolled)


def _conv_fwd(c, hh, c_prev, hh_prev, w_ref, first, tm):
    u = c * hh
    u_prev = jnp.where(first, 0.0, c_prev * hh_prev)
    u1 = _shift_rows(u, u_prev, 1, tm, True)
    u2 = _shift_rows(u, u_prev, 2, tm, True)
    y = w_ref[0:1, :] * u2 + w_ref[1:2, :] * u1 + w_ref[2:3, :] * u
    return u, u1, u2, y


def _rms(x, g):
    rs = lax.rsqrt(jnp.mean(x * x, axis=-1, keepdims=True) + RMS_EPS)
    return x * rs * g, rs


def _mixer_tail_fwd(o, bchf, conv_w, g_attn, g_conv, w_mo, x1, lg, lb):
    T = o.shape[0]
    tm = _tile(T, 512)
    hb = tm // HALO

    def body(o_ref, b_ref, c_ref, h_ref, cp_ref, hp_ref, w_ref, ga_ref, gc_ref, wmo_ref, x1_ref, lg_ref, lb_ref,
             x2_ref, r2_ref, mg_ref):
        first = pl.program_id(0) == 0
        _, _, _, y = _conv_fwd(c_ref[...], h_ref[...], cp_ref[...], hp_ref[...], w_ref, first, tm)
        na, _ = _rms(o_ref[...], ga_ref[...])
        nc, _ = _rms(b_ref[...] * y, gc_ref[...])
        nab = na.astype(BF16)
        ncb = nc.astype(BF16)
        mg_ref[:, 0:D_ATTN] = nab
        mg_ref[:, D_ATTN:] = ncb
        r2 = ALPHA * x1_ref[...] + _nn(nab, wmo_ref[0:D_ATTN, :]) + _nn(ncb, wmo_ref[D_ATTN:, :])
        r2_ref[...] = r2
        xhat, _ = _ln_stats(r2)
        x2_ref[...] = xhat * lg_ref[...] + lb_ref[...]

    row = lambda n, col=0: pl.BlockSpec((tm, n), lambda i: (i, col))
    prev = lambda col: pl.BlockSpec((HALO, D_CONV), lambda i: (jnp.maximum(i * hb - 1, 0), col))
    return pl.pallas_call(
        body, name="mixer_tail_fwd", grid=(T // tm,),
        in_specs=[row(D_ATTN), row(D_CONV, 0), row(D_CONV, 1), row(D_CONV, 2), prev(1), prev(2),
                  _resident((3, D_CONV)), _resident((1, D_ATTN)), _resident((1, D_CONV)),
                  _resident((D_MODEL, D_MODEL)), row(D_MODEL), _resident((1, D_MODEL)), _resident((1, D_MODEL))],
        out_specs=[row(D_MODEL), row(D_MODEL), row(D_MODEL)],
        out_shape=[jax.ShapeDtypeStruct((T, D_MODEL), F32), jax.ShapeDtypeStruct((T, D_MODEL), F32),
                   jax.ShapeDtypeStruct((T, D_MODEL), BF16)],
        compiler_params=_cp(1),
    )(o, bchf, bchf, bchf, bchf, bchf, conv_w, g_attn, g_conv, w_mo, x1, lg, lb)


def _head_sum_matrix():
    r = lax.broadcasted_iota(jnp.int32, (D_ATTN, D_ATTN), 0) // HEAD_DIM
    c = lax.broadcasted_iota(jnp.int32, (D_ATTN, D_ATTN), 1) // HEAD_DIM
    return jnp.where(r == c, 1.0, 0.0).astype(F32)


def _mixer_tail_bwd(dx2, r2, lg, w_mo, o, bchf, conv_w, g_attn, g_conv):
    T = o.shape[0]
    tm = _tile(T, 256)
    hb = tm // HALO

    def body(dx2_ref, r2_ref, lg_ref, wmo_ref, o_ref, b_ref, c_ref, h_ref, cp_ref, hp_ref, w_ref, ga_ref, gc_ref,
             dx1_ref, dr_ref, do_ref, dl_ref, dco_ref, dlg_ref, dlb_ref, dga_ref, dgc_ref):
        i = pl.program_id(0)

        @pl.when(i == 0)
        def _():
            for ref in (dlg_ref, dlb_ref, dga_ref, dgc_ref):
                ref[...] = jnp.zeros_like(ref)

        dy = dx2_ref[...]
        xhat, rstd = _ln_stats(r2_ref[...])
        dr = _ln_bwd(dy, xhat, rstd, lg_ref[...])
        dlg_ref[...] += _rowsum(dy * xhat)
        dlb_ref[...] += _rowsum(dy)
        dx1_ref[...] = ALPHA * dr
        drb = dr.astype(BF16)
        dr_ref[...] = drb
        dna = _nt(drb, wmo_ref[0:D_ATTN, :])
        dnc = _nt(drb, wmo_ref[D_ATTN:, :])

        def rms_bwd(x, g, dn):
            rs = lax.rsqrt(jnp.mean(x * x, axis=-1, keepdims=True) + RMS_EPS)
            dng = dn * g
            dx = rs * dng - x * (rs * rs * rs) * jnp.mean(dng * x, axis=-1, keepdims=True)
            return dx, _rowsum(dn * x * rs)

        oo = o_ref[...]
        do, dga = rms_bwd(oo, ga_ref[...], dna)
        dga_ref[...] += dga
        do_ref[...] = do.astype(BF16)
        dl_ref[...] = _f32dot(do * oo, _head_sum_matrix())
        _, _, _, y = _conv_fwd(c_ref[...], h_ref[...], cp_ref[...], hp_ref[...], w_ref, i == 0, tm)
        dco, dgc = rms_bwd(b_ref[...] * y, gc_ref[...], dnc)
        dgc_ref[...] += dgc
        dco_ref[...] = dco

    row = lambda n, col=0: pl.BlockSpec((tm, n), lambda i: (i, col))
    prev = lambda col: pl.BlockSpec((HALO, D_CONV), lambda i: (jnp.maximum(i * hb - 1, 0), col))
    vec = lambda n: _resident((1, n))
    return pl.pallas_call(
        body, name="mixer_tail_bwd", grid=(T // tm,),
        in_specs=[row(D_MODEL), row(D_MODEL), vec(D_MODEL), _resident((D_MODEL, D_MODEL)), row(D_ATTN),
                  row(D_CONV, 0), row(D_CONV, 1), row(D_CONV, 2), prev(1), prev(2), _resident((3, D_CONV)),
                  vec(D_ATTN), vec(D_CONV)],
        out_specs=[row(D_MODEL), row(D_MODEL), row(D_ATTN), row(D_ATTN), row(D_CONV),
                   vec(D_MODEL), vec(D_MODEL), vec(D_ATTN), vec(D_CONV)],
        out_shape=[jax.ShapeDtypeStruct((T, D_MODEL), F32), jax.ShapeDtypeStruct((T, D_MODEL), BF16),
                   jax.ShapeDtypeStruct((T, D_ATTN), BF16), jax.ShapeDtypeStruct((T, D_ATTN), F32),
                   jax.ShapeDtypeStruct((T, D_CONV), F32), jax.ShapeDtypeStruct((1, D_MODEL), F32),
                   jax.ShapeDtypeStruct((1, D_MODEL), F32), jax.ShapeDtypeStruct((1, D_ATTN), F32),
                   jax.ShapeDtypeStruct((1, D_CONV), F32)],
        compiler_params=_cp(1),
    )(dx2, r2, lg, w_mo, o, bchf, bchf, bchf, bchf, bchf, conv_w, g_attn, g_conv)


def _conv_bwd(dco, bchf, conv_w):
    T = dco.shape[0]
    tm = _tile(T, 512)
    hb = tm // HALO
    nt = T // tm

    def body(dco_ref, dcon_ref, b_ref, bn_ref, c_ref, h_ref, cp_ref, hp_ref, w_ref, dbch_ref, dw_ref):
        i = pl.program_id(0)

        @pl.when(i == 0)
        def _():
            dw_ref[...] = jnp.zeros_like(dw_ref)

        cc = c_ref[...]
        hh = h_ref[...]
        u, u1, u2, y = _conv_fwd(cc, hh, cp_ref[...], hp_ref[...], w_ref, i == 0, tm)
        dco = dco_ref[...]
        bb = b_ref[...]
        dyc = dco * bb
        dy_next = jnp.where(i == nt - 1, 0.0, dcon_ref[...] * bn_ref[...])
        d1 = _shift_rows(dyc, dy_next, 1, tm, False)
        d2 = _shift_rows(dyc, dy_next, 2, tm, False)
        du = w_ref[2:3, :] * dyc + w_ref[1:2, :] * d1 + w_ref[0:1, :] * d2
        dbch_ref[:, 0:D_CONV] = (dco * y).astype(BF16)
        dbch_ref[:, D_CONV:2 * D_CONV] = (du * hh).astype(BF16)
        dbch_ref[:, 2 * D_CONV:] = (du * cc).astype(BF16)
        dw_ref[0:1, :] += _rowsum(dyc * u2)
        dw_ref[1:2, :] += _rowsum(dyc * u1)
        dw_ref[2:3, :] += _rowsum(dyc * u)

    row = lambda n, col=0: pl.BlockSpec((tm, n), lambda i: (i, col))
    prev = lambda col: pl.BlockSpec((HALO, D_CONV), lambda i: (jnp.maximum(i * hb - 1, 0), col))
    nxt = lambda col: pl.BlockSpec((HALO, D_CONV), lambda i: (jnp.minimum((i + 1) * hb, T // HALO - 1), col))
    return pl.pallas_call(
        body, name="conv_bwd", grid=(nt,),
        in_specs=[row(D_CONV), nxt(0), row(D_CONV, 0), nxt(0), row(D_CONV, 1), row(D_CONV, 2), prev(1), prev(2),
                  _resident((3, D_CONV))],
        out_specs=[row(3 * D_CONV), _resident((8, D_CONV))],
        out_shape=[jax.ShapeDtypeStruct((T, 3 * D_CONV), BF16), jax.ShapeDtypeStruct((8, D_CONV), F32)],
        compiler_params=_cp(1),
    )(dco, dco, bchf, bchf, bchf, bchf, bchf, bchf, conv_w)


def _mixer_in_bwd(dx1a, dqkv, dbch, dfl, w_qkv, w_bch, w_f):
    T = dx1a.shape[0]
    tm = _tile(T, 512)

    def body(a_ref, dq_ref, db_ref, df_ref, wq_ref, wb_ref, wf_ref, o_ref):
        o_ref[...] = (a_ref[...] + _nt(dq_ref[...], wq_ref[...]) + _nt(db_ref[...], wb_ref[...])
                      + _nt(df_ref[...], wf_ref[...]))

    row = lambda n: pl.BlockSpec((tm, n), lambda i: (i, 0))
    return pl.pallas_call(
        body, name="mixer_in_bwd", grid=(T // tm,),
        in_specs=[row(D_MODEL), row(3 * D_ATTN), row(3 * D_CONV), row(N_FLOG),
                  _resident((D_MODEL, 3 * D_ATTN)), _resident((D_MODEL, 3 * D_CONV)), _resident((D_MODEL, N_FLOG))],
        out_specs=row(D_MODEL),
        out_shape=jax.ShapeDtypeStruct((T, D_MODEL), F32),
        compiler_params=_cp(1),
    )(dx1a, dqkv, dbch, dfl, w_qkv, w_bch, w_f)


def _ple_loss(x3, p, tgt, w_g, w_p, b_g, lg, lb):
    T = x3.shape[0]
    tm = _tile(T, 512)

    def body(x_ref, p_ref, t_ref, wg_ref, wp_ref, bg_ref, lg_ref, lb_ref,
             dx_ref, de_ref, dz_ref, loss_ref, dlg_ref, dlb_ref, dbg_ref):
        @pl.when(pl.program_id(0) == 0)
        def _():
            for ref in (loss_ref, dlg_ref, dlb_ref, dbg_ref):
                ref[...] = jnp.zeros_like(ref)

        xf = x_ref[...]
        gate = _sigmoid(_nn(xf.astype(BF16), wg_ref[...]) + bg_ref[...])
        e = _nn(p_ref[...].astype(BF16), wp_ref[...])
        xhat, rstd = _ln_stats(ALPHA * xf + gate * e)
        err = xhat * lg_ref[...] + lb_ref[...] - t_ref[...]
        sq = jnp.sum(_rowsum(err * err), axis=-1, keepdims=True)
        loss_ref[...] += jnp.broadcast_to(sq * (0.5 / D_MODEL), loss_ref.shape)
        dy = err * (1.0 / D_MODEL)
        dr = _ln_bwd(dy, xhat, rstd, lg_ref[...])
        dlg_ref[...] += _rowsum(dy * xhat)
        dlb_ref[...] += _rowsum(dy)
        de_ref[...] = (dr * gate).astype(BF16)
        dz = dr * e * gate * (1.0 - gate)
        dbg_ref[...] += _rowsum(dz)
        dzb = dz.astype(BF16)
        dz_ref[...] = dzb
        dx_ref[...] = ALPHA * dr + _nt(dzb, wg_ref[...])

    row = lambda n: pl.BlockSpec((tm, n), lambda i: (i, 0))
    vec = lambda n: _resident((1, n))
    return pl.pallas_call(
        body, name="ple_loss", grid=(T // tm,),
        in_specs=[row(D_MODEL), row(PLE_DIM), row(D_MODEL), _resident((D_MODEL, D_MODEL)),
                  _resident((PLE_DIM, D_MODEL)), vec(D_MODEL), vec(D_MODEL), vec(D_MODEL)],
        out_specs=[row(D_MODEL), row(D_MODEL), row(D_MODEL), vec(LANE), vec(D_MODEL), vec(D_MODEL), vec(D_MODEL)],
        out_shape=[jax.ShapeDtypeStruct((T, D_MODEL), F32), jax.ShapeDtypeStruct((T, D_MODEL), BF16),
                   jax.ShapeDtypeStruct((T, D_MODEL), BF16), jax.ShapeDtypeStruct((1, LANE), F32),
                   jax.ShapeDtypeStruct((1, D_MODEL), F32), jax.ShapeDtypeStruct((1, D_MODEL), F32),
                   jax.ShapeDtypeStruct((1, D_MODEL), F32)],
        compiler_params=_cp(1),
    )(x3, p, tgt, w_g, w_p, b_g, lg, lb)


def _lane_layout(v8):
    return jnp.repeat(v8, HEAD_DIM, axis=1)


def _row_layout(v8):
    t = v8.shape[0]
    return jnp.pad(v8.T.reshape(N_HEADS // 2, 2, t), ((0, 0), (0, 6), (0, 0)))


def _from_lane_layout(vl):
    return vl[:, ::HEAD_DIM]


def _from_row_layout(vr):
    return vr[:, :2, :].reshape(N_HEADS, -1).T


def _local_step(x, p, tgt, w):
    bf = lambda a: a.astype(BF16)
    w1i, w1o, w2i, w2o = bf(w["ffn1_w_in"]), bf(w["ffn1_w_out"]), bf(w["ffn2_w_in"]), bf(w["ffn2_w_out"])
    wmi = w["w_mix_in"]
    o_f = 3 * D_ATTN
    o_b = o_f + N_HEADS
    w_qkv = bf(wmi[:, :o_f])
    w_f = bf(jnp.pad(wmi[:, o_f:o_b], ((0, 0), (0, N_FLOG - N_HEADS))))
    w_bch = bf(wmi[:, o_b:])
    w_bchf = jnp.concatenate([w_bch, w_f], axis=1)
    w_mo, w_g, w_p = bf(w["w_mix_out"]), bf(w["w_ple_gate"]), bf(w["w_ple"])
    b_f = jnp.pad(w["b_forget"], ((0, 0), (0, N_FLOG - N_HEADS)))

    x1, r1, g1, u1, h1 = _ffn_fwd(x, w1i, w1o, w["ln1_g"], w["ln1_b"], "ffn1_fwd")
    q_scale = jnp.concatenate([jnp.full((1, D_ATTN), Q_SCALE, F32), jnp.ones((1, 2 * D_ATTN), F32)], axis=1)
    qkv = _matmul_nn(x1, w_qkv, q_scale, BF16, "proj_qkv")
    bchf = _matmul_nn(x1, w_bchf, jnp.ones((1, 3 * D_CONV + N_FLOG), F32), F32, "proj_bchf")
    fcol = 3 * D_CONV // N_FLOG
    c = _forget_cumsum(bchf, fcol, b_f)
    c8 = c[:, :N_HEADS]
    cb = _lane_layout(c8)
    o, alrow = _attn_fwd(qkv, qkv[:, 2 * D_ATTN:].T, cb, _row_layout(c8))
    x2, r2, merged = _mixer_tail_fwd(o, bchf, w["conv_w"], w["g_attn"], w["g_conv"], w_mo, x1, w["ln2_g"], w["ln2_b"])
    x3, r3, g2, u2, h2 = _ffn_fwd(x2, w2i, w2o, w["ln3_g"], w["ln3_b"], "ffn2_fwd")

    grads = {}
    dx3, de, dz, loss, grads["ln4_g"], grads["ln4_b"], grads["b_ple_gate"] = _ple_loss(
        x3, p, tgt, w_g, w_p, w["b_ple_gate"], w["ln4_g"], w["ln4_b"])
    grads["w_ple"] = _matmul_tn(p, de, "dw_ple")
    grads["w_ple_gate"] = _matmul_tn(x3, dz, "dw_ple_gate")

    dx2, dgu2, df2, grads["ln3_g"], grads["ln3_b"] = _ffn_bwd(dx3, r3, g2, u2, w2i, w2o, w["ln3_g"], "ffn2_bwd")
    grads["ffn2_w_in"] = _matmul_tn(x2, dgu2, "dw_ffn2_in")
    grads["ffn2_w_out"] = _matmul_tn(h2, df2, "dw_ffn2_out")

    (dx1a, dr2, dob, delta, dco, grads["ln2_g"], grads["ln2_b"], grads["g_attn"], grads["g_conv"]) = _mixer_tail_bwd(
        dx2, r2, w["ln2_g"], w_mo, o, bchf, w["conv_w"], w["g_attn"], w["g_conv"])
    grads["w_mix_out"] = _matmul_tn(merged, dr2, "dw_mix_out")
    dbch, dcw = _conv_bwd(dco, bchf, w["conv_w"])
    grads["conv_w"] = dcw[:3]
    dq, dk, dv, dck, dcq = _attn_bwd(qkv, dob, cb, alrow, _row_layout(_from_lane_layout(delta)))
    pad_heads = lambda v8: jnp.pad(v8, ((0, 0), (0, N_FLOG - N_HEADS)))
    dfl, dbf = _forget_bwd(pad_heads(_from_lane_layout(dck)), pad_heads(_from_row_layout(dcq)), bchf, fcol, b_f)
    grads["b_forget"] = dbf[:, :N_HEADS]
    dqkv = jnp.concatenate([bf(dq), bf(dk), bf(dv)], axis=1)
    dx1 = _mixer_in_bwd(dx1a, dqkv, dbch, dfl, w_qkv, w_bch, w_f)
    grads["w_mix_in"] = jnp.concatenate(
        [_matmul_tn(x1, dqkv, "dw_qkv"), _matmul_tn(x1, dfl, "dw_flog")[:, :N_HEADS], _matmul_tn(x1, dbch, "dw_bch")],
        axis=1)

    dx0, dgu1, df1, grads["ln1_g"], grads["ln1_b"] = _ffn_bwd(dx1, r1, g1, u1, w1i, w1o, w["ln1_g"], "ffn1_bwd")
    grads["ffn1_w_in"] = _matmul_tn(x, dgu1, "dw_ffn1_in")
    grads["ffn1_w_out"] = _matmul_tn(h1, df1, "dw_ffn1_out")
    return loss, dx0, grads


WEIGHTS = ["ffn1_w_in", "ffn1_w_out", "ln1_g", "ln1_b", "w_mix_in", "b_forget", "conv_w", "g_attn", "g_conv",
           "w_mix_out", "ln2_g", "ln2_b", "ffn2_w_in", "ffn2_w_out", "ln3_g", "ln3_b", "w_ple", "w_ple_gate",
           "b_ple_gate", "ln4_g", "ln4_b"]
LAYOUT = {
    "ffn1_w_in": ((D_MODEL, 2 * D_FF), 1), "ffn1_w_out": ((D_FF, D_MODEL), 0),
    "w_mix_in": ((D_MODEL, 3 * D_ATTN + N_HEADS + 3 * D_CONV), 1), "conv_w": ((3, D_CONV), 1),
    "w_mix_out": ((D_MODEL, D_MODEL), 0), "ffn2_w_in": ((D_MODEL, 2 * D_FF), 1), "ffn2_w_out": ((D_FF, D_MODEL), 0),
    "w_ple": ((PLE_DIM, D_MODEL), 1), "w_ple_gate": ((D_MODEL, D_MODEL), 0),
    "ln1_g": ((1, D_MODEL), None), "ln1_b": ((1, D_MODEL), None), "b_forget": ((1, N_HEADS), None),
    "g_attn": ((1, D_ATTN), None), "g_conv": ((1, D_CONV), None), "ln2_g": ((1, D_MODEL), None),
    "ln2_b": ((1, D_MODEL), None), "ln3_g": ((1, D_MODEL), None), "ln3_b": ((1, D_MODEL), None),
    "b_ple_gate": ((1, D_MODEL), None), "ln4_g": ((1, D_MODEL), None), "ln4_b": ((1, D_MODEL), None),
}
ROW = 1024
ROW_ALIGN = 64


def _shard_shape(name):
    shape, axis = LAYOUT[name]
    if axis is None:
        return shape
    return tuple(s // N_CHIPS if a == axis else s for a, s in enumerate(shape))


def _piece_len(name):
    n = math.prod(_shard_shape(name))
    return -(-n // ROW) * ROW


PAYLOAD_LEN = sum(_piece_len(n) for n in WEIGHTS)
PAYLOAD_ROWS = -(-PAYLOAD_LEN // (ROW * ROW_ALIGN)) * ROW_ALIGN


def _pack(pieces):
    lead = pieces[WEIGHTS[0]].shape[:-len(_shard_shape(WEIGHTS[0]))]
    flat = []
    for n in WEIGHTS:
        a = pieces[n].reshape(lead + (-1,))
        flat.append(jnp.pad(a, [(0, 0)] * len(lead) + [(0, _piece_len(n) - a.shape[-1])]))
    tail = PAYLOAD_ROWS * ROW - PAYLOAD_LEN
    if tail:
        flat.append(jnp.zeros(lead + (tail,), flat[0].dtype))
    return jnp.concatenate(flat, axis=-1).reshape(lead + (PAYLOAD_ROWS, ROW))


def _unpack(payload):
    lead = payload.shape[:-2]
    flat = payload.reshape(lead + (-1,))
    out, off = {}, 0
    for n in WEIGHTS:
        ss = _shard_shape(n)
        out[n] = flat[..., off:off + math.prod(ss)].reshape(lead + ss)
        off += _piece_len(n)
    return out


def _split_chips(name, full):
    shape, axis = LAYOUT[name]
    if axis is None:
        return jnp.broadcast_to(full[None], (N_CHIPS,) + shape)
    if axis == 0:
        return full.reshape((N_CHIPS, shape[0] // N_CHIPS) + shape[1:])
    return jnp.moveaxis(full.reshape(shape[:1] + (N_CHIPS, shape[1] // N_CHIPS)), 1, 0)


def _join_chips(name, parts):
    shape, axis = LAYOUT[name]
    if axis is None:
        return parts[0]
    if axis == 0:
        return parts.reshape(shape)
    return jnp.moveaxis(parts, 0, 1).reshape(shape)


def _place():
    x, y, c = lax.axis_index("x"), lax.axis_index("y"), lax.axis_index("c")
    others = [(1 - x, y), (x, 1 - y), (1 - x, 1 - y)]
    return x, y, c, others


def _all_gather_weights(wp):
    half = wp.shape[1:]

    def body(w_ref, out_ref, send_sems, recv_sems):
        x, y, c, others = _place()
        s = 2 * x + y

        def copy(k, chip, half_idx, to):
            slot = out_ref.at[2 * chip[0] + chip[1], half_idx]
            return pltpu.make_async_remote_copy(src_ref=slot, dst_ref=slot, send_sem=send_sems.at[k],
                                                recv_sem=recv_sems.at[k], device_id=to, device_id_type=MESH)

        first = []
        for j, chip in enumerate(others):
            slot = out_ref.at[s, c]
            first.append(pltpu.make_async_remote_copy(src_ref=w_ref.at[c], dst_ref=slot, send_sem=send_sems.at[j],
                                                      recv_sem=recv_sems.at[j], device_id=(*chip, c),
                                                      device_id_type=MESH))
        for cp in first:
            cp.start()
        passed = [copy(3 + j, chip, c, (x, y, 1 - c)) for j, chip in enumerate(others)]
        for j, chip in enumerate(others):
            copy(j, chip, c, (x, y, c)).wait_recv()
            passed[j].start()
        for j, chip in enumerate(others):
            copy(3 + j, chip, 1 - c, (x, y, c)).wait_recv()
        for cp in first + passed:
            cp.wait_send()

    any_spec = pl.BlockSpec(memory_space=pl.ANY)
    return pl.pallas_call(
        body, name="all_gather_weights",
        out_shape=jax.ShapeDtypeStruct((N_CHIPS, 2) + half, wp.dtype),
        in_specs=[any_spec], out_specs=any_spec,
        scratch_shapes=[pltpu.SemaphoreType.DMA((6,)), pltpu.SemaphoreType.DMA((6,))],
    )(wp)


def _swap_halves(g):
    rh = g.shape[2]

    def body(g_ref, out_ref, send_sem, recv_sem):
        x, y, c, _ = _place()
        cp = pltpu.make_async_remote_copy(src_ref=g_ref.at[:, 1 - c], dst_ref=out_ref, send_sem=send_sem,
                                          recv_sem=recv_sem, device_id=(x, y, 1 - c), device_id_type=MESH)
        cp.start()
        cp.wait()

    any_spec = pl.BlockSpec(memory_space=pl.ANY)
    return pl.pallas_call(
        body, name="grad_swap_halves",
        out_shape=jax.ShapeDtypeStruct((N_CHIPS, rh, ROW), g.dtype),
        in_specs=[any_spec], out_specs=any_spec,
        scratch_shapes=[pltpu.SemaphoreType.DMA, pltpu.SemaphoreType.DMA],
    )(g)


def _exchange_chips(pp):
    def body(p_ref, out_ref, send_sems, recv_sems):
        x, y, c, others = _place()
        s = 2 * x + y
        sends = []
        for j, chip in enumerate(others):
            sends.append(pltpu.make_async_remote_copy(
                src_ref=p_ref.at[2 * chip[0] + chip[1]], dst_ref=out_ref.at[s], send_sem=send_sems.at[j],
                recv_sem=recv_sems.at[j], device_id=(*chip, c), device_id_type=MESH))
        for cp in sends:
            cp.start()
        for j, chip in enumerate(others):
            slot = out_ref.at[2 * chip[0] + chip[1]]
            pltpu.make_async_remote_copy(src_ref=slot, dst_ref=slot, send_sem=send_sems.at[j],
                                         recv_sem=recv_sems.at[j], device_id=(x, y, c),
                                         device_id_type=MESH).wait_recv()
        for cp in sends:
            cp.wait_send()

    any_spec = pl.BlockSpec(memory_space=pl.ANY)
    return pl.pallas_call(
        body, name="grad_exchange_chips",
        out_shape=jax.ShapeDtypeStruct(pp.shape, pp.dtype),
        in_specs=[any_spec], out_specs=any_spec,
        scratch_shapes=[pltpu.SemaphoreType.DMA((3,)), pltpu.SemaphoreType.DMA((3,))],
    )(pp)


def _share_half(r):
    def body(r_ref, out_ref, send_sem, recv_sem):
        x, y, c, _ = _place()
        cp = pltpu.make_async_remote_copy(src_ref=r_ref, dst_ref=out_ref, send_sem=send_sem, recv_sem=recv_sem,
                                          device_id=(x, y, 1 - c), device_id_type=MESH)
        cp.start()
        cp.wait()

    any_spec = pl.BlockSpec(memory_space=pl.ANY)
    return pl.pallas_call(
        body, name="grad_share_half",
        out_shape=jax.ShapeDtypeStruct(r.shape, r.dtype),
        in_specs=[any_spec], out_specs=any_spec,
        scratch_shapes=[pltpu.SemaphoreType.DMA, pltpu.SemaphoreType.DMA],
    )(r)


def _add_my_half(g, sib, c_idx):
    rh = g.shape[2]
    tr = _tile(rh, 512)

    def body(c_ref, g_ref, s_ref, o_ref):
        o_ref[...] = (g_ref[...] + s_ref[...]).astype(BF16)

    return pl.pallas_call(
        body, name="grad_add_halves",
        grid_spec=pltpu.PrefetchScalarGridSpec(
            num_scalar_prefetch=1, grid=(N_CHIPS, rh // tr),
            in_specs=[pl.BlockSpec((None, None, tr, ROW), lambda s, i, c: (s, c[0], i, 0)),
                      pl.BlockSpec((None, tr, ROW), lambda s, i, c: (s, i, 0))],
            out_specs=pl.BlockSpec((None, tr, ROW), lambda s, i, c: (s, i, 0))),
        out_shape=jax.ShapeDtypeStruct((N_CHIPS, rh, ROW), BF16),
        compiler_params=_cp(2),
    )(c_idx, g, sib)


def _sum_chips(parts, pp, s_idx):
    rh = parts.shape[1]
    tr = _tile(rh, 512)

    def body(s_ref, p0, p1, p2, p3, mine_ref, o_ref):
        own = mine_ref[...]
        t = [jnp.where(s_ref[0] == k, own, p[...]).astype(F32) for k, p in enumerate((p0, p1, p2, p3))]
        o_ref[...] = ((t[0] + t[1]) + t[2]) + t[3]

    slot = lambda k: pl.BlockSpec((None, tr, ROW), lambda i, s: (jnp.where(s[0] == k, (k + 1) % N_CHIPS, k), i, 0))
    return pl.pallas_call(
        body, name="grad_sum_chips",
        grid_spec=pltpu.PrefetchScalarGridSpec(
            num_scalar_prefetch=1, grid=(rh // tr,),
            in_specs=[slot(0), slot(1), slot(2), slot(3), pl.BlockSpec((None, tr, ROW), lambda i, s: (s[0], i, 0))],
            out_specs=pl.BlockSpec((tr, ROW), lambda i, s: (i, 0))),
        out_shape=jax.ShapeDtypeStruct((rh, ROW), F32),
        compiler_params=_cp(1),
    )(s_idx, parts, parts, parts, parts, pp)


def _adamw(w, g_mine, g_sib, m, v, c_idx):
    rows = w.shape[0]
    tr = _tile(rows // 2, 512)
    nbh = rows // 2 // tr
    c1 = 1.0 - ADAM_B1 ** ADAM_STEP
    c2 = 1.0 - ADAM_B2 ** ADAM_STEP

    def body(c_ref, w_ref, gm_ref, gs_ref, m_ref, v_ref, g_ref, d_ref, mo_ref, vo_ref):
        g = jnp.where(pl.program_id(0) // nbh == c_ref[0], gm_ref[...], gs_ref[...])
        g_ref[...] = g
        m = ADAM_B1 * m_ref[...] + (1.0 - ADAM_B1) * g
        v = ADAM_B2 * v_ref[...] + (1.0 - ADAM_B2) * (g * g)
        mo_ref[...] = m
        vo_ref[...] = v
        d_ref[...] = -ADAM_LR * ((m / c1) / (jnp.sqrt(v / c2) + ADAM_EPS) + ADAM_WD * w_ref[...])

    spec = pl.BlockSpec((tr, ROW), lambda i, c: (i, 0))
    half = pl.BlockSpec((tr, ROW), lambda i, c: (i % nbh, 0))
    return pl.pallas_call(
        body, name="adamw",
        grid_spec=pltpu.PrefetchScalarGridSpec(
            num_scalar_prefetch=1, grid=(rows // tr,),
            in_specs=[spec, half, half, spec, spec], out_specs=[spec] * 4),
        out_shape=[jax.ShapeDtypeStruct(w.shape, F32)] * 4,
        compiler_params=_cp(1),
    )(c_idx, w, g_mine, g_sib, m, v)


def kernel(x, p, ffn1_w_in, ffn1_w_out, ln1_g, ln1_b, w_mix_in, b_forget, conv_w, g_attn, g_conv, w_mix_out, ln2_g, ln2_b, ffn2_w_in, ffn2_w_out, ln3_g, ln3_b, w_ple, w_ple_gate, b_ple_gate, ln4_g, ln4_b, loss_target, m_ffn1_w_in, m_ffn1_w_out, m_ln1_g, m_ln1_b, m_w_mix_in, m_b_forget, m_conv_w, m_g_attn, m_g_conv, m_w_mix_out, m_ln2_g, m_ln2_b, m_ffn2_w_in, m_ffn2_w_out, m_ln3_g, m_ln3_b, m_w_ple, m_w_ple_gate, m_b_ple_gate, m_ln4_g, m_ln4_b, v_ffn1_w_in, v_ffn1_w_out, v_ln1_g, v_ln1_b, v_w_mix_in, v_b_forget, v_conv_w, v_g_attn, v_g_conv, v_w_mix_out, v_ln2_g, v_ln2_b, v_ffn2_w_in, v_ffn2_w_out, v_ln3_g, v_ln3_b, v_w_ple, v_w_ple_gate, v_b_ple_gate, v_ln4_g, v_ln4_b):
    args = dict(locals())
    shard = {n: args[n][0] if LAYOUT[n][1] is not None else args[n] for n in WEIGHTS}
    m_shard = {n: args["m_" + n][0] if LAYOUT[n][1] is not None else args["m_" + n] for n in WEIGHTS}
    v_shard = {n: args["v_" + n][0] if LAYOUT[n][1] is not None else args["v_" + n] for n in WEIGHTS}
    rh = PAYLOAD_ROWS // 2

    c_idx = lax.axis_index("c").astype(jnp.int32).reshape(1)
    chip = (2 * lax.axis_index("x") + lax.axis_index("y")).astype(jnp.int32)
    w_pay = _pack(shard)
    w_bf = w_pay.astype(BF16).reshape(2, rh, ROW)
    gathered = lax.dynamic_update_slice(_all_gather_weights(w_bf), w_bf[None], (chip, 0, 0, 0))
    parts = _unpack(gathered.reshape(N_CHIPS, PAYLOAD_ROWS, ROW))
    full = {n: _join_chips(n, parts[n]) for n in WEIGHTS}
    full.update({n: shard[n] for n in WEIGHTS if LAYOUT[n][1] is None})

    loss_acc, grad_x, grads = _local_step(x[0], p[0, 0], loss_target[0], full)
    loss = lax.psum(loss_acc[0, 0], ("x", "y", "c"))

    g_pay = _pack({n: _split_chips(n, grads[n]) for n in WEIGHTS}).reshape(N_CHIPS, 2, rh, ROW)
    chip_sums = _add_my_half(g_pay, _swap_halves(g_pay), c_idx)
    my_half = _sum_chips(_exchange_chips(chip_sums), chip_sums, chip.reshape(1))
    reduced, delta, new_m, new_v = _adamw(w_pay, my_half, _share_half(my_half), _pack(m_shard), _pack(v_shard), c_idx)

    def outs(payload):
        u = _unpack(payload)
        return [u[n][None] if LAYOUT[n][1] is not None else u[n] for n in WEIGHTS]

    return (loss, grad_x[None], *outs(reduced), *outs(delta), *outs(new_m), *outs(new_v))
```

```python
import functools
import math

import jax
import jax.numpy as jnp
from jax import lax
from jax.experimental import pallas as pl
from jax.experimental.pallas import tpu as pltpu

F32 = jnp.float32
BF16 = jnp.bfloat16

D_MODEL = 1024
D_FF = 2816
N_HEADS = 8
HEAD_DIM = 64
D_ATTN = N_HEADS * HEAD_DIM
D_CONV = 512
PLE_DIM = 256
N_FLOG = 128
ALPHA = 2.0 ** 0.25
LN_EPS = 1e-5
RMS_EPS = 1e-6
NEG_INF = -1e30
Q_SCALE = 1.0 / math.sqrt(HEAD_DIM)
LOG2E = math.log2(math.e)

ADAM_LR = 0.001
ADAM_B1 = 0.9
ADAM_B2 = 0.999
ADAM_EPS = 1e-08
ADAM_WD = 0.01
ADAM_STEP = 10

V7X_VMEM_BYTES = 64 << 20
VMEM_LIMIT = V7X_VMEM_BYTES - (8 << 20)
LANE = 128
FF_CHUNK = 256
N_CHIPS = 4
MESH = pl.DeviceIdType.MESH


def _cp(n_axes):
    return pltpu.CompilerParams(dimension_semantics=("arbitrary",) * n_axes, vmem_limit_bytes=VMEM_LIMIT)


def _resident(shape):
    n = len(shape)
    return pl.BlockSpec(shape, lambda *_: (0,) * n, pipeline_mode=pl.Buffered(1))


def _nn(a, b):
    return jnp.dot(a, b, preferred_element_type=F32)


def _nt(a, b):
    return lax.dot_general(a, b, (((1,), (1,)), ((), ())), preferred_element_type=F32)


def _tn(a, b):
    return lax.dot_general(a, b, (((0,), (0,)), ((), ())), preferred_element_type=F32)


def _ln_stats(r):
    mu = jnp.mean(r, axis=-1, keepdims=True)
    xc = r - mu
    var = jnp.mean(xc * xc, axis=-1, keepdims=True)
    rstd = lax.rsqrt(var + LN_EPS)
    return xc * rstd, rstd


def _ln_bwd(dy, xhat, rstd, g):
    dxh = dy * g
    m1 = jnp.mean(dxh, axis=-1, keepdims=True)
    m2 = jnp.mean(dxh * xhat, axis=-1, keepdims=True)
    return rstd * (dxh - m1 - xhat * m2)


def _sigmoid(z):
    return 1.0 / (1.0 + jnp.exp(-z))


def _rowsum(a):
    return jnp.sum(a, axis=0, keepdims=True)


def _tile(total, want):
    if total <= want:
        return total
    for t in range(want - want % 8, 0, -8):
        if total % t == 0:
            return t
    raise ValueError((total, want))


def _ffn_fwd(x, w_in, w_out, lg, lb, name):
    T = x.shape[0]
    tm = _tile(T, 512)
    nf = D_FF // FF_CHUNK

    def body(x_ref, wi_ref, wo_ref, lg_ref, lb_ref, xo_ref, r_ref, g_ref, u_ref, h_ref):
        xf = x_ref[...]
        xb = xf.astype(BF16)
        acc = jnp.zeros((tm, D_MODEL), F32)
        for j in range(nf):
            c0 = j * FF_CHUNK
            g = _nn(xb, wi_ref[:, c0:c0 + FF_CHUNK])
            u = _nn(xb, wi_ref[:, D_FF + c0:D_FF + c0 + FF_CHUNK])
            hb = (g * _sigmoid(g) * u).astype(BF16)
            g_ref[:, c0:c0 + FF_CHUNK] = g.astype(BF16)
            u_ref[:, c0:c0 + FF_CHUNK] = u.astype(BF16)
            h_ref[:, c0:c0 + FF_CHUNK] = hb
            acc = acc + _nn(hb, wo_ref[c0:c0 + FF_CHUNK, :])
        r = ALPHA * xf + 0.5 * acc
        r_ref[...] = r
        xhat, _ = _ln_stats(r)
        xo_ref[...] = xhat * lg_ref[...] + lb_ref[...]

    row = lambda n: pl.BlockSpec((tm, n), lambda i: (i, 0))
    return pl.pallas_call(
        body, name=name, grid=(T // tm,),
        in_specs=[row(D_MODEL), _resident((D_MODEL, 2 * D_FF)), _resident((D_FF, D_MODEL)),
                  _resident((1, D_MODEL)), _resident((1, D_MODEL))],
        out_specs=[row(D_MODEL), row(D_MODEL), row(D_FF), row(D_FF), row(D_FF)],
        out_shape=[jax.ShapeDtypeStruct((T, D_MODEL), F32), jax.ShapeDtypeStruct((T, D_MODEL), F32),
                   jax.ShapeDtypeStruct((T, D_FF), BF16), jax.ShapeDtypeStruct((T, D_FF), BF16),
                   jax.ShapeDtypeStruct((T, D_FF), BF16)],
        compiler_params=_cp(1),
    )(x, w_in, w_out, lg, lb)


def _ffn_bwd(dxo, r, g, u, w_in, w_out, lg, name):
    T = r.shape[0]
    tm = _tile(T, 256)
    nf = D_FF // FF_CHUNK

    def body(dxo_ref, r_ref, g_ref, u_ref, wi_ref, wo_ref, lg_ref, dx_ref, dgu_ref, df_ref, dlg_ref, dlb_ref):
        i = pl.program_id(0)
        dy = dxo_ref[...]
        xhat, rstd = _ln_stats(r_ref[...])
        dr = _ln_bwd(dy, xhat, rstd, lg_ref[...])

        @pl.when(i == 0)
        def _():
            dlg_ref[...] = jnp.zeros_like(dlg_ref)
            dlb_ref[...] = jnp.zeros_like(dlb_ref)

        dlg_ref[...] += _rowsum(dy * xhat)
        dlb_ref[...] += _rowsum(dy)
        dfb = (0.5 * dr).astype(BF16)
        df_ref[...] = dfb
        acc = jnp.zeros((tm, D_MODEL), F32)
        for j in range(nf):
            c0 = j * FF_CHUNK
            dh = _nt(dfb, wo_ref[c0:c0 + FF_CHUNK, :])
            gg = g_ref[:, c0:c0 + FF_CHUNK].astype(F32)
            uu = u_ref[:, c0:c0 + FF_CHUNK].astype(F32)
            s = _sigmoid(gg)
            dgb = (dh * uu * s * (1.0 + gg * (1.0 - s))).astype(BF16)
            dub = (dh * gg * s).astype(BF16)
            dgu_ref[:, c0:c0 + FF_CHUNK] = dgb
            dgu_ref[:, D_FF + c0:D_FF + c0 + FF_CHUNK] = dub
            acc = acc + _nt(dgb, wi_ref[:, c0:c0 + FF_CHUNK]) + _nt(dub, wi_ref[:, D_FF + c0:D_FF + c0 + FF_CHUNK])
        dx_ref[...] = ALPHA * dr + acc

    row = lambda n: pl.BlockSpec((tm, n), lambda i: (i, 0))
    return pl.pallas_call(
        body, name=name, grid=(T // tm,),
        in_specs=[row(D_MODEL), row(D_MODEL), row(D_FF), row(D_FF), _resident((D_MODEL, 2 * D_FF)),
                  _resident((D_FF, D_MODEL)), _resident((1, D_MODEL))],
        out_specs=[row(D_MODEL), row(2 * D_FF), row(D_MODEL), _resident((1, D_MODEL)), _resident((1, D_MODEL))],
        out_shape=[jax.ShapeDtypeStruct((T, D_MODEL), F32), jax.ShapeDtypeStruct((T, 2 * D_FF), BF16),
                   jax.ShapeDtypeStruct((T, D_MODEL), BF16), jax.ShapeDtypeStruct((1, D_MODEL), F32),
                   jax.ShapeDtypeStruct((1, D_MODEL), F32)],
        compiler_params=_cp(1),
    )(dxo, r, g, u, w_in, w_out, lg)


def _matmul_tn(a, b, name, tn=None):
    T, K = a.shape
    N = b.shape[1]
    tt = _tile(T, 512)
    if tn is None:
        tn = N
        while K * tn * 4 > (6 << 20) and tn % 256 == 0:
            tn //= 2
    assert N % tn == 0

    def body(a_ref, b_ref, o_ref):
        @pl.when(pl.program_id(1) == 0)
        def _():
            o_ref[...] = jnp.zeros_like(o_ref)

        o_ref[...] += _tn(a_ref[...].astype(BF16), b_ref[...].astype(BF16))

    return pl.pallas_call(
        body, name=name, grid=(N // tn, T // tt),
        in_specs=[pl.BlockSpec((tt, K), lambda n, t: (t, 0)), pl.BlockSpec((tt, tn), lambda n, t: (t, n))],
        out_specs=pl.BlockSpec((K, tn), lambda n, t: (0, n)),
        out_shape=jax.ShapeDtypeStruct((K, N), F32),
        compiler_params=_cp(2),
    )(a, b)


def _matmul_nn(x, w, scale, out_dtype, name):
    T, K = x.shape
    N = w.shape[1]
    tm = _tile(T, 512)

    def body(x_ref, w_ref, s_ref, o_ref):
        o_ref[...] = (_nn(x_ref[...].astype(BF16), w_ref[...]) * s_ref[...]).astype(out_dtype)

    return pl.pallas_call(
        body, name=name, grid=(T // tm,),
        in_specs=[pl.BlockSpec((tm, K), lambda i: (i, 0)), _resident((K, N)), _resident((1, N))],
        out_specs=pl.BlockSpec((tm, N), lambda i: (i, 0)),
        out_shape=jax.ShapeDtypeStruct((T, N), out_dtype),
        compiler_params=_cp(1),
    )(x, w, scale)


def _log_sigmoid(z):
    return jnp.minimum(z, 0.0) - jnp.log1p(jnp.exp(-jnp.abs(z)))


def _tri(n, lower):
    r = lax.broadcasted_iota(jnp.int32, (n, n), 0)
    c = lax.broadcasted_iota(jnp.int32, (n, n), 1)
    return jnp.where((c <= r) if lower else (c >= r), 1.0, 0.0).astype(F32)


def _f32dot(a, b):
    return jnp.dot(a, b, preferred_element_type=F32, precision=lax.Precision.HIGHEST)


def _forget_cumsum(flog, col, bf):
    T = flog.shape[0]
    bt = _tile(T, 512)

    def body(f_ref, b_ref, c_ref, carry):
        @pl.when(pl.program_id(0) == 0)
        def _():
            carry[...] = jnp.zeros_like(carry)

        lf = _log_sigmoid(f_ref[...] + b_ref[...])
        c = _f32dot(_tri(bt, True), lf) + carry[...]
        c_ref[...] = c * LOG2E
        carry[...] = c[bt - 1:bt, :]

    return pl.pallas_call(
        body, name="forget_cumsum", grid=(T // bt,),
        in_specs=[pl.BlockSpec((bt, N_FLOG), lambda i: (i, col)), _resident((1, N_FLOG))],
        out_specs=pl.BlockSpec((bt, N_FLOG), lambda i: (i, 0)),
        out_shape=jax.ShapeDtypeStruct((T, N_FLOG), F32),
        scratch_shapes=[pltpu.VMEM((1, N_FLOG), F32)],
        compiler_params=_cp(1),
    )(flog, bf)


def _forget_bwd(dck, dcq, flog, col, bf):
    T = dcq.shape[0]
    bt = _tile(T, 512)
    nb = T // bt

    def body(k0_ref, k1_ref, k2_ref, k3_ref, dcq_ref, f_ref, b_ref, dz_ref, db_ref, carry):
        @pl.when(pl.program_id(0) == 0)
        def _():
            carry[...] = jnp.zeros_like(carry)
            db_ref[...] = jnp.zeros_like(db_ref)

        dc = ((k0_ref[...] + k1_ref[...]) + (k2_ref[...] + k3_ref[...])) + dcq_ref[...]
        dlf = _f32dot(_tri(bt, False), dc) + carry[...]
        carry[...] = dlf[0:1, :]
        z = f_ref[...] + b_ref[...]
        dz = dlf * _sigmoid(-z)
        dz_ref[...] = dz.astype(BF16)
        db_ref[...] += _rowsum(dz)

    slab = lambda j: pl.BlockSpec((None, bt, N_FLOG), lambda i: (j, nb - 1 - i, 0))
    return pl.pallas_call(
        body, name="forget_bwd", grid=(nb,),
        in_specs=[slab(0), slab(1), slab(2), slab(3),
                  pl.BlockSpec((bt, N_FLOG), lambda i: (nb - 1 - i, 0)),
                  pl.BlockSpec((bt, N_FLOG), lambda i: (nb - 1 - i, col)), _resident((1, N_FLOG))],
        out_specs=[pl.BlockSpec((bt, N_FLOG), lambda i: (nb - 1 - i, 0)), _resident((1, N_FLOG))],
        out_shape=[jax.ShapeDtypeStruct((T, N_FLOG), BF16), jax.ShapeDtypeStruct((1, N_FLOG), F32)],
        scratch_shapes=[pltpu.VMEM((1, N_FLOG), F32)],
        compiler_params=_cp(1),
    )(dck, dck, dck, dck, dcq, flog, bf)


def _head_masks():
    lane = lax.broadcasted_iota(jnp.int32, (1, LANE), 1)
    return lane < HEAD_DIM


def _split_heads(x2, is_a):
    zero = jnp.zeros_like(x2)
    return jnp.where(is_a, x2, zero), jnp.where(is_a, zero, x2)


def _attn_fwd(qkv, vt, cb):
    T = qkv.shape[0]
    tq = _tile(T, 512)
    tk = tq
    nq = T // tq
    npair = N_HEADS // 2

    def body(q_ref, k_ref, vt_ref, cb_ref, o_ref, al_ref, m_s, l_s, acc_s):
        i = pl.program_id(1)
        qs = _split_heads(q_ref[...], _head_masks())
        m_s[...] = jnp.full_like(m_s, NEG_INF)
        l_s[...] = jnp.zeros_like(l_s)
        acc_s[...] = jnp.zeros_like(acc_s)

        def step(kk, masked):
            k0 = pl.multiple_of(kk * tk, tk)
            k2 = k_ref[pl.ds(k0, tk), :]
            v2t = vt_ref[:, pl.ds(k0, tk)]
            csb = cb_ref[pl.ds(k0, tk), :]
            for h in range(2):
                cs = csb[:, h * HEAD_DIM:h * HEAD_DIM + 1]
                zt = _nt(k2, qs[h]) - cs
                if masked:
                    rr = lax.broadcasted_iota(jnp.int32, (tk, tq), 0)
                    cc = lax.broadcasted_iota(jnp.int32, (tk, tq), 1)
                    zt = jnp.where(cc >= rr, zt, NEG_INF)
                m_old = m_s[h]
                m_new = jnp.maximum(m_old, jnp.max(zt, axis=0, keepdims=True))
                p = jnp.exp2(zt - m_new)
                a = jnp.exp2(m_old - m_new)
                l_s[h] = a * l_s[h] + jnp.sum(p, axis=0, keepdims=True)
                acc_s[h] = a * acc_s[h] + _nn(v2t, p.astype(BF16))
                m_s[h] = m_new

        def loop_body(kk, carry):
            step(kk, False)
            return carry

        lax.fori_loop(0, i, loop_body, 0)
        step(i, True)
        outs = []
        for h in range(2):
            l = l_s[h]
            outs.append(acc_s[h] * (1.0 / l))
            al_ref[0, h:h + 1, :] = -(m_s[h] + jnp.log2(l))
        al_ref[0, 2:8, :] = jnp.zeros((6, tq), F32)
        dim = lax.broadcasted_iota(jnp.int32, (LANE, 1), 0)
        o_ref[...] = jnp.where(dim < HEAD_DIM, outs[0], outs[1]).T

    rowl = pl.BlockSpec((1, 8, tq), lambda j, i: (j, 0, i))
    return pl.pallas_call(
        body, name="attn_fwd", grid=(npair, nq),
        in_specs=[pl.BlockSpec((tq, LANE), lambda j, i: (i, j)),
                  pl.BlockSpec((T, LANE), lambda j, i: (0, npair + j), pipeline_mode=pl.Buffered(1)),
                  pl.BlockSpec((LANE, T), lambda j, i: (j, 0), pipeline_mode=pl.Buffered(1)),
                  pl.BlockSpec((T, LANE), lambda j, i: (0, j), pipeline_mode=pl.Buffered(1))],
        out_specs=[pl.BlockSpec((tq, LANE), lambda j, i: (i, j)), rowl],
        out_shape=[jax.ShapeDtypeStruct((T, D_ATTN), F32), jax.ShapeDtypeStruct((npair, 8, T), F32)],
        scratch_shapes=[pltpu.VMEM((2, 1, tq), F32), pltpu.VMEM((2, 1, tq), F32), pltpu.VMEM((2, LANE, tq), F32)],
        compiler_params=_cp(2),
    )(qkv, qkv, vt, cb)


def _attn_bwd(qkv, dob, cb, alrow, dlrow):
    T = qkv.shape[0]
    tq = _tile(T, 512)
    tk = tq
    nq = T // tq
    npair = N_HEADS // 2

    def body(q_ref, k_ref, v_ref, do_ref, cb_ref, al_ref, dl_ref, dq_ref, dk_ref, dv_ref, dc_ref, dcq_ref,
             dk_s, dv_s, dc_s):
        kj = pl.program_id(1)
        is_a = _head_masks()
        ks = _split_heads(k_ref[...], is_a)
        vs = _split_heads(v_ref[...], is_a)
        cs = (cb_ref[:, 0:1], cb_ref[:, HEAD_DIM:HEAD_DIM + 1])

        @pl.when(kj == 0)
        def _():
            dq_ref[...] = jnp.zeros_like(dq_ref)
            dcq_ref[...] = jnp.zeros_like(dcq_ref)

        dk_s[...] = jnp.zeros_like(dk_s)
        dv_s[...] = jnp.zeros_like(dv_s)
        dc_s[...] = jnp.zeros_like(dc_s)

        def step(qi, masked):
            q0 = pl.multiple_of(qi * tq, tq)
            q2 = q_ref[pl.ds(q0, tq), :]
            do2 = do_ref[pl.ds(q0, tq), :]
            for h in range(2):
                alr = al_ref[0, h:h + 1, pl.ds(q0, tq)]
                dlr = dl_ref[0, h:h + 1, pl.ds(q0, tq)]
                zt = _nt(ks[h], q2) + (alr - cs[h])
                if masked:
                    rr = lax.broadcasted_iota(jnp.int32, (tk, tq), 0)
                    cc = lax.broadcasted_iota(jnp.int32, (tk, tq), 1)
                    zt = jnp.where(cc >= rr, zt, NEG_INF)
                pt = jnp.exp2(zt)
                dst = pt * (_nt(vs[h], do2) - dlr)
                pb = pt.astype(BF16)
                dsb = dst.astype(BF16)
                dv_s[h] += _nn(pb, do2)
                dk_s[h] += _nn(dsb, q2)
                dc_s[h] += jnp.sum(dst, axis=-1, keepdims=True)
                dcq_ref[0, h:h + 1, pl.ds(q0, tq)] += jnp.sum(dst, axis=0, keepdims=True)
                dq_ref[pl.ds(q0, tq), :] += _tn(dsb, ks[h])

        step(kj, True)

        def loop_body(qi, carry):
            step(qi, False)
            return carry

        lax.fori_loop(kj + 1, nq, loop_body, 0)
        dk_ref[...] = (jnp.where(is_a, dk_s[0], dk_s[1]) * (1.0 / LOG2E)).astype(BF16)
        dv_ref[...] = jnp.where(is_a, dv_s[0], dv_s[1]).astype(BF16)
        lane = lax.broadcasted_iota(jnp.int32, (1, LANE), 1)
        head = 2 * pl.program_id(0)
        dc_ref[...] = jnp.where(lane == head, -dc_s[0], jnp.where(lane == head + 1, -dc_s[1], 0.0))

        @pl.when(kj == nq - 1)
        def _():
            dq_ref[...] = dq_ref[...] * Q_SCALE

    full = lambda col: pl.BlockSpec((T, LANE), lambda j, kj: (0, col(j)), pipeline_mode=pl.Buffered(1))
    tile = lambda col: pl.BlockSpec((tk, LANE), lambda j, kj: (kj, col(j)))
    rowl = pl.BlockSpec((1, 8, T), lambda j, kj: (j, 0, 0))
    return pl.pallas_call(
        body, name="attn_bwd", grid=(npair, nq),
        in_specs=[full(lambda j: j), tile(lambda j: npair + j), tile(lambda j: 2 * npair + j), full(lambda j: j),
                  tile(lambda j: j), rowl, rowl],
        out_specs=[pl.BlockSpec((T, LANE), lambda j, kj: (0, j)), tile(lambda j: j), tile(lambda j: j),
                   pl.BlockSpec((None, tk, LANE), lambda j, kj: (j, kj, 0)), rowl],
        out_shape=[jax.ShapeDtypeStruct((T, D_ATTN), F32), jax.ShapeDtypeStruct((T, D_ATTN), BF16),
                   jax.ShapeDtypeStruct((T, D_ATTN), BF16), jax.ShapeDtypeStruct((npair, T, LANE), F32),
                   jax.ShapeDtypeStruct((npair, 8, T), F32)],
        scratch_shapes=[pltpu.VMEM((2, tk, LANE), F32), pltpu.VMEM((2, tk, LANE), F32), pltpu.VMEM((2, tk, 1), F32)],
        compiler_params=_cp(2),
    )(qkv, qkv, qkv, dob, cb, alrow, dlrow)


HALO = 8


def _shift_rows(cur, other, k, tm, down):
    row = lax.broadcasted_iota(jnp.int32, (tm, 1), 0)
    reps = tm // HALO
    if down:
        rolled = pltpu.roll(cur, k, 0)
        fill = jnp.tile(pltpu.roll(other, k, 0), (reps, 1))
        return jnp.where(row < k, fill, rolled)
    rolled = pltpu.roll(cur, tm - k, 0)
    fill = jnp.tile(pltpu.roll(other, HALO - k, 0), (reps, 1))
    return jnp.where(row >= tm - k, fill, rolled)


def _conv_fwd(c, hh, c_prev, hh_prev, w_ref, first, tm):
    u = c * hh
    u_prev = jnp.where(first, 0.0, c_prev * hh_prev)
    u1 = _shift_rows(u, u_prev, 1, tm, True)
    u2 = _shift_rows(u, u_prev, 2, tm, True)
    y = w_ref[0:1, :] * u2 + w_ref[1:2, :] * u1 + w_ref[2:3, :] * u
    return u, u1, u2, y


def _rms(x, g):
    rs = lax.rsqrt(jnp.mean(x * x, axis=-1, keepdims=True) + RMS_EPS)
    return x * rs * g, rs


def _mixer_tail_fwd(o, bchf, conv_w, g_attn, g_conv, w_mo, x1, lg, lb):
    T = o.shape[0]
    tm = _tile(T, 512)
    hb = tm // HALO

    def body(o_ref, b_ref, c_ref, h_ref, cp_ref, hp_ref, w_ref, ga_ref, gc_ref, wmo_ref, x1_ref, lg_ref, lb_ref,
             x2_ref, r2_ref, mg_ref):
        first = pl.program_id(0) == 0
        _, _, _, y = _conv_fwd(c_ref[...], h_ref[...], cp_ref[...], hp_ref[...], w_ref, first, tm)
        na, _ = _rms(o_ref[...], ga_ref[...])
        nc, _ = _rms(b_ref[...] * y, gc_ref[...])
        nab = na.astype(BF16)
        ncb = nc.astype(BF16)
        mg_ref[:, 0:D_ATTN] = nab
        mg_ref[:, D_ATTN:] = ncb
        r2 = ALPHA * x1_ref[...] + _nn(nab, wmo_ref[0:D_ATTN, :]) + _nn(ncb, wmo_ref[D_ATTN:, :])
        r2_ref[...] = r2
        xhat, _ = _ln_stats(r2)
        x2_ref[...] = xhat * lg_ref[...] + lb_ref[...]

    row = lambda n, col=0: pl.BlockSpec((tm, n), lambda i: (i, col))
    prev = lambda col: pl.BlockSpec((HALO, D_CONV), lambda i: (jnp.maximum(i * hb - 1, 0), col))
    return pl.pallas_call(
        body, name="mixer_tail_fwd", grid=(T // tm,),
        in_specs=[row(D_ATTN), row(D_CONV, 0), row(D_CONV, 1), row(D_CONV, 2), prev(1), prev(2),
                  _resident((3, D_CONV)), _resident((1, D_ATTN)), _resident((1, D_CONV)),
                  _resident((D_MODEL, D_MODEL)), row(D_MODEL), _resident((1, D_MODEL)), _resident((1, D_MODEL))],
        out_specs=[row(D_MODEL), row(D_MODEL), row(D_MODEL)],
        out_shape=[jax.ShapeDtypeStruct((T, D_MODEL), F32), jax.ShapeDtypeStruct((T, D_MODEL), F32),
                   jax.ShapeDtypeStruct((T, D_MODEL), BF16)],
        compiler_params=_cp(1),
    )(o, bchf, bchf, bchf, bchf, bchf, conv_w, g_attn, g_conv, w_mo, x1, lg, lb)


def _head_sum_rows():
    row = lax.broadcasted_iota(jnp.int32, (4 * 8, D_ATTN), 0)
    head = lax.broadcasted_iota(jnp.int32, (4 * 8, D_ATTN), 1) // HEAD_DIM
    return jnp.where((row % 8 < 2) & (2 * (row // 8) + row % 8 == head), 1.0, 0.0).astype(F32)


def _mixer_tail_bwd(dx2, r2, lg, w_mo, o, bchf, conv_w, g_attn, g_conv):
    T = o.shape[0]
    tm = _tile(T, 256)
    hb = tm // HALO

    def body(dx2_ref, r2_ref, lg_ref, wmo_ref, o_ref, b_ref, c_ref, h_ref, cp_ref, hp_ref, w_ref, ga_ref, gc_ref,
             dx1_ref, dr_ref, do_ref, dl_ref, dco_ref, dlg_ref, dlb_ref, dga_ref, dgc_ref):
        i = pl.program_id(0)

        @pl.when(i == 0)
        def _():
            for ref in (dlg_ref, dlb_ref, dga_ref, dgc_ref):
                ref[...] = jnp.zeros_like(ref)

        dy = dx2_ref[...]
        xhat, rstd = _ln_stats(r2_ref[...])
        dr = _ln_bwd(dy, xhat, rstd, lg_ref[...])
        dlg_ref[...] += _rowsum(dy * xhat)
        dlb_ref[...] += _rowsum(dy)
        dx1_ref[...] = ALPHA * dr
        drb = dr.astype(BF16)
        dr_ref[...] = drb
        dna = _nt(drb, wmo_ref[0:D_ATTN, :])
        dnc = _nt(drb, wmo_ref[D_ATTN:, :])

        def rms_bwd(x, g, dn):
            rs = lax.rsqrt(jnp.mean(x * x, axis=-1, keepdims=True) + RMS_EPS)
            dng = dn * g
            dx = rs * dng - x * (rs * rs * rs) * jnp.mean(dng * x, axis=-1, keepdims=True)
            return dx, _rowsum(dn * x * rs)

        oo = o_ref[...]
        do, dga = rms_bwd(oo, ga_ref[...], dna)
        dga_ref[...] += dga
        do_ref[...] = do.astype(BF16)
        dl_ref[...] = lax.dot_general(_head_sum_rows(), do * oo, (((1,), (1,)), ((), ())),
                                      preferred_element_type=F32, precision=lax.Precision.HIGHEST)
        _, _, _, y = _conv_fwd(c_ref[...], h_ref[...], cp_ref[...], hp_ref[...], w_ref, i == 0, tm)
        dco, dgc = rms_bwd(b_ref[...] * y, gc_ref[...], dnc)
        dgc_ref[...] += dgc
        dco_ref[...] = dco

    row = lambda n, col=0: pl.BlockSpec((tm, n), lambda i: (i, col))
    prev = lambda col: pl.BlockSpec((HALO, D_CONV), lambda i: (jnp.maximum(i * hb - 1, 0), col))
    vec = lambda n: _resident((1, n))
    return pl.pallas_call(
        body, name="mixer_tail_bwd", grid=(T // tm,),
        in_specs=[row(D_MODEL), row(D_MODEL), vec(D_MODEL), _resident((D_MODEL, D_MODEL)), row(D_ATTN),
                  row(D_CONV, 0), row(D_CONV, 1), row(D_CONV, 2), prev(1), prev(2), _resident((3, D_CONV)),
                  vec(D_ATTN), vec(D_CONV)],
        out_specs=[row(D_MODEL), row(D_MODEL), row(D_ATTN), pl.BlockSpec((4 * 8, tm), lambda i: (0, i)), row(D_CONV),
                   vec(D_MODEL), vec(D_MODEL), vec(D_ATTN), vec(D_CONV)],
        out_shape=[jax.ShapeDtypeStruct((T, D_MODEL), F32), jax.ShapeDtypeStruct((T, D_MODEL), BF16),
                   jax.ShapeDtypeStruct((T, D_ATTN), BF16), jax.ShapeDtypeStruct((4 * 8, T), F32),
                   jax.ShapeDtypeStruct((T, D_CONV), F32), jax.ShapeDtypeStruct((1, D_MODEL), F32),
                   jax.ShapeDtypeStruct((1, D_MODEL), F32), jax.ShapeDtypeStruct((1, D_ATTN), F32),
                   jax.ShapeDtypeStruct((1, D_CONV), F32)],
        compiler_params=_cp(1),
    )(dx2, r2, lg, w_mo, o, bchf, bchf, bchf, bchf, bchf, conv_w, g_attn, g_conv)


def _conv_bwd(dco, bchf, conv_w):
    T = dco.shape[0]
    tm = _tile(T, 512)
    hb = tm // HALO
    nt = T // tm

    def body(dco_ref, dcon_ref, b_ref, bn_ref, c_ref, h_ref, cp_ref, hp_ref, w_ref, dbch_ref, dw_ref):
        i = pl.program_id(0)

        @pl.when(i == 0)
        def _():
            dw_ref[...] = jnp.zeros_like(dw_ref)

        cc = c_ref[...]
        hh = h_ref[...]
        u, u1, u2, y = _conv_fwd(cc, hh, cp_ref[...], hp_ref[...], w_ref, i == 0, tm)
        dco = dco_ref[...]
        bb = b_ref[...]
        dyc = dco * bb
        dy_next = jnp.where(i == nt - 1, 0.0, dcon_ref[...] * bn_ref[...])
        d1 = _shift_rows(dyc, dy_next, 1, tm, False)
        d2 = _shift_rows(dyc, dy_next, 2, tm, False)
        du = w_ref[2:3, :] * dyc + w_ref[1:2, :] * d1 + w_ref[0:1, :] * d2
        dbch_ref[:, 0:D_CONV] = (dco * y).astype(BF16)
        dbch_ref[:, D_CONV:2 * D_CONV] = (du * hh).astype(BF16)
        dbch_ref[:, 2 * D_CONV:] = (du * cc).astype(BF16)
        dw_ref[0:1, :] += _rowsum(dyc * u2)
        dw_ref[1:2, :] += _rowsum(dyc * u1)
        dw_ref[2:3, :] += _rowsum(dyc * u)

    row = lambda n, col=0: pl.BlockSpec((tm, n), lambda i: (i, col))
    prev = lambda col: pl.BlockSpec((HALO, D_CONV), lambda i: (jnp.maximum(i * hb - 1, 0), col))
    nxt = lambda col: pl.BlockSpec((HALO, D_CONV), lambda i: (jnp.minimum((i + 1) * hb, T // HALO - 1), col))
    return pl.pallas_call(
        body, name="conv_bwd", grid=(nt,),
        in_specs=[row(D_CONV), nxt(0), row(D_CONV, 0), nxt(0), row(D_CONV, 1), row(D_CONV, 2), prev(1), prev(2),
                  _resident((3, D_CONV))],
        out_specs=[row(3 * D_CONV), _resident((8, D_CONV))],
        out_shape=[jax.ShapeDtypeStruct((T, 3 * D_CONV), BF16), jax.ShapeDtypeStruct((8, D_CONV), F32)],
        compiler_params=_cp(1),
    )(dco, dco, bchf, bchf, bchf, bchf, bchf, bchf, conv_w)


def _mixer_in_bwd(dx1a, dq, dk, dv, dbch, dfl, w_qkv, w_bch, w_f):
    T = dx1a.shape[0]
    tm = _tile(T, 512)

    def body(a_ref, dq_ref, dk_ref, dv_ref, db_ref, df_ref, wq_ref, wb_ref, wf_ref, o_ref):
        acc = a_ref[...] + _nt(db_ref[...], wb_ref[...]) + _nt(df_ref[...], wf_ref[...])
        for n, ref in enumerate((dq_ref, dk_ref, dv_ref)):
            acc = acc + _nt(ref[...].astype(BF16), wq_ref[:, n * D_ATTN:(n + 1) * D_ATTN])
        o_ref[...] = acc

    row = lambda n: pl.BlockSpec((tm, n), lambda i: (i, 0))
    return pl.pallas_call(
        body, name="mixer_in_bwd", grid=(T // tm,),
        in_specs=[row(D_MODEL), row(D_ATTN), row(D_ATTN), row(D_ATTN), row(3 * D_CONV), row(N_FLOG),
                  _resident((D_MODEL, 3 * D_ATTN)), _resident((D_MODEL, 3 * D_CONV)), _resident((D_MODEL, N_FLOG))],
        out_specs=row(D_MODEL),
        out_shape=jax.ShapeDtypeStruct((T, D_MODEL), F32),
        compiler_params=_cp(1),
    )(dx1a, dq, dk, dv, dbch, dfl, w_qkv, w_bch, w_f)


def _ple_loss(x3, p, tgt, w_g, w_p, b_g, lg, lb):
    T = x3.shape[0]
    tm = _tile(T, 512)

    def body(x_ref, p_ref, t_ref, wg_ref, wp_ref, bg_ref, lg_ref, lb_ref,
             dx_ref, de_ref, dz_ref, loss_ref, dlg_ref, dlb_ref, dbg_ref):
        @pl.when(pl.program_id(0) == 0)
        def _():
            for ref in (loss_ref, dlg_ref, dlb_ref, dbg_ref):
                ref[...] = jnp.zeros_like(ref)

        xf = x_ref[...]
        gate = _sigmoid(_nn(xf.astype(BF16), wg_ref[...]) + bg_ref[...])
        e = _nn(p_ref[...].astype(BF16), wp_ref[...])
        xhat, rstd = _ln_stats(ALPHA * xf + gate * e)
        err = xhat * lg_ref[...] + lb_ref[...] - t_ref[...]
        sq = jnp.sum(_rowsum(err * err), axis=-1, keepdims=True)
        loss_ref[...] += jnp.broadcast_to(sq * (0.5 / D_MODEL), loss_ref.shape)
        dy = err * (1.0 / D_MODEL)
        dr = _ln_bwd(dy, xhat, rstd, lg_ref[...])
        dlg_ref[...] += _rowsum(dy * xhat)
        dlb_ref[...] += _rowsum(dy)
        de_ref[...] = (dr * gate).astype(BF16)
        dz = dr * e * gate * (1.0 - gate)
        dbg_ref[...] += _rowsum(dz)
        dzb = dz.astype(BF16)
        dz_ref[...] = dzb
        dx_ref[...] = ALPHA * dr + _nt(dzb, wg_ref[...])

    row = lambda n: pl.BlockSpec((tm, n), lambda i: (i, 0))
    vec = lambda n: _resident((1, n))
    return pl.pallas_call(
        body, name="ple_loss", grid=(T // tm,),
        in_specs=[row(D_MODEL), row(PLE_DIM), row(D_MODEL), _resident((D_MODEL, D_MODEL)),
                  _resident((PLE_DIM, D_MODEL)), vec(D_MODEL), vec(D_MODEL), vec(D_MODEL)],
        out_specs=[row(D_MODEL), row(D_MODEL), row(D_MODEL), vec(LANE), vec(D_MODEL), vec(D_MODEL), vec(D_MODEL)],
        out_shape=[jax.ShapeDtypeStruct((T, D_MODEL), F32), jax.ShapeDtypeStruct((T, D_MODEL), BF16),
                   jax.ShapeDtypeStruct((T, D_MODEL), BF16), jax.ShapeDtypeStruct((1, LANE), F32),
                   jax.ShapeDtypeStruct((1, D_MODEL), F32), jax.ShapeDtypeStruct((1, D_MODEL), F32),
                   jax.ShapeDtypeStruct((1, D_MODEL), F32)],
        compiler_params=_cp(1),
    )(x3, p, tgt, w_g, w_p, b_g, lg, lb)


def _lane_layout(v8):
    return jnp.repeat(v8, HEAD_DIM, axis=1)


def _row_layout(v8):
    t = v8.shape[0]
    return jnp.pad(v8.T.reshape(N_HEADS // 2, 2, t), ((0, 0), (0, 6), (0, 0)))


def _from_lane_layout(vl):
    return vl[:, ::HEAD_DIM]


def _from_row_layout(vr):
    return vr[:, :2, :].reshape(N_HEADS, -1).T


def _local_step(x, p, tgt, w):
    bf = lambda a: a.astype(BF16)
    w1i, w1o, w2i, w2o = bf(w["ffn1_w_in"]), bf(w["ffn1_w_out"]), bf(w["ffn2_w_in"]), bf(w["ffn2_w_out"])
    wmi = w["w_mix_in"]
    o_f = 3 * D_ATTN
    o_b = o_f + N_HEADS
    w_qkv = bf(wmi[:, :o_f])
    w_f = bf(jnp.pad(wmi[:, o_f:o_b], ((0, 0), (0, N_FLOG - N_HEADS))))
    w_bch = bf(wmi[:, o_b:])
    w_bchf = jnp.concatenate([w_bch, w_f], axis=1)
    w_mo, w_g, w_p = bf(w["w_mix_out"]), bf(w["w_ple_gate"]), bf(w["w_ple"])
    b_f = jnp.pad(w["b_forget"], ((0, 0), (0, N_FLOG - N_HEADS)))

    x1, r1, g1, u1, h1 = _ffn_fwd(x, w1i, w1o, w["ln1_g"], w["ln1_b"], "ffn1_fwd")
    q_scale = jnp.concatenate([jnp.full((1, D_ATTN), Q_SCALE * LOG2E, F32), jnp.ones((1, 2 * D_ATTN), F32)], axis=1)
    qkv = _matmul_nn(x1, w_qkv, q_scale, BF16, "proj_qkv")
    bchf = _matmul_nn(x1, w_bchf, jnp.ones((1, 3 * D_CONV + N_FLOG), F32), F32, "proj_bchf")
    fcol = 3 * D_CONV // N_FLOG
    c = _forget_cumsum(bchf, fcol, b_f)
    c8 = c[:, :N_HEADS]
    cb = _lane_layout(c8)
    o, alrow = _attn_fwd(qkv, qkv[:, 2 * D_ATTN:].T, cb)
    x2, r2, merged = _mixer_tail_fwd(o, bchf, w["conv_w"], w["g_attn"], w["g_conv"], w_mo, x1, w["ln2_g"], w["ln2_b"])
    x3, r3, g2, u2, h2 = _ffn_fwd(x2, w2i, w2o, w["ln3_g"], w["ln3_b"], "ffn2_fwd")

    grads = {}
    dx3, de, dz, loss, grads["ln4_g"], grads["ln4_b"], grads["b_ple_gate"] = _ple_loss(
        x3, p, tgt, w_g, w_p, w["b_ple_gate"], w["ln4_g"], w["ln4_b"])
    grads["w_ple"] = _matmul_tn(p, de, "dw_ple")
    grads["w_ple_gate"] = _matmul_tn(x3, dz, "dw_ple_gate")

    dx2, dgu2, df2, grads["ln3_g"], grads["ln3_b"] = _ffn_bwd(dx3, r3, g2, u2, w2i, w2o, w["ln3_g"], "ffn2_bwd")
    grads["ffn2_w_in"] = _matmul_tn(x2, dgu2, "dw_ffn2_in")
    grads["ffn2_w_out"] = _matmul_tn(h2, df2, "dw_ffn2_out")

    (dx1a, dr2, dob, delta, dco, grads["ln2_g"], grads["ln2_b"], grads["g_attn"], grads["g_conv"]) = _mixer_tail_bwd(
        dx2, r2, w["ln2_g"], w_mo, o, bchf, w["conv_w"], w["g_attn"], w["g_conv"])
    grads["w_mix_out"] = _matmul_tn(merged, dr2, "dw_mix_out")
    dbch, dcw = _conv_bwd(dco, bchf, w["conv_w"])
    grads["conv_w"] = dcw[:3]
    dq, dk, dv, dck, dcq = _attn_bwd(qkv, dob, cb, alrow, delta.reshape(N_HEADS // 2, 8, -1))
    dcq_lanes = jnp.pad(_from_row_layout(dcq), ((0, 0), (0, N_FLOG - N_HEADS)))
    dfl, dbf = _forget_bwd(dck, dcq_lanes, bchf, fcol, b_f)
    grads["b_forget"] = dbf[:, :N_HEADS]
    dx1 = _mixer_in_bwd(dx1a, dq, dk, dv, dbch, dfl, w_qkv, w_bch, w_f)
    grads["w_mix_in"] = jnp.concatenate(
        [_matmul_tn(x1, dq, "dw_q"), _matmul_tn(x1, dk, "dw_k"), _matmul_tn(x1, dv, "dw_v"),
         _matmul_tn(x1, dfl, "dw_flog")[:, :N_HEADS], _matmul_tn(x1, dbch, "dw_bch")], axis=1)

    dx0, dgu1, df1, grads["ln1_g"], grads["ln1_b"] = _ffn_bwd(dx1, r1, g1, u1, w1i, w1o, w["ln1_g"], "ffn1_bwd")
    grads["ffn1_w_in"] = _matmul_tn(x, dgu1, "dw_ffn1_in")
    grads["ffn1_w_out"] = _matmul_tn(h1, df1, "dw_ffn1_out")
    return loss, dx0, grads


WEIGHTS = ["ffn1_w_in", "ffn1_w_out", "ln1_g", "ln1_b", "w_mix_in", "b_forget", "conv_w", "g_attn", "g_conv",
           "w_mix_out", "ln2_g", "ln2_b", "ffn2_w_in", "ffn2_w_out", "ln3_g", "ln3_b", "w_ple", "w_ple_gate",
           "b_ple_gate", "ln4_g", "ln4_b"]
LAYOUT = {
    "ffn1_w_in": ((D_MODEL, 2 * D_FF), 1), "ffn1_w_out": ((D_FF, D_MODEL), 0),
    "w_mix_in": ((D_MODEL, 3 * D_ATTN + N_HEADS + 3 * D_CONV), 1), "conv_w": ((3, D_CONV), 1),
    "w_mix_out": ((D_MODEL, D_MODEL), 0), "ffn2_w_in": ((D_MODEL, 2 * D_FF), 1), "ffn2_w_out": ((D_FF, D_MODEL), 0),
    "w_ple": ((PLE_DIM, D_MODEL), 1), "w_ple_gate": ((D_MODEL, D_MODEL), 0),
    "ln1_g": ((1, D_MODEL), None), "ln1_b": ((1, D_MODEL), None), "b_forget": ((1, N_HEADS), None),
    "g_attn": ((1, D_ATTN), None), "g_conv": ((1, D_CONV), None), "ln2_g": ((1, D_MODEL), None),
    "ln2_b": ((1, D_MODEL), None), "ln3_g": ((1, D_MODEL), None), "ln3_b": ((1, D_MODEL), None),
    "b_ple_gate": ((1, D_MODEL), None), "ln4_g": ((1, D_MODEL), None), "ln4_b": ((1, D_MODEL), None),
}
ROW = 1024
ROW_ALIGN = 64


def _shard_shape(name):
    shape, axis = LAYOUT[name]
    if axis is None:
        return shape
    return tuple(s // N_CHIPS if a == axis else s for a, s in enumerate(shape))


def _piece_len(name):
    n = math.prod(_shard_shape(name))
    return -(-n // ROW) * ROW


PAYLOAD_LEN = sum(_piece_len(n) for n in WEIGHTS)
PAYLOAD_ROWS = -(-PAYLOAD_LEN // (ROW * ROW_ALIGN)) * ROW_ALIGN


def _pack(pieces):
    lead = pieces[WEIGHTS[0]].shape[:-len(_shard_shape(WEIGHTS[0]))]
    flat = []
    for n in WEIGHTS:
        a = pieces[n].reshape(lead + (-1,))
        flat.append(jnp.pad(a, [(0, 0)] * len(lead) + [(0, _piece_len(n) - a.shape[-1])]))
    tail = PAYLOAD_ROWS * ROW - PAYLOAD_LEN
    if tail:
        flat.append(jnp.zeros(lead + (tail,), flat[0].dtype))
    return jnp.concatenate(flat, axis=-1).reshape(lead + (PAYLOAD_ROWS, ROW))


def _unpack(payload):
    lead = payload.shape[:-2]
    flat = payload.reshape(lead + (-1,))
    out, off = {}, 0
    for n in WEIGHTS:
        ss = _shard_shape(n)
        out[n] = flat[..., off:off + math.prod(ss)].reshape(lead + ss)
        off += _piece_len(n)
    return out


def _split_chips(name, full):
    shape, axis = LAYOUT[name]
    if axis is None:
        return jnp.broadcast_to(full[None], (N_CHIPS,) + shape)
    if axis == 0:
        return full.reshape((N_CHIPS, shape[0] // N_CHIPS) + shape[1:])
    return jnp.moveaxis(full.reshape(shape[:1] + (N_CHIPS, shape[1] // N_CHIPS)), 1, 0)


def _join_chips(name, parts):
    shape, axis = LAYOUT[name]
    if axis is None:
        return parts[0]
    if axis == 0:
        return parts.reshape(shape)
    return jnp.moveaxis(parts, 0, 1).reshape(shape)


def _place():
    x, y, c = lax.axis_index("x"), lax.axis_index("y"), lax.axis_index("c")
    others = [(1 - x, y), (x, 1 - y), (1 - x, 1 - y)]
    return x, y, c, others


def _all_gather_weights(wp):
    half = wp.shape[1:]

    def body(w_ref, out_ref, send_sems, recv_sems):
        x, y, c, others = _place()
        s = 2 * x + y

        def copy(k, chip, half_idx, to):
            slot = out_ref.at[2 * chip[0] + chip[1], half_idx]
            return pltpu.make_async_remote_copy(src_ref=slot, dst_ref=slot, send_sem=send_sems.at[k],
                                                recv_sem=recv_sems.at[k], device_id=to, device_id_type=MESH)

        first = []
        for j, chip in enumerate(others):
            slot = out_ref.at[s, c]
            first.append(pltpu.make_async_remote_copy(src_ref=w_ref.at[c], dst_ref=slot, send_sem=send_sems.at[j],
                                                      recv_sem=recv_sems.at[j], device_id=(*chip, c),
                                                      device_id_type=MESH))
        for cp in first:
            cp.start()
        passed = [copy(3 + j, chip, c, (x, y, 1 - c)) for j, chip in enumerate(others)]
        for j, chip in enumerate(others):
            copy(j, chip, c, (x, y, c)).wait_recv()
            passed[j].start()
        for j, chip in enumerate(others):
            copy(3 + j, chip, 1 - c, (x, y, c)).wait_recv()
        for cp in first + passed:
            cp.wait_send()

    any_spec = pl.BlockSpec(memory_space=pl.ANY)
    return pl.pallas_call(
        body, name="all_gather_weights",
        out_shape=jax.ShapeDtypeStruct((N_CHIPS, 2) + half, wp.dtype),
        in_specs=[any_spec], out_specs=any_spec,
        scratch_shapes=[pltpu.SemaphoreType.DMA((6,)), pltpu.SemaphoreType.DMA((6,))],
    )(wp)


def _swap_halves(g):
    rh = g.shape[2]

    def body(g_ref, out_ref, send_sem, recv_sem):
        x, y, c, _ = _place()
        cp = pltpu.make_async_remote_copy(src_ref=g_ref.at[:, 1 - c], dst_ref=out_ref, send_sem=send_sem,
                                          recv_sem=recv_sem, device_id=(x, y, 1 - c), device_id_type=MESH)
        cp.start()
        cp.wait()

    any_spec = pl.BlockSpec(memory_space=pl.ANY)
    return pl.pallas_call(
        body, name="grad_swap_halves",
        out_shape=jax.ShapeDtypeStruct((N_CHIPS, rh, ROW), g.dtype),
        in_specs=[any_spec], out_specs=any_spec,
        scratch_shapes=[pltpu.SemaphoreType.DMA, pltpu.SemaphoreType.DMA],
    )(g)


def _exchange_chips(pp):
    def body(p_ref, out_ref, send_sems, recv_sems):
        x, y, c, others = _place()
        s = 2 * x + y
        sends = []
        for j, chip in enumerate(others):
            sends.append(pltpu.make_async_remote_copy(
                src_ref=p_ref.at[2 * chip[0] + chip[1]], dst_ref=out_ref.at[s], send_sem=send_sems.at[j],
                recv_sem=recv_sems.at[j], device_id=(*chip, c), device_id_type=MESH))
        for cp in sends:
            cp.start()
        for j, chip in enumerate(others):
            slot = out_ref.at[2 * chip[0] + chip[1]]
            pltpu.make_async_remote_copy(src_ref=slot, dst_ref=slot, send_sem=send_sems.at[j],
                                         recv_sem=recv_sems.at[j], device_id=(x, y, c),
                                         device_id_type=MESH).wait_recv()
        for cp in sends:
            cp.wait_send()

    any_spec = pl.BlockSpec(memory_space=pl.ANY)
    return pl.pallas_call(
        body, name="grad_exchange_chips",
        out_shape=jax.ShapeDtypeStruct(pp.shape, pp.dtype),
        in_specs=[any_spec], out_specs=any_spec,
        scratch_shapes=[pltpu.SemaphoreType.DMA((3,)), pltpu.SemaphoreType.DMA((3,))],
    )(pp)


def _share_half(r):
    def body(r_ref, out_ref, send_sem, recv_sem):
        x, y, c, _ = _place()
        cp = pltpu.make_async_remote_copy(src_ref=r_ref, dst_ref=out_ref, send_sem=send_sem, recv_sem=recv_sem,
                                          device_id=(x, y, 1 - c), device_id_type=MESH)
        cp.start()
        cp.wait()

    any_spec = pl.BlockSpec(memory_space=pl.ANY)
    return pl.pallas_call(
        body, name="grad_share_half",
        out_shape=jax.ShapeDtypeStruct(r.shape, r.dtype),
        in_specs=[any_spec], out_specs=any_spec,
        scratch_shapes=[pltpu.SemaphoreType.DMA, pltpu.SemaphoreType.DMA],
    )(r)


def _add_my_half(g, sib, c_idx):
    rh = g.shape[2]
    tr = _tile(rh, 512)

    def body(c_ref, g_ref, s_ref, o_ref):
        o_ref[...] = (g_ref[...] + s_ref[...]).astype(BF16)

    return pl.pallas_call(
        body, name="grad_add_halves",
        grid_spec=pltpu.PrefetchScalarGridSpec(
            num_scalar_prefetch=1, grid=(N_CHIPS, rh // tr),
            in_specs=[pl.BlockSpec((None, None, tr, ROW), lambda s, i, c: (s, c[0], i, 0)),
                      pl.BlockSpec((None, tr, ROW), lambda s, i, c: (s, i, 0))],
            out_specs=pl.BlockSpec((None, tr, ROW), lambda s, i, c: (s, i, 0))),
        out_shape=jax.ShapeDtypeStruct((N_CHIPS, rh, ROW), BF16),
        compiler_params=_cp(2),
    )(c_idx, g, sib)


def _sum_chips(parts, pp, s_idx):
    rh = parts.shape[1]
    tr = _tile(rh, 512)

    def body(s_ref, p0, p1, p2, p3, mine_ref, o_ref):
        own = mine_ref[...]
        t = [jnp.where(s_ref[0] == k, own, p[...]).astype(F32) for k, p in enumerate((p0, p1, p2, p3))]
        o_ref[...] = ((t[0] + t[1]) + t[2]) + t[3]

    slot = lambda k: pl.BlockSpec((None, tr, ROW), lambda i, s: (jnp.where(s[0] == k, (k + 1) % N_CHIPS, k), i, 0))
    return pl.pallas_call(
        body, name="grad_sum_chips",
        grid_spec=pltpu.PrefetchScalarGridSpec(
            num_scalar_prefetch=1, grid=(rh // tr,),
            in_specs=[slot(0), slot(1), slot(2), slot(3), pl.BlockSpec((None, tr, ROW), lambda i, s: (s[0], i, 0))],
            out_specs=pl.BlockSpec((tr, ROW), lambda i, s: (i, 0))),
        out_shape=jax.ShapeDtypeStruct((rh, ROW), F32),
        compiler_params=_cp(1),
    )(s_idx, parts, parts, parts, parts, pp)


def _adamw(w, g_mine, g_sib, m, v, c_idx):
    rows = w.shape[0]
    tr = _tile(rows // 2, 512)
    nbh = rows // 2 // tr
    c1 = 1.0 - ADAM_B1 ** ADAM_STEP
    c2 = 1.0 - ADAM_B2 ** ADAM_STEP

    def body(c_ref, w_ref, gm_ref, gs_ref, m_ref, v_ref, g_ref, d_ref, mo_ref, vo_ref):
        g = jnp.where(pl.program_id(0) // nbh == c_ref[0], gm_ref[...], gs_ref[...])
        g_ref[...] = g
        m = ADAM_B1 * m_ref[...] + (1.0 - ADAM_B1) * g
        v = ADAM_B2 * v_ref[...] + (1.0 - ADAM_B2) * (g * g)
        mo_ref[...] = m
        vo_ref[...] = v
        d_ref[...] = -ADAM_LR * ((m / c1) / (jnp.sqrt(v / c2) + ADAM_EPS) + ADAM_WD * w_ref[...])

    spec = pl.BlockSpec((tr, ROW), lambda i, c: (i, 0))
    half = pl.BlockSpec((tr, ROW), lambda i, c: (i % nbh, 0))
    return pl.pallas_call(
        body, name="adamw",
        grid_spec=pltpu.PrefetchScalarGridSpec(
            num_scalar_prefetch=1, grid=(rows // tr,),
            in_specs=[spec, half, half, spec, spec], out_specs=[spec] * 4),
        out_shape=[jax.ShapeDtypeStruct(w.shape, F32)] * 4,
        compiler_params=_cp(1),
    )(c_idx, w, g_mine, g_sib, m, v)


def kernel(x, p, ffn1_w_in, ffn1_w_out, ln1_g, ln1_b, w_mix_in, b_forget, conv_w, g_attn, g_conv, w_mix_out, ln2_g, ln2_b, ffn2_w_in, ffn2_w_out, ln3_g, ln3_b, w_ple, w_ple_gate, b_ple_gate, ln4_g, ln4_b, loss_target, m_ffn1_w_in, m_ffn1_w_out, m_ln1_g, m_ln1_b, m_w_mix_in, m_b_forget, m_conv_w, m_g_attn, m_g_conv, m_w_mix_out, m_ln2_g, m_ln2_b, m_ffn2_w_in, m_ffn2_w_out, m_ln3_g, m_ln3_b, m_w_ple, m_w_ple_gate, m_b_ple_gate, m_ln4_g, m_ln4_b, v_ffn1_w_in, v_ffn1_w_out, v_ln1_g, v_ln1_b, v_w_mix_in, v_b_forget, v_conv_w, v_g_attn, v_g_conv, v_w_mix_out, v_ln2_g, v_ln2_b, v_ffn2_w_in, v_ffn2_w_out, v_ln3_g, v_ln3_b, v_w_ple, v_w_ple_gate, v_b_ple_gate, v_ln4_g, v_ln4_b):
    args = dict(locals())
    shard = {n: args[n][0] if LAYOUT[n][1] is not None else args[n] for n in WEIGHTS}
    m_shard = {n: args["m_" + n][0] if LAYOUT[n][1] is not None else args["m_" + n] for n in WEIGHTS}
    v_shard = {n: args["v_" + n][0] if LAYOUT[n][1] is not None else args["v_" + n] for n in WEIGHTS}
    rh = PAYLOAD_ROWS // 2

    c_idx = lax.axis_index("c").astype(jnp.int32).reshape(1)
    chip = (2 * lax.axis_index("x") + lax.axis_index("y")).astype(jnp.int32)
    w_pay = _pack(shard)
    w_bf = w_pay.astype(BF16).reshape(2, rh, ROW)
    gathered = lax.dynamic_update_slice(_all_gather_weights(w_bf), w_bf[None], (chip, 0, 0, 0))
    parts = _unpack(gathered.reshape(N_CHIPS, PAYLOAD_ROWS, ROW))
    full = {n: _join_chips(n, parts[n]) for n in WEIGHTS}
    full.update({n: shard[n] for n in WEIGHTS if LAYOUT[n][1] is None})

    loss_acc, grad_x, grads = _local_step(x[0], p[0, 0], loss_target[0], full)
    loss = lax.psum(loss_acc[0, 0], ("x", "y", "c"))

    g_pay = _pack({n: _split_chips(n, grads[n]) for n in WEIGHTS}).reshape(N_CHIPS, 2, rh, ROW)
    chip_sums = _add_my_half(g_pay, _swap_halves(g_pay), c_idx)
    my_half = _sum_chips(_exchange_chips(chip_sums), chip_sums, chip.reshape(1))
    reduced, delta, new_m, new_v = _adamw(w_pay, my_half, _share_half(my_half), _pack(m_shard), _pack(v_shard), c_idx)

    def outs(payload):
        u = _unpack(payload)
        return [u[n][None] if LAYOUT[n][1] is not None else u[n] for n in WEIGHTS]

    return (loss, grad_x[None], *outs(reduced), *outs(delta), *outs(new_m), *outs(new_v))
```

```python
import functools
import math

import jax
import jax.numpy as jnp
from jax import lax
from jax.experimental import pallas as pl
from jax.experimental.pallas import tpu as pltpu

F32 = jnp.float32
BF16 = jnp.bfloat16

D_MODEL = 1024
D_FF = 2816
N_HEADS = 8
HEAD_DIM = 64
D_ATTN = N_HEADS * HEAD_DIM
D_CONV = 512
PLE_DIM = 256
N_FLOG = 128
ALPHA = 2.0 ** 0.25
LN_EPS = 1e-5
RMS_EPS = 1e-6
NEG_INF = -1e30
Q_SCALE = 1.0 / math.sqrt(HEAD_DIM)
LOG2E = math.log2(math.e)

ADAM_LR = 0.001
ADAM_B1 = 0.9
ADAM_B2 = 0.999
ADAM_EPS = 1e-08
ADAM_WD = 0.01
ADAM_STEP = 10

V7X_VMEM_BYTES = 64 << 20
VMEM_LIMIT = V7X_VMEM_BYTES - (8 << 20)
LANE = 128
FF_CHUNK = 256
N_CHIPS = 4
MESH = pl.DeviceIdType.MESH


def _cp(n_axes):
    return pltpu.CompilerParams(dimension_semantics=("arbitrary",) * n_axes, vmem_limit_bytes=VMEM_LIMIT)


def _resident(shape):
    n = len(shape)
    return pl.BlockSpec(shape, lambda *_: (0,) * n, pipeline_mode=pl.Buffered(1))


def _nn(a, b):
    return jnp.dot(a, b, preferred_element_type=F32)


def _nt(a, b):
    return lax.dot_general(a, b, (((1,), (1,)), ((), ())), preferred_element_type=F32)


def _tn(a, b):
    return lax.dot_general(a, b, (((0,), (0,)), ((), ())), preferred_element_type=F32)


def _ln_stats(r):
    mu = jnp.mean(r, axis=-1, keepdims=True)
    xc = r - mu
    var = jnp.mean(xc * xc, axis=-1, keepdims=True)
    rstd = lax.rsqrt(var + LN_EPS)
    return xc * rstd, rstd


def _ln_bwd(dy, xhat, rstd, g):
    dxh = dy * g
    m1 = jnp.mean(dxh, axis=-1, keepdims=True)
    m2 = jnp.mean(dxh * xhat, axis=-1, keepdims=True)
    return rstd * (dxh - m1 - xhat * m2)


def _sigmoid(z):
    return 1.0 / (1.0 + jnp.exp(-z))


def _rowsum(a):
    return jnp.sum(a, axis=0, keepdims=True)


def _tile(total, want):
    if total <= want:
        return total
    for t in range(want - want % 8, 0, -8):
        if total % t == 0:
            return t
    raise ValueError((total, want))


def _ffn_fwd(x, w_in, w_out, lg, lb, name):
    T = x.shape[0]
    tm = _tile(T, 512)
    nf = D_FF // FF_CHUNK

    def body(x_ref, wi_ref, wo_ref, lg_ref, lb_ref, xo_ref, r_ref, g_ref, u_ref, h_ref):
        xf = x_ref[...]
        xb = xf.astype(BF16)
        acc = jnp.zeros((tm, D_MODEL), F32)
        for j in range(nf):
            c0 = j * FF_CHUNK
            g = _nn(xb, wi_ref[:, c0:c0 + FF_CHUNK])
            u = _nn(xb, wi_ref[:, D_FF + c0:D_FF + c0 + FF_CHUNK])
            hb = (g * _sigmoid(g) * u).astype(BF16)
            g_ref[:, c0:c0 + FF_CHUNK] = g.astype(BF16)
            u_ref[:, c0:c0 + FF_CHUNK] = u.astype(BF16)
            h_ref[:, c0:c0 + FF_CHUNK] = hb
            acc = acc + _nn(hb, wo_ref[c0:c0 + FF_CHUNK, :])
        r = ALPHA * xf + 0.5 * acc
        r_ref[...] = r
        xhat, _ = _ln_stats(r)
        xo_ref[...] = xhat * lg_ref[...] + lb_ref[...]

    row = lambda n: pl.BlockSpec((tm, n), lambda i: (i, 0))
    return pl.pallas_call(
        body, name=name, grid=(T // tm,),
        in_specs=[row(D_MODEL), _resident((D_MODEL, 2 * D_FF)), _resident((D_FF, D_MODEL)),
                  _resident((1, D_MODEL)), _resident((1, D_MODEL))],
        out_specs=[row(D_MODEL), row(D_MODEL), row(D_FF), row(D_FF), row(D_FF)],
        out_shape=[jax.ShapeDtypeStruct((T, D_MODEL), F32), jax.ShapeDtypeStruct((T, D_MODEL), F32),
                   jax.ShapeDtypeStruct((T, D_FF), BF16), jax.ShapeDtypeStruct((T, D_FF), BF16),
                   jax.ShapeDtypeStruct((T, D_FF), BF16)],
        compiler_params=_cp(1),
    )(x, w_in, w_out, lg, lb)


def _ffn_bwd(dxo, r, g, u, w_in, w_out, lg, name):
    T = r.shape[0]
    tm = _tile(T, 256)
    nf = D_FF // FF_CHUNK

    def body(dxo_ref, r_ref, g_ref, u_ref, wi_ref, wo_ref, lg_ref, dx_ref, dgu_ref, df_ref, dlg_ref, dlb_ref):
        i = pl.program_id(0)
        dy = dxo_ref[...]
        xhat, rstd = _ln_stats(r_ref[...])
        dr = _ln_bwd(dy, xhat, rstd, lg_ref[...])

        @pl.when(i == 0)
        def _():
            dlg_ref[...] = jnp.zeros_like(dlg_ref)
            dlb_ref[...] = jnp.zeros_like(dlb_ref)

        dlg_ref[...] += _rowsum(dy * xhat)
        dlb_ref[...] += _rowsum(dy)
        dfb = (0.5 * dr).astype(BF16)
        df_ref[...] = dfb
        acc = jnp.zeros((tm, D_MODEL), F32)
        dh_ahead = _nt(dfb, wo_ref[0:FF_CHUNK, :])
        for j in range(nf):
            c0 = j * FF_CHUNK
            dh = dh_ahead
            if j + 1 < nf:
                dh_ahead = _nt(dfb, wo_ref[c0 + FF_CHUNK:c0 + 2 * FF_CHUNK, :])
            gg = g_ref[:, c0:c0 + FF_CHUNK].astype(F32)
            uu = u_ref[:, c0:c0 + FF_CHUNK].astype(F32)
            s = _sigmoid(gg)
            dgb = (dh * uu * s * (1.0 + gg * (1.0 - s))).astype(BF16)
            dub = (dh * gg * s).astype(BF16)
            dgu_ref[:, c0:c0 + FF_CHUNK] = dgb
            dgu_ref[:, D_FF + c0:D_FF + c0 + FF_CHUNK] = dub
            acc = acc + _nt(dgb, wi_ref[:, c0:c0 + FF_CHUNK]) + _nt(dub, wi_ref[:, D_FF + c0:D_FF + c0 + FF_CHUNK])
        dx_ref[...] = ALPHA * dr + acc

    row = lambda n: pl.BlockSpec((tm, n), lambda i: (i, 0))
    return pl.pallas_call(
        body, name=name, grid=(T // tm,),
        in_specs=[row(D_MODEL), row(D_MODEL), row(D_FF), row(D_FF), _resident((D_MODEL, 2 * D_FF)),
                  _resident((D_FF, D_MODEL)), _resident((1, D_MODEL))],
        out_specs=[row(D_MODEL), row(2 * D_FF), row(D_MODEL), _resident((1, D_MODEL)), _resident((1, D_MODEL))],
        out_shape=[jax.ShapeDtypeStruct((T, D_MODEL), F32), jax.ShapeDtypeStruct((T, 2 * D_FF), BF16),
                   jax.ShapeDtypeStruct((T, D_MODEL), BF16), jax.ShapeDtypeStruct((1, D_MODEL), F32),
                   jax.ShapeDtypeStruct((1, D_MODEL), F32)],
        compiler_params=_cp(1),
    )(dxo, r, g, u, w_in, w_out, lg)


def _matmul_tn(a, b, name, tn=None):
    T, K = a.shape
    N = b.shape[1]
    tt = _tile(T, 1024)
    if tn is None:
        tn = N
        while K * tn * 4 > (6 << 20) and tn % 256 == 0:
            tn //= 2
    assert N % tn == 0

    def body(a_ref, b_ref, o_ref):
        @pl.when(pl.program_id(1) == 0)
        def _():
            o_ref[...] = jnp.zeros_like(o_ref)

        o_ref[...] += _tn(a_ref[...].astype(BF16), b_ref[...].astype(BF16))

    return pl.pallas_call(
        body, name=name, grid=(N // tn, T // tt),
        in_specs=[pl.BlockSpec((tt, K), lambda n, t: (t, 0)), pl.BlockSpec((tt, tn), lambda n, t: (t, n))],
        out_specs=pl.BlockSpec((K, tn), lambda n, t: (0, n)),
        out_shape=jax.ShapeDtypeStruct((K, N), F32),
        compiler_params=_cp(2),
    )(a, b)


def _matmul_nn(x, w, scale, out_dtype, name):
    T, K = x.shape
    N = w.shape[1]
    tm = _tile(T, 512)

    def body(x_ref, w_ref, s_ref, o_ref):
        o_ref[...] = (_nn(x_ref[...].astype(BF16), w_ref[...]) * s_ref[...]).astype(out_dtype)

    return pl.pallas_call(
        body, name=name, grid=(T // tm,),
        in_specs=[pl.BlockSpec((tm, K), lambda i: (i, 0)), _resident((K, N)), _resident((1, N))],
        out_specs=pl.BlockSpec((tm, N), lambda i: (i, 0)),
        out_shape=jax.ShapeDtypeStruct((T, N), out_dtype),
        compiler_params=_cp(1),
    )(x, w, scale)


def _log_sigmoid(z):
    return jnp.minimum(z, 0.0) - jnp.log1p(jnp.exp(-jnp.abs(z)))


def _tri(n, lower):
    r = lax.broadcasted_iota(jnp.int32, (n, n), 0)
    c = lax.broadcasted_iota(jnp.int32, (n, n), 1)
    return jnp.where((c <= r) if lower else (c >= r), 1.0, 0.0).astype(F32)


def _f32dot(a, b):
    return jnp.dot(a, b, preferred_element_type=F32, precision=lax.Precision.HIGHEST)


def _forget_cumsum(flog, col, bf):
    T = flog.shape[0]
    bt = _tile(T, 512)

    def body(f_ref, b_ref, c_ref, carry):
        @pl.when(pl.program_id(0) == 0)
        def _():
            carry[...] = jnp.zeros_like(carry)

        lf = _log_sigmoid(f_ref[...] + b_ref[...])
        c = _f32dot(_tri(bt, True), lf) + carry[...]
        c_ref[...] = c * LOG2E
        carry[...] = c[bt - 1:bt, :]

    return pl.pallas_call(
        body, name="forget_cumsum", grid=(T // bt,),
        in_specs=[pl.BlockSpec((bt, N_FLOG), lambda i: (i, col)), _resident((1, N_FLOG))],
        out_specs=pl.BlockSpec((bt, N_FLOG), lambda i: (i, 0)),
        out_shape=jax.ShapeDtypeStruct((T, N_FLOG), F32),
        scratch_shapes=[pltpu.VMEM((1, N_FLOG), F32)],
        compiler_params=_cp(1),
    )(flog, bf)


def _forget_bwd(dck, dcq, flog, col, bf):
    T = dcq.shape[0]
    bt = _tile(T, 512)
    nb = T // bt

    def body(k0_ref, k1_ref, k2_ref, k3_ref, dcq_ref, f_ref, b_ref, dz_ref, db_ref, carry):
        @pl.when(pl.program_id(0) == 0)
        def _():
            carry[...] = jnp.zeros_like(carry)
            db_ref[...] = jnp.zeros_like(db_ref)

        dc = ((k0_ref[...] + k1_ref[...]) + (k2_ref[...] + k3_ref[...])) + dcq_ref[...]
        dlf = _f32dot(_tri(bt, False), dc) + carry[...]
        carry[...] = dlf[0:1, :]
        z = f_ref[...] + b_ref[...]
        dz = dlf * _sigmoid(-z)
        dz_ref[...] = dz.astype(BF16)
        db_ref[...] += _rowsum(dz)

    slab = lambda j: pl.BlockSpec((None, bt, N_FLOG), lambda i: (j, nb - 1 - i, 0))
    return pl.pallas_call(
        body, name="forget_bwd", grid=(nb,),
        in_specs=[slab(0), slab(1), slab(2), slab(3),
                  pl.BlockSpec((bt, N_FLOG), lambda i: (nb - 1 - i, 0)),
                  pl.BlockSpec((bt, N_FLOG), lambda i: (nb - 1 - i, col)), _resident((1, N_FLOG))],
        out_specs=[pl.BlockSpec((bt, N_FLOG), lambda i: (nb - 1 - i, 0)), _resident((1, N_FLOG))],
        out_shape=[jax.ShapeDtypeStruct((T, N_FLOG), BF16), jax.ShapeDtypeStruct((1, N_FLOG), F32)],
        scratch_shapes=[pltpu.VMEM((1, N_FLOG), F32)],
        compiler_params=_cp(1),
    )(dck, dck, dck, dck, dcq, flog, bf)


def _head_masks():
    lane = lax.broadcasted_iota(jnp.int32, (1, LANE), 1)
    return lane < HEAD_DIM


def _split_heads(x2, is_a):
    zero = jnp.zeros_like(x2)
    return jnp.where(is_a, x2, zero), jnp.where(is_a, zero, x2)


def _attn_fwd(qkv, vt, cb):
    T = qkv.shape[0]
    tq = _tile(T, 512)
    tk = tq
    nq = T // tq
    npair = N_HEADS // 2

    def body(q_ref, k_ref, vt_ref, cb_ref, o_ref, al_ref, m_s, l_s, acc_s):
        i = pl.program_id(1)
        qs = _split_heads(q_ref[...], _head_masks())
        m_s[...] = jnp.full_like(m_s, NEG_INF)
        l_s[...] = jnp.zeros_like(l_s)
        acc_s[...] = jnp.zeros_like(acc_s)

        def scores_at(kk):
            k2 = k_ref[pl.ds(pl.multiple_of(kk * tk, tk), tk), :]
            return tuple(_nt(k2, qs[h]) for h in range(2))

        def consume(kk, scores, masked):
            k0 = pl.multiple_of(kk * tk, tk)
            v2t = vt_ref[:, pl.ds(k0, tk)]
            csb = cb_ref[pl.ds(k0, tk), :]
            for h in range(2):
                cs = csb[:, h * HEAD_DIM:h * HEAD_DIM + 1]
                zt = scores[h] - cs
                if masked:
                    rr = lax.broadcasted_iota(jnp.int32, (tk, tq), 0)
                    cc = lax.broadcasted_iota(jnp.int32, (tk, tq), 1)
                    zt = jnp.where(cc >= rr, zt, NEG_INF)
                m_old = m_s[h]
                m_new = jnp.maximum(m_old, jnp.max(zt, axis=0, keepdims=True))
                p = jnp.exp2(zt - m_new)
                a = jnp.exp2(m_old - m_new)
                l_s[h] = a * l_s[h] + jnp.sum(p, axis=0, keepdims=True)
                acc_s[h] = a * acc_s[h] + _nn(v2t, p.astype(BF16))
                m_s[h] = m_new

        def two_tiles(kk, second_masked):
            first, second = scores_at(kk), scores_at(kk + 1)
            consume(kk, first, False)
            consume(kk + 1, second, second_masked)

        def loop_body(t, carry):
            two_tiles(2 * t, False)
            return carry

        lax.fori_loop(0, i // 2, loop_body, 0)

        @pl.when(i % 2 == 1)
        def _():
            two_tiles(i - 1, True)

        @pl.when(i % 2 == 0)
        def _():
            consume(i, scores_at(i), True)

        outs = []
        for h in range(2):
            l = l_s[h]
            outs.append(acc_s[h] * (1.0 / l))
            al_ref[0, h:h + 1, :] = -(m_s[h] + jnp.log2(l))
        al_ref[0, 2:8, :] = jnp.zeros((6, tq), F32)
        dim = lax.broadcasted_iota(jnp.int32, (LANE, 1), 0)
        o_ref[...] = jnp.where(dim < HEAD_DIM, outs[0], outs[1]).T

    rowl = pl.BlockSpec((1, 8, tq), lambda j, i: (j, 0, i))
    return pl.pallas_call(
        body, name="attn_fwd", grid=(npair, nq),
        in_specs=[pl.BlockSpec((tq, LANE), lambda j, i: (i, j)),
                  pl.BlockSpec((T, LANE), lambda j, i: (0, npair + j), pipeline_mode=pl.Buffered(1)),
                  pl.BlockSpec((LANE, T), lambda j, i: (j, 0), pipeline_mode=pl.Buffered(1)),
                  pl.BlockSpec((T, LANE), lambda j, i: (0, j), pipeline_mode=pl.Buffered(1))],
        out_specs=[pl.BlockSpec((tq, LANE), lambda j, i: (i, j)), rowl],
        out_shape=[jax.ShapeDtypeStruct((T, D_ATTN), F32), jax.ShapeDtypeStruct((npair, 8, T), F32)],
        scratch_shapes=[pltpu.VMEM((2, 1, tq), F32), pltpu.VMEM((2, 1, tq), F32), pltpu.VMEM((2, LANE, tq), F32)],
        compiler_params=_cp(2),
    )(qkv, qkv, vt, cb)


def _attn_bwd(qkv, dob, cb, alrow, dlrow):
    T = qkv.shape[0]
    tq = _tile(T, 512)
    tk = tq
    nq = T // tq
    npair = N_HEADS // 2

    def body(q_ref, k_ref, v_ref, do_ref, cb_ref, al_ref, dl_ref, dq_ref, dk_ref, dv_ref, dc_ref, dcq_ref,
             dk_s, dv_s, dc_s):
        kj = pl.program_id(1)
        is_a = _head_masks()
        ks = _split_heads(k_ref[...], is_a)
        vs = _split_heads(v_ref[...], is_a)
        cs = (cb_ref[:, 0:1], cb_ref[:, HEAD_DIM:HEAD_DIM + 1])

        @pl.when(kj == 0)
        def _():
            dq_ref[...] = jnp.zeros_like(dq_ref)
            dcq_ref[...] = jnp.zeros_like(dcq_ref)

        dk_s[...] = jnp.zeros_like(dk_s)
        dv_s[...] = jnp.zeros_like(dv_s)
        dc_s[...] = jnp.zeros_like(dc_s)

        def step(qi, masked):
            q0 = pl.multiple_of(qi * tq, tq)
            q2 = q_ref[pl.ds(q0, tq), :]
            do2 = do_ref[pl.ds(q0, tq), :]
            for h in range(2):
                alr = al_ref[0, h:h + 1, pl.ds(q0, tq)]
                dlr = dl_ref[0, h:h + 1, pl.ds(q0, tq)]
                zt = _nt(ks[h], q2) + (alr - cs[h])
                if masked:
                    rr = lax.broadcasted_iota(jnp.int32, (tk, tq), 0)
                    cc = lax.broadcasted_iota(jnp.int32, (tk, tq), 1)
                    zt = jnp.where(cc >= rr, zt, NEG_INF)
                pt = jnp.exp2(zt)
                dst = pt * (_nt(vs[h], do2) - dlr)
                pb = pt.astype(BF16)
                dsb = dst.astype(BF16)
                dv_s[h] += _nn(pb, do2)
                dk_s[h] += _nn(dsb, q2)
                dc_s[h] += jnp.sum(dst, axis=-1, keepdims=True)
                dcq_ref[0, h:h + 1, pl.ds(q0, tq)] += jnp.sum(dst, axis=0, keepdims=True)
                dq_ref[pl.ds(q0, tq), :] += _tn(dsb, ks[h])

        step(kj, True)

        def loop_body(qi, carry):
            step(qi, False)
            return carry

        lax.fori_loop(kj + 1, nq, loop_body, 0)
        dk_ref[...] = (jnp.where(is_a, dk_s[0], dk_s[1]) * (1.0 / LOG2E)).astype(BF16)
        dv_ref[...] = jnp.where(is_a, dv_s[0], dv_s[1]).astype(BF16)
        lane = lax.broadcasted_iota(jnp.int32, (1, LANE), 1)
        head = 2 * pl.program_id(0)
        dc_ref[...] = jnp.where(lane == head, -dc_s[0], jnp.where(lane == head + 1, -dc_s[1], 0.0))

        @pl.when(kj == nq - 1)
        def _():
            dq_ref[...] = dq_ref[...] * Q_SCALE

    full = lambda col: pl.BlockSpec((T, LANE), lambda j, kj: (0, col(j)), pipeline_mode=pl.Buffered(1))
    tile = lambda col: pl.BlockSpec((tk, LANE), lambda j, kj: (kj, col(j)))
    rowl = pl.BlockSpec((1, 8, T), lambda j, kj: (j, 0, 0))
    return pl.pallas_call(
        body, name="attn_bwd", grid=(npair, nq),
        in_specs=[full(lambda j: j), tile(lambda j: npair + j), tile(lambda j: 2 * npair + j), full(lambda j: j),
                  tile(lambda j: j), rowl, rowl],
        out_specs=[pl.BlockSpec((T, LANE), lambda j, kj: (0, j)), tile(lambda j: j), tile(lambda j: j),
                   pl.BlockSpec((None, tk, LANE), lambda j, kj: (j, kj, 0)), rowl],
        out_shape=[jax.ShapeDtypeStruct((T, D_ATTN), F32), jax.ShapeDtypeStruct((T, D_ATTN), BF16),
                   jax.ShapeDtypeStruct((T, D_ATTN), BF16), jax.ShapeDtypeStruct((npair, T, LANE), F32),
                   jax.ShapeDtypeStruct((npair, 8, T), F32)],
        scratch_shapes=[pltpu.VMEM((2, tk, LANE), F32), pltpu.VMEM((2, tk, LANE), F32), pltpu.VMEM((2, tk, 1), F32)],
        compiler_params=_cp(2),
    )(qkv, qkv, qkv, dob, cb, alrow, dlrow)


HALO = 8


def _shift_rows(cur, other, k, tm, down):
    row = lax.broadcasted_iota(jnp.int32, (tm, 1), 0)
    reps = tm // HALO
    if down:
        rolled = pltpu.roll(cur, k, 0)
        fill = jnp.tile(pltpu.roll(other, k, 0), (reps, 1))
        return jnp.where(row < k, fill, rolled)
    rolled = pltpu.roll(cur, tm - k, 0)
    fill = jnp.tile(pltpu.roll(other, HALO - k, 0), (reps, 1))
    return jnp.where(row >= tm - k, fill, rolled)


def _conv_fwd(c, hh, c_prev, hh_prev, w_ref, first, tm):
    u = c * hh
    u_prev = jnp.where(first, 0.0, c_prev * hh_prev)
    u1 = _shift_rows(u, u_prev, 1, tm, True)
    u2 = _shift_rows(u, u_prev, 2, tm, True)
    y = w_ref[0:1, :] * u2 + w_ref[1:2, :] * u1 + w_ref[2:3, :] * u
    return u, u1, u2, y


def _rms(x, g):
    rs = lax.rsqrt(jnp.mean(x * x, axis=-1, keepdims=True) + RMS_EPS)
    return x * rs * g, rs


def _mixer_tail_fwd(o, bchf, conv_w, g_attn, g_conv, w_mo, x1, lg, lb):
    T = o.shape[0]
    tm = _tile(T, 512)
    hb = tm // HALO

    def body(o_ref, b_ref, c_ref, h_ref, cp_ref, hp_ref, w_ref, ga_ref, gc_ref, wmo_ref, x1_ref, lg_ref, lb_ref,
             x2_ref, r2_ref, mg_ref):
        first = pl.program_id(0) == 0
        _, _, _, y = _conv_fwd(c_ref[...], h_ref[...], cp_ref[...], hp_ref[...], w_ref, first, tm)
        na, _ = _rms(o_ref[...], ga_ref[...])
        nc, _ = _rms(b_ref[...] * y, gc_ref[...])
        nab = na.astype(BF16)
        ncb = nc.astype(BF16)
        mg_ref[:, 0:D_ATTN] = nab
        mg_ref[:, D_ATTN:] = ncb
        r2 = ALPHA * x1_ref[...] + _nn(nab, wmo_ref[0:D_ATTN, :]) + _nn(ncb, wmo_ref[D_ATTN:, :])
        r2_ref[...] = r2
        xhat, _ = _ln_stats(r2)
        x2_ref[...] = xhat * lg_ref[...] + lb_ref[...]

    row = lambda n, col=0: pl.BlockSpec((tm, n), lambda i: (i, col))
    prev = lambda col: pl.BlockSpec((HALO, D_CONV), lambda i: (jnp.maximum(i * hb - 1, 0), col))
    return pl.pallas_call(
        body, name="mixer_tail_fwd", grid=(T // tm,),
        in_specs=[row(D_ATTN), row(D_CONV, 0), row(D_CONV, 1), row(D_CONV, 2), prev(1), prev(2),
                  _resident((3, D_CONV)), _resident((1, D_ATTN)), _resident((1, D_CONV)),
                  _resident((D_MODEL, D_MODEL)), row(D_MODEL), _resident((1, D_MODEL)), _resident((1, D_MODEL))],
        out_specs=[row(D_MODEL), row(D_MODEL), row(D_MODEL)],
        out_shape=[jax.ShapeDtypeStruct((T, D_MODEL), F32), jax.ShapeDtypeStruct((T, D_MODEL), F32),
                   jax.ShapeDtypeStruct((T, D_MODEL), BF16)],
        compiler_params=_cp(1),
    )(o, bchf, bchf, bchf, bchf, bchf, conv_w, g_attn, g_conv, w_mo, x1, lg, lb)


def _head_sum_rows():
    row = lax.broadcasted_iota(jnp.int32, (4 * 8, D_ATTN), 0)
    head = lax.broadcasted_iota(jnp.int32, (4 * 8, D_ATTN), 1) // HEAD_DIM
    return jnp.where((row % 8 < 2) & (2 * (row // 8) + row % 8 == head), 1.0, 0.0).astype(F32)


def _mixer_tail_bwd(dx2, r2, lg, w_mo, o, bchf, conv_w, g_attn, g_conv):
    T = o.shape[0]
    tm = _tile(T, 256)
    hb = tm // HALO

    def body(dx2_ref, r2_ref, lg_ref, wmo_ref, o_ref, b_ref, c_ref, h_ref, cp_ref, hp_ref, w_ref, ga_ref, gc_ref,
             dx1_ref, dr_ref, do_ref, dl_ref, dco_ref, dlg_ref, dlb_ref, dga_ref, dgc_ref):
        i = pl.program_id(0)

        @pl.when(i == 0)
        def _():
            for ref in (dlg_ref, dlb_ref, dga_ref, dgc_ref):
                ref[...] = jnp.zeros_like(ref)

        dy = dx2_ref[...]
        xhat, rstd = _ln_stats(r2_ref[...])
        dr = _ln_bwd(dy, xhat, rstd, lg_ref[...])
        dlg_ref[...] += _rowsum(dy * xhat)
        dlb_ref[...] += _rowsum(dy)
        dx1_ref[...] = ALPHA * dr
        drb = dr.astype(BF16)
        dr_ref[...] = drb
        dna = _nt(drb, wmo_ref[0:D_ATTN, :])
        dnc = _nt(drb, wmo_ref[D_ATTN:, :])

        def rms_bwd(x, g, dn):
            rs = lax.rsqrt(jnp.mean(x * x, axis=-1, keepdims=True) + RMS_EPS)
            dng = dn * g
            dx = rs * dng - x * (rs * rs * rs) * jnp.mean(dng * x, axis=-1, keepdims=True)
            return dx, _rowsum(dn * x * rs)

        oo = o_ref[...]
        do, dga = rms_bwd(oo, ga_ref[...], dna)
        dga_ref[...] += dga
        do_ref[...] = do.astype(BF16)
        dl_ref[...] = lax.dot_general(_head_sum_rows(), do * oo, (((1,), (1,)), ((), ())),
                                      preferred_element_type=F32, precision=lax.Precision.HIGHEST)
        _, _, _, y = _conv_fwd(c_ref[...], h_ref[...], cp_ref[...], hp_ref[...], w_ref, i == 0, tm)
        dco, dgc = rms_bwd(b_ref[...] * y, gc_ref[...], dnc)
        dgc_ref[...] += dgc
        dco_ref[...] = dco

    row = lambda n, col=0: pl.BlockSpec((tm, n), lambda i: (i, col))
    prev = lambda col: pl.BlockSpec((HALO, D_CONV), lambda i: (jnp.maximum(i * hb - 1, 0), col))
    vec = lambda n: _resident((1, n))
    return pl.pallas_call(
        body, name="mixer_tail_bwd", grid=(T // tm,),
        in_specs=[row(D_MODEL), row(D_MODEL), vec(D_MODEL), _resident((D_MODEL, D_MODEL)), row(D_ATTN),
                  row(D_CONV, 0), row(D_CONV, 1), row(D_CONV, 2), prev(1), prev(2), _resident((3, D_CONV)),
                  vec(D_ATTN), vec(D_CONV)],
        out_specs=[row(D_MODEL), row(D_MODEL), row(D_ATTN), pl.BlockSpec((4 * 8, tm), lambda i: (0, i)), row(D_CONV),
                   vec(D_MODEL), vec(D_MODEL), vec(D_ATTN), vec(D_CONV)],
        out_shape=[jax.ShapeDtypeStruct((T, D_MODEL), F32), jax.ShapeDtypeStruct((T, D_MODEL), BF16),
                   jax.ShapeDtypeStruct((T, D_ATTN), BF16), jax.ShapeDtypeStruct((4 * 8, T), F32),
                   jax.ShapeDtypeStruct((T, D_CONV), F32), jax.ShapeDtypeStruct((1, D_MODEL), F32),
                   jax.ShapeDtypeStruct((1, D_MODEL), F32), jax.ShapeDtypeStruct((1, D_ATTN), F32),
                   jax.ShapeDtypeStruct((1, D_CONV), F32)],
        compiler_params=_cp(1),
    )(dx2, r2, lg, w_mo, o, bchf, bchf, bchf, bchf, bchf, conv_w, g_attn, g_conv)


def _conv_bwd(dco, bchf, conv_w):
    T = dco.shape[0]
    tm = _tile(T, 512)
    hb = tm // HALO
    nt = T // tm

    def body(dco_ref, dcon_ref, b_ref, bn_ref, c_ref, h_ref, cp_ref, hp_ref, w_ref, dbch_ref, dw_ref):
        i = pl.program_id(0)

        @pl.when(i == 0)
        def _():
            dw_ref[...] = jnp.zeros_like(dw_ref)

        cc = c_ref[...]
        hh = h_ref[...]
        u, u1, u2, y = _conv_fwd(cc, hh, cp_ref[...], hp_ref[...], w_ref, i == 0, tm)
        dco = dco_ref[...]
        bb = b_ref[...]
        dyc = dco * bb
        dy_next = jnp.where(i == nt - 1, 0.0, dcon_ref[...] * bn_ref[...])
        d1 = _shift_rows(dyc, dy_next, 1, tm, False)
        d2 = _shift_rows(dyc, dy_next, 2, tm, False)
        du = w_ref[2:3, :] * dyc + w_ref[1:2, :] * d1 + w_ref[0:1, :] * d2
        dbch_ref[:, 0:D_CONV] = (dco * y).astype(BF16)
        dbch_ref[:, D_CONV:2 * D_CONV] = (du * hh).astype(BF16)
        dbch_ref[:, 2 * D_CONV:] = (du * cc).astype(BF16)
        dw_ref[0:1, :] += _rowsum(dyc * u2)
        dw_ref[1:2, :] += _rowsum(dyc * u1)
        dw_ref[2:3, :] += _rowsum(dyc * u)

    row = lambda n, col=0: pl.BlockSpec((tm, n), lambda i: (i, col))
    prev = lambda col: pl.BlockSpec((HALO, D_CONV), lambda i: (jnp.maximum(i * hb - 1, 0), col))
    nxt = lambda col: pl.BlockSpec((HALO, D_CONV), lambda i: (jnp.minimum((i + 1) * hb, T // HALO - 1), col))
    return pl.pallas_call(
        body, name="conv_bwd", grid=(nt,),
        in_specs=[row(D_CONV), nxt(0), row(D_CONV, 0), nxt(0), row(D_CONV, 1), row(D_CONV, 2), prev(1), prev(2),
                  _resident((3, D_CONV))],
        out_specs=[row(3 * D_CONV), _resident((8, D_CONV))],
        out_shape=[jax.ShapeDtypeStruct((T, 3 * D_CONV), BF16), jax.ShapeDtypeStruct((8, D_CONV), F32)],
        compiler_params=_cp(1),
    )(dco, dco, bchf, bchf, bchf, bchf, bchf, bchf, conv_w)


def _mixer_in_bwd(dx1a, dq, dk, dv, dbch, dfl, w_qkv, w_bch, w_f):
    T = dx1a.shape[0]
    tm = _tile(T, 512)

    def body(a_ref, dq_ref, dk_ref, dv_ref, db_ref, df_ref, wq_ref, wb_ref, wf_ref, o_ref):
        acc = a_ref[...] + _nt(db_ref[...], wb_ref[...]) + _nt(df_ref[...], wf_ref[...])
        for n, ref in enumerate((dq_ref, dk_ref, dv_ref)):
            acc = acc + _nt(ref[...].astype(BF16), wq_ref[:, n * D_ATTN:(n + 1) * D_ATTN])
        o_ref[...] = acc

    row = lambda n: pl.BlockSpec((tm, n), lambda i: (i, 0))
    return pl.pallas_call(
        body, name="mixer_in_bwd", grid=(T // tm,),
        in_specs=[row(D_MODEL), row(D_ATTN), row(D_ATTN), row(D_ATTN), row(3 * D_CONV), row(N_FLOG),
                  _resident((D_MODEL, 3 * D_ATTN)), _resident((D_MODEL, 3 * D_CONV)), _resident((D_MODEL, N_FLOG))],
        out_specs=row(D_MODEL),
        out_shape=jax.ShapeDtypeStruct((T, D_MODEL), F32),
        compiler_params=_cp(1),
    )(dx1a, dq, dk, dv, dbch, dfl, w_qkv, w_bch, w_f)


def _ple_loss(x3, p, tgt, w_g, w_p, b_g, lg, lb):
    T = x3.shape[0]
    tm = _tile(T, 512)

    def body(x_ref, p_ref, t_ref, wg_ref, wp_ref, bg_ref, lg_ref, lb_ref,
             dx_ref, de_ref, dz_ref, loss_ref, dlg_ref, dlb_ref, dbg_ref):
        @pl.when(pl.program_id(0) == 0)
        def _():
            for ref in (loss_ref, dlg_ref, dlb_ref, dbg_ref):
                ref[...] = jnp.zeros_like(ref)

        xf = x_ref[...]
        gate = _sigmoid(_nn(xf.astype(BF16), wg_ref[...]) + bg_ref[...])
        e = _nn(p_ref[...].astype(BF16), wp_ref[...])
        xhat, rstd = _ln_stats(ALPHA * xf + gate * e)
        err = xhat * lg_ref[...] + lb_ref[...] - t_ref[...]
        sq = jnp.sum(_rowsum(err * err), axis=-1, keepdims=True)
        loss_ref[...] += jnp.broadcast_to(sq * (0.5 / D_MODEL), loss_ref.shape)
        dy = err * (1.0 / D_MODEL)
        dr = _ln_bwd(dy, xhat, rstd, lg_ref[...])
        dlg_ref[...] += _rowsum(dy * xhat)
        dlb_ref[...] += _rowsum(dy)
        de_ref[...] = (dr * gate).astype(BF16)
        dz = dr * e * gate * (1.0 - gate)
        dbg_ref[...] += _rowsum(dz)
        dzb = dz.astype(BF16)
        dz_ref[...] = dzb
        dx_ref[...] = ALPHA * dr + _nt(dzb, wg_ref[...])

    row = lambda n: pl.BlockSpec((tm, n), lambda i: (i, 0))
    vec = lambda n: _resident((1, n))
    return pl.pallas_call(
        body, name="ple_loss", grid=(T // tm,),
        in_specs=[row(D_MODEL), row(PLE_DIM), row(D_MODEL), _resident((D_MODEL, D_MODEL)),
                  _resident((PLE_DIM, D_MODEL)), vec(D_MODEL), vec(D_MODEL), vec(D_MODEL)],
        out_specs=[row(D_MODEL), row(D_MODEL), row(D_MODEL), vec(LANE), vec(D_MODEL), vec(D_MODEL), vec(D_MODEL)],
        out_shape=[jax.ShapeDtypeStruct((T, D_MODEL), F32), jax.ShapeDtypeStruct((T, D_MODEL), BF16),
                   jax.ShapeDtypeStruct((T, D_MODEL), BF16), jax.ShapeDtypeStruct((1, LANE), F32),
                   jax.ShapeDtypeStruct((1, D_MODEL), F32), jax.ShapeDtypeStruct((1, D_MODEL), F32),
                   jax.ShapeDtypeStruct((1, D_MODEL), F32)],
        compiler_params=_cp(1),
    )(x3, p, tgt, w_g, w_p, b_g, lg, lb)


def _lane_layout(v8):
    return jnp.repeat(v8, HEAD_DIM, axis=1)


def _row_layout(v8):
    t = v8.shape[0]
    return jnp.pad(v8.T.reshape(N_HEADS // 2, 2, t), ((0, 0), (0, 6), (0, 0)))


def _from_lane_layout(vl):
    return vl[:, ::HEAD_DIM]


def _from_row_layout(vr):
    return vr[:, :2, :].reshape(N_HEADS, -1).T


def _local_step(x, p, tgt, w):
    bf = lambda a: a.astype(BF16)
    w1i, w1o, w2i, w2o = bf(w["ffn1_w_in"]), bf(w["ffn1_w_out"]), bf(w["ffn2_w_in"]), bf(w["ffn2_w_out"])
    wmi = w["w_mix_in"]
    o_f = 3 * D_ATTN
    o_b = o_f + N_HEADS
    w_qkv = bf(wmi[:, :o_f])
    w_f = bf(jnp.pad(wmi[:, o_f:o_b], ((0, 0), (0, N_FLOG - N_HEADS))))
    w_bch = bf(wmi[:, o_b:])
    w_bchf = jnp.concatenate([w_bch, w_f], axis=1)
    w_mo, w_g, w_p = bf(w["w_mix_out"]), bf(w["w_ple_gate"]), bf(w["w_ple"])
    b_f = jnp.pad(w["b_forget"], ((0, 0), (0, N_FLOG - N_HEADS)))

    x1, r1, g1, u1, h1 = _ffn_fwd(x, w1i, w1o, w["ln1_g"], w["ln1_b"], "ffn1_fwd")
    q_scale = jnp.concatenate([jnp.full((1, D_ATTN), Q_SCALE * LOG2E, F32), jnp.ones((1, 2 * D_ATTN), F32)], axis=1)
    qkv = _matmul_nn(x1, w_qkv, q_scale, BF16, "proj_qkv")
    bchf = _matmul_nn(x1, w_bchf, jnp.ones((1, 3 * D_CONV + N_FLOG), F32), F32, "proj_bchf")
    fcol = 3 * D_CONV // N_FLOG
    c = _forget_cumsum(bchf, fcol, b_f)
    c8 = c[:, :N_HEADS]
    cb = _lane_layout(c8)
    o, alrow = _attn_fwd(qkv, qkv[:, 2 * D_ATTN:].T, cb)
    x2, r2, merged = _mixer_tail_fwd(o, bchf, w["conv_w"], w["g_attn"], w["g_conv"], w_mo, x1, w["ln2_g"], w["ln2_b"])
    x3, r3, g2, u2, h2 = _ffn_fwd(x2, w2i, w2o, w["ln3_g"], w["ln3_b"], "ffn2_fwd")

    grads = {}
    dx3, de, dz, loss, grads["ln4_g"], grads["ln4_b"], grads["b_ple_gate"] = _ple_loss(
        x3, p, tgt, w_g, w_p, w["b_ple_gate"], w["ln4_g"], w["ln4_b"])
    grads["w_ple"] = _matmul_tn(p, de, "dw_ple")
    grads["w_ple_gate"] = _matmul_tn(x3, dz, "dw_ple_gate")

    dx2, dgu2, df2, grads["ln3_g"], grads["ln3_b"] = _ffn_bwd(dx3, r3, g2, u2, w2i, w2o, w["ln3_g"], "ffn2_bwd")
    grads["ffn2_w_in"] = _matmul_tn(x2, dgu2, "dw_ffn2_in")
    grads["ffn2_w_out"] = _matmul_tn(h2, df2, "dw_ffn2_out")

    (dx1a, dr2, dob, delta, dco, grads["ln2_g"], grads["ln2_b"], grads["g_attn"], grads["g_conv"]) = _mixer_tail_bwd(
        dx2, r2, w["ln2_g"], w_mo, o, bchf, w["conv_w"], w["g_attn"], w["g_conv"])
    grads["w_mix_out"] = _matmul_tn(merged, dr2, "dw_mix_out")
    dbch, dcw = _conv_bwd(dco, bchf, w["conv_w"])
    grads["conv_w"] = dcw[:3]
    dq, dk, dv, dck, dcq = _attn_bwd(qkv, dob, cb, alrow, delta.reshape(N_HEADS // 2, 8, -1))
    dcq_lanes = jnp.pad(_from_row_layout(dcq), ((0, 0), (0, N_FLOG - N_HEADS)))
    dfl, dbf = _forget_bwd(dck, dcq_lanes, bchf, fcol, b_f)
    grads["b_forget"] = dbf[:, :N_HEADS]
    dx1 = _mixer_in_bwd(dx1a, dq, dk, dv, dbch, dfl, w_qkv, w_bch, w_f)
    grads["w_mix_in"] = jnp.concatenate(
        [_matmul_tn(x1, dq, "dw_q"), _matmul_tn(x1, dk, "dw_k"), _matmul_tn(x1, dv, "dw_v"),
         _matmul_tn(x1, dfl, "dw_flog")[:, :N_HEADS], _matmul_tn(x1, dbch, "dw_bch")], axis=1)

    dx0, dgu1, df1, grads["ln1_g"], grads["ln1_b"] = _ffn_bwd(dx1, r1, g1, u1, w1i, w1o, w["ln1_g"], "ffn1_bwd")
    grads["ffn1_w_in"] = _matmul_tn(x, dgu1, "dw_ffn1_in")
    grads["ffn1_w_out"] = _matmul_tn(h1, df1, "dw_ffn1_out")
    return loss, dx0, grads


WEIGHTS = ["ffn1_w_in", "ffn1_w_out", "ln1_g", "ln1_b", "w_mix_in", "b_forget", "conv_w", "g_attn", "g_conv",
           "w_mix_out", "ln2_g", "ln2_b", "ffn2_w_in", "ffn2_w_out", "ln3_g", "ln3_b", "w_ple", "w_ple_gate",
           "b_ple_gate", "ln4_g", "ln4_b"]
LAYOUT = {
    "ffn1_w_in": ((D_MODEL, 2 * D_FF), 1), "ffn1_w_out": ((D_FF, D_MODEL), 0),
    "w_mix_in": ((D_MODEL, 3 * D_ATTN + N_HEADS + 3 * D_CONV), 1), "conv_w": ((3, D_CONV), 1),
    "w_mix_out": ((D_MODEL, D_MODEL), 0), "ffn2_w_in": ((D_MODEL, 2 * D_FF), 1), "ffn2_w_out": ((D_FF, D_MODEL), 0),
    "w_ple": ((PLE_DIM, D_MODEL), 1), "w_ple_gate": ((D_MODEL, D_MODEL), 0),
    "ln1_g": ((1, D_MODEL), None), "ln1_b": ((1, D_MODEL), None), "b_forget": ((1, N_HEADS), None),
    "g_attn": ((1, D_ATTN), None), "g_conv": ((1, D_CONV), None), "ln2_g": ((1, D_MODEL), None),
    "ln2_b": ((1, D_MODEL), None), "ln3_g": ((1, D_MODEL), None), "ln3_b": ((1, D_MODEL), None),
    "b_ple_gate": ((1, D_MODEL), None), "ln4_g": ((1, D_MODEL), None), "ln4_b": ((1, D_MODEL), None),
}
ROW = 1024
ROW_ALIGN = 64


def _shard_shape(name):
    shape, axis = LAYOUT[name]
    if axis is None:
        return shape
    return tuple(s // N_CHIPS if a == axis else s for a, s in enumerate(shape))


def _piece_len(name):
    n = math.prod(_shard_shape(name))
    return -(-n // ROW) * ROW


PAYLOAD_LEN = sum(_piece_len(n) for n in WEIGHTS)
PAYLOAD_ROWS = -(-PAYLOAD_LEN // (ROW * ROW_ALIGN)) * ROW_ALIGN


def _pack(pieces):
    lead = pieces[WEIGHTS[0]].shape[:-len(_shard_shape(WEIGHTS[0]))]
    flat = []
    for n in WEIGHTS:
        a = pieces[n].reshape(lead + (-1,))
        flat.append(jnp.pad(a, [(0, 0)] * len(lead) + [(0, _piece_len(n) - a.shape[-1])]))
    tail = PAYLOAD_ROWS * ROW - PAYLOAD_LEN
    if tail:
        flat.append(jnp.zeros(lead + (tail,), flat[0].dtype))
    return jnp.concatenate(flat, axis=-1).reshape(lead + (PAYLOAD_ROWS, ROW))


def _unpack(payload):
    lead = payload.shape[:-2]
    flat = payload.reshape(lead + (-1,))
    out, off = {}, 0
    for n in WEIGHTS:
        ss = _shard_shape(n)
        out[n] = flat[..., off:off + math.prod(ss)].reshape(lead + ss)
        off += _piece_len(n)
    return out


def _split_chips(name, full):
    shape, axis = LAYOUT[name]
    if axis is None:
        return jnp.broadcast_to(full[None], (N_CHIPS,) + shape)
    if axis == 0:
        return full.reshape((N_CHIPS, shape[0] // N_CHIPS) + shape[1:])
    return jnp.moveaxis(full.reshape(shape[:1] + (N_CHIPS, shape[1] // N_CHIPS)), 1, 0)


def _join_chips(name, parts):
    shape, axis = LAYOUT[name]
    if axis is None:
        return parts[0]
    if axis == 0:
        return parts.reshape(shape)
    return jnp.moveaxis(parts, 0, 1).reshape(shape)


def _place():
    x, y, c = lax.axis_index("x"), lax.axis_index("y"), lax.axis_index("c")
    others = [(1 - x, y), (x, 1 - y), (1 - x, 1 - y)]
    return x, y, c, others


def _all_gather_weights(wp):
    half = wp.shape[1:]

    def body(w_ref, out_ref, send_sems, recv_sems):
        x, y, c, others = _place()
        s = 2 * x + y

        def copy(k, chip, half_idx, to):
            slot = out_ref.at[2 * chip[0] + chip[1], half_idx]
            return pltpu.make_async_remote_copy(src_ref=slot, dst_ref=slot, send_sem=send_sems.at[k],
                                                recv_sem=recv_sems.at[k], device_id=to, device_id_type=MESH)

        first = []
        for j, chip in enumerate(others):
            slot = out_ref.at[s, c]
            first.append(pltpu.make_async_remote_copy(src_ref=w_ref.at[c], dst_ref=slot, send_sem=send_sems.at[j],
                                                      recv_sem=recv_sems.at[j], device_id=(*chip, c),
                                                      device_id_type=MESH))
        for cp in first:
            cp.start()
        passed = [copy(3 + j, chip, c, (x, y, 1 - c)) for j, chip in enumerate(others)]
        for j, chip in enumerate(others):
            copy(j, chip, c, (x, y, c)).wait_recv()
            passed[j].start()
        for j, chip in enumerate(others):
            copy(3 + j, chip, 1 - c, (x, y, c)).wait_recv()
        for cp in first + passed:
            cp.wait_send()

    any_spec = pl.BlockSpec(memory_space=pl.ANY)
    return pl.pallas_call(
        body, name="all_gather_weights",
        out_shape=jax.ShapeDtypeStruct((N_CHIPS, 2) + half, wp.dtype),
        in_specs=[any_spec], out_specs=any_spec,
        scratch_shapes=[pltpu.SemaphoreType.DMA((6,)), pltpu.SemaphoreType.DMA((6,))],
    )(wp)


def _swap_halves(g):
    rh = g.shape[2]

    def body(g_ref, out_ref, send_sem, recv_sem):
        x, y, c, _ = _place()
        cp = pltpu.make_async_remote_copy(src_ref=g_ref.at[:, 1 - c], dst_ref=out_ref, send_sem=send_sem,
                                          recv_sem=recv_sem, device_id=(x, y, 1 - c), device_id_type=MESH)
        cp.start()
        cp.wait()

    any_spec = pl.BlockSpec(memory_space=pl.ANY)
    return pl.pallas_call(
        body, name="grad_swap_halves",
        out_shape=jax.ShapeDtypeStruct((N_CHIPS, rh, ROW), g.dtype),
        in_specs=[any_spec], out_specs=any_spec,
        scratch_shapes=[pltpu.SemaphoreType.DMA, pltpu.SemaphoreType.DMA],
    )(g)


def _exchange_chips(pp):
    def body(p_ref, out_ref, send_sems, recv_sems):
        x, y, c, others = _place()
        s = 2 * x + y
        sends = []
        for j, chip in enumerate(others):
            sends.append(pltpu.make_async_remote_copy(
                src_ref=p_ref.at[2 * chip[0] + chip[1]], dst_ref=out_ref.at[s], send_sem=send_sems.at[j],
                recv_sem=recv_sems.at[j], device_id=(*chip, c), device_id_type=MESH))
        for cp in sends:
            cp.start()
        for j, chip in enumerate(others):
            slot = out_ref.at[2 * chip[0] + chip[1]]
            pltpu.make_async_remote_copy(src_ref=slot, dst_ref=slot, send_sem=send_sems.at[j],
                                         recv_sem=recv_sems.at[j], device_id=(x, y, c),
                                         device_id_type=MESH).wait_recv()
        for cp in sends:
            cp.wait_send()

    any_spec = pl.BlockSpec(memory_space=pl.ANY)
    return pl.pallas_call(
        body, name="grad_exchange_chips",
        out_shape=jax.ShapeDtypeStruct(pp.shape, pp.dtype),
        in_specs=[any_spec], out_specs=any_spec,
        scratch_shapes=[pltpu.SemaphoreType.DMA((3,)), pltpu.SemaphoreType.DMA((3,))],
    )(pp)


def _share_half(r):
    def body(r_ref, out_ref, send_sem, recv_sem):
        x, y, c, _ = _place()
        cp = pltpu.make_async_remote_copy(src_ref=r_ref, dst_ref=out_ref, send_sem=send_sem, recv_sem=recv_sem,
                                          device_id=(x, y, 1 - c), device_id_type=MESH)
        cp.start()
        cp.wait()

    any_spec = pl.BlockSpec(memory_space=pl.ANY)
    return pl.pallas_call(
        body, name="grad_share_half",
        out_shape=jax.ShapeDtypeStruct(r.shape, r.dtype),
        in_specs=[any_spec], out_specs=any_spec,
        scratch_shapes=[pltpu.SemaphoreType.DMA, pltpu.SemaphoreType.DMA],
    )(r)


def _add_my_half(g, sib, c_idx):
    rh = g.shape[2]
    tr = _tile(rh, 512)

    def body(c_ref, g_ref, s_ref, o_ref):
        o_ref[...] = (g_ref[...] + s_ref[...]).astype(BF16)

    return pl.pallas_call(
        body, name="grad_add_halves",
        grid_spec=pltpu.PrefetchScalarGridSpec(
            num_scalar_prefetch=1, grid=(N_CHIPS, rh // tr),
            in_specs=[pl.BlockSpec((None, None, tr, ROW), lambda s, i, c: (s, c[0], i, 0)),
                      pl.BlockSpec((None, tr, ROW), lambda s, i, c: (s, i, 0))],
            out_specs=pl.BlockSpec((None, tr, ROW), lambda s, i, c: (s, i, 0))),
        out_shape=jax.ShapeDtypeStruct((N_CHIPS, rh, ROW), BF16),
        compiler_params=_cp(2),
    )(c_idx, g, sib)


def _sum_chips(parts, pp, s_idx):
    rh = parts.shape[1]
    tr = _tile(rh, 512)

    def body(s_ref, p0, p1, p2, p3, mine_ref, o_ref):
        own = mine_ref[...]
        t = [jnp.where(s_ref[0] == k, own, p[...]).astype(F32) for k, p in enumerate((p0, p1, p2, p3))]
        o_ref[...] = ((t[0] + t[1]) + t[2]) + t[3]

    slot = lambda k: pl.BlockSpec((None, tr, ROW), lambda i, s: (jnp.where(s[0] == k, (k + 1) % N_CHIPS, k), i, 0))
    return pl.pallas_call(
        body, name="grad_sum_chips",
        grid_spec=pltpu.PrefetchScalarGridSpec(
            num_scalar_prefetch=1, grid=(rh // tr,),
            in_specs=[slot(0), slot(1), slot(2), slot(3), pl.BlockSpec((None, tr, ROW), lambda i, s: (s[0], i, 0))],
            out_specs=pl.BlockSpec((tr, ROW), lambda i, s: (i, 0))),
        out_shape=jax.ShapeDtypeStruct((rh, ROW), F32),
        compiler_params=_cp(1),
    )(s_idx, parts, parts, parts, parts, pp)


def _adamw(w, g_mine, g_sib, m, v, c_idx):
    rows = w.shape[0]
    tr = _tile(rows // 2, 512)
    nbh = rows // 2 // tr
    c1 = 1.0 - ADAM_B1 ** ADAM_STEP
    c2 = 1.0 - ADAM_B2 ** ADAM_STEP

    def body(c_ref, w_ref, gm_ref, gs_ref, m_ref, v_ref, g_ref, d_ref, mo_ref, vo_ref):
        g = jnp.where(pl.program_id(0) // nbh == c_ref[0], gm_ref[...], gs_ref[...])
        g_ref[...] = g
        m = ADAM_B1 * m_ref[...] + (1.0 - ADAM_B1) * g
        v = ADAM_B2 * v_ref[...] + (1.0 - ADAM_B2) * (g * g)
        mo_ref[...] = m
        vo_ref[...] = v
        d_ref[...] = -ADAM_LR * ((m / c1) / (jnp.sqrt(v / c2) + ADAM_EPS) + ADAM_WD * w_ref[...])

    spec = pl.BlockSpec((tr, ROW), lambda i, c: (i, 0))
    half = pl.BlockSpec((tr, ROW), lambda i, c: (i % nbh, 0))
    return pl.pallas_call(
        body, name="adamw",
        grid_spec=pltpu.PrefetchScalarGridSpec(
            num_scalar_prefetch=1, grid=(rows // tr,),
            in_specs=[spec, half, half, spec, spec], out_specs=[spec] * 4),
        out_shape=[jax.ShapeDtypeStruct(w.shape, F32)] * 4,
        compiler_params=_cp(1),
    )(c_idx, w, g_mine, g_sib, m, v)


def kernel(x, p, ffn1_w_in, ffn1_w_out, ln1_g, ln1_b, w_mix_in, b_forget, conv_w, g_attn, g_conv, w_mix_out, ln2_g, ln2_b, ffn2_w_in, ffn2_w_out, ln3_g, ln3_b, w_ple, w_ple_gate, b_ple_gate, ln4_g, ln4_b, loss_target, m_ffn1_w_in, m_ffn1_w_out, m_ln1_g, m_ln1_b, m_w_mix_in, m_b_forget, m_conv_w, m_g_attn, m_g_conv, m_w_mix_out, m_ln2_g, m_ln2_b, m_ffn2_w_in, m_ffn2_w_out, m_ln3_g, m_ln3_b, m_w_ple, m_w_ple_gate, m_b_ple_gate, m_ln4_g, m_ln4_b, v_ffn1_w_in, v_ffn1_w_out, v_ln1_g, v_ln1_b, v_w_mix_in, v_b_forget, v_conv_w, v_g_attn, v_g_conv, v_w_mix_out, v_ln2_g, v_ln2_b, v_ffn2_w_in, v_ffn2_w_out, v_ln3_g, v_ln3_b, v_w_ple, v_w_ple_gate, v_b_ple_gate, v_ln4_g, v_ln4_b):
    args = dict(locals())
    shard = {n: args[n][0] if LAYOUT[n][1] is not None else args[n] for n in WEIGHTS}
    m_shard = {n: args["m_" + n][0] if LAYOUT[n][1] is not None else args["m_" + n] for n in WEIGHTS}
    v_shard = {n: args["v_" + n][0] if LAYOUT[n][1] is not None else args["v_" + n] for n in WEIGHTS}
    rh = PAYLOAD_ROWS // 2

    c_idx = lax.axis_index("c").astype(jnp.int32).reshape(1)
    chip = (2 * lax.axis_index("x") + lax.axis_index("y")).astype(jnp.int32)
    w_pay = _pack(shard)
    w_bf = w_pay.astype(BF16).reshape(2, rh, ROW)
    gathered = lax.dynamic_update_slice(_all_gather_weights(w_bf), w_bf[None], (chip, 0, 0, 0))
    parts = _unpack(gathered.reshape(N_CHIPS, PAYLOAD_ROWS, ROW))
    full = {n: _join_chips(n, parts[n]) for n in WEIGHTS}
    full.update({n: shard[n] for n in WEIGHTS if LAYOUT[n][1] is None})

    loss_acc, grad_x, grads = _local_step(x[0], p[0, 0], loss_target[0], full)
    loss = lax.psum(loss_acc[0, 0], ("x", "y", "c"))

    g_pay = _pack({n: _split_chips(n, grads[n]) for n in WEIGHTS}).reshape(N_CHIPS, 2, rh, ROW)
    chip_sums = _add_my_half(g_pay, _swap_halves(g_pay), c_idx)
    my_half = _sum_chips(_exchange_chips(chip_sums), chip_sums, chip.reshape(1))
    reduced, delta, new_m, new_v = _adamw(w_pay, my_half, _share_half(my_half), _pack(m_shard), _pack(v_shard), c_idx)

    def outs(payload):
        u = _unpack(payload)
        return [u[n][None] if LAYOUT[n][1] is not None else u[n] for n in WEIGHTS]

    return (loss, grad_x[None], *outs(reduced), *outs(delta), *outs(new_m), *outs(new_v))
```

```python
import functools
import math

import jax
import jax.numpy as jnp
from jax import lax
from jax.experimental import pallas as pl
from jax.experimental.pallas import tpu as pltpu

F32 = jnp.float32
BF16 = jnp.bfloat16

D_MODEL = 1024
D_FF = 2816
N_HEADS = 8
HEAD_DIM = 64
D_ATTN = N_HEADS * HEAD_DIM
D_CONV = 512
PLE_DIM = 256
N_FLOG = 128
ALPHA = 2.0 ** 0.25
LN_EPS = 1e-5
RMS_EPS = 1e-6
NEG_INF = -1e30
Q_SCALE = 1.0 / math.sqrt(HEAD_DIM)
LOG2E = math.log2(math.e)

ADAM_LR = 0.001
ADAM_B1 = 0.9
ADAM_B2 = 0.999
ADAM_EPS = 1e-08
ADAM_WD = 0.01
ADAM_STEP = 10

V7X_VMEM_BYTES = 64 << 20
VMEM_LIMIT = V7X_VMEM_BYTES - (8 << 20)
LANE = 128
FF_CHUNK = 256
N_CHIPS = 4
MESH = pl.DeviceIdType.MESH


def _cp(n_axes):
    return pltpu.CompilerParams(dimension_semantics=("arbitrary",) * n_axes, vmem_limit_bytes=VMEM_LIMIT)


def _resident(shape):
    n = len(shape)
    return pl.BlockSpec(shape, lambda *_: (0,) * n, pipeline_mode=pl.Buffered(1))


def _nn(a, b):
    return jnp.dot(a, b, preferred_element_type=F32)


def _nt(a, b):
    return lax.dot_general(a, b, (((1,), (1,)), ((), ())), preferred_element_type=F32)


def _tn(a, b):
    return lax.dot_general(a, b, (((0,), (0,)), ((), ())), preferred_element_type=F32)


def _ln_stats(r):
    mu = jnp.mean(r, axis=-1, keepdims=True)
    xc = r - mu
    var = jnp.mean(xc * xc, axis=-1, keepdims=True)
    rstd = lax.rsqrt(var + LN_EPS)
    return xc * rstd, rstd


def _ln_bwd(dy, xhat, rstd, g):
    dxh = dy * g
    m1 = jnp.mean(dxh, axis=-1, keepdims=True)
    m2 = jnp.mean(dxh * xhat, axis=-1, keepdims=True)
    return rstd * (dxh - m1 - xhat * m2)


def _sigmoid(z):
    return 1.0 / (1.0 + jnp.exp(-z))


def _rowsum(a):
    return jnp.sum(a, axis=0, keepdims=True)


def _tile(total, want):
    if total <= want:
        return total
    for t in range(want - want % 8, 0, -8):
        if total % t == 0:
            return t
    raise ValueError((total, want))


def _ffn_fwd(x, w_in, w_out, lg, lb, name):
    T = x.shape[0]
    tm = _tile(T, 512)
    nf = D_FF // FF_CHUNK

    def body(x_ref, wi_ref, wo_ref, lg_ref, lb_ref, xo_ref, r_ref, g_ref, u_ref, h_ref):
        xf = x_ref[...]
        xb = xf.astype(BF16)
        acc = jnp.zeros((tm, D_MODEL), F32)
        for j in range(nf):
            c0 = j * FF_CHUNK
            g = _nn(xb, wi_ref[:, c0:c0 + FF_CHUNK])
            u = _nn(xb, wi_ref[:, D_FF + c0:D_FF + c0 + FF_CHUNK])
            hb = (g * _sigmoid(g) * u).astype(BF16)
            g_ref[:, c0:c0 + FF_CHUNK] = g.astype(BF16)
            u_ref[:, c0:c0 + FF_CHUNK] = u.astype(BF16)
            h_ref[:, c0:c0 + FF_CHUNK] = hb
            acc = acc + _nn(hb, wo_ref[c0:c0 + FF_CHUNK, :])
        r = ALPHA * xf + 0.5 * acc
        r_ref[...] = r
        xhat, _ = _ln_stats(r)
        xo_ref[...] = xhat * lg_ref[...] + lb_ref[...]

    row = lambda n: pl.BlockSpec((tm, n), lambda i: (i, 0))
    return pl.pallas_call(
        body, name=name, grid=(T // tm,),
        in_specs=[row(D_MODEL), _resident((D_MODEL, 2 * D_FF)), _resident((D_FF, D_MODEL)),
                  _resident((1, D_MODEL)), _resident((1, D_MODEL))],
        out_specs=[row(D_MODEL), row(D_MODEL), row(D_FF), row(D_FF), row(D_FF)],
        out_shape=[jax.ShapeDtypeStruct((T, D_MODEL), F32), jax.ShapeDtypeStruct((T, D_MODEL), F32),
                   jax.ShapeDtypeStruct((T, D_FF), BF16), jax.ShapeDtypeStruct((T, D_FF), BF16),
                   jax.ShapeDtypeStruct((T, D_FF), BF16)],
        compiler_params=_cp(1),
    )(x, w_in, w_out, lg, lb)


def _ffn_bwd(dxo, r, g, u, w_in, w_out, lg, name):
    T = r.shape[0]
    tm = _tile(T, 256)
    nf = D_FF // FF_CHUNK

    def body(dxo_ref, r_ref, g_ref, u_ref, wi_ref, wo_ref, lg_ref, dx_ref, dgu_ref, df_ref, dlg_ref, dlb_ref):
        i = pl.program_id(0)
        dy = dxo_ref[...]
        xhat, rstd = _ln_stats(r_ref[...])
        dr = _ln_bwd(dy, xhat, rstd, lg_ref[...])

        @pl.when(i == 0)
        def _():
            dlg_ref[...] = jnp.zeros_like(dlg_ref)
            dlb_ref[...] = jnp.zeros_like(dlb_ref)

        dlg_ref[...] += _rowsum(dy * xhat)
        dlb_ref[...] += _rowsum(dy)
        dfb = (0.5 * dr).astype(BF16)
        df_ref[...] = dfb
        acc = jnp.zeros((tm, D_MODEL), F32)
        dh_ahead = _nt(dfb, wo_ref[0:FF_CHUNK, :])
        for j in range(nf):
            c0 = j * FF_CHUNK
            dh = dh_ahead
            if j + 1 < nf:
                dh_ahead = _nt(dfb, wo_ref[c0 + FF_CHUNK:c0 + 2 * FF_CHUNK, :])
            gg = g_ref[:, c0:c0 + FF_CHUNK].astype(F32)
            uu = u_ref[:, c0:c0 + FF_CHUNK].astype(F32)
            s = _sigmoid(gg)
            dgb = (dh * uu * s * (1.0 + gg * (1.0 - s))).astype(BF16)
            dub = (dh * gg * s).astype(BF16)
            dgu_ref[:, c0:c0 + FF_CHUNK] = dgb
            dgu_ref[:, D_FF + c0:D_FF + c0 + FF_CHUNK] = dub
            acc = acc + _nt(dgb, wi_ref[:, c0:c0 + FF_CHUNK]) + _nt(dub, wi_ref[:, D_FF + c0:D_FF + c0 + FF_CHUNK])
        dx_ref[...] = ALPHA * dr + acc

    row = lambda n: pl.BlockSpec((tm, n), lambda i: (i, 0))
    return pl.pallas_call(
        body, name=name, grid=(T // tm,),
        in_specs=[row(D_MODEL), row(D_MODEL), row(D_FF), row(D_FF), _resident((D_MODEL, 2 * D_FF)),
                  _resident((D_FF, D_MODEL)), _resident((1, D_MODEL))],
        out_specs=[row(D_MODEL), row(2 * D_FF), row(D_MODEL), _resident((1, D_MODEL)), _resident((1, D_MODEL))],
        out_shape=[jax.ShapeDtypeStruct((T, D_MODEL), F32), jax.ShapeDtypeStruct((T, 2 * D_FF), BF16),
                   jax.ShapeDtypeStruct((T, D_MODEL), BF16), jax.ShapeDtypeStruct((1, D_MODEL), F32),
                   jax.ShapeDtypeStruct((1, D_MODEL), F32)],
        compiler_params=_cp(1),
    )(dxo, r, g, u, w_in, w_out, lg)


def _matmul_tn(a, b, name, tn=None):
    T, K = a.shape
    N = b.shape[1]
    tt = _tile(T, 1024)
    if tn is None:
        tn = N
        while K * tn * 4 > (6 << 20) and tn % 256 == 0:
            tn //= 2
    assert N % tn == 0

    def body(a_ref, b_ref, o_ref):
        @pl.when(pl.program_id(1) == 0)
        def _():
            o_ref[...] = jnp.zeros_like(o_ref)

        o_ref[...] += _tn(a_ref[...].astype(BF16), b_ref[...].astype(BF16))

    return pl.pallas_call(
        body, name=name, grid=(N // tn, T // tt),
        in_specs=[pl.BlockSpec((tt, K), lambda n, t: (t, 0)), pl.BlockSpec((tt, tn), lambda n, t: (t, n))],
        out_specs=pl.BlockSpec((K, tn), lambda n, t: (0, n)),
        out_shape=jax.ShapeDtypeStruct((K, N), F32),
        compiler_params=_cp(2),
    )(a, b)


def _matmul_nn(x, w, scale, out_dtype, name):
    T, K = x.shape
    N = w.shape[1]
    tm = _tile(T, 512)

    def body(x_ref, w_ref, s_ref, o_ref):
        o_ref[...] = (_nn(x_ref[...].astype(BF16), w_ref[...]) * s_ref[...]).astype(out_dtype)

    return pl.pallas_call(
        body, name=name, grid=(T // tm,),
        in_specs=[pl.BlockSpec((tm, K), lambda i: (i, 0)), _resident((K, N)), _resident((1, N))],
        out_specs=pl.BlockSpec((tm, N), lambda i: (i, 0)),
        out_shape=jax.ShapeDtypeStruct((T, N), out_dtype),
        compiler_params=_cp(1),
    )(x, w, scale)


def _log_sigmoid(z):
    return jnp.minimum(z, 0.0) - jnp.log1p(jnp.exp(-jnp.abs(z)))


def _tri(n, lower):
    r = lax.broadcasted_iota(jnp.int32, (n, n), 0)
    c = lax.broadcasted_iota(jnp.int32, (n, n), 1)
    return jnp.where((c <= r) if lower else (c >= r), 1.0, 0.0).astype(F32)


def _f32dot(a, b):
    return jnp.dot(a, b, preferred_element_type=F32, precision=lax.Precision.HIGHEST)


def _forget_cumsum(flog, col, bf):
    T = flog.shape[0]
    bt = _tile(T, 512)

    def body(f_ref, b_ref, c_ref, carry):
        @pl.when(pl.program_id(0) == 0)
        def _():
            carry[...] = jnp.zeros_like(carry)

        lf = _log_sigmoid(f_ref[...] + b_ref[...])
        c = _f32dot(_tri(bt, True), lf) + carry[...]
        c_ref[...] = c * LOG2E
        carry[...] = c[bt - 1:bt, :]

    return pl.pallas_call(
        body, name="forget_cumsum", grid=(T // bt,),
        in_specs=[pl.BlockSpec((bt, N_FLOG), lambda i: (i, col)), _resident((1, N_FLOG))],
        out_specs=pl.BlockSpec((bt, N_FLOG), lambda i: (i, 0)),
        out_shape=jax.ShapeDtypeStruct((T, N_FLOG), F32),
        scratch_shapes=[pltpu.VMEM((1, N_FLOG), F32)],
        compiler_params=_cp(1),
    )(flog, bf)


def _forget_bwd(dck, dcq, flog, col, bf):
    T = dcq.shape[0]
    bt = _tile(T, 512)
    nb = T // bt

    def body(k0_ref, k1_ref, k2_ref, k3_ref, dcq_ref, f_ref, b_ref, dz_ref, db_ref, carry):
        @pl.when(pl.program_id(0) == 0)
        def _():
            carry[...] = jnp.zeros_like(carry)
            db_ref[...] = jnp.zeros_like(db_ref)

        dc = ((k0_ref[...] + k1_ref[...]) + (k2_ref[...] + k3_ref[...])) + dcq_ref[...]
        dlf = _f32dot(_tri(bt, False), dc) + carry[...]
        carry[...] = dlf[0:1, :]
        z = f_ref[...] + b_ref[...]
        dz = dlf * _sigmoid(-z)
        dz_ref[...] = dz.astype(BF16)
        db_ref[...] += _rowsum(dz)

    slab = lambda j: pl.BlockSpec((None, bt, N_FLOG), lambda i: (j, nb - 1 - i, 0))
    return pl.pallas_call(
        body, name="forget_bwd", grid=(nb,),
        in_specs=[slab(0), slab(1), slab(2), slab(3),
                  pl.BlockSpec((bt, N_FLOG), lambda i: (nb - 1 - i, 0)),
                  pl.BlockSpec((bt, N_FLOG), lambda i: (nb - 1 - i, col)), _resident((1, N_FLOG))],
        out_specs=[pl.BlockSpec((bt, N_FLOG), lambda i: (nb - 1 - i, 0)), _resident((1, N_FLOG))],
        out_shape=[jax.ShapeDtypeStruct((T, N_FLOG), BF16), jax.ShapeDtypeStruct((1, N_FLOG), F32)],
        scratch_shapes=[pltpu.VMEM((1, N_FLOG), F32)],
        compiler_params=_cp(1),
    )(dck, dck, dck, dck, dcq, flog, bf)


def _head_masks():
    lane = lax.broadcasted_iota(jnp.int32, (1, LANE), 1)
    return lane < HEAD_DIM


def _split_heads(x2, is_a):
    zero = jnp.zeros_like(x2)
    return jnp.where(is_a, x2, zero), jnp.where(is_a, zero, x2)


def _attn_fwd(qkv, vt, cb):
    T = qkv.shape[0]
    tq = _tile(T, 512)
    tk = tq
    nq = T // tq
    npair = N_HEADS // 2

    def body(q_ref, k_ref, vt_ref, cb_ref, o_ref, al_ref, m_s, l_s, acc_s):
        i = pl.program_id(1)
        qs = _split_heads(q_ref[...], _head_masks())
        m_s[...] = jnp.full_like(m_s, NEG_INF)
        l_s[...] = jnp.zeros_like(l_s)
        acc_s[...] = jnp.zeros_like(acc_s)

        def scores_at(kk):
            k2 = k_ref[pl.ds(pl.multiple_of(kk * tk, tk), tk), :]
            return tuple(_nt(k2, qs[h]) for h in range(2))

        def consume(kk, scores, masked):
            k0 = pl.multiple_of(kk * tk, tk)
            v2t = vt_ref[:, pl.ds(k0, tk)]
            csb = cb_ref[pl.ds(k0, tk), :]
            for h in range(2):
                cs = csb[:, h * HEAD_DIM:h * HEAD_DIM + 1]
                zt = scores[h] - cs
                if masked:
                    rr = lax.broadcasted_iota(jnp.int32, (tk, tq), 0)
                    cc = lax.broadcasted_iota(jnp.int32, (tk, tq), 1)
                    zt = jnp.where(cc >= rr, zt, NEG_INF)
                m_old = m_s[h]
                m_new = jnp.maximum(m_old, jnp.max(zt, axis=0, keepdims=True))
                p = jnp.exp2(zt - m_new)
                a = jnp.exp2(m_old - m_new)
                l_s[h] = a * l_s[h] + jnp.sum(p, axis=0, keepdims=True)
                acc_s[h] = a * acc_s[h] + _nn(v2t, p.astype(BF16))
                m_s[h] = m_new

        def two_tiles(kk, second_masked):
            first, second = scores_at(kk), scores_at(kk + 1)
            consume(kk, first, False)
            consume(kk + 1, second, second_masked)

        def loop_body(t, carry):
            two_tiles(2 * t, False)
            return carry

        lax.fori_loop(0, i // 2, loop_body, 0)

        @pl.when(i % 2 == 1)
        def _():
            two_tiles(i - 1, True)

        @pl.when(i % 2 == 0)
        def _():
            consume(i, scores_at(i), True)

        outs = []
        for h in range(2):
            l = l_s[h]
            outs.append(acc_s[h] * (1.0 / l))
            al_ref[0, h:h + 1, :] = -(m_s[h] + jnp.log2(l))
        al_ref[0, 2:8, :] = jnp.zeros((6, tq), F32)
        dim = lax.broadcasted_iota(jnp.int32, (LANE, 1), 0)
        o_ref[...] = jnp.where(dim < HEAD_DIM, outs[0], outs[1]).T

    rowl = pl.BlockSpec((1, 8, tq), lambda j, i: (j, 0, i))
    return pl.pallas_call(
        body, name="attn_fwd", grid=(npair, nq),
        in_specs=[pl.BlockSpec((tq, LANE), lambda j, i: (i, j)),
                  pl.BlockSpec((T, LANE), lambda j, i: (0, npair + j), pipeline_mode=pl.Buffered(1)),
                  pl.BlockSpec((LANE, T), lambda j, i: (j, 0), pipeline_mode=pl.Buffered(1)),
                  pl.BlockSpec((T, LANE), lambda j, i: (0, j), pipeline_mode=pl.Buffered(1))],
        out_specs=[pl.BlockSpec((tq, LANE), lambda j, i: (i, j)), rowl],
        out_shape=[jax.ShapeDtypeStruct((T, D_ATTN), F32), jax.ShapeDtypeStruct((npair, 8, T), F32)],
        scratch_shapes=[pltpu.VMEM((2, 1, tq), F32), pltpu.VMEM((2, 1, tq), F32), pltpu.VMEM((2, LANE, tq), F32)],
        compiler_params=_cp(2),
    )(qkv, qkv, vt, cb)


def _attn_bwd(qkv, dob, cb, alrow, dlrow):
    T = qkv.shape[0]
    tq = _tile(T, 512)
    tk = tq
    nq = T // tq
    npair = N_HEADS // 2

    def body(q_ref, k_ref, v_ref, do_ref, cb_ref, al_ref, dl_ref, dq_ref, dk_ref, dv_ref, dc_ref, dcq_ref,
             dk_s, dv_s, dc_s):
        kj = pl.program_id(1)
        is_a = _head_masks()
        ks = _split_heads(k_ref[...], is_a)
        vs = _split_heads(v_ref[...], is_a)
        cs = (cb_ref[:, 0:1], cb_ref[:, HEAD_DIM:HEAD_DIM + 1])

        @pl.when(kj == 0)
        def _():
            dq_ref[...] = jnp.zeros_like(dq_ref)
            dcq_ref[...] = jnp.zeros_like(dcq_ref)

        dk_s[...] = jnp.zeros_like(dk_s)
        dv_s[...] = jnp.zeros_like(dv_s)
        dc_s[...] = jnp.zeros_like(dc_s)

        def step(qi, masked):
            q0 = pl.multiple_of(qi * tq, tq)
            q2 = q_ref[pl.ds(q0, tq), :]
            do2 = do_ref[pl.ds(q0, tq), :]
            for h in range(2):
                alr = al_ref[0, h:h + 1, pl.ds(q0, tq)]
                dlr = dl_ref[0, h:h + 1, pl.ds(q0, tq)]
                zt = _nt(ks[h], q2) + (alr - cs[h])
                if masked:
                    rr = lax.broadcasted_iota(jnp.int32, (tk, tq), 0)
                    cc = lax.broadcasted_iota(jnp.int32, (tk, tq), 1)
                    zt = jnp.where(cc >= rr, zt, NEG_INF)
                pt = jnp.exp2(zt)
                dst = pt * (_nt(vs[h], do2) - dlr)
                pb = pt.astype(BF16)
                dsb = dst.astype(BF16)
                dv_s[h] += _nn(pb, do2)
                dk_s[h] += _nn(dsb, q2)
                dc_s[h] += jnp.sum(dst, axis=-1, keepdims=True)
                dcq_ref[0, h:h + 1, pl.ds(q0, tq)] += jnp.sum(dst, axis=0, keepdims=True)
                dq_ref[pl.ds(q0, tq), :] += _tn(dsb, ks[h])

        step(kj, True)

        def loop_body(qi, carry):
            step(qi, False)
            return carry

        lax.fori_loop(kj + 1, nq, loop_body, 0)
        dk_ref[...] = (jnp.where(is_a, dk_s[0], dk_s[1]) * (1.0 / LOG2E)).astype(BF16)
        dv_ref[...] = jnp.where(is_a, dv_s[0], dv_s[1]).astype(BF16)
        lane = lax.broadcasted_iota(jnp.int32, (1, LANE), 1)
        head = 2 * pl.program_id(0)
        dc_ref[...] = jnp.where(lane == head, -dc_s[0], jnp.where(lane == head + 1, -dc_s[1], 0.0))

        @pl.when(kj == nq - 1)
        def _():
            dq_ref[...] = dq_ref[...] * Q_SCALE

    full = lambda col: pl.BlockSpec((T, LANE), lambda j, kj: (0, col(j)), pipeline_mode=pl.Buffered(1))
    tile = lambda col: pl.BlockSpec((tk, LANE), lambda j, kj: (kj, col(j)))
    rowl = pl.BlockSpec((1, 8, T), lambda j, kj: (j, 0, 0))
    return pl.pallas_call(
        body, name="attn_bwd", grid=(npair, nq),
        in_specs=[full(lambda j: j), tile(lambda j: npair + j), tile(lambda j: 2 * npair + j), full(lambda j: j),
                  tile(lambda j: j), rowl, rowl],
        out_specs=[pl.BlockSpec((T, LANE), lambda j, kj: (0, j)), tile(lambda j: j), tile(lambda j: j),
                   pl.BlockSpec((None, tk, LANE), lambda j, kj: (j, kj, 0)), rowl],
        out_shape=[jax.ShapeDtypeStruct((T, D_ATTN), F32), jax.ShapeDtypeStruct((T, D_ATTN), BF16),
                   jax.ShapeDtypeStruct((T, D_ATTN), BF16), jax.ShapeDtypeStruct((npair, T, LANE), F32),
                   jax.ShapeDtypeStruct((npair, 8, T), F32)],
        scratch_shapes=[pltpu.VMEM((2, tk, LANE), F32), pltpu.VMEM((2, tk, LANE), F32), pltpu.VMEM((2, tk, 1), F32)],
        compiler_params=_cp(2),
    )(qkv, qkv, qkv, dob, cb, alrow, dlrow)


HALO = 8


def _shift_rows(cur, other, k, tm, down):
    row = lax.broadcasted_iota(jnp.int32, (tm, 1), 0)
    reps = tm // HALO
    if down:
        rolled = pltpu.roll(cur, k, 0)
        fill = jnp.tile(pltpu.roll(other, k, 0), (reps, 1))
        return jnp.where(row < k, fill, rolled)
    rolled = pltpu.roll(cur, tm - k, 0)
    fill = jnp.tile(pltpu.roll(other, HALO - k, 0), (reps, 1))
    return jnp.where(row >= tm - k, fill, rolled)


def _conv_fwd(c, hh, c_prev, hh_prev, w_ref, first, tm):
    u = c * hh
    u_prev = jnp.where(first, 0.0, c_prev * hh_prev)
    u1 = _shift_rows(u, u_prev, 1, tm, True)
    u2 = _shift_rows(u, u_prev, 2, tm, True)
    y = w_ref[0:1, :] * u2 + w_ref[1:2, :] * u1 + w_ref[2:3, :] * u
    return u, u1, u2, y


def _rms(x, g):
    rs = lax.rsqrt(jnp.mean(x * x, axis=-1, keepdims=True) + RMS_EPS)
    return x * rs * g, rs


def _mixer_tail_fwd(o, bchf, conv_w, g_attn, g_conv, w_mo, x1, lg, lb):
    T = o.shape[0]
    tm = _tile(T, 512)
    hb = tm // HALO

    def body(o_ref, b_ref, c_ref, h_ref, cp_ref, hp_ref, w_ref, ga_ref, gc_ref, wmo_ref, x1_ref, lg_ref, lb_ref,
             x2_ref, r2_ref, mg_ref):
        first = pl.program_id(0) == 0
        _, _, _, y = _conv_fwd(c_ref[...], h_ref[...], cp_ref[...], hp_ref[...], w_ref, first, tm)
        na, _ = _rms(o_ref[...], ga_ref[...])
        nc, _ = _rms(b_ref[...] * y, gc_ref[...])
        nab = na.astype(BF16)
        ncb = nc.astype(BF16)
        mg_ref[:, 0:D_ATTN] = nab
        mg_ref[:, D_ATTN:] = ncb
        r2 = ALPHA * x1_ref[...] + _nn(nab, wmo_ref[0:D_ATTN, :]) + _nn(ncb, wmo_ref[D_ATTN:, :])
        r2_ref[...] = r2
        xhat, _ = _ln_stats(r2)
        x2_ref[...] = xhat * lg_ref[...] + lb_ref[...]

    row = lambda n, col=0: pl.BlockSpec((tm, n), lambda i: (i, col))
    prev = lambda col: pl.BlockSpec((HALO, D_CONV), lambda i: (jnp.maximum(i * hb - 1, 0), col))
    return pl.pallas_call(
        body, name="mixer_tail_fwd", grid=(T // tm,),
        in_specs=[row(D_ATTN), row(D_CONV, 0), row(D_CONV, 1), row(D_CONV, 2), prev(1), prev(2),
                  _resident((3, D_CONV)), _resident((1, D_ATTN)), _resident((1, D_CONV)),
                  _resident((D_MODEL, D_MODEL)), row(D_MODEL), _resident((1, D_MODEL)), _resident((1, D_MODEL))],
        out_specs=[row(D_MODEL), row(D_MODEL), row(D_MODEL)],
        out_shape=[jax.ShapeDtypeStruct((T, D_MODEL), F32), jax.ShapeDtypeStruct((T, D_MODEL), F32),
                   jax.ShapeDtypeStruct((T, D_MODEL), BF16)],
        compiler_params=_cp(1),
    )(o, bchf, bchf, bchf, bchf, bchf, conv_w, g_attn, g_conv, w_mo, x1, lg, lb)


def _head_sum_rows():
    row = lax.broadcasted_iota(jnp.int32, (4 * 8, D_ATTN), 0)
    head = lax.broadcasted_iota(jnp.int32, (4 * 8, D_ATTN), 1) // HEAD_DIM
    return jnp.where((row % 8 < 2) & (2 * (row // 8) + row % 8 == head), 1.0, 0.0).astype(F32)


def _mixer_tail_bwd(dx2, r2, lg, w_mo, o, bchf, conv_w, g_attn, g_conv):
    T = o.shape[0]
    tm = _tile(T, 256)
    hb = tm // HALO

    def body(dx2_ref, r2_ref, lg_ref, wmo_ref, o_ref, b_ref, c_ref, h_ref, cp_ref, hp_ref, w_ref, ga_ref, gc_ref,
             dx1_ref, dr_ref, do_ref, dl_ref, dco_ref, dlg_ref, dlb_ref, dga_ref, dgc_ref):
        i = pl.program_id(0)

        @pl.when(i == 0)
        def _():
            for ref in (dlg_ref, dlb_ref, dga_ref, dgc_ref):
                ref[...] = jnp.zeros_like(ref)

        dy = dx2_ref[...]
        xhat, rstd = _ln_stats(r2_ref[...])
        dr = _ln_bwd(dy, xhat, rstd, lg_ref[...])
        dlg_ref[...] += _rowsum(dy * xhat)
        dlb_ref[...] += _rowsum(dy)
        dx1_ref[...] = ALPHA * dr
        drb = dr.astype(BF16)
        dr_ref[...] = drb
        dna = _nt(drb, wmo_ref[0:D_ATTN, :])
        dnc = _nt(drb, wmo_ref[D_ATTN:, :])

        def rms_bwd(x, g, dn):
            rs = lax.rsqrt(jnp.mean(x * x, axis=-1, keepdims=True) + RMS_EPS)
            dng = dn * g
            dx = rs * dng - x * (rs * rs * rs) * jnp.mean(dng * x, axis=-1, keepdims=True)
            return dx, _rowsum(dn * x * rs)

        oo = o_ref[...]
        do, dga = rms_bwd(oo, ga_ref[...], dna)
        dga_ref[...] += dga
        do_ref[...] = do.astype(BF16)
        dl_ref[...] = lax.dot_general(_head_sum_rows(), do * oo, (((1,), (1,)), ((), ())),
                                      preferred_element_type=F32, precision=lax.Precision.HIGHEST)
        _, _, _, y = _conv_fwd(c_ref[...], h_ref[...], cp_ref[...], hp_ref[...], w_ref, i == 0, tm)
        dco, dgc = rms_bwd(b_ref[...] * y, gc_ref[...], dnc)
        dgc_ref[...] += dgc
        dco_ref[...] = dco

    row = lambda n, col=0: pl.BlockSpec((tm, n), lambda i: (i, col))
    prev = lambda col: pl.BlockSpec((HALO, D_CONV), lambda i: (jnp.maximum(i * hb - 1, 0), col))
    vec = lambda n: _resident((1, n))
    return pl.pallas_call(
        body, name="mixer_tail_bwd", grid=(T // tm,),
        in_specs=[row(D_MODEL), row(D_MODEL), vec(D_MODEL), _resident((D_MODEL, D_MODEL)), row(D_ATTN),
                  row(D_CONV, 0), row(D_CONV, 1), row(D_CONV, 2), prev(1), prev(2), _resident((3, D_CONV)),
                  vec(D_ATTN), vec(D_CONV)],
        out_specs=[row(D_MODEL), row(D_MODEL), row(D_ATTN), pl.BlockSpec((4 * 8, tm), lambda i: (0, i)), row(D_CONV),
                   vec(D_MODEL), vec(D_MODEL), vec(D_ATTN), vec(D_CONV)],
        out_shape=[jax.ShapeDtypeStruct((T, D_MODEL), F32), jax.ShapeDtypeStruct((T, D_MODEL), BF16),
                   jax.ShapeDtypeStruct((T, D_ATTN), BF16), jax.ShapeDtypeStruct((4 * 8, T), F32),
                   jax.ShapeDtypeStruct((T, D_CONV), F32), jax.ShapeDtypeStruct((1, D_MODEL), F32),
                   jax.ShapeDtypeStruct((1, D_MODEL), F32), jax.ShapeDtypeStruct((1, D_ATTN), F32),
                   jax.ShapeDtypeStruct((1, D_CONV), F32)],
        compiler_params=_cp(1),
    )(dx2, r2, lg, w_mo, o, bchf, bchf, bchf, bchf, bchf, conv_w, g_attn, g_conv)


def _conv_bwd(dco, bchf, conv_w):
    T = dco.shape[0]
    tm = _tile(T, 512)
    hb = tm // HALO
    nt = T // tm

    def body(dco_ref, dcon_ref, b_ref, bn_ref, c_ref, h_ref, cp_ref, hp_ref, w_ref, dbch_ref, dw_ref):
        i = pl.program_id(0)

        @pl.when(i == 0)
        def _():
            dw_ref[...] = jnp.zeros_like(dw_ref)

        cc = c_ref[...]
        hh = h_ref[...]
        u, u1, u2, y = _conv_fwd(cc, hh, cp_ref[...], hp_ref[...], w_ref, i == 0, tm)
        dco = dco_ref[...]
        bb = b_ref[...]
        dyc = dco * bb
        dy_next = jnp.where(i == nt - 1, 0.0, dcon_ref[...] * bn_ref[...])
        d1 = _shift_rows(dyc, dy_next, 1, tm, False)
        d2 = _shift_rows(dyc, dy_next, 2, tm, False)
        du = w_ref[2:3, :] * dyc + w_ref[1:2, :] * d1 + w_ref[0:1, :] * d2
        dbch_ref[:, 0:D_CONV] = (dco * y).astype(BF16)
        dbch_ref[:, D_CONV:2 * D_CONV] = (du * hh).astype(BF16)
        dbch_ref[:, 2 * D_CONV:] = (du * cc).astype(BF16)
        dw_ref[0:1, :] += _rowsum(dyc * u2)
        dw_ref[1:2, :] += _rowsum(dyc * u1)
        dw_ref[2:3, :] += _rowsum(dyc * u)

    row = lambda n, col=0: pl.BlockSpec((tm, n), lambda i: (i, col))
    prev = lambda col: pl.BlockSpec((HALO, D_CONV), lambda i: (jnp.maximum(i * hb - 1, 0), col))
    nxt = lambda col: pl.BlockSpec((HALO, D_CONV), lambda i: (jnp.minimum((i + 1) * hb, T // HALO - 1), col))
    return pl.pallas_call(
        body, name="conv_bwd", grid=(nt,),
        in_specs=[row(D_CONV), nxt(0), row(D_CONV, 0), nxt(0), row(D_CONV, 1), row(D_CONV, 2), prev(1), prev(2),
                  _resident((3, D_CONV))],
        out_specs=[row(3 * D_CONV), _resident((8, D_CONV))],
        out_shape=[jax.ShapeDtypeStruct((T, 3 * D_CONV), BF16), jax.ShapeDtypeStruct((8, D_CONV), F32)],
        compiler_params=_cp(1),
    )(dco, dco, bchf, bchf, bchf, bchf, bchf, bchf, conv_w)


def _mixer_in_bwd(dx1a, dq, dk, dv, dbch, dfl, w_qkv, w_bch, w_f):
    T = dx1a.shape[0]
    tm = _tile(T, 512)

    def body(a_ref, dq_ref, dk_ref, dv_ref, db_ref, df_ref, wq_ref, wb_ref, wf_ref, o_ref):
        acc = a_ref[...] + _nt(db_ref[...], wb_ref[...]) + _nt(df_ref[...], wf_ref[...])
        for n, ref in enumerate((dq_ref, dk_ref, dv_ref)):
            acc = acc + _nt(ref[...].astype(BF16), wq_ref[:, n * D_ATTN:(n + 1) * D_ATTN])
        o_ref[...] = acc

    row = lambda n: pl.BlockSpec((tm, n), lambda i: (i, 0))
    return pl.pallas_call(
        body, name="mixer_in_bwd", grid=(T // tm,),
        in_specs=[row(D_MODEL), row(D_ATTN), row(D_ATTN), row(D_ATTN), row(3 * D_CONV), row(N_FLOG),
                  _resident((D_MODEL, 3 * D_ATTN)), _resident((D_MODEL, 3 * D_CONV)), _resident((D_MODEL, N_FLOG))],
        out_specs=row(D_MODEL),
        out_shape=jax.ShapeDtypeStruct((T, D_MODEL), F32),
        compiler_params=_cp(1),
    )(dx1a, dq, dk, dv, dbch, dfl, w_qkv, w_bch, w_f)


def _ple_loss(x3, p, tgt, w_g, w_p, b_g, lg, lb):
    T = x3.shape[0]
    tm = _tile(T, 512)

    def body(x_ref, p_ref, t_ref, wg_ref, wp_ref, bg_ref, lg_ref, lb_ref,
             dx_ref, de_ref, dz_ref, loss_ref, dlg_ref, dlb_ref, dbg_ref):
        @pl.when(pl.program_id(0) == 0)
        def _():
            for ref in (loss_ref, dlg_ref, dlb_ref, dbg_ref):
                ref[...] = jnp.zeros_like(ref)

        xf = x_ref[...]
        gate = _sigmoid(_nn(xf.astype(BF16), wg_ref[...]) + bg_ref[...])
        e = _nn(p_ref[...].astype(BF16), wp_ref[...])
        xhat, rstd = _ln_stats(ALPHA * xf + gate * e)
        err = xhat * lg_ref[...] + lb_ref[...] - t_ref[...]
        sq = jnp.sum(_rowsum(err * err), axis=-1, keepdims=True)
        loss_ref[...] += jnp.broadcast_to(sq * (0.5 / D_MODEL), loss_ref.shape)
        dy = err * (1.0 / D_MODEL)
        dr = _ln_bwd(dy, xhat, rstd, lg_ref[...])
        dlg_ref[...] += _rowsum(dy * xhat)
        dlb_ref[...] += _rowsum(dy)
        de_ref[...] = (dr * gate).astype(BF16)
        dz = dr * e * gate * (1.0 - gate)
        dbg_ref[...] += _rowsum(dz)
        dzb = dz.astype(BF16)
        dz_ref[...] = dzb
        dx_ref[...] = ALPHA * dr + _nt(dzb, wg_ref[...])

    row = lambda n: pl.BlockSpec((tm, n), lambda i: (i, 0))
    vec = lambda n: _resident((1, n))
    return pl.pallas_call(
        body, name="ple_loss", grid=(T // tm,),
        in_specs=[row(D_MODEL), row(PLE_DIM), row(D_MODEL), _resident((D_MODEL, D_MODEL)),
                  _resident((PLE_DIM, D_MODEL)), vec(D_MODEL), vec(D_MODEL), vec(D_MODEL)],
        out_specs=[row(D_MODEL), row(D_MODEL), row(D_MODEL), vec(LANE), vec(D_MODEL), vec(D_MODEL), vec(D_MODEL)],
        out_shape=[jax.ShapeDtypeStruct((T, D_MODEL), F32), jax.ShapeDtypeStruct((T, D_MODEL), BF16),
                   jax.ShapeDtypeStruct((T, D_MODEL), BF16), jax.ShapeDtypeStruct((1, LANE), F32),
                   jax.ShapeDtypeStruct((1, D_MODEL), F32), jax.ShapeDtypeStruct((1, D_MODEL), F32),
                   jax.ShapeDtypeStruct((1, D_MODEL), F32)],
        compiler_params=_cp(1),
    )(x3, p, tgt, w_g, w_p, b_g, lg, lb)


def _lane_layout(v8):
    return jnp.repeat(v8, HEAD_DIM, axis=1)


def _row_layout(v8):
    t = v8.shape[0]
    return jnp.pad(v8.T.reshape(N_HEADS // 2, 2, t), ((0, 0), (0, 6), (0, 0)))


def _from_lane_layout(vl):
    return vl[:, ::HEAD_DIM]


def _from_row_layout(vr):
    return vr[:, :2, :].reshape(N_HEADS, -1).T


def _local_step(x, p, tgt, w):
    bf = lambda a: a.astype(BF16)
    w1i, w1o, w2i, w2o = bf(w["ffn1_w_in"]), bf(w["ffn1_w_out"]), bf(w["ffn2_w_in"]), bf(w["ffn2_w_out"])
    wmi = w["w_mix_in"]
    o_f = 3 * D_ATTN
    o_b = o_f + N_HEADS
    w_qkv = bf(wmi[:, :o_f])
    w_f = bf(jnp.pad(wmi[:, o_f:o_b], ((0, 0), (0, N_FLOG - N_HEADS))))
    w_bch = bf(wmi[:, o_b:])
    w_bchf = jnp.concatenate([w_bch, w_f], axis=1)
    w_mo, w_g, w_p = bf(w["w_mix_out"]), bf(w["w_ple_gate"]), bf(w["w_ple"])
    b_f = jnp.pad(w["b_forget"], ((0, 0), (0, N_FLOG - N_HEADS)))

    x1, r1, g1, u1, h1 = _ffn_fwd(x, w1i, w1o, w["ln1_g"], w["ln1_b"], "ffn1_fwd")
    q_scale = jnp.concatenate([jnp.full((1, D_ATTN), Q_SCALE * LOG2E, F32), jnp.ones((1, 2 * D_ATTN), F32)], axis=1)
    qkv = _matmul_nn(x1, w_qkv, q_scale, BF16, "proj_qkv")
    bchf = _matmul_nn(x1, w_bchf, jnp.ones((1, 3 * D_CONV + N_FLOG), F32), F32, "proj_bchf")
    fcol = 3 * D_CONV // N_FLOG
    c = _forget_cumsum(bchf, fcol, b_f)
    c8 = c[:, :N_HEADS]
    cb = _lane_layout(c8)
    o, alrow = _attn_fwd(qkv, qkv[:, 2 * D_ATTN:].T, cb)
    x2, r2, merged = _mixer_tail_fwd(o, bchf, w["conv_w"], w["g_attn"], w["g_conv"], w_mo, x1, w["ln2_g"], w["ln2_b"])
    x3, r3, g2, u2, h2 = _ffn_fwd(x2, w2i, w2o, w["ln3_g"], w["ln3_b"], "ffn2_fwd")

    grads = {}
    dx3, de, dz, loss, grads["ln4_g"], grads["ln4_b"], grads["b_ple_gate"] = _ple_loss(
        x3, p, tgt, w_g, w_p, w["b_ple_gate"], w["ln4_g"], w["ln4_b"])
    grads["w_ple"] = _matmul_tn(p, de, "dw_ple")
    grads["w_ple_gate"] = _matmul_tn(x3, dz, "dw_ple_gate")

    dx2, dgu2, df2, grads["ln3_g"], grads["ln3_b"] = _ffn_bwd(dx3, r3, g2, u2, w2i, w2o, w["ln3_g"], "ffn2_bwd")
    grads["ffn2_w_in"] = _matmul_tn(x2, dgu2, "dw_ffn2_in")
    grads["ffn2_w_out"] = _matmul_tn(h2, df2, "dw_ffn2_out")

    (dx1a, dr2, dob, delta, dco, grads["ln2_g"], grads["ln2_b"], grads["g_attn"], grads["g_conv"]) = _mixer_tail_bwd(
        dx2, r2, w["ln2_g"], w_mo, o, bchf, w["conv_w"], w["g_attn"], w["g_conv"])
    grads["w_mix_out"] = _matmul_tn(merged, dr2, "dw_mix_out")
    dbch, dcw = _conv_bwd(dco, bchf, w["conv_w"])
    grads["conv_w"] = dcw[:3]
    dq, dk, dv, dck, dcq = _attn_bwd(qkv, dob, cb, alrow, delta.reshape(N_HEADS // 2, 8, -1))
    dcq_lanes = jnp.pad(_from_row_layout(dcq), ((0, 0), (0, N_FLOG - N_HEADS)))
    dfl, dbf = _forget_bwd(dck, dcq_lanes, bchf, fcol, b_f)
    grads["b_forget"] = dbf[:, :N_HEADS]
    dx1 = _mixer_in_bwd(dx1a, dq, dk, dv, dbch, dfl, w_qkv, w_bch, w_f)
    grads["w_mix_in"] = jnp.concatenate(
        [_matmul_tn(x1, dq, "dw_q"), _matmul_tn(x1, dk, "dw_k"), _matmul_tn(x1, dv, "dw_v"),
         _matmul_tn(x1, dfl, "dw_flog")[:, :N_HEADS], _matmul_tn(x1, dbch, "dw_bch")], axis=1)

    dx0, dgu1, df1, grads["ln1_g"], grads["ln1_b"] = _ffn_bwd(dx1, r1, g1, u1, w1i, w1o, w["ln1_g"], "ffn1_bwd")
    grads["ffn1_w_in"] = _matmul_tn(x, dgu1, "dw_ffn1_in")
    grads["ffn1_w_out"] = _matmul_tn(h1, df1, "dw_ffn1_out")
    return loss, dx0, grads


WEIGHTS = ["ffn1_w_in", "ffn1_w_out", "ln1_g", "ln1_b", "w_mix_in", "b_forget", "conv_w", "g_attn", "g_conv",
           "w_mix_out", "ln2_g", "ln2_b", "ffn2_w_in", "ffn2_w_out", "ln3_g", "ln3_b", "w_ple", "w_ple_gate",
           "b_ple_gate", "ln4_g", "ln4_b"]
LAYOUT = {
    "ffn1_w_in": ((D_MODEL, 2 * D_FF), 1), "ffn1_w_out": ((D_FF, D_MODEL), 0),
    "w_mix_in": ((D_MODEL, 3 * D_ATTN + N_HEADS + 3 * D_CONV), 1), "conv_w": ((3, D_CONV), 1),
    "w_mix_out": ((D_MODEL, D_MODEL), 0), "ffn2_w_in": ((D_MODEL, 2 * D_FF), 1), "ffn2_w_out": ((D_FF, D_MODEL), 0),
    "w_ple": ((PLE_DIM, D_MODEL), 1), "w_ple_gate": ((D_MODEL, D_MODEL), 0),
    "ln1_g": ((1, D_MODEL), None), "ln1_b": ((1, D_MODEL), None), "b_forget": ((1, N_HEADS), None),
    "g_attn": ((1, D_ATTN), None), "g_conv": ((1, D_CONV), None), "ln2_g": ((1, D_MODEL), None),
    "ln2_b": ((1, D_MODEL), None), "ln3_g": ((1, D_MODEL), None), "ln3_b": ((1, D_MODEL), None),
    "b_ple_gate": ((1, D_MODEL), None), "ln4_g": ((1, D_MODEL), None), "ln4_b": ((1, D_MODEL), None),
}
BIG = [n for n in WEIGHTS if LAYOUT[n][1] is not None and n != "conv_w"]
SMALL = [n for n in WEIGHTS if n not in BIG]
ROW = 1024
SMALL_ROWS = 16


def _shard_shape(name):
    shape, axis = LAYOUT[name]
    if axis is None:
        return shape
    return tuple(s // N_CHIPS if a == axis else s for a, s in enumerate(shape))


def _halves(a):
    return a.reshape(a.shape[:-2] + (2, a.shape[-2] // 2, a.shape[-1]))


def _split_chips(name, full):
    shape, axis = LAYOUT[name]
    if axis == 0:
        return full.reshape((N_CHIPS, shape[0] // N_CHIPS) + shape[1:])
    return jnp.moveaxis(full.reshape(shape[:1] + (N_CHIPS, shape[1] // N_CHIPS)), 1, 0)


def _join_chips(name, parts):
    shape, axis = LAYOUT[name]
    if axis == 0:
        return parts.reshape(shape)
    return jnp.moveaxis(parts, 0, 1).reshape(shape)


SMALL_AT = {"ln1_g": (0, 0), "ln1_b": (1, 0), "ln2_g": (2, 0), "ln2_b": (3, 0), "ln3_g": (4, 0), "ln3_b": (5, 0),
            "b_ple_gate": (6, 0), "ln4_g": (7, 0), "ln4_b": (8, 0), "g_attn": (9, 0), "g_conv": (9, D_ATTN),
            "b_forget": (10, 0), "conv_w": (10, LANE)}
CONV_SHARD = D_CONV // N_CHIPS


def _pack_small_grads(grads):
    def body(*refs):
        ins, o_ref = dict(zip(SMALL, refs[:-1])), refs[-1]
        o_ref[...] = jnp.zeros_like(o_ref)
        for s in range(N_CHIPS):
            for n in SMALL:
                r, c0 = SMALL_AT[n]
                if n == "conv_w":
                    for k in range(3):
                        o_ref[s, r:r + 1, c0 + k * CONV_SHARD:c0 + (k + 1) * CONV_SHARD] = (
                            ins[n][k:k + 1, s * CONV_SHARD:(s + 1) * CONV_SHARD])
                else:
                    o_ref[s, r:r + 1, c0:c0 + ins[n].shape[1]] = ins[n][...]

    return pl.pallas_call(
        body, name="pack_small_grads",
        out_shape=jax.ShapeDtypeStruct((N_CHIPS, SMALL_ROWS, ROW), F32),
    )(*[grads[n] for n in SMALL])


def _adamw_math(w, g, m, v):
    c1 = 1.0 - ADAM_B1 ** ADAM_STEP
    c2 = 1.0 - ADAM_B2 ** ADAM_STEP
    m = ADAM_B1 * m + (1.0 - ADAM_B1) * g
    v = ADAM_B2 * v + (1.0 - ADAM_B2) * (g * g)
    return -ADAM_LR * ((m / c1) / (jnp.sqrt(v / c2) + ADAM_EPS) + ADAM_WD * w), m, v


def _adamw_small(g_mine, g_sib, c_idx, w, m, v):
    ns = len(SMALL)

    def body(c_ref, gm_ref, gs_ref, *refs):
        ws, ms, vs = refs[:ns], refs[ns:2 * ns], refs[2 * ns:3 * ns]
        outs = refs[3 * ns:]
        mine_first = c_ref[0] == 0
        top = jnp.where(mine_first, gm_ref[...], gs_ref[...])
        bot = jnp.where(mine_first, gs_ref[...], gm_ref[...])
        for i, n in enumerate(SMALL):
            r, c0 = SMALL_AT[n]
            blk, rr = (top, r) if r < SMALL_ROWS // 2 else (bot, r - SMALL_ROWS // 2)
            rows, width = ws[i].shape
            for k in range(rows):
                g = blk[rr:rr + 1, c0 + k * width:c0 + (k + 1) * width]
                d, mn, vn = _adamw_math(ws[i][k:k + 1, :], g, ms[i][k:k + 1, :], vs[i][k:k + 1, :])
                for q, val in enumerate((g, d, mn, vn)):
                    outs[q * ns + i][k:k + 1, :] = val

    shapes = [jax.ShapeDtypeStruct(a.shape, F32) for a in w]
    vmem = pl.BlockSpec(memory_space=pltpu.VMEM)
    res = pl.pallas_call(
        body, name="adamw_small",
        in_specs=[pl.BlockSpec(memory_space=pltpu.SMEM)] + [vmem] * (2 + 3 * ns),
        out_specs=[vmem] * (4 * ns),
        out_shape=shapes * 4,
    )(c_idx, g_mine, g_sib, *w, *m, *v)
    return [res[q * ns:(q + 1) * ns] for q in range(4)]


def _place():
    x, y, c = lax.axis_index("x"), lax.axis_index("y"), lax.axis_index("c")
    others = [(1 - x, y), (x, 1 - y), (1 - x, 1 - y)]
    return x, y, c, others


ANY_SPEC = pl.BlockSpec(memory_space=pl.ANY)


def _remote(src, dst, send_sems, recv_sems, k, to):
    return pltpu.make_async_remote_copy(src_ref=src, dst_ref=dst, send_sem=send_sems.at[k], recv_sem=recv_sems.at[k],
                                        device_id=to, device_id_type=MESH)


def _all_gather(shards):
    n = len(shards)

    def body(*refs):
        ins, outs, send_sems, recv_sems = refs[:n], refs[n:2 * n], refs[2 * n], refs[2 * n + 1]
        x, y, c, others = _place()
        s = 2 * x + y
        slot = lambda t, chip, half: outs[t].at[2 * chip[0] + chip[1], half]
        first = [_remote(ins[t].at[c], outs[t].at[s, c], send_sems, recv_sems, 6 * t + j, (*chip, c))
                 for t in range(n) for j, chip in enumerate(others)]
        for cp in first:
            cp.start()
        passed = []
        for t in range(n):
            for j, chip in enumerate(others):
                landed = slot(t, chip, c)
                _remote(landed, landed, send_sems, recv_sems, 6 * t + j, (x, y, c)).wait_recv()
                passed.append(_remote(landed, landed, send_sems, recv_sems, 6 * t + 3 + j, (x, y, 1 - c)))
                passed[-1].start()
        for t in range(n):
            for j, chip in enumerate(others):
                landed = slot(t, chip, 1 - c)
                _remote(landed, landed, send_sems, recv_sems, 6 * t + 3 + j, (x, y, c)).wait_recv()
        for cp in first + passed:
            cp.wait_send()

    return pl.pallas_call(
        body, name="all_gather_weights",
        out_shape=[jax.ShapeDtypeStruct((N_CHIPS,) + a.shape, a.dtype) for a in shards],
        in_specs=[ANY_SPEC] * n, out_specs=[ANY_SPEC] * n,
        scratch_shapes=[pltpu.SemaphoreType.DMA((6 * n,)), pltpu.SemaphoreType.DMA((6 * n,))],
    )(*shards)


def _swap_halves(gs):
    n = len(gs)

    def body(*refs):
        ins, outs, send_sems, recv_sems = refs[:n], refs[n:2 * n], refs[2 * n], refs[2 * n + 1]
        x, y, c, _ = _place()
        copies = [_remote(ins[t].at[:, 1 - c], outs[t], send_sems, recv_sems, t, (x, y, 1 - c)) for t in range(n)]
        for cp in copies:
            cp.start()
        for cp in copies:
            cp.wait()

    return pl.pallas_call(
        body, name="grad_swap_halves",
        out_shape=[jax.ShapeDtypeStruct(g.shape[:1] + g.shape[2:], g.dtype) for g in gs],
        in_specs=[ANY_SPEC] * n, out_specs=[ANY_SPEC] * n,
        scratch_shapes=[pltpu.SemaphoreType.DMA((n,)), pltpu.SemaphoreType.DMA((n,))],
    )(*gs)


def _exchange_chips(pps):
    n = len(pps)

    def body(*refs):
        ins, outs, send_sems, recv_sems = refs[:n], refs[n:2 * n], refs[2 * n], refs[2 * n + 1]
        x, y, c, others = _place()
        s = 2 * x + y
        sends = [_remote(ins[t].at[2 * chip[0] + chip[1]], outs[t].at[s], send_sems, recv_sems, 3 * t + j, (*chip, c))
                 for t in range(n) for j, chip in enumerate(others)]
        for cp in sends:
            cp.start()
        for t in range(n):
            for j, chip in enumerate(others):
                landed = outs[t].at[2 * chip[0] + chip[1]]
                _remote(landed, landed, send_sems, recv_sems, 3 * t + j, (x, y, c)).wait_recv()
        for cp in sends:
            cp.wait_send()

    return pl.pallas_call(
        body, name="grad_exchange_chips",
        out_shape=[jax.ShapeDtypeStruct(p.shape, p.dtype) for p in pps],
        in_specs=[ANY_SPEC] * n, out_specs=[ANY_SPEC] * n,
        scratch_shapes=[pltpu.SemaphoreType.DMA((3 * n,)), pltpu.SemaphoreType.DMA((3 * n,))],
    )(*pps)


def _share_half(rs):
    n = len(rs)

    def body(*refs):
        ins, outs, send_sems, recv_sems = refs[:n], refs[n:2 * n], refs[2 * n], refs[2 * n + 1]
        x, y, c, _ = _place()
        copies = [_remote(ins[t], outs[t], send_sems, recv_sems, t, (x, y, 1 - c)) for t in range(n)]
        for cp in copies:
            cp.start()
        for cp in copies:
            cp.wait()

    return pl.pallas_call(
        body, name="grad_share_half",
        out_shape=[jax.ShapeDtypeStruct(r.shape, r.dtype) for r in rs],
        in_specs=[ANY_SPEC] * n, out_specs=[ANY_SPEC] * n,
        scratch_shapes=[pltpu.SemaphoreType.DMA((n,)), pltpu.SemaphoreType.DMA((n,))],
    )(*rs)


ELEMENTWISE_BLOCK_BYTES = 1 << 20


def _row_tile(rows, cols):
    return _tile(rows, max(8, ELEMENTWISE_BLOCK_BYTES // (4 * cols) // 8 * 8))


def _add_my_half(g, sib, c_idx, name):
    rh, cols = g.shape[2:]
    tr = _row_tile(rh, cols)

    def body(c_ref, g_ref, s_ref, o_ref):
        o_ref[...] = (g_ref[...] + s_ref[...]).astype(BF16)

    return pl.pallas_call(
        body, name="grad_add_halves_" + name,
        grid_spec=pltpu.PrefetchScalarGridSpec(
            num_scalar_prefetch=1, grid=(N_CHIPS, rh // tr),
            in_specs=[pl.BlockSpec((None, None, tr, cols), lambda s, i, c: (s, c[0], i, 0)),
                      pl.BlockSpec((None, tr, cols), lambda s, i, c: (s, i, 0))],
            out_specs=pl.BlockSpec((None, tr, cols), lambda s, i, c: (s, i, 0))),
        out_shape=jax.ShapeDtypeStruct((N_CHIPS, rh, cols), BF16),
        compiler_params=_cp(2),
    )(c_idx, g, sib)


def _sum_chips(parts, pp, s_idx, name):
    rh, cols = parts.shape[1:]
    tr = _row_tile(rh, cols)

    def body(s_ref, p0, p1, p2, p3, mine_ref, o_ref):
        own = mine_ref[...]
        t = [jnp.where(s_ref[0] == k, own, p[...]).astype(F32) for k, p in enumerate((p0, p1, p2, p3))]
        o_ref[...] = ((t[0] + t[1]) + t[2]) + t[3]

    slot = lambda k: pl.BlockSpec((None, tr, cols), lambda i, s: (jnp.where(s[0] == k, (k + 1) % N_CHIPS, k), i, 0))
    return pl.pallas_call(
        body, name="grad_sum_chips_" + name,
        grid_spec=pltpu.PrefetchScalarGridSpec(
            num_scalar_prefetch=1, grid=(rh // tr,),
            in_specs=[slot(0), slot(1), slot(2), slot(3), pl.BlockSpec((None, tr, cols), lambda i, s: (s[0], i, 0))],
            out_specs=pl.BlockSpec((tr, cols), lambda i, s: (i, 0))),
        out_shape=jax.ShapeDtypeStruct((rh, cols), F32),
        compiler_params=_cp(1),
    )(s_idx, parts, parts, parts, parts, pp)


def _adamw(w, g_mine, g_sib, m, v, c_idx, name):
    rows, cols = w.shape
    tr = _row_tile(rows // 2, cols)
    nbh = rows // 2 // tr

    def body(c_ref, w_ref, gm_ref, gs_ref, m_ref, v_ref, g_ref, d_ref, mo_ref, vo_ref):
        g = jnp.where(pl.program_id(0) // nbh == c_ref[0], gm_ref[...], gs_ref[...])
        g_ref[...] = g
        d_ref[...], mo_ref[...], vo_ref[...] = _adamw_math(w_ref[...], g, m_ref[...], v_ref[...])

    spec = pl.BlockSpec((tr, cols), lambda i, c: (i, 0))
    half = pl.BlockSpec((tr, cols), lambda i, c: (i % nbh, 0))
    return pl.pallas_call(
        body, name="adamw_" + name,
        grid_spec=pltpu.PrefetchScalarGridSpec(
            num_scalar_prefetch=1, grid=(rows // tr,),
            in_specs=[spec, half, half, spec, spec], out_specs=[spec] * 4),
        out_shape=[jax.ShapeDtypeStruct(w.shape, F32)] * 4,
        compiler_params=_cp(1),
    )(c_idx, w, g_mine, g_sib, m, v)


def kernel(x, p, ffn1_w_in, ffn1_w_out, ln1_g, ln1_b, w_mix_in, b_forget, conv_w, g_attn, g_conv, w_mix_out, ln2_g, ln2_b, ffn2_w_in, ffn2_w_out, ln3_g, ln3_b, w_ple, w_ple_gate, b_ple_gate, ln4_g, ln4_b, loss_target, m_ffn1_w_in, m_ffn1_w_out, m_ln1_g, m_ln1_b, m_w_mix_in, m_b_forget, m_conv_w, m_g_attn, m_g_conv, m_w_mix_out, m_ln2_g, m_ln2_b, m_ffn2_w_in, m_ffn2_w_out, m_ln3_g, m_ln3_b, m_w_ple, m_w_ple_gate, m_b_ple_gate, m_ln4_g, m_ln4_b, v_ffn1_w_in, v_ffn1_w_out, v_ln1_g, v_ln1_b, v_w_mix_in, v_b_forget, v_conv_w, v_g_attn, v_g_conv, v_w_mix_out, v_ln2_g, v_ln2_b, v_ffn2_w_in, v_ffn2_w_out, v_ln3_g, v_ln3_b, v_w_ple, v_w_ple_gate, v_b_ple_gate, v_ln4_g, v_ln4_b):
    args = dict(locals())
    shard = {n: args[n][0] if LAYOUT[n][1] is not None else args[n] for n in WEIGHTS}
    m_shard = {n: args["m_" + n][0] if LAYOUT[n][1] is not None else args["m_" + n] for n in WEIGHTS}
    v_shard = {n: args["v_" + n][0] if LAYOUT[n][1] is not None else args["v_" + n] for n in WEIGHTS}
    c_idx = lax.axis_index("c").astype(jnp.int32).reshape(1)
    chip = (2 * lax.axis_index("x") + lax.axis_index("y")).astype(jnp.int32)

    conv_rows = SMALL_ROWS - shard["conv_w"].shape[0]
    mine = [_halves(shard[n].astype(BF16)) for n in BIG] + [_halves(jnp.pad(shard["conv_w"], ((0, conv_rows), (0, 0))))]
    gathered = [lax.dynamic_update_slice(theirs, own[None], (chip, 0, 0, 0))
                for theirs, own in zip(_all_gather(mine), mine)]
    full = {n: _join_chips(n, g.reshape((N_CHIPS,) + _shard_shape(n))) for n, g in zip(BIG, gathered)}
    full["conv_w"] = _join_chips("conv_w", gathered[-1].reshape(N_CHIPS, SMALL_ROWS, CONV_SHARD)[:, :3])
    full.update({n: shard[n] for n in SMALL if n != "conv_w"})

    loss_acc, grad_x, grads = _local_step(x[0], p[0, 0], loss_target[0], full)
    loss = lax.psum(loss_acc[0, 0], ("x", "y", "c"))

    names = BIG + ["small"]
    per_chip = [_halves(_split_chips(n, grads[n])) for n in BIG] + [_halves(_pack_small_grads(grads))]
    chip_sums = [_add_my_half(g, sib, c_idx, n) for n, g, sib in zip(names, per_chip, _swap_halves(per_chip))]
    my_half = [_sum_chips(parts, own, chip.reshape(1), n)
               for n, parts, own in zip(names, _exchange_chips(chip_sums), chip_sums)]
    sib_half = _share_half(my_half)

    out = {}
    for n, gm, gs in zip(BIG, my_half, sib_half):
        out[n] = [a[None] for a in _adamw(shard[n], gm, gs, m_shard[n], v_shard[n], c_idx, n)]
    small = _adamw_small(my_half[-1], sib_half[-1], c_idx, [shard[n] for n in SMALL], [m_shard[n] for n in SMALL],
                         [v_shard[n] for n in SMALL])
    for i, n in enumerate(SMALL):
        out[n] = [small[q][i][None] if n == "conv_w" else small[q][i] for q in range(4)]
    return (loss, grad_x[None], *[out[n][q] for q in range(4) for n in WEIGHTS])
```

```python
import functools
import math

import jax
import jax.numpy as jnp
from jax import lax
from jax.experimental import pallas as pl
from jax.experimental.pallas import tpu as pltpu

F32 = jnp.float32
BF16 = jnp.bfloat16

D_MODEL = 1024
D_FF = 2816
N_HEADS = 8
HEAD_DIM = 64
D_ATTN = N_HEADS * HEAD_DIM
D_CONV = 512
PLE_DIM = 256
N_FLOG = 128
ALPHA = 2.0 ** 0.25
LN_EPS = 1e-5
RMS_EPS = 1e-6
NEG_INF = -1e30
Q_SCALE = 1.0 / math.sqrt(HEAD_DIM)
LOG2E = math.log2(math.e)

ADAM_LR = 0.001
ADAM_B1 = 0.9
ADAM_B2 = 0.999
ADAM_EPS = 1e-08
ADAM_WD = 0.01
ADAM_STEP = 10

V7X_VMEM_BYTES = 64 << 20
VMEM_LIMIT = V7X_VMEM_BYTES - (8 << 20)
LANE = 128
FF_CHUNK = 256
N_CHIPS = 4
MESH = pl.DeviceIdType.MESH


def _cp(n_axes):
    return pltpu.CompilerParams(dimension_semantics=("arbitrary",) * n_axes, vmem_limit_bytes=VMEM_LIMIT)


def _resident(shape):
    n = len(shape)
    return pl.BlockSpec(shape, lambda *_: (0,) * n, pipeline_mode=pl.Buffered(1))


def _nn(a, b):
    return jnp.dot(a, b, preferred_element_type=F32)


def _nt(a, b):
    return lax.dot_general(a, b, (((1,), (1,)), ((), ())), preferred_element_type=F32)


def _tn(a, b):
    return lax.dot_general(a, b, (((0,), (0,)), ((), ())), preferred_element_type=F32)


def _ln_stats(r):
    mu = jnp.mean(r, axis=-1, keepdims=True)
    xc = r - mu
    var = jnp.mean(xc * xc, axis=-1, keepdims=True)
    rstd = lax.rsqrt(var + LN_EPS)
    return xc * rstd, rstd


def _ln_bwd(dy, xhat, rstd, g):
    dxh = dy * g
    m1 = jnp.mean(dxh, axis=-1, keepdims=True)
    m2 = jnp.mean(dxh * xhat, axis=-1, keepdims=True)
    return rstd * (dxh - m1 - xhat * m2)


def _sigmoid(z):
    return 1.0 / (1.0 + jnp.exp(-z))


def _rowsum(a):
    return jnp.sum(a, axis=0, keepdims=True)


def _tile(total, want):
    if total <= want:
        return total
    for t in range(want - want % 8, 0, -8):
        if total % t == 0:
            return t
    raise ValueError((total, want))


def _ffn_fwd(x, w_in, w_out, lg, lb, name):
    T = x.shape[0]
    tm = _tile(T, 512)
    nf = D_FF // FF_CHUNK

    def body(x_ref, wi_ref, wo_ref, lg_ref, lb_ref, xo_ref, r_ref, g_ref, u_ref, h_ref):
        xf = x_ref[...]
        xb = xf.astype(BF16)
        acc = jnp.zeros((tm, D_MODEL), F32)
        for j in range(nf):
            c0 = j * FF_CHUNK
            g = _nn(xb, wi_ref[:, c0:c0 + FF_CHUNK])
            u = _nn(xb, wi_ref[:, D_FF + c0:D_FF + c0 + FF_CHUNK])
            hb = (g * _sigmoid(g) * u).astype(BF16)
            g_ref[:, c0:c0 + FF_CHUNK] = g.astype(BF16)
            u_ref[:, c0:c0 + FF_CHUNK] = u.astype(BF16)
            h_ref[:, c0:c0 + FF_CHUNK] = hb
            acc = acc + _nn(hb, wo_ref[c0:c0 + FF_CHUNK, :])
        r = ALPHA * xf + 0.5 * acc
        r_ref[...] = r
        xhat, _ = _ln_stats(r)
        xo_ref[...] = xhat * lg_ref[...] + lb_ref[...]

    row = lambda n: pl.BlockSpec((tm, n), lambda i: (i, 0))
    return pl.pallas_call(
        body, name=name, grid=(T // tm,),
        in_specs=[row(D_MODEL), _resident((D_MODEL, 2 * D_FF)), _resident((D_FF, D_MODEL)),
                  _resident((1, D_MODEL)), _resident((1, D_MODEL))],
        out_specs=[row(D_MODEL), row(D_MODEL), row(D_FF), row(D_FF), row(D_FF)],
        out_shape=[jax.ShapeDtypeStruct((T, D_MODEL), F32), jax.ShapeDtypeStruct((T, D_MODEL), F32),
                   jax.ShapeDtypeStruct((T, D_FF), BF16), jax.ShapeDtypeStruct((T, D_FF), BF16),
                   jax.ShapeDtypeStruct((T, D_FF), BF16)],
        compiler_params=_cp(1),
    )(x, w_in, w_out, lg, lb)


def _ffn_bwd(dxo, r, g, u, w_in, w_out, lg, name):
    T = r.shape[0]
    tm = _tile(T, 256)
    nf = D_FF // FF_CHUNK

    def body(dxo_ref, r_ref, g_ref, u_ref, wi_ref, wo_ref, lg_ref, dx_ref, dgu_ref, df_ref, dlg_ref, dlb_ref):
        i = pl.program_id(0)
        dy = dxo_ref[...]
        xhat, rstd = _ln_stats(r_ref[...])
        dr = _ln_bwd(dy, xhat, rstd, lg_ref[...])

        @pl.when(i == 0)
        def _():
            dlg_ref[...] = jnp.zeros_like(dlg_ref)
            dlb_ref[...] = jnp.zeros_like(dlb_ref)

        dlg_ref[...] += _rowsum(dy * xhat)
        dlb_ref[...] += _rowsum(dy)
        dfb = (0.5 * dr).astype(BF16)
        df_ref[...] = dfb
        acc = jnp.zeros((tm, D_MODEL), F32)
        dh_ahead = _nt(dfb, wo_ref[0:FF_CHUNK, :])
        for j in range(nf):
            c0 = j * FF_CHUNK
            dh = dh_ahead
            if j + 1 < nf:
                dh_ahead = _nt(dfb, wo_ref[c0 + FF_CHUNK:c0 + 2 * FF_CHUNK, :])
            gg = g_ref[:, c0:c0 + FF_CHUNK].astype(F32)
            uu = u_ref[:, c0:c0 + FF_CHUNK].astype(F32)
            s = _sigmoid(gg)
            dgb = (dh * uu * s * (1.0 + gg * (1.0 - s))).astype(BF16)
            dub = (dh * gg * s).astype(BF16)
            dgu_ref[:, c0:c0 + FF_CHUNK] = dgb
            dgu_ref[:, D_FF + c0:D_FF + c0 + FF_CHUNK] = dub
            acc = acc + _nt(dgb, wi_ref[:, c0:c0 + FF_CHUNK]) + _nt(dub, wi_ref[:, D_FF + c0:D_FF + c0 + FF_CHUNK])
        dx_ref[...] = ALPHA * dr + acc

    row = lambda n: pl.BlockSpec((tm, n), lambda i: (i, 0))
    return pl.pallas_call(
        body, name=name, grid=(T // tm,),
        in_specs=[row(D_MODEL), row(D_MODEL), row(D_FF), row(D_FF), _resident((D_MODEL, 2 * D_FF)),
                  _resident((D_FF, D_MODEL)), _resident((1, D_MODEL))],
        out_specs=[row(D_MODEL), row(2 * D_FF), row(D_MODEL), _resident((1, D_MODEL)), _resident((1, D_MODEL))],
        out_shape=[jax.ShapeDtypeStruct((T, D_MODEL), F32), jax.ShapeDtypeStruct((T, 2 * D_FF), BF16),
                   jax.ShapeDtypeStruct((T, D_MODEL), BF16), jax.ShapeDtypeStruct((1, D_MODEL), F32),
                   jax.ShapeDtypeStruct((1, D_MODEL), F32)],
        compiler_params=_cp(1),
    )(dxo, r, g, u, w_in, w_out, lg)


def _matmul_tn(a, b, name, tn=None):
    T, K = a.shape
    N = b.shape[1]
    tt = _tile(T, 1024)
    if tn is None:
        tn = N
        while K * tn * 4 > (6 << 20) and tn % 256 == 0:
            tn //= 2
    assert N % tn == 0

    def body(a_ref, b_ref, o_ref):
        @pl.when(pl.program_id(1) == 0)
        def _():
            o_ref[...] = jnp.zeros_like(o_ref)

        o_ref[...] += _tn(a_ref[...].astype(BF16), b_ref[...].astype(BF16))

    return pl.pallas_call(
        body, name=name, grid=(N // tn, T // tt),
        in_specs=[pl.BlockSpec((tt, K), lambda n, t: (t, 0)), pl.BlockSpec((tt, tn), lambda n, t: (t, n))],
        out_specs=pl.BlockSpec((K, tn), lambda n, t: (0, n)),
        out_shape=jax.ShapeDtypeStruct((K, N), F32),
        compiler_params=_cp(2),
    )(a, b)


def _matmul_tokens(at, b, name):
    M, T = at.shape
    N = b.shape[1]
    tt = _tile(T, 1024)

    def body(a_ref, b_ref, o_ref):
        @pl.when(pl.program_id(0) == 0)
        def _():
            o_ref[...] = jnp.zeros_like(o_ref)

        o_ref[...] += _nn(a_ref[...].astype(BF16), b_ref[...].astype(BF16))

    return pl.pallas_call(
        body, name=name, grid=(T // tt,),
        in_specs=[pl.BlockSpec((M, tt), lambda t: (0, t)), pl.BlockSpec((tt, N), lambda t: (t, 0))],
        out_specs=pl.BlockSpec((M, N), lambda t: (0, 0)),
        out_shape=jax.ShapeDtypeStruct((M, N), F32),
        compiler_params=_cp(1),
    )(at, b)


def _matmul_nn(x, w, scale, out_dtype, name):
    T, K = x.shape
    N = w.shape[1]
    tm = _tile(T, 512)

    def body(x_ref, w_ref, s_ref, o_ref):
        o_ref[...] = (_nn(x_ref[...].astype(BF16), w_ref[...]) * s_ref[...]).astype(out_dtype)

    return pl.pallas_call(
        body, name=name, grid=(T // tm,),
        in_specs=[pl.BlockSpec((tm, K), lambda i: (i, 0)), _resident((K, N)), _resident((1, N))],
        out_specs=pl.BlockSpec((tm, N), lambda i: (i, 0)),
        out_shape=jax.ShapeDtypeStruct((T, N), out_dtype),
        compiler_params=_cp(1),
    )(x, w, scale)


def _log_sigmoid(z):
    return jnp.minimum(z, 0.0) - jnp.log1p(jnp.exp(-jnp.abs(z)))


def _tri(n, lower):
    r = lax.broadcasted_iota(jnp.int32, (n, n), 0)
    c = lax.broadcasted_iota(jnp.int32, (n, n), 1)
    return jnp.where((c <= r) if lower else (c >= r), 1.0, 0.0).astype(F32)


def _f32dot(a, b):
    return jnp.dot(a, b, preferred_element_type=F32, precision=lax.Precision.HIGHEST)


def _forget_cumsum(flog, col, bf):
    T = flog.shape[0]
    bt = _tile(T, 512)

    def body(f_ref, b_ref, c_ref, carry):
        @pl.when(pl.program_id(0) == 0)
        def _():
            carry[...] = jnp.zeros_like(carry)

        lf = _log_sigmoid(f_ref[...] + b_ref[...])
        c = _f32dot(_tri(bt, True), lf) + carry[...]
        c_ref[...] = c * LOG2E
        carry[...] = c[bt - 1:bt, :]

    return pl.pallas_call(
        body, name="forget_cumsum", grid=(T // bt,),
        in_specs=[pl.BlockSpec((bt, N_FLOG), lambda i: (i, col)), _resident((1, N_FLOG))],
        out_specs=pl.BlockSpec((bt, N_FLOG), lambda i: (i, 0)),
        out_shape=jax.ShapeDtypeStruct((T, N_FLOG), F32),
        scratch_shapes=[pltpu.VMEM((1, N_FLOG), F32)],
        compiler_params=_cp(1),
    )(flog, bf)


def _forget_bwd(dck, dcq, flog, col, bf):
    T = dcq.shape[0]
    bt = _tile(T, 512)
    nb = T // bt

    def body(k0_ref, k1_ref, k2_ref, k3_ref, dcq_ref, f_ref, b_ref, dz_ref, db_ref, carry):
        @pl.when(pl.program_id(0) == 0)
        def _():
            carry[...] = jnp.zeros_like(carry)
            db_ref[...] = jnp.zeros_like(db_ref)

        dc = ((k0_ref[...] + k1_ref[...]) + (k2_ref[...] + k3_ref[...])) + dcq_ref[...]
        dlf = _f32dot(_tri(bt, False), dc) + carry[...]
        carry[...] = dlf[0:1, :]
        z = f_ref[...] + b_ref[...]
        dz = dlf * _sigmoid(-z)
        dz_ref[...] = dz.astype(BF16)
        db_ref[...] += _rowsum(dz)

    slab = lambda j: pl.BlockSpec((None, bt, N_FLOG), lambda i: (j, nb - 1 - i, 0))
    return pl.pallas_call(
        body, name="forget_bwd", grid=(nb,),
        in_specs=[slab(0), slab(1), slab(2), slab(3),
                  pl.BlockSpec((bt, N_FLOG), lambda i: (nb - 1 - i, 0)),
                  pl.BlockSpec((bt, N_FLOG), lambda i: (nb - 1 - i, col)), _resident((1, N_FLOG))],
        out_specs=[pl.BlockSpec((bt, N_FLOG), lambda i: (nb - 1 - i, 0)), _resident((1, N_FLOG))],
        out_shape=[jax.ShapeDtypeStruct((T, N_FLOG), BF16), jax.ShapeDtypeStruct((1, N_FLOG), F32)],
        scratch_shapes=[pltpu.VMEM((1, N_FLOG), F32)],
        compiler_params=_cp(1),
    )(dck, dck, dck, dck, dcq, flog, bf)


def _head_masks():
    lane = lax.broadcasted_iota(jnp.int32, (1, LANE), 1)
    return lane < HEAD_DIM


def _split_heads(x2, is_a):
    zero = jnp.zeros_like(x2)
    return jnp.where(is_a, x2, zero), jnp.where(is_a, zero, x2)


BIAS_PARTS = 3


def _bias_lanes(h):
    lane = lax.broadcasted_iota(jnp.int32, (1, LANE), 1)
    first = (1 - h) * HEAD_DIM
    return lane, first


def _fold_key_bias(qkv, c):
    T = qkv.shape[0]
    tm = _tile(T, 512)
    npair = N_HEADS // 2

    def body(k_ref, c_ref, o_ref):
        cc = c_ref[...]
        parts, rest = [], cc
        for _ in range(BIAS_PARTS):
            piece = rest.astype(BF16)
            parts.append(piece)
            rest = rest - piece.astype(F32)
        for j in range(npair):
            k2 = k_ref[:, j * LANE:(j + 1) * LANE]
            for h in range(2):
                lane, first = _bias_lanes(h)
                out = k2
                for n, piece in enumerate(parts):
                    col = piece[:, 2 * j + h:2 * j + h + 1]
                    out = jnp.where(lane == first + n, col, out)
                o_ref[:, (2 * j + h) * LANE:(2 * j + h + 1) * LANE] = out

    return pl.pallas_call(
        body, name="fold_key_bias", grid=(T // tm,),
        in_specs=[pl.BlockSpec((tm, D_ATTN), lambda i: (i, 1)), pl.BlockSpec((tm, N_FLOG), lambda i: (i, 0))],
        out_specs=pl.BlockSpec((tm, 2 * D_ATTN), lambda i: (i, 0)),
        out_shape=jax.ShapeDtypeStruct((T, 2 * D_ATTN), BF16),
        compiler_params=_cp(1),
    )(qkv, c)


def _attn_fwd(qkv, vt, kb):
    T = qkv.shape[0]
    tq = _tile(T, 512)
    tk = tq
    nq = T // tq
    npair = N_HEADS // 2

    def body(q_ref, ka_ref, kb_ref, vt_ref, o_ref, al_ref, m_s, l_s, acc_s):
        i = pl.program_id(1)
        qs = []
        for h, qh in enumerate(_split_heads(q_ref[...], _head_masks())):
            lane, first = _bias_lanes(h)
            qs.append(jnp.where((lane >= first) & (lane < first + BIAS_PARTS), -1.0, qh).astype(BF16))
        k_refs = (ka_ref, kb_ref)
        m_s[...] = jnp.full_like(m_s, NEG_INF)
        l_s[...] = jnp.zeros_like(l_s)
        acc_s[...] = jnp.zeros_like(acc_s)

        def scores_at(kk):
            k0 = pl.multiple_of(kk * tk, tk)
            return tuple(_nt(k_refs[h][pl.ds(k0, tk), :], qs[h]) for h in range(2))

        def consume(kk, scores, masked):
            k0 = pl.multiple_of(kk * tk, tk)
            v2t = vt_ref[:, pl.ds(k0, tk)]
            for h in range(2):
                zt = scores[h]
                if masked:
                    rr = lax.broadcasted_iota(jnp.int32, (tk, tq), 0)
                    cc = lax.broadcasted_iota(jnp.int32, (tk, tq), 1)
                    zt = jnp.where(cc >= rr, zt, NEG_INF)
                m_old = m_s[h]
                m_new = jnp.maximum(m_old, jnp.max(zt, axis=0, keepdims=True))
                p = jnp.exp2(zt - m_new)
                a = jnp.exp2(m_old - m_new)
                l_s[h] = a * l_s[h] + jnp.sum(p, axis=0, keepdims=True)
                acc_s[h] = a * acc_s[h] + _nn(v2t, p.astype(BF16))
                m_s[h] = m_new

        def two_tiles(kk, second_masked):
            first, second = scores_at(kk), scores_at(kk + 1)
            consume(kk, first, False)
            consume(kk + 1, second, second_masked)

        def loop_body(t, carry):
            two_tiles(2 * t, False)
            return carry

        lax.fori_loop(0, i // 2, loop_body, 0)

        @pl.when(i % 2 == 1)
        def _():
            two_tiles(i - 1, True)

        @pl.when(i % 2 == 0)
        def _():
            consume(i, scores_at(i), True)

        outs = []
        for h in range(2):
            l = l_s[h]
            outs.append(acc_s[h] * (1.0 / l))
            al_ref[0, h:h + 1, :] = -(m_s[h] + jnp.log2(l))
        al_ref[0, 2:8, :] = jnp.zeros((6, tq), F32)
        dim = lax.broadcasted_iota(jnp.int32, (LANE, 1), 0)
        o_ref[...] = jnp.where(dim < HEAD_DIM, outs[0], outs[1]).T

    rowl = pl.BlockSpec((1, 8, tq), lambda j, i: (j, 0, i))
    return pl.pallas_call(
        body, name="attn_fwd", grid=(npair, nq),
        in_specs=[pl.BlockSpec((tq, LANE), lambda j, i: (i, j)),
                  pl.BlockSpec((T, LANE), lambda j, i: (0, 2 * j), pipeline_mode=pl.Buffered(1)),
                  pl.BlockSpec((T, LANE), lambda j, i: (0, 2 * j + 1), pipeline_mode=pl.Buffered(1)),
                  pl.BlockSpec((LANE, T), lambda j, i: (2 * npair + j, 0), pipeline_mode=pl.Buffered(1))],
        out_specs=[pl.BlockSpec((tq, LANE), lambda j, i: (i, j)), rowl],
        out_shape=[jax.ShapeDtypeStruct((T, D_ATTN), F32), jax.ShapeDtypeStruct((npair, 8, T), F32)],
        scratch_shapes=[pltpu.VMEM((2, 1, tq), F32), pltpu.VMEM((2, 1, tq), F32), pltpu.VMEM((2, LANE, tq), F32)],
        compiler_params=_cp(2),
    )(qkv, kb, kb, vt)


def _attn_bwd(qkv, qkvt, dob, dobt, cb, alrow, dlrow):
    T = qkv.shape[0]
    tq = _tile(T, 512)
    tk = tq
    nq = T // tq
    npair = N_HEADS // 2

    def body(q_ref, qt_ref, k_ref, kt_ref, v_ref, do_ref, dot_ref, cb_ref, al_ref, dl_ref,
             dq_ref, dk_ref, dv_ref, dc_ref, dcq_ref, dk_s, dv_s, dc_s):
        kj = pl.program_id(1)
        is_a = _head_masks()
        ks = _split_heads(k_ref[...], is_a)
        vs = _split_heads(v_ref[...], is_a)
        dim_a = lax.broadcasted_iota(jnp.int32, (LANE, 1), 0) < HEAD_DIM
        kts = _split_heads(kt_ref[...], dim_a)
        cs = (cb_ref[:, 0:1], cb_ref[:, HEAD_DIM:HEAD_DIM + 1])

        @pl.when(kj == 0)
        def _():
            dq_ref[...] = jnp.zeros_like(dq_ref)
            dcq_ref[...] = jnp.zeros_like(dcq_ref)

        dk_s[...] = jnp.zeros_like(dk_s)
        dv_s[...] = jnp.zeros_like(dv_s)
        dc_s[...] = jnp.zeros_like(dc_s)

        def step(qi, masked):
            q0 = pl.multiple_of(qi * tq, tq)
            q2 = q_ref[pl.ds(q0, tq), :]
            do2 = do_ref[pl.ds(q0, tq), :]
            qt2 = qt_ref[:, pl.ds(q0, tq)]
            dot2 = dot_ref[:, pl.ds(q0, tq)]
            for h in range(2):
                alr = al_ref[0, h:h + 1, pl.ds(q0, tq)]
                dlr = dl_ref[0, h:h + 1, pl.ds(q0, tq)]
                zt = _nt(ks[h], q2) + (alr - cs[h])
                if masked:
                    rr = lax.broadcasted_iota(jnp.int32, (tk, tq), 0)
                    cc = lax.broadcasted_iota(jnp.int32, (tk, tq), 1)
                    zt = jnp.where(cc >= rr, zt, NEG_INF)
                pt = jnp.exp2(zt)
                dst = pt * (_nt(vs[h], do2) - dlr)
                pb = pt.astype(BF16)
                dsb = dst.astype(BF16)
                dv_s[h] += _nt(dot2, pb)
                dk_s[h] += _nt(qt2, dsb)
                dc_s[h] += jnp.sum(dst, axis=-1, keepdims=True)
                dcq_ref[0, h:h + 1, pl.ds(q0, tq)] += jnp.sum(dst, axis=0, keepdims=True)
                dq_ref[:, pl.ds(q0, tq)] += _nn(kts[h], dsb)

        step(kj, True)

        def loop_body(qi, carry):
            step(qi, False)
            return carry

        lax.fori_loop(kj + 1, nq, loop_body, 0)
        dk_ref[...] = (jnp.where(dim_a, dk_s[0], dk_s[1]) * (1.0 / LOG2E)).astype(BF16)
        dv_ref[...] = jnp.where(dim_a, dv_s[0], dv_s[1]).astype(BF16)
        lane = lax.broadcasted_iota(jnp.int32, (1, LANE), 1)
        head = 2 * pl.program_id(0)
        dc_ref[...] = jnp.where(lane == head, -dc_s[0], jnp.where(lane == head + 1, -dc_s[1], 0.0))

        @pl.when(kj == nq - 1)
        def _():
            dq_ref[...] = dq_ref[...] * Q_SCALE

    full = lambda col: pl.BlockSpec((T, LANE), lambda j, kj: (0, col(j)), pipeline_mode=pl.Buffered(1))
    fullt = lambda row: pl.BlockSpec((LANE, T), lambda j, kj: (row(j), 0), pipeline_mode=pl.Buffered(1))
    tile = lambda col: pl.BlockSpec((tk, LANE), lambda j, kj: (kj, col(j)))
    tilet = lambda row: pl.BlockSpec((LANE, tk), lambda j, kj: (row(j), kj))
    rowl = pl.BlockSpec((1, 8, T), lambda j, kj: (j, 0, 0))
    return pl.pallas_call(
        body, name="attn_bwd", grid=(npair, nq),
        in_specs=[full(lambda j: j), fullt(lambda j: j), tile(lambda j: npair + j), tilet(lambda j: npair + j),
                  tile(lambda j: 2 * npair + j), full(lambda j: j), fullt(lambda j: j), tile(lambda j: j), rowl, rowl],
        out_specs=[pl.BlockSpec((LANE, T), lambda j, kj: (j, 0)), tilet(lambda j: j), tilet(lambda j: j),
                   pl.BlockSpec((None, tk, LANE), lambda j, kj: (j, kj, 0)), rowl],
        out_shape=[jax.ShapeDtypeStruct((D_ATTN, T), F32), jax.ShapeDtypeStruct((D_ATTN, T), BF16),
                   jax.ShapeDtypeStruct((D_ATTN, T), BF16), jax.ShapeDtypeStruct((npair, T, LANE), F32),
                   jax.ShapeDtypeStruct((npair, 8, T), F32)],
        scratch_shapes=[pltpu.VMEM((2, LANE, tk), F32), pltpu.VMEM((2, LANE, tk), F32), pltpu.VMEM((2, tk, 1), F32)],
        compiler_params=_cp(2),
    )(qkv, qkvt, qkv, qkvt, qkv, dob, dobt, cb, alrow, dlrow)


HALO = 8


def _shift_rows(cur, other, k, tm, down):
    row = lax.broadcasted_iota(jnp.int32, (tm, 1), 0)
    reps = tm // HALO
    if down:
        rolled = pltpu.roll(cur, k, 0)
        fill = jnp.tile(pltpu.roll(other, k, 0), (reps, 1))
        return jnp.where(row < k, fill, rolled)
    rolled = pltpu.roll(cur, tm - k, 0)
    fill = jnp.tile(pltpu.roll(other, HALO - k, 0), (reps, 1))
    return jnp.where(row >= tm - k, fill, rolled)


def _conv_fwd(c, hh, c_prev, hh_prev, w_ref, first, tm):
    u = c * hh
    u_prev = jnp.where(first, 0.0, c_prev * hh_prev)
    u1 = _shift_rows(u, u_prev, 1, tm, True)
    u2 = _shift_rows(u, u_prev, 2, tm, True)
    y = w_ref[0:1, :] * u2 + w_ref[1:2, :] * u1 + w_ref[2:3, :] * u
    return u, u1, u2, y


def _rms(x, g):
    rs = lax.rsqrt(jnp.mean(x * x, axis=-1, keepdims=True) + RMS_EPS)
    return x * rs * g, rs


def _mixer_tail_fwd(o, bchf, conv_w, g_attn, g_conv, w_mo, x1, lg, lb):
    T = o.shape[0]
    tm = _tile(T, 512)
    hb = tm // HALO

    def body(o_ref, b_ref, c_ref, h_ref, cp_ref, hp_ref, w_ref, ga_ref, gc_ref, wmo_ref, x1_ref, lg_ref, lb_ref,
             x2_ref, r2_ref, mg_ref):
        first = pl.program_id(0) == 0
        _, _, _, y = _conv_fwd(c_ref[...], h_ref[...], cp_ref[...], hp_ref[...], w_ref, first, tm)
        na, _ = _rms(o_ref[...], ga_ref[...])
        nc, _ = _rms(b_ref[...] * y, gc_ref[...])
        nab = na.astype(BF16)
        ncb = nc.astype(BF16)
        mg_ref[:, 0:D_ATTN] = nab
        mg_ref[:, D_ATTN:] = ncb
        r2 = ALPHA * x1_ref[...] + _nn(nab, wmo_ref[0:D_ATTN, :]) + _nn(ncb, wmo_ref[D_ATTN:, :])
        r2_ref[...] = r2
        xhat, _ = _ln_stats(r2)
        x2_ref[...] = xhat * lg_ref[...] + lb_ref[...]

    row = lambda n, col=0: pl.BlockSpec((tm, n), lambda i: (i, col))
    prev = lambda col: pl.BlockSpec((HALO, D_CONV), lambda i: (jnp.maximum(i * hb - 1, 0), col))
    return pl.pallas_call(
        body, name="mixer_tail_fwd", grid=(T // tm,),
        in_specs=[row(D_ATTN), row(D_CONV, 0), row(D_CONV, 1), row(D_CONV, 2), prev(1), prev(2),
                  _resident((3, D_CONV)), _resident((1, D_ATTN)), _resident((1, D_CONV)),
                  _resident((D_MODEL, D_MODEL)), row(D_MODEL), _resident((1, D_MODEL)), _resident((1, D_MODEL))],
        out_specs=[row(D_MODEL), row(D_MODEL), row(D_MODEL)],
        out_shape=[jax.ShapeDtypeStruct((T, D_MODEL), F32), jax.ShapeDtypeStruct((T, D_MODEL), F32),
                   jax.ShapeDtypeStruct((T, D_MODEL), BF16)],
        compiler_params=_cp(1),
    )(o, bchf, bchf, bchf, bchf, bchf, conv_w, g_attn, g_conv, w_mo, x1, lg, lb)


def _head_sum_rows():
    row = lax.broadcasted_iota(jnp.int32, (4 * 8, D_ATTN), 0)
    head = lax.broadcasted_iota(jnp.int32, (4 * 8, D_ATTN), 1) // HEAD_DIM
    return jnp.where((row % 8 < 2) & (2 * (row // 8) + row % 8 == head), 1.0, 0.0).astype(F32)


def _mixer_tail_bwd(dx2, r2, lg, w_mo, o, bchf, conv_w, g_attn, g_conv):
    T = o.shape[0]
    tm = _tile(T, 256)
    hb = tm // HALO

    def body(dx2_ref, r2_ref, lg_ref, wmo_ref, o_ref, b_ref, c_ref, h_ref, cp_ref, hp_ref, w_ref, ga_ref, gc_ref,
             dx1_ref, dr_ref, do_ref, dot_ref, dl_ref, dco_ref, dlg_ref, dlb_ref, dga_ref, dgc_ref):
        i = pl.program_id(0)

        @pl.when(i == 0)
        def _():
            for ref in (dlg_ref, dlb_ref, dga_ref, dgc_ref):
                ref[...] = jnp.zeros_like(ref)

        dy = dx2_ref[...]
        xhat, rstd = _ln_stats(r2_ref[...])
        dr = _ln_bwd(dy, xhat, rstd, lg_ref[...])
        dlg_ref[...] += _rowsum(dy * xhat)
        dlb_ref[...] += _rowsum(dy)
        dx1_ref[...] = ALPHA * dr
        drb = dr.astype(BF16)
        dr_ref[...] = drb
        dna = _nt(drb, wmo_ref[0:D_ATTN, :])
        dnc = _nt(drb, wmo_ref[D_ATTN:, :])

        def rms_bwd(x, g, dn):
            rs = lax.rsqrt(jnp.mean(x * x, axis=-1, keepdims=True) + RMS_EPS)
            dng = dn * g
            dx = rs * dng - x * (rs * rs * rs) * jnp.mean(dng * x, axis=-1, keepdims=True)
            return dx, _rowsum(dn * x * rs)

        oo = o_ref[...]
        do, dga = rms_bwd(oo, ga_ref[...], dna)
        dga_ref[...] += dga
        do_ref[...] = do.astype(BF16)
        dot_ref[...] = do.T.astype(BF16)
        dl_ref[...] = lax.dot_general(_head_sum_rows(), do * oo, (((1,), (1,)), ((), ())),
                                      preferred_element_type=F32, precision=lax.Precision.HIGHEST)
        _, _, _, y = _conv_fwd(c_ref[...], h_ref[...], cp_ref[...], hp_ref[...], w_ref, i == 0, tm)
        dco, dgc = rms_bwd(b_ref[...] * y, gc_ref[...], dnc)
        dgc_ref[...] += dgc
        dco_ref[...] = dco

    row = lambda n, col=0: pl.BlockSpec((tm, n), lambda i: (i, col))
    prev = lambda col: pl.BlockSpec((HALO, D_CONV), lambda i: (jnp.maximum(i * hb - 1, 0), col))
    vec = lambda n: _resident((1, n))
    return pl.pallas_call(
        body, name="mixer_tail_bwd", grid=(T // tm,),
        in_specs=[row(D_MODEL), row(D_MODEL), vec(D_MODEL), _resident((D_MODEL, D_MODEL)), row(D_ATTN),
                  row(D_CONV, 0), row(D_CONV, 1), row(D_CONV, 2), prev(1), prev(2), _resident((3, D_CONV)),
                  vec(D_ATTN), vec(D_CONV)],
        out_specs=[row(D_MODEL), row(D_MODEL), row(D_ATTN), pl.BlockSpec((D_ATTN, tm), lambda i: (0, i)),
                   pl.BlockSpec((4 * 8, tm), lambda i: (0, i)), row(D_CONV),
                   vec(D_MODEL), vec(D_MODEL), vec(D_ATTN), vec(D_CONV)],
        out_shape=[jax.ShapeDtypeStruct((T, D_MODEL), F32), jax.ShapeDtypeStruct((T, D_MODEL), BF16),
                   jax.ShapeDtypeStruct((T, D_ATTN), BF16), jax.ShapeDtypeStruct((D_ATTN, T), BF16),
                   jax.ShapeDtypeStruct((4 * 8, T), F32),
                   jax.ShapeDtypeStruct((T, D_CONV), F32), jax.ShapeDtypeStruct((1, D_MODEL), F32),
                   jax.ShapeDtypeStruct((1, D_MODEL), F32), jax.ShapeDtypeStruct((1, D_ATTN), F32),
                   jax.ShapeDtypeStruct((1, D_CONV), F32)],
        compiler_params=_cp(1),
    )(dx2, r2, lg, w_mo, o, bchf, bchf, bchf, bchf, bchf, conv_w, g_attn, g_conv)


def _conv_bwd(dco, bchf, conv_w):
    T = dco.shape[0]
    tm = _tile(T, 512)
    hb = tm // HALO
    nt = T // tm

    def body(dco_ref, dcon_ref, b_ref, bn_ref, c_ref, h_ref, cp_ref, hp_ref, w_ref, dbch_ref, dw_ref):
        i = pl.program_id(0)

        @pl.when(i == 0)
        def _():
            dw_ref[...] = jnp.zeros_like(dw_ref)

        cc = c_ref[...]
        hh = h_ref[...]
        u, u1, u2, y = _conv_fwd(cc, hh, cp_ref[...], hp_ref[...], w_ref, i == 0, tm)
        dco = dco_ref[...]
        bb = b_ref[...]
        dyc = dco * bb
        dy_next = jnp.where(i == nt - 1, 0.0, dcon_ref[...] * bn_ref[...])
        d1 = _shift_rows(dyc, dy_next, 1, tm, False)
        d2 = _shift_rows(dyc, dy_next, 2, tm, False)
        du = w_ref[2:3, :] * dyc + w_ref[1:2, :] * d1 + w_ref[0:1, :] * d2
        dbch_ref[:, 0:D_CONV] = (dco * y).astype(BF16)
        dbch_ref[:, D_CONV:2 * D_CONV] = (du * hh).astype(BF16)
        dbch_ref[:, 2 * D_CONV:] = (du * cc).astype(BF16)
        dw_ref[0:1, :] += _rowsum(dyc * u2)
        dw_ref[1:2, :] += _rowsum(dyc * u1)
        dw_ref[2:3, :] += _rowsum(dyc * u)

    row = lambda n, col=0: pl.BlockSpec((tm, n), lambda i: (i, col))
    prev = lambda col: pl.BlockSpec((HALO, D_CONV), lambda i: (jnp.maximum(i * hb - 1, 0), col))
    nxt = lambda col: pl.BlockSpec((HALO, D_CONV), lambda i: (jnp.minimum((i + 1) * hb, T // HALO - 1), col))
    return pl.pallas_call(
        body, name="conv_bwd", grid=(nt,),
        in_specs=[row(D_CONV), nxt(0), row(D_CONV, 0), nxt(0), row(D_CONV, 1), row(D_CONV, 2), prev(1), prev(2),
                  _resident((3, D_CONV))],
        out_specs=[row(3 * D_CONV), _resident((8, D_CONV))],
        out_shape=[jax.ShapeDtypeStruct((T, 3 * D_CONV), BF16), jax.ShapeDtypeStruct((8, D_CONV), F32)],
        compiler_params=_cp(1),
    )(dco, dco, bchf, bchf, bchf, bchf, bchf, bchf, conv_w)


def _mixer_in_bwd(dx1a, dqt, dkt, dvt, dbch, dfl, w_qkvt, w_bch, w_f):
    T = dx1a.shape[0]
    tm = _tile(T, 512)

    def body(a_ref, dq_ref, dk_ref, dv_ref, db_ref, df_ref, wq_ref, wb_ref, wf_ref, o_ref):
        acc = a_ref[...] + _nt(db_ref[...], wb_ref[...]) + _nt(df_ref[...], wf_ref[...])
        for n, ref in enumerate((dq_ref, dk_ref, dv_ref)):
            acc = acc + _tn(ref[...].astype(BF16), wq_ref[n * D_ATTN:(n + 1) * D_ATTN, :])
        o_ref[...] = acc

    row = lambda n: pl.BlockSpec((tm, n), lambda i: (i, 0))
    col = pl.BlockSpec((D_ATTN, tm), lambda i: (0, i))
    return pl.pallas_call(
        body, name="mixer_in_bwd", grid=(T // tm,),
        in_specs=[row(D_MODEL), col, col, col, row(3 * D_CONV), row(N_FLOG),
                  _resident((3 * D_ATTN, D_MODEL)), _resident((D_MODEL, 3 * D_CONV)), _resident((D_MODEL, N_FLOG))],
        out_specs=row(D_MODEL),
        out_shape=jax.ShapeDtypeStruct((T, D_MODEL), F32),
        compiler_params=_cp(1),
    )(dx1a, dqt, dkt, dvt, dbch, dfl, w_qkvt, w_bch, w_f)


def _ple_loss(x3, p, tgt, w_g, w_p, b_g, lg, lb):
    T = x3.shape[0]
    tm = _tile(T, 512)

    def body(x_ref, p_ref, t_ref, wg_ref, wp_ref, bg_ref, lg_ref, lb_ref,
             dx_ref, de_ref, dz_ref, loss_ref, dlg_ref, dlb_ref, dbg_ref):
        @pl.when(pl.program_id(0) == 0)
        def _():
            for ref in (loss_ref, dlg_ref, dlb_ref, dbg_ref):
                ref[...] = jnp.zeros_like(ref)

        xf = x_ref[...]
        gate = _sigmoid(_nn(xf.astype(BF16), wg_ref[...]) + bg_ref[...])
        e = _nn(p_ref[...].astype(BF16), wp_ref[...])
        xhat, rstd = _ln_stats(ALPHA * xf + gate * e)
        err = xhat * lg_ref[...] + lb_ref[...] - t_ref[...]
        sq = jnp.sum(_rowsum(err * err), axis=-1, keepdims=True)
        loss_ref[...] += jnp.broadcast_to(sq * (0.5 / D_MODEL), loss_ref.shape)
        dy = err * (1.0 / D_MODEL)
        dr = _ln_bwd(dy, xhat, rstd, lg_ref[...])
        dlg_ref[...] += _rowsum(dy * xhat)
        dlb_ref[...] += _rowsum(dy)
        de_ref[...] = (dr * gate).astype(BF16)
        dz = dr * e * gate * (1.0 - gate)
        dbg_ref[...] += _rowsum(dz)
        dzb = dz.astype(BF16)
        dz_ref[...] = dzb
        dx_ref[...] = ALPHA * dr + _nt(dzb, wg_ref[...])

    row = lambda n: pl.BlockSpec((tm, n), lambda i: (i, 0))
    vec = lambda n: _resident((1, n))
    return pl.pallas_call(
        body, name="ple_loss", grid=(T // tm,),
        in_specs=[row(D_MODEL), row(PLE_DIM), row(D_MODEL), _resident((D_MODEL, D_MODEL)),
                  _resident((PLE_DIM, D_MODEL)), vec(D_MODEL), vec(D_MODEL), vec(D_MODEL)],
        out_specs=[row(D_MODEL), row(D_MODEL), row(D_MODEL), vec(LANE), vec(D_MODEL), vec(D_MODEL), vec(D_MODEL)],
        out_shape=[jax.ShapeDtypeStruct((T, D_MODEL), F32), jax.ShapeDtypeStruct((T, D_MODEL), BF16),
                   jax.ShapeDtypeStruct((T, D_MODEL), BF16), jax.ShapeDtypeStruct((1, LANE), F32),
                   jax.ShapeDtypeStruct((1, D_MODEL), F32), jax.ShapeDtypeStruct((1, D_MODEL), F32),
                   jax.ShapeDtypeStruct((1, D_MODEL), F32)],
        compiler_params=_cp(1),
    )(x3, p, tgt, w_g, w_p, b_g, lg, lb)


def _lane_layout(v8):
    return jnp.repeat(v8, HEAD_DIM, axis=1)


def _row_layout(v8):
    t = v8.shape[0]
    return jnp.pad(v8.T.reshape(N_HEADS // 2, 2, t), ((0, 0), (0, 6), (0, 0)))


def _from_lane_layout(vl):
    return vl[:, ::HEAD_DIM]


def _from_row_layout(vr):
    return vr[:, :2, :].reshape(N_HEADS, -1).T


def _local_step(x, p, tgt, w):
    bf = lambda a: a.astype(BF16)
    w1i, w1o, w2i, w2o = bf(w["ffn1_w_in"]), bf(w["ffn1_w_out"]), bf(w["ffn2_w_in"]), bf(w["ffn2_w_out"])
    wmi = w["w_mix_in"]
    o_f = 3 * D_ATTN
    o_b = o_f + N_HEADS
    w_qkv = bf(wmi[:, :o_f])
    w_f = bf(jnp.pad(wmi[:, o_f:o_b], ((0, 0), (0, N_FLOG - N_HEADS))))
    w_bch = bf(wmi[:, o_b:])
    w_bchf = jnp.concatenate([w_bch, w_f], axis=1)
    w_mo, w_g, w_p = bf(w["w_mix_out"]), bf(w["w_ple_gate"]), bf(w["w_ple"])
    b_f = jnp.pad(w["b_forget"], ((0, 0), (0, N_FLOG - N_HEADS)))

    x1, r1, g1, u1, h1 = _ffn_fwd(x, w1i, w1o, w["ln1_g"], w["ln1_b"], "ffn1_fwd")
    q_scale = jnp.concatenate([jnp.full((1, D_ATTN), Q_SCALE * LOG2E, F32), jnp.ones((1, 2 * D_ATTN), F32)], axis=1)
    qkv = _matmul_nn(x1, w_qkv, q_scale, BF16, "proj_qkv")
    bchf = _matmul_nn(x1, w_bchf, jnp.ones((1, 3 * D_CONV + N_FLOG), F32), F32, "proj_bchf")
    fcol = 3 * D_CONV // N_FLOG
    c = _forget_cumsum(bchf, fcol, b_f)
    c8 = c[:, :N_HEADS]
    cb = _lane_layout(c8)
    qkvt = qkv.T
    o, alrow = _attn_fwd(qkv, qkvt, _fold_key_bias(qkv, c))
    x2, r2, merged = _mixer_tail_fwd(o, bchf, w["conv_w"], w["g_attn"], w["g_conv"], w_mo, x1, w["ln2_g"], w["ln2_b"])
    x3, r3, g2, u2, h2 = _ffn_fwd(x2, w2i, w2o, w["ln3_g"], w["ln3_b"], "ffn2_fwd")

    grads = {}
    dx3, de, dz, loss, grads["ln4_g"], grads["ln4_b"], grads["b_ple_gate"] = _ple_loss(
        x3, p, tgt, w_g, w_p, w["b_ple_gate"], w["ln4_g"], w["ln4_b"])
    grads["w_ple"] = _matmul_tn(p, de, "dw_ple")
    grads["w_ple_gate"] = _matmul_tn(x3, dz, "dw_ple_gate")

    dx2, dgu2, df2, grads["ln3_g"], grads["ln3_b"] = _ffn_bwd(dx3, r3, g2, u2, w2i, w2o, w["ln3_g"], "ffn2_bwd")
    grads["ffn2_w_in"] = _matmul_tn(x2, dgu2, "dw_ffn2_in")
    grads["ffn2_w_out"] = _matmul_tn(h2, df2, "dw_ffn2_out")

    (dx1a, dr2, dob, dobt, delta, dco, grads["ln2_g"], grads["ln2_b"], grads["g_attn"],
     grads["g_conv"]) = _mixer_tail_bwd(dx2, r2, w["ln2_g"], w_mo, o, bchf, w["conv_w"], w["g_attn"], w["g_conv"])
    grads["w_mix_out"] = _matmul_tn(merged, dr2, "dw_mix_out")
    dbch, dcw = _conv_bwd(dco, bchf, w["conv_w"])
    grads["conv_w"] = dcw[:3]
    dqt, dkt, dvt, dck, dcq = _attn_bwd(qkv, qkvt, dob, dobt, cb, alrow, delta.reshape(N_HEADS // 2, 8, -1))
    dcq_lanes = jnp.pad(_from_row_layout(dcq), ((0, 0), (0, N_FLOG - N_HEADS)))
    dfl, dbf = _forget_bwd(dck, dcq_lanes, bchf, fcol, b_f)
    grads["b_forget"] = dbf[:, :N_HEADS]
    dx1 = _mixer_in_bwd(dx1a, dqt, dkt, dvt, dbch, dfl, w_qkv.T, w_bch, w_f)
    grads["w_mix_in"] = jnp.concatenate(
        [_matmul_tokens(dqt, x1, "dw_q").T, _matmul_tokens(dkt, x1, "dw_k").T, _matmul_tokens(dvt, x1, "dw_v").T,
         _matmul_tn(x1, dfl, "dw_flog")[:, :N_HEADS], _matmul_tn(x1, dbch, "dw_bch")], axis=1)

    dx0, dgu1, df1, grads["ln1_g"], grads["ln1_b"] = _ffn_bwd(dx1, r1, g1, u1, w1i, w1o, w["ln1_g"], "ffn1_bwd")
    grads["ffn1_w_in"] = _matmul_tn(x, dgu1, "dw_ffn1_in")
    grads["ffn1_w_out"] = _matmul_tn(h1, df1, "dw_ffn1_out")
    return loss, dx0, grads


WEIGHTS = ["ffn1_w_in", "ffn1_w_out", "ln1_g", "ln1_b", "w_mix_in", "b_forget", "conv_w", "g_attn", "g_conv",
           "w_mix_out", "ln2_g", "ln2_b", "ffn2_w_in", "ffn2_w_out", "ln3_g", "ln3_b", "w_ple", "w_ple_gate",
           "b_ple_gate", "ln4_g", "ln4_b"]
LAYOUT = {
    "ffn1_w_in": ((D_MODEL, 2 * D_FF), 1), "ffn1_w_out": ((D_FF, D_MODEL), 0),
    "w_mix_in": ((D_MODEL, 3 * D_ATTN + N_HEADS + 3 * D_CONV), 1), "conv_w": ((3, D_CONV), 1),
    "w_mix_out": ((D_MODEL, D_MODEL), 0), "ffn2_w_in": ((D_MODEL, 2 * D_FF), 1), "ffn2_w_out": ((D_FF, D_MODEL), 0),
    "w_ple": ((PLE_DIM, D_MODEL), 1), "w_ple_gate": ((D_MODEL, D_MODEL), 0),
    "ln1_g": ((1, D_MODEL), None), "ln1_b": ((1, D_MODEL), None), "b_forget": ((1, N_HEADS), None),
    "g_attn": ((1, D_ATTN), None), "g_conv": ((1, D_CONV), None), "ln2_g": ((1, D_MODEL), None),
    "ln2_b": ((1, D_MODEL), None), "ln3_g": ((1, D_MODEL), None), "ln3_b": ((1, D_MODEL), None),
    "b_ple_gate": ((1, D_MODEL), None), "ln4_g": ((1, D_MODEL), None), "ln4_b": ((1, D_MODEL), None),
}
BIG = [n for n in WEIGHTS if LAYOUT[n][1] is not None and n != "conv_w"]
SMALL = [n for n in WEIGHTS if n not in BIG]
ROW = 1024
SMALL_ROWS = 16


def _shard_shape(name):
    shape, axis = LAYOUT[name]
    if axis is None:
        return shape
    return tuple(s // N_CHIPS if a == axis else s for a, s in enumerate(shape))


def _halves(a):
    return a.reshape(a.shape[:-2] + (2, a.shape[-2] // 2, a.shape[-1]))


def _split_chips(name, full):
    shape, axis = LAYOUT[name]
    if axis == 0:
        return full.reshape((N_CHIPS, shape[0] // N_CHIPS) + shape[1:])
    return jnp.moveaxis(full.reshape(shape[:1] + (N_CHIPS, shape[1] // N_CHIPS)), 1, 0)


def _join_chips(name, parts):
    shape, axis = LAYOUT[name]
    if axis == 0:
        return parts.reshape(shape)
    return jnp.moveaxis(parts, 0, 1).reshape(shape)


SMALL_AT = {"ln1_g": (0, 0), "ln1_b": (1, 0), "ln2_g": (2, 0), "ln2_b": (3, 0), "ln3_g": (4, 0), "ln3_b": (5, 0),
            "b_ple_gate": (6, 0), "ln4_g": (7, 0), "ln4_b": (8, 0), "g_attn": (9, 0), "g_conv": (9, D_ATTN),
            "b_forget": (10, 0), "conv_w": (10, LANE)}
CONV_SHARD = D_CONV // N_CHIPS


def _pack_small_grads(grads):
    def body(*refs):
        ins, o_ref = dict(zip(SMALL, refs[:-1])), refs[-1]
        o_ref[...] = jnp.zeros_like(o_ref)
        for s in range(N_CHIPS):
            for n in SMALL:
                r, c0 = SMALL_AT[n]
                if n == "conv_w":
                    for k in range(3):
                        o_ref[s, r:r + 1, c0 + k * CONV_SHARD:c0 + (k + 1) * CONV_SHARD] = (
                            ins[n][k:k + 1, s * CONV_SHARD:(s + 1) * CONV_SHARD])
                else:
                    o_ref[s, r:r + 1, c0:c0 + ins[n].shape[1]] = ins[n][...]

    return pl.pallas_call(
        body, name="pack_small_grads",
        out_shape=jax.ShapeDtypeStruct((N_CHIPS, SMALL_ROWS, ROW), F32),
    )(*[grads[n] for n in SMALL])


def _adamw_math(w, g, m, v):
    c1 = 1.0 - ADAM_B1 ** ADAM_STEP
    c2 = 1.0 - ADAM_B2 ** ADAM_STEP
    m = ADAM_B1 * m + (1.0 - ADAM_B1) * g
    v = ADAM_B2 * v + (1.0 - ADAM_B2) * (g * g)
    return -ADAM_LR * ((m / c1) / (jnp.sqrt(v / c2) + ADAM_EPS) + ADAM_WD * w), m, v


def _adamw_small(g_mine, g_sib, c_idx, w, m, v):
    ns = len(SMALL)

    def body(c_ref, gm_ref, gs_ref, *refs):
        ws, ms, vs = refs[:ns], refs[ns:2 * ns], refs[2 * ns:3 * ns]
        outs = refs[3 * ns:]
        mine_first = c_ref[0] == 0
        top = jnp.where(mine_first, gm_ref[...], gs_ref[...])
        bot = jnp.where(mine_first, gs_ref[...], gm_ref[...])
        for i, n in enumerate(SMALL):
            r, c0 = SMALL_AT[n]
            blk, rr = (top, r) if r < SMALL_ROWS // 2 else (bot, r - SMALL_ROWS // 2)
            rows, width = ws[i].shape
            for k in range(rows):
                g = blk[rr:rr + 1, c0 + k * width:c0 + (k + 1) * width]
                d, mn, vn = _adamw_math(ws[i][k:k + 1, :], g, ms[i][k:k + 1, :], vs[i][k:k + 1, :])
                for q, val in enumerate((g, d, mn, vn)):
                    outs[q * ns + i][k:k + 1, :] = val

    shapes = [jax.ShapeDtypeStruct(a.shape, F32) for a in w]
    vmem = pl.BlockSpec(memory_space=pltpu.VMEM)
    res = pl.pallas_call(
        body, name="adamw_small",
        in_specs=[pl.BlockSpec(memory_space=pltpu.SMEM)] + [vmem] * (2 + 3 * ns),
        out_specs=[vmem] * (4 * ns),
        out_shape=shapes * 4,
    )(c_idx, g_mine, g_sib, *w, *m, *v)
    return [res[q * ns:(q + 1) * ns] for q in range(4)]


def _place():
    x, y, c = lax.axis_index("x"), lax.axis_index("y"), lax.axis_index("c")
    others = [(1 - x, y), (x, 1 - y), (1 - x, 1 - y)]
    return x, y, c, others


ANY_SPEC = pl.BlockSpec(memory_space=pl.ANY)


def _remote(src, dst, send_sems, recv_sems, k, to):
    return pltpu.make_async_remote_copy(src_ref=src, dst_ref=dst, send_sem=send_sems.at[k], recv_sem=recv_sems.at[k],
                                        device_id=to, device_id_type=MESH)


def _all_gather(shards):
    n = len(shards)

    def body(*refs):
        ins, outs, send_sems, recv_sems = refs[:n], refs[n:2 * n], refs[2 * n], refs[2 * n + 1]
        x, y, c, others = _place()
        s = 2 * x + y
        slot = lambda t, chip, half: outs[t].at[2 * chip[0] + chip[1], half]
        first = [_remote(ins[t].at[c], outs[t].at[s, c], send_sems, recv_sems, 6 * t + j, (*chip, c))
                 for t in range(n) for j, chip in enumerate(others)]
        for cp in first:
            cp.start()
        passed = []
        for t in range(n):
            for j, chip in enumerate(others):
                landed = slot(t, chip, c)
                _remote(landed, landed, send_sems, recv_sems, 6 * t + j, (x, y, c)).wait_recv()
                passed.append(_remote(landed, landed, send_sems, recv_sems, 6 * t + 3 + j, (x, y, 1 - c)))
                passed[-1].start()
        for t in range(n):
            for j, chip in enumerate(others):
                landed = slot(t, chip, 1 - c)
                _remote(landed, landed, send_sems, recv_sems, 6 * t + 3 + j, (x, y, c)).wait_recv()
        for cp in first + passed:
            cp.wait_send()

    return pl.pallas_call(
        body, name="all_gather_weights",
        out_shape=[jax.ShapeDtypeStruct((N_CHIPS,) + a.shape, a.dtype) for a in shards],
        in_specs=[ANY_SPEC] * n, out_specs=[ANY_SPEC] * n,
        scratch_shapes=[pltpu.SemaphoreType.DMA((6 * n,)), pltpu.SemaphoreType.DMA((6 * n,))],
    )(*shards)


def _swap_halves(gs):
    n = len(gs)

    def body(*refs):
        ins, outs, send_sems, recv_sems = refs[:n], refs[n:2 * n], refs[2 * n], refs[2 * n + 1]
        x, y, c, _ = _place()
        copies = [_remote(ins[t].at[:, 1 - c], outs[t], send_sems, recv_sems, t, (x, y, 1 - c)) for t in range(n)]
        for cp in copies:
            cp.start()
        for cp in copies:
            cp.wait()

    return pl.pallas_call(
        body, name="grad_swap_halves",
        out_shape=[jax.ShapeDtypeStruct(g.shape[:1] + g.shape[2:], g.dtype) for g in gs],
        in_specs=[ANY_SPEC] * n, out_specs=[ANY_SPEC] * n,
        scratch_shapes=[pltpu.SemaphoreType.DMA((n,)), pltpu.SemaphoreType.DMA((n,))],
    )(*gs)


def _exchange_chips(pps):
    n = len(pps)

    def body(*refs):
        ins, outs, send_sems, recv_sems = refs[:n], refs[n:2 * n], refs[2 * n], refs[2 * n + 1]
        x, y, c, others = _place()
        s = 2 * x + y
        sends = [_remote(ins[t].at[2 * chip[0] + chip[1]], outs[t].at[s], send_sems, recv_sems, 3 * t + j, (*chip, c))
                 for t in range(n) for j, chip in enumerate(others)]
        for cp in sends:
            cp.start()
        for t in range(n):
            for j, chip in enumerate(others):
                landed = outs[t].at[2 * chip[0] + chip[1]]
                _remote(landed, landed, send_sems, recv_sems, 3 * t + j, (x, y, c)).wait_recv()
        for cp in sends:
            cp.wait_send()

    return pl.pallas_call(
        body, name="grad_exchange_chips",
        out_shape=[jax.ShapeDtypeStruct(p.shape, p.dtype) for p in pps],
        in_specs=[ANY_SPEC] * n, out_specs=[ANY_SPEC] * n,
        scratch_shapes=[pltpu.SemaphoreType.DMA((3 * n,)), pltpu.SemaphoreType.DMA((3 * n,))],
    )(*pps)


def _share_half(rs):
    n = len(rs)

    def body(*refs):
        ins, outs, send_sems, recv_sems = refs[:n], refs[n:2 * n], refs[2 * n], refs[2 * n + 1]
        x, y, c, _ = _place()
        copies = [_remote(ins[t], outs[t], send_sems, recv_sems, t, (x, y, 1 - c)) for t in range(n)]
        for cp in copies:
            cp.start()
        for cp in copies:
            cp.wait()

    return pl.pallas_call(
        body, name="grad_share_half",
        out_shape=[jax.ShapeDtypeStruct(r.shape, r.dtype) for r in rs],
        in_specs=[ANY_SPEC] * n, out_specs=[ANY_SPEC] * n,
        scratch_shapes=[pltpu.SemaphoreType.DMA((n,)), pltpu.SemaphoreType.DMA((n,))],
    )(*rs)


ELEMENTWISE_BLOCK_BYTES = 1 << 20


def _row_tile(rows, cols):
    return _tile(rows, max(8, ELEMENTWISE_BLOCK_BYTES // (4 * cols) // 8 * 8))


def _add_my_half(g, sib, c_idx, name):
    rh, cols = g.shape[2:]
    tr = _row_tile(rh, cols)

    def body(c_ref, g_ref, s_ref, o_ref):
        o_ref[...] = (g_ref[...] + s_ref[...]).astype(BF16)

    return pl.pallas_call(
        body, name="grad_add_halves_" + name,
        grid_spec=pltpu.PrefetchScalarGridSpec(
            num_scalar_prefetch=1, grid=(N_CHIPS, rh // tr),
            in_specs=[pl.BlockSpec((None, None, tr, cols), lambda s, i, c: (s, c[0], i, 0)),
                      pl.BlockSpec((None, tr, cols), lambda s, i, c: (s, i, 0))],
            out_specs=pl.BlockSpec((None, tr, cols), lambda s, i, c: (s, i, 0))),
        out_shape=jax.ShapeDtypeStruct((N_CHIPS, rh, cols), BF16),
        compiler_params=_cp(2),
    )(c_idx, g, sib)


def _sum_chips(parts, pp, s_idx, name):
    rh, cols = parts.shape[1:]
    tr = _row_tile(rh, cols)

    def body(s_ref, p0, p1, p2, p3, mine_ref, o_ref):
        own = mine_ref[...]
        t = [jnp.where(s_ref[0] == k, own, p[...]).astype(F32) for k, p in enumerate((p0, p1, p2, p3))]
        o_ref[...] = ((t[0] + t[1]) + t[2]) + t[3]

    slot = lambda k: pl.BlockSpec((None, tr, cols), lambda i, s: (jnp.where(s[0] == k, (k + 1) % N_CHIPS, k), i, 0))
    return pl.pallas_call(
        body, name="grad_sum_chips_" + name,
        grid_spec=pltpu.PrefetchScalarGridSpec(
            num_scalar_prefetch=1, grid=(rh // tr,),
            in_specs=[slot(0), slot(1), slot(2), slot(3), pl.BlockSpec((None, tr, cols), lambda i, s: (s[0], i, 0))],
            out_specs=pl.BlockSpec((tr, cols), lambda i, s: (i, 0))),
        out_shape=jax.ShapeDtypeStruct((rh, cols), F32),
        compiler_params=_cp(1),
    )(s_idx, parts, parts, parts, parts, pp)


def _adamw(w, g_mine, g_sib, m, v, c_idx, name):
    rows, cols = w.shape
    tr = _row_tile(rows // 2, cols)
    nbh = rows // 2 // tr

    def body(c_ref, w_ref, gm_ref, gs_ref, m_ref, v_ref, g_ref, d_ref, mo_ref, vo_ref):
        g = jnp.where(pl.program_id(0) // nbh == c_ref[0], gm_ref[...], gs_ref[...])
        g_ref[...] = g
        d_ref[...], mo_ref[...], vo_ref[...] = _adamw_math(w_ref[...], g, m_ref[...], v_ref[...])

    spec = pl.BlockSpec((tr, cols), lambda i, c: (i, 0))
    half = pl.BlockSpec((tr, cols), lambda i, c: (i % nbh, 0))
    return pl.pallas_call(
        body, name="adamw_" + name,
        grid_spec=pltpu.PrefetchScalarGridSpec(
            num_scalar_prefetch=1, grid=(rows // tr,),
            in_specs=[spec, half, half, spec, spec], out_specs=[spec] * 4),
        out_shape=[jax.ShapeDtypeStruct(w.shape, F32)] * 4,
        compiler_params=_cp(1),
    )(c_idx, w, g_mine, g_sib, m, v)


def kernel(x, p, ffn1_w_in, ffn1_w_out, ln1_g, ln1_b, w_mix_in, b_forget, conv_w, g_attn, g_conv, w_mix_out, ln2_g, ln2_b, ffn2_w_in, ffn2_w_out, ln3_g, ln3_b, w_ple, w_ple_gate, b_ple_gate, ln4_g, ln4_b, loss_target, m_ffn1_w_in, m_ffn1_w_out, m_ln1_g, m_ln1_b, m_w_mix_in, m_b_forget, m_conv_w, m_g_attn, m_g_conv, m_w_mix_out, m_ln2_g, m_ln2_b, m_ffn2_w_in, m_ffn2_w_out, m_ln3_g, m_ln3_b, m_w_ple, m_w_ple_gate, m_b_ple_gate, m_ln4_g, m_ln4_b, v_ffn1_w_in, v_ffn1_w_out, v_ln1_g, v_ln1_b, v_w_mix_in, v_b_forget, v_conv_w, v_g_attn, v_g_conv, v_w_mix_out, v_ln2_g, v_ln2_b, v_ffn2_w_in, v_ffn2_w_out, v_ln3_g, v_ln3_b, v_w_ple, v_w_ple_gate, v_b_ple_gate, v_ln4_g, v_ln4_b):
    args = dict(locals())
    shard = {n: args[n][0] if LAYOUT[n][1] is not None else args[n] for n in WEIGHTS}
    m_shard = {n: args["m_" + n][0] if LAYOUT[n][1] is not None else args["m_" + n] for n in WEIGHTS}
    v_shard = {n: args["v_" + n][0] if LAYOUT[n][1] is not None else args["v_" + n] for n in WEIGHTS}
    c_idx = lax.axis_index("c").astype(jnp.int32).reshape(1)
    chip = (2 * lax.axis_index("x") + lax.axis_index("y")).astype(jnp.int32)

    conv_rows = SMALL_ROWS - shard["conv_w"].shape[0]
    mine = [_halves(shard[n].astype(BF16)) for n in BIG] + [_halves(jnp.pad(shard["conv_w"], ((0, conv_rows), (0, 0))))]
    gathered = [lax.dynamic_update_slice(theirs, own[None], (chip, 0, 0, 0))
                for theirs, own in zip(_all_gather(mine), mine)]
    full = {n: _join_chips(n, g.reshape((N_CHIPS,) + _shard_shape(n))) for n, g in zip(BIG, gathered)}
    full["conv_w"] = _join_chips("conv_w", gathered[-1].reshape(N_CHIPS, SMALL_ROWS, CONV_SHARD)[:, :3])
    full.update({n: shard[n] for n in SMALL if n != "conv_w"})

    loss_acc, grad_x, grads = _local_step(x[0], p[0, 0], loss_target[0], full)
    loss = lax.psum(loss_acc[0, 0], ("x", "y", "c"))

    names = BIG + ["small"]
    per_chip = [_halves(_split_chips(n, grads[n])) for n in BIG] + [_halves(_pack_small_grads(grads))]
    chip_sums = [_add_my_half(g, sib, c_idx, n) for n, g, sib in zip(names, per_chip, _swap_halves(per_chip))]
    my_half = [_sum_chips(parts, own, chip.reshape(1), n)
               for n, parts, own in zip(names, _exchange_chips(chip_sums), chip_sums)]
    sib_half = _share_half(my_half)

    out = {}
    for n, gm, gs in zip(BIG, my_half, sib_half):
        out[n] = [a[None] for a in _adamw(shard[n], gm, gs, m_shard[n], v_shard[n], c_idx, n)]
    small = _adamw_small(my_half[-1], sib_half[-1], c_idx, [shard[n] for n in SMALL], [m_shard[n] for n in SMALL],
                         [v_shard[n] for n in SMALL])
    for i, n in enumerate(SMALL):
        out[n] = [small[q][i][None] if n == "conv_w" else small[q][i] for q in range(4)]
    return (loss, grad_x[None], *[out[n][q] for q in range(4) for n in WEIGHTS])
```

```python
import functools
import math

import jax
import jax.numpy as jnp
from jax import lax
from jax.experimental import pallas as pl
from jax.experimental.pallas import tpu as pltpu

F32 = jnp.float32
BF16 = jnp.bfloat16

D_MODEL = 1024
D_FF = 2816
N_HEADS = 8
HEAD_DIM = 64
D_ATTN = N_HEADS * HEAD_DIM
D_CONV = 512
PLE_DIM = 256
N_FLOG = 128
ALPHA = 2.0 ** 0.25
LN_EPS = 1e-5
RMS_EPS = 1e-6
NEG_INF = -1e30
Q_SCALE = 1.0 / math.sqrt(HEAD_DIM)
LOG2E = math.log2(math.e)

ADAM_LR = 0.001
ADAM_B1 = 0.9
ADAM_B2 = 0.999
ADAM_EPS = 1e-08
ADAM_WD = 0.01
ADAM_STEP = 10

V7X_VMEM_BYTES = 64 << 20
VMEM_LIMIT = V7X_VMEM_BYTES - (8 << 20)
LANE = 128
FF_CHUNK = 256
N_CHIPS = 4
MESH = pl.DeviceIdType.MESH


def _cp(n_axes):
    return pltpu.CompilerParams(dimension_semantics=("arbitrary",) * n_axes, vmem_limit_bytes=VMEM_LIMIT)


def _resident(shape):
    n = len(shape)
    return pl.BlockSpec(shape, lambda *_: (0,) * n, pipeline_mode=pl.Buffered(1))


def _nn(a, b):
    return jnp.dot(a, b, preferred_element_type=F32)


def _nt(a, b):
    return lax.dot_general(a, b, (((1,), (1,)), ((), ())), preferred_element_type=F32)


def _tn(a, b):
    return lax.dot_general(a, b, (((0,), (0,)), ((), ())), preferred_element_type=F32)


def _ln_stats(r):
    mu = jnp.mean(r, axis=-1, keepdims=True)
    xc = r - mu
    var = jnp.mean(xc * xc, axis=-1, keepdims=True)
    rstd = lax.rsqrt(var + LN_EPS)
    return xc * rstd, rstd


def _ln_bwd(dy, xhat, rstd, g):
    dxh = dy * g
    m1 = jnp.mean(dxh, axis=-1, keepdims=True)
    m2 = jnp.mean(dxh * xhat, axis=-1, keepdims=True)
    return rstd * (dxh - m1 - xhat * m2)


def _sigmoid(z):
    return 1.0 / (1.0 + jnp.exp(-z))


def _rowsum(a):
    return jnp.sum(a, axis=0, keepdims=True)


def _tile(total, want):
    if total <= want:
        return total
    for t in range(want - want % 8, 0, -8):
        if total % t == 0:
            return t
    raise ValueError((total, want))


def _ffn_fwd(x, w_in, w_out, lg, lb, name, gather=()):
    T = x.shape[0]
    tm = _tile(T, 512)
    nf = D_FF // FF_CHUNK
    ng = len(gather)
    last = T // tm - 1

    def body(x_ref, wi_ref, wo_ref, lg_ref, lb_ref, *rest):
        comm_in, (xo_ref, r_ref, g_ref, u_ref, h_ref) = rest[:ng], rest[ng:ng + 5]
        comm_out, sems = rest[ng + 5:2 * ng + 5], rest[2 * ng + 5:]
        if ng:
            @pl.when(pl.program_id(0) == 0)
            def _():
                _gather_start(comm_in, comm_out, *sems)

        xf = x_ref[...]
        xb = xf.astype(BF16)
        acc = jnp.zeros((tm, D_MODEL), F32)
        for j in range(nf):
            c0 = j * FF_CHUNK
            g = _nn(xb, wi_ref[:, c0:c0 + FF_CHUNK])
            u = _nn(xb, wi_ref[:, D_FF + c0:D_FF + c0 + FF_CHUNK])
            hb = (g * _sigmoid(g) * u).astype(BF16)
            g_ref[:, c0:c0 + FF_CHUNK] = g.astype(BF16)
            u_ref[:, c0:c0 + FF_CHUNK] = u.astype(BF16)
            h_ref[:, c0:c0 + FF_CHUNK] = hb
            acc = acc + _nn(hb, wo_ref[c0:c0 + FF_CHUNK, :])
        r = ALPHA * xf + 0.5 * acc
        r_ref[...] = r
        xhat, _ = _ln_stats(r)
        xo_ref[...] = xhat * lg_ref[...] + lb_ref[...]
        if ng:
            @pl.when(pl.program_id(0) == last)
            def _():
                _gather_finish(comm_in, comm_out, *sems)

    row = lambda n: pl.BlockSpec((tm, n), lambda i: (i, 0))
    return pl.pallas_call(
        body, name=name, grid=(T // tm,),
        in_specs=[row(D_MODEL), _resident((D_MODEL, 2 * D_FF)), _resident((D_FF, D_MODEL)),
                  _resident((1, D_MODEL)), _resident((1, D_MODEL))] + [ANY_SPEC] * ng,
        out_specs=[row(D_MODEL), row(D_MODEL), row(D_FF), row(D_FF), row(D_FF)] + [ANY_SPEC] * ng,
        out_shape=[jax.ShapeDtypeStruct((T, D_MODEL), F32), jax.ShapeDtypeStruct((T, D_MODEL), F32),
                   jax.ShapeDtypeStruct((T, D_FF), BF16), jax.ShapeDtypeStruct((T, D_FF), BF16),
                   jax.ShapeDtypeStruct((T, D_FF), BF16)] + _gather_shapes(gather),
        scratch_shapes=_gather_sems(ng) if ng else [],
        compiler_params=_cp(1),
    )(x, w_in, w_out, lg, lb, *gather)


def _ffn_bwd(dxo, r, g, u, w_in, w_out, lg, name, exchange=()):
    T = r.shape[0]
    tm = _tile(T, 256)
    nf = D_FF // FF_CHUNK
    ne = len(exchange)
    last = T // tm - 1

    def body(dxo_ref, r_ref, g_ref, u_ref, wi_ref, wo_ref, lg_ref, *rest):
        comm_in, (dx_ref, dgu_ref, df_ref, dlg_ref, dlb_ref) = rest[:ne], rest[ne:ne + 5]
        comm_out, sems = rest[ne + 5:2 * ne + 5], rest[2 * ne + 5:]
        i = pl.program_id(0)
        if ne:
            @pl.when(i == 0)
            def _():
                _exchange_start(comm_in, comm_out, *sems)

        dy = dxo_ref[...]
        xhat, rstd = _ln_stats(r_ref[...])
        dr = _ln_bwd(dy, xhat, rstd, lg_ref[...])

        @pl.when(i == 0)
        def _():
            dlg_ref[...] = jnp.zeros_like(dlg_ref)
            dlb_ref[...] = jnp.zeros_like(dlb_ref)

        dlg_ref[...] += _rowsum(dy * xhat)
        dlb_ref[...] += _rowsum(dy)
        dfb = (0.5 * dr).astype(BF16)
        df_ref[...] = dfb
        acc = jnp.zeros((tm, D_MODEL), F32)
        dh_ahead = _nt(dfb, wo_ref[0:FF_CHUNK, :])
        for j in range(nf):
            c0 = j * FF_CHUNK
            dh = dh_ahead
            if j + 1 < nf:
                dh_ahead = _nt(dfb, wo_ref[c0 + FF_CHUNK:c0 + 2 * FF_CHUNK, :])
            gg = g_ref[:, c0:c0 + FF_CHUNK].astype(F32)
            uu = u_ref[:, c0:c0 + FF_CHUNK].astype(F32)
            s = _sigmoid(gg)
            dgb = (dh * uu * s * (1.0 + gg * (1.0 - s))).astype(BF16)
            dub = (dh * gg * s).astype(BF16)
            dgu_ref[:, c0:c0 + FF_CHUNK] = dgb
            dgu_ref[:, D_FF + c0:D_FF + c0 + FF_CHUNK] = dub
            acc = acc + _nt(dgb, wi_ref[:, c0:c0 + FF_CHUNK]) + _nt(dub, wi_ref[:, D_FF + c0:D_FF + c0 + FF_CHUNK])
        dx_ref[...] = ALPHA * dr + acc
        if ne:
            @pl.when(i == last)
            def _():
                _exchange_finish(comm_in, comm_out, *sems)

    row = lambda n: pl.BlockSpec((tm, n), lambda i: (i, 0))
    return pl.pallas_call(
        body, name=name, grid=(T // tm,),
        in_specs=[row(D_MODEL), row(D_MODEL), row(D_FF), row(D_FF), _resident((D_MODEL, 2 * D_FF)),
                  _resident((D_FF, D_MODEL)), _resident((1, D_MODEL))] + [ANY_SPEC] * ne,
        out_specs=[row(D_MODEL), row(2 * D_FF), row(D_MODEL), _resident((1, D_MODEL)), _resident((1, D_MODEL))]
        + [ANY_SPEC] * ne,
        out_shape=[jax.ShapeDtypeStruct((T, D_MODEL), F32), jax.ShapeDtypeStruct((T, 2 * D_FF), BF16),
                   jax.ShapeDtypeStruct((T, D_MODEL), BF16), jax.ShapeDtypeStruct((1, D_MODEL), F32),
                   jax.ShapeDtypeStruct((1, D_MODEL), F32)] + _exchange_shapes(exchange),
        scratch_shapes=_exchange_sems(ne) if ne else [],
        compiler_params=_cp(1),
    )(dxo, r, g, u, w_in, w_out, lg, *exchange)


def _matmul_tn(a, b, name, tn=None):
    T, K = a.shape
    N = b.shape[1]
    tt = _tile(T, 1024)
    if tn is None:
        tn = N
        while K * tn * 4 > (6 << 20) and tn % 256 == 0:
            tn //= 2
    assert N % tn == 0

    def body(a_ref, b_ref, o_ref):
        @pl.when(pl.program_id(1) == 0)
        def _():
            o_ref[...] = jnp.zeros_like(o_ref)

        o_ref[...] += _tn(a_ref[...].astype(BF16), b_ref[...].astype(BF16))

    return pl.pallas_call(
        body, name=name, grid=(N // tn, T // tt),
        in_specs=[pl.BlockSpec((tt, K), lambda n, t: (t, 0)), pl.BlockSpec((tt, tn), lambda n, t: (t, n))],
        out_specs=pl.BlockSpec((K, tn), lambda n, t: (0, n)),
        out_shape=jax.ShapeDtypeStruct((K, N), F32),
        compiler_params=_cp(2),
    )(a, b)


def _matmul_tokens(at, b, name):
    M, T = at.shape
    N = b.shape[1]
    tt = _tile(T, 1024)

    def body(a_ref, b_ref, o_ref):
        @pl.when(pl.program_id(0) == 0)
        def _():
            o_ref[...] = jnp.zeros_like(o_ref)

        o_ref[...] += _nn(a_ref[...].astype(BF16), b_ref[...].astype(BF16))

    return pl.pallas_call(
        body, name=name, grid=(T // tt,),
        in_specs=[pl.BlockSpec((M, tt), lambda t: (0, t)), pl.BlockSpec((tt, N), lambda t: (t, 0))],
        out_specs=pl.BlockSpec((M, N), lambda t: (0, 0)),
        out_shape=jax.ShapeDtypeStruct((M, N), F32),
        compiler_params=_cp(1),
    )(at, b)


def _matmul_nn(x, w, scale, out_dtype, name):
    T, K = x.shape
    N = w.shape[1]
    tm = _tile(T, 512)

    def body(x_ref, w_ref, s_ref, o_ref):
        o_ref[...] = (_nn(x_ref[...].astype(BF16), w_ref[...]) * s_ref[...]).astype(out_dtype)

    return pl.pallas_call(
        body, name=name, grid=(T // tm,),
        in_specs=[pl.BlockSpec((tm, K), lambda i: (i, 0)), _resident((K, N)), _resident((1, N))],
        out_specs=pl.BlockSpec((tm, N), lambda i: (i, 0)),
        out_shape=jax.ShapeDtypeStruct((T, N), out_dtype),
        compiler_params=_cp(1),
    )(x, w, scale)


def _log_sigmoid(z):
    return jnp.minimum(z, 0.0) - jnp.log1p(jnp.exp(-jnp.abs(z)))


def _tri(n, lower):
    r = lax.broadcasted_iota(jnp.int32, (n, n), 0)
    c = lax.broadcasted_iota(jnp.int32, (n, n), 1)
    return jnp.where((c <= r) if lower else (c >= r), 1.0, 0.0).astype(F32)


def _f32dot(a, b):
    return jnp.dot(a, b, preferred_element_type=F32, precision=lax.Precision.HIGHEST)


def _forget_cumsum(flog, col, bf):
    T = flog.shape[0]
    bt = _tile(T, 512)

    def body(f_ref, b_ref, c_ref, carry):
        @pl.when(pl.program_id(0) == 0)
        def _():
            carry[...] = jnp.zeros_like(carry)

        lf = _log_sigmoid(f_ref[...] + b_ref[...])
        c = _f32dot(_tri(bt, True), lf) + carry[...]
        c_ref[...] = c * LOG2E
        carry[...] = c[bt - 1:bt, :]

    return pl.pallas_call(
        body, name="forget_cumsum", grid=(T // bt,),
        in_specs=[pl.BlockSpec((bt, N_FLOG), lambda i: (i, col)), _resident((1, N_FLOG))],
        out_specs=pl.BlockSpec((bt, N_FLOG), lambda i: (i, 0)),
        out_shape=jax.ShapeDtypeStruct((T, N_FLOG), F32),
        scratch_shapes=[pltpu.VMEM((1, N_FLOG), F32)],
        compiler_params=_cp(1),
    )(flog, bf)


def _forget_bwd(dck, dcq, flog, col, bf):
    T = dcq.shape[0]
    bt = _tile(T, 512)
    nb = T // bt

    def body(k0_ref, k1_ref, k2_ref, k3_ref, dcq_ref, f_ref, b_ref, dz_ref, db_ref, carry):
        @pl.when(pl.program_id(0) == 0)
        def _():
            carry[...] = jnp.zeros_like(carry)
            db_ref[...] = jnp.zeros_like(db_ref)

        dc = ((k0_ref[...] + k1_ref[...]) + (k2_ref[...] + k3_ref[...])) + dcq_ref[...]
        dlf = _f32dot(_tri(bt, False), dc) + carry[...]
        carry[...] = dlf[0:1, :]
        z = f_ref[...] + b_ref[...]
        dz = dlf * _sigmoid(-z)
        dz_ref[...] = dz.astype(BF16)
        db_ref[...] += _rowsum(dz)

    slab = lambda j: pl.BlockSpec((None, bt, N_FLOG), lambda i: (j, nb - 1 - i, 0))
    return pl.pallas_call(
        body, name="forget_bwd", grid=(nb,),
        in_specs=[slab(0), slab(1), slab(2), slab(3),
                  pl.BlockSpec((bt, N_FLOG), lambda i: (nb - 1 - i, 0)),
                  pl.BlockSpec((bt, N_FLOG), lambda i: (nb - 1 - i, col)), _resident((1, N_FLOG))],
        out_specs=[pl.BlockSpec((bt, N_FLOG), lambda i: (nb - 1 - i, 0)), _resident((1, N_FLOG))],
        out_shape=[jax.ShapeDtypeStruct((T, N_FLOG), BF16), jax.ShapeDtypeStruct((1, N_FLOG), F32)],
        scratch_shapes=[pltpu.VMEM((1, N_FLOG), F32)],
        compiler_params=_cp(1),
    )(dck, dck, dck, dck, dcq, flog, bf)


def _head_masks():
    lane = lax.broadcasted_iota(jnp.int32, (1, LANE), 1)
    return lane < HEAD_DIM


def _split_heads(x2, is_a):
    zero = jnp.zeros_like(x2)
    return jnp.where(is_a, x2, zero), jnp.where(is_a, zero, x2)


BIAS_PARTS = 3


def _bias_lanes(h):
    lane = lax.broadcasted_iota(jnp.int32, (1, LANE), 1)
    first = (1 - h) * HEAD_DIM
    return lane, first


def _fold_key_bias(qkv, c):
    T = qkv.shape[0]
    tm = _tile(T, 512)
    npair = N_HEADS // 2

    def body(k_ref, c_ref, o_ref):
        cc = c_ref[...]
        parts, rest = [], cc
        for _ in range(BIAS_PARTS):
            piece = rest.astype(BF16)
            parts.append(piece)
            rest = rest - piece.astype(F32)
        for j in range(npair):
            k2 = k_ref[:, j * LANE:(j + 1) * LANE]
            for h in range(2):
                lane, first = _bias_lanes(h)
                out = k2
                for n, piece in enumerate(parts):
                    col = piece[:, 2 * j + h:2 * j + h + 1]
                    out = jnp.where(lane == first + n, col, out)
                o_ref[:, (2 * j + h) * LANE:(2 * j + h + 1) * LANE] = out

    return pl.pallas_call(
        body, name="fold_key_bias", grid=(T // tm,),
        in_specs=[pl.BlockSpec((tm, D_ATTN), lambda i: (i, 1)), pl.BlockSpec((tm, N_FLOG), lambda i: (i, 0))],
        out_specs=pl.BlockSpec((tm, 2 * D_ATTN), lambda i: (i, 0)),
        out_shape=jax.ShapeDtypeStruct((T, 2 * D_ATTN), BF16),
        compiler_params=_cp(1),
    )(qkv, c)


def _attn_fwd(qkv, vt, kb):
    T = qkv.shape[0]
    tq = _tile(T, 512)
    tk = tq
    nq = T // tq
    npair = N_HEADS // 2

    def body(q_ref, ka_ref, kb_ref, vt_ref, o_ref, al_ref, m_s, l_s, acc_s):
        i = pl.program_id(1)
        qs = []
        for h, qh in enumerate(_split_heads(q_ref[...], _head_masks())):
            lane, first = _bias_lanes(h)
            qs.append(jnp.where((lane >= first) & (lane < first + BIAS_PARTS), -1.0, qh).astype(BF16))
        k_refs = (ka_ref, kb_ref)
        m_s[...] = jnp.full_like(m_s, NEG_INF)
        l_s[...] = jnp.zeros_like(l_s)
        acc_s[...] = jnp.zeros_like(acc_s)

        def scores_at(kk):
            k0 = pl.multiple_of(kk * tk, tk)
            return tuple(_nt(k_refs[h][pl.ds(k0, tk), :], qs[h]) for h in range(2))

        def consume(kk, scores, masked):
            k0 = pl.multiple_of(kk * tk, tk)
            v2t = vt_ref[:, pl.ds(k0, tk)]
            for h in range(2):
                zt = scores[h]
                if masked:
                    rr = lax.broadcasted_iota(jnp.int32, (tk, tq), 0)
                    cc = lax.broadcasted_iota(jnp.int32, (tk, tq), 1)
                    zt = jnp.where(cc >= rr, zt, NEG_INF)
                m_old = m_s[h]
                m_new = jnp.maximum(m_old, jnp.max(zt, axis=0, keepdims=True))
                p = jnp.exp2(zt - m_new)
                a = jnp.exp2(m_old - m_new)
                l_s[h] = a * l_s[h] + jnp.sum(p, axis=0, keepdims=True)
                acc_s[h] = a * acc_s[h] + _nn(v2t, p.astype(BF16))
                m_s[h] = m_new

        def two_tiles(kk, second_masked):
            first, second = scores_at(kk), scores_at(kk + 1)
            consume(kk, first, False)
            consume(kk + 1, second, second_masked)

        def loop_body(t, carry):
            two_tiles(2 * t, False)
            return carry

        lax.fori_loop(0, i // 2, loop_body, 0)

        @pl.when(i % 2 == 1)
        def _():
            two_tiles(i - 1, True)

        @pl.when(i % 2 == 0)
        def _():
            consume(i, scores_at(i), True)

        outs = []
        for h in range(2):
            l = l_s[h]
            outs.append(acc_s[h] * (1.0 / l))
            al_ref[0, h:h + 1, :] = -(m_s[h] + jnp.log2(l))
        al_ref[0, 2:8, :] = jnp.zeros((6, tq), F32)
        dim = lax.broadcasted_iota(jnp.int32, (LANE, 1), 0)
        o_ref[...] = jnp.where(dim < HEAD_DIM, outs[0], outs[1]).T

    rowl = pl.BlockSpec((1, 8, tq), lambda j, i: (j, 0, i))
    return pl.pallas_call(
        body, name="attn_fwd", grid=(npair, nq),
        in_specs=[pl.BlockSpec((tq, LANE), lambda j, i: (i, j)),
                  pl.BlockSpec((T, LANE), lambda j, i: (0, 2 * j), pipeline_mode=pl.Buffered(1)),
                  pl.BlockSpec((T, LANE), lambda j, i: (0, 2 * j + 1), pipeline_mode=pl.Buffered(1)),
                  pl.BlockSpec((LANE, T), lambda j, i: (2 * npair + j, 0), pipeline_mode=pl.Buffered(1))],
        out_specs=[pl.BlockSpec((tq, LANE), lambda j, i: (i, j)), rowl],
        out_shape=[jax.ShapeDtypeStruct((T, D_ATTN), F32), jax.ShapeDtypeStruct((npair, 8, T), F32)],
        scratch_shapes=[pltpu.VMEM((2, 1, tq), F32), pltpu.VMEM((2, 1, tq), F32), pltpu.VMEM((2, LANE, tq), F32)],
        compiler_params=_cp(2),
    )(qkv, kb, kb, vt)


def _attn_bwd(qkv, qkvt, dob, dobt, cb, alrow, dlrow):
    T = qkv.shape[0]
    tq = _tile(T, 512)
    tk = tq
    nq = T // tq
    npair = N_HEADS // 2

    def body(q_ref, qt_ref, k_ref, kt_ref, v_ref, do_ref, dot_ref, cb_ref, al_ref, dl_ref,
             dq_ref, dk_ref, dv_ref, dc_ref, dcq_ref, dk_s, dv_s, dc_s):
        kj = pl.program_id(1)
        is_a = _head_masks()
        ks = _split_heads(k_ref[...], is_a)
        vs = _split_heads(v_ref[...], is_a)
        dim_a = lax.broadcasted_iota(jnp.int32, (LANE, 1), 0) < HEAD_DIM
        kts = _split_heads(kt_ref[...], dim_a)
        cs = (cb_ref[:, 0:1], cb_ref[:, HEAD_DIM:HEAD_DIM + 1])

        @pl.when(kj == 0)
        def _():
            dq_ref[...] = jnp.zeros_like(dq_ref)
            dcq_ref[...] = jnp.zeros_like(dcq_ref)

        dk_s[...] = jnp.zeros_like(dk_s)
        dv_s[...] = jnp.zeros_like(dv_s)
        dc_s[...] = jnp.zeros_like(dc_s)

        def step(qi, masked):
            q0 = pl.multiple_of(qi * tq, tq)
            q2 = q_ref[pl.ds(q0, tq), :]
            do2 = do_ref[pl.ds(q0, tq), :]
            qt2 = qt_ref[:, pl.ds(q0, tq)]
            dot2 = dot_ref[:, pl.ds(q0, tq)]
            for h in range(2):
                alr = al_ref[0, h:h + 1, pl.ds(q0, tq)]
                dlr = dl_ref[0, h:h + 1, pl.ds(q0, tq)]
                zt = _nt(ks[h], q2) + (alr - cs[h])
                if masked:
                    rr = lax.broadcasted_iota(jnp.int32, (tk, tq), 0)
                    cc = lax.broadcasted_iota(jnp.int32, (tk, tq), 1)
                    zt = jnp.where(cc >= rr, zt, NEG_INF)
                pt = jnp.exp2(zt)
                dst = pt * (_nt(vs[h], do2) - dlr)
                pb = pt.astype(BF16)
                dsb = dst.astype(BF16)
                dv_s[h] += _nt(dot2, pb)
                dk_s[h] += _nt(qt2, dsb)
                dc_s[h] += jnp.sum(dst, axis=-1, keepdims=True)
                dcq_ref[0, h:h + 1, pl.ds(q0, tq)] += jnp.sum(dst, axis=0, keepdims=True)
                dq_ref[:, pl.ds(q0, tq)] += _nn(kts[h], dsb)

        step(kj, True)

        def loop_body(qi, carry):
            step(qi, False)
            return carry

        lax.fori_loop(kj + 1, nq, loop_body, 0)
        dk_ref[...] = (jnp.where(dim_a, dk_s[0], dk_s[1]) * (1.0 / LOG2E)).astype(BF16)
        dv_ref[...] = jnp.where(dim_a, dv_s[0], dv_s[1]).astype(BF16)
        lane = lax.broadcasted_iota(jnp.int32, (1, LANE), 1)
        head = 2 * pl.program_id(0)
        dc_ref[...] = jnp.where(lane == head, -dc_s[0], jnp.where(lane == head + 1, -dc_s[1], 0.0))

        @pl.when(kj == nq - 1)
        def _():
            dq_ref[...] = dq_ref[...] * Q_SCALE

    full = lambda col: pl.BlockSpec((T, LANE), lambda j, kj: (0, col(j)), pipeline_mode=pl.Buffered(1))
    fullt = lambda row: pl.BlockSpec((LANE, T), lambda j, kj: (row(j), 0), pipeline_mode=pl.Buffered(1))
    tile = lambda col: pl.BlockSpec((tk, LANE), lambda j, kj: (kj, col(j)))
    tilet = lambda row: pl.BlockSpec((LANE, tk), lambda j, kj: (row(j), kj))
    rowl = pl.BlockSpec((1, 8, T), lambda j, kj: (j, 0, 0))
    return pl.pallas_call(
        body, name="attn_bwd", grid=(npair, nq),
        in_specs=[full(lambda j: j), fullt(lambda j: j), tile(lambda j: npair + j), tilet(lambda j: npair + j),
                  tile(lambda j: 2 * npair + j), full(lambda j: j), fullt(lambda j: j), tile(lambda j: j), rowl, rowl],
        out_specs=[pl.BlockSpec((LANE, T), lambda j, kj: (j, 0)), tilet(lambda j: j), tilet(lambda j: j),
                   pl.BlockSpec((None, tk, LANE), lambda j, kj: (j, kj, 0)), rowl],
        out_shape=[jax.ShapeDtypeStruct((D_ATTN, T), F32), jax.ShapeDtypeStruct((D_ATTN, T), BF16),
                   jax.ShapeDtypeStruct((D_ATTN, T), BF16), jax.ShapeDtypeStruct((npair, T, LANE), F32),
                   jax.ShapeDtypeStruct((npair, 8, T), F32)],
        scratch_shapes=[pltpu.VMEM((2, LANE, tk), F32), pltpu.VMEM((2, LANE, tk), F32), pltpu.VMEM((2, tk, 1), F32)],
        compiler_params=_cp(2),
    )(qkv, qkvt, qkv, qkvt, qkv, dob, dobt, cb, alrow, dlrow)


HALO = 8


def _shift_rows(cur, other, k, tm, down):
    row = lax.broadcasted_iota(jnp.int32, (tm, 1), 0)
    reps = tm // HALO
    if down:
        rolled = pltpu.roll(cur, k, 0)
        fill = jnp.tile(pltpu.roll(other, k, 0), (reps, 1))
        return jnp.where(row < k, fill, rolled)
    rolled = pltpu.roll(cur, tm - k, 0)
    fill = jnp.tile(pltpu.roll(other, HALO - k, 0), (reps, 1))
    return jnp.where(row >= tm - k, fill, rolled)


def _conv_fwd(c, hh, c_prev, hh_prev, w_ref, first, tm):
    u = c * hh
    u_prev = jnp.where(first, 0.0, c_prev * hh_prev)
    u1 = _shift_rows(u, u_prev, 1, tm, True)
    u2 = _shift_rows(u, u_prev, 2, tm, True)
    y = w_ref[0:1, :] * u2 + w_ref[1:2, :] * u1 + w_ref[2:3, :] * u
    return u, u1, u2, y


def _rms(x, g):
    rs = lax.rsqrt(jnp.mean(x * x, axis=-1, keepdims=True) + RMS_EPS)
    return x * rs * g, rs


def _mixer_tail_fwd(o, bchf, conv_w, g_attn, g_conv, w_mo, x1, lg, lb):
    T = o.shape[0]
    tm = _tile(T, 512)
    hb = tm // HALO

    def body(o_ref, b_ref, c_ref, h_ref, cp_ref, hp_ref, w_ref, ga_ref, gc_ref, wmo_ref, x1_ref, lg_ref, lb_ref,
             x2_ref, r2_ref, mg_ref):
        first = pl.program_id(0) == 0
        _, _, _, y = _conv_fwd(c_ref[...], h_ref[...], cp_ref[...], hp_ref[...], w_ref, first, tm)
        na, _ = _rms(o_ref[...], ga_ref[...])
        nc, _ = _rms(b_ref[...] * y, gc_ref[...])
        nab = na.astype(BF16)
        ncb = nc.astype(BF16)
        mg_ref[:, 0:D_ATTN] = nab
        mg_ref[:, D_ATTN:] = ncb
        r2 = ALPHA * x1_ref[...] + _nn(nab, wmo_ref[0:D_ATTN, :]) + _nn(ncb, wmo_ref[D_ATTN:, :])
        r2_ref[...] = r2
        xhat, _ = _ln_stats(r2)
        x2_ref[...] = xhat * lg_ref[...] + lb_ref[...]

    row = lambda n, col=0: pl.BlockSpec((tm, n), lambda i: (i, col))
    prev = lambda col: pl.BlockSpec((HALO, D_CONV), lambda i: (jnp.maximum(i * hb - 1, 0), col))
    return pl.pallas_call(
        body, name="mixer_tail_fwd", grid=(T // tm,),
        in_specs=[row(D_ATTN), row(D_CONV, 0), row(D_CONV, 1), row(D_CONV, 2), prev(1), prev(2),
                  _resident((3, D_CONV)), _resident((1, D_ATTN)), _resident((1, D_CONV)),
                  _resident((D_MODEL, D_MODEL)), row(D_MODEL), _resident((1, D_MODEL)), _resident((1, D_MODEL))],
        out_specs=[row(D_MODEL), row(D_MODEL), row(D_MODEL)],
        out_shape=[jax.ShapeDtypeStruct((T, D_MODEL), F32), jax.ShapeDtypeStruct((T, D_MODEL), F32),
                   jax.ShapeDtypeStruct((T, D_MODEL), BF16)],
        compiler_params=_cp(1),
    )(o, bchf, bchf, bchf, bchf, bchf, conv_w, g_attn, g_conv, w_mo, x1, lg, lb)


def _head_sum_rows():
    row = lax.broadcasted_iota(jnp.int32, (4 * 8, D_ATTN), 0)
    head = lax.broadcasted_iota(jnp.int32, (4 * 8, D_ATTN), 1) // HEAD_DIM
    return jnp.where((row % 8 < 2) & (2 * (row // 8) + row % 8 == head), 1.0, 0.0).astype(F32)


def _mixer_tail_bwd(dx2, r2, lg, w_mo, o, bchf, conv_w, g_attn, g_conv):
    T = o.shape[0]
    tm = _tile(T, 256)
    hb = tm // HALO

    def body(dx2_ref, r2_ref, lg_ref, wmo_ref, o_ref, b_ref, c_ref, h_ref, cp_ref, hp_ref, w_ref, ga_ref, gc_ref,
             dx1_ref, dr_ref, do_ref, dot_ref, dl_ref, dco_ref, dlg_ref, dlb_ref, dga_ref, dgc_ref):
        i = pl.program_id(0)

        @pl.when(i == 0)
        def _():
            for ref in (dlg_ref, dlb_ref, dga_ref, dgc_ref):
                ref[...] = jnp.zeros_like(ref)

        dy = dx2_ref[...]
        xhat, rstd = _ln_stats(r2_ref[...])
        dr = _ln_bwd(dy, xhat, rstd, lg_ref[...])
        dlg_ref[...] += _rowsum(dy * xhat)
        dlb_ref[...] += _rowsum(dy)
        dx1_ref[...] = ALPHA * dr
        drb = dr.astype(BF16)
        dr_ref[...] = drb
        dna = _nt(drb, wmo_ref[0:D_ATTN, :])
        dnc = _nt(drb, wmo_ref[D_ATTN:, :])

        def rms_bwd(x, g, dn):
            rs = lax.rsqrt(jnp.mean(x * x, axis=-1, keepdims=True) + RMS_EPS)
            dng = dn * g
            dx = rs * dng - x * (rs * rs * rs) * jnp.mean(dng * x, axis=-1, keepdims=True)
            return dx, _rowsum(dn * x * rs)

        oo = o_ref[...]
        do, dga = rms_bwd(oo, ga_ref[...], dna)
        dga_ref[...] += dga
        do_ref[...] = do.astype(BF16)
        dot_ref[...] = do.T.astype(BF16)
        dl_ref[...] = lax.dot_general(_head_sum_rows(), do * oo, (((1,), (1,)), ((), ())),
                                      preferred_element_type=F32, precision=lax.Precision.HIGHEST)
        _, _, _, y = _conv_fwd(c_ref[...], h_ref[...], cp_ref[...], hp_ref[...], w_ref, i == 0, tm)
        dco, dgc = rms_bwd(b_ref[...] * y, gc_ref[...], dnc)
        dgc_ref[...] += dgc
        dco_ref[...] = dco

    row = lambda n, col=0: pl.BlockSpec((tm, n), lambda i: (i, col))
    prev = lambda col: pl.BlockSpec((HALO, D_CONV), lambda i: (jnp.maximum(i * hb - 1, 0), col))
    vec = lambda n: _resident((1, n))
    return pl.pallas_call(
        body, name="mixer_tail_bwd", grid=(T // tm,),
        in_specs=[row(D_MODEL), row(D_MODEL), vec(D_MODEL), _resident((D_MODEL, D_MODEL)), row(D_ATTN),
                  row(D_CONV, 0), row(D_CONV, 1), row(D_CONV, 2), prev(1), prev(2), _resident((3, D_CONV)),
                  vec(D_ATTN), vec(D_CONV)],
        out_specs=[row(D_MODEL), row(D_MODEL), row(D_ATTN), pl.BlockSpec((D_ATTN, tm), lambda i: (0, i)),
                   pl.BlockSpec((4 * 8, tm), lambda i: (0, i)), row(D_CONV),
                   vec(D_MODEL), vec(D_MODEL), vec(D_ATTN), vec(D_CONV)],
        out_shape=[jax.ShapeDtypeStruct((T, D_MODEL), F32), jax.ShapeDtypeStruct((T, D_MODEL), BF16),
                   jax.ShapeDtypeStruct((T, D_ATTN), BF16), jax.ShapeDtypeStruct((D_ATTN, T), BF16),
                   jax.ShapeDtypeStruct((4 * 8, T), F32),
                   jax.ShapeDtypeStruct((T, D_CONV), F32), jax.ShapeDtypeStruct((1, D_MODEL), F32),
                   jax.ShapeDtypeStruct((1, D_MODEL), F32), jax.ShapeDtypeStruct((1, D_ATTN), F32),
                   jax.ShapeDtypeStruct((1, D_CONV), F32)],
        compiler_params=_cp(1),
    )(dx2, r2, lg, w_mo, o, bchf, bchf, bchf, bchf, bchf, conv_w, g_attn, g_conv)


def _conv_bwd(dco, bchf, conv_w):
    T = dco.shape[0]
    tm = _tile(T, 512)
    hb = tm // HALO
    nt = T // tm

    def body(dco_ref, dcon_ref, b_ref, bn_ref, c_ref, h_ref, cp_ref, hp_ref, w_ref, dbch_ref, dw_ref):
        i = pl.program_id(0)

        @pl.when(i == 0)
        def _():
            dw_ref[...] = jnp.zeros_like(dw_ref)

        cc = c_ref[...]
        hh = h_ref[...]
        u, u1, u2, y = _conv_fwd(cc, hh, cp_ref[...], hp_ref[...], w_ref, i == 0, tm)
        dco = dco_ref[...]
        bb = b_ref[...]
        dyc = dco * bb
        dy_next = jnp.where(i == nt - 1, 0.0, dcon_ref[...] * bn_ref[...])
        d1 = _shift_rows(dyc, dy_next, 1, tm, False)
        d2 = _shift_rows(dyc, dy_next, 2, tm, False)
        du = w_ref[2:3, :] * dyc + w_ref[1:2, :] * d1 + w_ref[0:1, :] * d2
        dbch_ref[:, 0:D_CONV] = (dco * y).astype(BF16)
        dbch_ref[:, D_CONV:2 * D_CONV] = (du * hh).astype(BF16)
        dbch_ref[:, 2 * D_CONV:] = (du * cc).astype(BF16)
        dw_ref[0:1, :] += _rowsum(dyc * u2)
        dw_ref[1:2, :] += _rowsum(dyc * u1)
        dw_ref[2:3, :] += _rowsum(dyc * u)

    row = lambda n, col=0: pl.BlockSpec((tm, n), lambda i: (i, col))
    prev = lambda col: pl.BlockSpec((HALO, D_CONV), lambda i: (jnp.maximum(i * hb - 1, 0), col))
    nxt = lambda col: pl.BlockSpec((HALO, D_CONV), lambda i: (jnp.minimum((i + 1) * hb, T // HALO - 1), col))
    return pl.pallas_call(
        body, name="conv_bwd", grid=(nt,),
        in_specs=[row(D_CONV), nxt(0), row(D_CONV, 0), nxt(0), row(D_CONV, 1), row(D_CONV, 2), prev(1), prev(2),
                  _resident((3, D_CONV))],
        out_specs=[row(3 * D_CONV), _resident((8, D_CONV))],
        out_shape=[jax.ShapeDtypeStruct((T, 3 * D_CONV), BF16), jax.ShapeDtypeStruct((8, D_CONV), F32)],
        compiler_params=_cp(1),
    )(dco, dco, bchf, bchf, bchf, bchf, bchf, bchf, conv_w)


def _mixer_in_bwd(dx1a, dqt, dkt, dvt, dbch, dfl, w_qkvt, w_bch, w_f):
    T = dx1a.shape[0]
    tm = _tile(T, 512)

    def body(a_ref, dq_ref, dk_ref, dv_ref, db_ref, df_ref, wq_ref, wb_ref, wf_ref, o_ref):
        acc = a_ref[...] + _nt(db_ref[...], wb_ref[...]) + _nt(df_ref[...], wf_ref[...])
        for n, ref in enumerate((dq_ref, dk_ref, dv_ref)):
            acc = acc + _tn(ref[...].astype(BF16), wq_ref[n * D_ATTN:(n + 1) * D_ATTN, :])
        o_ref[...] = acc

    row = lambda n: pl.BlockSpec((tm, n), lambda i: (i, 0))
    col = pl.BlockSpec((D_ATTN, tm), lambda i: (0, i))
    return pl.pallas_call(
        body, name="mixer_in_bwd", grid=(T // tm,),
        in_specs=[row(D_MODEL), col, col, col, row(3 * D_CONV), row(N_FLOG),
                  _resident((3 * D_ATTN, D_MODEL)), _resident((D_MODEL, 3 * D_CONV)), _resident((D_MODEL, N_FLOG))],
        out_specs=row(D_MODEL),
        out_shape=jax.ShapeDtypeStruct((T, D_MODEL), F32),
        compiler_params=_cp(1),
    )(dx1a, dqt, dkt, dvt, dbch, dfl, w_qkvt, w_bch, w_f)


def _ple_loss(x3, p, tgt, w_g, w_p, b_g, lg, lb):
    T = x3.shape[0]
    tm = _tile(T, 512)

    def body(x_ref, p_ref, t_ref, wg_ref, wp_ref, bg_ref, lg_ref, lb_ref,
             dx_ref, de_ref, dz_ref, loss_ref, dlg_ref, dlb_ref, dbg_ref):
        @pl.when(pl.program_id(0) == 0)
        def _():
            for ref in (loss_ref, dlg_ref, dlb_ref, dbg_ref):
                ref[...] = jnp.zeros_like(ref)

        xf = x_ref[...]
        gate = _sigmoid(_nn(xf.astype(BF16), wg_ref[...]) + bg_ref[...])
        e = _nn(p_ref[...].astype(BF16), wp_ref[...])
        xhat, rstd = _ln_stats(ALPHA * xf + gate * e)
        err = xhat * lg_ref[...] + lb_ref[...] - t_ref[...]
        sq = jnp.sum(_rowsum(err * err), axis=-1, keepdims=True)
        loss_ref[...] += jnp.broadcast_to(sq * (0.5 / D_MODEL), loss_ref.shape)
        dy = err * (1.0 / D_MODEL)
        dr = _ln_bwd(dy, xhat, rstd, lg_ref[...])
        dlg_ref[...] += _rowsum(dy * xhat)
        dlb_ref[...] += _rowsum(dy)
        de_ref[...] = (dr * gate).astype(BF16)
        dz = dr * e * gate * (1.0 - gate)
        dbg_ref[...] += _rowsum(dz)
        dzb = dz.astype(BF16)
        dz_ref[...] = dzb
        dx_ref[...] = ALPHA * dr + _nt(dzb, wg_ref[...])

    row = lambda n: pl.BlockSpec((tm, n), lambda i: (i, 0))
    vec = lambda n: _resident((1, n))
    return pl.pallas_call(
        body, name="ple_loss", grid=(T // tm,),
        in_specs=[row(D_MODEL), row(PLE_DIM), row(D_MODEL), _resident((D_MODEL, D_MODEL)),
                  _resident((PLE_DIM, D_MODEL)), vec(D_MODEL), vec(D_MODEL), vec(D_MODEL)],
        out_specs=[row(D_MODEL), row(D_MODEL), row(D_MODEL), vec(LANE), vec(D_MODEL), vec(D_MODEL), vec(D_MODEL)],
        out_shape=[jax.ShapeDtypeStruct((T, D_MODEL), F32), jax.ShapeDtypeStruct((T, D_MODEL), BF16),
                   jax.ShapeDtypeStruct((T, D_MODEL), BF16), jax.ShapeDtypeStruct((1, LANE), F32),
                   jax.ShapeDtypeStruct((1, D_MODEL), F32), jax.ShapeDtypeStruct((1, D_MODEL), F32),
                   jax.ShapeDtypeStruct((1, D_MODEL), F32)],
        compiler_params=_cp(1),
    )(x3, p, tgt, w_g, w_p, b_g, lg, lb)


def _lane_layout(v8):
    return jnp.repeat(v8, HEAD_DIM, axis=1)


def _row_layout(v8):
    t = v8.shape[0]
    return jnp.pad(v8.T.reshape(N_HEADS // 2, 2, t), ((0, 0), (0, 6), (0, 0)))


def _from_lane_layout(vl):
    return vl[:, ::HEAD_DIM]


def _from_row_layout(vr):
    return vr[:, :2, :].reshape(N_HEADS, -1).T


def _local_step(x, p, tgt, w, overlap=None):
    bf = lambda a: a.astype(BF16)
    w1i, w1o = bf(w["ffn1_w_in"]), bf(w["ffn1_w_out"])
    first = _ffn_fwd(x, w1i, w1o, w["ln1_g"], w["ln1_b"], "ffn1_fwd", overlap["gather"] if overlap else ())
    x1, r1, g1, u1, h1 = first[:5]
    if overlap:
        w = {**w, **overlap["weights"](first[5:])}
    w2i, w2o = bf(w["ffn2_w_in"]), bf(w["ffn2_w_out"])
    wmi = w["w_mix_in"]
    o_f = 3 * D_ATTN
    o_b = o_f + N_HEADS
    w_qkv = bf(wmi[:, :o_f])
    w_f = bf(jnp.pad(wmi[:, o_f:o_b], ((0, 0), (0, N_FLOG - N_HEADS))))
    w_bch = bf(wmi[:, o_b:])
    w_bchf = jnp.concatenate([w_bch, w_f], axis=1)
    w_mo, w_g, w_p = bf(w["w_mix_out"]), bf(w["w_ple_gate"]), bf(w["w_ple"])
    b_f = jnp.pad(w["b_forget"], ((0, 0), (0, N_FLOG - N_HEADS)))

    q_scale = jnp.concatenate([jnp.full((1, D_ATTN), Q_SCALE * LOG2E, F32), jnp.ones((1, 2 * D_ATTN), F32)], axis=1)
    qkv = _matmul_nn(x1, w_qkv, q_scale, BF16, "proj_qkv")
    bchf = _matmul_nn(x1, w_bchf, jnp.ones((1, 3 * D_CONV + N_FLOG), F32), F32, "proj_bchf")
    fcol = 3 * D_CONV // N_FLOG
    c = _forget_cumsum(bchf, fcol, b_f)
    c8 = c[:, :N_HEADS]
    cb = _lane_layout(c8)
    qkvt = qkv.T
    o, alrow = _attn_fwd(qkv, qkvt, _fold_key_bias(qkv, c))
    x2, r2, merged = _mixer_tail_fwd(o, bchf, w["conv_w"], w["g_attn"], w["g_conv"], w_mo, x1, w["ln2_g"], w["ln2_b"])
    x3, r3, g2, u2, h2 = _ffn_fwd(x2, w2i, w2o, w["ln3_g"], w["ln3_b"], "ffn2_fwd")

    grads = {}
    dx3, de, dz, loss, grads["ln4_g"], grads["ln4_b"], grads["b_ple_gate"] = _ple_loss(
        x3, p, tgt, w_g, w_p, w["b_ple_gate"], w["ln4_g"], w["ln4_b"])
    grads["w_ple"] = _matmul_tn(p, de, "dw_ple")
    grads["w_ple_gate"] = _matmul_tn(x3, dz, "dw_ple_gate")

    dx2, dgu2, df2, grads["ln3_g"], grads["ln3_b"] = _ffn_bwd(dx3, r3, g2, u2, w2i, w2o, w["ln3_g"], "ffn2_bwd")
    grads["ffn2_w_in"] = _matmul_tn(x2, dgu2, "dw_ffn2_in")
    grads["ffn2_w_out"] = _matmul_tn(h2, df2, "dw_ffn2_out")

    (dx1a, dr2, dob, dobt, delta, dco, grads["ln2_g"], grads["ln2_b"], grads["g_attn"],
     grads["g_conv"]) = _mixer_tail_bwd(dx2, r2, w["ln2_g"], w_mo, o, bchf, w["conv_w"], w["g_attn"], w["g_conv"])
    grads["w_mix_out"] = _matmul_tn(merged, dr2, "dw_mix_out")
    dbch, dcw = _conv_bwd(dco, bchf, w["conv_w"])
    grads["conv_w"] = dcw[:3]
    dqt, dkt, dvt, dck, dcq = _attn_bwd(qkv, qkvt, dob, dobt, cb, alrow, delta.reshape(N_HEADS // 2, 8, -1))
    dcq_lanes = jnp.pad(_from_row_layout(dcq), ((0, 0), (0, N_FLOG - N_HEADS)))
    dfl, dbf = _forget_bwd(dck, dcq_lanes, bchf, fcol, b_f)
    grads["b_forget"] = dbf[:, :N_HEADS]
    dx1 = _mixer_in_bwd(dx1a, dqt, dkt, dvt, dbch, dfl, w_qkv.T, w_bch, w_f)
    grads["w_mix_in"] = jnp.concatenate(
        [_matmul_tokens(dqt, x1, "dw_q").T, _matmul_tokens(dkt, x1, "dw_k").T, _matmul_tokens(dvt, x1, "dw_v").T,
         _matmul_tn(x1, dfl, "dw_flog")[:, :N_HEADS], _matmul_tn(x1, dbch, "dw_bch")], axis=1)

    sums = overlap["chip_sums"](grads) if overlap else ()
    last = _ffn_bwd(dx1, r1, g1, u1, w1i, w1o, w["ln1_g"], "ffn1_bwd", sums)
    dx0, dgu1, df1, grads["ln1_g"], grads["ln1_b"] = last[:5]
    grads["ffn1_w_in"] = _matmul_tn(x, dgu1, "dw_ffn1_in")
    grads["ffn1_w_out"] = _matmul_tn(h1, df1, "dw_ffn1_out")
    if overlap:
        return loss, dx0, grads, sums, last[5:]
    return loss, dx0, grads


WEIGHTS = ["ffn1_w_in", "ffn1_w_out", "ln1_g", "ln1_b", "w_mix_in", "b_forget", "conv_w", "g_attn", "g_conv",
           "w_mix_out", "ln2_g", "ln2_b", "ffn2_w_in", "ffn2_w_out", "ln3_g", "ln3_b", "w_ple", "w_ple_gate",
           "b_ple_gate", "ln4_g", "ln4_b"]
LAYOUT = {
    "ffn1_w_in": ((D_MODEL, 2 * D_FF), 1), "ffn1_w_out": ((D_FF, D_MODEL), 0),
    "w_mix_in": ((D_MODEL, 3 * D_ATTN + N_HEADS + 3 * D_CONV), 1), "conv_w": ((3, D_CONV), 1),
    "w_mix_out": ((D_MODEL, D_MODEL), 0), "ffn2_w_in": ((D_MODEL, 2 * D_FF), 1), "ffn2_w_out": ((D_FF, D_MODEL), 0),
    "w_ple": ((PLE_DIM, D_MODEL), 1), "w_ple_gate": ((D_MODEL, D_MODEL), 0),
    "ln1_g": ((1, D_MODEL), None), "ln1_b": ((1, D_MODEL), None), "b_forget": ((1, N_HEADS), None),
    "g_attn": ((1, D_ATTN), None), "g_conv": ((1, D_CONV), None), "ln2_g": ((1, D_MODEL), None),
    "ln2_b": ((1, D_MODEL), None), "ln3_g": ((1, D_MODEL), None), "ln3_b": ((1, D_MODEL), None),
    "b_ple_gate": ((1, D_MODEL), None), "ln4_g": ((1, D_MODEL), None), "ln4_b": ((1, D_MODEL), None),
}
BIG = [n for n in WEIGHTS if LAYOUT[n][1] is not None and n != "conv_w"]
SMALL = [n for n in WEIGHTS if n not in BIG]
ROW = 1024
SMALL_ROWS = 16


def _shard_shape(name):
    shape, axis = LAYOUT[name]
    if axis is None:
        return shape
    return tuple(s // N_CHIPS if a == axis else s for a, s in enumerate(shape))


def _halves(a):
    return a.reshape(a.shape[:-2] + (2, a.shape[-2] // 2, a.shape[-1]))


def _split_chips(name, full):
    shape, axis = LAYOUT[name]
    if axis == 0:
        return full.reshape((N_CHIPS, shape[0] // N_CHIPS) + shape[1:])
    return jnp.moveaxis(full.reshape(shape[:1] + (N_CHIPS, shape[1] // N_CHIPS)), 1, 0)


def _join_chips(name, parts):
    shape, axis = LAYOUT[name]
    if axis == 0:
        return parts.reshape(shape)
    return jnp.moveaxis(parts, 0, 1).reshape(shape)


SMALL_AT = {"ln1_g": (0, 0), "ln1_b": (1, 0), "ln2_g": (2, 0), "ln2_b": (3, 0), "ln3_g": (4, 0), "ln3_b": (5, 0),
            "b_ple_gate": (6, 0), "ln4_g": (7, 0), "ln4_b": (8, 0), "g_attn": (9, 0), "g_conv": (9, D_ATTN),
            "b_forget": (10, 0), "conv_w": (10, LANE)}
CONV_SHARD = D_CONV // N_CHIPS


def _pack_small_grads(grads):
    def body(*refs):
        ins, o_ref = dict(zip(SMALL, refs[:-1])), refs[-1]
        o_ref[...] = jnp.zeros_like(o_ref)
        for s in range(N_CHIPS):
            for n in SMALL:
                r, c0 = SMALL_AT[n]
                if n == "conv_w":
                    for k in range(3):
                        o_ref[s, r:r + 1, c0 + k * CONV_SHARD:c0 + (k + 1) * CONV_SHARD] = (
                            ins[n][k:k + 1, s * CONV_SHARD:(s + 1) * CONV_SHARD])
                else:
                    o_ref[s, r:r + 1, c0:c0 + ins[n].shape[1]] = ins[n][...]

    return pl.pallas_call(
        body, name="pack_small_grads",
        out_shape=jax.ShapeDtypeStruct((N_CHIPS, SMALL_ROWS, ROW), F32),
    )(*[grads[n] for n in SMALL])


def _adamw_math(w, g, m, v):
    c1 = 1.0 - ADAM_B1 ** ADAM_STEP
    c2 = 1.0 - ADAM_B2 ** ADAM_STEP
    m = ADAM_B1 * m + (1.0 - ADAM_B1) * g
    v = ADAM_B2 * v + (1.0 - ADAM_B2) * (g * g)
    return -ADAM_LR * ((m / c1) / (jnp.sqrt(v / c2) + ADAM_EPS) + ADAM_WD * w), m, v


def _adamw_small(g_mine, g_sib, c_idx, w, m, v):
    ns = len(SMALL)

    def body(c_ref, gm_ref, gs_ref, *refs):
        ws, ms, vs = refs[:ns], refs[ns:2 * ns], refs[2 * ns:3 * ns]
        outs = refs[3 * ns:]
        mine_first = c_ref[0] == 0
        top = jnp.where(mine_first, gm_ref[...], gs_ref[...])
        bot = jnp.where(mine_first, gs_ref[...], gm_ref[...])
        for i, n in enumerate(SMALL):
            r, c0 = SMALL_AT[n]
            blk, rr = (top, r) if r < SMALL_ROWS // 2 else (bot, r - SMALL_ROWS // 2)
            rows, width = ws[i].shape
            for k in range(rows):
                g = blk[rr:rr + 1, c0 + k * width:c0 + (k + 1) * width]
                d, mn, vn = _adamw_math(ws[i][k:k + 1, :], g, ms[i][k:k + 1, :], vs[i][k:k + 1, :])
                for q, val in enumerate((g, d, mn, vn)):
                    outs[q * ns + i][k:k + 1, :] = val

    shapes = [jax.ShapeDtypeStruct(a.shape, F32) for a in w]
    vmem = pl.BlockSpec(memory_space=pltpu.VMEM)
    res = pl.pallas_call(
        body, name="adamw_small",
        in_specs=[pl.BlockSpec(memory_space=pltpu.SMEM)] + [vmem] * (2 + 3 * ns),
        out_specs=[vmem] * (4 * ns),
        out_shape=shapes * 4,
    )(c_idx, g_mine, g_sib, *w, *m, *v)
    return [res[q * ns:(q + 1) * ns] for q in range(4)]


def _place():
    x, y, c = lax.axis_index("x"), lax.axis_index("y"), lax.axis_index("c")
    others = [(1 - x, y), (x, 1 - y), (1 - x, 1 - y)]
    return x, y, c, others


ANY_SPEC = pl.BlockSpec(memory_space=pl.ANY)


def _remote(src, dst, send_sems, recv_sems, k, to):
    return pltpu.make_async_remote_copy(src_ref=src, dst_ref=dst, send_sem=send_sems.at[k], recv_sem=recv_sems.at[k],
                                        device_id=to, device_id_type=MESH)


def _all_gather(shards):
    n = len(shards)

    def body(*refs):
        ins, outs, send_sems, recv_sems = refs[:n], refs[n:2 * n], refs[2 * n], refs[2 * n + 1]
        _gather_start(ins, outs, send_sems, recv_sems)
        _gather_finish(ins, outs, send_sems, recv_sems)

    return pl.pallas_call(
        body, name="all_gather_weights",
        out_shape=_gather_shapes(shards), in_specs=[ANY_SPEC] * n, out_specs=[ANY_SPEC] * n,
        scratch_shapes=_gather_sems(n),
    )(*shards)


def _gather_shapes(shards):
    return [jax.ShapeDtypeStruct((N_CHIPS,) + a.shape, a.dtype) for a in shards]


def _gather_sems(n):
    return [pltpu.SemaphoreType.DMA((6 * n,)), pltpu.SemaphoreType.DMA((6 * n,))]


def _gather_sends(ins, outs, send_sems, recv_sems):
    x, y, c, others = _place()
    s = 2 * x + y
    return [_remote(ins[t].at[c], outs[t].at[s, c], send_sems, recv_sems, 6 * t + j, (*chip, c))
            for t in range(len(ins)) for j, chip in enumerate(others)]


def _gather_start(ins, outs, send_sems, recv_sems):
    for cp in _gather_sends(ins, outs, send_sems, recv_sems):
        cp.start()


def _gather_finish(ins, outs, send_sems, recv_sems):
    x, y, c, others = _place()
    slot = lambda t, chip, half: outs[t].at[2 * chip[0] + chip[1], half]
    passed = []
    for t in range(len(ins)):
        for j, chip in enumerate(others):
            landed = slot(t, chip, c)
            _remote(landed, landed, send_sems, recv_sems, 6 * t + j, (x, y, c)).wait_recv()
            passed.append(_remote(landed, landed, send_sems, recv_sems, 6 * t + 3 + j, (x, y, 1 - c)))
            passed[-1].start()
    for t in range(len(ins)):
        for j, chip in enumerate(others):
            landed = slot(t, chip, 1 - c)
            _remote(landed, landed, send_sems, recv_sems, 6 * t + 3 + j, (x, y, c)).wait_recv()
    for cp in _gather_sends(ins, outs, send_sems, recv_sems) + passed:
        cp.wait_send()


def _swap_halves(gs, tag):
    n = len(gs)

    def body(*refs):
        ins, outs, send_sems, recv_sems = refs[:n], refs[n:2 * n], refs[2 * n], refs[2 * n + 1]
        x, y, c, _ = _place()
        copies = [_remote(ins[t].at[:, 1 - c], outs[t], send_sems, recv_sems, t, (x, y, 1 - c)) for t in range(n)]
        for cp in copies:
            cp.start()
        for cp in copies:
            cp.wait()

    return pl.pallas_call(
        body, name="grad_swap_halves_" + tag,
        out_shape=[jax.ShapeDtypeStruct(g.shape[:1] + g.shape[2:], g.dtype) for g in gs],
        in_specs=[ANY_SPEC] * n, out_specs=[ANY_SPEC] * n,
        scratch_shapes=[pltpu.SemaphoreType.DMA((n,)), pltpu.SemaphoreType.DMA((n,))],
    )(*gs)


def _exchange_chips(pps):
    n = len(pps)

    def body(*refs):
        ins, outs, send_sems, recv_sems = refs[:n], refs[n:2 * n], refs[2 * n], refs[2 * n + 1]
        _exchange_start(ins, outs, send_sems, recv_sems)
        _exchange_finish(ins, outs, send_sems, recv_sems)

    return pl.pallas_call(
        body, name="grad_exchange_chips",
        out_shape=_exchange_shapes(pps), in_specs=[ANY_SPEC] * n, out_specs=[ANY_SPEC] * n,
        scratch_shapes=_exchange_sems(n),
    )(*pps)


def _exchange_shapes(pps):
    return [jax.ShapeDtypeStruct(p.shape, p.dtype) for p in pps]


def _exchange_sems(n):
    return [pltpu.SemaphoreType.DMA((3 * n,)), pltpu.SemaphoreType.DMA((3 * n,))]


def _exchange_sends(ins, outs, send_sems, recv_sems):
    x, y, c, others = _place()
    s = 2 * x + y
    return [_remote(ins[t].at[2 * chip[0] + chip[1]], outs[t].at[s], send_sems, recv_sems, 3 * t + j, (*chip, c))
            for t in range(len(ins)) for j, chip in enumerate(others)]


def _exchange_start(ins, outs, send_sems, recv_sems):
    for cp in _exchange_sends(ins, outs, send_sems, recv_sems):
        cp.start()


def _exchange_finish(ins, outs, send_sems, recv_sems):
    x, y, c, others = _place()
    for t in range(len(ins)):
        for j, chip in enumerate(others):
            landed = outs[t].at[2 * chip[0] + chip[1]]
            _remote(landed, landed, send_sems, recv_sems, 3 * t + j, (x, y, c)).wait_recv()
    for cp in _exchange_sends(ins, outs, send_sems, recv_sems):
        cp.wait_send()


def _share_half(rs):
    n = len(rs)

    def body(*refs):
        ins, outs, send_sems, recv_sems = refs[:n], refs[n:2 * n], refs[2 * n], refs[2 * n + 1]
        x, y, c, _ = _place()
        copies = [_remote(ins[t], outs[t], send_sems, recv_sems, t, (x, y, 1 - c)) for t in range(n)]
        for cp in copies:
            cp.start()
        for cp in copies:
            cp.wait()

    return pl.pallas_call(
        body, name="grad_share_half",
        out_shape=[jax.ShapeDtypeStruct(r.shape, r.dtype) for r in rs],
        in_specs=[ANY_SPEC] * n, out_specs=[ANY_SPEC] * n,
        scratch_shapes=[pltpu.SemaphoreType.DMA((n,)), pltpu.SemaphoreType.DMA((n,))],
    )(*rs)


ELEMENTWISE_BLOCK_BYTES = 1 << 20


def _row_tile(rows, cols):
    return _tile(rows, max(8, ELEMENTWISE_BLOCK_BYTES // (4 * cols) // 8 * 8))


def _add_my_half(g, sib, c_idx, name):
    rh, cols = g.shape[2:]
    tr = _row_tile(rh, cols)

    def body(c_ref, g_ref, s_ref, o_ref):
        o_ref[...] = (g_ref[...] + s_ref[...]).astype(BF16)

    return pl.pallas_call(
        body, name="grad_add_halves_" + name,
        grid_spec=pltpu.PrefetchScalarGridSpec(
            num_scalar_prefetch=1, grid=(N_CHIPS, rh // tr),
            in_specs=[pl.BlockSpec((None, None, tr, cols), lambda s, i, c: (s, c[0], i, 0)),
                      pl.BlockSpec((None, tr, cols), lambda s, i, c: (s, i, 0))],
            out_specs=pl.BlockSpec((None, tr, cols), lambda s, i, c: (s, i, 0))),
        out_shape=jax.ShapeDtypeStruct((N_CHIPS, rh, cols), BF16),
        compiler_params=_cp(2),
    )(c_idx, g, sib)


def _sum_chips(parts, pp, s_idx, name):
    rh, cols = parts.shape[1:]
    tr = _row_tile(rh, cols)

    def body(s_ref, p0, p1, p2, p3, mine_ref, o_ref):
        own = mine_ref[...]
        t = [jnp.where(s_ref[0] == k, own, p[...]).astype(F32) for k, p in enumerate((p0, p1, p2, p3))]
        o_ref[...] = ((t[0] + t[1]) + t[2]) + t[3]

    slot = lambda k: pl.BlockSpec((None, tr, cols), lambda i, s: (jnp.where(s[0] == k, (k + 1) % N_CHIPS, k), i, 0))
    return pl.pallas_call(
        body, name="grad_sum_chips_" + name,
        grid_spec=pltpu.PrefetchScalarGridSpec(
            num_scalar_prefetch=1, grid=(rh // tr,),
            in_specs=[slot(0), slot(1), slot(2), slot(3), pl.BlockSpec((None, tr, cols), lambda i, s: (s[0], i, 0))],
            out_specs=pl.BlockSpec((tr, cols), lambda i, s: (i, 0))),
        out_shape=jax.ShapeDtypeStruct((rh, cols), F32),
        compiler_params=_cp(1),
    )(s_idx, parts, parts, parts, parts, pp)


def _adamw(w, g_mine, g_sib, m, v, c_idx, name):
    rows, cols = w.shape
    tr = _row_tile(rows // 2, cols)
    nbh = rows // 2 // tr

    def body(c_ref, w_ref, gm_ref, gs_ref, m_ref, v_ref, g_ref, d_ref, mo_ref, vo_ref):
        g = jnp.where(pl.program_id(0) // nbh == c_ref[0], gm_ref[...], gs_ref[...])
        g_ref[...] = g
        d_ref[...], mo_ref[...], vo_ref[...] = _adamw_math(w_ref[...], g, m_ref[...], v_ref[...])

    spec = pl.BlockSpec((tr, cols), lambda i, c: (i, 0))
    half = pl.BlockSpec((tr, cols), lambda i, c: (i % nbh, 0))
    return pl.pallas_call(
        body, name="adamw_" + name,
        grid_spec=pltpu.PrefetchScalarGridSpec(
            num_scalar_prefetch=1, grid=(rows // tr,),
            in_specs=[spec, half, half, spec, spec], out_specs=[spec] * 4),
        out_shape=[jax.ShapeDtypeStruct(w.shape, F32)] * 4,
        compiler_params=_cp(1),
    )(c_idx, w, g_mine, g_sib, m, v)


def kernel(x, p, ffn1_w_in, ffn1_w_out, ln1_g, ln1_b, w_mix_in, b_forget, conv_w, g_attn, g_conv, w_mix_out, ln2_g, ln2_b, ffn2_w_in, ffn2_w_out, ln3_g, ln3_b, w_ple, w_ple_gate, b_ple_gate, ln4_g, ln4_b, loss_target, m_ffn1_w_in, m_ffn1_w_out, m_ln1_g, m_ln1_b, m_w_mix_in, m_b_forget, m_conv_w, m_g_attn, m_g_conv, m_w_mix_out, m_ln2_g, m_ln2_b, m_ffn2_w_in, m_ffn2_w_out, m_ln3_g, m_ln3_b, m_w_ple, m_w_ple_gate, m_b_ple_gate, m_ln4_g, m_ln4_b, v_ffn1_w_in, v_ffn1_w_out, v_ln1_g, v_ln1_b, v_w_mix_in, v_b_forget, v_conv_w, v_g_attn, v_g_conv, v_w_mix_out, v_ln2_g, v_ln2_b, v_ffn2_w_in, v_ffn2_w_out, v_ln3_g, v_ln3_b, v_w_ple, v_w_ple_gate, v_b_ple_gate, v_ln4_g, v_ln4_b):
    args = dict(locals())
    shard = {n: args[n][0] if LAYOUT[n][1] is not None else args[n] for n in WEIGHTS}
    m_shard = {n: args["m_" + n][0] if LAYOUT[n][1] is not None else args["m_" + n] for n in WEIGHTS}
    v_shard = {n: args["v_" + n][0] if LAYOUT[n][1] is not None else args["v_" + n] for n in WEIGHTS}
    c_idx = lax.axis_index("c").astype(jnp.int32).reshape(1)
    chip = (2 * lax.axis_index("x") + lax.axis_index("y")).astype(jnp.int32)

    conv_rows = SMALL_ROWS - shard["conv_w"].shape[0]
    mine = {n: _halves(shard[n].astype(BF16)) for n in BIG}
    mine["conv_w"] = _halves(jnp.pad(shard["conv_w"], ((0, conv_rows), (0, 0))))
    early_w = ["ffn1_w_in", "ffn1_w_out"]
    late_w = [n for n in BIG if n not in early_w] + ["conv_w"]

    def full_weights(names, gathered):
        out = {}
        for n, theirs in zip(names, gathered):
            g = lax.dynamic_update_slice(theirs, mine[n][None], (chip, 0, 0, 0))
            if n == "conv_w":
                out[n] = _join_chips(n, g.reshape(N_CHIPS, SMALL_ROWS, CONV_SHARD)[:, :3])
            else:
                out[n] = _join_chips(n, g.reshape((N_CHIPS,) + _shard_shape(n)))
        return out

    full = full_weights(early_w, _all_gather([mine[n] for n in early_w]))
    full.update({n: shard[n] for n in SMALL if n != "conv_w"})

    def chip_sums(names, grads):
        per_chip = [_halves(_pack_small_grads(grads)) if n == "small" else _halves(_split_chips(n, grads[n]))
                    for n in names]
        sibs = _swap_halves(per_chip, names[0])
        return [_add_my_half(g, sib, c_idx, n) for n, g, sib in zip(names, per_chip, sibs)]

    early_g = [n for n in BIG if n not in early_w]
    late_g = early_w + ["small"]
    loss_acc, grad_x, grads, early_sums, early_parts = _local_step(
        x[0], p[0, 0], loss_target[0], full,
        overlap={"gather": [mine[n] for n in late_w], "weights": lambda gathered: full_weights(late_w, gathered),
                 "chip_sums": lambda grads: chip_sums(early_g, grads)})
    loss = lax.psum(loss_acc[0, 0], ("x", "y", "c"))

    late_sums = chip_sums(late_g, grads)
    names = early_g + late_g
    sums = list(early_sums) + late_sums
    parts = list(early_parts) + list(_exchange_chips(late_sums))
    half_of = {n: _sum_chips(pt, own, chip.reshape(1), n) for n, pt, own in zip(names, parts, sums)}
    names = BIG + ["small"]
    my_half = [half_of[n] for n in names]
    sib_half = _share_half(my_half)

    out = {}
    for n, gm, gs in zip(BIG, my_half, sib_half):
        out[n] = [a[None] for a in _adamw(shard[n], gm, gs, m_shard[n], v_shard[n], c_idx, n)]
    small = _adamw_small(my_half[-1], sib_half[-1], c_idx, [shard[n] for n in SMALL], [m_shard[n] for n in SMALL],
                         [v_shard[n] for n in SMALL])
    for i, n in enumerate(SMALL):
        out[n] = [small[q][i][None] if n == "conv_w" else small[q][i] for q in range(4)]
    return (loss, grad_x[None], *[out[n][q] for q in range(4) for n in WEIGHTS])
```

```python
import functools
import math

import jax
import jax.numpy as jnp
from jax import lax
from jax.experimental import pallas as pl
from jax.experimental.pallas import tpu as pltpu

F32 = jnp.float32
BF16 = jnp.bfloat16

D_MODEL = 1024
D_FF = 2816
N_HEADS = 8
HEAD_DIM = 64
D_ATTN = N_HEADS * HEAD_DIM
D_CONV = 512
PLE_DIM = 256
N_FLOG = 128
ALPHA = 2.0 ** 0.25
LN_EPS = 1e-5
RMS_EPS = 1e-6
NEG_INF = -1e30
Q_SCALE = 1.0 / math.sqrt(HEAD_DIM)
LOG2E = math.log2(math.e)

ADAM_LR = 0.001
ADAM_B1 = 0.9
ADAM_B2 = 0.999
ADAM_EPS = 1e-08
ADAM_WD = 0.01
ADAM_STEP = 10

V7X_VMEM_BYTES = 64 << 20
VMEM_LIMIT = V7X_VMEM_BYTES - (8 << 20)
LANE = 128
FF_CHUNK = 256
N_CHIPS = 4
MESH = pl.DeviceIdType.MESH


def _cp(n_axes):
    return pltpu.CompilerParams(dimension_semantics=("arbitrary",) * n_axes, vmem_limit_bytes=VMEM_LIMIT)


def _resident(shape):
    n = len(shape)
    return pl.BlockSpec(shape, lambda *_: (0,) * n, pipeline_mode=pl.Buffered(1))


def _nn(a, b):
    return jnp.dot(a, b, preferred_element_type=F32)


def _nt(a, b):
    return lax.dot_general(a, b, (((1,), (1,)), ((), ())), preferred_element_type=F32)


def _tn(a, b):
    return lax.dot_general(a, b, (((0,), (0,)), ((), ())), preferred_element_type=F32)


def _ln_stats(r):
    mu = jnp.mean(r, axis=-1, keepdims=True)
    xc = r - mu
    var = jnp.mean(xc * xc, axis=-1, keepdims=True)
    rstd = lax.rsqrt(var + LN_EPS)
    return xc * rstd, rstd


def _ln_bwd(dy, xhat, rstd, g):
    dxh = dy * g
    m1 = jnp.mean(dxh, axis=-1, keepdims=True)
    m2 = jnp.mean(dxh * xhat, axis=-1, keepdims=True)
    return rstd * (dxh - m1 - xhat * m2)


def _sigmoid(z):
    return 1.0 / (1.0 + jnp.exp(-z))


def _rowsum(a):
    return jnp.sum(a, axis=0, keepdims=True)


def _tile(total, want):
    if total <= want:
        return total
    for t in range(want - want % 8, 0, -8):
        if total % t == 0:
            return t
    raise ValueError((total, want))


def _ffn_fwd(x, w_in, w_out, lg, lb, name, gather=()):
    T = x.shape[0]
    tm = _tile(T, 512)
    nf = D_FF // FF_CHUNK
    ng = len(gather)
    last = T // tm - 1

    def body(x_ref, wi_ref, wo_ref, lg_ref, lb_ref, *rest):
        comm_in, (xo_ref, r_ref, g_ref, u_ref, h_ref) = rest[:ng], rest[ng:ng + 5]
        comm_out, sems = rest[ng + 5:2 * ng + 5], rest[2 * ng + 5:]
        if ng:
            @pl.when(pl.program_id(0) == 0)
            def _():
                _gather_start(comm_in, comm_out, *sems)

        xf = x_ref[...]
        xb = xf.astype(BF16)
        acc = jnp.zeros((tm, D_MODEL), F32)
        for j in range(nf):
            c0 = j * FF_CHUNK
            g = _nn(xb, wi_ref[:, c0:c0 + FF_CHUNK])
            u = _nn(xb, wi_ref[:, D_FF + c0:D_FF + c0 + FF_CHUNK])
            hb = (g * _sigmoid(g) * u).astype(BF16)
            g_ref[:, c0:c0 + FF_CHUNK] = g.astype(BF16)
            u_ref[:, c0:c0 + FF_CHUNK] = u.astype(BF16)
            h_ref[:, c0:c0 + FF_CHUNK] = hb
            acc = acc + _nn(hb, wo_ref[c0:c0 + FF_CHUNK, :])
        r = ALPHA * xf + 0.5 * acc
        r_ref[...] = r
        xhat, _ = _ln_stats(r)
        xo_ref[...] = xhat * lg_ref[...] + lb_ref[...]
        if ng:
            @pl.when(pl.program_id(0) == last)
            def _():
                _gather_finish(comm_in, comm_out, *sems)

    row = lambda n: pl.BlockSpec((tm, n), lambda i: (i, 0))
    return pl.pallas_call(
        body, name=name, grid=(T // tm,),
        in_specs=[row(D_MODEL), _resident((D_MODEL, 2 * D_FF)), _resident((D_FF, D_MODEL)),
                  _resident((1, D_MODEL)), _resident((1, D_MODEL))] + [ANY_SPEC] * ng,
        out_specs=[row(D_MODEL), row(D_MODEL), row(D_FF), row(D_FF), row(D_FF)] + [ANY_SPEC] * ng,
        out_shape=[jax.ShapeDtypeStruct((T, D_MODEL), F32), jax.ShapeDtypeStruct((T, D_MODEL), F32),
                   jax.ShapeDtypeStruct((T, D_FF), BF16), jax.ShapeDtypeStruct((T, D_FF), BF16),
                   jax.ShapeDtypeStruct((T, D_FF), BF16)] + _gather_shapes(gather),
        scratch_shapes=_gather_sems(ng) if ng else [],
        compiler_params=_cp(1),
    )(x, w_in, w_out, lg, lb, *gather)


def _ffn_bwd(dxo, r, g, u, w_in, w_out, lg, name, exchange=()):
    T = r.shape[0]
    tm = _tile(T, 256)
    nf = D_FF // FF_CHUNK
    ne = len(exchange)
    last = T // tm - 1

    def body(dxo_ref, r_ref, g_ref, u_ref, wi_ref, wo_ref, lg_ref, *rest):
        comm_in, (dx_ref, dgu_ref, df_ref, dlg_ref, dlb_ref) = rest[:ne], rest[ne:ne + 5]
        comm_out, sems = rest[ne + 5:2 * ne + 5], rest[2 * ne + 5:]
        i = pl.program_id(0)
        if ne:
            @pl.when(i == 0)
            def _():
                _exchange_start(comm_in, comm_out, *sems)

        dy = dxo_ref[...]
        xhat, rstd = _ln_stats(r_ref[...])
        dr = _ln_bwd(dy, xhat, rstd, lg_ref[...])

        @pl.when(i == 0)
        def _():
            dlg_ref[...] = jnp.zeros_like(dlg_ref)
            dlb_ref[...] = jnp.zeros_like(dlb_ref)

        dlg_ref[...] += _rowsum(dy * xhat)
        dlb_ref[...] += _rowsum(dy)
        dfb = (0.5 * dr).astype(BF16)
        df_ref[...] = dfb
        acc = jnp.zeros((tm, D_MODEL), F32)
        dh_ahead = _nt(dfb, wo_ref[0:FF_CHUNK, :])
        for j in range(nf):
            c0 = j * FF_CHUNK
            dh = dh_ahead
            if j + 1 < nf:
                dh_ahead = _nt(dfb, wo_ref[c0 + FF_CHUNK:c0 + 2 * FF_CHUNK, :])
            gg = g_ref[:, c0:c0 + FF_CHUNK].astype(F32)
            uu = u_ref[:, c0:c0 + FF_CHUNK].astype(F32)
            s = _sigmoid(gg)
            dgb = (dh * uu * s * (1.0 + gg * (1.0 - s))).astype(BF16)
            dub = (dh * gg * s).astype(BF16)
            dgu_ref[:, c0:c0 + FF_CHUNK] = dgb
            dgu_ref[:, D_FF + c0:D_FF + c0 + FF_CHUNK] = dub
            acc = acc + _nt(dgb, wi_ref[:, c0:c0 + FF_CHUNK]) + _nt(dub, wi_ref[:, D_FF + c0:D_FF + c0 + FF_CHUNK])
        dx_ref[...] = ALPHA * dr + acc
        if ne:
            @pl.when(i == last)
            def _():
                _exchange_finish(comm_in, comm_out, *sems)

    row = lambda n: pl.BlockSpec((tm, n), lambda i: (i, 0))
    return pl.pallas_call(
        body, name=name, grid=(T // tm,),
        in_specs=[row(D_MODEL), row(D_MODEL), row(D_FF), row(D_FF), _resident((D_MODEL, 2 * D_FF)),
                  _resident((D_FF, D_MODEL)), _resident((1, D_MODEL))] + [ANY_SPEC] * ne,
        out_specs=[row(D_MODEL), row(2 * D_FF), row(D_MODEL), _resident((1, D_MODEL)), _resident((1, D_MODEL))]
        + [ANY_SPEC] * ne,
        out_shape=[jax.ShapeDtypeStruct((T, D_MODEL), F32), jax.ShapeDtypeStruct((T, 2 * D_FF), BF16),
                   jax.ShapeDtypeStruct((T, D_MODEL), BF16), jax.ShapeDtypeStruct((1, D_MODEL), F32),
                   jax.ShapeDtypeStruct((1, D_MODEL), F32)] + _exchange_shapes(exchange),
        scratch_shapes=_exchange_sems(ne) if ne else [],
        compiler_params=_cp(1),
    )(dxo, r, g, u, w_in, w_out, lg, *exchange)


def _matmul_tn(a, b, name, tn=None, exchange=()):
    T, K = a.shape
    N = b.shape[1]
    tt = _tile(T, 1024)
    if tn is None:
        tn = N
        while K * tn * 4 > (6 << 20) and tn % 256 == 0:
            tn //= 2
    assert N % tn == 0
    ne = len(exchange)
    grid = (N // tn, T // tt)

    def body(a_ref, b_ref, *rest):
        comm_in, o_ref, comm_out, sems = rest[:ne], rest[ne], rest[ne + 1:2 * ne + 1], rest[2 * ne + 1:]
        n, t = pl.program_id(0), pl.program_id(1)
        if ne:
            @pl.when((n == 0) & (t == 0))
            def _():
                _exchange_start(comm_in, comm_out, *sems)

        @pl.when(t == 0)
        def _():
            o_ref[...] = jnp.zeros_like(o_ref)

        o_ref[...] += _tn(a_ref[...].astype(BF16), b_ref[...].astype(BF16))
        if ne:
            @pl.when((n == grid[0] - 1) & (t == grid[1] - 1))
            def _():
                _exchange_finish(comm_in, comm_out, *sems)

    res = pl.pallas_call(
        body, name=name, grid=grid,
        in_specs=[pl.BlockSpec((tt, K), lambda n, t: (t, 0)), pl.BlockSpec((tt, tn), lambda n, t: (t, n))]
        + [ANY_SPEC] * ne,
        out_specs=[pl.BlockSpec((K, tn), lambda n, t: (0, n))] + [ANY_SPEC] * ne,
        out_shape=[jax.ShapeDtypeStruct((K, N), F32)] + _exchange_shapes(exchange),
        scratch_shapes=_exchange_sems(ne) if ne else [],
        compiler_params=_cp(2),
    )(a, b, *exchange)
    return res if ne else res[0]


def _matmul_tokens(at, b, name):
    M, T = at.shape
    N = b.shape[1]
    tt = _tile(T, 1024)

    def body(a_ref, b_ref, o_ref):
        @pl.when(pl.program_id(0) == 0)
        def _():
            o_ref[...] = jnp.zeros_like(o_ref)

        o_ref[...] += _nn(a_ref[...].astype(BF16), b_ref[...].astype(BF16))

    return pl.pallas_call(
        body, name=name, grid=(T // tt,),
        in_specs=[pl.BlockSpec((M, tt), lambda t: (0, t)), pl.BlockSpec((tt, N), lambda t: (t, 0))],
        out_specs=pl.BlockSpec((M, N), lambda t: (0, 0)),
        out_shape=jax.ShapeDtypeStruct((M, N), F32),
        compiler_params=_cp(1),
    )(at, b)


def _matmul_nn(x, w, scale, out_dtype, name):
    T, K = x.shape
    N = w.shape[1]
    tm = _tile(T, 512)

    def body(x_ref, w_ref, s_ref, o_ref):
        o_ref[...] = (_nn(x_ref[...].astype(BF16), w_ref[...]) * s_ref[...]).astype(out_dtype)

    return pl.pallas_call(
        body, name=name, grid=(T // tm,),
        in_specs=[pl.BlockSpec((tm, K), lambda i: (i, 0)), _resident((K, N)), _resident((1, N))],
        out_specs=pl.BlockSpec((tm, N), lambda i: (i, 0)),
        out_shape=jax.ShapeDtypeStruct((T, N), out_dtype),
        compiler_params=_cp(1),
    )(x, w, scale)


def _log_sigmoid(z):
    return jnp.minimum(z, 0.0) - jnp.log1p(jnp.exp(-jnp.abs(z)))


def _tri(n, lower):
    r = lax.broadcasted_iota(jnp.int32, (n, n), 0)
    c = lax.broadcasted_iota(jnp.int32, (n, n), 1)
    return jnp.where((c <= r) if lower else (c >= r), 1.0, 0.0).astype(F32)


def _f32dot(a, b):
    return jnp.dot(a, b, preferred_element_type=F32, precision=lax.Precision.HIGHEST)


def _forget_cumsum(flog, col, bf):
    T = flog.shape[0]
    bt = _tile(T, 512)

    def body(f_ref, b_ref, c_ref, carry):
        @pl.when(pl.program_id(0) == 0)
        def _():
            carry[...] = jnp.zeros_like(carry)

        lf = _log_sigmoid(f_ref[...] + b_ref[...])
        c = _f32dot(_tri(bt, True), lf) + carry[...]
        c_ref[...] = c * LOG2E
        carry[...] = c[bt - 1:bt, :]

    return pl.pallas_call(
        body, name="forget_cumsum", grid=(T // bt,),
        in_specs=[pl.BlockSpec((bt, N_FLOG), lambda i: (i, col)), _resident((1, N_FLOG))],
        out_specs=pl.BlockSpec((bt, N_FLOG), lambda i: (i, 0)),
        out_shape=jax.ShapeDtypeStruct((T, N_FLOG), F32),
        scratch_shapes=[pltpu.VMEM((1, N_FLOG), F32)],
        compiler_params=_cp(1),
    )(flog, bf)


def _forget_bwd(dck, dcq, flog, col, bf):
    T = dcq.shape[0]
    bt = _tile(T, 512)
    nb = T // bt

    def body(k0_ref, k1_ref, k2_ref, k3_ref, dcq_ref, f_ref, b_ref, dz_ref, db_ref, carry):
        @pl.when(pl.program_id(0) == 0)
        def _():
            carry[...] = jnp.zeros_like(carry)
            db_ref[...] = jnp.zeros_like(db_ref)

        dc = ((k0_ref[...] + k1_ref[...]) + (k2_ref[...] + k3_ref[...])) + dcq_ref[...]
        dlf = _f32dot(_tri(bt, False), dc) + carry[...]
        carry[...] = dlf[0:1, :]
        z = f_ref[...] + b_ref[...]
        dz = dlf * _sigmoid(-z)
        dz_ref[...] = dz.astype(BF16)
        db_ref[...] += _rowsum(dz)

    slab = lambda j: pl.BlockSpec((None, bt, N_FLOG), lambda i: (j, nb - 1 - i, 0))
    return pl.pallas_call(
        body, name="forget_bwd", grid=(nb,),
        in_specs=[slab(0), slab(1), slab(2), slab(3),
                  pl.BlockSpec((bt, N_FLOG), lambda i: (nb - 1 - i, 0)),
                  pl.BlockSpec((bt, N_FLOG), lambda i: (nb - 1 - i, col)), _resident((1, N_FLOG))],
        out_specs=[pl.BlockSpec((bt, N_FLOG), lambda i: (nb - 1 - i, 0)), _resident((1, N_FLOG))],
        out_shape=[jax.ShapeDtypeStruct((T, N_FLOG), BF16), jax.ShapeDtypeStruct((1, N_FLOG), F32)],
        scratch_shapes=[pltpu.VMEM((1, N_FLOG), F32)],
        compiler_params=_cp(1),
    )(dck, dck, dck, dck, dcq, flog, bf)


def _head_masks():
    lane = lax.broadcasted_iota(jnp.int32, (1, LANE), 1)
    return lane < HEAD_DIM


def _split_heads(x2, is_a):
    zero = jnp.zeros_like(x2)
    return jnp.where(is_a, x2, zero), jnp.where(is_a, zero, x2)


BIAS_PARTS = 3


def _bias_lanes(h):
    lane = lax.broadcasted_iota(jnp.int32, (1, LANE), 1)
    first = (1 - h) * HEAD_DIM
    return lane, first


def _fold_key_bias(qkv, c):
    T = qkv.shape[0]
    tm = _tile(T, 512)
    npair = N_HEADS // 2

    def body(k_ref, c_ref, o_ref):
        cc = c_ref[...]
        parts, rest = [], cc
        for _ in range(BIAS_PARTS):
            piece = rest.astype(BF16)
            parts.append(piece)
            rest = rest - piece.astype(F32)
        for j in range(npair):
            k2 = k_ref[:, j * LANE:(j + 1) * LANE]
            for h in range(2):
                lane, first = _bias_lanes(h)
                out = k2
                for n, piece in enumerate(parts):
                    col = piece[:, 2 * j + h:2 * j + h + 1]
                    out = jnp.where(lane == first + n, col, out)
                o_ref[:, (2 * j + h) * LANE:(2 * j + h + 1) * LANE] = out

    return pl.pallas_call(
        body, name="fold_key_bias", grid=(T // tm,),
        in_specs=[pl.BlockSpec((tm, D_ATTN), lambda i: (i, 1)), pl.BlockSpec((tm, N_FLOG), lambda i: (i, 0))],
        out_specs=pl.BlockSpec((tm, 2 * D_ATTN), lambda i: (i, 0)),
        out_shape=jax.ShapeDtypeStruct((T, 2 * D_ATTN), BF16),
        compiler_params=_cp(1),
    )(qkv, c)


def _attn_fwd(qkv, vt, kb):
    T = qkv.shape[0]
    tq = _tile(T, 512)
    tk = tq
    nq = T // tq
    npair = N_HEADS // 2

    def body(q_ref, ka_ref, kb_ref, vt_ref, o_ref, al_ref, m_s, l_s, acc_s):
        i = pl.program_id(1)
        qs = []
        for h, qh in enumerate(_split_heads(q_ref[...], _head_masks())):
            lane, first = _bias_lanes(h)
            qs.append(jnp.where((lane >= first) & (lane < first + BIAS_PARTS), -1.0, qh).astype(BF16))
        k_refs = (ka_ref, kb_ref)
        m_s[...] = jnp.full_like(m_s, NEG_INF)
        l_s[...] = jnp.zeros_like(l_s)
        acc_s[...] = jnp.zeros_like(acc_s)

        def scores_at(kk):
            k0 = pl.multiple_of(kk * tk, tk)
            return tuple(_nt(k_refs[h][pl.ds(k0, tk), :], qs[h]) for h in range(2))

        def consume(kk, scores, masked):
            k0 = pl.multiple_of(kk * tk, tk)
            v2t = vt_ref[:, pl.ds(k0, tk)]
            for h in range(2):
                zt = scores[h]
                if masked:
                    rr = lax.broadcasted_iota(jnp.int32, (tk, tq), 0)
                    cc = lax.broadcasted_iota(jnp.int32, (tk, tq), 1)
                    zt = jnp.where(cc >= rr, zt, NEG_INF)
                m_old = m_s[h]
                m_new = jnp.maximum(m_old, jnp.max(zt, axis=0, keepdims=True))
                p = jnp.exp2(zt - m_new)
                a = jnp.exp2(m_old - m_new)
                l_s[h] = a * l_s[h] + jnp.sum(p, axis=0, keepdims=True)
                acc_s[h] = a * acc_s[h] + _nn(v2t, p.astype(BF16))
                m_s[h] = m_new

        def two_tiles(kk, second_masked):
            first, second = scores_at(kk), scores_at(kk + 1)
            consume(kk, first, False)
            consume(kk + 1, second, second_masked)

        def loop_body(t, carry):
            two_tiles(2 * t, False)
            return carry

        lax.fori_loop(0, i // 2, loop_body, 0)

        @pl.when(i % 2 == 1)
        def _():
            two_tiles(i - 1, True)

        @pl.when(i % 2 == 0)
        def _():
            consume(i, scores_at(i), True)

        outs = []
        for h in range(2):
            l = l_s[h]
            outs.append(acc_s[h] * (1.0 / l))
            al_ref[0, h:h + 1, :] = -(m_s[h] + jnp.log2(l))
        al_ref[0, 2:8, :] = jnp.zeros((6, tq), F32)
        dim = lax.broadcasted_iota(jnp.int32, (LANE, 1), 0)
        o_ref[...] = jnp.where(dim < HEAD_DIM, outs[0], outs[1]).T

    rowl = pl.BlockSpec((1, 8, tq), lambda j, i: (j, 0, i))
    return pl.pallas_call(
        body, name="attn_fwd", grid=(npair, nq),
        in_specs=[pl.BlockSpec((tq, LANE), lambda j, i: (i, j)),
                  pl.BlockSpec((T, LANE), lambda j, i: (0, 2 * j), pipeline_mode=pl.Buffered(1)),
                  pl.BlockSpec((T, LANE), lambda j, i: (0, 2 * j + 1), pipeline_mode=pl.Buffered(1)),
                  pl.BlockSpec((LANE, T), lambda j, i: (2 * npair + j, 0), pipeline_mode=pl.Buffered(1))],
        out_specs=[pl.BlockSpec((tq, LANE), lambda j, i: (i, j)), rowl],
        out_shape=[jax.ShapeDtypeStruct((T, D_ATTN), F32), jax.ShapeDtypeStruct((npair, 8, T), F32)],
        scratch_shapes=[pltpu.VMEM((2, 1, tq), F32), pltpu.VMEM((2, 1, tq), F32), pltpu.VMEM((2, LANE, tq), F32)],
        compiler_params=_cp(2),
    )(qkv, kb, kb, vt)


def _attn_bwd(qkv, qkvt, dob, dobt, cb, alrow, dlrow):
    T = qkv.shape[0]
    tq = _tile(T, 512)
    tk = tq
    nq = T // tq
    npair = N_HEADS // 2

    def body(q_ref, qt_ref, k_ref, kt_ref, v_ref, do_ref, dot_ref, cb_ref, al_ref, dl_ref,
             dq_ref, dk_ref, dv_ref, dc_ref, dcq_ref, dk_s, dv_s, dc_s):
        kj = pl.program_id(1)
        is_a = _head_masks()
        ks = _split_heads(k_ref[...], is_a)
        vs = _split_heads(v_ref[...], is_a)
        dim_a = lax.broadcasted_iota(jnp.int32, (LANE, 1), 0) < HEAD_DIM
        kts = _split_heads(kt_ref[...], dim_a)
        cs = (cb_ref[:, 0:1], cb_ref[:, HEAD_DIM:HEAD_DIM + 1])

        @pl.when(kj == 0)
        def _():
            dq_ref[...] = jnp.zeros_like(dq_ref)
            dcq_ref[...] = jnp.zeros_like(dcq_ref)

        dk_s[...] = jnp.zeros_like(dk_s)
        dv_s[...] = jnp.zeros_like(dv_s)
        dc_s[...] = jnp.zeros_like(dc_s)

        def step(qi, masked):
            q0 = pl.multiple_of(qi * tq, tq)
            q2 = q_ref[pl.ds(q0, tq), :]
            do2 = do_ref[pl.ds(q0, tq), :]
            qt2 = qt_ref[:, pl.ds(q0, tq)]
            dot2 = dot_ref[:, pl.ds(q0, tq)]
            for h in range(2):
                alr = al_ref[0, h:h + 1, pl.ds(q0, tq)]
                dlr = dl_ref[0, h:h + 1, pl.ds(q0, tq)]
                zt = _nt(ks[h], q2) + (alr - cs[h])
                if masked:
                    rr = lax.broadcasted_iota(jnp.int32, (tk, tq), 0)
                    cc = lax.broadcasted_iota(jnp.int32, (tk, tq), 1)
                    zt = jnp.where(cc >= rr, zt, NEG_INF)
                pt = jnp.exp2(zt)
                dst = pt * (_nt(vs[h], do2) - dlr)
                pb = pt.astype(BF16)
                dsb = dst.astype(BF16)
                dv_s[h] += _nt(dot2, pb)
                dk_s[h] += _nt(qt2, dsb)
                dc_s[h] += jnp.sum(dst, axis=-1, keepdims=True)
                dcq_ref[0, h:h + 1, pl.ds(q0, tq)] += jnp.sum(dst, axis=0, keepdims=True)
                dq_ref[:, pl.ds(q0, tq)] += _nn(kts[h], dsb)

        step(kj, True)

        def loop_body(qi, carry):
            step(qi, False)
            return carry

        lax.fori_loop(kj + 1, nq, loop_body, 0)
        dk_ref[...] = (jnp.where(dim_a, dk_s[0], dk_s[1]) * (1.0 / LOG2E)).astype(BF16)
        dv_ref[...] = jnp.where(dim_a, dv_s[0], dv_s[1]).astype(BF16)
        lane = lax.broadcasted_iota(jnp.int32, (1, LANE), 1)
        head = 2 * pl.program_id(0)
        dc_ref[...] = jnp.where(lane == head, -dc_s[0], jnp.where(lane == head + 1, -dc_s[1], 0.0))

        @pl.when(kj == nq - 1)
        def _():
            dq_ref[...] = dq_ref[...] * Q_SCALE

    full = lambda col: pl.BlockSpec((T, LANE), lambda j, kj: (0, col(j)), pipeline_mode=pl.Buffered(1))
    fullt = lambda row: pl.BlockSpec((LANE, T), lambda j, kj: (row(j), 0), pipeline_mode=pl.Buffered(1))
    tile = lambda col: pl.BlockSpec((tk, LANE), lambda j, kj: (kj, col(j)))
    tilet = lambda row: pl.BlockSpec((LANE, tk), lambda j, kj: (row(j), kj))
    rowl = pl.BlockSpec((1, 8, T), lambda j, kj: (j, 0, 0))
    return pl.pallas_call(
        body, name="attn_bwd", grid=(npair, nq),
        in_specs=[full(lambda j: j), fullt(lambda j: j), tile(lambda j: npair + j), tilet(lambda j: npair + j),
                  tile(lambda j: 2 * npair + j), full(lambda j: j), fullt(lambda j: j), tile(lambda j: j), rowl, rowl],
        out_specs=[pl.BlockSpec((LANE, T), lambda j, kj: (j, 0)), tilet(lambda j: j), tilet(lambda j: j),
                   pl.BlockSpec((None, tk, LANE), lambda j, kj: (j, kj, 0)), rowl],
        out_shape=[jax.ShapeDtypeStruct((D_ATTN, T), F32), jax.ShapeDtypeStruct((D_ATTN, T), BF16),
                   jax.ShapeDtypeStruct((D_ATTN, T), BF16), jax.ShapeDtypeStruct((npair, T, LANE), F32),
                   jax.ShapeDtypeStruct((npair, 8, T), F32)],
        scratch_shapes=[pltpu.VMEM((2, LANE, tk), F32), pltpu.VMEM((2, LANE, tk), F32), pltpu.VMEM((2, tk, 1), F32)],
        compiler_params=_cp(2),
    )(qkv, qkvt, qkv, qkvt, qkv, dob, dobt, cb, alrow, dlrow)


HALO = 8


def _shift_rows(cur, other, k, tm, down):
    row = lax.broadcasted_iota(jnp.int32, (tm, 1), 0)
    reps = tm // HALO
    if down:
        rolled = pltpu.roll(cur, k, 0)
        fill = jnp.tile(pltpu.roll(other, k, 0), (reps, 1))
        return jnp.where(row < k, fill, rolled)
    rolled = pltpu.roll(cur, tm - k, 0)
    fill = jnp.tile(pltpu.roll(other, HALO - k, 0), (reps, 1))
    return jnp.where(row >= tm - k, fill, rolled)


def _conv_fwd(c, hh, c_prev, hh_prev, w_ref, first, tm):
    u = c * hh
    u_prev = jnp.where(first, 0.0, c_prev * hh_prev)
    u1 = _shift_rows(u, u_prev, 1, tm, True)
    u2 = _shift_rows(u, u_prev, 2, tm, True)
    y = w_ref[0:1, :] * u2 + w_ref[1:2, :] * u1 + w_ref[2:3, :] * u
    return u, u1, u2, y


def _rms(x, g):
    rs = lax.rsqrt(jnp.mean(x * x, axis=-1, keepdims=True) + RMS_EPS)
    return x * rs * g, rs


def _mixer_tail_fwd(o, bchf, conv_w, g_attn, g_conv, w_mo, x1, lg, lb):
    T = o.shape[0]
    tm = _tile(T, 512)
    hb = tm // HALO

    def body(o_ref, b_ref, c_ref, h_ref, cp_ref, hp_ref, w_ref, ga_ref, gc_ref, wmo_ref, x1_ref, lg_ref, lb_ref,
             x2_ref, r2_ref, mg_ref):
        first = pl.program_id(0) == 0
        _, _, _, y = _conv_fwd(c_ref[...], h_ref[...], cp_ref[...], hp_ref[...], w_ref, first, tm)
        na, _ = _rms(o_ref[...], ga_ref[...])
        nc, _ = _rms(b_ref[...] * y, gc_ref[...])
        nab = na.astype(BF16)
        ncb = nc.astype(BF16)
        mg_ref[:, 0:D_ATTN] = nab
        mg_ref[:, D_ATTN:] = ncb
        r2 = ALPHA * x1_ref[...] + _nn(nab, wmo_ref[0:D_ATTN, :]) + _nn(ncb, wmo_ref[D_ATTN:, :])
        r2_ref[...] = r2
        xhat, _ = _ln_stats(r2)
        x2_ref[...] = xhat * lg_ref[...] + lb_ref[...]

    row = lambda n, col=0: pl.BlockSpec((tm, n), lambda i: (i, col))
    prev = lambda col: pl.BlockSpec((HALO, D_CONV), lambda i: (jnp.maximum(i * hb - 1, 0), col))
    return pl.pallas_call(
        body, name="mixer_tail_fwd", grid=(T // tm,),
        in_specs=[row(D_ATTN), row(D_CONV, 0), row(D_CONV, 1), row(D_CONV, 2), prev(1), prev(2),
                  _resident((3, D_CONV)), _resident((1, D_ATTN)), _resident((1, D_CONV)),
                  _resident((D_MODEL, D_MODEL)), row(D_MODEL), _resident((1, D_MODEL)), _resident((1, D_MODEL))],
        out_specs=[row(D_MODEL), row(D_MODEL), row(D_MODEL)],
        out_shape=[jax.ShapeDtypeStruct((T, D_MODEL), F32), jax.ShapeDtypeStruct((T, D_MODEL), F32),
                   jax.ShapeDtypeStruct((T, D_MODEL), BF16)],
        compiler_params=_cp(1),
    )(o, bchf, bchf, bchf, bchf, bchf, conv_w, g_attn, g_conv, w_mo, x1, lg, lb)


def _head_sum_rows():
    row = lax.broadcasted_iota(jnp.int32, (4 * 8, D_ATTN), 0)
    head = lax.broadcasted_iota(jnp.int32, (4 * 8, D_ATTN), 1) // HEAD_DIM
    return jnp.where((row % 8 < 2) & (2 * (row // 8) + row % 8 == head), 1.0, 0.0).astype(F32)


def _mixer_tail_bwd(dx2, r2, lg, w_mo, o, bchf, conv_w, g_attn, g_conv, swap=()):
    T = o.shape[0]
    tm = _tile(T, 256)
    hb = tm // HALO
    ns = len(swap)
    last = T // tm - 1

    def body(dx2_ref, r2_ref, lg_ref, wmo_ref, o_ref, b_ref, c_ref, h_ref, cp_ref, hp_ref, w_ref, ga_ref, gc_ref,
             *rest):
        comm_in = rest[:ns]
        dx1_ref, dr_ref, do_ref, dot_ref, dl_ref, dco_ref, dlg_ref, dlb_ref, dga_ref, dgc_ref = rest[ns:ns + 10]
        comm_out, sems = rest[ns + 10:2 * ns + 10], rest[2 * ns + 10:]
        i = pl.program_id(0)
        if ns:
            @pl.when(i == 0)
            def _():
                _swap_start(comm_in, comm_out, *sems)

            @pl.when(i == last)
            def _():
                _swap_finish(comm_in, comm_out, *sems)

        @pl.when(i == 0)
        def _():
            for ref in (dlg_ref, dlb_ref, dga_ref, dgc_ref):
                ref[...] = jnp.zeros_like(ref)

        dy = dx2_ref[...]
        xhat, rstd = _ln_stats(r2_ref[...])
        dr = _ln_bwd(dy, xhat, rstd, lg_ref[...])
        dlg_ref[...] += _rowsum(dy * xhat)
        dlb_ref[...] += _rowsum(dy)
        dx1_ref[...] = ALPHA * dr
        drb = dr.astype(BF16)
        dr_ref[...] = drb
        dna = _nt(drb, wmo_ref[0:D_ATTN, :])
        dnc = _nt(drb, wmo_ref[D_ATTN:, :])

        def rms_bwd(x, g, dn):
            rs = lax.rsqrt(jnp.mean(x * x, axis=-1, keepdims=True) + RMS_EPS)
            dng = dn * g
            dx = rs * dng - x * (rs * rs * rs) * jnp.mean(dng * x, axis=-1, keepdims=True)
            return dx, _rowsum(dn * x * rs)

        oo = o_ref[...]
        do, dga = rms_bwd(oo, ga_ref[...], dna)
        dga_ref[...] += dga
        do_ref[...] = do.astype(BF16)
        dot_ref[...] = do.T.astype(BF16)
        dl_ref[...] = lax.dot_general(_head_sum_rows(), do * oo, (((1,), (1,)), ((), ())),
                                      preferred_element_type=F32, precision=lax.Precision.HIGHEST)
        _, _, _, y = _conv_fwd(c_ref[...], h_ref[...], cp_ref[...], hp_ref[...], w_ref, i == 0, tm)
        dco, dgc = rms_bwd(b_ref[...] * y, gc_ref[...], dnc)
        dgc_ref[...] += dgc
        dco_ref[...] = dco

    row = lambda n, col=0: pl.BlockSpec((tm, n), lambda i: (i, col))
    prev = lambda col: pl.BlockSpec((HALO, D_CONV), lambda i: (jnp.maximum(i * hb - 1, 0), col))
    vec = lambda n: _resident((1, n))
    return pl.pallas_call(
        body, name="mixer_tail_bwd", grid=(T // tm,),
        in_specs=[row(D_MODEL), row(D_MODEL), vec(D_MODEL), _resident((D_MODEL, D_MODEL)), row(D_ATTN),
                  row(D_CONV, 0), row(D_CONV, 1), row(D_CONV, 2), prev(1), prev(2), _resident((3, D_CONV)),
                  vec(D_ATTN), vec(D_CONV)] + [ANY_SPEC] * ns,
        out_specs=[row(D_MODEL), row(D_MODEL), row(D_ATTN), pl.BlockSpec((D_ATTN, tm), lambda i: (0, i)),
                   pl.BlockSpec((4 * 8, tm), lambda i: (0, i)), row(D_CONV),
                   vec(D_MODEL), vec(D_MODEL), vec(D_ATTN), vec(D_CONV)] + [ANY_SPEC] * ns,
        out_shape=[jax.ShapeDtypeStruct((T, D_MODEL), F32), jax.ShapeDtypeStruct((T, D_MODEL), BF16),
                   jax.ShapeDtypeStruct((T, D_ATTN), BF16), jax.ShapeDtypeStruct((D_ATTN, T), BF16),
                   jax.ShapeDtypeStruct((4 * 8, T), F32),
                   jax.ShapeDtypeStruct((T, D_CONV), F32), jax.ShapeDtypeStruct((1, D_MODEL), F32),
                   jax.ShapeDtypeStruct((1, D_MODEL), F32), jax.ShapeDtypeStruct((1, D_ATTN), F32),
                   jax.ShapeDtypeStruct((1, D_CONV), F32)] + _swap_shapes(swap),
        scratch_shapes=_swap_sems(ns) if ns else [],
        compiler_params=_cp(1),
    )(dx2, r2, lg, w_mo, o, bchf, bchf, bchf, bchf, bchf, conv_w, g_attn, g_conv, *swap)


def _conv_bwd(dco, bchf, conv_w):
    T = dco.shape[0]
    tm = _tile(T, 512)
    hb = tm // HALO
    nt = T // tm

    def body(dco_ref, dcon_ref, b_ref, bn_ref, c_ref, h_ref, cp_ref, hp_ref, w_ref, dbch_ref, dw_ref):
        i = pl.program_id(0)

        @pl.when(i == 0)
        def _():
            dw_ref[...] = jnp.zeros_like(dw_ref)

        cc = c_ref[...]
        hh = h_ref[...]
        u, u1, u2, y = _conv_fwd(cc, hh, cp_ref[...], hp_ref[...], w_ref, i == 0, tm)
        dco = dco_ref[...]
        bb = b_ref[...]
        dyc = dco * bb
        dy_next = jnp.where(i == nt - 1, 0.0, dcon_ref[...] * bn_ref[...])
        d1 = _shift_rows(dyc, dy_next, 1, tm, False)
        d2 = _shift_rows(dyc, dy_next, 2, tm, False)
        du = w_ref[2:3, :] * dyc + w_ref[1:2, :] * d1 + w_ref[0:1, :] * d2
        dbch_ref[:, 0:D_CONV] = (dco * y).astype(BF16)
        dbch_ref[:, D_CONV:2 * D_CONV] = (du * hh).astype(BF16)
        dbch_ref[:, 2 * D_CONV:] = (du * cc).astype(BF16)
        dw_ref[0:1, :] += _rowsum(dyc * u2)
        dw_ref[1:2, :] += _rowsum(dyc * u1)
        dw_ref[2:3, :] += _rowsum(dyc * u)

    row = lambda n, col=0: pl.BlockSpec((tm, n), lambda i: (i, col))
    prev = lambda col: pl.BlockSpec((HALO, D_CONV), lambda i: (jnp.maximum(i * hb - 1, 0), col))
    nxt = lambda col: pl.BlockSpec((HALO, D_CONV), lambda i: (jnp.minimum((i + 1) * hb, T // HALO - 1), col))
    return pl.pallas_call(
        body, name="conv_bwd", grid=(nt,),
        in_specs=[row(D_CONV), nxt(0), row(D_CONV, 0), nxt(0), row(D_CONV, 1), row(D_CONV, 2), prev(1), prev(2),
                  _resident((3, D_CONV))],
        out_specs=[row(3 * D_CONV), _resident((8, D_CONV))],
        out_shape=[jax.ShapeDtypeStruct((T, 3 * D_CONV), BF16), jax.ShapeDtypeStruct((8, D_CONV), F32)],
        compiler_params=_cp(1),
    )(dco, dco, bchf, bchf, bchf, bchf, bchf, bchf, conv_w)


def _mixer_in_bwd(dx1a, dqt, dkt, dvt, dbch, dfl, w_qkvt, w_bch, w_f):
    T = dx1a.shape[0]
    tm = _tile(T, 512)

    def body(a_ref, dq_ref, dk_ref, dv_ref, db_ref, df_ref, wq_ref, wb_ref, wf_ref, o_ref):
        acc = a_ref[...] + _nt(db_ref[...], wb_ref[...]) + _nt(df_ref[...], wf_ref[...])
        for n, ref in enumerate((dq_ref, dk_ref, dv_ref)):
            acc = acc + _tn(ref[...].astype(BF16), wq_ref[n * D_ATTN:(n + 1) * D_ATTN, :])
        o_ref[...] = acc

    row = lambda n: pl.BlockSpec((tm, n), lambda i: (i, 0))
    col = pl.BlockSpec((D_ATTN, tm), lambda i: (0, i))
    return pl.pallas_call(
        body, name="mixer_in_bwd", grid=(T // tm,),
        in_specs=[row(D_MODEL), col, col, col, row(3 * D_CONV), row(N_FLOG),
                  _resident((3 * D_ATTN, D_MODEL)), _resident((D_MODEL, 3 * D_CONV)), _resident((D_MODEL, N_FLOG))],
        out_specs=row(D_MODEL),
        out_shape=jax.ShapeDtypeStruct((T, D_MODEL), F32),
        compiler_params=_cp(1),
    )(dx1a, dqt, dkt, dvt, dbch, dfl, w_qkvt, w_bch, w_f)


def _ple_loss(x3, p, tgt, w_g, w_p, b_g, lg, lb):
    T = x3.shape[0]
    tm = _tile(T, 512)

    def body(x_ref, p_ref, t_ref, wg_ref, wp_ref, bg_ref, lg_ref, lb_ref,
             dx_ref, de_ref, dz_ref, loss_ref, dlg_ref, dlb_ref, dbg_ref):
        @pl.when(pl.program_id(0) == 0)
        def _():
            for ref in (loss_ref, dlg_ref, dlb_ref, dbg_ref):
                ref[...] = jnp.zeros_like(ref)

        xf = x_ref[...]
        gate = _sigmoid(_nn(xf.astype(BF16), wg_ref[...]) + bg_ref[...])
        e = _nn(p_ref[...].astype(BF16), wp_ref[...])
        xhat, rstd = _ln_stats(ALPHA * xf + gate * e)
        err = xhat * lg_ref[...] + lb_ref[...] - t_ref[...]
        sq = jnp.sum(_rowsum(err * err), axis=-1, keepdims=True)
        loss_ref[...] += jnp.broadcast_to(sq * (0.5 / D_MODEL), loss_ref.shape)
        dy = err * (1.0 / D_MODEL)
        dr = _ln_bwd(dy, xhat, rstd, lg_ref[...])
        dlg_ref[...] += _rowsum(dy * xhat)
        dlb_ref[...] += _rowsum(dy)
        de_ref[...] = (dr * gate).astype(BF16)
        dz = dr * e * gate * (1.0 - gate)
        dbg_ref[...] += _rowsum(dz)
        dzb = dz.astype(BF16)
        dz_ref[...] = dzb
        dx_ref[...] = ALPHA * dr + _nt(dzb, wg_ref[...])

    row = lambda n: pl.BlockSpec((tm, n), lambda i: (i, 0))
    vec = lambda n: _resident((1, n))
    return pl.pallas_call(
        body, name="ple_loss", grid=(T // tm,),
        in_specs=[row(D_MODEL), row(PLE_DIM), row(D_MODEL), _resident((D_MODEL, D_MODEL)),
                  _resident((PLE_DIM, D_MODEL)), vec(D_MODEL), vec(D_MODEL), vec(D_MODEL)],
        out_specs=[row(D_MODEL), row(D_MODEL), row(D_MODEL), vec(LANE), vec(D_MODEL), vec(D_MODEL), vec(D_MODEL)],
        out_shape=[jax.ShapeDtypeStruct((T, D_MODEL), F32), jax.ShapeDtypeStruct((T, D_MODEL), BF16),
                   jax.ShapeDtypeStruct((T, D_MODEL), BF16), jax.ShapeDtypeStruct((1, LANE), F32),
                   jax.ShapeDtypeStruct((1, D_MODEL), F32), jax.ShapeDtypeStruct((1, D_MODEL), F32),
                   jax.ShapeDtypeStruct((1, D_MODEL), F32)],
        compiler_params=_cp(1),
    )(x3, p, tgt, w_g, w_p, b_g, lg, lb)


def _lane_layout(v8):
    return jnp.repeat(v8, HEAD_DIM, axis=1)


def _row_layout(v8):
    t = v8.shape[0]
    return jnp.pad(v8.T.reshape(N_HEADS // 2, 2, t), ((0, 0), (0, 6), (0, 0)))


def _from_lane_layout(vl):
    return vl[:, ::HEAD_DIM]


def _from_row_layout(vr):
    return vr[:, :2, :].reshape(N_HEADS, -1).T


def _local_step(x, p, tgt, w, overlap=None):
    bf = lambda a: a.astype(BF16)
    w1i, w1o = bf(w["ffn1_w_in"]), bf(w["ffn1_w_out"])
    first = _ffn_fwd(x, w1i, w1o, w["ln1_g"], w["ln1_b"], "ffn1_fwd", overlap["gather"] if overlap else ())
    x1, r1, g1, u1, h1 = first[:5]
    if overlap:
        w = {**w, **overlap["weights"](first[5:])}
    w2i, w2o = bf(w["ffn2_w_in"]), bf(w["ffn2_w_out"])
    wmi = w["w_mix_in"]
    o_f = 3 * D_ATTN
    o_b = o_f + N_HEADS
    w_qkv = bf(wmi[:, :o_f])
    w_f = bf(jnp.pad(wmi[:, o_f:o_b], ((0, 0), (0, N_FLOG - N_HEADS))))
    w_bch = bf(wmi[:, o_b:])
    w_bchf = jnp.concatenate([w_bch, w_f], axis=1)
    w_mo, w_g, w_p = bf(w["w_mix_out"]), bf(w["w_ple_gate"]), bf(w["w_ple"])
    b_f = jnp.pad(w["b_forget"], ((0, 0), (0, N_FLOG - N_HEADS)))

    q_scale = jnp.concatenate([jnp.full((1, D_ATTN), Q_SCALE * LOG2E, F32), jnp.ones((1, 2 * D_ATTN), F32)], axis=1)
    qkv = _matmul_nn(x1, w_qkv, q_scale, BF16, "proj_qkv")
    bchf = _matmul_nn(x1, w_bchf, jnp.ones((1, 3 * D_CONV + N_FLOG), F32), F32, "proj_bchf")
    fcol = 3 * D_CONV // N_FLOG
    c = _forget_cumsum(bchf, fcol, b_f)
    c8 = c[:, :N_HEADS]
    cb = _lane_layout(c8)
    qkvt = qkv.T
    o, alrow = _attn_fwd(qkv, qkvt, _fold_key_bias(qkv, c))
    x2, r2, merged = _mixer_tail_fwd(o, bchf, w["conv_w"], w["g_attn"], w["g_conv"], w_mo, x1, w["ln2_g"], w["ln2_b"])
    x3, r3, g2, u2, h2 = _ffn_fwd(x2, w2i, w2o, w["ln3_g"], w["ln3_b"], "ffn2_fwd")

    grads = {}
    dx3, de, dz, loss, grads["ln4_g"], grads["ln4_b"], grads["b_ple_gate"] = _ple_loss(
        x3, p, tgt, w_g, w_p, w["b_ple_gate"], w["ln4_g"], w["ln4_b"])
    grads["w_ple"] = _matmul_tn(p, de, "dw_ple")
    grads["w_ple_gate"] = _matmul_tn(x3, dz, "dw_ple_gate")

    dx2, dgu2, df2, grads["ln3_g"], grads["ln3_b"] = _ffn_bwd(dx3, r3, g2, u2, w2i, w2o, w["ln3_g"], "ffn2_bwd")
    grads["ffn2_w_in"] = _matmul_tn(x2, dgu2, "dw_ffn2_in")
    grads["ffn2_w_out"] = _matmul_tn(h2, df2, "dw_ffn2_out")

    to_swap = overlap["swap"](grads) if overlap else ()
    tail = _mixer_tail_bwd(dx2, r2, w["ln2_g"], w_mo, o, bchf, w["conv_w"], w["g_attn"], w["g_conv"], to_swap)
    (dx1a, dr2, dob, dobt, delta, dco, grads["ln2_g"], grads["ln2_b"], grads["g_attn"], grads["g_conv"]) = tail[:10]
    grads["w_mix_out"] = _matmul_tn(merged, dr2, "dw_mix_out")
    dbch, dcw = _conv_bwd(dco, bchf, w["conv_w"])
    grads["conv_w"] = dcw[:3]
    dqt, dkt, dvt, dck, dcq = _attn_bwd(qkv, qkvt, dob, dobt, cb, alrow, delta.reshape(N_HEADS // 2, 8, -1))
    dcq_lanes = jnp.pad(_from_row_layout(dcq), ((0, 0), (0, N_FLOG - N_HEADS)))
    dfl, dbf = _forget_bwd(dck, dcq_lanes, bchf, fcol, b_f)
    grads["b_forget"] = dbf[:, :N_HEADS]
    dx1 = _mixer_in_bwd(dx1a, dqt, dkt, dvt, dbch, dfl, w_qkv.T, w_bch, w_f)
    grads["w_mix_in"] = jnp.concatenate(
        [_matmul_tokens(dqt, x1, "dw_q").T, _matmul_tokens(dkt, x1, "dw_k").T, _matmul_tokens(dvt, x1, "dw_v").T,
         _matmul_tn(x1, dfl, "dw_flog")[:, :N_HEADS], _matmul_tn(x1, dbch, "dw_bch")], axis=1)

    dx0, dgu1, df1, grads["ln1_g"], grads["ln1_b"] = _ffn_bwd(dx1, r1, g1, u1, w1i, w1o, w["ln1_g"], "ffn1_bwd")
    grads["ffn1_w_out"] = _matmul_tn(h1, df1, "dw_ffn1_out")
    if not overlap:
        grads["ffn1_w_in"] = _matmul_tn(x, dgu1, "dw_ffn1_in")
        return loss, dx0, grads
    sums = overlap["chip_sums"](grads, to_swap, tail[10:])
    grads["ffn1_w_in"], *received = _matmul_tn(x, dgu1, "dw_ffn1_in", exchange=sums)
    return loss, dx0, grads, sums, received


WEIGHTS = ["ffn1_w_in", "ffn1_w_out", "ln1_g", "ln1_b", "w_mix_in", "b_forget", "conv_w", "g_attn", "g_conv",
           "w_mix_out", "ln2_g", "ln2_b", "ffn2_w_in", "ffn2_w_out", "ln3_g", "ln3_b", "w_ple", "w_ple_gate",
           "b_ple_gate", "ln4_g", "ln4_b"]
LAYOUT = {
    "ffn1_w_in": ((D_MODEL, 2 * D_FF), 1), "ffn1_w_out": ((D_FF, D_MODEL), 0),
    "w_mix_in": ((D_MODEL, 3 * D_ATTN + N_HEADS + 3 * D_CONV), 1), "conv_w": ((3, D_CONV), 1),
    "w_mix_out": ((D_MODEL, D_MODEL), 0), "ffn2_w_in": ((D_MODEL, 2 * D_FF), 1), "ffn2_w_out": ((D_FF, D_MODEL), 0),
    "w_ple": ((PLE_DIM, D_MODEL), 1), "w_ple_gate": ((D_MODEL, D_MODEL), 0),
    "ln1_g": ((1, D_MODEL), None), "ln1_b": ((1, D_MODEL), None), "b_forget": ((1, N_HEADS), None),
    "g_attn": ((1, D_ATTN), None), "g_conv": ((1, D_CONV), None), "ln2_g": ((1, D_MODEL), None),
    "ln2_b": ((1, D_MODEL), None), "ln3_g": ((1, D_MODEL), None), "ln3_b": ((1, D_MODEL), None),
    "b_ple_gate": ((1, D_MODEL), None), "ln4_g": ((1, D_MODEL), None), "ln4_b": ((1, D_MODEL), None),
}
BIG = [n for n in WEIGHTS if LAYOUT[n][1] is not None and n != "conv_w"]
SMALL = [n for n in WEIGHTS if n not in BIG]
ROW = 1024
SMALL_ROWS = 16


def _shard_shape(name):
    shape, axis = LAYOUT[name]
    if axis is None:
        return shape
    return tuple(s // N_CHIPS if a == axis else s for a, s in enumerate(shape))


def _halves(a):
    return a.reshape(a.shape[:-2] + (2, a.shape[-2] // 2, a.shape[-1]))


def _split_chips(name, full):
    shape, axis = LAYOUT[name]
    if axis == 0:
        return full.reshape((N_CHIPS, shape[0] // N_CHIPS) + shape[1:])
    return jnp.moveaxis(full.reshape(shape[:1] + (N_CHIPS, shape[1] // N_CHIPS)), 1, 0)


def _join_chips(name, parts):
    shape, axis = LAYOUT[name]
    if axis == 0:
        return parts.reshape(shape)
    return jnp.moveaxis(parts, 0, 1).reshape(shape)


SMALL_AT = {"ln1_g": (0, 0), "ln1_b": (1, 0), "ln2_g": (2, 0), "ln2_b": (3, 0), "ln3_g": (4, 0), "ln3_b": (5, 0),
            "b_ple_gate": (6, 0), "ln4_g": (7, 0), "ln4_b": (8, 0), "g_attn": (9, 0), "g_conv": (9, D_ATTN),
            "b_forget": (10, 0), "conv_w": (10, LANE)}
CONV_SHARD = D_CONV // N_CHIPS


def _pack_small_grads(grads):
    def body(*refs):
        ins, o_ref = dict(zip(SMALL, refs[:-1])), refs[-1]
        o_ref[...] = jnp.zeros_like(o_ref)
        for s in range(N_CHIPS):
            for n in SMALL:
                r, c0 = SMALL_AT[n]
                if n == "conv_w":
                    for k in range(3):
                        o_ref[s, r:r + 1, c0 + k * CONV_SHARD:c0 + (k + 1) * CONV_SHARD] = (
                            ins[n][k:k + 1, s * CONV_SHARD:(s + 1) * CONV_SHARD])
                else:
                    o_ref[s, r:r + 1, c0:c0 + ins[n].shape[1]] = ins[n][...]

    return pl.pallas_call(
        body, name="pack_small_grads",
        out_shape=jax.ShapeDtypeStruct((N_CHIPS, SMALL_ROWS, ROW), F32),
    )(*[grads[n] for n in SMALL])


def _adamw_math(w, g, m, v):
    c1 = 1.0 - ADAM_B1 ** ADAM_STEP
    c2 = 1.0 - ADAM_B2 ** ADAM_STEP
    m = ADAM_B1 * m + (1.0 - ADAM_B1) * g
    v = ADAM_B2 * v + (1.0 - ADAM_B2) * (g * g)
    return -ADAM_LR * ((m / c1) / (jnp.sqrt(v / c2) + ADAM_EPS) + ADAM_WD * w), m, v


def _adamw_small(g_mine, g_sib, c_idx, w, m, v):
    ns = len(SMALL)

    def body(c_ref, gm_ref, gs_ref, *refs):
        ws, ms, vs = refs[:ns], refs[ns:2 * ns], refs[2 * ns:3 * ns]
        outs = refs[3 * ns:]
        mine_first = c_ref[0] == 0
        top = jnp.where(mine_first, gm_ref[...], gs_ref[...])
        bot = jnp.where(mine_first, gs_ref[...], gm_ref[...])
        for i, n in enumerate(SMALL):
            r, c0 = SMALL_AT[n]
            blk, rr = (top, r) if r < SMALL_ROWS // 2 else (bot, r - SMALL_ROWS // 2)
            rows, width = ws[i].shape
            for k in range(rows):
                g = blk[rr:rr + 1, c0 + k * width:c0 + (k + 1) * width]
                d, mn, vn = _adamw_math(ws[i][k:k + 1, :], g, ms[i][k:k + 1, :], vs[i][k:k + 1, :])
                for q, val in enumerate((g, d, mn, vn)):
                    outs[q * ns + i][k:k + 1, :] = val

    shapes = [jax.ShapeDtypeStruct(a.shape, F32) for a in w]
    vmem = pl.BlockSpec(memory_space=pltpu.VMEM)
    res = pl.pallas_call(
        body, name="adamw_small",
        in_specs=[pl.BlockSpec(memory_space=pltpu.SMEM)] + [vmem] * (2 + 3 * ns),
        out_specs=[vmem] * (4 * ns),
        out_shape=shapes * 4,
    )(c_idx, g_mine, g_sib, *w, *m, *v)
    return [res[q * ns:(q + 1) * ns] for q in range(4)]


def _place():
    x, y, c = lax.axis_index("x"), lax.axis_index("y"), lax.axis_index("c")
    others = [(1 - x, y), (x, 1 - y), (1 - x, 1 - y)]
    return x, y, c, others


ANY_SPEC = pl.BlockSpec(memory_space=pl.ANY)


def _remote(src, dst, send_sems, recv_sems, k, to):
    return pltpu.make_async_remote_copy(src_ref=src, dst_ref=dst, send_sem=send_sems.at[k], recv_sem=recv_sems.at[k],
                                        device_id=to, device_id_type=MESH)


def _all_gather(shards):
    n = len(shards)

    def body(*refs):
        ins, outs, send_sems, recv_sems = refs[:n], refs[n:2 * n], refs[2 * n], refs[2 * n + 1]
        _gather_start(ins, outs, send_sems, recv_sems)
        _gather_finish(ins, outs, send_sems, recv_sems)

    return pl.pallas_call(
        body, name="all_gather_weights",
        out_shape=_gather_shapes(shards), in_specs=[ANY_SPEC] * n, out_specs=[ANY_SPEC] * n,
        scratch_shapes=_gather_sems(n),
    )(*shards)


def _gather_shapes(shards):
    return [jax.ShapeDtypeStruct((N_CHIPS,) + a.shape, a.dtype) for a in shards]


def _gather_sems(n):
    return [pltpu.SemaphoreType.DMA((6 * n,)), pltpu.SemaphoreType.DMA((6 * n,))]


def _gather_sends(ins, outs, send_sems, recv_sems):
    x, y, c, others = _place()
    s = 2 * x + y
    return [_remote(ins[t].at[c], outs[t].at[s, c], send_sems, recv_sems, 6 * t + j, (*chip, c))
            for t in range(len(ins)) for j, chip in enumerate(others)]


def _gather_start(ins, outs, send_sems, recv_sems):
    for cp in _gather_sends(ins, outs, send_sems, recv_sems):
        cp.start()


def _gather_finish(ins, outs, send_sems, recv_sems):
    x, y, c, others = _place()
    slot = lambda t, chip, half: outs[t].at[2 * chip[0] + chip[1], half]
    passed = []
    for t in range(len(ins)):
        for j, chip in enumerate(others):
            landed = slot(t, chip, c)
            _remote(landed, landed, send_sems, recv_sems, 6 * t + j, (x, y, c)).wait_recv()
            passed.append(_remote(landed, landed, send_sems, recv_sems, 6 * t + 3 + j, (x, y, 1 - c)))
            passed[-1].start()
    for t in range(len(ins)):
        for j, chip in enumerate(others):
            landed = slot(t, chip, 1 - c)
            _remote(landed, landed, send_sems, recv_sems, 6 * t + 3 + j, (x, y, c)).wait_recv()
    for cp in _gather_sends(ins, outs, send_sems, recv_sems) + passed:
        cp.wait_send()


def _swap_halves(gs, tag):
    n = len(gs)

    def body(*refs):
        ins, outs, send_sems, recv_sems = refs[:n], refs[n:2 * n], refs[2 * n], refs[2 * n + 1]
        _swap_start(ins, outs, send_sems, recv_sems)
        _swap_finish(ins, outs, send_sems, recv_sems)

    return pl.pallas_call(
        body, name="grad_swap_halves_" + tag,
        out_shape=_swap_shapes(gs), in_specs=[ANY_SPEC] * n, out_specs=[ANY_SPEC] * n,
        scratch_shapes=_swap_sems(n),
    )(*gs)


def _swap_shapes(gs):
    return [jax.ShapeDtypeStruct(g.shape[:1] + g.shape[2:], g.dtype) for g in gs]


def _swap_sems(n):
    return [pltpu.SemaphoreType.DMA((n,)), pltpu.SemaphoreType.DMA((n,))]


def _swap_copies(ins, outs, send_sems, recv_sems):
    x, y, c, _ = _place()
    return [_remote(ins[t].at[:, 1 - c], outs[t], send_sems, recv_sems, t, (x, y, 1 - c)) for t in range(len(ins))]


def _swap_start(ins, outs, send_sems, recv_sems):
    for cp in _swap_copies(ins, outs, send_sems, recv_sems):
        cp.start()


def _swap_finish(ins, outs, send_sems, recv_sems):
    for cp in _swap_copies(ins, outs, send_sems, recv_sems):
        cp.wait()


def _exchange_chips(pps):
    n = len(pps)

    def body(*refs):
        ins, outs, send_sems, recv_sems = refs[:n], refs[n:2 * n], refs[2 * n], refs[2 * n + 1]
        _exchange_start(ins, outs, send_sems, recv_sems)
        _exchange_finish(ins, outs, send_sems, recv_sems)

    return pl.pallas_call(
        body, name="grad_exchange_chips",
        out_shape=_exchange_shapes(pps), in_specs=[ANY_SPEC] * n, out_specs=[ANY_SPEC] * n,
        scratch_shapes=_exchange_sems(n),
    )(*pps)


def _exchange_shapes(pps):
    return [jax.ShapeDtypeStruct(p.shape, p.dtype) for p in pps]


def _exchange_sems(n):
    return [pltpu.SemaphoreType.DMA((3 * n,)), pltpu.SemaphoreType.DMA((3 * n,))]


def _exchange_sends(ins, outs, send_sems, recv_sems):
    x, y, c, others = _place()
    s = 2 * x + y
    return [_remote(ins[t].at[2 * chip[0] + chip[1]], outs[t].at[s], send_sems, recv_sems, 3 * t + j, (*chip, c))
            for t in range(len(ins)) for j, chip in enumerate(others)]


def _exchange_start(ins, outs, send_sems, recv_sems):
    for cp in _exchange_sends(ins, outs, send_sems, recv_sems):
        cp.start()


def _exchange_finish(ins, outs, send_sems, recv_sems):
    x, y, c, others = _place()
    for t in range(len(ins)):
        for j, chip in enumerate(others):
            landed = outs[t].at[2 * chip[0] + chip[1]]
            _remote(landed, landed, send_sems, recv_sems, 3 * t + j, (x, y, c)).wait_recv()
    for cp in _exchange_sends(ins, outs, send_sems, recv_sems):
        cp.wait_send()


def _share_half(rs):
    n = len(rs)

    def body(*refs):
        ins, outs, send_sems, recv_sems = refs[:n], refs[n:2 * n], refs[2 * n], refs[2 * n + 1]
        x, y, c, _ = _place()
        copies = [_remote(ins[t], outs[t], send_sems, recv_sems, t, (x, y, 1 - c)) for t in range(n)]
        for cp in copies:
            cp.start()
        for cp in copies:
            cp.wait()

    return pl.pallas_call(
        body, name="grad_share_half",
        out_shape=[jax.ShapeDtypeStruct(r.shape, r.dtype) for r in rs],
        in_specs=[ANY_SPEC] * n, out_specs=[ANY_SPEC] * n,
        scratch_shapes=[pltpu.SemaphoreType.DMA((n,)), pltpu.SemaphoreType.DMA((n,))],
    )(*rs)


ELEMENTWISE_BLOCK_BYTES = 1 << 20


def _row_tile(rows, cols):
    return _tile(rows, max(8, ELEMENTWISE_BLOCK_BYTES // (4 * cols) // 8 * 8))


def _add_my_half(g, sib, c_idx, name):
    rh, cols = g.shape[2:]
    tr = _row_tile(rh, cols)

    def body(c_ref, g_ref, s_ref, o_ref):
        o_ref[...] = (g_ref[...] + s_ref[...]).astype(BF16)

    return pl.pallas_call(
        body, name="grad_add_halves_" + name,
        grid_spec=pltpu.PrefetchScalarGridSpec(
            num_scalar_prefetch=1, grid=(N_CHIPS, rh // tr),
            in_specs=[pl.BlockSpec((None, None, tr, cols), lambda s, i, c: (s, c[0], i, 0)),
                      pl.BlockSpec((None, tr, cols), lambda s, i, c: (s, i, 0))],
            out_specs=pl.BlockSpec((None, tr, cols), lambda s, i, c: (s, i, 0))),
        out_shape=jax.ShapeDtypeStruct((N_CHIPS, rh, cols), BF16),
        compiler_params=_cp(2),
    )(c_idx, g, sib)


def _sum_chips(parts, pp, s_idx, name):
    rh, cols = parts.shape[1:]
    tr = _row_tile(rh, cols)

    def body(s_ref, p0, p1, p2, p3, mine_ref, o_ref):
        own = mine_ref[...]
        t = [jnp.where(s_ref[0] == k, own, p[...]).astype(F32) for k, p in enumerate((p0, p1, p2, p3))]
        o_ref[...] = ((t[0] + t[1]) + t[2]) + t[3]

    slot = lambda k: pl.BlockSpec((None, tr, cols), lambda i, s: (jnp.where(s[0] == k, (k + 1) % N_CHIPS, k), i, 0))
    return pl.pallas_call(
        body, name="grad_sum_chips_" + name,
        grid_spec=pltpu.PrefetchScalarGridSpec(
            num_scalar_prefetch=1, grid=(rh // tr,),
            in_specs=[slot(0), slot(1), slot(2), slot(3), pl.BlockSpec((None, tr, cols), lambda i, s: (s[0], i, 0))],
            out_specs=pl.BlockSpec((tr, cols), lambda i, s: (i, 0))),
        out_shape=jax.ShapeDtypeStruct((rh, cols), F32),
        compiler_params=_cp(1),
    )(s_idx, parts, parts, parts, parts, pp)


def _adamw(w, g_mine, g_sib, m, v, c_idx, name):
    rows, cols = w.shape
    tr = _row_tile(rows // 2, cols)
    nbh = rows // 2 // tr

    def body(c_ref, w_ref, gm_ref, gs_ref, m_ref, v_ref, g_ref, d_ref, mo_ref, vo_ref):
        g = jnp.where(pl.program_id(0) // nbh == c_ref[0], gm_ref[...], gs_ref[...])
        g_ref[...] = g
        d_ref[...], mo_ref[...], vo_ref[...] = _adamw_math(w_ref[...], g, m_ref[...], v_ref[...])

    spec = pl.BlockSpec((tr, cols), lambda i, c: (i, 0))
    half = pl.BlockSpec((tr, cols), lambda i, c: (i % nbh, 0))
    return pl.pallas_call(
        body, name="adamw_" + name,
        grid_spec=pltpu.PrefetchScalarGridSpec(
            num_scalar_prefetch=1, grid=(rows // tr,),
            in_specs=[spec, half, half, spec, spec], out_specs=[spec] * 4),
        out_shape=[jax.ShapeDtypeStruct(w.shape, F32)] * 4,
        compiler_params=_cp(1),
    )(c_idx, w, g_mine, g_sib, m, v)


def kernel(x, p, ffn1_w_in, ffn1_w_out, ln1_g, ln1_b, w_mix_in, b_forget, conv_w, g_attn, g_conv, w_mix_out, ln2_g, ln2_b, ffn2_w_in, ffn2_w_out, ln3_g, ln3_b, w_ple, w_ple_gate, b_ple_gate, ln4_g, ln4_b, loss_target, m_ffn1_w_in, m_ffn1_w_out, m_ln1_g, m_ln1_b, m_w_mix_in, m_b_forget, m_conv_w, m_g_attn, m_g_conv, m_w_mix_out, m_ln2_g, m_ln2_b, m_ffn2_w_in, m_ffn2_w_out, m_ln3_g, m_ln3_b, m_w_ple, m_w_ple_gate, m_b_ple_gate, m_ln4_g, m_ln4_b, v_ffn1_w_in, v_ffn1_w_out, v_ln1_g, v_ln1_b, v_w_mix_in, v_b_forget, v_conv_w, v_g_attn, v_g_conv, v_w_mix_out, v_ln2_g, v_ln2_b, v_ffn2_w_in, v_ffn2_w_out, v_ln3_g, v_ln3_b, v_w_ple, v_w_ple_gate, v_b_ple_gate, v_ln4_g, v_ln4_b):
    args = dict(locals())
    shard = {n: args[n][0] if LAYOUT[n][1] is not None else args[n] for n in WEIGHTS}
    m_shard = {n: args["m_" + n][0] if LAYOUT[n][1] is not None else args["m_" + n] for n in WEIGHTS}
    v_shard = {n: args["v_" + n][0] if LAYOUT[n][1] is not None else args["v_" + n] for n in WEIGHTS}
    c_idx = lax.axis_index("c").astype(jnp.int32).reshape(1)
    chip = (2 * lax.axis_index("x") + lax.axis_index("y")).astype(jnp.int32)

    conv_rows = SMALL_ROWS - shard["conv_w"].shape[0]
    mine = {n: _halves(shard[n].astype(BF16)) for n in BIG}
    mine["conv_w"] = _halves(jnp.pad(shard["conv_w"], ((0, conv_rows), (0, 0))))
    early_w = ["ffn1_w_in", "ffn1_w_out"]
    late_w = [n for n in BIG if n not in early_w] + ["conv_w"]

    def full_weights(names, gathered):
        out = {}
        for n, theirs in zip(names, gathered):
            g = lax.dynamic_update_slice(theirs, mine[n][None], (chip, 0, 0, 0))
            if n == "conv_w":
                out[n] = _join_chips(n, g.reshape(N_CHIPS, SMALL_ROWS, CONV_SHARD)[:, :3])
            else:
                out[n] = _join_chips(n, g.reshape((N_CHIPS,) + _shard_shape(n)))
        return out

    full = full_weights(early_w, _all_gather([mine[n] for n in early_w]))
    full.update({n: shard[n] for n in SMALL if n != "conv_w"})

    def per_chip(names, grads):
        return [_halves(_pack_small_grads(grads)) if n == "small" else _halves(_split_chips(n, grads[n]))
                for n in names]

    def add_halves(names, mine_, sibs):
        return [_add_my_half(g, sib, c_idx, n) for n, g, sib in zip(names, mine_, sibs)]

    def chip_sums(names, grads):
        mine_ = per_chip(names, grads)
        return add_halves(names, mine_, _swap_halves(mine_, names[0]))

    ready_a = ["ffn2_w_in", "ffn2_w_out", "w_ple", "w_ple_gate"]
    ready_b = ["w_mix_in", "w_mix_out"]
    early_g = ready_a + ready_b
    late_g = early_w + ["small"]
    loss_acc, grad_x, grads, early_sums, early_parts = _local_step(
        x[0], p[0, 0], loss_target[0], full,
        overlap={"gather": [mine[n] for n in late_w], "weights": lambda gathered: full_weights(late_w, gathered),
                 "swap": lambda grads: per_chip(ready_a, grads),
                 "chip_sums": lambda grads, swapped, received: (add_halves(ready_a, swapped, received)
                                                                + chip_sums(ready_b, grads))})
    loss = lax.psum(loss_acc[0, 0], ("x", "y", "c"))

    late_sums = chip_sums(late_g, grads)
    names = early_g + late_g
    sums = list(early_sums) + late_sums
    parts = list(early_parts) + list(_exchange_chips(late_sums))
    half_of = {n: _sum_chips(pt, own, chip.reshape(1), n) for n, pt, own in zip(names, parts, sums)}
    names = BIG + ["small"]
    my_half = [half_of[n] for n in names]
    sib_half = _share_half(my_half)

    out = {}
    for n, gm, gs in zip(BIG, my_half, sib_half):
        out[n] = [a[None] for a in _adamw(shard[n], gm, gs, m_shard[n], v_shard[n], c_idx, n)]
    small = _adamw_small(my_half[-1], sib_half[-1], c_idx, [shard[n] for n in SMALL], [m_shard[n] for n in SMALL],
                         [v_shard[n] for n in SMALL])
    for i, n in enumerate(SMALL):
        out[n] = [small[q][i][None] if n == "conv_w" else small[q][i] for q in range(4)]
    return (loss, grad_x[None], *[out[n][q] for q in range(4) for n in WEIGHTS])
```

```python
import functools
import math

import jax
import jax.numpy as jnp
from jax import lax
from jax.experimental import pallas as pl
from jax.experimental.pallas import tpu as pltpu

F32 = jnp.float32
BF16 = jnp.bfloat16

D_MODEL = 1024
D_FF = 2816
N_HEADS = 8
HEAD_DIM = 64
D_ATTN = N_HEADS * HEAD_DIM
D_CONV = 512
PLE_DIM = 256
N_FLOG = 128
ALPHA = 2.0 ** 0.25
LN_EPS = 1e-5
RMS_EPS = 1e-6
NEG_INF = -1e30
Q_SCALE = 1.0 / math.sqrt(HEAD_DIM)
LOG2E = math.log2(math.e)

ADAM_LR = 0.001
ADAM_B1 = 0.9
ADAM_B2 = 0.999
ADAM_EPS = 1e-08
ADAM_WD = 0.01
ADAM_STEP = 10

V7X_VMEM_BYTES = 64 << 20
VMEM_LIMIT = V7X_VMEM_BYTES - (8 << 20)
LANE = 128
FF_CHUNK = 256
N_CHIPS = 4
MESH = pl.DeviceIdType.MESH


def _cp(n_axes):
    return pltpu.CompilerParams(dimension_semantics=("arbitrary",) * n_axes, vmem_limit_bytes=VMEM_LIMIT)


def _resident(shape):
    n = len(shape)
    return pl.BlockSpec(shape, lambda *_: (0,) * n, pipeline_mode=pl.Buffered(1))


def _nn(a, b):
    return jnp.dot(a, b, preferred_element_type=F32)


def _nt(a, b):
    return lax.dot_general(a, b, (((1,), (1,)), ((), ())), preferred_element_type=F32)


def _tn(a, b):
    return lax.dot_general(a, b, (((0,), (0,)), ((), ())), preferred_element_type=F32)


def _ln_stats(r):
    mu = jnp.mean(r, axis=-1, keepdims=True)
    xc = r - mu
    var = jnp.mean(xc * xc, axis=-1, keepdims=True)
    rstd = lax.rsqrt(var + LN_EPS)
    return xc * rstd, rstd


def _ln_bwd(dy, xhat, rstd, g):
    dxh = dy * g
    m1 = jnp.mean(dxh, axis=-1, keepdims=True)
    m2 = jnp.mean(dxh * xhat, axis=-1, keepdims=True)
    return rstd * (dxh - m1 - xhat * m2)


def _sigmoid(z):
    return 1.0 / (1.0 + jnp.exp(-z))


def _rowsum(a):
    return jnp.sum(a, axis=0, keepdims=True)


def _tile(total, want):
    if total <= want:
        return total
    for t in range(want - want % 8, 0, -8):
        if total % t == 0:
            return t
    raise ValueError((total, want))


def _ffn_fwd(x, w_in, w_out, lg, lb, name, gather=()):
    T = x.shape[0]
    tm = _tile(T, 512)
    nf = D_FF // FF_CHUNK
    ng = len(gather)
    last = T // tm - 1

    def body(x_ref, wi_ref, wo_ref, lg_ref, lb_ref, *rest):
        comm_in, (xo_ref, r_ref, g_ref, u_ref, h_ref) = rest[:ng], rest[ng:ng + 5]
        comm_out, sems = rest[ng + 5:2 * ng + 5], rest[2 * ng + 5:]
        if ng:
            @pl.when(pl.program_id(0) == 0)
            def _():
                _gather_start(comm_in, comm_out, *sems)

        xf = x_ref[...]
        xb = xf.astype(BF16)
        acc = jnp.zeros((tm, D_MODEL), F32)
        for j in range(nf):
            c0 = j * FF_CHUNK
            g = _nn(xb, wi_ref[:, c0:c0 + FF_CHUNK])
            u = _nn(xb, wi_ref[:, D_FF + c0:D_FF + c0 + FF_CHUNK])
            hb = (g * _sigmoid(g) * u).astype(BF16)
            g_ref[:, c0:c0 + FF_CHUNK] = g.astype(BF16)
            u_ref[:, c0:c0 + FF_CHUNK] = u.astype(BF16)
            h_ref[:, c0:c0 + FF_CHUNK] = hb
            acc = acc + _nn(hb, wo_ref[c0:c0 + FF_CHUNK, :])
        r = ALPHA * xf + 0.5 * acc
        r_ref[...] = r
        xhat, _ = _ln_stats(r)
        xo_ref[...] = xhat * lg_ref[...] + lb_ref[...]
        if ng:
            @pl.when(pl.program_id(0) == last)
            def _():
                _gather_finish(comm_in, comm_out, *sems)

    row = lambda n: pl.BlockSpec((tm, n), lambda i: (i, 0))
    return pl.pallas_call(
        body, name=name, grid=(T // tm,),
        in_specs=[row(D_MODEL), _resident((D_MODEL, 2 * D_FF)), _resident((D_FF, D_MODEL)),
                  _resident((1, D_MODEL)), _resident((1, D_MODEL))] + [ANY_SPEC] * ng,
        out_specs=[row(D_MODEL), row(D_MODEL), row(D_FF), row(D_FF), row(D_FF)] + [ANY_SPEC] * ng,
        out_shape=[jax.ShapeDtypeStruct((T, D_MODEL), F32), jax.ShapeDtypeStruct((T, D_MODEL), F32),
                   jax.ShapeDtypeStruct((T, D_FF), BF16), jax.ShapeDtypeStruct((T, D_FF), BF16),
                   jax.ShapeDtypeStruct((T, D_FF), BF16)] + _gather_shapes(gather),
        scratch_shapes=_gather_sems(ng) if ng else [],
        compiler_params=_cp(1),
    )(x, w_in, w_out, lg, lb, *gather)


def _ffn_bwd(dxo, r, g, u, w_in, w_out, lg, name, exchange=()):
    T = r.shape[0]
    tm = _tile(T, 256)
    nf = D_FF // FF_CHUNK
    ne = len(exchange)
    last = T // tm - 1

    def body(dxo_ref, r_ref, g_ref, u_ref, wi_ref, wo_ref, lg_ref, *rest):
        comm_in, (dx_ref, dgu_ref, df_ref, dlg_ref, dlb_ref) = rest[:ne], rest[ne:ne + 5]
        comm_out, sems = rest[ne + 5:2 * ne + 5], rest[2 * ne + 5:]
        i = pl.program_id(0)
        if ne:
            @pl.when(i == 0)
            def _():
                _exchange_start(comm_in, comm_out, *sems)

        dy = dxo_ref[...]
        xhat, rstd = _ln_stats(r_ref[...])
        dr = _ln_bwd(dy, xhat, rstd, lg_ref[...])

        @pl.when(i == 0)
        def _():
            dlg_ref[...] = jnp.zeros_like(dlg_ref)
            dlb_ref[...] = jnp.zeros_like(dlb_ref)

        dlg_ref[...] += _rowsum(dy * xhat)
        dlb_ref[...] += _rowsum(dy)
        dfb = (0.5 * dr).astype(BF16)
        df_ref[...] = dfb
        acc = jnp.zeros((tm, D_MODEL), F32)
        dh_ahead = _nt(dfb, wo_ref[0:FF_CHUNK, :])
        for j in range(nf):
            c0 = j * FF_CHUNK
            dh = dh_ahead
            if j + 1 < nf:
                dh_ahead = _nt(dfb, wo_ref[c0 + FF_CHUNK:c0 + 2 * FF_CHUNK, :])
            gg = g_ref[:, c0:c0 + FF_CHUNK].astype(F32)
            uu = u_ref[:, c0:c0 + FF_CHUNK].astype(F32)
            s = _sigmoid(gg)
            dgb = (dh * uu * s * (1.0 + gg * (1.0 - s))).astype(BF16)
            dub = (dh * gg * s).astype(BF16)
            dgu_ref[:, c0:c0 + FF_CHUNK] = dgb
            dgu_ref[:, D_FF + c0:D_FF + c0 + FF_CHUNK] = dub
            acc = acc + _nt(dgb, wi_ref[:, c0:c0 + FF_CHUNK]) + _nt(dub, wi_ref[:, D_FF + c0:D_FF + c0 + FF_CHUNK])
        dx_ref[...] = ALPHA * dr + acc
        if ne:
            @pl.when(i == last)
            def _():
                _exchange_finish(comm_in, comm_out, *sems)

    row = lambda n: pl.BlockSpec((tm, n), lambda i: (i, 0))
    return pl.pallas_call(
        body, name=name, grid=(T // tm,),
        in_specs=[row(D_MODEL), row(D_MODEL), row(D_FF), row(D_FF), _resident((D_MODEL, 2 * D_FF)),
                  _resident((D_FF, D_MODEL)), _resident((1, D_MODEL))] + [ANY_SPEC] * ne,
        out_specs=[row(D_MODEL), row(2 * D_FF), row(D_MODEL), _resident((1, D_MODEL)), _resident((1, D_MODEL))]
        + [ANY_SPEC] * ne,
        out_shape=[jax.ShapeDtypeStruct((T, D_MODEL), F32), jax.ShapeDtypeStruct((T, 2 * D_FF), BF16),
                   jax.ShapeDtypeStruct((T, D_MODEL), BF16), jax.ShapeDtypeStruct((1, D_MODEL), F32),
                   jax.ShapeDtypeStruct((1, D_MODEL), F32)] + _exchange_shapes(exchange),
        scratch_shapes=_exchange_sems(ne) if ne else [],
        compiler_params=_cp(1),
    )(dxo, r, g, u, w_in, w_out, lg, *exchange)


def _matmul_tn(a, b, name, tn=None, exchange=()):
    T, K = a.shape
    N = b.shape[1]
    tt = _tile(T, 1024)
    if tn is None:
        tn = N
        while K * tn * 4 > (6 << 20) and tn % 256 == 0:
            tn //= 2
    assert N % tn == 0
    ne = len(exchange)
    grid = (N // tn, T // tt)

    def body(a_ref, b_ref, *rest):
        comm_in, o_ref, comm_out, sems = rest[:ne], rest[ne], rest[ne + 1:2 * ne + 1], rest[2 * ne + 1:]
        n, t = pl.program_id(0), pl.program_id(1)
        if ne:
            @pl.when((n == 0) & (t == 0))
            def _():
                _exchange_start(comm_in, comm_out, *sems)

        @pl.when(t == 0)
        def _():
            o_ref[...] = jnp.zeros_like(o_ref)

        o_ref[...] += _tn(a_ref[...].astype(BF16), b_ref[...].astype(BF16))
        if ne:
            @pl.when((n == grid[0] - 1) & (t == grid[1] - 1))
            def _():
                _exchange_finish(comm_in, comm_out, *sems)

    res = pl.pallas_call(
        body, name=name, grid=grid,
        in_specs=[pl.BlockSpec((tt, K), lambda n, t: (t, 0)), pl.BlockSpec((tt, tn), lambda n, t: (t, n))]
        + [ANY_SPEC] * ne,
        out_specs=[pl.BlockSpec((K, tn), lambda n, t: (0, n))] + [ANY_SPEC] * ne,
        out_shape=[jax.ShapeDtypeStruct((K, N), F32)] + _exchange_shapes(exchange),
        scratch_shapes=_exchange_sems(ne) if ne else [],
        compiler_params=_cp(2),
    )(a, b, *exchange)
    return res if ne else res[0]


def _matmul_tokens(at, b, name):
    M, T = at.shape
    N = b.shape[1]
    tt = _tile(T, 1024)

    def body(a_ref, b_ref, o_ref):
        @pl.when(pl.program_id(0) == 0)
        def _():
            o_ref[...] = jnp.zeros_like(o_ref)

        o_ref[...] += _nn(a_ref[...].astype(BF16), b_ref[...].astype(BF16))

    return pl.pallas_call(
        body, name=name, grid=(T // tt,),
        in_specs=[pl.BlockSpec((M, tt), lambda t: (0, t)), pl.BlockSpec((tt, N), lambda t: (t, 0))],
        out_specs=pl.BlockSpec((M, N), lambda t: (0, 0)),
        out_shape=jax.ShapeDtypeStruct((M, N), F32),
        compiler_params=_cp(1),
    )(at, b)


def _matmul_nn(x, w, scale, out_dtype, name, also_transposed=False):
    T, K = x.shape
    N = w.shape[1]
    tm = _tile(T, 512)

    def body(x_ref, w_ref, s_ref, o_ref, *ot_ref):
        res = _nn(x_ref[...].astype(BF16), w_ref[...]) * s_ref[...]
        o_ref[...] = res.astype(out_dtype)
        if also_transposed:
            ot_ref[0][...] = res.T.astype(out_dtype)

    res = pl.pallas_call(
        body, name=name, grid=(T // tm,),
        in_specs=[pl.BlockSpec((tm, K), lambda i: (i, 0)), _resident((K, N)), _resident((1, N))],
        out_specs=[pl.BlockSpec((tm, N), lambda i: (i, 0))] + [pl.BlockSpec((N, tm), lambda i: (0, i))] * also_transposed,
        out_shape=[jax.ShapeDtypeStruct((T, N), out_dtype)] + [jax.ShapeDtypeStruct((N, T), out_dtype)] * also_transposed,
        compiler_params=_cp(1),
    )(x, w, scale)
    return res if also_transposed else res[0]


def _log_sigmoid(z):
    return jnp.minimum(z, 0.0) - jnp.log1p(jnp.exp(-jnp.abs(z)))


def _tri(n, lower):
    r = lax.broadcasted_iota(jnp.int32, (n, n), 0)
    c = lax.broadcasted_iota(jnp.int32, (n, n), 1)
    return jnp.where((c <= r) if lower else (c >= r), 1.0, 0.0).astype(F32)


def _f32dot(a, b):
    return jnp.dot(a, b, preferred_element_type=F32, precision=lax.Precision.HIGHEST)


def _forget_cumsum(flog, col, bf):
    T = flog.shape[0]
    bt = _tile(T, 512)

    def body(f_ref, b_ref, c_ref, carry):
        @pl.when(pl.program_id(0) == 0)
        def _():
            carry[...] = jnp.zeros_like(carry)

        lf = _log_sigmoid(f_ref[...] + b_ref[...])
        c = _f32dot(_tri(bt, True), lf) + carry[...]
        c_ref[...] = c * LOG2E
        carry[...] = c[bt - 1:bt, :]

    return pl.pallas_call(
        body, name="forget_cumsum", grid=(T // bt,),
        in_specs=[pl.BlockSpec((bt, N_FLOG), lambda i: (i, col)), _resident((1, N_FLOG))],
        out_specs=pl.BlockSpec((bt, N_FLOG), lambda i: (i, 0)),
        out_shape=jax.ShapeDtypeStruct((T, N_FLOG), F32),
        scratch_shapes=[pltpu.VMEM((1, N_FLOG), F32)],
        compiler_params=_cp(1),
    )(flog, bf)


def _forget_bwd(dck, dcq, flog, col, bf):
    T = dcq.shape[0]
    bt = _tile(T, 512)
    nb = T // bt

    def body(k0_ref, k1_ref, k2_ref, k3_ref, dcq_ref, f_ref, b_ref, dz_ref, db_ref, carry):
        @pl.when(pl.program_id(0) == 0)
        def _():
            carry[...] = jnp.zeros_like(carry)
            db_ref[...] = jnp.zeros_like(db_ref)

        dc = ((k0_ref[...] + k1_ref[...]) + (k2_ref[...] + k3_ref[...])) + dcq_ref[...]
        dlf = _f32dot(_tri(bt, False), dc) + carry[...]
        carry[...] = dlf[0:1, :]
        z = f_ref[...] + b_ref[...]
        dz = dlf * _sigmoid(-z)
        dz_ref[...] = dz.astype(BF16)
        db_ref[...] += _rowsum(dz)

    slab = lambda j: pl.BlockSpec((None, bt, N_FLOG), lambda i: (j, nb - 1 - i, 0))
    return pl.pallas_call(
        body, name="forget_bwd", grid=(nb,),
        in_specs=[slab(0), slab(1), slab(2), slab(3),
                  pl.BlockSpec((bt, N_FLOG), lambda i: (nb - 1 - i, 0)),
                  pl.BlockSpec((bt, N_FLOG), lambda i: (nb - 1 - i, col)), _resident((1, N_FLOG))],
        out_specs=[pl.BlockSpec((bt, N_FLOG), lambda i: (nb - 1 - i, 0)), _resident((1, N_FLOG))],
        out_shape=[jax.ShapeDtypeStruct((T, N_FLOG), BF16), jax.ShapeDtypeStruct((1, N_FLOG), F32)],
        scratch_shapes=[pltpu.VMEM((1, N_FLOG), F32)],
        compiler_params=_cp(1),
    )(dck, dck, dck, dck, dcq, flog, bf)


def _head_masks():
    lane = lax.broadcasted_iota(jnp.int32, (1, LANE), 1)
    return lane < HEAD_DIM


def _split_heads(x2, is_a):
    zero = jnp.zeros_like(x2)
    return jnp.where(is_a, x2, zero), jnp.where(is_a, zero, x2)


BIAS_PARTS = 3


def _bias_lanes(h):
    lane = lax.broadcasted_iota(jnp.int32, (1, LANE), 1)
    first = (1 - h) * HEAD_DIM
    return lane, first


def _fold_key_bias(qkv, c):
    T = qkv.shape[0]
    tm = _tile(T, 512)
    npair = N_HEADS // 2

    def body(k_ref, c_ref, o_ref):
        cc = c_ref[...]
        parts, rest = [], cc
        for _ in range(BIAS_PARTS):
            piece = rest.astype(BF16)
            parts.append(piece)
            rest = rest - piece.astype(F32)
        for j in range(npair):
            k2 = k_ref[:, j * LANE:(j + 1) * LANE]
            for h in range(2):
                lane, first = _bias_lanes(h)
                out = k2
                for n, piece in enumerate(parts):
                    col = piece[:, 2 * j + h:2 * j + h + 1]
                    out = jnp.where(lane == first + n, col, out)
                o_ref[:, (2 * j + h) * LANE:(2 * j + h + 1) * LANE] = out

    return pl.pallas_call(
        body, name="fold_key_bias", grid=(T // tm,),
        in_specs=[pl.BlockSpec((tm, D_ATTN), lambda i: (i, 1)), pl.BlockSpec((tm, N_FLOG), lambda i: (i, 0))],
        out_specs=pl.BlockSpec((tm, 2 * D_ATTN), lambda i: (i, 0)),
        out_shape=jax.ShapeDtypeStruct((T, 2 * D_ATTN), BF16),
        compiler_params=_cp(1),
    )(qkv, c)


def _attn_fwd(qkv, vt, kb):
    T = qkv.shape[0]
    tq = _tile(T, 512)
    tk = tq
    nq = T // tq
    npair = N_HEADS // 2

    def body(q_ref, ka_ref, kb_ref, vt_ref, o_ref, al_ref, m_s, l_s, acc_s):
        i = pl.program_id(1)
        qs = []
        for h, qh in enumerate(_split_heads(q_ref[...], _head_masks())):
            lane, first = _bias_lanes(h)
            qs.append(jnp.where((lane >= first) & (lane < first + BIAS_PARTS), -1.0, qh).astype(BF16))
        k_refs = (ka_ref, kb_ref)
        m_s[...] = jnp.full_like(m_s, NEG_INF)
        l_s[...] = jnp.zeros_like(l_s)
        acc_s[...] = jnp.zeros_like(acc_s)

        def scores_at(kk):
            k0 = pl.multiple_of(kk * tk, tk)
            return tuple(_nt(k_refs[h][pl.ds(k0, tk), :], qs[h]) for h in range(2))

        def consume(kk, scores, masked):
            k0 = pl.multiple_of(kk * tk, tk)
            v2t = vt_ref[:, pl.ds(k0, tk)]
            for h in range(2):
                zt = scores[h]
                if masked:
                    rr = lax.broadcasted_iota(jnp.int32, (tk, tq), 0)
                    cc = lax.broadcasted_iota(jnp.int32, (tk, tq), 1)
                    zt = jnp.where(cc >= rr, zt, NEG_INF)
                m_old = m_s[h]
                m_new = jnp.maximum(m_old, jnp.max(zt, axis=0, keepdims=True))
                p = jnp.exp2(zt - m_new)
                a = jnp.exp2(m_old - m_new)
                l_s[h] = a * l_s[h] + jnp.sum(p, axis=0, keepdims=True)
                acc_s[h] = a * acc_s[h] + _nn(v2t, p.astype(BF16))
                m_s[h] = m_new

        def two_tiles(kk, second_masked):
            first, second = scores_at(kk), scores_at(kk + 1)
            consume(kk, first, False)
            consume(kk + 1, second, second_masked)

        def loop_body(t, carry):
            two_tiles(2 * t, False)
            return carry

        lax.fori_loop(0, i // 2, loop_body, 0)

        @pl.when(i % 2 == 1)
        def _():
            two_tiles(i - 1, True)

        @pl.when(i % 2 == 0)
        def _():
            consume(i, scores_at(i), True)

        outs = []
        for h in range(2):
            l = l_s[h]
            outs.append(acc_s[h] * (1.0 / l))
            al_ref[0, h:h + 1, :] = -(m_s[h] + jnp.log2(l))
        al_ref[0, 2:8, :] = jnp.zeros((6, tq), F32)
        dim = lax.broadcasted_iota(jnp.int32, (LANE, 1), 0)
        o_ref[...] = jnp.where(dim < HEAD_DIM, outs[0], outs[1]).T

    rowl = pl.BlockSpec((1, 8, tq), lambda j, i: (j, 0, i))
    return pl.pallas_call(
        body, name="attn_fwd", grid=(npair, nq),
        in_specs=[pl.BlockSpec((tq, LANE), lambda j, i: (i, j)),
                  pl.BlockSpec((T, LANE), lambda j, i: (0, 2 * j), pipeline_mode=pl.Buffered(1)),
                  pl.BlockSpec((T, LANE), lambda j, i: (0, 2 * j + 1), pipeline_mode=pl.Buffered(1)),
                  pl.BlockSpec((LANE, T), lambda j, i: (2 * npair + j, 0), pipeline_mode=pl.Buffered(1))],
        out_specs=[pl.BlockSpec((tq, LANE), lambda j, i: (i, j)), rowl],
        out_shape=[jax.ShapeDtypeStruct((T, D_ATTN), F32), jax.ShapeDtypeStruct((npair, 8, T), F32)],
        scratch_shapes=[pltpu.VMEM((2, 1, tq), F32), pltpu.VMEM((2, 1, tq), F32), pltpu.VMEM((2, LANE, tq), F32)],
        compiler_params=_cp(2),
    )(qkv, kb, kb, vt)


def _attn_bwd(qkv, qkvt, dob, dobt, cb, alrow, dlrow):
    T = qkv.shape[0]
    tq = _tile(T, 512)
    tk = tq
    nq = T // tq
    npair = N_HEADS // 2

    def body(q_ref, qt_ref, k_ref, kt_ref, v_ref, do_ref, dot_ref, cb_ref, al_ref, dl_ref,
             dq_ref, dk_ref, dv_ref, dc_ref, dcq_ref, dk_s, dv_s, dc_s):
        kj = pl.program_id(1)
        is_a = _head_masks()
        ks = _split_heads(k_ref[...], is_a)
        vs = _split_heads(v_ref[...], is_a)
        dim_a = lax.broadcasted_iota(jnp.int32, (LANE, 1), 0) < HEAD_DIM
        kts = _split_heads(kt_ref[...], dim_a)
        head_lane = lax.broadcasted_iota(jnp.int32, (1, LANE), 1) - 2 * pl.program_id(0)
        cs = tuple(jnp.sum(jnp.where(head_lane == h, cb_ref[...], 0.0), axis=-1, keepdims=True) for h in range(2))

        @pl.when(kj == 0)
        def _():
            dq_ref[...] = jnp.zeros_like(dq_ref)
            dcq_ref[...] = jnp.zeros_like(dcq_ref)

        dk_s[...] = jnp.zeros_like(dk_s)
        dv_s[...] = jnp.zeros_like(dv_s)
        dc_s[...] = jnp.zeros_like(dc_s)

        def step(qi, masked):
            q0 = pl.multiple_of(qi * tq, tq)
            q2 = q_ref[pl.ds(q0, tq), :]
            do2 = do_ref[pl.ds(q0, tq), :]
            qt2 = qt_ref[:, pl.ds(q0, tq)]
            dot2 = dot_ref[:, pl.ds(q0, tq)]
            for h in range(2):
                alr = al_ref[0, h:h + 1, pl.ds(q0, tq)]
                dlr = dl_ref[0, h:h + 1, pl.ds(q0, tq)]
                zt = _nt(ks[h], q2) + (alr - cs[h])
                if masked:
                    rr = lax.broadcasted_iota(jnp.int32, (tk, tq), 0)
                    cc = lax.broadcasted_iota(jnp.int32, (tk, tq), 1)
                    zt = jnp.where(cc >= rr, zt, NEG_INF)
                pt = jnp.exp2(zt)
                dst = pt * (_nt(vs[h], do2) - dlr)
                pb = pt.astype(BF16)
                dsb = dst.astype(BF16)
                dv_s[h] += _nt(dot2, pb)
                dk_s[h] += _nt(qt2, dsb)
                dc_s[h] += jnp.sum(dst, axis=-1, keepdims=True)
                dcq_ref[0, h:h + 1, pl.ds(q0, tq)] += jnp.sum(dst, axis=0, keepdims=True)
                dq_ref[:, pl.ds(q0, tq)] += _nn(kts[h], dsb)

        step(kj, True)

        def loop_body(qi, carry):
            step(qi, False)
            return carry

        lax.fori_loop(kj + 1, nq, loop_body, 0)
        dk_ref[...] = (jnp.where(dim_a, dk_s[0], dk_s[1]) * (1.0 / LOG2E)).astype(BF16)
        dv_ref[...] = jnp.where(dim_a, dv_s[0], dv_s[1]).astype(BF16)
        lane = lax.broadcasted_iota(jnp.int32, (1, LANE), 1)
        head = 2 * pl.program_id(0)
        dc_ref[...] = jnp.where(lane == head, -dc_s[0], jnp.where(lane == head + 1, -dc_s[1], 0.0))

        @pl.when(kj == nq - 1)
        def _():
            dq_ref[...] = dq_ref[...] * Q_SCALE

    full = lambda col: pl.BlockSpec((T, LANE), lambda j, kj: (0, col(j)), pipeline_mode=pl.Buffered(1))
    fullt = lambda row: pl.BlockSpec((LANE, T), lambda j, kj: (row(j), 0), pipeline_mode=pl.Buffered(1))
    tile = lambda col: pl.BlockSpec((tk, LANE), lambda j, kj: (kj, col(j)))
    tilet = lambda row: pl.BlockSpec((LANE, tk), lambda j, kj: (row(j), kj))
    rowl = pl.BlockSpec((1, 8, T), lambda j, kj: (j, 0, 0))
    return pl.pallas_call(
        body, name="attn_bwd", grid=(npair, nq),
        in_specs=[full(lambda j: j), fullt(lambda j: j), tile(lambda j: npair + j), tilet(lambda j: npair + j),
                  tile(lambda j: 2 * npair + j), full(lambda j: j), fullt(lambda j: j), tile(lambda j: 0), rowl, rowl],
        out_specs=[pl.BlockSpec((LANE, T), lambda j, kj: (j, 0)), tilet(lambda j: j), tilet(lambda j: j),
                   pl.BlockSpec((None, tk, LANE), lambda j, kj: (j, kj, 0)), rowl],
        out_shape=[jax.ShapeDtypeStruct((D_ATTN, T), F32), jax.ShapeDtypeStruct((D_ATTN, T), BF16),
                   jax.ShapeDtypeStruct((D_ATTN, T), BF16), jax.ShapeDtypeStruct((npair, T, LANE), F32),
                   jax.ShapeDtypeStruct((npair, 8, T), F32)],
        scratch_shapes=[pltpu.VMEM((2, LANE, tk), F32), pltpu.VMEM((2, LANE, tk), F32), pltpu.VMEM((2, tk, 1), F32)],
        compiler_params=_cp(2),
    )(qkv, qkvt, qkv, qkvt, qkv, dob, dobt, cb, alrow, dlrow)


HALO = 8


def _shift_rows(cur, other, k, tm, down):
    row = lax.broadcasted_iota(jnp.int32, (tm, 1), 0)
    reps = tm // HALO
    if down:
        rolled = pltpu.roll(cur, k, 0)
        fill = jnp.tile(pltpu.roll(other, k, 0), (reps, 1))
        return jnp.where(row < k, fill, rolled)
    rolled = pltpu.roll(cur, tm - k, 0)
    fill = jnp.tile(pltpu.roll(other, HALO - k, 0), (reps, 1))
    return jnp.where(row >= tm - k, fill, rolled)


def _conv_fwd(c, hh, c_prev, hh_prev, w_ref, first, tm):
    u = c * hh
    u_prev = jnp.where(first, 0.0, c_prev * hh_prev)
    u1 = _shift_rows(u, u_prev, 1, tm, True)
    u2 = _shift_rows(u, u_prev, 2, tm, True)
    y = w_ref[0:1, :] * u2 + w_ref[1:2, :] * u1 + w_ref[2:3, :] * u
    return u, u1, u2, y


def _rms(x, g):
    rs = lax.rsqrt(jnp.mean(x * x, axis=-1, keepdims=True) + RMS_EPS)
    return x * rs * g, rs


def _mixer_tail_fwd(o, bchf, conv_w, g_attn, g_conv, w_mo, x1, lg, lb):
    T = o.shape[0]
    tm = _tile(T, 512)
    hb = tm // HALO

    def body(o_ref, b_ref, c_ref, h_ref, cp_ref, hp_ref, w_ref, ga_ref, gc_ref, wmo_ref, x1_ref, lg_ref, lb_ref,
             x2_ref, r2_ref, mg_ref):
        first = pl.program_id(0) == 0
        _, _, _, y = _conv_fwd(c_ref[...], h_ref[...], cp_ref[...], hp_ref[...], w_ref, first, tm)
        na, _ = _rms(o_ref[...], ga_ref[...])
        nc, _ = _rms(b_ref[...] * y, gc_ref[...])
        nab = na.astype(BF16)
        ncb = nc.astype(BF16)
        mg_ref[:, 0:D_ATTN] = nab
        mg_ref[:, D_ATTN:] = ncb
        r2 = ALPHA * x1_ref[...] + _nn(nab, wmo_ref[0:D_ATTN, :]) + _nn(ncb, wmo_ref[D_ATTN:, :])
        r2_ref[...] = r2
        xhat, _ = _ln_stats(r2)
        x2_ref[...] = xhat * lg_ref[...] + lb_ref[...]

    row = lambda n, col=0: pl.BlockSpec((tm, n), lambda i: (i, col))
    prev = lambda col: pl.BlockSpec((HALO, D_CONV), lambda i: (jnp.maximum(i * hb - 1, 0), col))
    return pl.pallas_call(
        body, name="mixer_tail_fwd", grid=(T // tm,),
        in_specs=[row(D_ATTN), row(D_CONV, 0), row(D_CONV, 1), row(D_CONV, 2), prev(1), prev(2),
                  _resident((3, D_CONV)), _resident((1, D_ATTN)), _resident((1, D_CONV)),
                  _resident((D_MODEL, D_MODEL)), row(D_MODEL), _resident((1, D_MODEL)), _resident((1, D_MODEL))],
        out_specs=[row(D_MODEL), row(D_MODEL), row(D_MODEL)],
        out_shape=[jax.ShapeDtypeStruct((T, D_MODEL), F32), jax.ShapeDtypeStruct((T, D_MODEL), F32),
                   jax.ShapeDtypeStruct((T, D_MODEL), BF16)],
        compiler_params=_cp(1),
    )(o, bchf, bchf, bchf, bchf, bchf, conv_w, g_attn, g_conv, w_mo, x1, lg, lb)


def _head_sum_rows():
    row = lax.broadcasted_iota(jnp.int32, (4 * 8, D_ATTN), 0)
    head = lax.broadcasted_iota(jnp.int32, (4 * 8, D_ATTN), 1) // HEAD_DIM
    return jnp.where((row % 8 < 2) & (2 * (row // 8) + row % 8 == head), 1.0, 0.0).astype(F32)


def _mixer_tail_bwd(dx2, r2, lg, w_mo, o, bchf, conv_w, g_attn, g_conv, swap=()):
    T = o.shape[0]
    tm = _tile(T, 256)
    hb = tm // HALO
    ns = len(swap)
    last = T // tm - 1

    def body(dx2_ref, r2_ref, lg_ref, wmo_ref, o_ref, b_ref, c_ref, h_ref, cp_ref, hp_ref, w_ref, ga_ref, gc_ref,
             *rest):
        comm_in = rest[:ns]
        dx1_ref, dr_ref, do_ref, dot_ref, dl_ref, dco_ref, dlg_ref, dlb_ref, dga_ref, dgc_ref = rest[ns:ns + 10]
        comm_out, sems = rest[ns + 10:2 * ns + 10], rest[2 * ns + 10:]
        i = pl.program_id(0)
        if ns:
            @pl.when(i == 0)
            def _():
                _swap_start(comm_in, comm_out, *sems)

            @pl.when(i == last)
            def _():
                _swap_finish(comm_in, comm_out, *sems)

        @pl.when(i == 0)
        def _():
            for ref in (dlg_ref, dlb_ref, dga_ref, dgc_ref):
                ref[...] = jnp.zeros_like(ref)

        dy = dx2_ref[...]
        xhat, rstd = _ln_stats(r2_ref[...])
        dr = _ln_bwd(dy, xhat, rstd, lg_ref[...])
        dlg_ref[...] += _rowsum(dy * xhat)
        dlb_ref[...] += _rowsum(dy)
        dx1_ref[...] = ALPHA * dr
        drb = dr.astype(BF16)
        dr_ref[...] = drb
        dna = _nt(drb, wmo_ref[0:D_ATTN, :])
        dnc = _nt(drb, wmo_ref[D_ATTN:, :])

        def rms_bwd(x, g, dn):
            rs = lax.rsqrt(jnp.mean(x * x, axis=-1, keepdims=True) + RMS_EPS)
            dng = dn * g
            dx = rs * dng - x * (rs * rs * rs) * jnp.mean(dng * x, axis=-1, keepdims=True)
            return dx, _rowsum(dn * x * rs)

        oo = o_ref[...]
        do, dga = rms_bwd(oo, ga_ref[...], dna)
        dga_ref[...] += dga
        do_ref[...] = do.astype(BF16)
        dot_ref[...] = do.T.astype(BF16)
        dl_ref[...] = lax.dot_general(_head_sum_rows(), do * oo, (((1,), (1,)), ((), ())),
                                      preferred_element_type=F32, precision=lax.Precision.HIGHEST)
        _, _, _, y = _conv_fwd(c_ref[...], h_ref[...], cp_ref[...], hp_ref[...], w_ref, i == 0, tm)
        dco, dgc = rms_bwd(b_ref[...] * y, gc_ref[...], dnc)
        dgc_ref[...] += dgc
        dco_ref[...] = dco

    row = lambda n, col=0: pl.BlockSpec((tm, n), lambda i: (i, col))
    prev = lambda col: pl.BlockSpec((HALO, D_CONV), lambda i: (jnp.maximum(i * hb - 1, 0), col))
    vec = lambda n: _resident((1, n))
    return pl.pallas_call(
        body, name="mixer_tail_bwd", grid=(T // tm,),
        in_specs=[row(D_MODEL), row(D_MODEL), vec(D_MODEL), _resident((D_MODEL, D_MODEL)), row(D_ATTN),
                  row(D_CONV, 0), row(D_CONV, 1), row(D_CONV, 2), prev(1), prev(2), _resident((3, D_CONV)),
                  vec(D_ATTN), vec(D_CONV)] + [ANY_SPEC] * ns,
        out_specs=[row(D_MODEL), row(D_MODEL), row(D_ATTN), pl.BlockSpec((D_ATTN, tm), lambda i: (0, i)),
                   pl.BlockSpec((4 * 8, tm), lambda i: (0, i)), row(D_CONV),
                   vec(D_MODEL), vec(D_MODEL), vec(D_ATTN), vec(D_CONV)] + [ANY_SPEC] * ns,
        out_shape=[jax.ShapeDtypeStruct((T, D_MODEL), F32), jax.ShapeDtypeStruct((T, D_MODEL), BF16),
                   jax.ShapeDtypeStruct((T, D_ATTN), BF16), jax.ShapeDtypeStruct((D_ATTN, T), BF16),
                   jax.ShapeDtypeStruct((4 * 8, T), F32),
                   jax.ShapeDtypeStruct((T, D_CONV), F32), jax.ShapeDtypeStruct((1, D_MODEL), F32),
                   jax.ShapeDtypeStruct((1, D_MODEL), F32), jax.ShapeDtypeStruct((1, D_ATTN), F32),
                   jax.ShapeDtypeStruct((1, D_CONV), F32)] + _swap_shapes(swap),
        scratch_shapes=_swap_sems(ns) if ns else [],
        compiler_params=_cp(1),
    )(dx2, r2, lg, w_mo, o, bchf, bchf, bchf, bchf, bchf, conv_w, g_attn, g_conv, *swap)


def _conv_bwd(dco, bchf, conv_w):
    T = dco.shape[0]
    tm = _tile(T, 512)
    hb = tm // HALO
    nt = T // tm

    def body(dco_ref, dcon_ref, b_ref, bn_ref, c_ref, h_ref, cp_ref, hp_ref, w_ref, dbch_ref, dw_ref):
        i = pl.program_id(0)

        @pl.when(i == 0)
        def _():
            dw_ref[...] = jnp.zeros_like(dw_ref)

        cc = c_ref[...]
        hh = h_ref[...]
        u, u1, u2, y = _conv_fwd(cc, hh, cp_ref[...], hp_ref[...], w_ref, i == 0, tm)
        dco = dco_ref[...]
        bb = b_ref[...]
        dyc = dco * bb
        dy_next = jnp.where(i == nt - 1, 0.0, dcon_ref[...] * bn_ref[...])
        d1 = _shift_rows(dyc, dy_next, 1, tm, False)
        d2 = _shift_rows(dyc, dy_next, 2, tm, False)
        du = w_ref[2:3, :] * dyc + w_ref[1:2, :] * d1 + w_ref[0:1, :] * d2
        dbch_ref[:, 0:D_CONV] = (dco * y).astype(BF16)
        dbch_ref[:, D_CONV:2 * D_CONV] = (du * hh).astype(BF16)
        dbch_ref[:, 2 * D_CONV:] = (du * cc).astype(BF16)
        dw_ref[0:1, :] += _rowsum(dyc * u2)
        dw_ref[1:2, :] += _rowsum(dyc * u1)
        dw_ref[2:3, :] += _rowsum(dyc * u)

    row = lambda n, col=0: pl.BlockSpec((tm, n), lambda i: (i, col))
    prev = lambda col: pl.BlockSpec((HALO, D_CONV), lambda i: (jnp.maximum(i * hb - 1, 0), col))
    nxt = lambda col: pl.BlockSpec((HALO, D_CONV), lambda i: (jnp.minimum((i + 1) * hb, T // HALO - 1), col))
    return pl.pallas_call(
        body, name="conv_bwd", grid=(nt,),
        in_specs=[row(D_CONV), nxt(0), row(D_CONV, 0), nxt(0), row(D_CONV, 1), row(D_CONV, 2), prev(1), prev(2),
                  _resident((3, D_CONV))],
        out_specs=[row(3 * D_CONV), _resident((8, D_CONV))],
        out_shape=[jax.ShapeDtypeStruct((T, 3 * D_CONV), BF16), jax.ShapeDtypeStruct((8, D_CONV), F32)],
        compiler_params=_cp(1),
    )(dco, dco, bchf, bchf, bchf, bchf, bchf, bchf, conv_w)


def _mixer_in_bwd(dx1a, dqt, dkt, dvt, dbch, dfl, w_qkvt, w_bch, w_f):
    T = dx1a.shape[0]
    tm = _tile(T, 512)

    def body(a_ref, dq_ref, dk_ref, dv_ref, db_ref, df_ref, wq_ref, wb_ref, wf_ref, o_ref):
        acc = a_ref[...] + _nt(db_ref[...], wb_ref[...]) + _nt(df_ref[...], wf_ref[...])
        for n, ref in enumerate((dq_ref, dk_ref, dv_ref)):
            acc = acc + _tn(ref[...].astype(BF16), wq_ref[n * D_ATTN:(n + 1) * D_ATTN, :])
        o_ref[...] = acc

    row = lambda n: pl.BlockSpec((tm, n), lambda i: (i, 0))
    col = pl.BlockSpec((D_ATTN, tm), lambda i: (0, i))
    return pl.pallas_call(
        body, name="mixer_in_bwd", grid=(T // tm,),
        in_specs=[row(D_MODEL), col, col, col, row(3 * D_CONV), row(N_FLOG),
                  _resident((3 * D_ATTN, D_MODEL)), _resident((D_MODEL, 3 * D_CONV)), _resident((D_MODEL, N_FLOG))],
        out_specs=row(D_MODEL),
        out_shape=jax.ShapeDtypeStruct((T, D_MODEL), F32),
        compiler_params=_cp(1),
    )(dx1a, dqt, dkt, dvt, dbch, dfl, w_qkvt, w_bch, w_f)


def _ple_loss(x3, p, tgt, w_g, w_p, b_g, lg, lb):
    T = x3.shape[0]
    tm = _tile(T, 512)

    def body(x_ref, p_ref, t_ref, wg_ref, wp_ref, bg_ref, lg_ref, lb_ref,
             dx_ref, de_ref, dz_ref, loss_ref, dlg_ref, dlb_ref, dbg_ref):
        @pl.when(pl.program_id(0) == 0)
        def _():
            for ref in (loss_ref, dlg_ref, dlb_ref, dbg_ref):
                ref[...] = jnp.zeros_like(ref)

        xf = x_ref[...]
        gate = _sigmoid(_nn(xf.astype(BF16), wg_ref[...]) + bg_ref[...])
        e = _nn(p_ref[...].astype(BF16), wp_ref[...])
        xhat, rstd = _ln_stats(ALPHA * xf + gate * e)
        err = xhat * lg_ref[...] + lb_ref[...] - t_ref[...]
        sq = jnp.sum(_rowsum(err * err), axis=-1, keepdims=True)
        loss_ref[...] += jnp.broadcast_to(sq * (0.5 / D_MODEL), loss_ref.shape)
        dy = err * (1.0 / D_MODEL)
        dr = _ln_bwd(dy, xhat, rstd, lg_ref[...])
        dlg_ref[...] += _rowsum(dy * xhat)
        dlb_ref[...] += _rowsum(dy)
        de_ref[...] = (dr * gate).astype(BF16)
        dz = dr * e * gate * (1.0 - gate)
        dbg_ref[...] += _rowsum(dz)
        dzb = dz.astype(BF16)
        dz_ref[...] = dzb
        dx_ref[...] = ALPHA * dr + _nt(dzb, wg_ref[...])

    row = lambda n: pl.BlockSpec((tm, n), lambda i: (i, 0))
    vec = lambda n: _resident((1, n))
    return pl.pallas_call(
        body, name="ple_loss", grid=(T // tm,),
        in_specs=[row(D_MODEL), row(PLE_DIM), row(D_MODEL), _resident((D_MODEL, D_MODEL)),
                  _resident((PLE_DIM, D_MODEL)), vec(D_MODEL), vec(D_MODEL), vec(D_MODEL)],
        out_specs=[row(D_MODEL), row(D_MODEL), row(D_MODEL), vec(LANE), vec(D_MODEL), vec(D_MODEL), vec(D_MODEL)],
        out_shape=[jax.ShapeDtypeStruct((T, D_MODEL), F32), jax.ShapeDtypeStruct((T, D_MODEL), BF16),
                   jax.ShapeDtypeStruct((T, D_MODEL), BF16), jax.ShapeDtypeStruct((1, LANE), F32),
                   jax.ShapeDtypeStruct((1, D_MODEL), F32), jax.ShapeDtypeStruct((1, D_MODEL), F32),
                   jax.ShapeDtypeStruct((1, D_MODEL), F32)],
        compiler_params=_cp(1),
    )(x3, p, tgt, w_g, w_p, b_g, lg, lb)


def _lane_layout(v8):
    return jnp.repeat(v8, HEAD_DIM, axis=1)


def _row_layout(v8):
    t = v8.shape[0]
    return jnp.pad(v8.T.reshape(N_HEADS // 2, 2, t), ((0, 0), (0, 6), (0, 0)))


def _from_lane_layout(vl):
    return vl[:, ::HEAD_DIM]


def _from_row_layout(vr):
    return vr[:, :2, :].reshape(N_HEADS, -1).T


def _local_step(x, p, tgt, w, overlap=None):
    bf = lambda a: a.astype(BF16)
    w1i, w1o = bf(w["ffn1_w_in"]), bf(w["ffn1_w_out"])
    first = _ffn_fwd(x, w1i, w1o, w["ln1_g"], w["ln1_b"], "ffn1_fwd", overlap["gather"] if overlap else ())
    x1, r1, g1, u1, h1 = first[:5]
    if overlap:
        w = {**w, **overlap["weights"](first[5:])}
    w2i, w2o = bf(w["ffn2_w_in"]), bf(w["ffn2_w_out"])
    wmi = w["w_mix_in"]
    o_f = 3 * D_ATTN
    o_b = o_f + N_HEADS
    w_qkv = bf(wmi[:, :o_f])
    w_f = bf(jnp.pad(wmi[:, o_f:o_b], ((0, 0), (0, N_FLOG - N_HEADS))))
    w_bch = bf(wmi[:, o_b:])
    w_bchf = jnp.concatenate([w_bch, w_f], axis=1)
    w_mo, w_g, w_p = bf(w["w_mix_out"]), bf(w["w_ple_gate"]), bf(w["w_ple"])
    b_f = jnp.pad(w["b_forget"], ((0, 0), (0, N_FLOG - N_HEADS)))

    q_scale = jnp.concatenate([jnp.full((1, D_ATTN), Q_SCALE * LOG2E, F32), jnp.ones((1, 2 * D_ATTN), F32)], axis=1)
    qkv, qkvt = _matmul_nn(x1, w_qkv, q_scale, BF16, "proj_qkv", also_transposed=True)
    bchf = _matmul_nn(x1, w_bchf, jnp.ones((1, 3 * D_CONV + N_FLOG), F32), F32, "proj_bchf")
    fcol = 3 * D_CONV // N_FLOG
    c = _forget_cumsum(bchf, fcol, b_f)
    o, alrow = _attn_fwd(qkv, qkvt, _fold_key_bias(qkv, c))
    x2, r2, merged = _mixer_tail_fwd(o, bchf, w["conv_w"], w["g_attn"], w["g_conv"], w_mo, x1, w["ln2_g"], w["ln2_b"])
    x3, r3, g2, u2, h2 = _ffn_fwd(x2, w2i, w2o, w["ln3_g"], w["ln3_b"], "ffn2_fwd")

    grads = {}
    dx3, de, dz, loss, grads["ln4_g"], grads["ln4_b"], grads["b_ple_gate"] = _ple_loss(
        x3, p, tgt, w_g, w_p, w["b_ple_gate"], w["ln4_g"], w["ln4_b"])
    grads["w_ple"] = _matmul_tn(p, de, "dw_ple")
    grads["w_ple_gate"] = _matmul_tn(x3, dz, "dw_ple_gate")

    dx2, dgu2, df2, grads["ln3_g"], grads["ln3_b"] = _ffn_bwd(dx3, r3, g2, u2, w2i, w2o, w["ln3_g"], "ffn2_bwd")
    grads["ffn2_w_in"] = _matmul_tn(x2, dgu2, "dw_ffn2_in")
    grads["ffn2_w_out"] = _matmul_tn(h2, df2, "dw_ffn2_out")

    to_swap = overlap["swap"](grads) if overlap else ()
    tail = _mixer_tail_bwd(dx2, r2, w["ln2_g"], w_mo, o, bchf, w["conv_w"], w["g_attn"], w["g_conv"], to_swap)
    (dx1a, dr2, dob, dobt, delta, dco, grads["ln2_g"], grads["ln2_b"], grads["g_attn"], grads["g_conv"]) = tail[:10]
    grads["w_mix_out"] = _matmul_tn(merged, dr2, "dw_mix_out")
    dbch, dcw = _conv_bwd(dco, bchf, w["conv_w"])
    grads["conv_w"] = dcw[:3]
    dqt, dkt, dvt, dck, dcq = _attn_bwd(qkv, qkvt, dob, dobt, c, alrow, delta.reshape(N_HEADS // 2, 8, -1))
    dcq_lanes = jnp.pad(_from_row_layout(dcq), ((0, 0), (0, N_FLOG - N_HEADS)))
    dfl, dbf = _forget_bwd(dck, dcq_lanes, bchf, fcol, b_f)
    grads["b_forget"] = dbf[:, :N_HEADS]
    dx1 = _mixer_in_bwd(dx1a, dqt, dkt, dvt, dbch, dfl, w_qkv.T, w_bch, w_f)
    grads["w_mix_in"] = jnp.concatenate(
        [_matmul_tokens(dqt, x1, "dw_q").T, _matmul_tokens(dkt, x1, "dw_k").T, _matmul_tokens(dvt, x1, "dw_v").T,
         _matmul_tn(x1, dfl, "dw_flog")[:, :N_HEADS], _matmul_tn(x1, dbch, "dw_bch")], axis=1)

    dx0, dgu1, df1, grads["ln1_g"], grads["ln1_b"] = _ffn_bwd(dx1, r1, g1, u1, w1i, w1o, w["ln1_g"], "ffn1_bwd")
    grads["ffn1_w_out"] = _matmul_tn(h1, df1, "dw_ffn1_out")
    if not overlap:
        grads["ffn1_w_in"] = _matmul_tn(x, dgu1, "dw_ffn1_in")
        return loss, dx0, grads
    sums = overlap["chip_sums"](grads, to_swap, tail[10:])
    grads["ffn1_w_in"], *received = _matmul_tn(x, dgu1, "dw_ffn1_in", exchange=sums)
    return loss, dx0, grads, sums, received


WEIGHTS = ["ffn1_w_in", "ffn1_w_out", "ln1_g", "ln1_b", "w_mix_in", "b_forget", "conv_w", "g_attn", "g_conv",
           "w_mix_out", "ln2_g", "ln2_b", "ffn2_w_in", "ffn2_w_out", "ln3_g", "ln3_b", "w_ple", "w_ple_gate",
           "b_ple_gate", "ln4_g", "ln4_b"]
LAYOUT = {
    "ffn1_w_in": ((D_MODEL, 2 * D_FF), 1), "ffn1_w_out": ((D_FF, D_MODEL), 0),
    "w_mix_in": ((D_MODEL, 3 * D_ATTN + N_HEADS + 3 * D_CONV), 1), "conv_w": ((3, D_CONV), 1),
    "w_mix_out": ((D_MODEL, D_MODEL), 0), "ffn2_w_in": ((D_MODEL, 2 * D_FF), 1), "ffn2_w_out": ((D_FF, D_MODEL), 0),
    "w_ple": ((PLE_DIM, D_MODEL), 1), "w_ple_gate": ((D_MODEL, D_MODEL), 0),
    "ln1_g": ((1, D_MODEL), None), "ln1_b": ((1, D_MODEL), None), "b_forget": ((1, N_HEADS), None),
    "g_attn": ((1, D_ATTN), None), "g_conv": ((1, D_CONV), None), "ln2_g": ((1, D_MODEL), None),
    "ln2_b": ((1, D_MODEL), None), "ln3_g": ((1, D_MODEL), None), "ln3_b": ((1, D_MODEL), None),
    "b_ple_gate": ((1, D_MODEL), None), "ln4_g": ((1, D_MODEL), None), "ln4_b": ((1, D_MODEL), None),
}
BIG = [n for n in WEIGHTS if LAYOUT[n][1] is not None and n != "conv_w"]
SMALL = [n for n in WEIGHTS if n not in BIG]
ROW = 1024
SMALL_ROWS = 16


def _shard_shape(name):
    shape, axis = LAYOUT[name]
    if axis is None:
        return shape
    return tuple(s // N_CHIPS if a == axis else s for a, s in enumerate(shape))


def _halves(a):
    return a.reshape(a.shape[:-2] + (2, a.shape[-2] // 2, a.shape[-1]))


def _split_chips(name, full):
    shape, axis = LAYOUT[name]
    if axis == 0:
        return full.reshape((N_CHIPS, shape[0] // N_CHIPS) + shape[1:])
    return jnp.moveaxis(full.reshape(shape[:1] + (N_CHIPS, shape[1] // N_CHIPS)), 1, 0)


def _join_chips(name, parts):
    shape, axis = LAYOUT[name]
    if axis == 0:
        return parts.reshape(shape)
    return jnp.moveaxis(parts, 0, 1).reshape(shape)


SMALL_AT = {"ln1_g": (0, 0), "ln1_b": (1, 0), "ln2_g": (2, 0), "ln2_b": (3, 0), "ln3_g": (4, 0), "ln3_b": (5, 0),
            "b_ple_gate": (6, 0), "ln4_g": (7, 0), "ln4_b": (8, 0), "g_attn": (9, 0), "g_conv": (9, D_ATTN),
            "b_forget": (10, 0), "conv_w": (10, LANE)}
CONV_SHARD = D_CONV // N_CHIPS


def _pack_small_grads(grads):
    def body(*refs):
        ins, o_ref = dict(zip(SMALL, refs[:-1])), refs[-1]
        o_ref[...] = jnp.zeros_like(o_ref)
        for s in range(N_CHIPS):
            for n in SMALL:
                r, c0 = SMALL_AT[n]
                if n == "conv_w":
                    for k in range(3):
                        o_ref[s, r:r + 1, c0 + k * CONV_SHARD:c0 + (k + 1) * CONV_SHARD] = (
                            ins[n][k:k + 1, s * CONV_SHARD:(s + 1) * CONV_SHARD])
                else:
                    o_ref[s, r:r + 1, c0:c0 + ins[n].shape[1]] = ins[n][...]

    return pl.pallas_call(
        body, name="pack_small_grads",
        out_shape=jax.ShapeDtypeStruct((N_CHIPS, SMALL_ROWS, ROW), F32),
    )(*[grads[n] for n in SMALL])


def _adamw_math(w, g, m, v):
    c1 = 1.0 - ADAM_B1 ** ADAM_STEP
    c2 = 1.0 - ADAM_B2 ** ADAM_STEP
    m = ADAM_B1 * m + (1.0 - ADAM_B1) * g
    v = ADAM_B2 * v + (1.0 - ADAM_B2) * (g * g)
    return -ADAM_LR * ((m / c1) / (jnp.sqrt(v / c2) + ADAM_EPS) + ADAM_WD * w), m, v


def _adamw_small(g_mine, g_sib, c_idx, w, m, v):
    ns = len(SMALL)

    def body(c_ref, gm_ref, gs_ref, *refs):
        ws, ms, vs = refs[:ns], refs[ns:2 * ns], refs[2 * ns:3 * ns]
        outs = refs[3 * ns:]
        mine_first = c_ref[0] == 0
        top = jnp.where(mine_first, gm_ref[...], gs_ref[...])
        bot = jnp.where(mine_first, gs_ref[...], gm_ref[...])
        for i, n in enumerate(SMALL):
            r, c0 = SMALL_AT[n]
            blk, rr = (top, r) if r < SMALL_ROWS // 2 else (bot, r - SMALL_ROWS // 2)
            rows, width = ws[i].shape
            for k in range(rows):
                g = blk[rr:rr + 1, c0 + k * width:c0 + (k + 1) * width]
                d, mn, vn = _adamw_math(ws[i][k:k + 1, :], g, ms[i][k:k + 1, :], vs[i][k:k + 1, :])
                for q, val in enumerate((g, d, mn, vn)):
                    outs[q * ns + i][k:k + 1, :] = val

    shapes = [jax.ShapeDtypeStruct(a.shape, F32) for a in w]
    vmem = pl.BlockSpec(memory_space=pltpu.VMEM)
    res = pl.pallas_call(
        body, name="adamw_small",
        in_specs=[pl.BlockSpec(memory_space=pltpu.SMEM)] + [vmem] * (2 + 3 * ns),
        out_specs=[vmem] * (4 * ns),
        out_shape=shapes * 4,
    )(c_idx, g_mine, g_sib, *w, *m, *v)
    return [res[q * ns:(q + 1) * ns] for q in range(4)]


def _place():
    x, y, c = lax.axis_index("x"), lax.axis_index("y"), lax.axis_index("c")
    others = [(1 - x, y), (x, 1 - y), (1 - x, 1 - y)]
    return x, y, c, others


ANY_SPEC = pl.BlockSpec(memory_space=pl.ANY)


def _remote(src, dst, send_sems, recv_sems, k, to):
    return pltpu.make_async_remote_copy(src_ref=src, dst_ref=dst, send_sem=send_sems.at[k], recv_sem=recv_sems.at[k],
                                        device_id=to, device_id_type=MESH)


def _all_gather(shards):
    n = len(shards)

    def body(*refs):
        ins, outs, send_sems, recv_sems = refs[:n], refs[n:2 * n], refs[2 * n], refs[2 * n + 1]
        _gather_start(ins, outs, send_sems, recv_sems)
        _gather_finish(ins, outs, send_sems, recv_sems)

    return pl.pallas_call(
        body, name="all_gather_weights",
        out_shape=_gather_shapes(shards), in_specs=[ANY_SPEC] * n, out_specs=[ANY_SPEC] * n,
        scratch_shapes=_gather_sems(n),
    )(*shards)


def _gather_shapes(shards):
    return [jax.ShapeDtypeStruct((N_CHIPS,) + a.shape, a.dtype) for a in shards]


def _gather_sems(n):
    return [pltpu.SemaphoreType.DMA((6 * n,)), pltpu.SemaphoreType.DMA((6 * n,))]


def _gather_sends(ins, outs, send_sems, recv_sems):
    x, y, c, others = _place()
    s = 2 * x + y
    return [_remote(ins[t].at[c], outs[t].at[s, c], send_sems, recv_sems, 6 * t + j, (*chip, c))
            for t in range(len(ins)) for j, chip in enumerate(others)]


def _gather_start(ins, outs, send_sems, recv_sems):
    for cp in _gather_sends(ins, outs, send_sems, recv_sems):
        cp.start()


def _gather_finish(ins, outs, send_sems, recv_sems):
    x, y, c, others = _place()
    slot = lambda t, chip, half: outs[t].at[2 * chip[0] + chip[1], half]
    passed = []
    for t in range(len(ins)):
        for j, chip in enumerate(others):
            landed = slot(t, chip, c)
            _remote(landed, landed, send_sems, recv_sems, 6 * t + j, (x, y, c)).wait_recv()
            passed.append(_remote(landed, landed, send_sems, recv_sems, 6 * t + 3 + j, (x, y, 1 - c)))
            passed[-1].start()
    for t in range(len(ins)):
        for j, chip in enumerate(others):
            landed = slot(t, chip, 1 - c)
            _remote(landed, landed, send_sems, recv_sems, 6 * t + 3 + j, (x, y, c)).wait_recv()
    for cp in _gather_sends(ins, outs, send_sems, recv_sems) + passed:
        cp.wait_send()


def _swap_halves(gs, tag):
    n = len(gs)

    def body(*refs):
        ins, outs, send_sems, recv_sems = refs[:n], refs[n:2 * n], refs[2 * n], refs[2 * n + 1]
        _swap_start(ins, outs, send_sems, recv_sems)
        _swap_finish(ins, outs, send_sems, recv_sems)

    return pl.pallas_call(
        body, name="grad_swap_halves_" + tag,
        out_shape=_swap_shapes(gs), in_specs=[ANY_SPEC] * n, out_specs=[ANY_SPEC] * n,
        scratch_shapes=_swap_sems(n),
    )(*gs)


def _swap_shapes(gs):
    return [jax.ShapeDtypeStruct(g.shape[:1] + g.shape[2:], g.dtype) for g in gs]


def _swap_sems(n):
    return [pltpu.SemaphoreType.DMA((n,)), pltpu.SemaphoreType.DMA((n,))]


def _swap_copies(ins, outs, send_sems, recv_sems):
    x, y, c, _ = _place()
    return [_remote(ins[t].at[:, 1 - c], outs[t], send_sems, recv_sems, t, (x, y, 1 - c)) for t in range(len(ins))]


def _swap_start(ins, outs, send_sems, recv_sems):
    for cp in _swap_copies(ins, outs, send_sems, recv_sems):
        cp.start()


def _swap_finish(ins, outs, send_sems, recv_sems):
    for cp in _swap_copies(ins, outs, send_sems, recv_sems):
        cp.wait()


def _exchange_chips(pps):
    n = len(pps)

    def body(*refs):
        ins, outs, send_sems, recv_sems = refs[:n], refs[n:2 * n], refs[2 * n], refs[2 * n + 1]
        _exchange_start(ins, outs, send_sems, recv_sems)
        _exchange_finish(ins, outs, send_sems, recv_sems)

    return pl.pallas_call(
        body, name="grad_exchange_chips",
        out_shape=_exchange_shapes(pps), in_specs=[ANY_SPEC] * n, out_specs=[ANY_SPEC] * n,
        scratch_shapes=_exchange_sems(n),
    )(*pps)


def _exchange_shapes(pps):
    return [jax.ShapeDtypeStruct(p.shape, p.dtype) for p in pps]


def _exchange_sems(n):
    return [pltpu.SemaphoreType.DMA((3 * n,)), pltpu.SemaphoreType.DMA((3 * n,))]


def _exchange_sends(ins, outs, send_sems, recv_sems):
    x, y, c, others = _place()
    s = 2 * x + y
    return [_remote(ins[t].at[2 * chip[0] + chip[1]], outs[t].at[s], send_sems, recv_sems, 3 * t + j, (*chip, c))
            for t in range(len(ins)) for j, chip in enumerate(others)]


def _exchange_start(ins, outs, send_sems, recv_sems):
    for cp in _exchange_sends(ins, outs, send_sems, recv_sems):
        cp.start()


def _exchange_finish(ins, outs, send_sems, recv_sems):
    x, y, c, others = _place()
    for t in range(len(ins)):
        for j, chip in enumerate(others):
            landed = outs[t].at[2 * chip[0] + chip[1]]
            _remote(landed, landed, send_sems, recv_sems, 3 * t + j, (x, y, c)).wait_recv()
    for cp in _exchange_sends(ins, outs, send_sems, recv_sems):
        cp.wait_send()


def _share_half(rs):
    n = len(rs)

    def body(*refs):
        ins, outs, send_sems, recv_sems = refs[:n], refs[n:2 * n], refs[2 * n], refs[2 * n + 1]
        x, y, c, _ = _place()
        copies = [_remote(ins[t], outs[t], send_sems, recv_sems, t, (x, y, 1 - c)) for t in range(n)]
        for cp in copies:
            cp.start()
        for cp in copies:
            cp.wait()

    return pl.pallas_call(
        body, name="grad_share_half",
        out_shape=[jax.ShapeDtypeStruct(r.shape, r.dtype) for r in rs],
        in_specs=[ANY_SPEC] * n, out_specs=[ANY_SPEC] * n,
        scratch_shapes=[pltpu.SemaphoreType.DMA((n,)), pltpu.SemaphoreType.DMA((n,))],
    )(*rs)


ELEMENTWISE_BLOCK_BYTES = 1 << 20


def _row_tile(rows, cols):
    return _tile(rows, max(8, ELEMENTWISE_BLOCK_BYTES // (4 * cols) // 8 * 8))


def _add_my_half(g, sib, c_idx, name):
    rh, cols = g.shape[2:]
    tr = _row_tile(rh, cols)

    def body(c_ref, g_ref, s_ref, o_ref):
        o_ref[...] = (g_ref[...] + s_ref[...]).astype(BF16)

    return pl.pallas_call(
        body, name="grad_add_halves_" + name,
        grid_spec=pltpu.PrefetchScalarGridSpec(
            num_scalar_prefetch=1, grid=(N_CHIPS, rh // tr),
            in_specs=[pl.BlockSpec((None, None, tr, cols), lambda s, i, c: (s, c[0], i, 0)),
                      pl.BlockSpec((None, tr, cols), lambda s, i, c: (s, i, 0))],
            out_specs=pl.BlockSpec((None, tr, cols), lambda s, i, c: (s, i, 0))),
        out_shape=jax.ShapeDtypeStruct((N_CHIPS, rh, cols), BF16),
        compiler_params=_cp(2),
    )(c_idx, g, sib)


def _sum_chips(parts, pp, s_idx, name):
    rh, cols = parts.shape[1:]
    tr = _row_tile(rh, cols)

    def body(s_ref, p0, p1, p2, p3, mine_ref, o_ref):
        own = mine_ref[...]
        t = [jnp.where(s_ref[0] == k, own, p[...]).astype(F32) for k, p in enumerate((p0, p1, p2, p3))]
        o_ref[...] = ((t[0] + t[1]) + t[2]) + t[3]

    slot = lambda k: pl.BlockSpec((None, tr, cols), lambda i, s: (jnp.where(s[0] == k, (k + 1) % N_CHIPS, k), i, 0))
    return pl.pallas_call(
        body, name="grad_sum_chips_" + name,
        grid_spec=pltpu.PrefetchScalarGridSpec(
            num_scalar_prefetch=1, grid=(rh // tr,),
            in_specs=[slot(0), slot(1), slot(2), slot(3), pl.BlockSpec((None, tr, cols), lambda i, s: (s[0], i, 0))],
            out_specs=pl.BlockSpec((tr, cols), lambda i, s: (i, 0))),
        out_shape=jax.ShapeDtypeStruct((rh, cols), F32),
        compiler_params=_cp(1),
    )(s_idx, parts, parts, parts, parts, pp)


def _adamw(w, g_mine, g_sib, m, v, c_idx, name):
    rows, cols = w.shape
    tr = _row_tile(rows // 2, cols)
    nbh = rows // 2 // tr

    def body(c_ref, w_ref, gm_ref, gs_ref, m_ref, v_ref, g_ref, d_ref, mo_ref, vo_ref):
        g = jnp.where(pl.program_id(0) // nbh == c_ref[0], gm_ref[...], gs_ref[...])
        g_ref[...] = g
        d_ref[...], mo_ref[...], vo_ref[...] = _adamw_math(w_ref[...], g, m_ref[...], v_ref[...])

    spec = pl.BlockSpec((tr, cols), lambda i, c: (i, 0))
    half = pl.BlockSpec((tr, cols), lambda i, c: (i % nbh, 0))
    return pl.pallas_call(
        body, name="adamw_" + name,
        grid_spec=pltpu.PrefetchScalarGridSpec(
            num_scalar_prefetch=1, grid=(rows // tr,),
            in_specs=[spec, half, half, spec, spec], out_specs=[spec] * 4),
        out_shape=[jax.ShapeDtypeStruct(w.shape, F32)] * 4,
        compiler_params=_cp(1),
    )(c_idx, w, g_mine, g_sib, m, v)


def kernel(x, p, ffn1_w_in, ffn1_w_out, ln1_g, ln1_b, w_mix_in, b_forget, conv_w, g_attn, g_conv, w_mix_out, ln2_g, ln2_b, ffn2_w_in, ffn2_w_out, ln3_g, ln3_b, w_ple, w_ple_gate, b_ple_gate, ln4_g, ln4_b, loss_target, m_ffn1_w_in, m_ffn1_w_out, m_ln1_g, m_ln1_b, m_w_mix_in, m_b_forget, m_conv_w, m_g_attn, m_g_conv, m_w_mix_out, m_ln2_g, m_ln2_b, m_ffn2_w_in, m_ffn2_w_out, m_ln3_g, m_ln3_b, m_w_ple, m_w_ple_gate, m_b_ple_gate, m_ln4_g, m_ln4_b, v_ffn1_w_in, v_ffn1_w_out, v_ln1_g, v_ln1_b, v_w_mix_in, v_b_forget, v_conv_w, v_g_attn, v_g_conv, v_w_mix_out, v_ln2_g, v_ln2_b, v_ffn2_w_in, v_ffn2_w_out, v_ln3_g, v_ln3_b, v_w_ple, v_w_ple_gate, v_b_ple_gate, v_ln4_g, v_ln4_b):
    args = dict(locals())
    shard = {n: args[n][0] if LAYOUT[n][1] is not None else args[n] for n in WEIGHTS}
    m_shard = {n: args["m_" + n][0] if LAYOUT[n][1] is not None else args["m_" + n] for n in WEIGHTS}
    v_shard = {n: args["v_" + n][0] if LAYOUT[n][1] is not None else args["v_" + n] for n in WEIGHTS}
    c_idx = lax.axis_index("c").astype(jnp.int32).reshape(1)
    chip = (2 * lax.axis_index("x") + lax.axis_index("y")).astype(jnp.int32)

    conv_rows = SMALL_ROWS - shard["conv_w"].shape[0]
    mine = {n: _halves(shard[n].astype(BF16)) for n in BIG}
    mine["conv_w"] = _halves(jnp.pad(shard["conv_w"], ((0, conv_rows), (0, 0))))
    early_w = ["ffn1_w_in", "ffn1_w_out"]
    late_w = [n for n in BIG if n not in early_w] + ["conv_w"]

    def full_weights(names, gathered):
        out = {}
        for n, theirs in zip(names, gathered):
            g = lax.dynamic_update_slice(theirs, mine[n][None], (chip, 0, 0, 0))
            if n == "conv_w":
                out[n] = _join_chips(n, g.reshape(N_CHIPS, SMALL_ROWS, CONV_SHARD)[:, :3])
            else:
                out[n] = _join_chips(n, g.reshape((N_CHIPS,) + _shard_shape(n)))
        return out

    full = full_weights(early_w, _all_gather([mine[n] for n in early_w]))
    full.update({n: shard[n] for n in SMALL if n != "conv_w"})

    def per_chip(names, grads):
        return [_halves(_pack_small_grads(grads)) if n == "small" else _halves(_split_chips(n, grads[n]))
                for n in names]

    def add_halves(names, mine_, sibs):
        return [_add_my_half(g, sib, c_idx, n) for n, g, sib in zip(names, mine_, sibs)]

    def chip_sums(names, grads):
        mine_ = per_chip(names, grads)
        return add_halves(names, mine_, _swap_halves(mine_, names[0]))

    ready_a = ["ffn2_w_in", "ffn2_w_out", "w_ple", "w_ple_gate"]
    ready_b = ["w_mix_in", "w_mix_out"]
    early_g = ready_a + ready_b
    late_g = early_w + ["small"]
    loss_acc, grad_x, grads, early_sums, early_parts = _local_step(
        x[0], p[0, 0], loss_target[0], full,
        overlap={"gather": [mine[n] for n in late_w], "weights": lambda gathered: full_weights(late_w, gathered),
                 "swap": lambda grads: per_chip(ready_a, grads),
                 "chip_sums": lambda grads, swapped, received: (add_halves(ready_a, swapped, received)
                                                                + chip_sums(ready_b, grads))})
    loss = lax.psum(loss_acc[0, 0], ("x", "y", "c"))

    late_sums = chip_sums(late_g, grads)
    names = early_g + late_g
    sums = list(early_sums) + late_sums
    parts = list(early_parts) + list(_exchange_chips(late_sums))
    half_of = {n: _sum_chips(pt, own, chip.reshape(1), n) for n, pt, own in zip(names, parts, sums)}
    names = BIG + ["small"]
    my_half = [half_of[n] for n in names]
    sib_half = _share_half(my_half)

    out = {}
    for n, gm, gs in zip(BIG, my_half, sib_half):
        out[n] = [a[None] for a in _adamw(shard[n], gm, gs, m_shard[n], v_shard[n], c_idx, n)]
    small = _adamw_small(my_half[-1], sib_half[-1], c_idx, [shard[n] for n in SMALL], [m_shard[n] for n in SMALL],
                         [v_shard[n] for n in SMALL])
    for i, n in enumerate(SMALL):
        out[n] = [small[q][i][None] if n == "conv_w" else small[q][i] for q in range(4)]
    return (loss, grad_x[None], *[out[n][q] for q in range(4) for n in WEIGHTS])
```

```python
import functools
import math

import jax
import jax.numpy as jnp
from jax import lax
from jax.experimental import pallas as pl
from jax.experimental.pallas import tpu as pltpu

F32 = jnp.float32
BF16 = jnp.bfloat16

D_MODEL = 1024
D_FF = 2816
N_HEADS = 8
HEAD_DIM = 64
D_ATTN = N_HEADS * HEAD_DIM
D_CONV = 512
PLE_DIM = 256
N_FLOG = 128
ALPHA = 2.0 ** 0.25
LN_EPS = 1e-5
RMS_EPS = 1e-6
NEG_INF = -1e30
Q_SCALE = 1.0 / math.sqrt(HEAD_DIM)
LOG2E = math.log2(math.e)

ADAM_LR = 0.001
ADAM_B1 = 0.9
ADAM_B2 = 0.999
ADAM_EPS = 1e-08
ADAM_WD = 0.01
ADAM_STEP = 10

V7X_VMEM_BYTES = 64 << 20
VMEM_LIMIT = V7X_VMEM_BYTES - (8 << 20)
LANE = 128
FF_CHUNK = 256
N_CHIPS = 4
MESH = pl.DeviceIdType.MESH


def _cp(n_axes):
    return pltpu.CompilerParams(dimension_semantics=("arbitrary",) * n_axes, vmem_limit_bytes=VMEM_LIMIT)


def _resident(shape):
    n = len(shape)
    return pl.BlockSpec(shape, lambda *_: (0,) * n, pipeline_mode=pl.Buffered(1))


def _nn(a, b):
    return jnp.dot(a, b, preferred_element_type=F32)


def _nt(a, b):
    return lax.dot_general(a, b, (((1,), (1,)), ((), ())), preferred_element_type=F32)


def _tn(a, b):
    return lax.dot_general(a, b, (((0,), (0,)), ((), ())), preferred_element_type=F32)


def _ln_stats(r):
    mu = jnp.mean(r, axis=-1, keepdims=True)
    xc = r - mu
    var = jnp.mean(xc * xc, axis=-1, keepdims=True)
    rstd = lax.rsqrt(var + LN_EPS)
    return xc * rstd, rstd


def _ln_bwd(dy, xhat, rstd, g):
    dxh = dy * g
    m1 = jnp.mean(dxh, axis=-1, keepdims=True)
    m2 = jnp.mean(dxh * xhat, axis=-1, keepdims=True)
    return rstd * (dxh - m1 - xhat * m2)


def _sigmoid(z):
    return 1.0 / (1.0 + jnp.exp(-z))


def _rowsum(a):
    return jnp.sum(a, axis=0, keepdims=True)


def _tile(total, want):
    if total <= want:
        return total
    for t in range(want - want % 8, 0, -8):
        if total % t == 0:
            return t
    raise ValueError((total, want))


def _ffn_fwd(x, w_in, w_out, lg, lb, name, gather=()):
    T = x.shape[0]
    tm = _tile(T, 512)
    nf = D_FF // FF_CHUNK
    ng = len(gather)
    last = T // tm - 1

    def body(x_ref, wi_ref, wo_ref, lg_ref, lb_ref, *rest):
        comm_in, (xo_ref, r_ref, g_ref, u_ref, h_ref) = rest[:ng], rest[ng:ng + 5]
        comm_out, sems = rest[ng + 5:2 * ng + 5], rest[2 * ng + 5:]
        if ng:
            @pl.when(pl.program_id(0) == 0)
            def _():
                _gather_start(comm_in, comm_out, *sems)

        xf = x_ref[...]
        xb = xf.astype(BF16)
        acc = jnp.zeros((tm, D_MODEL), F32)
        for j in range(nf):
            c0 = j * FF_CHUNK
            g = _nn(xb, wi_ref[:, c0:c0 + FF_CHUNK])
            u = _nn(xb, wi_ref[:, D_FF + c0:D_FF + c0 + FF_CHUNK])
            hb = (g * _sigmoid(g) * u).astype(BF16)
            g_ref[:, c0:c0 + FF_CHUNK] = g.astype(BF16)
            u_ref[:, c0:c0 + FF_CHUNK] = u.astype(BF16)
            h_ref[:, c0:c0 + FF_CHUNK] = hb
            acc = acc + _nn(hb, wo_ref[c0:c0 + FF_CHUNK, :])
        r = ALPHA * xf + 0.5 * acc
        r_ref[...] = r
        xhat, _ = _ln_stats(r)
        xo_ref[...] = xhat * lg_ref[...] + lb_ref[...]
        if ng:
            @pl.when(pl.program_id(0) == last)
            def _():
                _gather_finish(comm_in, comm_out, *sems)

    row = lambda n: pl.BlockSpec((tm, n), lambda i: (i, 0))
    return pl.pallas_call(
        body, name=name, grid=(T // tm,),
        in_specs=[row(D_MODEL), _resident((D_MODEL, 2 * D_FF)), _resident((D_FF, D_MODEL)),
                  _resident((1, D_MODEL)), _resident((1, D_MODEL))] + [ANY_SPEC] * ng,
        out_specs=[row(D_MODEL), row(D_MODEL), row(D_FF), row(D_FF), row(D_FF)] + [ANY_SPEC] * ng,
        out_shape=[jax.ShapeDtypeStruct((T, D_MODEL), F32), jax.ShapeDtypeStruct((T, D_MODEL), F32),
                   jax.ShapeDtypeStruct((T, D_FF), BF16), jax.ShapeDtypeStruct((T, D_FF), BF16),
                   jax.ShapeDtypeStruct((T, D_FF), BF16)] + _gather_shapes(gather),
        scratch_shapes=_gather_sems(ng) if ng else [],
        compiler_params=_cp(1),
    )(x, w_in, w_out, lg, lb, *gather)


def _ffn_bwd(dxo, r, g, u, w_in, w_out, lg, name, exchange=()):
    T = r.shape[0]
    tm = _tile(T, 256)
    nf = D_FF // FF_CHUNK
    ne = len(exchange)
    last = T // tm - 1

    def body(dxo_ref, r_ref, g_ref, u_ref, wi_ref, wo_ref, lg_ref, *rest):
        comm_in, (dx_ref, dgu_ref, df_ref, dlg_ref, dlb_ref) = rest[:ne], rest[ne:ne + 5]
        comm_out, sems = rest[ne + 5:2 * ne + 5], rest[2 * ne + 5:]
        i = pl.program_id(0)
        if ne:
            @pl.when(i == 0)
            def _():
                _exchange_start(comm_in, comm_out, *sems)

        dy = dxo_ref[...]
        xhat, rstd = _ln_stats(r_ref[...])
        dr = _ln_bwd(dy, xhat, rstd, lg_ref[...])

        @pl.when(i == 0)
        def _():
            dlg_ref[...] = jnp.zeros_like(dlg_ref)
            dlb_ref[...] = jnp.zeros_like(dlb_ref)

        dlg_ref[...] += _rowsum(dy * xhat)
        dlb_ref[...] += _rowsum(dy)
        dfb = (0.5 * dr).astype(BF16)
        df_ref[...] = dfb
        acc = jnp.zeros((tm, D_MODEL), F32)
        dh_ahead = _nt(dfb, wo_ref[0:FF_CHUNK, :])
        for j in range(nf):
            c0 = j * FF_CHUNK
            dh = dh_ahead
            if j + 1 < nf:
                dh_ahead = _nt(dfb, wo_ref[c0 + FF_CHUNK:c0 + 2 * FF_CHUNK, :])
            gg = g_ref[:, c0:c0 + FF_CHUNK].astype(F32)
            uu = u_ref[:, c0:c0 + FF_CHUNK].astype(F32)
            s = _sigmoid(gg)
            dgb = (dh * uu * s * (1.0 + gg * (1.0 - s))).astype(BF16)
            dub = (dh * gg * s).astype(BF16)
            dgu_ref[:, c0:c0 + FF_CHUNK] = dgb
            dgu_ref[:, D_FF + c0:D_FF + c0 + FF_CHUNK] = dub
            acc = acc + _nt(dgb, wi_ref[:, c0:c0 + FF_CHUNK]) + _nt(dub, wi_ref[:, D_FF + c0:D_FF + c0 + FF_CHUNK])
        dx_ref[...] = ALPHA * dr + acc
        if ne:
            @pl.when(i == last)
            def _():
                _exchange_finish(comm_in, comm_out, *sems)

    row = lambda n: pl.BlockSpec((tm, n), lambda i: (i, 0))
    return pl.pallas_call(
        body, name=name, grid=(T // tm,),
        in_specs=[row(D_MODEL), row(D_MODEL), row(D_FF), row(D_FF), _resident((D_MODEL, 2 * D_FF)),
                  _resident((D_FF, D_MODEL)), _resident((1, D_MODEL))] + [ANY_SPEC] * ne,
        out_specs=[row(D_MODEL), row(2 * D_FF), row(D_MODEL), _resident((1, D_MODEL)), _resident((1, D_MODEL))]
        + [ANY_SPEC] * ne,
        out_shape=[jax.ShapeDtypeStruct((T, D_MODEL), F32), jax.ShapeDtypeStruct((T, 2 * D_FF), BF16),
                   jax.ShapeDtypeStruct((T, D_MODEL), BF16), jax.ShapeDtypeStruct((1, D_MODEL), F32),
                   jax.ShapeDtypeStruct((1, D_MODEL), F32)] + _exchange_shapes(exchange),
        scratch_shapes=_exchange_sems(ne) if ne else [],
        compiler_params=_cp(1),
    )(dxo, r, g, u, w_in, w_out, lg, *exchange)


def _matmul_tn(a, b, name, by_chip=False, exchange=()):
    T, K = a.shape
    N = b.shape[1]
    tt = _tile(T, 1024)
    tn = N // N_CHIPS if by_chip else N
    while K * tn * 4 > (6 << 20) and tn % 256 == 0 and not by_chip:
        tn //= 2
    assert N % tn == 0
    ne = len(exchange)
    grid = (N // tn, T // tt)

    def body(a_ref, b_ref, *rest):
        comm_in, o_ref, comm_out, sems = rest[:ne], rest[ne], rest[ne + 1:2 * ne + 1], rest[2 * ne + 1:]
        n, t = pl.program_id(0), pl.program_id(1)
        if ne:
            @pl.when((n == 0) & (t == 0))
            def _():
                _exchange_start(comm_in, comm_out, *sems)

        @pl.when(t == 0)
        def _():
            o_ref[...] = jnp.zeros_like(o_ref)

        o_ref[...] += _tn(a_ref[...].astype(BF16), b_ref[...].astype(BF16))
        if ne:
            @pl.when((n == grid[0] - 1) & (t == grid[1] - 1))
            def _():
                _exchange_finish(comm_in, comm_out, *sems)

    res = pl.pallas_call(
        body, name=name, grid=grid,
        in_specs=[pl.BlockSpec((tt, K), lambda n, t: (t, 0)), pl.BlockSpec((tt, tn), lambda n, t: (t, n))]
        + [ANY_SPEC] * ne,
        out_specs=[pl.BlockSpec((None, K, tn), lambda n, t: (n, 0, 0)) if by_chip
                   else pl.BlockSpec((K, tn), lambda n, t: (0, n))] + [ANY_SPEC] * ne,
        out_shape=[jax.ShapeDtypeStruct((N_CHIPS, K, tn) if by_chip else (K, N), F32)] + _exchange_shapes(exchange),
        scratch_shapes=_exchange_sems(ne) if ne else [],
        compiler_params=_cp(2),
    )(a, b, *exchange)
    return res if ne else res[0]


def _matmul_tokens(at, b, name):
    M, T = at.shape
    N = b.shape[1]
    tt = _tile(T, 1024)

    def body(a_ref, b_ref, o_ref):
        @pl.when(pl.program_id(0) == 0)
        def _():
            o_ref[...] = jnp.zeros_like(o_ref)

        o_ref[...] += _nn(a_ref[...].astype(BF16), b_ref[...].astype(BF16))

    return pl.pallas_call(
        body, name=name, grid=(T // tt,),
        in_specs=[pl.BlockSpec((M, tt), lambda t: (0, t)), pl.BlockSpec((tt, N), lambda t: (t, 0))],
        out_specs=pl.BlockSpec((M, N), lambda t: (0, 0)),
        out_shape=jax.ShapeDtypeStruct((M, N), F32),
        compiler_params=_cp(1),
    )(at, b)


def _matmul_nn(x, w, scale, out_dtype, name, also_transposed=False):
    T, K = x.shape
    N = w.shape[1]
    tm = _tile(T, 512)

    def body(x_ref, w_ref, s_ref, o_ref, *ot_ref):
        res = _nn(x_ref[...].astype(BF16), w_ref[...]) * s_ref[...]
        o_ref[...] = res.astype(out_dtype)
        if also_transposed:
            ot_ref[0][...] = res.T.astype(out_dtype)

    res = pl.pallas_call(
        body, name=name, grid=(T // tm,),
        in_specs=[pl.BlockSpec((tm, K), lambda i: (i, 0)), _resident((K, N)), _resident((1, N))],
        out_specs=[pl.BlockSpec((tm, N), lambda i: (i, 0))] + [pl.BlockSpec((N, tm), lambda i: (0, i))] * also_transposed,
        out_shape=[jax.ShapeDtypeStruct((T, N), out_dtype)] + [jax.ShapeDtypeStruct((N, T), out_dtype)] * also_transposed,
        compiler_params=_cp(1),
    )(x, w, scale)
    return res if also_transposed else res[0]


def _log_sigmoid(z):
    return jnp.minimum(z, 0.0) - jnp.log1p(jnp.exp(-jnp.abs(z)))


def _tri(n, lower):
    r = lax.broadcasted_iota(jnp.int32, (n, n), 0)
    c = lax.broadcasted_iota(jnp.int32, (n, n), 1)
    return jnp.where((c <= r) if lower else (c >= r), 1.0, 0.0).astype(F32)


def _f32dot(a, b):
    return jnp.dot(a, b, preferred_element_type=F32, precision=lax.Precision.HIGHEST)


def _forget_cumsum(flog, col, bf):
    T = flog.shape[0]
    bt = _tile(T, 512)

    def body(f_ref, b_ref, c_ref, carry):
        @pl.when(pl.program_id(0) == 0)
        def _():
            carry[...] = jnp.zeros_like(carry)

        lf = _log_sigmoid(f_ref[...] + b_ref[...])
        c = _f32dot(_tri(bt, True), lf) + carry[...]
        c_ref[...] = c * LOG2E
        carry[...] = c[bt - 1:bt, :]

    return pl.pallas_call(
        body, name="forget_cumsum", grid=(T // bt,),
        in_specs=[pl.BlockSpec((bt, N_FLOG), lambda i: (i, col)), _resident((1, N_FLOG))],
        out_specs=pl.BlockSpec((bt, N_FLOG), lambda i: (i, 0)),
        out_shape=jax.ShapeDtypeStruct((T, N_FLOG), F32),
        scratch_shapes=[pltpu.VMEM((1, N_FLOG), F32)],
        compiler_params=_cp(1),
    )(flog, bf)


def _forget_bwd(dck, dcq, flog, col, bf):
    T = dcq.shape[0]
    bt = _tile(T, 512)
    nb = T // bt

    def body(k0_ref, k1_ref, k2_ref, k3_ref, dcq_ref, f_ref, b_ref, dz_ref, db_ref, carry):
        @pl.when(pl.program_id(0) == 0)
        def _():
            carry[...] = jnp.zeros_like(carry)
            db_ref[...] = jnp.zeros_like(db_ref)

        dc = ((k0_ref[...] + k1_ref[...]) + (k2_ref[...] + k3_ref[...])) + dcq_ref[...]
        dlf = _f32dot(_tri(bt, False), dc) + carry[...]
        carry[...] = dlf[0:1, :]
        z = f_ref[...] + b_ref[...]
        dz = dlf * _sigmoid(-z)
        dz_ref[...] = dz.astype(BF16)
        db_ref[...] += _rowsum(dz)

    slab = lambda j: pl.BlockSpec((None, bt, N_FLOG), lambda i: (j, nb - 1 - i, 0))
    return pl.pallas_call(
        body, name="forget_bwd", grid=(nb,),
        in_specs=[slab(0), slab(1), slab(2), slab(3),
                  pl.BlockSpec((bt, N_FLOG), lambda i: (nb - 1 - i, 0)),
                  pl.BlockSpec((bt, N_FLOG), lambda i: (nb - 1 - i, col)), _resident((1, N_FLOG))],
        out_specs=[pl.BlockSpec((bt, N_FLOG), lambda i: (nb - 1 - i, 0)), _resident((1, N_FLOG))],
        out_shape=[jax.ShapeDtypeStruct((T, N_FLOG), BF16), jax.ShapeDtypeStruct((1, N_FLOG), F32)],
        scratch_shapes=[pltpu.VMEM((1, N_FLOG), F32)],
        compiler_params=_cp(1),
    )(dck, dck, dck, dck, dcq, flog, bf)


def _head_masks():
    lane = lax.broadcasted_iota(jnp.int32, (1, LANE), 1)
    return lane < HEAD_DIM


def _split_heads(x2, is_a):
    zero = jnp.zeros_like(x2)
    return jnp.where(is_a, x2, zero), jnp.where(is_a, zero, x2)


BIAS_PARTS = 3


def _bias_lanes(h):
    lane = lax.broadcasted_iota(jnp.int32, (1, LANE), 1)
    first = (1 - h) * HEAD_DIM
    return lane, first


def _fold_key_bias(qkv, c):
    T = qkv.shape[0]
    tm = _tile(T, 512)
    npair = N_HEADS // 2

    def body(k_ref, c_ref, o_ref):
        cc = c_ref[...]
        parts, rest = [], cc
        for _ in range(BIAS_PARTS):
            piece = rest.astype(BF16)
            parts.append(piece)
            rest = rest - piece.astype(F32)
        for j in range(npair):
            k2 = k_ref[:, j * LANE:(j + 1) * LANE]
            for h in range(2):
                lane, first = _bias_lanes(h)
                out = k2
                for n, piece in enumerate(parts):
                    col = piece[:, 2 * j + h:2 * j + h + 1]
                    out = jnp.where(lane == first + n, col, out)
                o_ref[:, (2 * j + h) * LANE:(2 * j + h + 1) * LANE] = out

    return pl.pallas_call(
        body, name="fold_key_bias", grid=(T // tm,),
        in_specs=[pl.BlockSpec((tm, D_ATTN), lambda i: (i, 1)), pl.BlockSpec((tm, N_FLOG), lambda i: (i, 0))],
        out_specs=pl.BlockSpec((tm, 2 * D_ATTN), lambda i: (i, 0)),
        out_shape=jax.ShapeDtypeStruct((T, 2 * D_ATTN), BF16),
        compiler_params=_cp(1),
    )(qkv, c)


def _attn_fwd(qkv, vt, kb):
    T = qkv.shape[0]
    tq = _tile(T, 512)
    tk = tq
    nq = T // tq
    npair = N_HEADS // 2

    def body(q_ref, ka_ref, kb_ref, vt_ref, o_ref, al_ref, m_s, l_s, acc_s):
        i = pl.program_id(1)
        qs = []
        for h, qh in enumerate(_split_heads(q_ref[...], _head_masks())):
            lane, first = _bias_lanes(h)
            qs.append(jnp.where((lane >= first) & (lane < first + BIAS_PARTS), -1.0, qh).astype(BF16))
        k_refs = (ka_ref, kb_ref)
        m_s[...] = jnp.full_like(m_s, NEG_INF)
        l_s[...] = jnp.zeros_like(l_s)
        acc_s[...] = jnp.zeros_like(acc_s)

        def scores_at(kk):
            k0 = pl.multiple_of(kk * tk, tk)
            return tuple(_nt(k_refs[h][pl.ds(k0, tk), :], qs[h]) for h in range(2))

        def consume(kk, scores, masked):
            k0 = pl.multiple_of(kk * tk, tk)
            v2t = vt_ref[:, pl.ds(k0, tk)]
            for h in range(2):
                zt = scores[h]
                if masked:
                    rr = lax.broadcasted_iota(jnp.int32, (tk, tq), 0)
                    cc = lax.broadcasted_iota(jnp.int32, (tk, tq), 1)
                    zt = jnp.where(cc >= rr, zt, NEG_INF)
                m_old = m_s[h]
                m_new = jnp.maximum(m_old, jnp.max(zt, axis=0, keepdims=True))
                p = jnp.exp2(zt - m_new)
                a = jnp.exp2(m_old - m_new)
                l_s[h] = a * l_s[h] + jnp.sum(p, axis=0, keepdims=True)
                acc_s[h] = a * acc_s[h] + _nn(v2t, p.astype(BF16))
                m_s[h] = m_new

        def two_tiles(kk, second_masked):
            first, second = scores_at(kk), scores_at(kk + 1)
            consume(kk, first, False)
            consume(kk + 1, second, second_masked)

        def loop_body(t, carry):
            two_tiles(2 * t, False)
            return carry

        lax.fori_loop(0, i // 2, loop_body, 0)

        @pl.when(i % 2 == 1)
        def _():
            two_tiles(i - 1, True)

        @pl.when(i % 2 == 0)
        def _():
            consume(i, scores_at(i), True)

        outs = []
        for h in range(2):
            l = l_s[h]
            outs.append(acc_s[h] * (1.0 / l))
            al_ref[0, h:h + 1, :] = -(m_s[h] + jnp.log2(l))
        al_ref[0, 2:8, :] = jnp.zeros((6, tq), F32)
        dim = lax.broadcasted_iota(jnp.int32, (LANE, 1), 0)
        o_ref[...] = jnp.where(dim < HEAD_DIM, outs[0], outs[1]).T

    rowl = pl.BlockSpec((1, 8, tq), lambda j, i: (j, 0, i))
    return pl.pallas_call(
        body, name="attn_fwd", grid=(npair, nq),
        in_specs=[pl.BlockSpec((tq, LANE), lambda j, i: (i, j)),
                  pl.BlockSpec((T, LANE), lambda j, i: (0, 2 * j), pipeline_mode=pl.Buffered(1)),
                  pl.BlockSpec((T, LANE), lambda j, i: (0, 2 * j + 1), pipeline_mode=pl.Buffered(1)),
                  pl.BlockSpec((LANE, T), lambda j, i: (2 * npair + j, 0), pipeline_mode=pl.Buffered(1))],
        out_specs=[pl.BlockSpec((tq, LANE), lambda j, i: (i, j)), rowl],
        out_shape=[jax.ShapeDtypeStruct((T, D_ATTN), F32), jax.ShapeDtypeStruct((npair, 8, T), F32)],
        scratch_shapes=[pltpu.VMEM((2, 1, tq), F32), pltpu.VMEM((2, 1, tq), F32), pltpu.VMEM((2, LANE, tq), F32)],
        compiler_params=_cp(2),
    )(qkv, kb, kb, vt)


def _attn_bwd(qkv, qkvt, dob, dobt, cb, alrow, dlrow):
    T = qkv.shape[0]
    tq = _tile(T, 512)
    tk = tq
    nq = T // tq
    npair = N_HEADS // 2

    def body(q_ref, qt_ref, k_ref, kt_ref, v_ref, do_ref, dot_ref, cb_ref, al_ref, dl_ref,
             dq_ref, dk_ref, dv_ref, dc_ref, dcq_ref, dk_s, dv_s, dc_s):
        kj = pl.program_id(1)
        is_a = _head_masks()
        ks = _split_heads(k_ref[...], is_a)
        vs = _split_heads(v_ref[...], is_a)
        dim_a = lax.broadcasted_iota(jnp.int32, (LANE, 1), 0) < HEAD_DIM
        kts = _split_heads(kt_ref[...], dim_a)
        head_lane = lax.broadcasted_iota(jnp.int32, (1, LANE), 1) - 2 * pl.program_id(0)
        cs = tuple(jnp.sum(jnp.where(head_lane == h, cb_ref[...], 0.0), axis=-1, keepdims=True) for h in range(2))

        @pl.when(kj == 0)
        def _():
            dq_ref[...] = jnp.zeros_like(dq_ref)
            dcq_ref[...] = jnp.zeros_like(dcq_ref)

        dk_s[...] = jnp.zeros_like(dk_s)
        dv_s[...] = jnp.zeros_like(dv_s)
        dc_s[...] = jnp.zeros_like(dc_s)

        def step(qi, masked):
            q0 = pl.multiple_of(qi * tq, tq)
            q2 = q_ref[pl.ds(q0, tq), :]
            do2 = do_ref[pl.ds(q0, tq), :]
            qt2 = qt_ref[:, pl.ds(q0, tq)]
            dot2 = dot_ref[:, pl.ds(q0, tq)]
            for h in range(2):
                alr = al_ref[0, h:h + 1, pl.ds(q0, tq)]
                dlr = dl_ref[0, h:h + 1, pl.ds(q0, tq)]
                zt = _nt(ks[h], q2) + (alr - cs[h])
                if masked:
                    rr = lax.broadcasted_iota(jnp.int32, (tk, tq), 0)
                    cc = lax.broadcasted_iota(jnp.int32, (tk, tq), 1)
                    zt = jnp.where(cc >= rr, zt, NEG_INF)
                pt = jnp.exp2(zt)
                dst = pt * (_nt(vs[h], do2) - dlr)
                pb = pt.astype(BF16)
                dsb = dst.astype(BF16)
                dv_s[h] += _nt(dot2, pb)
                dk_s[h] += _nt(qt2, dsb)
                dc_s[h] += jnp.sum(dst, axis=-1, keepdims=True)
                dcq_ref[0, h:h + 1, pl.ds(q0, tq)] += jnp.sum(dst, axis=0, keepdims=True)
                dq_ref[:, pl.ds(q0, tq)] += _nn(kts[h], dsb)

        step(kj, True)

        def loop_body(qi, carry):
            step(qi, False)
            return carry

        lax.fori_loop(kj + 1, nq, loop_body, 0)
        dk_ref[...] = (jnp.where(dim_a, dk_s[0], dk_s[1]) * (1.0 / LOG2E)).astype(BF16)
        dv_ref[...] = jnp.where(dim_a, dv_s[0], dv_s[1]).astype(BF16)
        lane = lax.broadcasted_iota(jnp.int32, (1, LANE), 1)
        head = 2 * pl.program_id(0)
        dc_ref[...] = jnp.where(lane == head, -dc_s[0], jnp.where(lane == head + 1, -dc_s[1], 0.0))

        @pl.when(kj == nq - 1)
        def _():
            dq_ref[...] = dq_ref[...] * Q_SCALE

    full = lambda col: pl.BlockSpec((T, LANE), lambda j, kj: (0, col(j)), pipeline_mode=pl.Buffered(1))
    fullt = lambda row: pl.BlockSpec((LANE, T), lambda j, kj: (row(j), 0), pipeline_mode=pl.Buffered(1))
    tile = lambda col: pl.BlockSpec((tk, LANE), lambda j, kj: (kj, col(j)))
    tilet = lambda row: pl.BlockSpec((LANE, tk), lambda j, kj: (row(j), kj))
    rowl = pl.BlockSpec((1, 8, T), lambda j, kj: (j, 0, 0))
    return pl.pallas_call(
        body, name="attn_bwd", grid=(npair, nq),
        in_specs=[full(lambda j: j), fullt(lambda j: j), tile(lambda j: npair + j), tilet(lambda j: npair + j),
                  tile(lambda j: 2 * npair + j), full(lambda j: j), fullt(lambda j: j), tile(lambda j: 0), rowl, rowl],
        out_specs=[pl.BlockSpec((LANE, T), lambda j, kj: (j, 0)), tilet(lambda j: j), tilet(lambda j: j),
                   pl.BlockSpec((None, tk, LANE), lambda j, kj: (j, kj, 0)), rowl],
        out_shape=[jax.ShapeDtypeStruct((D_ATTN, T), F32), jax.ShapeDtypeStruct((D_ATTN, T), BF16),
                   jax.ShapeDtypeStruct((D_ATTN, T), BF16), jax.ShapeDtypeStruct((npair, T, LANE), F32),
                   jax.ShapeDtypeStruct((npair, 8, T), F32)],
        scratch_shapes=[pltpu.VMEM((2, LANE, tk), F32), pltpu.VMEM((2, LANE, tk), F32), pltpu.VMEM((2, tk, 1), F32)],
        compiler_params=_cp(2),
    )(qkv, qkvt, qkv, qkvt, qkv, dob, dobt, cb, alrow, dlrow)


HALO = 8


def _shift_rows(cur, other, k, tm, down):
    row = lax.broadcasted_iota(jnp.int32, (tm, 1), 0)
    reps = tm // HALO
    if down:
        rolled = pltpu.roll(cur, k, 0)
        fill = jnp.tile(pltpu.roll(other, k, 0), (reps, 1))
        return jnp.where(row < k, fill, rolled)
    rolled = pltpu.roll(cur, tm - k, 0)
    fill = jnp.tile(pltpu.roll(other, HALO - k, 0), (reps, 1))
    return jnp.where(row >= tm - k, fill, rolled)


def _conv_fwd(c, hh, c_prev, hh_prev, w_ref, first, tm):
    u = c * hh
    u_prev = jnp.where(first, 0.0, c_prev * hh_prev)
    u1 = _shift_rows(u, u_prev, 1, tm, True)
    u2 = _shift_rows(u, u_prev, 2, tm, True)
    y = w_ref[0:1, :] * u2 + w_ref[1:2, :] * u1 + w_ref[2:3, :] * u
    return u, u1, u2, y


def _rms(x, g):
    rs = lax.rsqrt(jnp.mean(x * x, axis=-1, keepdims=True) + RMS_EPS)
    return x * rs * g, rs


def _mixer_tail_fwd(o, bchf, conv_w, g_attn, g_conv, w_mo, x1, lg, lb):
    T = o.shape[0]
    tm = _tile(T, 512)
    hb = tm // HALO

    def body(o_ref, b_ref, c_ref, h_ref, cp_ref, hp_ref, w_ref, ga_ref, gc_ref, wmo_ref, x1_ref, lg_ref, lb_ref,
             x2_ref, r2_ref, mg_ref):
        first = pl.program_id(0) == 0
        _, _, _, y = _conv_fwd(c_ref[...], h_ref[...], cp_ref[...], hp_ref[...], w_ref, first, tm)
        na, _ = _rms(o_ref[...], ga_ref[...])
        nc, _ = _rms(b_ref[...] * y, gc_ref[...])
        nab = na.astype(BF16)
        ncb = nc.astype(BF16)
        mg_ref[:, 0:D_ATTN] = nab
        mg_ref[:, D_ATTN:] = ncb
        r2 = ALPHA * x1_ref[...] + _nn(nab, wmo_ref[0:D_ATTN, :]) + _nn(ncb, wmo_ref[D_ATTN:, :])
        r2_ref[...] = r2
        xhat, _ = _ln_stats(r2)
        x2_ref[...] = xhat * lg_ref[...] + lb_ref[...]

    row = lambda n, col=0: pl.BlockSpec((tm, n), lambda i: (i, col))
    prev = lambda col: pl.BlockSpec((HALO, D_CONV), lambda i: (jnp.maximum(i * hb - 1, 0), col))
    return pl.pallas_call(
        body, name="mixer_tail_fwd", grid=(T // tm,),
        in_specs=[row(D_ATTN), row(D_CONV, 0), row(D_CONV, 1), row(D_CONV, 2), prev(1), prev(2),
                  _resident((3, D_CONV)), _resident((1, D_ATTN)), _resident((1, D_CONV)),
                  _resident((D_MODEL, D_MODEL)), row(D_MODEL), _resident((1, D_MODEL)), _resident((1, D_MODEL))],
        out_specs=[row(D_MODEL), row(D_MODEL), row(D_MODEL)],
        out_shape=[jax.ShapeDtypeStruct((T, D_MODEL), F32), jax.ShapeDtypeStruct((T, D_MODEL), F32),
                   jax.ShapeDtypeStruct((T, D_MODEL), BF16)],
        compiler_params=_cp(1),
    )(o, bchf, bchf, bchf, bchf, bchf, conv_w, g_attn, g_conv, w_mo, x1, lg, lb)


def _head_sum_rows():
    row = lax.broadcasted_iota(jnp.int32, (4 * 8, D_ATTN), 0)
    head = lax.broadcasted_iota(jnp.int32, (4 * 8, D_ATTN), 1) // HEAD_DIM
    return jnp.where((row % 8 < 2) & (2 * (row // 8) + row % 8 == head), 1.0, 0.0).astype(F32)


def _mixer_tail_bwd(dx2, r2, lg, w_mo, o, bchf, conv_w, g_attn, g_conv, swap=()):
    T = o.shape[0]
    tm = _tile(T, 256)
    hb = tm // HALO
    ns = len(swap)
    last = T // tm - 1

    def body(dx2_ref, r2_ref, lg_ref, wmo_ref, o_ref, b_ref, c_ref, h_ref, cp_ref, hp_ref, w_ref, ga_ref, gc_ref,
             *rest):
        comm_in = rest[:ns]
        dx1_ref, dr_ref, do_ref, dot_ref, dl_ref, dco_ref, dlg_ref, dlb_ref, dga_ref, dgc_ref = rest[ns:ns + 10]
        comm_out, sems = rest[ns + 10:2 * ns + 10], rest[2 * ns + 10:]
        i = pl.program_id(0)
        if ns:
            @pl.when(i == 0)
            def _():
                _swap_start(comm_in, comm_out, *sems)

            @pl.when(i == last)
            def _():
                _swap_finish(comm_in, comm_out, *sems)

        @pl.when(i == 0)
        def _():
            for ref in (dlg_ref, dlb_ref, dga_ref, dgc_ref):
                ref[...] = jnp.zeros_like(ref)

        dy = dx2_ref[...]
        xhat, rstd = _ln_stats(r2_ref[...])
        dr = _ln_bwd(dy, xhat, rstd, lg_ref[...])
        dlg_ref[...] += _rowsum(dy * xhat)
        dlb_ref[...] += _rowsum(dy)
        dx1_ref[...] = ALPHA * dr
        drb = dr.astype(BF16)
        dr_ref[...] = drb
        dna = _nt(drb, wmo_ref[0:D_ATTN, :])
        dnc = _nt(drb, wmo_ref[D_ATTN:, :])

        def rms_bwd(x, g, dn):
            rs = lax.rsqrt(jnp.mean(x * x, axis=-1, keepdims=True) + RMS_EPS)
            dng = dn * g
            dx = rs * dng - x * (rs * rs * rs) * jnp.mean(dng * x, axis=-1, keepdims=True)
            return dx, _rowsum(dn * x * rs)

        oo = o_ref[...]
        do, dga = rms_bwd(oo, ga_ref[...], dna)
        dga_ref[...] += dga
        do_ref[...] = do.astype(BF16)
        dot_ref[...] = do.T.astype(BF16)
        dl_ref[...] = lax.dot_general(_head_sum_rows(), do * oo, (((1,), (1,)), ((), ())),
                                      preferred_element_type=F32, precision=lax.Precision.HIGHEST)
        _, _, _, y = _conv_fwd(c_ref[...], h_ref[...], cp_ref[...], hp_ref[...], w_ref, i == 0, tm)
        dco, dgc = rms_bwd(b_ref[...] * y, gc_ref[...], dnc)
        dgc_ref[...] += dgc
        dco_ref[...] = dco

    row = lambda n, col=0: pl.BlockSpec((tm, n), lambda i: (i, col))
    prev = lambda col: pl.BlockSpec((HALO, D_CONV), lambda i: (jnp.maximum(i * hb - 1, 0), col))
    vec = lambda n: _resident((1, n))
    return pl.pallas_call(
        body, name="mixer_tail_bwd", grid=(T // tm,),
        in_specs=[row(D_MODEL), row(D_MODEL), vec(D_MODEL), _resident((D_MODEL, D_MODEL)), row(D_ATTN),
                  row(D_CONV, 0), row(D_CONV, 1), row(D_CONV, 2), prev(1), prev(2), _resident((3, D_CONV)),
                  vec(D_ATTN), vec(D_CONV)] + [ANY_SPEC] * ns,
        out_specs=[row(D_MODEL), row(D_MODEL), row(D_ATTN), pl.BlockSpec((D_ATTN, tm), lambda i: (0, i)),
                   pl.BlockSpec((4 * 8, tm), lambda i: (0, i)), row(D_CONV),
                   vec(D_MODEL), vec(D_MODEL), vec(D_ATTN), vec(D_CONV)] + [ANY_SPEC] * ns,
        out_shape=[jax.ShapeDtypeStruct((T, D_MODEL), F32), jax.ShapeDtypeStruct((T, D_MODEL), BF16),
                   jax.ShapeDtypeStruct((T, D_ATTN), BF16), jax.ShapeDtypeStruct((D_ATTN, T), BF16),
                   jax.ShapeDtypeStruct((4 * 8, T), F32),
                   jax.ShapeDtypeStruct((T, D_CONV), F32), jax.ShapeDtypeStruct((1, D_MODEL), F32),
                   jax.ShapeDtypeStruct((1, D_MODEL), F32), jax.ShapeDtypeStruct((1, D_ATTN), F32),
                   jax.ShapeDtypeStruct((1, D_CONV), F32)] + _swap_shapes(swap),
        scratch_shapes=_swap_sems(ns) if ns else [],
        compiler_params=_cp(1),
    )(dx2, r2, lg, w_mo, o, bchf, bchf, bchf, bchf, bchf, conv_w, g_attn, g_conv, *swap)


def _conv_bwd(dco, bchf, conv_w):
    T = dco.shape[0]
    tm = _tile(T, 512)
    hb = tm // HALO
    nt = T // tm

    def body(dco_ref, dcon_ref, b_ref, bn_ref, c_ref, h_ref, cp_ref, hp_ref, w_ref, dbch_ref, dw_ref):
        i = pl.program_id(0)

        @pl.when(i == 0)
        def _():
            dw_ref[...] = jnp.zeros_like(dw_ref)

        cc = c_ref[...]
        hh = h_ref[...]
        u, u1, u2, y = _conv_fwd(cc, hh, cp_ref[...], hp_ref[...], w_ref, i == 0, tm)
        dco = dco_ref[...]
        bb = b_ref[...]
        dyc = dco * bb
        dy_next = jnp.where(i == nt - 1, 0.0, dcon_ref[...] * bn_ref[...])
        d1 = _shift_rows(dyc, dy_next, 1, tm, False)
        d2 = _shift_rows(dyc, dy_next, 2, tm, False)
        du = w_ref[2:3, :] * dyc + w_ref[1:2, :] * d1 + w_ref[0:1, :] * d2
        dbch_ref[:, 0:D_CONV] = (dco * y).astype(BF16)
        dbch_ref[:, D_CONV:2 * D_CONV] = (du * hh).astype(BF16)
        dbch_ref[:, 2 * D_CONV:] = (du * cc).astype(BF16)
        dw_ref[0:1, :] += _rowsum(dyc * u2)
        dw_ref[1:2, :] += _rowsum(dyc * u1)
        dw_ref[2:3, :] += _rowsum(dyc * u)

    row = lambda n, col=0: pl.BlockSpec((tm, n), lambda i: (i, col))
    prev = lambda col: pl.BlockSpec((HALO, D_CONV), lambda i: (jnp.maximum(i * hb - 1, 0), col))
    nxt = lambda col: pl.BlockSpec((HALO, D_CONV), lambda i: (jnp.minimum((i + 1) * hb, T // HALO - 1), col))
    return pl.pallas_call(
        body, name="conv_bwd", grid=(nt,),
        in_specs=[row(D_CONV), nxt(0), row(D_CONV, 0), nxt(0), row(D_CONV, 1), row(D_CONV, 2), prev(1), prev(2),
                  _resident((3, D_CONV))],
        out_specs=[row(3 * D_CONV), _resident((8, D_CONV))],
        out_shape=[jax.ShapeDtypeStruct((T, 3 * D_CONV), BF16), jax.ShapeDtypeStruct((8, D_CONV), F32)],
        compiler_params=_cp(1),
    )(dco, dco, bchf, bchf, bchf, bchf, bchf, bchf, conv_w)


def _mixer_in_bwd(dx1a, dqt, dkt, dvt, dbch, dfl, w_qkvt, w_bch, w_f):
    T = dx1a.shape[0]
    tm = _tile(T, 512)

    def body(a_ref, dq_ref, dk_ref, dv_ref, db_ref, df_ref, wq_ref, wb_ref, wf_ref, o_ref):
        acc = a_ref[...] + _nt(db_ref[...], wb_ref[...]) + _nt(df_ref[...], wf_ref[...])
        for n, ref in enumerate((dq_ref, dk_ref, dv_ref)):
            acc = acc + _tn(ref[...].astype(BF16), wq_ref[n * D_ATTN:(n + 1) * D_ATTN, :])
        o_ref[...] = acc

    row = lambda n: pl.BlockSpec((tm, n), lambda i: (i, 0))
    col = pl.BlockSpec((D_ATTN, tm), lambda i: (0, i))
    return pl.pallas_call(
        body, name="mixer_in_bwd", grid=(T // tm,),
        in_specs=[row(D_MODEL), col, col, col, row(3 * D_CONV), row(N_FLOG),
                  _resident((3 * D_ATTN, D_MODEL)), _resident((D_MODEL, 3 * D_CONV)), _resident((D_MODEL, N_FLOG))],
        out_specs=row(D_MODEL),
        out_shape=jax.ShapeDtypeStruct((T, D_MODEL), F32),
        compiler_params=_cp(1),
    )(dx1a, dqt, dkt, dvt, dbch, dfl, w_qkvt, w_bch, w_f)


def _ple_loss(x3, p, tgt, w_g, w_p, b_g, lg, lb):
    T = x3.shape[0]
    tm = _tile(T, 512)

    def body(x_ref, p_ref, t_ref, wg_ref, wp_ref, bg_ref, lg_ref, lb_ref,
             dx_ref, de_ref, dz_ref, loss_ref, dlg_ref, dlb_ref, dbg_ref):
        @pl.when(pl.program_id(0) == 0)
        def _():
            for ref in (loss_ref, dlg_ref, dlb_ref, dbg_ref):
                ref[...] = jnp.zeros_like(ref)

        xf = x_ref[...]
        gate = _sigmoid(_nn(xf.astype(BF16), wg_ref[...]) + bg_ref[...])
        e = _nn(p_ref[...].astype(BF16), wp_ref[...])
        xhat, rstd = _ln_stats(ALPHA * xf + gate * e)
        err = xhat * lg_ref[...] + lb_ref[...] - t_ref[...]
        sq = jnp.sum(_rowsum(err * err), axis=-1, keepdims=True)
        loss_ref[...] += jnp.broadcast_to(sq * (0.5 / D_MODEL), loss_ref.shape)
        dy = err * (1.0 / D_MODEL)
        dr = _ln_bwd(dy, xhat, rstd, lg_ref[...])
        dlg_ref[...] += _rowsum(dy * xhat)
        dlb_ref[...] += _rowsum(dy)
        de_ref[...] = (dr * gate).astype(BF16)
        dz = dr * e * gate * (1.0 - gate)
        dbg_ref[...] += _rowsum(dz)
        dzb = dz.astype(BF16)
        dz_ref[...] = dzb
        dx_ref[...] = ALPHA * dr + _nt(dzb, wg_ref[...])

    row = lambda n: pl.BlockSpec((tm, n), lambda i: (i, 0))
    vec = lambda n: _resident((1, n))
    return pl.pallas_call(
        body, name="ple_loss", grid=(T // tm,),
        in_specs=[row(D_MODEL), row(PLE_DIM), row(D_MODEL), _resident((D_MODEL, D_MODEL)),
                  _resident((PLE_DIM, D_MODEL)), vec(D_MODEL), vec(D_MODEL), vec(D_MODEL)],
        out_specs=[row(D_MODEL), row(D_MODEL), row(D_MODEL), vec(LANE), vec(D_MODEL), vec(D_MODEL), vec(D_MODEL)],
        out_shape=[jax.ShapeDtypeStruct((T, D_MODEL), F32), jax.ShapeDtypeStruct((T, D_MODEL), BF16),
                   jax.ShapeDtypeStruct((T, D_MODEL), BF16), jax.ShapeDtypeStruct((1, LANE), F32),
                   jax.ShapeDtypeStruct((1, D_MODEL), F32), jax.ShapeDtypeStruct((1, D_MODEL), F32),
                   jax.ShapeDtypeStruct((1, D_MODEL), F32)],
        compiler_params=_cp(1),
    )(x3, p, tgt, w_g, w_p, b_g, lg, lb)


def _lane_layout(v8):
    return jnp.repeat(v8, HEAD_DIM, axis=1)


def _row_layout(v8):
    t = v8.shape[0]
    return jnp.pad(v8.T.reshape(N_HEADS // 2, 2, t), ((0, 0), (0, 6), (0, 0)))


def _from_lane_layout(vl):
    return vl[:, ::HEAD_DIM]


def _from_row_layout(vr):
    return vr[:, :2, :].reshape(N_HEADS, -1).T


def _local_step(x, p, tgt, w, overlap=None):
    bf = lambda a: a.astype(BF16)
    w1i, w1o = bf(w["ffn1_w_in"]), bf(w["ffn1_w_out"])
    first = _ffn_fwd(x, w1i, w1o, w["ln1_g"], w["ln1_b"], "ffn1_fwd", overlap["gather"] if overlap else ())
    x1, r1, g1, u1, h1 = first[:5]
    if overlap:
        w = {**w, **overlap["weights"](first[5:])}
    w2i, w2o = bf(w["ffn2_w_in"]), bf(w["ffn2_w_out"])
    wmi = w["w_mix_in"]
    o_f = 3 * D_ATTN
    o_b = o_f + N_HEADS
    w_qkv = bf(wmi[:, :o_f])
    w_f = bf(jnp.pad(wmi[:, o_f:o_b], ((0, 0), (0, N_FLOG - N_HEADS))))
    w_bch = bf(wmi[:, o_b:])
    w_bchf = jnp.concatenate([w_bch, w_f], axis=1)
    w_mo, w_g, w_p = bf(w["w_mix_out"]), bf(w["w_ple_gate"]), bf(w["w_ple"])
    b_f = jnp.pad(w["b_forget"], ((0, 0), (0, N_FLOG - N_HEADS)))

    q_scale = jnp.concatenate([jnp.full((1, D_ATTN), Q_SCALE * LOG2E, F32), jnp.ones((1, 2 * D_ATTN), F32)], axis=1)
    qkv, qkvt = _matmul_nn(x1, w_qkv, q_scale, BF16, "proj_qkv", also_transposed=True)
    bchf = _matmul_nn(x1, w_bchf, jnp.ones((1, 3 * D_CONV + N_FLOG), F32), F32, "proj_bchf")
    fcol = 3 * D_CONV // N_FLOG
    c = _forget_cumsum(bchf, fcol, b_f)
    o, alrow = _attn_fwd(qkv, qkvt, _fold_key_bias(qkv, c))
    x2, r2, merged = _mixer_tail_fwd(o, bchf, w["conv_w"], w["g_attn"], w["g_conv"], w_mo, x1, w["ln2_g"], w["ln2_b"])
    x3, r3, g2, u2, h2 = _ffn_fwd(x2, w2i, w2o, w["ln3_g"], w["ln3_b"], "ffn2_fwd")

    grads = {}
    dx3, de, dz, loss, grads["ln4_g"], grads["ln4_b"], grads["b_ple_gate"] = _ple_loss(
        x3, p, tgt, w_g, w_p, w["b_ple_gate"], w["ln4_g"], w["ln4_b"])
    by_chip = overlap is not None
    grads["w_ple"] = _matmul_tn(p, de, "dw_ple", by_chip)
    grads["w_ple_gate"] = _matmul_tn(x3, dz, "dw_ple_gate")

    dx2, dgu2, df2, grads["ln3_g"], grads["ln3_b"] = _ffn_bwd(dx3, r3, g2, u2, w2i, w2o, w["ln3_g"], "ffn2_bwd")
    grads["ffn2_w_in"] = _matmul_tn(x2, dgu2, "dw_ffn2_in", by_chip)
    grads["ffn2_w_out"] = _matmul_tn(h2, df2, "dw_ffn2_out")

    to_swap = overlap["swap"](grads) if overlap else ()
    tail = _mixer_tail_bwd(dx2, r2, w["ln2_g"], w_mo, o, bchf, w["conv_w"], w["g_attn"], w["g_conv"], to_swap)
    (dx1a, dr2, dob, dobt, delta, dco, grads["ln2_g"], grads["ln2_b"], grads["g_attn"], grads["g_conv"]) = tail[:10]
    grads["w_mix_out"] = _matmul_tn(merged, dr2, "dw_mix_out")
    dbch, dcw = _conv_bwd(dco, bchf, w["conv_w"])
    grads["conv_w"] = dcw[:3]
    dqt, dkt, dvt, dck, dcq = _attn_bwd(qkv, qkvt, dob, dobt, c, alrow, delta.reshape(N_HEADS // 2, 8, -1))
    dcq_lanes = jnp.pad(_from_row_layout(dcq), ((0, 0), (0, N_FLOG - N_HEADS)))
    dfl, dbf = _forget_bwd(dck, dcq_lanes, bchf, fcol, b_f)
    grads["b_forget"] = dbf[:, :N_HEADS]
    dx1 = _mixer_in_bwd(dx1a, dqt, dkt, dvt, dbch, dfl, w_qkv.T, w_bch, w_f)
    grads["w_mix_in"] = jnp.concatenate(
        [_matmul_tokens(dqt, x1, "dw_q").T, _matmul_tokens(dkt, x1, "dw_k").T, _matmul_tokens(dvt, x1, "dw_v").T,
         _matmul_tn(x1, dfl, "dw_flog")[:, :N_HEADS], _matmul_tn(x1, dbch, "dw_bch")], axis=1)

    dx0, dgu1, df1, grads["ln1_g"], grads["ln1_b"] = _ffn_bwd(dx1, r1, g1, u1, w1i, w1o, w["ln1_g"], "ffn1_bwd")
    grads["ffn1_w_out"] = _matmul_tn(h1, df1, "dw_ffn1_out")
    if not overlap:
        grads["ffn1_w_in"] = _matmul_tn(x, dgu1, "dw_ffn1_in")
        return loss, dx0, grads
    sums = overlap["chip_sums"](grads, to_swap, tail[10:])
    grads["ffn1_w_in"], *received = _matmul_tn(x, dgu1, "dw_ffn1_in", by_chip, exchange=sums)
    return loss, dx0, grads, sums, received


WEIGHTS = ["ffn1_w_in", "ffn1_w_out", "ln1_g", "ln1_b", "w_mix_in", "b_forget", "conv_w", "g_attn", "g_conv",
           "w_mix_out", "ln2_g", "ln2_b", "ffn2_w_in", "ffn2_w_out", "ln3_g", "ln3_b", "w_ple", "w_ple_gate",
           "b_ple_gate", "ln4_g", "ln4_b"]
LAYOUT = {
    "ffn1_w_in": ((D_MODEL, 2 * D_FF), 1), "ffn1_w_out": ((D_FF, D_MODEL), 0),
    "w_mix_in": ((D_MODEL, 3 * D_ATTN + N_HEADS + 3 * D_CONV), 1), "conv_w": ((3, D_CONV), 1),
    "w_mix_out": ((D_MODEL, D_MODEL), 0), "ffn2_w_in": ((D_MODEL, 2 * D_FF), 1), "ffn2_w_out": ((D_FF, D_MODEL), 0),
    "w_ple": ((PLE_DIM, D_MODEL), 1), "w_ple_gate": ((D_MODEL, D_MODEL), 0),
    "ln1_g": ((1, D_MODEL), None), "ln1_b": ((1, D_MODEL), None), "b_forget": ((1, N_HEADS), None),
    "g_attn": ((1, D_ATTN), None), "g_conv": ((1, D_CONV), None), "ln2_g": ((1, D_MODEL), None),
    "ln2_b": ((1, D_MODEL), None), "ln3_g": ((1, D_MODEL), None), "ln3_b": ((1, D_MODEL), None),
    "b_ple_gate": ((1, D_MODEL), None), "ln4_g": ((1, D_MODEL), None), "ln4_b": ((1, D_MODEL), None),
}
BIG = [n for n in WEIGHTS if LAYOUT[n][1] is not None and n != "conv_w"]
SMALL = [n for n in WEIGHTS if n not in BIG]
ROW = 1024
SMALL_ROWS = 16


def _shard_shape(name):
    shape, axis = LAYOUT[name]
    if axis is None:
        return shape
    return tuple(s // N_CHIPS if a == axis else s for a, s in enumerate(shape))


def _halves(a):
    return a.reshape(a.shape[:-2] + (2, a.shape[-2] // 2, a.shape[-1]))


def _split_chips(name, full):
    shape, axis = LAYOUT[name]
    if axis == 0:
        return full.reshape((N_CHIPS, shape[0] // N_CHIPS) + shape[1:])
    return jnp.moveaxis(full.reshape(shape[:1] + (N_CHIPS, shape[1] // N_CHIPS)), 1, 0)


def _join_chips(name, parts):
    shape, axis = LAYOUT[name]
    if axis == 0:
        return parts.reshape(shape)
    return jnp.moveaxis(parts, 0, 1).reshape(shape)


SMALL_AT = {"ln1_g": (0, 0), "ln1_b": (1, 0), "ln2_g": (2, 0), "ln2_b": (3, 0), "ln3_g": (4, 0), "ln3_b": (5, 0),
            "b_ple_gate": (6, 0), "ln4_g": (7, 0), "ln4_b": (8, 0), "g_attn": (9, 0), "g_conv": (9, D_ATTN),
            "b_forget": (10, 0), "conv_w": (10, LANE)}
CONV_SHARD = D_CONV // N_CHIPS


def _pack_small_grads(grads):
    def body(*refs):
        ins, o_ref = dict(zip(SMALL, refs[:-1])), refs[-1]
        o_ref[...] = jnp.zeros_like(o_ref)
        for s in range(N_CHIPS):
            for n in SMALL:
                r, c0 = SMALL_AT[n]
                if n == "conv_w":
                    for k in range(3):
                        o_ref[s, r:r + 1, c0 + k * CONV_SHARD:c0 + (k + 1) * CONV_SHARD] = (
                            ins[n][k:k + 1, s * CONV_SHARD:(s + 1) * CONV_SHARD])
                else:
                    o_ref[s, r:r + 1, c0:c0 + ins[n].shape[1]] = ins[n][...]

    return pl.pallas_call(
        body, name="pack_small_grads",
        out_shape=jax.ShapeDtypeStruct((N_CHIPS, SMALL_ROWS, ROW), F32),
    )(*[grads[n] for n in SMALL])


def _adamw_math(w, g, m, v):
    c1 = 1.0 - ADAM_B1 ** ADAM_STEP
    c2 = 1.0 - ADAM_B2 ** ADAM_STEP
    m = ADAM_B1 * m + (1.0 - ADAM_B1) * g
    v = ADAM_B2 * v + (1.0 - ADAM_B2) * (g * g)
    return -ADAM_LR * ((m / c1) / (jnp.sqrt(v / c2) + ADAM_EPS) + ADAM_WD * w), m, v


def _adamw_small(g_mine, g_sib, c_idx, w, m, v):
    ns = len(SMALL)

    def body(c_ref, gm_ref, gs_ref, *refs):
        ws, ms, vs = refs[:ns], refs[ns:2 * ns], refs[2 * ns:3 * ns]
        outs = refs[3 * ns:]
        mine_first = c_ref[0] == 0
        top = jnp.where(mine_first, gm_ref[...], gs_ref[...])
        bot = jnp.where(mine_first, gs_ref[...], gm_ref[...])
        for i, n in enumerate(SMALL):
            r, c0 = SMALL_AT[n]
            blk, rr = (top, r) if r < SMALL_ROWS // 2 else (bot, r - SMALL_ROWS // 2)
            rows, width = ws[i].shape
            for k in range(rows):
                g = blk[rr:rr + 1, c0 + k * width:c0 + (k + 1) * width]
                d, mn, vn = _adamw_math(ws[i][k:k + 1, :], g, ms[i][k:k + 1, :], vs[i][k:k + 1, :])
                for q, val in enumerate((g, d, mn, vn)):
                    outs[q * ns + i][k:k + 1, :] = val

    shapes = [jax.ShapeDtypeStruct(a.shape, F32) for a in w]
    vmem = pl.BlockSpec(memory_space=pltpu.VMEM)
    res = pl.pallas_call(
        body, name="adamw_small",
        in_specs=[pl.BlockSpec(memory_space=pltpu.SMEM)] + [vmem] * (2 + 3 * ns),
        out_specs=[vmem] * (4 * ns),
        out_shape=shapes * 4,
    )(c_idx, g_mine, g_sib, *w, *m, *v)
    return [res[q * ns:(q + 1) * ns] for q in range(4)]


def _place():
    x, y, c = lax.axis_index("x"), lax.axis_index("y"), lax.axis_index("c")
    others = [(1 - x, y), (x, 1 - y), (1 - x, 1 - y)]
    return x, y, c, others


ANY_SPEC = pl.BlockSpec(memory_space=pl.ANY)


def _remote(src, dst, send_sems, recv_sems, k, to):
    return pltpu.make_async_remote_copy(src_ref=src, dst_ref=dst, send_sem=send_sems.at[k], recv_sem=recv_sems.at[k],
                                        device_id=to, device_id_type=MESH)


def _all_gather(shards):
    n = len(shards)

    def body(*refs):
        ins, outs, send_sems, recv_sems = refs[:n], refs[n:2 * n], refs[2 * n], refs[2 * n + 1]
        _gather_start(ins, outs, send_sems, recv_sems)
        _gather_finish(ins, outs, send_sems, recv_sems)

    return pl.pallas_call(
        body, name="all_gather_weights",
        out_shape=_gather_shapes(shards), in_specs=[ANY_SPEC] * n, out_specs=[ANY_SPEC] * n,
        scratch_shapes=_gather_sems(n),
    )(*shards)


def _gather_shapes(shards):
    return [jax.ShapeDtypeStruct((N_CHIPS,) + a.shape, a.dtype) for a in shards]


def _gather_sems(n):
    return [pltpu.SemaphoreType.DMA((6 * n,)), pltpu.SemaphoreType.DMA((6 * n,))]


def _gather_sends(ins, outs, send_sems, recv_sems):
    x, y, c, others = _place()
    s = 2 * x + y
    return [_remote(ins[t].at[c], outs[t].at[s, c], send_sems, recv_sems, 6 * t + j, (*chip, c))
            for t in range(len(ins)) for j, chip in enumerate(others)]


def _gather_start(ins, outs, send_sems, recv_sems):
    for cp in _gather_sends(ins, outs, send_sems, recv_sems):
        cp.start()


def _gather_finish(ins, outs, send_sems, recv_sems):
    x, y, c, others = _place()
    slot = lambda t, chip, half: outs[t].at[2 * chip[0] + chip[1], half]
    passed = []
    for t in range(len(ins)):
        for j, chip in enumerate(others):
            landed = slot(t, chip, c)
            _remote(landed, landed, send_sems, recv_sems, 6 * t + j, (x, y, c)).wait_recv()
            passed.append(_remote(landed, landed, send_sems, recv_sems, 6 * t + 3 + j, (x, y, 1 - c)))
            passed[-1].start()
    for t in range(len(ins)):
        for j, chip in enumerate(others):
            landed = slot(t, chip, 1 - c)
            _remote(landed, landed, send_sems, recv_sems, 6 * t + 3 + j, (x, y, c)).wait_recv()
    for cp in _gather_sends(ins, outs, send_sems, recv_sems) + passed:
        cp.wait_send()


def _swap_halves(gs, tag):
    n = len(gs)

    def body(*refs):
        ins, outs, send_sems, recv_sems = refs[:n], refs[n:2 * n], refs[2 * n], refs[2 * n + 1]
        _swap_start(ins, outs, send_sems, recv_sems)
        _swap_finish(ins, outs, send_sems, recv_sems)

    return pl.pallas_call(
        body, name="grad_swap_halves_" + tag,
        out_shape=_swap_shapes(gs), in_specs=[ANY_SPEC] * n, out_specs=[ANY_SPEC] * n,
        scratch_shapes=_swap_sems(n),
    )(*gs)


def _swap_shapes(gs):
    return [jax.ShapeDtypeStruct(g.shape[:1] + g.shape[2:], g.dtype) for g in gs]


def _swap_sems(n):
    return [pltpu.SemaphoreType.DMA((n,)), pltpu.SemaphoreType.DMA((n,))]


def _swap_copies(ins, outs, send_sems, recv_sems):
    x, y, c, _ = _place()
    return [_remote(ins[t].at[:, 1 - c], outs[t], send_sems, recv_sems, t, (x, y, 1 - c)) for t in range(len(ins))]


def _swap_start(ins, outs, send_sems, recv_sems):
    for cp in _swap_copies(ins, outs, send_sems, recv_sems):
        cp.start()


def _swap_finish(ins, outs, send_sems, recv_sems):
    for cp in _swap_copies(ins, outs, send_sems, recv_sems):
        cp.wait()


def _exchange_chips(pps):
    n = len(pps)

    def body(*refs):
        ins, outs, send_sems, recv_sems = refs[:n], refs[n:2 * n], refs[2 * n], refs[2 * n + 1]
        _exchange_start(ins, outs, send_sems, recv_sems)
        _exchange_finish(ins, outs, send_sems, recv_sems)

    return pl.pallas_call(
        body, name="grad_exchange_chips",
        out_shape=_exchange_shapes(pps), in_specs=[ANY_SPEC] * n, out_specs=[ANY_SPEC] * n,
        scratch_shapes=_exchange_sems(n),
    )(*pps)


def _exchange_shapes(pps):
    return [jax.ShapeDtypeStruct(p.shape, p.dtype) for p in pps]


def _exchange_sems(n):
    return [pltpu.SemaphoreType.DMA((3 * n,)), pltpu.SemaphoreType.DMA((3 * n,))]


def _exchange_sends(ins, outs, send_sems, recv_sems):
    x, y, c, others = _place()
    s = 2 * x + y
    return [_remote(ins[t].at[2 * chip[0] + chip[1]], outs[t].at[s], send_sems, recv_sems, 3 * t + j, (*chip, c))
            for t in range(len(ins)) for j, chip in enumerate(others)]


def _exchange_start(ins, outs, send_sems, recv_sems):
    for cp in _exchange_sends(ins, outs, send_sems, recv_sems):
        cp.start()


def _exchange_finish(ins, outs, send_sems, recv_sems):
    x, y, c, others = _place()
    for t in range(len(ins)):
        for j, chip in enumerate(others):
            landed = outs[t].at[2 * chip[0] + chip[1]]
            _remote(landed, landed, send_sems, recv_sems, 3 * t + j, (x, y, c)).wait_recv()
    for cp in _exchange_sends(ins, outs, send_sems, recv_sems):
        cp.wait_send()


def _share_half(rs):
    n = len(rs)

    def body(*refs):
        ins, outs, send_sems, recv_sems = refs[:n], refs[n:2 * n], refs[2 * n], refs[2 * n + 1]
        x, y, c, _ = _place()
        copies = [_remote(ins[t], outs[t], send_sems, recv_sems, t, (x, y, 1 - c)) for t in range(n)]
        for cp in copies:
            cp.start()
        for cp in copies:
            cp.wait()

    return pl.pallas_call(
        body, name="grad_share_half",
        out_shape=[jax.ShapeDtypeStruct(r.shape, r.dtype) for r in rs],
        in_specs=[ANY_SPEC] * n, out_specs=[ANY_SPEC] * n,
        scratch_shapes=[pltpu.SemaphoreType.DMA((n,)), pltpu.SemaphoreType.DMA((n,))],
    )(*rs)


ELEMENTWISE_BLOCK_BYTES = 1 << 20


def _row_tile(rows, cols):
    return _tile(rows, max(8, ELEMENTWISE_BLOCK_BYTES // (4 * cols) // 8 * 8))


def _add_my_half(g, sib, c_idx, name):
    rh, cols = g.shape[2:]
    tr = _row_tile(rh, cols)

    def body(c_ref, g_ref, s_ref, o_ref):
        o_ref[...] = (g_ref[...] + s_ref[...]).astype(BF16)

    return pl.pallas_call(
        body, name="grad_add_halves_" + name,
        grid_spec=pltpu.PrefetchScalarGridSpec(
            num_scalar_prefetch=1, grid=(N_CHIPS, rh // tr),
            in_specs=[pl.BlockSpec((None, None, tr, cols), lambda s, i, c: (s, c[0], i, 0)),
                      pl.BlockSpec((None, tr, cols), lambda s, i, c: (s, i, 0))],
            out_specs=pl.BlockSpec((None, tr, cols), lambda s, i, c: (s, i, 0))),
        out_shape=jax.ShapeDtypeStruct((N_CHIPS, rh, cols), BF16),
        compiler_params=_cp(2),
    )(c_idx, g, sib)


def _sum_chips(parts, pp, s_idx, name):
    rh, cols = parts.shape[1:]
    tr = _row_tile(rh, cols)

    def body(s_ref, p0, p1, p2, p3, mine_ref, o_ref):
        own = mine_ref[...]
        t = [jnp.where(s_ref[0] == k, own, p[...]).astype(F32) for k, p in enumerate((p0, p1, p2, p3))]
        o_ref[...] = ((t[0] + t[1]) + t[2]) + t[3]

    slot = lambda k: pl.BlockSpec((None, tr, cols), lambda i, s: (jnp.where(s[0] == k, (k + 1) % N_CHIPS, k), i, 0))
    return pl.pallas_call(
        body, name="grad_sum_chips_" + name,
        grid_spec=pltpu.PrefetchScalarGridSpec(
            num_scalar_prefetch=1, grid=(rh // tr,),
            in_specs=[slot(0), slot(1), slot(2), slot(3), pl.BlockSpec((None, tr, cols), lambda i, s: (s[0], i, 0))],
            out_specs=pl.BlockSpec((tr, cols), lambda i, s: (i, 0))),
        out_shape=jax.ShapeDtypeStruct((rh, cols), F32),
        compiler_params=_cp(1),
    )(s_idx, parts, parts, parts, parts, pp)


def _adamw(w, g_mine, g_sib, m, v, c_idx, name):
    rows, cols = w.shape
    tr = _row_tile(rows // 2, cols)
    nbh = rows // 2 // tr

    def body(c_ref, w_ref, gm_ref, gs_ref, m_ref, v_ref, g_ref, d_ref, mo_ref, vo_ref):
        g = jnp.where(pl.program_id(0) // nbh == c_ref[0], gm_ref[...], gs_ref[...])
        g_ref[...] = g
        d_ref[...], mo_ref[...], vo_ref[...] = _adamw_math(w_ref[...], g, m_ref[...], v_ref[...])

    spec = pl.BlockSpec((tr, cols), lambda i, c: (i, 0))
    half = pl.BlockSpec((tr, cols), lambda i, c: (i % nbh, 0))
    return pl.pallas_call(
        body, name="adamw_" + name,
        grid_spec=pltpu.PrefetchScalarGridSpec(
            num_scalar_prefetch=1, grid=(rows // tr,),
            in_specs=[spec, half, half, spec, spec], out_specs=[spec] * 4),
        out_shape=[jax.ShapeDtypeStruct(w.shape, F32)] * 4,
        compiler_params=_cp(1),
    )(c_idx, w, g_mine, g_sib, m, v)


def kernel(x, p, ffn1_w_in, ffn1_w_out, ln1_g, ln1_b, w_mix_in, b_forget, conv_w, g_attn, g_conv, w_mix_out, ln2_g, ln2_b, ffn2_w_in, ffn2_w_out, ln3_g, ln3_b, w_ple, w_ple_gate, b_ple_gate, ln4_g, ln4_b, loss_target, m_ffn1_w_in, m_ffn1_w_out, m_ln1_g, m_ln1_b, m_w_mix_in, m_b_forget, m_conv_w, m_g_attn, m_g_conv, m_w_mix_out, m_ln2_g, m_ln2_b, m_ffn2_w_in, m_ffn2_w_out, m_ln3_g, m_ln3_b, m_w_ple, m_w_ple_gate, m_b_ple_gate, m_ln4_g, m_ln4_b, v_ffn1_w_in, v_ffn1_w_out, v_ln1_g, v_ln1_b, v_w_mix_in, v_b_forget, v_conv_w, v_g_attn, v_g_conv, v_w_mix_out, v_ln2_g, v_ln2_b, v_ffn2_w_in, v_ffn2_w_out, v_ln3_g, v_ln3_b, v_w_ple, v_w_ple_gate, v_b_ple_gate, v_ln4_g, v_ln4_b):
    args = dict(locals())
    shard = {n: args[n][0] if LAYOUT[n][1] is not None else args[n] for n in WEIGHTS}
    m_shard = {n: args["m_" + n][0] if LAYOUT[n][1] is not None else args["m_" + n] for n in WEIGHTS}
    v_shard = {n: args["v_" + n][0] if LAYOUT[n][1] is not None else args["v_" + n] for n in WEIGHTS}
    c_idx = lax.axis_index("c").astype(jnp.int32).reshape(1)
    chip = (2 * lax.axis_index("x") + lax.axis_index("y")).astype(jnp.int32)

    conv_rows = SMALL_ROWS - shard["conv_w"].shape[0]
    mine = {n: _halves(shard[n].astype(BF16)) for n in BIG}
    mine["conv_w"] = _halves(jnp.pad(shard["conv_w"], ((0, conv_rows), (0, 0))))
    early_w = ["ffn1_w_in", "ffn1_w_out"]
    late_w = [n for n in BIG if n not in early_w] + ["conv_w"]

    def full_weights(names, gathered):
        out = {}
        for n, theirs in zip(names, gathered):
            g = lax.dynamic_update_slice(theirs, mine[n][None], (chip, 0, 0, 0))
            if n == "conv_w":
                out[n] = _join_chips(n, g.reshape(N_CHIPS, SMALL_ROWS, CONV_SHARD)[:, :3])
            else:
                out[n] = _join_chips(n, g.reshape((N_CHIPS,) + _shard_shape(n)))
        return out

    full = full_weights(early_w, _all_gather([mine[n] for n in early_w]))
    full.update({n: shard[n] for n in SMALL if n != "conv_w"})

    def per_chip(names, grads):
        by_chip = lambda n: grads[n] if grads[n].ndim == 3 else _split_chips(n, grads[n])
        return [_halves(_pack_small_grads(grads) if n == "small" else by_chip(n)) for n in names]

    def add_halves(names, mine_, sibs):
        return [_add_my_half(g, sib, c_idx, n) for n, g, sib in zip(names, mine_, sibs)]

    def chip_sums(names, grads):
        mine_ = per_chip(names, grads)
        return add_halves(names, mine_, _swap_halves(mine_, names[0]))

    ready_a = ["ffn2_w_in", "ffn2_w_out", "w_ple", "w_ple_gate"]
    ready_b = ["w_mix_in", "w_mix_out"]
    early_g = ready_a + ready_b
    late_g = early_w + ["small"]
    loss_acc, grad_x, grads, early_sums, early_parts = _local_step(
        x[0], p[0, 0], loss_target[0], full,
        overlap={"gather": [mine[n] for n in late_w], "weights": lambda gathered: full_weights(late_w, gathered),
                 "swap": lambda grads: per_chip(ready_a, grads),
                 "chip_sums": lambda grads, swapped, received: (add_halves(ready_a, swapped, received)
                                                                + chip_sums(ready_b, grads))})
    loss = lax.psum(loss_acc[0, 0], ("x", "y", "c"))

    late_sums = chip_sums(late_g, grads)
    names = early_g + late_g
    sums = list(early_sums) + late_sums
    parts = list(early_parts) + list(_exchange_chips(late_sums))
    half_of = {n: _sum_chips(pt, own, chip.reshape(1), n) for n, pt, own in zip(names, parts, sums)}
    names = BIG + ["small"]
    my_half = [half_of[n] for n in names]
    sib_half = _share_half(my_half)

    out = {}
    for n, gm, gs in zip(BIG, my_half, sib_half):
        out[n] = [a[None] for a in _adamw(shard[n], gm, gs, m_shard[n], v_shard[n], c_idx, n)]
    small = _adamw_small(my_half[-1], sib_half[-1], c_idx, [shard[n] for n in SMALL], [m_shard[n] for n in SMALL],
                         [v_shard[n] for n in SMALL])
    for i, n in enumerate(SMALL):
        out[n] = [small[q][i][None] if n == "conv_w" else small[q][i] for q in range(4)]
    return (loss, grad_x[None], *[out[n][q] for q in range(4) for n in WEIGHTS])
```

```python
import functools
import math

import jax
import jax.numpy as jnp
from jax import lax
from jax.experimental import pallas as pl
from jax.experimental.pallas import tpu as pltpu

F32 = jnp.float32
BF16 = jnp.bfloat16

D_MODEL = 1024
D_FF = 2816
N_HEADS = 8
HEAD_DIM = 64
D_ATTN = N_HEADS * HEAD_DIM
D_CONV = 512
PLE_DIM = 256
N_FLOG = 128
ALPHA = 2.0 ** 0.25
LN_EPS = 1e-5
RMS_EPS = 1e-6
NEG_INF = -1e30
Q_SCALE = 1.0 / math.sqrt(HEAD_DIM)
LOG2E = math.log2(math.e)

ADAM_LR = 0.001
ADAM_B1 = 0.9
ADAM_B2 = 0.999
ADAM_EPS = 1e-08
ADAM_WD = 0.01
ADAM_STEP = 10

V7X_VMEM_BYTES = 64 << 20
VMEM_LIMIT = V7X_VMEM_BYTES - (8 << 20)
LANE = 128
FF_CHUNK = 256
N_CHIPS = 4
MESH = pl.DeviceIdType.MESH


def _cp(n_axes):
    return pltpu.CompilerParams(dimension_semantics=("arbitrary",) * n_axes, vmem_limit_bytes=VMEM_LIMIT)


def _resident(shape):
    n = len(shape)
    return pl.BlockSpec(shape, lambda *_: (0,) * n, pipeline_mode=pl.Buffered(1))


def _nn(a, b):
    return jnp.dot(a, b, preferred_element_type=F32)


def _nt(a, b):
    return lax.dot_general(a, b, (((1,), (1,)), ((), ())), preferred_element_type=F32)


def _tn(a, b):
    return lax.dot_general(a, b, (((0,), (0,)), ((), ())), preferred_element_type=F32)


def _ln_stats(r):
    mu = jnp.mean(r, axis=-1, keepdims=True)
    xc = r - mu
    var = jnp.mean(xc * xc, axis=-1, keepdims=True)
    rstd = lax.rsqrt(var + LN_EPS)
    return xc * rstd, rstd


def _ln_bwd(dy, xhat, rstd, g):
    dxh = dy * g
    m1 = jnp.mean(dxh, axis=-1, keepdims=True)
    m2 = jnp.mean(dxh * xhat, axis=-1, keepdims=True)
    return rstd * (dxh - m1 - xhat * m2)


def _sigmoid(z):
    return 1.0 / (1.0 + jnp.exp(-z))


def _rowsum(a):
    return jnp.sum(a, axis=0, keepdims=True)


def _tile(total, want):
    if total <= want:
        return total
    for t in range(want - want % 8, 0, -8):
        if total % t == 0:
            return t
    raise ValueError((total, want))


def _ffn_fwd(x, w_in, w_out, lg, lb, name, gather=()):
    T = x.shape[0]
    tm = _tile(T, 512)
    nf = D_FF // FF_CHUNK
    ng = len(gather)
    last = T // tm - 1

    def body(x_ref, wi_ref, wo_ref, lg_ref, lb_ref, *rest):
        comm_in, (xo_ref, r_ref, g_ref, u_ref, h_ref) = rest[:ng], rest[ng:ng + 5]
        comm_out, sems = rest[ng + 5:2 * ng + 5], rest[2 * ng + 5:]
        if ng:
            @pl.when(pl.program_id(0) == 0)
            def _():
                _gather_start(comm_in, comm_out, *sems)

        xf = x_ref[...]
        xb = xf.astype(BF16)
        acc = jnp.zeros((tm, D_MODEL), F32)
        for j in range(nf):
            c0 = j * FF_CHUNK
            g = _nn(xb, wi_ref[:, c0:c0 + FF_CHUNK])
            u = _nn(xb, wi_ref[:, D_FF + c0:D_FF + c0 + FF_CHUNK])
            hb = (g * _sigmoid(g) * u).astype(BF16)
            g_ref[:, c0:c0 + FF_CHUNK] = g.astype(BF16)
            u_ref[:, c0:c0 + FF_CHUNK] = u.astype(BF16)
            h_ref[:, c0:c0 + FF_CHUNK] = hb
            acc = acc + _nn(hb, wo_ref[c0:c0 + FF_CHUNK, :])
        r = ALPHA * xf + 0.5 * acc
        r_ref[...] = r
        xhat, _ = _ln_stats(r)
        xo_ref[...] = xhat * lg_ref[...] + lb_ref[...]
        if ng:
            @pl.when(pl.program_id(0) == last)
            def _():
                _gather_finish(comm_in, comm_out, *sems)

    row = lambda n: pl.BlockSpec((tm, n), lambda i: (i, 0))
    return pl.pallas_call(
        body, name=name, grid=(T // tm,),
        in_specs=[row(D_MODEL), _resident((D_MODEL, 2 * D_FF)), _resident((D_FF, D_MODEL)),
                  _resident((1, D_MODEL)), _resident((1, D_MODEL))] + [ANY_SPEC] * ng,
        out_specs=[row(D_MODEL), row(D_MODEL), row(D_FF), row(D_FF), row(D_FF)] + [ANY_SPEC] * ng,
        out_shape=[jax.ShapeDtypeStruct((T, D_MODEL), F32), jax.ShapeDtypeStruct((T, D_MODEL), F32),
                   jax.ShapeDtypeStruct((T, D_FF), BF16), jax.ShapeDtypeStruct((T, D_FF), BF16),
                   jax.ShapeDtypeStruct((T, D_FF), BF16)] + _gather_shapes(gather),
        scratch_shapes=_gather_sems(ng) if ng else [],
        compiler_params=_cp(1),
    )(x, w_in, w_out, lg, lb, *gather)


def _ffn_bwd(dxo, r, g, u, w_in, w_out, lg, name, exchange=()):
    T = r.shape[0]
    tm = _tile(T, 256)
    nf = D_FF // FF_CHUNK
    ne = len(exchange)
    last = T // tm - 1

    def body(dxo_ref, r_ref, g_ref, u_ref, wi_ref, wo_ref, lg_ref, *rest):
        comm_in, (dx_ref, dgu_ref, df_ref, dlg_ref, dlb_ref) = rest[:ne], rest[ne:ne + 5]
        comm_out, sems = rest[ne + 5:2 * ne + 5], rest[2 * ne + 5:]
        i = pl.program_id(0)
        if ne:
            @pl.when(i == 0)
            def _():
                _exchange_start(comm_in, comm_out, *sems)

        dy = dxo_ref[...]
        xhat, rstd = _ln_stats(r_ref[...])
        dr = _ln_bwd(dy, xhat, rstd, lg_ref[...])

        @pl.when(i == 0)
        def _():
            dlg_ref[...] = jnp.zeros_like(dlg_ref)
            dlb_ref[...] = jnp.zeros_like(dlb_ref)

        dlg_ref[...] += _rowsum(dy * xhat)
        dlb_ref[...] += _rowsum(dy)
        dfb = (0.5 * dr).astype(BF16)
        df_ref[...] = dfb
        acc = jnp.zeros((tm, D_MODEL), F32)
        dh_ahead = _nt(dfb, wo_ref[0:FF_CHUNK, :])
        for j in range(nf):
            c0 = j * FF_CHUNK
            dh = dh_ahead
            if j + 1 < nf:
                dh_ahead = _nt(dfb, wo_ref[c0 + FF_CHUNK:c0 + 2 * FF_CHUNK, :])
            gg = g_ref[:, c0:c0 + FF_CHUNK].astype(F32)
            uu = u_ref[:, c0:c0 + FF_CHUNK].astype(F32)
            s = _sigmoid(gg)
            dgb = (dh * uu * s * (1.0 + gg * (1.0 - s))).astype(BF16)
            dub = (dh * gg * s).astype(BF16)
            dgu_ref[:, c0:c0 + FF_CHUNK] = dgb
            dgu_ref[:, D_FF + c0:D_FF + c0 + FF_CHUNK] = dub
            acc = acc + _nt(dgb, wi_ref[:, c0:c0 + FF_CHUNK]) + _nt(dub, wi_ref[:, D_FF + c0:D_FF + c0 + FF_CHUNK])
        dx_ref[...] = ALPHA * dr + acc
        if ne:
            @pl.when(i == last)
            def _():
                _exchange_finish(comm_in, comm_out, *sems)

    row = lambda n: pl.BlockSpec((tm, n), lambda i: (i, 0))
    return pl.pallas_call(
        body, name=name, grid=(T // tm,),
        in_specs=[row(D_MODEL), row(D_MODEL), row(D_FF), row(D_FF), _resident((D_MODEL, 2 * D_FF)),
                  _resident((D_FF, D_MODEL)), _resident((1, D_MODEL))] + [ANY_SPEC] * ne,
        out_specs=[row(D_MODEL), row(2 * D_FF), row(D_MODEL), _resident((1, D_MODEL)), _resident((1, D_MODEL))]
        + [ANY_SPEC] * ne,
        out_shape=[jax.ShapeDtypeStruct((T, D_MODEL), F32), jax.ShapeDtypeStruct((T, 2 * D_FF), BF16),
                   jax.ShapeDtypeStruct((T, D_MODEL), BF16), jax.ShapeDtypeStruct((1, D_MODEL), F32),
                   jax.ShapeDtypeStruct((1, D_MODEL), F32)] + _exchange_shapes(exchange),
        scratch_shapes=_exchange_sems(ne) if ne else [],
        compiler_params=_cp(1),
    )(dxo, r, g, u, w_in, w_out, lg, *exchange)


def _matmul_tn(a, b, name, by_chip=False, exchange=()):
    T, K = a.shape
    N = b.shape[1]
    tt = _tile(T, 1024)
    tn = N // N_CHIPS if by_chip else N
    while K * tn * 4 > (6 << 20) and tn % 256 == 0 and not by_chip:
        tn //= 2
    assert N % tn == 0
    ne = len(exchange)
    grid = (N // tn, T // tt)

    def body(a_ref, b_ref, *rest):
        comm_in, o_ref, comm_out, sems = rest[:ne], rest[ne], rest[ne + 1:2 * ne + 1], rest[2 * ne + 1:]
        n, t = pl.program_id(0), pl.program_id(1)
        if ne:
            @pl.when((n == 0) & (t == 0))
            def _():
                _exchange_start(comm_in, comm_out, *sems)

        @pl.when(t == 0)
        def _():
            o_ref[...] = jnp.zeros_like(o_ref)

        o_ref[...] += _tn(a_ref[...].astype(BF16), b_ref[...].astype(BF16))
        if ne:
            @pl.when((n == grid[0] - 1) & (t == grid[1] - 1))
            def _():
                _exchange_finish(comm_in, comm_out, *sems)

    res = pl.pallas_call(
        body, name=name, grid=grid,
        in_specs=[pl.BlockSpec((tt, K), lambda n, t: (t, 0)), pl.BlockSpec((tt, tn), lambda n, t: (t, n))]
        + [ANY_SPEC] * ne,
        out_specs=[pl.BlockSpec((None, K, tn), lambda n, t: (n, 0, 0)) if by_chip
                   else pl.BlockSpec((K, tn), lambda n, t: (0, n))] + [ANY_SPEC] * ne,
        out_shape=[jax.ShapeDtypeStruct((N_CHIPS, K, tn) if by_chip else (K, N), F32)] + _exchange_shapes(exchange),
        scratch_shapes=_exchange_sems(ne) if ne else [],
        compiler_params=_cp(2),
    )(a, b, *exchange)
    return res if ne else res[0]


def _matmul_tokens(at, b, name):
    M, T = at.shape
    N = b.shape[1]
    tt = _tile(T, 1024)

    def body(a_ref, b_ref, o_ref):
        @pl.when(pl.program_id(0) == 0)
        def _():
            o_ref[...] = jnp.zeros_like(o_ref)

        o_ref[...] += _nn(a_ref[...].astype(BF16), b_ref[...].astype(BF16))

    return pl.pallas_call(
        body, name=name, grid=(T // tt,),
        in_specs=[pl.BlockSpec((M, tt), lambda t: (0, t)), pl.BlockSpec((tt, N), lambda t: (t, 0))],
        out_specs=pl.BlockSpec((M, N), lambda t: (0, 0)),
        out_shape=jax.ShapeDtypeStruct((M, N), F32),
        compiler_params=_cp(1),
    )(at, b)


def _matmul_nn(x, w, scale, out_dtype, name, also_transposed=False):
    T, K = x.shape
    N = w.shape[1]
    tm = _tile(T, 512)

    def body(x_ref, w_ref, s_ref, o_ref, *ot_ref):
        res = _nn(x_ref[...].astype(BF16), w_ref[...]) * s_ref[...]
        o_ref[...] = res.astype(out_dtype)
        if also_transposed:
            ot_ref[0][...] = res.T.astype(out_dtype)

    res = pl.pallas_call(
        body, name=name, grid=(T // tm,),
        in_specs=[pl.BlockSpec((tm, K), lambda i: (i, 0)), _resident((K, N)), _resident((1, N))],
        out_specs=[pl.BlockSpec((tm, N), lambda i: (i, 0))] + [pl.BlockSpec((N, tm), lambda i: (0, i))] * also_transposed,
        out_shape=[jax.ShapeDtypeStruct((T, N), out_dtype)] + [jax.ShapeDtypeStruct((N, T), out_dtype)] * also_transposed,
        compiler_params=_cp(1),
    )(x, w, scale)
    return res if also_transposed else res[0]


def _log_sigmoid(z):
    return jnp.minimum(z, 0.0) - jnp.log1p(jnp.exp(-jnp.abs(z)))


def _tri(n, lower):
    r = lax.broadcasted_iota(jnp.int32, (n, n), 0)
    c = lax.broadcasted_iota(jnp.int32, (n, n), 1)
    return jnp.where((c <= r) if lower else (c >= r), 1.0, 0.0).astype(F32)


def _f32dot(a, b):
    return jnp.dot(a, b, preferred_element_type=F32, precision=lax.Precision.HIGHEST)


def _forget_cumsum(flog, col, bf):
    T = flog.shape[0]
    bt = _tile(T, 512)

    def body(f_ref, b_ref, c_ref, carry):
        @pl.when(pl.program_id(0) == 0)
        def _():
            carry[...] = jnp.zeros_like(carry)

        lf = _log_sigmoid(f_ref[...] + b_ref[...])
        c = _f32dot(_tri(bt, True), lf) + carry[...]
        c_ref[...] = c * LOG2E
        carry[...] = c[bt - 1:bt, :]

    return pl.pallas_call(
        body, name="forget_cumsum", grid=(T // bt,),
        in_specs=[pl.BlockSpec((bt, N_FLOG), lambda i: (i, col)), _resident((1, N_FLOG))],
        out_specs=pl.BlockSpec((bt, N_FLOG), lambda i: (i, 0)),
        out_shape=jax.ShapeDtypeStruct((T, N_FLOG), F32),
        scratch_shapes=[pltpu.VMEM((1, N_FLOG), F32)],
        compiler_params=_cp(1),
    )(flog, bf)


def _forget_bwd(dck, dcq, flog, col, bf):
    T = dcq.shape[0]
    bt = _tile(T, 512)
    nb = T // bt

    def body(k0_ref, k1_ref, k2_ref, k3_ref, dcq_ref, f_ref, b_ref, dz_ref, db_ref, carry):
        @pl.when(pl.program_id(0) == 0)
        def _():
            carry[...] = jnp.zeros_like(carry)
            db_ref[...] = jnp.zeros_like(db_ref)

        dc = ((k0_ref[...] + k1_ref[...]) + (k2_ref[...] + k3_ref[...])) + dcq_ref[...]
        dlf = _f32dot(_tri(bt, False), dc) + carry[...]
        carry[...] = dlf[0:1, :]
        z = f_ref[...] + b_ref[...]
        dz = dlf * _sigmoid(-z)
        dz_ref[...] = dz.astype(BF16)
        db_ref[...] += _rowsum(dz)

    slab = lambda j: pl.BlockSpec((None, bt, N_FLOG), lambda i: (j, nb - 1 - i, 0))
    return pl.pallas_call(
        body, name="forget_bwd", grid=(nb,),
        in_specs=[slab(0), slab(1), slab(2), slab(3),
                  pl.BlockSpec((bt, N_FLOG), lambda i: (nb - 1 - i, 0)),
                  pl.BlockSpec((bt, N_FLOG), lambda i: (nb - 1 - i, col)), _resident((1, N_FLOG))],
        out_specs=[pl.BlockSpec((bt, N_FLOG), lambda i: (nb - 1 - i, 0)), _resident((1, N_FLOG))],
        out_shape=[jax.ShapeDtypeStruct((T, N_FLOG), BF16), jax.ShapeDtypeStruct((1, N_FLOG), F32)],
        scratch_shapes=[pltpu.VMEM((1, N_FLOG), F32)],
        compiler_params=_cp(1),
    )(dck, dck, dck, dck, dcq, flog, bf)


def _head_masks():
    lane = lax.broadcasted_iota(jnp.int32, (1, LANE), 1)
    return lane < HEAD_DIM


def _split_heads(x2, is_a):
    zero = jnp.zeros_like(x2)
    return jnp.where(is_a, x2, zero), jnp.where(is_a, zero, x2)


BIAS_PARTS = 3


def _bias_lanes(h):
    lane = lax.broadcasted_iota(jnp.int32, (1, LANE), 1)
    first = (1 - h) * HEAD_DIM
    return lane, first


def _fold_key_bias(qkv, c):
    T = qkv.shape[0]
    tm = _tile(T, 512)
    npair = N_HEADS // 2

    def body(k_ref, c_ref, o_ref):
        cc = c_ref[...]
        parts, rest = [], cc
        for _ in range(BIAS_PARTS):
            piece = rest.astype(BF16)
            parts.append(piece)
            rest = rest - piece.astype(F32)
        for j in range(npair):
            k2 = k_ref[:, j * LANE:(j + 1) * LANE]
            for h in range(2):
                lane, first = _bias_lanes(h)
                out = k2
                for n, piece in enumerate(parts):
                    col = piece[:, 2 * j + h:2 * j + h + 1]
                    out = jnp.where(lane == first + n, col, out)
                o_ref[:, (2 * j + h) * LANE:(2 * j + h + 1) * LANE] = out

    return pl.pallas_call(
        body, name="fold_key_bias", grid=(T // tm,),
        in_specs=[pl.BlockSpec((tm, D_ATTN), lambda i: (i, 1)), pl.BlockSpec((tm, N_FLOG), lambda i: (i, 0))],
        out_specs=pl.BlockSpec((tm, 2 * D_ATTN), lambda i: (i, 0)),
        out_shape=jax.ShapeDtypeStruct((T, 2 * D_ATTN), BF16),
        compiler_params=_cp(1),
    )(qkv, c)


def _attn_fwd(qkv, vt, kb):
    T = qkv.shape[0]
    tq = _tile(T, 512)
    tk = tq
    nq = T // tq
    npair = N_HEADS // 2

    def body(q_ref, ka_ref, kb_ref, vt_ref, o_ref, al_ref, m_s, l_s, acc_s):
        i = pl.program_id(1)
        qs = []
        for h, qh in enumerate(_split_heads(q_ref[...], _head_masks())):
            lane, first = _bias_lanes(h)
            qs.append(jnp.where((lane >= first) & (lane < first + BIAS_PARTS), -1.0, qh).astype(BF16))
        k_refs = (ka_ref, kb_ref)
        m_s[...] = jnp.full_like(m_s, NEG_INF)
        l_s[...] = jnp.zeros_like(l_s)
        acc_s[...] = jnp.zeros_like(acc_s)

        def scores_at(kk):
            k0 = pl.multiple_of(kk * tk, tk)
            return tuple(_nt(k_refs[h][pl.ds(k0, tk), :], qs[h]) for h in range(2))

        def consume(kk, scores, masked):
            k0 = pl.multiple_of(kk * tk, tk)
            v2t = vt_ref[:, pl.ds(k0, tk)]
            for h in range(2):
                zt = scores[h]
                if masked:
                    rr = lax.broadcasted_iota(jnp.int32, (tk, tq), 0)
                    cc = lax.broadcasted_iota(jnp.int32, (tk, tq), 1)
                    zt = jnp.where(cc >= rr, zt, NEG_INF)
                m_old = m_s[h]
                m_new = jnp.maximum(m_old, jnp.max(zt, axis=0, keepdims=True))
                p = jnp.exp2(zt - m_new)
                a = jnp.exp2(m_old - m_new)
                l_s[h] = a * l_s[h] + jnp.sum(p, axis=0, keepdims=True)
                acc_s[h] = a * acc_s[h] + _nn(v2t, p.astype(BF16))
                m_s[h] = m_new

        def two_tiles(kk, second_masked):
            first, second = scores_at(kk), scores_at(kk + 1)
            consume(kk, first, False)
            consume(kk + 1, second, second_masked)

        def loop_body(t, carry):
            two_tiles(2 * t, False)
            return carry

        lax.fori_loop(0, i // 2, loop_body, 0)

        @pl.when(i % 2 == 1)
        def _():
            two_tiles(i - 1, True)

        @pl.when(i % 2 == 0)
        def _():
            consume(i, scores_at(i), True)

        outs = []
        for h in range(2):
            l = l_s[h]
            outs.append(acc_s[h] * (1.0 / l))
            al_ref[0, h:h + 1, :] = -(m_s[h] + jnp.log2(l))
        al_ref[0, 2:8, :] = jnp.zeros((6, tq), F32)
        dim = lax.broadcasted_iota(jnp.int32, (LANE, 1), 0)
        o_ref[...] = jnp.where(dim < HEAD_DIM, outs[0], outs[1]).T

    rowl = pl.BlockSpec((1, 8, tq), lambda j, i: (j, 0, i))
    return pl.pallas_call(
        body, name="attn_fwd", grid=(npair, nq),
        in_specs=[pl.BlockSpec((tq, LANE), lambda j, i: (i, j)),
                  pl.BlockSpec((T, LANE), lambda j, i: (0, 2 * j), pipeline_mode=pl.Buffered(1)),
                  pl.BlockSpec((T, LANE), lambda j, i: (0, 2 * j + 1), pipeline_mode=pl.Buffered(1)),
                  pl.BlockSpec((LANE, T), lambda j, i: (2 * npair + j, 0), pipeline_mode=pl.Buffered(1))],
        out_specs=[pl.BlockSpec((tq, LANE), lambda j, i: (i, j)), rowl],
        out_shape=[jax.ShapeDtypeStruct((T, D_ATTN), F32), jax.ShapeDtypeStruct((npair, 8, T), F32)],
        scratch_shapes=[pltpu.VMEM((2, 1, tq), F32), pltpu.VMEM((2, 1, tq), F32), pltpu.VMEM((2, LANE, tq), F32)],
        compiler_params=_cp(2),
    )(qkv, kb, kb, vt)


def _attn_bwd(qkv, qkvt, dob, dobt, cb, alrow, dlrow):
    T = qkv.shape[0]
    tq = _tile(T, 512)
    tk = tq
    nq = T // tq
    npair = N_HEADS // 2

    def body(q_ref, qt_ref, k_ref, kt_ref, v_ref, do_ref, dot_ref, cb_ref, al_ref, dl_ref,
             dq_ref, dk_ref, dv_ref, dc_ref, dcq_ref, dk_s, dv_s, dc_s):
        kj = pl.program_id(1)
        is_a = _head_masks()
        ks = _split_heads(k_ref[...], is_a)
        vs = _split_heads(v_ref[...], is_a)
        dim_a = lax.broadcasted_iota(jnp.int32, (LANE, 1), 0) < HEAD_DIM
        kts = _split_heads(kt_ref[...], dim_a)
        head_lane = lax.broadcasted_iota(jnp.int32, (1, LANE), 1) - 2 * pl.program_id(0)
        cs = tuple(jnp.sum(jnp.where(head_lane == h, cb_ref[...], 0.0), axis=-1, keepdims=True) for h in range(2))

        @pl.when(kj == 0)
        def _():
            dq_ref[...] = jnp.zeros_like(dq_ref)
            dcq_ref[...] = jnp.zeros_like(dcq_ref)

        dk_s[...] = jnp.zeros_like(dk_s)
        dv_s[...] = jnp.zeros_like(dv_s)
        dc_s[...] = jnp.zeros_like(dc_s)

        def step(qi, masked):
            q0 = pl.multiple_of(qi * tq, tq)
            q2 = q_ref[pl.ds(q0, tq), :]
            do2 = do_ref[pl.ds(q0, tq), :]
            qt2 = qt_ref[:, pl.ds(q0, tq)]
            dot2 = dot_ref[:, pl.ds(q0, tq)]
            for h in range(2):
                alr = al_ref[0, h:h + 1, pl.ds(q0, tq)]
                dlr = dl_ref[0, h:h + 1, pl.ds(q0, tq)]
                zt = _nt(ks[h], q2) + (alr - cs[h])
                if masked:
                    rr = lax.broadcasted_iota(jnp.int32, (tk, tq), 0)
                    cc = lax.broadcasted_iota(jnp.int32, (tk, tq), 1)
                    zt = jnp.where(cc >= rr, zt, NEG_INF)
                pt = jnp.exp2(zt)
                dst = pt * (_nt(vs[h], do2) - dlr)
                pb = pt.astype(BF16)
                dsb = dst.astype(BF16)
                dv_s[h] += _nt(dot2, pb)
                dk_s[h] += _nt(qt2, dsb)
                dc_s[h] += jnp.sum(dst, axis=-1, keepdims=True)
                dcq_ref[0, h:h + 1, pl.ds(q0, tq)] += jnp.sum(dst, axis=0, keepdims=True)
                dq_ref[:, pl.ds(q0, tq)] += _nn(kts[h], dsb)

        step(kj, True)
        rest = nq - 1 - kj

        def loop_body(t, carry):
            step(kj + 1 + 2 * t, False)
            step(kj + 2 + 2 * t, False)
            return carry

        lax.fori_loop(0, rest // 2, loop_body, 0)

        @pl.when(rest % 2 == 1)
        def _():
            step(nq - 1, False)
        dk_ref[...] = (jnp.where(dim_a, dk_s[0], dk_s[1]) * (1.0 / LOG2E)).astype(BF16)
        dv_ref[...] = jnp.where(dim_a, dv_s[0], dv_s[1]).astype(BF16)
        lane = lax.broadcasted_iota(jnp.int32, (1, LANE), 1)
        head = 2 * pl.program_id(0)
        dc_ref[...] = jnp.where(lane == head, -dc_s[0], jnp.where(lane == head + 1, -dc_s[1], 0.0))

        @pl.when(kj == nq - 1)
        def _():
            dq_ref[...] = dq_ref[...] * Q_SCALE

    full = lambda col: pl.BlockSpec((T, LANE), lambda j, kj: (0, col(j)), pipeline_mode=pl.Buffered(1))
    fullt = lambda row: pl.BlockSpec((LANE, T), lambda j, kj: (row(j), 0), pipeline_mode=pl.Buffered(1))
    tile = lambda col: pl.BlockSpec((tk, LANE), lambda j, kj: (kj, col(j)))
    tilet = lambda row: pl.BlockSpec((LANE, tk), lambda j, kj: (row(j), kj))
    rowl = pl.BlockSpec((1, 8, T), lambda j, kj: (j, 0, 0))
    return pl.pallas_call(
        body, name="attn_bwd", grid=(npair, nq),
        in_specs=[full(lambda j: j), fullt(lambda j: j), tile(lambda j: npair + j), tilet(lambda j: npair + j),
                  tile(lambda j: 2 * npair + j), full(lambda j: j), fullt(lambda j: j), tile(lambda j: 0), rowl, rowl],
        out_specs=[pl.BlockSpec((LANE, T), lambda j, kj: (j, 0)), tilet(lambda j: j), tilet(lambda j: j),
                   pl.BlockSpec((None, tk, LANE), lambda j, kj: (j, kj, 0)), rowl],
        out_shape=[jax.ShapeDtypeStruct((D_ATTN, T), F32), jax.ShapeDtypeStruct((D_ATTN, T), BF16),
                   jax.ShapeDtypeStruct((D_ATTN, T), BF16), jax.ShapeDtypeStruct((npair, T, LANE), F32),
                   jax.ShapeDtypeStruct((npair, 8, T), F32)],
        scratch_shapes=[pltpu.VMEM((2, LANE, tk), F32), pltpu.VMEM((2, LANE, tk), F32), pltpu.VMEM((2, tk, 1), F32)],
        compiler_params=_cp(2),
    )(qkv, qkvt, qkv, qkvt, qkv, dob, dobt, cb, alrow, dlrow)


HALO = 8


def _shift_rows(cur, other, k, tm, down):
    row = lax.broadcasted_iota(jnp.int32, (tm, 1), 0)
    reps = tm // HALO
    if down:
        rolled = pltpu.roll(cur, k, 0)
        fill = jnp.tile(pltpu.roll(other, k, 0), (reps, 1))
        return jnp.where(row < k, fill, rolled)
    rolled = pltpu.roll(cur, tm - k, 0)
    fill = jnp.tile(pltpu.roll(other, HALO - k, 0), (reps, 1))
    return jnp.where(row >= tm - k, fill, rolled)


def _conv_fwd(c, hh, c_prev, hh_prev, w_ref, first, tm):
    u = c * hh
    u_prev = jnp.where(first, 0.0, c_prev * hh_prev)
    u1 = _shift_rows(u, u_prev, 1, tm, True)
    u2 = _shift_rows(u, u_prev, 2, tm, True)
    y = w_ref[0:1, :] * u2 + w_ref[1:2, :] * u1 + w_ref[2:3, :] * u
    return u, u1, u2, y


def _rms(x, g):
    rs = lax.rsqrt(jnp.mean(x * x, axis=-1, keepdims=True) + RMS_EPS)
    return x * rs * g, rs


def _mixer_tail_fwd(o, bchf, conv_w, g_attn, g_conv, w_mo, x1, lg, lb):
    T = o.shape[0]
    tm = _tile(T, 512)
    hb = tm // HALO

    def body(o_ref, b_ref, c_ref, h_ref, cp_ref, hp_ref, w_ref, ga_ref, gc_ref, wmo_ref, x1_ref, lg_ref, lb_ref,
             x2_ref, r2_ref, mg_ref):
        first = pl.program_id(0) == 0
        _, _, _, y = _conv_fwd(c_ref[...], h_ref[...], cp_ref[...], hp_ref[...], w_ref, first, tm)
        na, _ = _rms(o_ref[...], ga_ref[...])
        nc, _ = _rms(b_ref[...] * y, gc_ref[...])
        nab = na.astype(BF16)
        ncb = nc.astype(BF16)
        mg_ref[:, 0:D_ATTN] = nab
        mg_ref[:, D_ATTN:] = ncb
        r2 = ALPHA * x1_ref[...] + _nn(nab, wmo_ref[0:D_ATTN, :]) + _nn(ncb, wmo_ref[D_ATTN:, :])
        r2_ref[...] = r2
        xhat, _ = _ln_stats(r2)
        x2_ref[...] = xhat * lg_ref[...] + lb_ref[...]

    row = lambda n, col=0: pl.BlockSpec((tm, n), lambda i: (i, col))
    prev = lambda col: pl.BlockSpec((HALO, D_CONV), lambda i: (jnp.maximum(i * hb - 1, 0), col))
    return pl.pallas_call(
        body, name="mixer_tail_fwd", grid=(T // tm,),
        in_specs=[row(D_ATTN), row(D_CONV, 0), row(D_CONV, 1), row(D_CONV, 2), prev(1), prev(2),
                  _resident((3, D_CONV)), _resident((1, D_ATTN)), _resident((1, D_CONV)),
                  _resident((D_MODEL, D_MODEL)), row(D_MODEL), _resident((1, D_MODEL)), _resident((1, D_MODEL))],
        out_specs=[row(D_MODEL), row(D_MODEL), row(D_MODEL)],
        out_shape=[jax.ShapeDtypeStruct((T, D_MODEL), F32), jax.ShapeDtypeStruct((T, D_MODEL), F32),
                   jax.ShapeDtypeStruct((T, D_MODEL), BF16)],
        compiler_params=_cp(1),
    )(o, bchf, bchf, bchf, bchf, bchf, conv_w, g_attn, g_conv, w_mo, x1, lg, lb)


def _head_sum_rows():
    row = lax.broadcasted_iota(jnp.int32, (4 * 8, D_ATTN), 0)
    head = lax.broadcasted_iota(jnp.int32, (4 * 8, D_ATTN), 1) // HEAD_DIM
    return jnp.where((row % 8 < 2) & (2 * (row // 8) + row % 8 == head), 1.0, 0.0).astype(F32)


def _mixer_tail_bwd(dx2, r2, lg, w_mo, o, bchf, conv_w, g_attn, g_conv, swap=()):
    T = o.shape[0]
    tm = _tile(T, 256)
    hb = tm // HALO
    ns = len(swap)
    last = T // tm - 1

    def body(dx2_ref, r2_ref, lg_ref, wmo_ref, o_ref, b_ref, c_ref, h_ref, cp_ref, hp_ref, w_ref, ga_ref, gc_ref,
             *rest):
        comm_in = rest[:ns]
        dx1_ref, dr_ref, do_ref, dot_ref, dl_ref, dco_ref, dlg_ref, dlb_ref, dga_ref, dgc_ref = rest[ns:ns + 10]
        comm_out, sems = rest[ns + 10:2 * ns + 10], rest[2 * ns + 10:]
        i = pl.program_id(0)
        if ns:
            @pl.when(i == 0)
            def _():
                _swap_start(comm_in, comm_out, *sems)

            @pl.when(i == last)
            def _():
                _swap_finish(comm_in, comm_out, *sems)

        @pl.when(i == 0)
        def _():
            for ref in (dlg_ref, dlb_ref, dga_ref, dgc_ref):
                ref[...] = jnp.zeros_like(ref)

        dy = dx2_ref[...]
        xhat, rstd = _ln_stats(r2_ref[...])
        dr = _ln_bwd(dy, xhat, rstd, lg_ref[...])
        dlg_ref[...] += _rowsum(dy * xhat)
        dlb_ref[...] += _rowsum(dy)
        dx1_ref[...] = ALPHA * dr
        drb = dr.astype(BF16)
        dr_ref[...] = drb
        dna = _nt(drb, wmo_ref[0:D_ATTN, :])
        dnc = _nt(drb, wmo_ref[D_ATTN:, :])

        def rms_bwd(x, g, dn):
            rs = lax.rsqrt(jnp.mean(x * x, axis=-1, keepdims=True) + RMS_EPS)
            dng = dn * g
            dx = rs * dng - x * (rs * rs * rs) * jnp.mean(dng * x, axis=-1, keepdims=True)
            return dx, _rowsum(dn * x * rs)

        oo = o_ref[...]
        do, dga = rms_bwd(oo, ga_ref[...], dna)
        dga_ref[...] += dga
        do_ref[...] = do.astype(BF16)
        dot_ref[...] = do.T.astype(BF16)
        dl_ref[...] = lax.dot_general(_head_sum_rows(), do * oo, (((1,), (1,)), ((), ())),
                                      preferred_element_type=F32, precision=lax.Precision.HIGHEST)
        _, _, _, y = _conv_fwd(c_ref[...], h_ref[...], cp_ref[...], hp_ref[...], w_ref, i == 0, tm)
        dco, dgc = rms_bwd(b_ref[...] * y, gc_ref[...], dnc)
        dgc_ref[...] += dgc
        dco_ref[...] = dco

    row = lambda n, col=0: pl.BlockSpec((tm, n), lambda i: (i, col))
    prev = lambda col: pl.BlockSpec((HALO, D_CONV), lambda i: (jnp.maximum(i * hb - 1, 0), col))
    vec = lambda n: _resident((1, n))
    return pl.pallas_call(
        body, name="mixer_tail_bwd", grid=(T // tm,),
        in_specs=[row(D_MODEL), row(D_MODEL), vec(D_MODEL), _resident((D_MODEL, D_MODEL)), row(D_ATTN),
                  row(D_CONV, 0), row(D_CONV, 1), row(D_CONV, 2), prev(1), prev(2), _resident((3, D_CONV)),
                  vec(D_ATTN), vec(D_CONV)] + [ANY_SPEC] * ns,
        out_specs=[row(D_MODEL), row(D_MODEL), row(D_ATTN), pl.BlockSpec((D_ATTN, tm), lambda i: (0, i)),
                   pl.BlockSpec((4 * 8, tm), lambda i: (0, i)), row(D_CONV),
                   vec(D_MODEL), vec(D_MODEL), vec(D_ATTN), vec(D_CONV)] + [ANY_SPEC] * ns,
        out_shape=[jax.ShapeDtypeStruct((T, D_MODEL), F32), jax.ShapeDtypeStruct((T, D_MODEL), BF16),
                   jax.ShapeDtypeStruct((T, D_ATTN), BF16), jax.ShapeDtypeStruct((D_ATTN, T), BF16),
                   jax.ShapeDtypeStruct((4 * 8, T), F32),
                   jax.ShapeDtypeStruct((T, D_CONV), F32), jax.ShapeDtypeStruct((1, D_MODEL), F32),
                   jax.ShapeDtypeStruct((1, D_MODEL), F32), jax.ShapeDtypeStruct((1, D_ATTN), F32),
                   jax.ShapeDtypeStruct((1, D_CONV), F32)] + _swap_shapes(swap),
        scratch_shapes=_swap_sems(ns) if ns else [],
        compiler_params=_cp(1),
    )(dx2, r2, lg, w_mo, o, bchf, bchf, bchf, bchf, bchf, conv_w, g_attn, g_conv, *swap)


def _conv_bwd(dco, bchf, conv_w):
    T = dco.shape[0]
    tm = _tile(T, 512)
    hb = tm // HALO
    nt = T // tm

    def body(dco_ref, dcon_ref, b_ref, bn_ref, c_ref, h_ref, cp_ref, hp_ref, w_ref, dbch_ref, dw_ref):
        i = pl.program_id(0)

        @pl.when(i == 0)
        def _():
            dw_ref[...] = jnp.zeros_like(dw_ref)

        cc = c_ref[...]
        hh = h_ref[...]
        u, u1, u2, y = _conv_fwd(cc, hh, cp_ref[...], hp_ref[...], w_ref, i == 0, tm)
        dco = dco_ref[...]
        bb = b_ref[...]
        dyc = dco * bb
        dy_next = jnp.where(i == nt - 1, 0.0, dcon_ref[...] * bn_ref[...])
        d1 = _shift_rows(dyc, dy_next, 1, tm, False)
        d2 = _shift_rows(dyc, dy_next, 2, tm, False)
        du = w_ref[2:3, :] * dyc + w_ref[1:2, :] * d1 + w_ref[0:1, :] * d2
        dbch_ref[:, 0:D_CONV] = (dco * y).astype(BF16)
        dbch_ref[:, D_CONV:2 * D_CONV] = (du * hh).astype(BF16)
        dbch_ref[:, 2 * D_CONV:] = (du * cc).astype(BF16)
        dw_ref[0:1, :] += _rowsum(dyc * u2)
        dw_ref[1:2, :] += _rowsum(dyc * u1)
        dw_ref[2:3, :] += _rowsum(dyc * u)

    row = lambda n, col=0: pl.BlockSpec((tm, n), lambda i: (i, col))
    prev = lambda col: pl.BlockSpec((HALO, D_CONV), lambda i: (jnp.maximum(i * hb - 1, 0), col))
    nxt = lambda col: pl.BlockSpec((HALO, D_CONV), lambda i: (jnp.minimum((i + 1) * hb, T // HALO - 1), col))
    return pl.pallas_call(
        body, name="conv_bwd", grid=(nt,),
        in_specs=[row(D_CONV), nxt(0), row(D_CONV, 0), nxt(0), row(D_CONV, 1), row(D_CONV, 2), prev(1), prev(2),
                  _resident((3, D_CONV))],
        out_specs=[row(3 * D_CONV), _resident((8, D_CONV))],
        out_shape=[jax.ShapeDtypeStruct((T, 3 * D_CONV), BF16), jax.ShapeDtypeStruct((8, D_CONV), F32)],
        compiler_params=_cp(1),
    )(dco, dco, bchf, bchf, bchf, bchf, bchf, bchf, conv_w)


def _mixer_in_bwd(dx1a, dqt, dkt, dvt, dbch, dfl, w_qkvt, w_bch, w_f):
    T = dx1a.shape[0]
    tm = _tile(T, 512)

    def body(a_ref, dq_ref, dk_ref, dv_ref, db_ref, df_ref, wq_ref, wb_ref, wf_ref, o_ref):
        acc = a_ref[...] + _nt(db_ref[...], wb_ref[...]) + _nt(df_ref[...], wf_ref[...])
        for n, ref in enumerate((dq_ref, dk_ref, dv_ref)):
            acc = acc + _tn(ref[...].astype(BF16), wq_ref[n * D_ATTN:(n + 1) * D_ATTN, :])
        o_ref[...] = acc

    row = lambda n: pl.BlockSpec((tm, n), lambda i: (i, 0))
    col = pl.BlockSpec((D_ATTN, tm), lambda i: (0, i))
    return pl.pallas_call(
        body, name="mixer_in_bwd", grid=(T // tm,),
        in_specs=[row(D_MODEL), col, col, col, row(3 * D_CONV), row(N_FLOG),
                  _resident((3 * D_ATTN, D_MODEL)), _resident((D_MODEL, 3 * D_CONV)), _resident((D_MODEL, N_FLOG))],
        out_specs=row(D_MODEL),
        out_shape=jax.ShapeDtypeStruct((T, D_MODEL), F32),
        compiler_params=_cp(1),
    )(dx1a, dqt, dkt, dvt, dbch, dfl, w_qkvt, w_bch, w_f)


def _ple_loss(x3, p, tgt, w_g, w_p, b_g, lg, lb):
    T = x3.shape[0]
    tm = _tile(T, 512)

    def body(x_ref, p_ref, t_ref, wg_ref, wp_ref, bg_ref, lg_ref, lb_ref,
             dx_ref, de_ref, dz_ref, loss_ref, dlg_ref, dlb_ref, dbg_ref):
        @pl.when(pl.program_id(0) == 0)
        def _():
            for ref in (loss_ref, dlg_ref, dlb_ref, dbg_ref):
                ref[...] = jnp.zeros_like(ref)

        xf = x_ref[...]
        gate = _sigmoid(_nn(xf.astype(BF16), wg_ref[...]) + bg_ref[...])
        e = _nn(p_ref[...].astype(BF16), wp_ref[...])
        xhat, rstd = _ln_stats(ALPHA * xf + gate * e)
        err = xhat * lg_ref[...] + lb_ref[...] - t_ref[...]
        sq = jnp.sum(_rowsum(err * err), axis=-1, keepdims=True)
        loss_ref[...] += jnp.broadcast_to(sq * (0.5 / D_MODEL), loss_ref.shape)
        dy = err * (1.0 / D_MODEL)
        dr = _ln_bwd(dy, xhat, rstd, lg_ref[...])
        dlg_ref[...] += _rowsum(dy * xhat)
        dlb_ref[...] += _rowsum(dy)
        de_ref[...] = (dr * gate).astype(BF16)
        dz = dr * e * gate * (1.0 - gate)
        dbg_ref[...] += _rowsum(dz)
        dzb = dz.astype(BF16)
        dz_ref[...] = dzb
        dx_ref[...] = ALPHA * dr + _nt(dzb, wg_ref[...])

    row = lambda n: pl.BlockSpec((tm, n), lambda i: (i, 0))
    vec = lambda n: _resident((1, n))
    return pl.pallas_call(
        body, name="ple_loss", grid=(T // tm,),
        in_specs=[row(D_MODEL), row(PLE_DIM), row(D_MODEL), _resident((D_MODEL, D_MODEL)),
                  _resident((PLE_DIM, D_MODEL)), vec(D_MODEL), vec(D_MODEL), vec(D_MODEL)],
        out_specs=[row(D_MODEL), row(D_MODEL), row(D_MODEL), vec(LANE), vec(D_MODEL), vec(D_MODEL), vec(D_MODEL)],
        out_shape=[jax.ShapeDtypeStruct((T, D_MODEL), F32), jax.ShapeDtypeStruct((T, D_MODEL), BF16),
                   jax.ShapeDtypeStruct((T, D_MODEL), BF16), jax.ShapeDtypeStruct((1, LANE), F32),
                   jax.ShapeDtypeStruct((1, D_MODEL), F32), jax.ShapeDtypeStruct((1, D_MODEL), F32),
                   jax.ShapeDtypeStruct((1, D_MODEL), F32)],
        compiler_params=_cp(1),
    )(x3, p, tgt, w_g, w_p, b_g, lg, lb)


def _lane_layout(v8):
    return jnp.repeat(v8, HEAD_DIM, axis=1)


def _row_layout(v8):
    t = v8.shape[0]
    return jnp.pad(v8.T.reshape(N_HEADS // 2, 2, t), ((0, 0), (0, 6), (0, 0)))


def _from_lane_layout(vl):
    return vl[:, ::HEAD_DIM]


def _from_row_layout(vr):
    return vr[:, :2, :].reshape(N_HEADS, -1).T


def _local_step(x, p, tgt, w, overlap=None):
    bf = lambda a: a.astype(BF16)
    w1i, w1o = bf(w["ffn1_w_in"]), bf(w["ffn1_w_out"])
    first = _ffn_fwd(x, w1i, w1o, w["ln1_g"], w["ln1_b"], "ffn1_fwd", overlap["gather"] if overlap else ())
    x1, r1, g1, u1, h1 = first[:5]
    if overlap:
        w = {**w, **overlap["weights"](first[5:])}
    w2i, w2o = bf(w["ffn2_w_in"]), bf(w["ffn2_w_out"])
    wmi = w["w_mix_in"]
    o_f = 3 * D_ATTN
    o_b = o_f + N_HEADS
    w_qkv = bf(wmi[:, :o_f])
    w_f = bf(jnp.pad(wmi[:, o_f:o_b], ((0, 0), (0, N_FLOG - N_HEADS))))
    w_bch = bf(wmi[:, o_b:])
    w_bchf = jnp.concatenate([w_bch, w_f], axis=1)
    w_mo, w_g, w_p = bf(w["w_mix_out"]), bf(w["w_ple_gate"]), bf(w["w_ple"])
    b_f = jnp.pad(w["b_forget"], ((0, 0), (0, N_FLOG - N_HEADS)))

    q_scale = jnp.concatenate([jnp.full((1, D_ATTN), Q_SCALE * LOG2E, F32), jnp.ones((1, 2 * D_ATTN), F32)], axis=1)
    qkv, qkvt = _matmul_nn(x1, w_qkv, q_scale, BF16, "proj_qkv", also_transposed=True)
    bchf = _matmul_nn(x1, w_bchf, jnp.ones((1, 3 * D_CONV + N_FLOG), F32), F32, "proj_bchf")
    fcol = 3 * D_CONV // N_FLOG
    c = _forget_cumsum(bchf, fcol, b_f)
    o, alrow = _attn_fwd(qkv, qkvt, _fold_key_bias(qkv, c))
    x2, r2, merged = _mixer_tail_fwd(o, bchf, w["conv_w"], w["g_attn"], w["g_conv"], w_mo, x1, w["ln2_g"], w["ln2_b"])
    x3, r3, g2, u2, h2 = _ffn_fwd(x2, w2i, w2o, w["ln3_g"], w["ln3_b"], "ffn2_fwd")

    grads = {}
    dx3, de, dz, loss, grads["ln4_g"], grads["ln4_b"], grads["b_ple_gate"] = _ple_loss(
        x3, p, tgt, w_g, w_p, w["b_ple_gate"], w["ln4_g"], w["ln4_b"])
    by_chip = overlap is not None
    grads["w_ple"] = _matmul_tn(p, de, "dw_ple", by_chip)
    grads["w_ple_gate"] = _matmul_tn(x3, dz, "dw_ple_gate")

    dx2, dgu2, df2, grads["ln3_g"], grads["ln3_b"] = _ffn_bwd(dx3, r3, g2, u2, w2i, w2o, w["ln3_g"], "ffn2_bwd")
    grads["ffn2_w_in"] = _matmul_tn(x2, dgu2, "dw_ffn2_in", by_chip)
    grads["ffn2_w_out"] = _matmul_tn(h2, df2, "dw_ffn2_out")

    to_swap = overlap["swap"](grads) if overlap else ()
    tail = _mixer_tail_bwd(dx2, r2, w["ln2_g"], w_mo, o, bchf, w["conv_w"], w["g_attn"], w["g_conv"], to_swap)
    (dx1a, dr2, dob, dobt, delta, dco, grads["ln2_g"], grads["ln2_b"], grads["g_attn"], grads["g_conv"]) = tail[:10]
    grads["w_mix_out"] = _matmul_tn(merged, dr2, "dw_mix_out")
    dbch, dcw = _conv_bwd(dco, bchf, w["conv_w"])
    grads["conv_w"] = dcw[:3]
    dqt, dkt, dvt, dck, dcq = _attn_bwd(qkv, qkvt, dob, dobt, c, alrow, delta.reshape(N_HEADS // 2, 8, -1))
    dcq_lanes = jnp.pad(_from_row_layout(dcq), ((0, 0), (0, N_FLOG - N_HEADS)))
    dfl, dbf = _forget_bwd(dck, dcq_lanes, bchf, fcol, b_f)
    grads["b_forget"] = dbf[:, :N_HEADS]
    dx1 = _mixer_in_bwd(dx1a, dqt, dkt, dvt, dbch, dfl, w_qkv.T, w_bch, w_f)
    grads["w_mix_in"] = jnp.concatenate(
        [_matmul_tokens(dqt, x1, "dw_q").T, _matmul_tokens(dkt, x1, "dw_k").T, _matmul_tokens(dvt, x1, "dw_v").T,
         _matmul_tn(x1, dfl, "dw_flog")[:, :N_HEADS], _matmul_tn(x1, dbch, "dw_bch")], axis=1)

    dx0, dgu1, df1, grads["ln1_g"], grads["ln1_b"] = _ffn_bwd(dx1, r1, g1, u1, w1i, w1o, w["ln1_g"], "ffn1_bwd")
    grads["ffn1_w_out"] = _matmul_tn(h1, df1, "dw_ffn1_out")
    if not overlap:
        grads["ffn1_w_in"] = _matmul_tn(x, dgu1, "dw_ffn1_in")
        return loss, dx0, grads
    sums = overlap["chip_sums"](grads, to_swap, tail[10:])
    grads["ffn1_w_in"], *received = _matmul_tn(x, dgu1, "dw_ffn1_in", by_chip, exchange=sums)
    return loss, dx0, grads, sums, received


WEIGHTS = ["ffn1_w_in", "ffn1_w_out", "ln1_g", "ln1_b", "w_mix_in", "b_forget", "conv_w", "g_attn", "g_conv",
           "w_mix_out", "ln2_g", "ln2_b", "ffn2_w_in", "ffn2_w_out", "ln3_g", "ln3_b", "w_ple", "w_ple_gate",
           "b_ple_gate", "ln4_g", "ln4_b"]
LAYOUT = {
    "ffn1_w_in": ((D_MODEL, 2 * D_FF), 1), "ffn1_w_out": ((D_FF, D_MODEL), 0),
    "w_mix_in": ((D_MODEL, 3 * D_ATTN + N_HEADS + 3 * D_CONV), 1), "conv_w": ((3, D_CONV), 1),
    "w_mix_out": ((D_MODEL, D_MODEL), 0), "ffn2_w_in": ((D_MODEL, 2 * D_FF), 1), "ffn2_w_out": ((D_FF, D_MODEL), 0),
    "w_ple": ((PLE_DIM, D_MODEL), 1), "w_ple_gate": ((D_MODEL, D_MODEL), 0),
    "ln1_g": ((1, D_MODEL), None), "ln1_b": ((1, D_MODEL), None), "b_forget": ((1, N_HEADS), None),
    "g_attn": ((1, D_ATTN), None), "g_conv": ((1, D_CONV), None), "ln2_g": ((1, D_MODEL), None),
    "ln2_b": ((1, D_MODEL), None), "ln3_g": ((1, D_MODEL), None), "ln3_b": ((1, D_MODEL), None),
    "b_ple_gate": ((1, D_MODEL), None), "ln4_g": ((1, D_MODEL), None), "ln4_b": ((1, D_MODEL), None),
}
BIG = [n for n in WEIGHTS if LAYOUT[n][1] is not None and n != "conv_w"]
SMALL = [n for n in WEIGHTS if n not in BIG]
ROW = 1024
SMALL_ROWS = 16


def _shard_shape(name):
    shape, axis = LAYOUT[name]
    if axis is None:
        return shape
    return tuple(s // N_CHIPS if a == axis else s for a, s in enumerate(shape))


def _halves(a):
    return a.reshape(a.shape[:-2] + (2, a.shape[-2] // 2, a.shape[-1]))


def _split_chips(name, full):
    shape, axis = LAYOUT[name]
    if axis == 0:
        return full.reshape((N_CHIPS, shape[0] // N_CHIPS) + shape[1:])
    return jnp.moveaxis(full.reshape(shape[:1] + (N_CHIPS, shape[1] // N_CHIPS)), 1, 0)


def _join_chips(name, parts):
    shape, axis = LAYOUT[name]
    if axis == 0:
        return parts.reshape(shape)
    return jnp.moveaxis(parts, 0, 1).reshape(shape)


SMALL_AT = {"ln1_g": (0, 0), "ln1_b": (1, 0), "ln2_g": (2, 0), "ln2_b": (3, 0), "ln3_g": (4, 0), "ln3_b": (5, 0),
            "b_ple_gate": (6, 0), "ln4_g": (7, 0), "ln4_b": (8, 0), "g_attn": (9, 0), "g_conv": (9, D_ATTN),
            "b_forget": (10, 0), "conv_w": (10, LANE)}
CONV_SHARD = D_CONV // N_CHIPS


def _pack_small_grads(grads):
    def body(*refs):
        ins, o_ref = dict(zip(SMALL, refs[:-1])), refs[-1]
        o_ref[...] = jnp.zeros_like(o_ref)
        for s in range(N_CHIPS):
            for n in SMALL:
                r, c0 = SMALL_AT[n]
                if n == "conv_w":
                    for k in range(3):
                        o_ref[s, r:r + 1, c0 + k * CONV_SHARD:c0 + (k + 1) * CONV_SHARD] = (
                            ins[n][k:k + 1, s * CONV_SHARD:(s + 1) * CONV_SHARD])
                else:
                    o_ref[s, r:r + 1, c0:c0 + ins[n].shape[1]] = ins[n][...]

    return pl.pallas_call(
        body, name="pack_small_grads",
        out_shape=jax.ShapeDtypeStruct((N_CHIPS, SMALL_ROWS, ROW), F32),
    )(*[grads[n] for n in SMALL])


def _adamw_math(w, g, m, v):
    c1 = 1.0 - ADAM_B1 ** ADAM_STEP
    c2 = 1.0 - ADAM_B2 ** ADAM_STEP
    m = ADAM_B1 * m + (1.0 - ADAM_B1) * g
    v = ADAM_B2 * v + (1.0 - ADAM_B2) * (g * g)
    return -ADAM_LR * ((m / c1) / (jnp.sqrt(v / c2) + ADAM_EPS) + ADAM_WD * w), m, v


def _adamw_small(g_mine, g_sib, c_idx, w, m, v):
    ns = len(SMALL)

    def body(c_ref, gm_ref, gs_ref, *refs):
        ws, ms, vs = refs[:ns], refs[ns:2 * ns], refs[2 * ns:3 * ns]
        outs = refs[3 * ns:]
        mine_first = c_ref[0] == 0
        top = jnp.where(mine_first, gm_ref[...], gs_ref[...])
        bot = jnp.where(mine_first, gs_ref[...], gm_ref[...])
        for i, n in enumerate(SMALL):
            r, c0 = SMALL_AT[n]
            blk, rr = (top, r) if r < SMALL_ROWS // 2 else (bot, r - SMALL_ROWS // 2)
            rows, width = ws[i].shape
            for k in range(rows):
                g = blk[rr:rr + 1, c0 + k * width:c0 + (k + 1) * width]
                d, mn, vn = _adamw_math(ws[i][k:k + 1, :], g, ms[i][k:k + 1, :], vs[i][k:k + 1, :])
                for q, val in enumerate((g, d, mn, vn)):
                    outs[q * ns + i][k:k + 1, :] = val

    shapes = [jax.ShapeDtypeStruct(a.shape, F32) for a in w]
    vmem = pl.BlockSpec(memory_space=pltpu.VMEM)
    res = pl.pallas_call(
        body, name="adamw_small",
        in_specs=[pl.BlockSpec(memory_space=pltpu.SMEM)] + [vmem] * (2 + 3 * ns),
        out_specs=[vmem] * (4 * ns),
        out_shape=shapes * 4,
    )(c_idx, g_mine, g_sib, *w, *m, *v)
    return [res[q * ns:(q + 1) * ns] for q in range(4)]


def _place():
    x, y, c = lax.axis_index("x"), lax.axis_index("y"), lax.axis_index("c")
    others = [(1 - x, y), (x, 1 - y), (1 - x, 1 - y)]
    return x, y, c, others


ANY_SPEC = pl.BlockSpec(memory_space=pl.ANY)


def _remote(src, dst, send_sems, recv_sems, k, to):
    return pltpu.make_async_remote_copy(src_ref=src, dst_ref=dst, send_sem=send_sems.at[k], recv_sem=recv_sems.at[k],
                                        device_id=to, device_id_type=MESH)


def _all_gather(shards):
    n = len(shards)

    def body(*refs):
        ins, outs, send_sems, recv_sems = refs[:n], refs[n:2 * n], refs[2 * n], refs[2 * n + 1]
        _gather_start(ins, outs, send_sems, recv_sems)
        _gather_finish(ins, outs, send_sems, recv_sems)

    return pl.pallas_call(
        body, name="all_gather_weights",
        out_shape=_gather_shapes(shards), in_specs=[ANY_SPEC] * n, out_specs=[ANY_SPEC] * n,
        scratch_shapes=_gather_sems(n),
    )(*shards)


def _gather_shapes(shards):
    return [jax.ShapeDtypeStruct((N_CHIPS,) + a.shape, a.dtype) for a in shards]


def _gather_sems(n):
    return [pltpu.SemaphoreType.DMA((6 * n,)), pltpu.SemaphoreType.DMA((6 * n,))]


def _gather_sends(ins, outs, send_sems, recv_sems):
    x, y, c, others = _place()
    s = 2 * x + y
    return [_remote(ins[t].at[c], outs[t].at[s, c], send_sems, recv_sems, 6 * t + j, (*chip, c))
            for t in range(len(ins)) for j, chip in enumerate(others)]


def _gather_start(ins, outs, send_sems, recv_sems):
    for cp in _gather_sends(ins, outs, send_sems, recv_sems):
        cp.start()


def _gather_finish(ins, outs, send_sems, recv_sems):
    x, y, c, others = _place()
    slot = lambda t, chip, half: outs[t].at[2 * chip[0] + chip[1], half]
    passed = []
    for t in range(len(ins)):
        for j, chip in enumerate(others):
            landed = slot(t, chip, c)
            _remote(landed, landed, send_sems, recv_sems, 6 * t + j, (x, y, c)).wait_recv()
            passed.append(_remote(landed, landed, send_sems, recv_sems, 6 * t + 3 + j, (x, y, 1 - c)))
            passed[-1].start()
    for t in range(len(ins)):
        for j, chip in enumerate(others):
            landed = slot(t, chip, 1 - c)
            _remote(landed, landed, send_sems, recv_sems, 6 * t + 3 + j, (x, y, c)).wait_recv()
    for cp in _gather_sends(ins, outs, send_sems, recv_sems) + passed:
        cp.wait_send()


def _swap_halves(gs, tag):
    n = len(gs)

    def body(*refs):
        ins, outs, send_sems, recv_sems = refs[:n], refs[n:2 * n], refs[2 * n], refs[2 * n + 1]
        _swap_start(ins, outs, send_sems, recv_sems)
        _swap_finish(ins, outs, send_sems, recv_sems)

    return pl.pallas_call(
        body, name="grad_swap_halves_" + tag,
        out_shape=_swap_shapes(gs), in_specs=[ANY_SPEC] * n, out_specs=[ANY_SPEC] * n,
        scratch_shapes=_swap_sems(n),
    )(*gs)


def _swap_shapes(gs):
    return [jax.ShapeDtypeStruct(g.shape[:1] + g.shape[2:], g.dtype) for g in gs]


def _swap_sems(n):
    return [pltpu.SemaphoreType.DMA((n,)), pltpu.SemaphoreType.DMA((n,))]


def _swap_copies(ins, outs, send_sems, recv_sems):
    x, y, c, _ = _place()
    return [_remote(ins[t].at[:, 1 - c], outs[t], send_sems, recv_sems, t, (x, y, 1 - c)) for t in range(len(ins))]


def _swap_start(ins, outs, send_sems, recv_sems):
    for cp in _swap_copies(ins, outs, send_sems, recv_sems):
        cp.start()


def _swap_finish(ins, outs, send_sems, recv_sems):
    for cp in _swap_copies(ins, outs, send_sems, recv_sems):
        cp.wait()


def _exchange_chips(pps):
    n = len(pps)

    def body(*refs):
        ins, outs, send_sems, recv_sems = refs[:n], refs[n:2 * n], refs[2 * n], refs[2 * n + 1]
        _exchange_start(ins, outs, send_sems, recv_sems)
        _exchange_finish(ins, outs, send_sems, recv_sems)

    return pl.pallas_call(
        body, name="grad_exchange_chips",
        out_shape=_exchange_shapes(pps), in_specs=[ANY_SPEC] * n, out_specs=[ANY_SPEC] * n,
        scratch_shapes=_exchange_sems(n),
    )(*pps)


def _exchange_shapes(pps):
    return [jax.ShapeDtypeStruct(p.shape, p.dtype) for p in pps]


def _exchange_sems(n):
    return [pltpu.SemaphoreType.DMA((3 * n,)), pltpu.SemaphoreType.DMA((3 * n,))]


def _exchange_sends(ins, outs, send_sems, recv_sems):
    x, y, c, others = _place()
    s = 2 * x + y
    return [_remote(ins[t].at[2 * chip[0] + chip[1]], outs[t].at[s], send_sems, recv_sems, 3 * t + j, (*chip, c))
            for t in range(len(ins)) for j, chip in enumerate(others)]


def _exchange_start(ins, outs, send_sems, recv_sems):
    for cp in _exchange_sends(ins, outs, send_sems, recv_sems):
        cp.start()


def _exchange_finish(ins, outs, send_sems, recv_sems):
    x, y, c, others = _place()
    for t in range(len(ins)):
        for j, chip in enumerate(others):
            landed = outs[t].at[2 * chip[0] + chip[1]]
            _remote(landed, landed, send_sems, recv_sems, 3 * t + j, (x, y, c)).wait_recv()
    for cp in _exchange_sends(ins, outs, send_sems, recv_sems):
        cp.wait_send()


def _share_half(rs):
    n = len(rs)

    def body(*refs):
        ins, outs, send_sems, recv_sems = refs[:n], refs[n:2 * n], refs[2 * n], refs[2 * n + 1]
        x, y, c, _ = _place()
        copies = [_remote(ins[t], outs[t], send_sems, recv_sems, t, (x, y, 1 - c)) for t in range(n)]
        for cp in copies:
            cp.start()
        for cp in copies:
            cp.wait()

    return pl.pallas_call(
        body, name="grad_share_half",
        out_shape=[jax.ShapeDtypeStruct(r.shape, r.dtype) for r in rs],
        in_specs=[ANY_SPEC] * n, out_specs=[ANY_SPEC] * n,
        scratch_shapes=[pltpu.SemaphoreType.DMA((n,)), pltpu.SemaphoreType.DMA((n,))],
    )(*rs)


ELEMENTWISE_BLOCK_BYTES = 1 << 20


def _row_tile(rows, cols):
    return _tile(rows, max(8, ELEMENTWISE_BLOCK_BYTES // (4 * cols) // 8 * 8))


def _add_my_half(g, sib, c_idx, name):
    rh, cols = g.shape[2:]
    tr = _row_tile(rh, cols)

    def body(c_ref, g_ref, s_ref, o_ref):
        o_ref[...] = (g_ref[...] + s_ref[...]).astype(BF16)

    return pl.pallas_call(
        body, name="grad_add_halves_" + name,
        grid_spec=pltpu.PrefetchScalarGridSpec(
            num_scalar_prefetch=1, grid=(N_CHIPS, rh // tr),
            in_specs=[pl.BlockSpec((None, None, tr, cols), lambda s, i, c: (s, c[0], i, 0)),
                      pl.BlockSpec((None, tr, cols), lambda s, i, c: (s, i, 0))],
            out_specs=pl.BlockSpec((None, tr, cols), lambda s, i, c: (s, i, 0))),
        out_shape=jax.ShapeDtypeStruct((N_CHIPS, rh, cols), BF16),
        compiler_params=_cp(2),
    )(c_idx, g, sib)


def _sum_chips(parts, pp, s_idx, name):
    rh, cols = parts.shape[1:]
    tr = _row_tile(rh, cols)

    def body(s_ref, p0, p1, p2, p3, mine_ref, o_ref):
        own = mine_ref[...]
        t = [jnp.where(s_ref[0] == k, own, p[...]).astype(F32) for k, p in enumerate((p0, p1, p2, p3))]
        o_ref[...] = ((t[0] + t[1]) + t[2]) + t[3]

    slot = lambda k: pl.BlockSpec((None, tr, cols), lambda i, s: (jnp.where(s[0] == k, (k + 1) % N_CHIPS, k), i, 0))
    return pl.pallas_call(
        body, name="grad_sum_chips_" + name,
        grid_spec=pltpu.PrefetchScalarGridSpec(
            num_scalar_prefetch=1, grid=(rh // tr,),
            in_specs=[slot(0), slot(1), slot(2), slot(3), pl.BlockSpec((None, tr, cols), lambda i, s: (s[0], i, 0))],
            out_specs=pl.BlockSpec((tr, cols), lambda i, s: (i, 0))),
        out_shape=jax.ShapeDtypeStruct((rh, cols), F32),
        compiler_params=_cp(1),
    )(s_idx, parts, parts, parts, parts, pp)


def _adamw(w, g_mine, g_sib, m, v, c_idx, name):
    rows, cols = w.shape
    tr = _row_tile(rows // 2, cols)
    nbh = rows // 2 // tr

    def body(c_ref, w_ref, gm_ref, gs_ref, m_ref, v_ref, g_ref, d_ref, mo_ref, vo_ref):
        g = jnp.where(pl.program_id(0) // nbh == c_ref[0], gm_ref[...], gs_ref[...])
        g_ref[...] = g
        d_ref[...], mo_ref[...], vo_ref[...] = _adamw_math(w_ref[...], g, m_ref[...], v_ref[...])

    spec = pl.BlockSpec((tr, cols), lambda i, c: (i, 0))
    half = pl.BlockSpec((tr, cols), lambda i, c: (i % nbh, 0))
    return pl.pallas_call(
        body, name="adamw_" + name,
        grid_spec=pltpu.PrefetchScalarGridSpec(
            num_scalar_prefetch=1, grid=(rows // tr,),
            in_specs=[spec, half, half, spec, spec], out_specs=[spec] * 4),
        out_shape=[jax.ShapeDtypeStruct(w.shape, F32)] * 4,
        compiler_params=_cp(1),
    )(c_idx, w, g_mine, g_sib, m, v)


def kernel(x, p, ffn1_w_in, ffn1_w_out, ln1_g, ln1_b, w_mix_in, b_forget, conv_w, g_attn, g_conv, w_mix_out, ln2_g, ln2_b, ffn2_w_in, ffn2_w_out, ln3_g, ln3_b, w_ple, w_ple_gate, b_ple_gate, ln4_g, ln4_b, loss_target, m_ffn1_w_in, m_ffn1_w_out, m_ln1_g, m_ln1_b, m_w_mix_in, m_b_forget, m_conv_w, m_g_attn, m_g_conv, m_w_mix_out, m_ln2_g, m_ln2_b, m_ffn2_w_in, m_ffn2_w_out, m_ln3_g, m_ln3_b, m_w_ple, m_w_ple_gate, m_b_ple_gate, m_ln4_g, m_ln4_b, v_ffn1_w_in, v_ffn1_w_out, v_ln1_g, v_ln1_b, v_w_mix_in, v_b_forget, v_conv_w, v_g_attn, v_g_conv, v_w_mix_out, v_ln2_g, v_ln2_b, v_ffn2_w_in, v_ffn2_w_out, v_ln3_g, v_ln3_b, v_w_ple, v_w_ple_gate, v_b_ple_gate, v_ln4_g, v_ln4_b):
    args = dict(locals())
    shard = {n: args[n][0] if LAYOUT[n][1] is not None else args[n] for n in WEIGHTS}
    m_shard = {n: args["m_" + n][0] if LAYOUT[n][1] is not None else args["m_" + n] for n in WEIGHTS}
    v_shard = {n: args["v_" + n][0] if LAYOUT[n][1] is not None else args["v_" + n] for n in WEIGHTS}
    c_idx = lax.axis_index("c").astype(jnp.int32).reshape(1)
    chip = (2 * lax.axis_index("x") + lax.axis_index("y")).astype(jnp.int32)

    conv_rows = SMALL_ROWS - shard["conv_w"].shape[0]
    mine = {n: _halves(shard[n].astype(BF16)) for n in BIG}
    mine["conv_w"] = _halves(jnp.pad(shard["conv_w"], ((0, conv_rows), (0, 0))))
    early_w = ["ffn1_w_in", "ffn1_w_out"]
    late_w = [n for n in BIG if n not in early_w] + ["conv_w"]

    def full_weights(names, gathered):
        out = {}
        for n, theirs in zip(names, gathered):
            g = lax.dynamic_update_slice(theirs, mine[n][None], (chip, 0, 0, 0))
            if n == "conv_w":
                out[n] = _join_chips(n, g.reshape(N_CHIPS, SMALL_ROWS, CONV_SHARD)[:, :3])
            else:
                out[n] = _join_chips(n, g.reshape((N_CHIPS,) + _shard_shape(n)))
        return out

    full = full_weights(early_w, _all_gather([mine[n] for n in early_w]))
    full.update({n: shard[n] for n in SMALL if n != "conv_w"})

    def per_chip(names, grads):
        by_chip = lambda n: grads[n] if grads[n].ndim == 3 else _split_chips(n, grads[n])
        return [_halves(_pack_small_grads(grads) if n == "small" else by_chip(n)) for n in names]

    def add_halves(names, mine_, sibs):
        return [_add_my_half(g, sib, c_idx, n) for n, g, sib in zip(names, mine_, sibs)]

    def chip_sums(names, grads):
        mine_ = per_chip(names, grads)
        return add_halves(names, mine_, _swap_halves(mine_, names[0]))

    ready_a = ["ffn2_w_in", "ffn2_w_out", "w_ple", "w_ple_gate"]
    ready_b = ["w_mix_in", "w_mix_out"]
    early_g = ready_a + ready_b
    late_g = early_w + ["small"]
    loss_acc, grad_x, grads, early_sums, early_parts = _local_step(
        x[0], p[0, 0], loss_target[0], full,
        overlap={"gather": [mine[n] for n in late_w], "weights": lambda gathered: full_weights(late_w, gathered),
                 "swap": lambda grads: per_chip(ready_a, grads),
                 "chip_sums": lambda grads, swapped, received: (add_halves(ready_a, swapped, received)
                                                                + chip_sums(ready_b, grads))})
    loss = lax.psum(loss_acc[0, 0], ("x", "y", "c"))

    late_sums = chip_sums(late_g, grads)
    names = early_g + late_g
    sums = list(early_sums) + late_sums
    parts = list(early_parts) + list(_exchange_chips(late_sums))
    half_of = {n: _sum_chips(pt, own, chip.reshape(1), n) for n, pt, own in zip(names, parts, sums)}
    names = BIG + ["small"]
    my_half = [half_of[n] for n in names]
    sib_half = _share_half(my_half)

    out = {}
    for n, gm, gs in zip(BIG, my_half, sib_half):
        out[n] = [a[None] for a in _adamw(shard[n], gm, gs, m_shard[n], v_shard[n], c_idx, n)]
    small = _adamw_small(my_half[-1], sib_half[-1], c_idx, [shard[n] for n in SMALL], [m_shard[n] for n in SMALL],
                         [v_shard[n] for n in SMALL])
    for i, n in enumerate(SMALL):
        out[n] = [small[q][i][None] if n == "conv_w" else small[q][i] for q in range(4)]
    return (loss, grad_x[None], *[out[n][q] for q in range(4) for n in WEIGHTS])
```

```python
import functools
import math

import jax
import jax.numpy as jnp
from jax import lax
from jax.experimental import pallas as pl
from jax.experimental.pallas import tpu as pltpu

F32 = jnp.float32
BF16 = jnp.bfloat16

D_MODEL = 1024
D_FF = 2816
N_HEADS = 8
HEAD_DIM = 64
D_ATTN = N_HEADS * HEAD_DIM
D_CONV = 512
PLE_DIM = 256
N_FLOG = 128
ALPHA = 2.0 ** 0.25
LN_EPS = 1e-5
RMS_EPS = 1e-6
NEG_INF = -1e30
Q_SCALE = 1.0 / math.sqrt(HEAD_DIM)
LOG2E = math.log2(math.e)

ADAM_LR = 0.001
ADAM_B1 = 0.9
ADAM_B2 = 0.999
ADAM_EPS = 1e-08
ADAM_WD = 0.01
ADAM_STEP = 10

V7X_VMEM_BYTES = 64 << 20
VMEM_LIMIT = V7X_VMEM_BYTES - (8 << 20)
LANE = 128
FF_CHUNK = 256
N_CHIPS = 4
MESH = pl.DeviceIdType.MESH


def _cp(n_axes):
    return pltpu.CompilerParams(dimension_semantics=("arbitrary",) * n_axes, vmem_limit_bytes=VMEM_LIMIT)


def _resident(shape):
    n = len(shape)
    return pl.BlockSpec(shape, lambda *_: (0,) * n, pipeline_mode=pl.Buffered(1))


def _nn(a, b):
    return jnp.dot(a, b, preferred_element_type=F32)


def _nt(a, b):
    return lax.dot_general(a, b, (((1,), (1,)), ((), ())), preferred_element_type=F32)


def _tn(a, b):
    return lax.dot_general(a, b, (((0,), (0,)), ((), ())), preferred_element_type=F32)


def _ln_stats(r):
    mu = jnp.mean(r, axis=-1, keepdims=True)
    xc = r - mu
    var = jnp.mean(xc * xc, axis=-1, keepdims=True)
    rstd = lax.rsqrt(var + LN_EPS)
    return xc * rstd, rstd


def _ln_bwd(dy, xhat, rstd, g):
    dxh = dy * g
    m1 = jnp.mean(dxh, axis=-1, keepdims=True)
    m2 = jnp.mean(dxh * xhat, axis=-1, keepdims=True)
    return rstd * (dxh - m1 - xhat * m2)


def _sigmoid(z):
    return 1.0 / (1.0 + jnp.exp(-z))


def _rowsum(a):
    return jnp.sum(a, axis=0, keepdims=True)


def _tile(total, want):
    if total <= want:
        return total
    for t in range(want - want % 8, 0, -8):
        if total % t == 0:
            return t
    raise ValueError((total, want))


def _ffn_fwd(x, w_in, w_out, lg, lb, name, gather=()):
    T = x.shape[0]
    tm = _tile(T, 512)
    nf = D_FF // FF_CHUNK
    ng = len(gather)
    last = T // tm - 1

    def body(x_ref, wi_ref, wo_ref, lg_ref, lb_ref, *rest):
        comm_in, (xo_ref, r_ref, g_ref, u_ref, h_ref) = rest[:ng], rest[ng:ng + 5]
        comm_out, sems = rest[ng + 5:2 * ng + 5], rest[2 * ng + 5:]
        if ng:
            @pl.when(pl.program_id(0) == 0)
            def _():
                _gather_start(comm_in, comm_out, *sems)

        xf = x_ref[...]
        xb = xf.astype(BF16)
        acc = jnp.zeros((tm, D_MODEL), F32)
        for j in range(nf):
            c0 = j * FF_CHUNK
            g = _nn(xb, wi_ref[:, c0:c0 + FF_CHUNK])
            u = _nn(xb, wi_ref[:, D_FF + c0:D_FF + c0 + FF_CHUNK])
            hb = (g * _sigmoid(g) * u).astype(BF16)
            g_ref[:, c0:c0 + FF_CHUNK] = g.astype(BF16)
            u_ref[:, c0:c0 + FF_CHUNK] = u.astype(BF16)
            h_ref[:, c0:c0 + FF_CHUNK] = hb
            acc = acc + _nn(hb, wo_ref[c0:c0 + FF_CHUNK, :])
        r = ALPHA * xf + 0.5 * acc
        r_ref[...] = r
        xhat, _ = _ln_stats(r)
        xo_ref[...] = xhat * lg_ref[...] + lb_ref[...]
        if ng:
            @pl.when(pl.program_id(0) == last)
            def _():
                _gather_finish(comm_in, comm_out, *sems)

    row = lambda n: pl.BlockSpec((tm, n), lambda i: (i, 0))
    return pl.pallas_call(
        body, name=name, grid=(T // tm,),
        in_specs=[row(D_MODEL), _resident((D_MODEL, 2 * D_FF)), _resident((D_FF, D_MODEL)),
                  _resident((1, D_MODEL)), _resident((1, D_MODEL))] + [ANY_SPEC] * ng,
        out_specs=[row(D_MODEL), row(D_MODEL), row(D_FF), row(D_FF), row(D_FF)] + [ANY_SPEC] * ng,
        out_shape=[jax.ShapeDtypeStruct((T, D_MODEL), F32), jax.ShapeDtypeStruct((T, D_MODEL), F32),
                   jax.ShapeDtypeStruct((T, D_FF), BF16), jax.ShapeDtypeStruct((T, D_FF), BF16),
                   jax.ShapeDtypeStruct((T, D_FF), BF16)] + _gather_shapes(gather),
        scratch_shapes=_gather_sems(ng) if ng else [],
        compiler_params=_cp(1),
    )(x, w_in, w_out, lg, lb, *gather)


def _ffn_bwd(dxo, r, g, u, w_in, w_out, lg, name, exchange=()):
    T = r.shape[0]
    tm = _tile(T, 256)
    nf = D_FF // FF_CHUNK
    ne = len(exchange)
    last = T // tm - 1

    def body(dxo_ref, r_ref, g_ref, u_ref, wi_ref, wo_ref, lg_ref, *rest):
        comm_in, (dx_ref, dgu_ref, df_ref, dlg_ref, dlb_ref) = rest[:ne], rest[ne:ne + 5]
        comm_out, sems = rest[ne + 5:2 * ne + 5], rest[2 * ne + 5:]
        i = pl.program_id(0)
        if ne:
            @pl.when(i == 0)
            def _():
                _exchange_start(comm_in, comm_out, *sems)

        dy = dxo_ref[...]
        xhat, rstd = _ln_stats(r_ref[...])
        dr = _ln_bwd(dy, xhat, rstd, lg_ref[...])

        @pl.when(i == 0)
        def _():
            dlg_ref[...] = jnp.zeros_like(dlg_ref)
            dlb_ref[...] = jnp.zeros_like(dlb_ref)

        dlg_ref[...] += _rowsum(dy * xhat)
        dlb_ref[...] += _rowsum(dy)
        dfb = (0.5 * dr).astype(BF16)
        df_ref[...] = dfb
        acc = jnp.zeros((tm, D_MODEL), F32)
        dh_ahead = _nt(dfb, wo_ref[0:FF_CHUNK, :])
        for j in range(nf):
            c0 = j * FF_CHUNK
            dh = dh_ahead
            if j + 1 < nf:
                dh_ahead = _nt(dfb, wo_ref[c0 + FF_CHUNK:c0 + 2 * FF_CHUNK, :])
            gg = g_ref[:, c0:c0 + FF_CHUNK].astype(F32)
            uu = u_ref[:, c0:c0 + FF_CHUNK].astype(F32)
            s = _sigmoid(gg)
            dgb = (dh * uu * s * (1.0 + gg * (1.0 - s))).astype(BF16)
            dub = (dh * gg * s).astype(BF16)
            dgu_ref[:, c0:c0 + FF_CHUNK] = dgb
            dgu_ref[:, D_FF + c0:D_FF + c0 + FF_CHUNK] = dub
            acc = acc + _nt(dgb, wi_ref[:, c0:c0 + FF_CHUNK]) + _nt(dub, wi_ref[:, D_FF + c0:D_FF + c0 + FF_CHUNK])
        dx_ref[...] = ALPHA * dr + acc
        if ne:
            @pl.when(i == last)
            def _():
                _exchange_finish(comm_in, comm_out, *sems)

    row = lambda n: pl.BlockSpec((tm, n), lambda i: (i, 0))
    return pl.pallas_call(
        body, name=name, grid=(T // tm,),
        in_specs=[row(D_MODEL), row(D_MODEL), row(D_FF), row(D_FF), _resident((D_MODEL, 2 * D_FF)),
                  _resident((D_FF, D_MODEL)), _resident((1, D_MODEL))] + [ANY_SPEC] * ne,
        out_specs=[row(D_MODEL), row(2 * D_FF), row(D_MODEL), _resident((1, D_MODEL)), _resident((1, D_MODEL))]
        + [ANY_SPEC] * ne,
        out_shape=[jax.ShapeDtypeStruct((T, D_MODEL), F32), jax.ShapeDtypeStruct((T, 2 * D_FF), BF16),
                   jax.ShapeDtypeStruct((T, D_MODEL), BF16), jax.ShapeDtypeStruct((1, D_MODEL), F32),
                   jax.ShapeDtypeStruct((1, D_MODEL), F32)] + _exchange_shapes(exchange),
        scratch_shapes=_exchange_sems(ne) if ne else [],
        compiler_params=_cp(1),
    )(dxo, r, g, u, w_in, w_out, lg, *exchange)


def _matmul_tn(a, b, name, by_chip=False, exchange=()):
    T, K = a.shape
    N = b.shape[1]
    tt = _tile(T, 1024)
    tn = N // N_CHIPS if by_chip else N
    while K * tn * 4 > (6 << 20) and tn % 256 == 0 and not by_chip:
        tn //= 2
    assert N % tn == 0
    ne = len(exchange)
    grid = (N // tn, T // tt)

    def body(a_ref, b_ref, *rest):
        comm_in, o_ref, comm_out, sems = rest[:ne], rest[ne], rest[ne + 1:2 * ne + 1], rest[2 * ne + 1:]
        n, t = pl.program_id(0), pl.program_id(1)
        if ne:
            @pl.when((n == 0) & (t == 0))
            def _():
                _exchange_start(comm_in, comm_out, *sems)

        @pl.when(t == 0)
        def _():
            o_ref[...] = jnp.zeros_like(o_ref)

        o_ref[...] += _tn(a_ref[...].astype(BF16), b_ref[...].astype(BF16))
        if ne:
            @pl.when((n == grid[0] - 1) & (t == grid[1] - 1))
            def _():
                _exchange_finish(comm_in, comm_out, *sems)

    res = pl.pallas_call(
        body, name=name, grid=grid,
        in_specs=[pl.BlockSpec((tt, K), lambda n, t: (t, 0)), pl.BlockSpec((tt, tn), lambda n, t: (t, n))]
        + [ANY_SPEC] * ne,
        out_specs=[pl.BlockSpec((None, K, tn), lambda n, t: (n, 0, 0)) if by_chip
                   else pl.BlockSpec((K, tn), lambda n, t: (0, n))] + [ANY_SPEC] * ne,
        out_shape=[jax.ShapeDtypeStruct((N_CHIPS, K, tn) if by_chip else (K, N), F32)] + _exchange_shapes(exchange),
        scratch_shapes=_exchange_sems(ne) if ne else [],
        compiler_params=_cp(2),
    )(a, b, *exchange)
    return res if ne else res[0]


def _matmul_tokens(at, b, name):
    M, T = at.shape
    N = b.shape[1]
    tt = _tile(T, 1024)

    def body(a_ref, b_ref, o_ref):
        @pl.when(pl.program_id(0) == 0)
        def _():
            o_ref[...] = jnp.zeros_like(o_ref)

        o_ref[...] += _nn(a_ref[...].astype(BF16), b_ref[...].astype(BF16))

    return pl.pallas_call(
        body, name=name, grid=(T // tt,),
        in_specs=[pl.BlockSpec((M, tt), lambda t: (0, t)), pl.BlockSpec((tt, N), lambda t: (t, 0))],
        out_specs=pl.BlockSpec((M, N), lambda t: (0, 0)),
        out_shape=jax.ShapeDtypeStruct((M, N), F32),
        compiler_params=_cp(1),
    )(at, b)


def _matmul_nn(x, w, scale, out_dtype, name, also_transposed=False):
    T, K = x.shape
    N = w.shape[1]
    tm = _tile(T, 512)

    def body(x_ref, w_ref, s_ref, o_ref, *ot_ref):
        res = _nn(x_ref[...].astype(BF16), w_ref[...]) * s_ref[...]
        o_ref[...] = res.astype(out_dtype)
        if also_transposed:
            ot_ref[0][...] = res.T.astype(out_dtype)

    res = pl.pallas_call(
        body, name=name, grid=(T // tm,),
        in_specs=[pl.BlockSpec((tm, K), lambda i: (i, 0)), _resident((K, N)), _resident((1, N))],
        out_specs=[pl.BlockSpec((tm, N), lambda i: (i, 0))] + [pl.BlockSpec((N, tm), lambda i: (0, i))] * also_transposed,
        out_shape=[jax.ShapeDtypeStruct((T, N), out_dtype)] + [jax.ShapeDtypeStruct((N, T), out_dtype)] * also_transposed,
        compiler_params=_cp(1),
    )(x, w, scale)
    return res if also_transposed else res[0]


def _log_sigmoid(z):
    return jnp.minimum(z, 0.0) - jnp.log1p(jnp.exp(-jnp.abs(z)))


def _tri(n, lower):
    r = lax.broadcasted_iota(jnp.int32, (n, n), 0)
    c = lax.broadcasted_iota(jnp.int32, (n, n), 1)
    return jnp.where((c <= r) if lower else (c >= r), 1.0, 0.0).astype(F32)


def _f32dot(a, b):
    return jnp.dot(a, b, preferred_element_type=F32, precision=lax.Precision.HIGHEST)


def _forget_cumsum(flog, col, bf):
    T = flog.shape[0]
    bt = _tile(T, 512)

    def body(f_ref, b_ref, c_ref, carry):
        @pl.when(pl.program_id(0) == 0)
        def _():
            carry[...] = jnp.zeros_like(carry)

        lf = _log_sigmoid(f_ref[...] + b_ref[...])
        c = _f32dot(_tri(bt, True), lf) + carry[...]
        c_ref[...] = c * LOG2E
        carry[...] = c[bt - 1:bt, :]

    return pl.pallas_call(
        body, name="forget_cumsum", grid=(T // bt,),
        in_specs=[pl.BlockSpec((bt, N_FLOG), lambda i: (i, col)), _resident((1, N_FLOG))],
        out_specs=pl.BlockSpec((bt, N_FLOG), lambda i: (i, 0)),
        out_shape=jax.ShapeDtypeStruct((T, N_FLOG), F32),
        scratch_shapes=[pltpu.VMEM((1, N_FLOG), F32)],
        compiler_params=_cp(1),
    )(flog, bf)


def _forget_bwd(dck, dcq, flog, col, bf):
    T = dcq.shape[0]
    bt = _tile(T, 512)
    nb = T // bt

    def body(k0_ref, k1_ref, k2_ref, k3_ref, dcq_ref, f_ref, b_ref, dz_ref, db_ref, carry):
        @pl.when(pl.program_id(0) == 0)
        def _():
            carry[...] = jnp.zeros_like(carry)
            db_ref[...] = jnp.zeros_like(db_ref)

        dc = ((k0_ref[...] + k1_ref[...]) + (k2_ref[...] + k3_ref[...])) + dcq_ref[...]
        dlf = _f32dot(_tri(bt, False), dc) + carry[...]
        carry[...] = dlf[0:1, :]
        z = f_ref[...] + b_ref[...]
        dz = dlf * _sigmoid(-z)
        dz_ref[...] = dz.astype(BF16)
        db_ref[...] += _rowsum(dz)

    slab = lambda j: pl.BlockSpec((None, bt, N_FLOG), lambda i: (j, nb - 1 - i, 0))
    return pl.pallas_call(
        body, name="forget_bwd", grid=(nb,),
        in_specs=[slab(0), slab(1), slab(2), slab(3),
                  pl.BlockSpec((bt, N_FLOG), lambda i: (nb - 1 - i, 0)),
                  pl.BlockSpec((bt, N_FLOG), lambda i: (nb - 1 - i, col)), _resident((1, N_FLOG))],
        out_specs=[pl.BlockSpec((bt, N_FLOG), lambda i: (nb - 1 - i, 0)), _resident((1, N_FLOG))],
        out_shape=[jax.ShapeDtypeStruct((T, N_FLOG), BF16), jax.ShapeDtypeStruct((1, N_FLOG), F32)],
        scratch_shapes=[pltpu.VMEM((1, N_FLOG), F32)],
        compiler_params=_cp(1),
    )(dck, dck, dck, dck, dcq, flog, bf)


def _head_masks():
    lane = lax.broadcasted_iota(jnp.int32, (1, LANE), 1)
    return lane < HEAD_DIM


def _split_heads(x2, is_a):
    zero = jnp.zeros_like(x2)
    return jnp.where(is_a, x2, zero), jnp.where(is_a, zero, x2)


BIAS_PARTS = 3
ATTN_GROUP = 4


def _bias_lanes(h):
    lane = lax.broadcasted_iota(jnp.int32, (1, LANE), 1)
    first = (1 - h) * HEAD_DIM
    return lane, first


def _fold_key_bias(qkv, c):
    T = qkv.shape[0]
    tm = _tile(T, 512)
    npair = N_HEADS // 2

    def body(k_ref, c_ref, o_ref):
        cc = c_ref[...]
        parts, rest = [], cc
        for _ in range(BIAS_PARTS):
            piece = rest.astype(BF16)
            parts.append(piece)
            rest = rest - piece.astype(F32)
        for j in range(npair):
            k2 = k_ref[:, j * LANE:(j + 1) * LANE]
            for h in range(2):
                lane, first = _bias_lanes(h)
                out = k2
                for n, piece in enumerate(parts):
                    col = piece[:, 2 * j + h:2 * j + h + 1]
                    out = jnp.where(lane == first + n, col, out)
                o_ref[:, (2 * j + h) * LANE:(2 * j + h + 1) * LANE] = out

    return pl.pallas_call(
        body, name="fold_key_bias", grid=(T // tm,),
        in_specs=[pl.BlockSpec((tm, D_ATTN), lambda i: (i, 1)), pl.BlockSpec((tm, N_FLOG), lambda i: (i, 0))],
        out_specs=pl.BlockSpec((tm, 2 * D_ATTN), lambda i: (i, 0)),
        out_shape=jax.ShapeDtypeStruct((T, 2 * D_ATTN), BF16),
        compiler_params=_cp(1),
    )(qkv, c)


def _attn_fwd(qkv, vt, kb):
    T = qkv.shape[0]
    tq = _tile(T, 512)
    tk = tq
    nq = T // tq
    npair = N_HEADS // 2

    def body(q_ref, ka_ref, kb_ref, vt_ref, o_ref, al_ref, m_s, l_s, acc_s):
        i = pl.program_id(1)
        qs = []
        for h, qh in enumerate(_split_heads(q_ref[...], _head_masks())):
            lane, first = _bias_lanes(h)
            qs.append(jnp.where((lane >= first) & (lane < first + BIAS_PARTS), -1.0, qh).astype(BF16))
        k_refs = (ka_ref, kb_ref)
        m_s[...] = jnp.full_like(m_s, NEG_INF)
        l_s[...] = jnp.zeros_like(l_s)
        acc_s[...] = jnp.zeros_like(acc_s)

        def scores_at(kk):
            k0 = pl.multiple_of(kk * tk, tk)
            return tuple(_nt(k_refs[h][pl.ds(k0, tk), :], qs[h]) for h in range(2))

        def consume(kk, scores, masked):
            k0 = pl.multiple_of(kk * tk, tk)
            v2t = vt_ref[:, pl.ds(k0, tk)]
            for h in range(2):
                zt = scores[h]
                if masked:
                    rr = lax.broadcasted_iota(jnp.int32, (tk, tq), 0)
                    cc = lax.broadcasted_iota(jnp.int32, (tk, tq), 1)
                    zt = jnp.where(cc >= rr, zt, NEG_INF)
                m_old = m_s[h]
                m_new = jnp.maximum(m_old, jnp.max(zt, axis=0, keepdims=True))
                p = jnp.exp2(zt - m_new)
                a = jnp.exp2(m_old - m_new)
                l_s[h] = a * l_s[h] + jnp.sum(p, axis=0, keepdims=True)
                acc_s[h] = a * acc_s[h] + _nn(v2t, p.astype(BF16))
                m_s[h] = m_new

        def group(kk, n, last_masked):
            scores = [scores_at(kk + u) for u in range(n)]
            for u in range(n):
                consume(kk + u, scores[u], last_masked and u == n - 1)

        def loop_body(t, carry):
            group(ATTN_GROUP * t, ATTN_GROUP, False)
            return carry

        lax.fori_loop(0, i // ATTN_GROUP, loop_body, 0)
        for left in range(ATTN_GROUP):
            @pl.when(i % ATTN_GROUP == left)
            def _():
                group(i - left, left + 1, True)

        outs = []
        for h in range(2):
            l = l_s[h]
            outs.append(acc_s[h] * (1.0 / l))
            al_ref[0, h:h + 1, :] = -(m_s[h] + jnp.log2(l))
        al_ref[0, 2:8, :] = jnp.zeros((6, tq), F32)
        dim = lax.broadcasted_iota(jnp.int32, (LANE, 1), 0)
        o_ref[...] = jnp.where(dim < HEAD_DIM, outs[0], outs[1]).T

    rowl = pl.BlockSpec((1, 8, tq), lambda j, i: (j, 0, i))
    return pl.pallas_call(
        body, name="attn_fwd", grid=(npair, nq),
        in_specs=[pl.BlockSpec((tq, LANE), lambda j, i: (i, j)),
                  pl.BlockSpec((T, LANE), lambda j, i: (0, 2 * j), pipeline_mode=pl.Buffered(1)),
                  pl.BlockSpec((T, LANE), lambda j, i: (0, 2 * j + 1), pipeline_mode=pl.Buffered(1)),
                  pl.BlockSpec((LANE, T), lambda j, i: (2 * npair + j, 0), pipeline_mode=pl.Buffered(1))],
        out_specs=[pl.BlockSpec((tq, LANE), lambda j, i: (i, j)), rowl],
        out_shape=[jax.ShapeDtypeStruct((T, D_ATTN), F32), jax.ShapeDtypeStruct((npair, 8, T), F32)],
        scratch_shapes=[pltpu.VMEM((2, 1, tq), F32), pltpu.VMEM((2, 1, tq), F32), pltpu.VMEM((2, LANE, tq), F32)],
        compiler_params=_cp(2),
    )(qkv, kb, kb, vt)


def _attn_bwd(qkv, qkvt, dob, dobt, cb, alrow, dlrow):
    T = qkv.shape[0]
    tq = _tile(T, 512)
    tk = tq
    nq = T // tq
    npair = N_HEADS // 2

    def body(q_ref, qt_ref, k_ref, kt_ref, v_ref, do_ref, dot_ref, cb_ref, al_ref, dl_ref,
             dq_ref, dk_ref, dv_ref, dc_ref, dcq_ref, dk_s, dv_s, dc_s):
        kj = pl.program_id(1)
        is_a = _head_masks()
        ks = _split_heads(k_ref[...], is_a)
        vs = _split_heads(v_ref[...], is_a)
        dim_a = lax.broadcasted_iota(jnp.int32, (LANE, 1), 0) < HEAD_DIM
        kts = _split_heads(kt_ref[...], dim_a)
        head_lane = lax.broadcasted_iota(jnp.int32, (1, LANE), 1) - 2 * pl.program_id(0)
        cs = tuple(jnp.sum(jnp.where(head_lane == h, cb_ref[...], 0.0), axis=-1, keepdims=True) for h in range(2))

        @pl.when(kj == 0)
        def _():
            dq_ref[...] = jnp.zeros_like(dq_ref)
            dcq_ref[...] = jnp.zeros_like(dcq_ref)

        dk_s[...] = jnp.zeros_like(dk_s)
        dv_s[...] = jnp.zeros_like(dv_s)
        dc_s[...] = jnp.zeros_like(dc_s)

        def step(qi, masked):
            q0 = pl.multiple_of(qi * tq, tq)
            q2 = q_ref[pl.ds(q0, tq), :]
            do2 = do_ref[pl.ds(q0, tq), :]
            qt2 = qt_ref[:, pl.ds(q0, tq)]
            dot2 = dot_ref[:, pl.ds(q0, tq)]
            for h in range(2):
                alr = al_ref[0, h:h + 1, pl.ds(q0, tq)]
                dlr = dl_ref[0, h:h + 1, pl.ds(q0, tq)]
                zt = _nt(ks[h], q2) + (alr - cs[h])
                if masked:
                    rr = lax.broadcasted_iota(jnp.int32, (tk, tq), 0)
                    cc = lax.broadcasted_iota(jnp.int32, (tk, tq), 1)
                    zt = jnp.where(cc >= rr, zt, NEG_INF)
                pt = jnp.exp2(zt)
                dst = pt * (_nt(vs[h], do2) - dlr)
                pb = pt.astype(BF16)
                dsb = dst.astype(BF16)
                dv_s[h] += _nt(dot2, pb)
                dk_s[h] += _nt(qt2, dsb)
                dc_s[h] += jnp.sum(dst, axis=-1, keepdims=True)
                dcq_ref[0, h:h + 1, pl.ds(q0, tq)] += jnp.sum(dst, axis=0, keepdims=True)
                dq_ref[:, pl.ds(q0, tq)] += _nn(kts[h], dsb)

        rest = nq - 1 - kj
        for left in range(ATTN_GROUP):
            @pl.when(rest % ATTN_GROUP == left)
            def _():
                for u in range(left + 1):
                    step(kj + u, u == 0)

        def loop_body(t, carry):
            for u in range(ATTN_GROUP):
                step(kj + 1 + rest % ATTN_GROUP + ATTN_GROUP * t + u, False)
            return carry

        lax.fori_loop(0, rest // ATTN_GROUP, loop_body, 0)
        dk_ref[...] = (jnp.where(dim_a, dk_s[0], dk_s[1]) * (1.0 / LOG2E)).astype(BF16)
        dv_ref[...] = jnp.where(dim_a, dv_s[0], dv_s[1]).astype(BF16)
        lane = lax.broadcasted_iota(jnp.int32, (1, LANE), 1)
        head = 2 * pl.program_id(0)
        dc_ref[...] = jnp.where(lane == head, -dc_s[0], jnp.where(lane == head + 1, -dc_s[1], 0.0))

        @pl.when(kj == nq - 1)
        def _():
            dq_ref[...] = dq_ref[...] * Q_SCALE

    full = lambda col: pl.BlockSpec((T, LANE), lambda j, kj: (0, col(j)), pipeline_mode=pl.Buffered(1))
    fullt = lambda row: pl.BlockSpec((LANE, T), lambda j, kj: (row(j), 0), pipeline_mode=pl.Buffered(1))
    tile = lambda col: pl.BlockSpec((tk, LANE), lambda j, kj: (kj, col(j)))
    tilet = lambda row: pl.BlockSpec((LANE, tk), lambda j, kj: (row(j), kj))
    rowl = pl.BlockSpec((1, 8, T), lambda j, kj: (j, 0, 0))
    return pl.pallas_call(
        body, name="attn_bwd", grid=(npair, nq),
        in_specs=[full(lambda j: j), fullt(lambda j: j), tile(lambda j: npair + j), tilet(lambda j: npair + j),
                  tile(lambda j: 2 * npair + j), full(lambda j: j), fullt(lambda j: j), tile(lambda j: 0), rowl, rowl],
        out_specs=[pl.BlockSpec((LANE, T), lambda j, kj: (j, 0)), tilet(lambda j: j), tilet(lambda j: j),
                   pl.BlockSpec((None, tk, LANE), lambda j, kj: (j, kj, 0)), rowl],
        out_shape=[jax.ShapeDtypeStruct((D_ATTN, T), F32), jax.ShapeDtypeStruct((D_ATTN, T), BF16),
                   jax.ShapeDtypeStruct((D_ATTN, T), BF16), jax.ShapeDtypeStruct((npair, T, LANE), F32),
                   jax.ShapeDtypeStruct((npair, 8, T), F32)],
        scratch_shapes=[pltpu.VMEM((2, LANE, tk), F32), pltpu.VMEM((2, LANE, tk), F32), pltpu.VMEM((2, tk, 1), F32)],
        compiler_params=_cp(2),
    )(qkv, qkvt, qkv, qkvt, qkv, dob, dobt, cb, alrow, dlrow)


HALO = 8


def _shift_rows(cur, other, k, tm, down):
    row = lax.broadcasted_iota(jnp.int32, (tm, 1), 0)
    reps = tm // HALO
    if down:
        rolled = pltpu.roll(cur, k, 0)
        fill = jnp.tile(pltpu.roll(other, k, 0), (reps, 1))
        return jnp.where(row < k, fill, rolled)
    rolled = pltpu.roll(cur, tm - k, 0)
    fill = jnp.tile(pltpu.roll(other, HALO - k, 0), (reps, 1))
    return jnp.where(row >= tm - k, fill, rolled)


def _conv_fwd(c, hh, c_prev, hh_prev, w_ref, first, tm):
    u = c * hh
    u_prev = jnp.where(first, 0.0, c_prev * hh_prev)
    u1 = _shift_rows(u, u_prev, 1, tm, True)
    u2 = _shift_rows(u, u_prev, 2, tm, True)
    y = w_ref[0:1, :] * u2 + w_ref[1:2, :] * u1 + w_ref[2:3, :] * u
    return u, u1, u2, y


def _rms(x, g):
    rs = lax.rsqrt(jnp.mean(x * x, axis=-1, keepdims=True) + RMS_EPS)
    return x * rs * g, rs


def _mixer_tail_fwd(o, bchf, conv_w, g_attn, g_conv, w_mo, x1, lg, lb):
    T = o.shape[0]
    tm = _tile(T, 512)
    hb = tm // HALO

    def body(o_ref, b_ref, c_ref, h_ref, cp_ref, hp_ref, w_ref, ga_ref, gc_ref, wmo_ref, x1_ref, lg_ref, lb_ref,
             x2_ref, r2_ref, mg_ref):
        first = pl.program_id(0) == 0
        _, _, _, y = _conv_fwd(c_ref[...], h_ref[...], cp_ref[...], hp_ref[...], w_ref, first, tm)
        na, _ = _rms(o_ref[...], ga_ref[...])
        nc, _ = _rms(b_ref[...] * y, gc_ref[...])
        nab = na.astype(BF16)
        ncb = nc.astype(BF16)
        mg_ref[:, 0:D_ATTN] = nab
        mg_ref[:, D_ATTN:] = ncb
        r2 = ALPHA * x1_ref[...] + _nn(nab, wmo_ref[0:D_ATTN, :]) + _nn(ncb, wmo_ref[D_ATTN:, :])
        r2_ref[...] = r2
        xhat, _ = _ln_stats(r2)
        x2_ref[...] = xhat * lg_ref[...] + lb_ref[...]

    row = lambda n, col=0: pl.BlockSpec((tm, n), lambda i: (i, col))
    prev = lambda col: pl.BlockSpec((HALO, D_CONV), lambda i: (jnp.maximum(i * hb - 1, 0), col))
    return pl.pallas_call(
        body, name="mixer_tail_fwd", grid=(T // tm,),
        in_specs=[row(D_ATTN), row(D_CONV, 0), row(D_CONV, 1), row(D_CONV, 2), prev(1), prev(2),
                  _resident((3, D_CONV)), _resident((1, D_ATTN)), _resident((1, D_CONV)),
                  _resident((D_MODEL, D_MODEL)), row(D_MODEL), _resident((1, D_MODEL)), _resident((1, D_MODEL))],
        out_specs=[row(D_MODEL), row(D_MODEL), row(D_MODEL)],
        out_shape=[jax.ShapeDtypeStruct((T, D_MODEL), F32), jax.ShapeDtypeStruct((T, D_MODEL), F32),
                   jax.ShapeDtypeStruct((T, D_MODEL), BF16)],
        compiler_params=_cp(1),
    )(o, bchf, bchf, bchf, bchf, bchf, conv_w, g_attn, g_conv, w_mo, x1, lg, lb)


def _head_sum_rows():
    row = lax.broadcasted_iota(jnp.int32, (4 * 8, D_ATTN), 0)
    head = lax.broadcasted_iota(jnp.int32, (4 * 8, D_ATTN), 1) // HEAD_DIM
    return jnp.where((row % 8 < 2) & (2 * (row // 8) + row % 8 == head), 1.0, 0.0).astype(F32)


def _mixer_tail_bwd(dx2, r2, lg, w_mo, o, bchf, conv_w, g_attn, g_conv, swap=()):
    T = o.shape[0]
    tm = _tile(T, 256)
    hb = tm // HALO
    ns = len(swap)
    last = T // tm - 1

    def body(dx2_ref, r2_ref, lg_ref, wmo_ref, o_ref, b_ref, c_ref, h_ref, cp_ref, hp_ref, w_ref, ga_ref, gc_ref,
             *rest):
        comm_in = rest[:ns]
        dx1_ref, dr_ref, do_ref, dot_ref, dl_ref, dco_ref, dlg_ref, dlb_ref, dga_ref, dgc_ref = rest[ns:ns + 10]
        comm_out, sems = rest[ns + 10:2 * ns + 10], rest[2 * ns + 10:]
        i = pl.program_id(0)
        if ns:
            @pl.when(i == 0)
            def _():
                _swap_start(comm_in, comm_out, *sems)

            @pl.when(i == last)
            def _():
                _swap_finish(comm_in, comm_out, *sems)

        @pl.when(i == 0)
        def _():
            for ref in (dlg_ref, dlb_ref, dga_ref, dgc_ref):
                ref[...] = jnp.zeros_like(ref)

        dy = dx2_ref[...]
        xhat, rstd = _ln_stats(r2_ref[...])
        dr = _ln_bwd(dy, xhat, rstd, lg_ref[...])
        dlg_ref[...] += _rowsum(dy * xhat)
        dlb_ref[...] += _rowsum(dy)
        dx1_ref[...] = ALPHA * dr
        drb = dr.astype(BF16)
        dr_ref[...] = drb
        dna = _nt(drb, wmo_ref[0:D_ATTN, :])
        dnc = _nt(drb, wmo_ref[D_ATTN:, :])

        def rms_bwd(x, g, dn):
            rs = lax.rsqrt(jnp.mean(x * x, axis=-1, keepdims=True) + RMS_EPS)
            dng = dn * g
            dx = rs * dng - x * (rs * rs * rs) * jnp.mean(dng * x, axis=-1, keepdims=True)
            return dx, _rowsum(dn * x * rs)

        oo = o_ref[...]
        do, dga = rms_bwd(oo, ga_ref[...], dna)
        dga_ref[...] += dga
        do_ref[...] = do.astype(BF16)
        dot_ref[...] = do.T.astype(BF16)
        dl_ref[...] = lax.dot_general(_head_sum_rows(), do * oo, (((1,), (1,)), ((), ())),
                                      preferred_element_type=F32, precision=lax.Precision.HIGHEST)
        _, _, _, y = _conv_fwd(c_ref[...], h_ref[...], cp_ref[...], hp_ref[...], w_ref, i == 0, tm)
        dco, dgc = rms_bwd(b_ref[...] * y, gc_ref[...], dnc)
        dgc_ref[...] += dgc
        dco_ref[...] = dco

    row = lambda n, col=0: pl.BlockSpec((tm, n), lambda i: (i, col))
    prev = lambda col: pl.BlockSpec((HALO, D_CONV), lambda i: (jnp.maximum(i * hb - 1, 0), col))
    vec = lambda n: _resident((1, n))
    return pl.pallas_call(
        body, name="mixer_tail_bwd", grid=(T // tm,),
        in_specs=[row(D_MODEL), row(D_MODEL), vec(D_MODEL), _resident((D_MODEL, D_MODEL)), row(D_ATTN),
                  row(D_CONV, 0), row(D_CONV, 1), row(D_CONV, 2), prev(1), prev(2), _resident((3, D_CONV)),
                  vec(D_ATTN), vec(D_CONV)] + [ANY_SPEC] * ns,
        out_specs=[row(D_MODEL), row(D_MODEL), row(D_ATTN), pl.BlockSpec((D_ATTN, tm), lambda i: (0, i)),
                   pl.BlockSpec((4 * 8, tm), lambda i: (0, i)), row(D_CONV),
                   vec(D_MODEL), vec(D_MODEL), vec(D_ATTN), vec(D_CONV)] + [ANY_SPEC] * ns,
        out_shape=[jax.ShapeDtypeStruct((T, D_MODEL), F32), jax.ShapeDtypeStruct((T, D_MODEL), BF16),
                   jax.ShapeDtypeStruct((T, D_ATTN), BF16), jax.ShapeDtypeStruct((D_ATTN, T), BF16),
                   jax.ShapeDtypeStruct((4 * 8, T), F32),
                   jax.ShapeDtypeStruct((T, D_CONV), F32), jax.ShapeDtypeStruct((1, D_MODEL), F32),
                   jax.ShapeDtypeStruct((1, D_MODEL), F32), jax.ShapeDtypeStruct((1, D_ATTN), F32),
                   jax.ShapeDtypeStruct((1, D_CONV), F32)] + _swap_shapes(swap),
        scratch_shapes=_swap_sems(ns) if ns else [],
        compiler_params=_cp(1),
    )(dx2, r2, lg, w_mo, o, bchf, bchf, bchf, bchf, bchf, conv_w, g_attn, g_conv, *swap)


def _conv_bwd(dco, bchf, conv_w):
    T = dco.shape[0]
    tm = _tile(T, 512)
    hb = tm // HALO
    nt = T // tm

    def body(dco_ref, dcon_ref, b_ref, bn_ref, c_ref, h_ref, cp_ref, hp_ref, w_ref, dbch_ref, dw_ref):
        i = pl.program_id(0)

        @pl.when(i == 0)
        def _():
            dw_ref[...] = jnp.zeros_like(dw_ref)

        cc = c_ref[...]
        hh = h_ref[...]
        u, u1, u2, y = _conv_fwd(cc, hh, cp_ref[...], hp_ref[...], w_ref, i == 0, tm)
        dco = dco_ref[...]
        bb = b_ref[...]
        dyc = dco * bb
        dy_next = jnp.where(i == nt - 1, 0.0, dcon_ref[...] * bn_ref[...])
        d1 = _shift_rows(dyc, dy_next, 1, tm, False)
        d2 = _shift_rows(dyc, dy_next, 2, tm, False)
        du = w_ref[2:3, :] * dyc + w_ref[1:2, :] * d1 + w_ref[0:1, :] * d2
        dbch_ref[:, 0:D_CONV] = (dco * y).astype(BF16)
        dbch_ref[:, D_CONV:2 * D_CONV] = (du * hh).astype(BF16)
        dbch_ref[:, 2 * D_CONV:] = (du * cc).astype(BF16)
        dw_ref[0:1, :] += _rowsum(dyc * u2)
        dw_ref[1:2, :] += _rowsum(dyc * u1)
        dw_ref[2:3, :] += _rowsum(dyc * u)

    row = lambda n, col=0: pl.BlockSpec((tm, n), lambda i: (i, col))
    prev = lambda col: pl.BlockSpec((HALO, D_CONV), lambda i: (jnp.maximum(i * hb - 1, 0), col))
    nxt = lambda col: pl.BlockSpec((HALO, D_CONV), lambda i: (jnp.minimum((i + 1) * hb, T // HALO - 1), col))
    return pl.pallas_call(
        body, name="conv_bwd", grid=(nt,),
        in_specs=[row(D_CONV), nxt(0), row(D_CONV, 0), nxt(0), row(D_CONV, 1), row(D_CONV, 2), prev(1), prev(2),
                  _resident((3, D_CONV))],
        out_specs=[row(3 * D_CONV), _resident((8, D_CONV))],
        out_shape=[jax.ShapeDtypeStruct((T, 3 * D_CONV), BF16), jax.ShapeDtypeStruct((8, D_CONV), F32)],
        compiler_params=_cp(1),
    )(dco, dco, bchf, bchf, bchf, bchf, bchf, bchf, conv_w)


def _mixer_in_bwd(dx1a, dqt, dkt, dvt, dbch, dfl, w_qkvt, w_bch, w_f):
    T = dx1a.shape[0]
    tm = _tile(T, 512)

    def body(a_ref, dq_ref, dk_ref, dv_ref, db_ref, df_ref, wq_ref, wb_ref, wf_ref, o_ref):
        acc = a_ref[...] + _nt(db_ref[...], wb_ref[...]) + _nt(df_ref[...], wf_ref[...])
        for n, ref in enumerate((dq_ref, dk_ref, dv_ref)):
            acc = acc + _tn(ref[...].astype(BF16), wq_ref[n * D_ATTN:(n + 1) * D_ATTN, :])
        o_ref[...] = acc

    row = lambda n: pl.BlockSpec((tm, n), lambda i: (i, 0))
    col = pl.BlockSpec((D_ATTN, tm), lambda i: (0, i))
    return pl.pallas_call(
        body, name="mixer_in_bwd", grid=(T // tm,),
        in_specs=[row(D_MODEL), col, col, col, row(3 * D_CONV), row(N_FLOG),
                  _resident((3 * D_ATTN, D_MODEL)), _resident((D_MODEL, 3 * D_CONV)), _resident((D_MODEL, N_FLOG))],
        out_specs=row(D_MODEL),
        out_shape=jax.ShapeDtypeStruct((T, D_MODEL), F32),
        compiler_params=_cp(1),
    )(dx1a, dqt, dkt, dvt, dbch, dfl, w_qkvt, w_bch, w_f)


def _ple_loss(x3, p, tgt, w_g, w_p, b_g, lg, lb):
    T = x3.shape[0]
    tm = _tile(T, 512)

    def body(x_ref, p_ref, t_ref, wg_ref, wp_ref, bg_ref, lg_ref, lb_ref,
             dx_ref, de_ref, dz_ref, loss_ref, dlg_ref, dlb_ref, dbg_ref):
        @pl.when(pl.program_id(0) == 0)
        def _():
            for ref in (loss_ref, dlg_ref, dlb_ref, dbg_ref):
                ref[...] = jnp.zeros_like(ref)

        xf = x_ref[...]
        gate = _sigmoid(_nn(xf.astype(BF16), wg_ref[...]) + bg_ref[...])
        e = _nn(p_ref[...].astype(BF16), wp_ref[...])
        xhat, rstd = _ln_stats(ALPHA * xf + gate * e)
        err = xhat * lg_ref[...] + lb_ref[...] - t_ref[...]
        sq = jnp.sum(_rowsum(err * err), axis=-1, keepdims=True)
        loss_ref[...] += jnp.broadcast_to(sq * (0.5 / D_MODEL), loss_ref.shape)
        dy = err * (1.0 / D_MODEL)
        dr = _ln_bwd(dy, xhat, rstd, lg_ref[...])
        dlg_ref[...] += _rowsum(dy * xhat)
        dlb_ref[...] += _rowsum(dy)
        de_ref[...] = (dr * gate).astype(BF16)
        dz = dr * e * gate * (1.0 - gate)
        dbg_ref[...] += _rowsum(dz)
        dzb = dz.astype(BF16)
        dz_ref[...] = dzb
        dx_ref[...] = ALPHA * dr + _nt(dzb, wg_ref[...])

    row = lambda n: pl.BlockSpec((tm, n), lambda i: (i, 0))
    vec = lambda n: _resident((1, n))
    return pl.pallas_call(
        body, name="ple_loss", grid=(T // tm,),
        in_specs=[row(D_MODEL), row(PLE_DIM), row(D_MODEL), _resident((D_MODEL, D_MODEL)),
                  _resident((PLE_DIM, D_MODEL)), vec(D_MODEL), vec(D_MODEL), vec(D_MODEL)],
        out_specs=[row(D_MODEL), row(D_MODEL), row(D_MODEL), vec(LANE), vec(D_MODEL), vec(D_MODEL), vec(D_MODEL)],
        out_shape=[jax.ShapeDtypeStruct((T, D_MODEL), F32), jax.ShapeDtypeStruct((T, D_MODEL), BF16),
                   jax.ShapeDtypeStruct((T, D_MODEL), BF16), jax.ShapeDtypeStruct((1, LANE), F32),
                   jax.ShapeDtypeStruct((1, D_MODEL), F32), jax.ShapeDtypeStruct((1, D_MODEL), F32),
                   jax.ShapeDtypeStruct((1, D_MODEL), F32)],
        compiler_params=_cp(1),
    )(x3, p, tgt, w_g, w_p, b_g, lg, lb)


def _lane_layout(v8):
    return jnp.repeat(v8, HEAD_DIM, axis=1)


def _row_layout(v8):
    t = v8.shape[0]
    return jnp.pad(v8.T.reshape(N_HEADS // 2, 2, t), ((0, 0), (0, 6), (0, 0)))


def _from_lane_layout(vl):
    return vl[:, ::HEAD_DIM]


def _from_row_layout(vr):
    return vr[:, :2, :].reshape(N_HEADS, -1).T


def _local_step(x, p, tgt, w, overlap=None):
    bf = lambda a: a.astype(BF16)
    w1i, w1o = bf(w["ffn1_w_in"]), bf(w["ffn1_w_out"])
    first = _ffn_fwd(x, w1i, w1o, w["ln1_g"], w["ln1_b"], "ffn1_fwd", overlap["gather"] if overlap else ())
    x1, r1, g1, u1, h1 = first[:5]
    if overlap:
        w = {**w, **overlap["weights"](first[5:])}
    w2i, w2o = bf(w["ffn2_w_in"]), bf(w["ffn2_w_out"])
    wmi = w["w_mix_in"]
    o_f = 3 * D_ATTN
    o_b = o_f + N_HEADS
    w_qkv = bf(wmi[:, :o_f])
    w_f = bf(jnp.pad(wmi[:, o_f:o_b], ((0, 0), (0, N_FLOG - N_HEADS))))
    w_bch = bf(wmi[:, o_b:])
    w_bchf = jnp.concatenate([w_bch, w_f], axis=1)
    w_mo, w_g, w_p = bf(w["w_mix_out"]), bf(w["w_ple_gate"]), bf(w["w_ple"])
    b_f = jnp.pad(w["b_forget"], ((0, 0), (0, N_FLOG - N_HEADS)))

    q_scale = jnp.concatenate([jnp.full((1, D_ATTN), Q_SCALE * LOG2E, F32), jnp.ones((1, 2 * D_ATTN), F32)], axis=1)
    qkv, qkvt = _matmul_nn(x1, w_qkv, q_scale, BF16, "proj_qkv", also_transposed=True)
    bchf = _matmul_nn(x1, w_bchf, jnp.ones((1, 3 * D_CONV + N_FLOG), F32), F32, "proj_bchf")
    fcol = 3 * D_CONV // N_FLOG
    c = _forget_cumsum(bchf, fcol, b_f)
    o, alrow = _attn_fwd(qkv, qkvt, _fold_key_bias(qkv, c))
    x2, r2, merged = _mixer_tail_fwd(o, bchf, w["conv_w"], w["g_attn"], w["g_conv"], w_mo, x1, w["ln2_g"], w["ln2_b"])
    x3, r3, g2, u2, h2 = _ffn_fwd(x2, w2i, w2o, w["ln3_g"], w["ln3_b"], "ffn2_fwd")

    grads = {}
    dx3, de, dz, loss, grads["ln4_g"], grads["ln4_b"], grads["b_ple_gate"] = _ple_loss(
        x3, p, tgt, w_g, w_p, w["b_ple_gate"], w["ln4_g"], w["ln4_b"])
    by_chip = overlap is not None
    grads["w_ple"] = _matmul_tn(p, de, "dw_ple", by_chip)
    grads["w_ple_gate"] = _matmul_tn(x3, dz, "dw_ple_gate")

    dx2, dgu2, df2, grads["ln3_g"], grads["ln3_b"] = _ffn_bwd(dx3, r3, g2, u2, w2i, w2o, w["ln3_g"], "ffn2_bwd")
    grads["ffn2_w_in"] = _matmul_tn(x2, dgu2, "dw_ffn2_in", by_chip)
    grads["ffn2_w_out"] = _matmul_tn(h2, df2, "dw_ffn2_out")

    to_swap = overlap["swap"](grads) if overlap else ()
    tail = _mixer_tail_bwd(dx2, r2, w["ln2_g"], w_mo, o, bchf, w["conv_w"], w["g_attn"], w["g_conv"], to_swap)
    (dx1a, dr2, dob, dobt, delta, dco, grads["ln2_g"], grads["ln2_b"], grads["g_attn"], grads["g_conv"]) = tail[:10]
    grads["w_mix_out"] = _matmul_tn(merged, dr2, "dw_mix_out")
    dbch, dcw = _conv_bwd(dco, bchf, w["conv_w"])
    grads["conv_w"] = dcw[:3]
    dqt, dkt, dvt, dck, dcq = _attn_bwd(qkv, qkvt, dob, dobt, c, alrow, delta.reshape(N_HEADS // 2, 8, -1))
    dcq_lanes = jnp.pad(_from_row_layout(dcq), ((0, 0), (0, N_FLOG - N_HEADS)))
    dfl, dbf = _forget_bwd(dck, dcq_lanes, bchf, fcol, b_f)
    grads["b_forget"] = dbf[:, :N_HEADS]
    dx1 = _mixer_in_bwd(dx1a, dqt, dkt, dvt, dbch, dfl, w_qkv.T, w_bch, w_f)
    grads["w_mix_in"] = jnp.concatenate(
        [_matmul_tokens(dqt, x1, "dw_q").T, _matmul_tokens(dkt, x1, "dw_k").T, _matmul_tokens(dvt, x1, "dw_v").T,
         _matmul_tn(x1, dfl, "dw_flog")[:, :N_HEADS], _matmul_tn(x1, dbch, "dw_bch")], axis=1)

    dx0, dgu1, df1, grads["ln1_g"], grads["ln1_b"] = _ffn_bwd(dx1, r1, g1, u1, w1i, w1o, w["ln1_g"], "ffn1_bwd")
    grads["ffn1_w_out"] = _matmul_tn(h1, df1, "dw_ffn1_out")
    if not overlap:
        grads["ffn1_w_in"] = _matmul_tn(x, dgu1, "dw_ffn1_in")
        return loss, dx0, grads
    sums = overlap["chip_sums"](grads, to_swap, tail[10:])
    grads["ffn1_w_in"], *received = _matmul_tn(x, dgu1, "dw_ffn1_in", by_chip, exchange=sums)
    return loss, dx0, grads, sums, received


WEIGHTS = ["ffn1_w_in", "ffn1_w_out", "ln1_g", "ln1_b", "w_mix_in", "b_forget", "conv_w", "g_attn", "g_conv",
           "w_mix_out", "ln2_g", "ln2_b", "ffn2_w_in", "ffn2_w_out", "ln3_g", "ln3_b", "w_ple", "w_ple_gate",
           "b_ple_gate", "ln4_g", "ln4_b"]
LAYOUT = {
    "ffn1_w_in": ((D_MODEL, 2 * D_FF), 1), "ffn1_w_out": ((D_FF, D_MODEL), 0),
    "w_mix_in": ((D_MODEL, 3 * D_ATTN + N_HEADS + 3 * D_CONV), 1), "conv_w": ((3, D_CONV), 1),
    "w_mix_out": ((D_MODEL, D_MODEL), 0), "ffn2_w_in": ((D_MODEL, 2 * D_FF), 1), "ffn2_w_out": ((D_FF, D_MODEL), 0),
    "w_ple": ((PLE_DIM, D_MODEL), 1), "w_ple_gate": ((D_MODEL, D_MODEL), 0),
    "ln1_g": ((1, D_MODEL), None), "ln1_b": ((1, D_MODEL), None), "b_forget": ((1, N_HEADS), None),
    "g_attn": ((1, D_ATTN), None), "g_conv": ((1, D_CONV), None), "ln2_g": ((1, D_MODEL), None),
    "ln2_b": ((1, D_MODEL), None), "ln3_g": ((1, D_MODEL), None), "ln3_b": ((1, D_MODEL), None),
    "b_ple_gate": ((1, D_MODEL), None), "ln4_g": ((1, D_MODEL), None), "ln4_b": ((1, D_MODEL), None),
}
BIG = [n for n in WEIGHTS if LAYOUT[n][1] is not None and n != "conv_w"]
SMALL = [n for n in WEIGHTS if n not in BIG]
ROW = 1024
SMALL_ROWS = 16


def _shard_shape(name):
    shape, axis = LAYOUT[name]
    if axis is None:
        return shape
    return tuple(s // N_CHIPS if a == axis else s for a, s in enumerate(shape))


def _halves(a):
    return a.reshape(a.shape[:-2] + (2, a.shape[-2] // 2, a.shape[-1]))


def _split_chips(name, full):
    shape, axis = LAYOUT[name]
    if axis == 0:
        return full.reshape((N_CHIPS, shape[0] // N_CHIPS) + shape[1:])
    return jnp.moveaxis(full.reshape(shape[:1] + (N_CHIPS, shape[1] // N_CHIPS)), 1, 0)


def _join_chips(name, parts):
    shape, axis = LAYOUT[name]
    if axis == 0:
        return parts.reshape(shape)
    return jnp.moveaxis(parts, 0, 1).reshape(shape)


SMALL_AT = {"ln1_g": (0, 0), "ln1_b": (1, 0), "ln2_g": (2, 0), "ln2_b": (3, 0), "ln3_g": (4, 0), "ln3_b": (5, 0),
            "b_ple_gate": (6, 0), "ln4_g": (7, 0), "ln4_b": (8, 0), "g_attn": (9, 0), "g_conv": (9, D_ATTN),
            "b_forget": (10, 0), "conv_w": (10, LANE)}
CONV_SHARD = D_CONV // N_CHIPS


def _pack_small_grads(grads):
    def body(*refs):
        ins, o_ref = dict(zip(SMALL, refs[:-1])), refs[-1]
        o_ref[...] = jnp.zeros_like(o_ref)
        for s in range(N_CHIPS):
            for n in SMALL:
                r, c0 = SMALL_AT[n]
                if n == "conv_w":
                    for k in range(3):
                        o_ref[s, r:r + 1, c0 + k * CONV_SHARD:c0 + (k + 1) * CONV_SHARD] = (
                            ins[n][k:k + 1, s * CONV_SHARD:(s + 1) * CONV_SHARD])
                else:
                    o_ref[s, r:r + 1, c0:c0 + ins[n].shape[1]] = ins[n][...]

    return pl.pallas_call(
        body, name="pack_small_grads",
        out_shape=jax.ShapeDtypeStruct((N_CHIPS, SMALL_ROWS, ROW), F32),
    )(*[grads[n] for n in SMALL])


def _adamw_math(w, g, m, v):
    c1 = 1.0 - ADAM_B1 ** ADAM_STEP
    c2 = 1.0 - ADAM_B2 ** ADAM_STEP
    m = ADAM_B1 * m + (1.0 - ADAM_B1) * g
    v = ADAM_B2 * v + (1.0 - ADAM_B2) * (g * g)
    return -ADAM_LR * ((m / c1) / (jnp.sqrt(v / c2) + ADAM_EPS) + ADAM_WD * w), m, v


def _adamw_small(g_mine, g_sib, c_idx, w, m, v):
    ns = len(SMALL)

    def body(c_ref, gm_ref, gs_ref, *refs):
        ws, ms, vs = refs[:ns], refs[ns:2 * ns], refs[2 * ns:3 * ns]
        outs = refs[3 * ns:]
        mine_first = c_ref[0] == 0
        top = jnp.where(mine_first, gm_ref[...], gs_ref[...])
        bot = jnp.where(mine_first, gs_ref[...], gm_ref[...])
        for i, n in enumerate(SMALL):
            r, c0 = SMALL_AT[n]
            blk, rr = (top, r) if r < SMALL_ROWS // 2 else (bot, r - SMALL_ROWS // 2)
            rows, width = ws[i].shape
            for k in range(rows):
                g = blk[rr:rr + 1, c0 + k * width:c0 + (k + 1) * width]
                d, mn, vn = _adamw_math(ws[i][k:k + 1, :], g, ms[i][k:k + 1, :], vs[i][k:k + 1, :])
                for q, val in enumerate((g, d, mn, vn)):
                    outs[q * ns + i][k:k + 1, :] = val

    shapes = [jax.ShapeDtypeStruct(a.shape, F32) for a in w]
    vmem = pl.BlockSpec(memory_space=pltpu.VMEM)
    res = pl.pallas_call(
        body, name="adamw_small",
        in_specs=[pl.BlockSpec(memory_space=pltpu.SMEM)] + [vmem] * (2 + 3 * ns),
        out_specs=[vmem] * (4 * ns),
        out_shape=shapes * 4,
    )(c_idx, g_mine, g_sib, *w, *m, *v)
    return [res[q * ns:(q + 1) * ns] for q in range(4)]


def _place():
    x, y, c = lax.axis_index("x"), lax.axis_index("y"), lax.axis_index("c")
    others = [(1 - x, y), (x, 1 - y), (1 - x, 1 - y)]
    return x, y, c, others


ANY_SPEC = pl.BlockSpec(memory_space=pl.ANY)


def _remote(src, dst, send_sems, recv_sems, k, to):
    return pltpu.make_async_remote_copy(src_ref=src, dst_ref=dst, send_sem=send_sems.at[k], recv_sem=recv_sems.at[k],
                                        device_id=to, device_id_type=MESH)


def _all_gather(shards):
    n = len(shards)

    def body(*refs):
        ins, outs, send_sems, recv_sems = refs[:n], refs[n:2 * n], refs[2 * n], refs[2 * n + 1]
        _gather_start(ins, outs, send_sems, recv_sems)
        _gather_finish(ins, outs, send_sems, recv_sems)

    return pl.pallas_call(
        body, name="all_gather_weights",
        out_shape=_gather_shapes(shards), in_specs=[ANY_SPEC] * n, out_specs=[ANY_SPEC] * n,
        scratch_shapes=_gather_sems(n),
    )(*shards)


def _gather_shapes(shards):
    return [jax.ShapeDtypeStruct((N_CHIPS,) + a.shape, a.dtype) for a in shards]


def _gather_sems(n):
    return [pltpu.SemaphoreType.DMA((6 * n,)), pltpu.SemaphoreType.DMA((6 * n,))]


def _gather_sends(ins, outs, send_sems, recv_sems):
    x, y, c, others = _place()
    s = 2 * x + y
    return [_remote(ins[t].at[c], outs[t].at[s, c], send_sems, recv_sems, 6 * t + j, (*chip, c))
            for t in range(len(ins)) for j, chip in enumerate(others)]


def _gather_start(ins, outs, send_sems, recv_sems):
    for cp in _gather_sends(ins, outs, send_sems, recv_sems):
        cp.start()


def _gather_finish(ins, outs, send_sems, recv_sems):
    x, y, c, others = _place()
    slot = lambda t, chip, half: outs[t].at[2 * chip[0] + chip[1], half]
    passed = []
    for t in range(len(ins)):
        for j, chip in enumerate(others):
            landed = slot(t, chip, c)
            _remote(landed, landed, send_sems, recv_sems, 6 * t + j, (x, y, c)).wait_recv()
            passed.append(_remote(landed, landed, send_sems, recv_sems, 6 * t + 3 + j, (x, y, 1 - c)))
            passed[-1].start()
    for t in range(len(ins)):
        for j, chip in enumerate(others):
            landed = slot(t, chip, 1 - c)
            _remote(landed, landed, send_sems, recv_sems, 6 * t + 3 + j, (x, y, c)).wait_recv()
    for cp in _gather_sends(ins, outs, send_sems, recv_sems) + passed:
        cp.wait_send()


def _swap_halves(gs, tag):
    n = len(gs)

    def body(*refs):
        ins, outs, send_sems, recv_sems = refs[:n], refs[n:2 * n], refs[2 * n], refs[2 * n + 1]
        _swap_start(ins, outs, send_sems, recv_sems)
        _swap_finish(ins, outs, send_sems, recv_sems)

    return pl.pallas_call(
        body, name="grad_swap_halves_" + tag,
        out_shape=_swap_shapes(gs), in_specs=[ANY_SPEC] * n, out_specs=[ANY_SPEC] * n,
        scratch_shapes=_swap_sems(n),
    )(*gs)


def _swap_shapes(gs):
    return [jax.ShapeDtypeStruct(g.shape[:1] + g.shape[2:], g.dtype) for g in gs]


def _swap_sems(n):
    return [pltpu.SemaphoreType.DMA((n,)), pltpu.SemaphoreType.DMA((n,))]


def _swap_copies(ins, outs, send_sems, recv_sems):
    x, y, c, _ = _place()
    return [_remote(ins[t].at[:, 1 - c], outs[t], send_sems, recv_sems, t, (x, y, 1 - c)) for t in range(len(ins))]


def _swap_start(ins, outs, send_sems, recv_sems):
    for cp in _swap_copies(ins, outs, send_sems, recv_sems):
        cp.start()


def _swap_finish(ins, outs, send_sems, recv_sems):
    for cp in _swap_copies(ins, outs, send_sems, recv_sems):
        cp.wait()


def _exchange_chips(pps):
    n = len(pps)

    def body(*refs):
        ins, outs, send_sems, recv_sems = refs[:n], refs[n:2 * n], refs[2 * n], refs[2 * n + 1]
        _exchange_start(ins, outs, send_sems, recv_sems)
        _exchange_finish(ins, outs, send_sems, recv_sems)

    return pl.pallas_call(
        body, name="grad_exchange_chips",
        out_shape=_exchange_shapes(pps), in_specs=[ANY_SPEC] * n, out_specs=[ANY_SPEC] * n,
        scratch_shapes=_exchange_sems(n),
    )(*pps)


def _exchange_shapes(pps):
    return [jax.ShapeDtypeStruct(p.shape, p.dtype) for p in pps]


def _exchange_sems(n):
    return [pltpu.SemaphoreType.DMA((3 * n,)), pltpu.SemaphoreType.DMA((3 * n,))]


def _exchange_sends(ins, outs, send_sems, recv_sems):
    x, y, c, others = _place()
    s = 2 * x + y
    return [_remote(ins[t].at[2 * chip[0] + chip[1]], outs[t].at[s], send_sems, recv_sems, 3 * t + j, (*chip, c))
            for t in range(len(ins)) for j, chip in enumerate(others)]


def _exchange_start(ins, outs, send_sems, recv_sems):
    for cp in _exchange_sends(ins, outs, send_sems, recv_sems):
        cp.start()


def _exchange_finish(ins, outs, send_sems, recv_sems):
    x, y, c, others = _place()
    for t in range(len(ins)):
        for j, chip in enumerate(others):
            landed = outs[t].at[2 * chip[0] + chip[1]]
            _remote(landed, landed, send_sems, recv_sems, 3 * t + j, (x, y, c)).wait_recv()
    for cp in _exchange_sends(ins, outs, send_sems, recv_sems):
        cp.wait_send()


def _share_half(rs):
    n = len(rs)

    def body(*refs):
        ins, outs, send_sems, recv_sems = refs[:n], refs[n:2 * n], refs[2 * n], refs[2 * n + 1]
        x, y, c, _ = _place()
        copies = [_remote(ins[t], outs[t], send_sems, recv_sems, t, (x, y, 1 - c)) for t in range(n)]
        for cp in copies:
            cp.start()
        for cp in copies:
            cp.wait()

    return pl.pallas_call(
        body, name="grad_share_half",
        out_shape=[jax.ShapeDtypeStruct(r.shape, r.dtype) for r in rs],
        in_specs=[ANY_SPEC] * n, out_specs=[ANY_SPEC] * n,
        scratch_shapes=[pltpu.SemaphoreType.DMA((n,)), pltpu.SemaphoreType.DMA((n,))],
    )(*rs)


ELEMENTWISE_BLOCK_BYTES = 1 << 20


def _row_tile(rows, cols):
    return _tile(rows, max(8, ELEMENTWISE_BLOCK_BYTES // (4 * cols) // 8 * 8))


def _add_my_half(g, sib, c_idx, name):
    rh, cols = g.shape[2:]
    tr = _row_tile(rh, cols)

    def body(c_ref, g_ref, s_ref, o_ref):
        o_ref[...] = (g_ref[...] + s_ref[...]).astype(BF16)

    return pl.pallas_call(
        body, name="grad_add_halves_" + name,
        grid_spec=pltpu.PrefetchScalarGridSpec(
            num_scalar_prefetch=1, grid=(N_CHIPS, rh // tr),
            in_specs=[pl.BlockSpec((None, None, tr, cols), lambda s, i, c: (s, c[0], i, 0)),
                      pl.BlockSpec((None, tr, cols), lambda s, i, c: (s, i, 0))],
            out_specs=pl.BlockSpec((None, tr, cols), lambda s, i, c: (s, i, 0))),
        out_shape=jax.ShapeDtypeStruct((N_CHIPS, rh, cols), BF16),
        compiler_params=_cp(2),
    )(c_idx, g, sib)


def _sum_chips(parts, pp, s_idx, name):
    rh, cols = parts.shape[1:]
    tr = _row_tile(rh, cols)

    def body(s_ref, p0, p1, p2, p3, mine_ref, o_ref):
        own = mine_ref[...]
        t = [jnp.where(s_ref[0] == k, own, p[...]).astype(F32) for k, p in enumerate((p0, p1, p2, p3))]
        o_ref[...] = ((t[0] + t[1]) + t[2]) + t[3]

    slot = lambda k: pl.BlockSpec((None, tr, cols), lambda i, s: (jnp.where(s[0] == k, (k + 1) % N_CHIPS, k), i, 0))
    return pl.pallas_call(
        body, name="grad_sum_chips_" + name,
        grid_spec=pltpu.PrefetchScalarGridSpec(
            num_scalar_prefetch=1, grid=(rh // tr,),
            in_specs=[slot(0), slot(1), slot(2), slot(3), pl.BlockSpec((None, tr, cols), lambda i, s: (s[0], i, 0))],
            out_specs=pl.BlockSpec((tr, cols), lambda i, s: (i, 0))),
        out_shape=jax.ShapeDtypeStruct((rh, cols), F32),
        compiler_params=_cp(1),
    )(s_idx, parts, parts, parts, parts, pp)


def _adamw(w, g_mine, g_sib, m, v, c_idx, name):
    rows, cols = w.shape
    tr = _row_tile(rows // 2, cols)
    nbh = rows // 2 // tr

    def body(c_ref, w_ref, gm_ref, gs_ref, m_ref, v_ref, g_ref, d_ref, mo_ref, vo_ref):
        g = jnp.where(pl.program_id(0) // nbh == c_ref[0], gm_ref[...], gs_ref[...])
        g_ref[...] = g
        d_ref[...], mo_ref[...], vo_ref[...] = _adamw_math(w_ref[...], g, m_ref[...], v_ref[...])

    spec = pl.BlockSpec((tr, cols), lambda i, c: (i, 0))
    half = pl.BlockSpec((tr, cols), lambda i, c: (i % nbh, 0))
    return pl.pallas_call(
        body, name="adamw_" + name,
        grid_spec=pltpu.PrefetchScalarGridSpec(
            num_scalar_prefetch=1, grid=(rows // tr,),
            in_specs=[spec, half, half, spec, spec], out_specs=[spec] * 4),
        out_shape=[jax.ShapeDtypeStruct(w.shape, F32)] * 4,
        compiler_params=_cp(1),
    )(c_idx, w, g_mine, g_sib, m, v)


def kernel(x, p, ffn1_w_in, ffn1_w_out, ln1_g, ln1_b, w_mix_in, b_forget, conv_w, g_attn, g_conv, w_mix_out, ln2_g, ln2_b, ffn2_w_in, ffn2_w_out, ln3_g, ln3_b, w_ple, w_ple_gate, b_ple_gate, ln4_g, ln4_b, loss_target, m_ffn1_w_in, m_ffn1_w_out, m_ln1_g, m_ln1_b, m_w_mix_in, m_b_forget, m_conv_w, m_g_attn, m_g_conv, m_w_mix_out, m_ln2_g, m_ln2_b, m_ffn2_w_in, m_ffn2_w_out, m_ln3_g, m_ln3_b, m_w_ple, m_w_ple_gate, m_b_ple_gate, m_ln4_g, m_ln4_b, v_ffn1_w_in, v_ffn1_w_out, v_ln1_g, v_ln1_b, v_w_mix_in, v_b_forget, v_conv_w, v_g_attn, v_g_conv, v_w_mix_out, v_ln2_g, v_ln2_b, v_ffn2_w_in, v_ffn2_w_out, v_ln3_g, v_ln3_b, v_w_ple, v_w_ple_gate, v_b_ple_gate, v_ln4_g, v_ln4_b):
    args = dict(locals())
    shard = {n: args[n][0] if LAYOUT[n][1] is not None else args[n] for n in WEIGHTS}
    m_shard = {n: args["m_" + n][0] if LAYOUT[n][1] is not None else args["m_" + n] for n in WEIGHTS}
    v_shard = {n: args["v_" + n][0] if LAYOUT[n][1] is not None else args["v_" + n] for n in WEIGHTS}
    c_idx = lax.axis_index("c").astype(jnp.int32).reshape(1)
    chip = (2 * lax.axis_index("x") + lax.axis_index("y")).astype(jnp.int32)

    conv_rows = SMALL_ROWS - shard["conv_w"].shape[0]
    mine = {n: _halves(shard[n].astype(BF16)) for n in BIG}
    mine["conv_w"] = _halves(jnp.pad(shard["conv_w"], ((0, conv_rows), (0, 0))))
    early_w = ["ffn1_w_in", "ffn1_w_out"]
    late_w = [n for n in BIG if n not in early_w] + ["conv_w"]

    def full_weights(names, gathered):
        out = {}
        for n, theirs in zip(names, gathered):
            g = lax.dynamic_update_slice(theirs, mine[n][None], (chip, 0, 0, 0))
            if n == "conv_w":
                out[n] = _join_chips(n, g.reshape(N_CHIPS, SMALL_ROWS, CONV_SHARD)[:, :3])
            else:
                out[n] = _join_chips(n, g.reshape((N_CHIPS,) + _shard_shape(n)))
        return out

    full = full_weights(early_w, _all_gather([mine[n] for n in early_w]))
    full.update({n: shard[n] for n in SMALL if n != "conv_w"})

    def per_chip(names, grads):
        by_chip = lambda n: grads[n] if grads[n].ndim == 3 else _split_chips(n, grads[n])
        return [_halves(_pack_small_grads(grads) if n == "small" else by_chip(n)) for n in names]

    def add_halves(names, mine_, sibs):
        return [_add_my_half(g, sib, c_idx, n) for n, g, sib in zip(names, mine_, sibs)]

    def chip_sums(names, grads):
        mine_ = per_chip(names, grads)
        return add_halves(names, mine_, _swap_halves(mine_, names[0]))

    ready_a = ["ffn2_w_in", "ffn2_w_out", "w_ple", "w_ple_gate"]
    ready_b = ["w_mix_in", "w_mix_out"]
    early_g = ready_a + ready_b
    late_g = early_w + ["small"]
    loss_acc, grad_x, grads, early_sums, early_parts = _local_step(
        x[0], p[0, 0], loss_target[0], full,
        overlap={"gather": [mine[n] for n in late_w], "weights": lambda gathered: full_weights(late_w, gathered),
                 "swap": lambda grads: per_chip(ready_a, grads),
                 "chip_sums": lambda grads, swapped, received: (add_halves(ready_a, swapped, received)
                                                                + chip_sums(ready_b, grads))})
    loss = lax.psum(loss_acc[0, 0], ("x", "y", "c"))

    late_sums = chip_sums(late_g, grads)
    names = early_g + late_g
    sums = list(early_sums) + late_sums
    parts = list(early_parts) + list(_exchange_chips(late_sums))
    half_of = {n: _sum_chips(pt, own, chip.reshape(1), n) for n, pt, own in zip(names, parts, sums)}
    names = BIG + ["small"]
    my_half = [half_of[n] for n in names]
    sib_half = _share_half(my_half)

    out = {}
    for n, gm, gs in zip(BIG, my_half, sib_half):
        out[n] = [a[None] for a in _adamw(shard[n], gm, gs, m_shard[n], v_shard[n], c_idx, n)]
    small = _adamw_small(my_half[-1], sib_half[-1], c_idx, [shard[n] for n in SMALL], [m_shard[n] for n in SMALL],
                         [v_shard[n] for n in SMALL])
    for i, n in enumerate(SMALL):
        out[n] = [small[q][i][None] if n == "conv_w" else small[q][i] for q in range(4)]
    return (loss, grad_x[None], *[out[n][q] for q in range(4) for n in WEIGHTS])
```

```python
import functools
import math

import jax
import jax.numpy as jnp
from jax import lax
from jax.experimental import pallas as pl
from jax.experimental.pallas import tpu as pltpu

F32 = jnp.float32
BF16 = jnp.bfloat16

D_MODEL = 1024
D_FF = 2816
N_HEADS = 8
HEAD_DIM = 64
D_ATTN = N_HEADS * HEAD_DIM
D_CONV = 512
PLE_DIM = 256
N_FLOG = 128
ALPHA = 2.0 ** 0.25
LN_EPS = 1e-5
RMS_EPS = 1e-6
NEG_INF = -1e30
Q_SCALE = 1.0 / math.sqrt(HEAD_DIM)
LOG2E = math.log2(math.e)

ADAM_LR = 0.001
ADAM_B1 = 0.9
ADAM_B2 = 0.999
ADAM_EPS = 1e-08
ADAM_WD = 0.01
ADAM_STEP = 10

V7X_VMEM_BYTES = 64 << 20
VMEM_LIMIT = V7X_VMEM_BYTES - (8 << 20)
LANE = 128
FF_CHUNK = 256
N_CHIPS = 4
MESH = pl.DeviceIdType.MESH


def _cp(n_axes):
    return pltpu.CompilerParams(dimension_semantics=("arbitrary",) * n_axes, vmem_limit_bytes=VMEM_LIMIT)


def _resident(shape):
    n = len(shape)
    return pl.BlockSpec(shape, lambda *_: (0,) * n, pipeline_mode=pl.Buffered(1))


def _nn(a, b):
    return jnp.dot(a, b, preferred_element_type=F32)


def _nt(a, b):
    return lax.dot_general(a, b, (((1,), (1,)), ((), ())), preferred_element_type=F32)


def _tn(a, b):
    return lax.dot_general(a, b, (((0,), (0,)), ((), ())), preferred_element_type=F32)


def _ln_stats(r):
    mu = jnp.mean(r, axis=-1, keepdims=True)
    xc = r - mu
    var = jnp.mean(xc * xc, axis=-1, keepdims=True)
    rstd = lax.rsqrt(var + LN_EPS)
    return xc * rstd, rstd


def _ln_bwd(dy, xhat, rstd, g):
    dxh = dy * g
    m1 = jnp.mean(dxh, axis=-1, keepdims=True)
    m2 = jnp.mean(dxh * xhat, axis=-1, keepdims=True)
    return rstd * (dxh - m1 - xhat * m2)


def _sigmoid(z):
    return 1.0 / (1.0 + jnp.exp(-z))


def _rowsum(a):
    return jnp.sum(a, axis=0, keepdims=True)


def _tile(total, want):
    if total <= want:
        return total
    for t in range(want - want % 8, 0, -8):
        if total % t == 0:
            return t
    raise ValueError((total, want))


def _ffn_fwd(x, w_in, w_out, lg, lb, name, gather=()):
    T = x.shape[0]
    tm = _tile(T, 512)
    nf = D_FF // FF_CHUNK
    ng = len(gather)
    last = T // tm - 1

    def body(x_ref, wi_ref, wo_ref, lg_ref, lb_ref, *rest):
        comm_in, (xo_ref, r_ref, g_ref, u_ref, h_ref) = rest[:ng], rest[ng:ng + 5]
        comm_out, sems = rest[ng + 5:2 * ng + 5], rest[2 * ng + 5:]
        if ng:
            @pl.when(pl.program_id(0) == 0)
            def _():
                _gather_start(comm_in, comm_out, *sems)

        xf = x_ref[...]
        xb = xf.astype(BF16)
        acc = jnp.zeros((tm, D_MODEL), F32)
        for j in range(nf):
            c0 = j * FF_CHUNK
            g = _nn(xb, wi_ref[:, c0:c0 + FF_CHUNK])
            u = _nn(xb, wi_ref[:, D_FF + c0:D_FF + c0 + FF_CHUNK])
            hb = (g * _sigmoid(g) * u).astype(BF16)
            g_ref[:, c0:c0 + FF_CHUNK] = g.astype(BF16)
            u_ref[:, c0:c0 + FF_CHUNK] = u.astype(BF16)
            h_ref[:, c0:c0 + FF_CHUNK] = hb
            acc = acc + _nn(hb, wo_ref[c0:c0 + FF_CHUNK, :])
        r = ALPHA * xf + 0.5 * acc
        r_ref[...] = r
        xhat, _ = _ln_stats(r)
        xo_ref[...] = xhat * lg_ref[...] + lb_ref[...]
        if ng:
            @pl.when(pl.program_id(0) == last)
            def _():
                _gather_finish(comm_in, comm_out, *sems)

    row = lambda n: pl.BlockSpec((tm, n), lambda i: (i, 0))
    return pl.pallas_call(
        body, name=name, grid=(T // tm,),
        in_specs=[row(D_MODEL), _resident((D_MODEL, 2 * D_FF)), _resident((D_FF, D_MODEL)),
                  _resident((1, D_MODEL)), _resident((1, D_MODEL))] + [ANY_SPEC] * ng,
        out_specs=[row(D_MODEL), row(D_MODEL), row(D_FF), row(D_FF), row(D_FF)] + [ANY_SPEC] * ng,
        out_shape=[jax.ShapeDtypeStruct((T, D_MODEL), F32), jax.ShapeDtypeStruct((T, D_MODEL), F32),
                   jax.ShapeDtypeStruct((T, D_FF), BF16), jax.ShapeDtypeStruct((T, D_FF), BF16),
                   jax.ShapeDtypeStruct((T, D_FF), BF16)] + _gather_shapes(gather),
        scratch_shapes=_gather_sems(ng) if ng else [],
        compiler_params=_cp(1),
    )(x, w_in, w_out, lg, lb, *gather)


def _ffn_bwd(dxo, r, g, u, w_in, w_out, lg, name, exchange=()):
    T = r.shape[0]
    tm = _tile(T, 256)
    nf = D_FF // FF_CHUNK
    ne = len(exchange)
    last = T // tm - 1

    def body(dxo_ref, r_ref, g_ref, u_ref, wi_ref, wo_ref, lg_ref, *rest):
        comm_in, (dx_ref, dgu_ref, df_ref, dlg_ref, dlb_ref) = rest[:ne], rest[ne:ne + 5]
        comm_out, sems = rest[ne + 5:2 * ne + 5], rest[2 * ne + 5:]
        i = pl.program_id(0)
        if ne:
            @pl.when(i == 0)
            def _():
                _exchange_start(comm_in, comm_out, *sems)

        dy = dxo_ref[...]
        xhat, rstd = _ln_stats(r_ref[...])
        dr = _ln_bwd(dy, xhat, rstd, lg_ref[...])

        @pl.when(i == 0)
        def _():
            dlg_ref[...] = jnp.zeros_like(dlg_ref)
            dlb_ref[...] = jnp.zeros_like(dlb_ref)

        dlg_ref[...] += _rowsum(dy * xhat)
        dlb_ref[...] += _rowsum(dy)
        dfb = (0.5 * dr).astype(BF16)
        df_ref[...] = dfb
        acc = jnp.zeros((tm, D_MODEL), F32)
        dh_ahead = _nt(dfb, wo_ref[0:FF_CHUNK, :])
        for j in range(nf):
            c0 = j * FF_CHUNK
            dh = dh_ahead
            if j + 1 < nf:
                dh_ahead = _nt(dfb, wo_ref[c0 + FF_CHUNK:c0 + 2 * FF_CHUNK, :])
            gg = g_ref[:, c0:c0 + FF_CHUNK].astype(F32)
            uu = u_ref[:, c0:c0 + FF_CHUNK].astype(F32)
            s = _sigmoid(gg)
            dgb = (dh * uu * s * (1.0 + gg * (1.0 - s))).astype(BF16)
            dub = (dh * gg * s).astype(BF16)
            dgu_ref[:, c0:c0 + FF_CHUNK] = dgb
            dgu_ref[:, D_FF + c0:D_FF + c0 + FF_CHUNK] = dub
            acc = acc + _nt(dgb, wi_ref[:, c0:c0 + FF_CHUNK]) + _nt(dub, wi_ref[:, D_FF + c0:D_FF + c0 + FF_CHUNK])
        dx_ref[...] = ALPHA * dr + acc
        if ne:
            @pl.when(i == last)
            def _():
                _exchange_finish(comm_in, comm_out, *sems)

    row = lambda n: pl.BlockSpec((tm, n), lambda i: (i, 0))
    return pl.pallas_call(
        body, name=name, grid=(T // tm,),
        in_specs=[row(D_MODEL), row(D_MODEL), row(D_FF), row(D_FF), _resident((D_MODEL, 2 * D_FF)),
                  _resident((D_FF, D_MODEL)), _resident((1, D_MODEL))] + [ANY_SPEC] * ne,
        out_specs=[row(D_MODEL), row(2 * D_FF), row(D_MODEL), _resident((1, D_MODEL)), _resident((1, D_MODEL))]
        + [ANY_SPEC] * ne,
        out_shape=[jax.ShapeDtypeStruct((T, D_MODEL), F32), jax.ShapeDtypeStruct((T, 2 * D_FF), BF16),
                   jax.ShapeDtypeStruct((T, D_MODEL), BF16), jax.ShapeDtypeStruct((1, D_MODEL), F32),
                   jax.ShapeDtypeStruct((1, D_MODEL), F32)] + _exchange_shapes(exchange),
        scratch_shapes=_exchange_sems(ne) if ne else [],
        compiler_params=_cp(1),
    )(dxo, r, g, u, w_in, w_out, lg, *exchange)


def _matmul_tn(a, b, name, by_chip=False, exchange=()):
    T, K = a.shape
    N = b.shape[1]
    tt = _tile(T, 1024)
    tn = N // N_CHIPS if by_chip else N
    while K * tn * 4 > (6 << 20) and tn % 256 == 0 and not by_chip:
        tn //= 2
    assert N % tn == 0
    ne = len(exchange)
    grid = (N // tn, T // tt)

    def body(a_ref, b_ref, *rest):
        comm_in, o_ref, comm_out, sems = rest[:ne], rest[ne], rest[ne + 1:2 * ne + 1], rest[2 * ne + 1:]
        n, t = pl.program_id(0), pl.program_id(1)
        if ne:
            @pl.when((n == 0) & (t == 0))
            def _():
                _exchange_start(comm_in, comm_out, *sems)

        @pl.when(t == 0)
        def _():
            o_ref[...] = jnp.zeros_like(o_ref)

        o_ref[...] += _tn(a_ref[...].astype(BF16), b_ref[...].astype(BF16))
        if ne:
            @pl.when((n == grid[0] - 1) & (t == grid[1] - 1))
            def _():
                _exchange_finish(comm_in, comm_out, *sems)

    res = pl.pallas_call(
        body, name=name, grid=grid,
        in_specs=[pl.BlockSpec((tt, K), lambda n, t: (t, 0)), pl.BlockSpec((tt, tn), lambda n, t: (t, n))]
        + [ANY_SPEC] * ne,
        out_specs=[pl.BlockSpec((None, K, tn), lambda n, t: (n, 0, 0)) if by_chip
                   else pl.BlockSpec((K, tn), lambda n, t: (0, n))] + [ANY_SPEC] * ne,
        out_shape=[jax.ShapeDtypeStruct((N_CHIPS, K, tn) if by_chip else (K, N), F32)] + _exchange_shapes(exchange),
        scratch_shapes=_exchange_sems(ne) if ne else [],
        compiler_params=_cp(2),
    )(a, b, *exchange)
    return res if ne else res[0]


def _matmul_tokens(at, b, name):
    M, T = at.shape
    N = b.shape[1]
    tt = _tile(T, 1024)

    def body(a_ref, b_ref, o_ref):
        @pl.when(pl.program_id(0) == 0)
        def _():
            o_ref[...] = jnp.zeros_like(o_ref)

        o_ref[...] += _nn(a_ref[...].astype(BF16), b_ref[...].astype(BF16))

    return pl.pallas_call(
        body, name=name, grid=(T // tt,),
        in_specs=[pl.BlockSpec((M, tt), lambda t: (0, t)), pl.BlockSpec((tt, N), lambda t: (t, 0))],
        out_specs=pl.BlockSpec((M, N), lambda t: (0, 0)),
        out_shape=jax.ShapeDtypeStruct((M, N), F32),
        compiler_params=_cp(1),
    )(at, b)


def _matmul_nn(x, w, scale, out_dtype, name, also_transposed=False):
    T, K = x.shape
    N = w.shape[1]
    tm = _tile(T, 512)

    def body(x_ref, w_ref, s_ref, o_ref, *ot_ref):
        res = _nn(x_ref[...].astype(BF16), w_ref[...]) * s_ref[...]
        o_ref[...] = res.astype(out_dtype)
        if also_transposed:
            ot_ref[0][...] = res.T.astype(out_dtype)

    res = pl.pallas_call(
        body, name=name, grid=(T // tm,),
        in_specs=[pl.BlockSpec((tm, K), lambda i: (i, 0)), _resident((K, N)), _resident((1, N))],
        out_specs=[pl.BlockSpec((tm, N), lambda i: (i, 0))] + [pl.BlockSpec((N, tm), lambda i: (0, i))] * also_transposed,
        out_shape=[jax.ShapeDtypeStruct((T, N), out_dtype)] + [jax.ShapeDtypeStruct((N, T), out_dtype)] * also_transposed,
        compiler_params=_cp(1),
    )(x, w, scale)
    return res if also_transposed else res[0]


def _log_sigmoid(z):
    return jnp.minimum(z, 0.0) - jnp.log1p(jnp.exp(-jnp.abs(z)))


def _tri(n, lower):
    r = lax.broadcasted_iota(jnp.int32, (n, n), 0)
    c = lax.broadcasted_iota(jnp.int32, (n, n), 1)
    return jnp.where((c <= r) if lower else (c >= r), 1.0, 0.0).astype(F32)


def _f32dot(a, b):
    return jnp.dot(a, b, preferred_element_type=F32, precision=lax.Precision.HIGHEST)


def _forget_cumsum(flog, col, bf):
    T = flog.shape[0]
    bt = _tile(T, 512)

    def body(f_ref, b_ref, c_ref, carry):
        @pl.when(pl.program_id(0) == 0)
        def _():
            carry[...] = jnp.zeros_like(carry)

        lf = _log_sigmoid(f_ref[...] + b_ref[...])
        c = _f32dot(_tri(bt, True), lf) + carry[...]
        c_ref[...] = c * LOG2E
        carry[...] = c[bt - 1:bt, :]

    return pl.pallas_call(
        body, name="forget_cumsum", grid=(T // bt,),
        in_specs=[pl.BlockSpec((bt, N_FLOG), lambda i: (i, col)), _resident((1, N_FLOG))],
        out_specs=pl.BlockSpec((bt, N_FLOG), lambda i: (i, 0)),
        out_shape=jax.ShapeDtypeStruct((T, N_FLOG), F32),
        scratch_shapes=[pltpu.VMEM((1, N_FLOG), F32)],
        compiler_params=_cp(1),
    )(flog, bf)


def _forget_bwd(dck, dcq, flog, col, bf):
    T = dcq.shape[0]
    bt = _tile(T, 512)
    nb = T // bt

    def body(k0_ref, k1_ref, k2_ref, k3_ref, dcq_ref, f_ref, b_ref, dz_ref, db_ref, carry):
        @pl.when(pl.program_id(0) == 0)
        def _():
            carry[...] = jnp.zeros_like(carry)
            db_ref[...] = jnp.zeros_like(db_ref)

        dc = ((k0_ref[...] + k1_ref[...]) + (k2_ref[...] + k3_ref[...])) + dcq_ref[...]
        dlf = _f32dot(_tri(bt, False), dc) + carry[...]
        carry[...] = dlf[0:1, :]
        z = f_ref[...] + b_ref[...]
        dz = dlf * _sigmoid(-z)
        dz_ref[...] = dz.astype(BF16)
        db_ref[...] += _rowsum(dz)

    slab = lambda j: pl.BlockSpec((None, bt, N_FLOG), lambda i: (j, nb - 1 - i, 0))
    return pl.pallas_call(
        body, name="forget_bwd", grid=(nb,),
        in_specs=[slab(0), slab(1), slab(2), slab(3),
                  pl.BlockSpec((bt, N_FLOG), lambda i: (nb - 1 - i, 0)),
                  pl.BlockSpec((bt, N_FLOG), lambda i: (nb - 1 - i, col)), _resident((1, N_FLOG))],
        out_specs=[pl.BlockSpec((bt, N_FLOG), lambda i: (nb - 1 - i, 0)), _resident((1, N_FLOG))],
        out_shape=[jax.ShapeDtypeStruct((T, N_FLOG), BF16), jax.ShapeDtypeStruct((1, N_FLOG), F32)],
        scratch_shapes=[pltpu.VMEM((1, N_FLOG), F32)],
        compiler_params=_cp(1),
    )(dck, dck, dck, dck, dcq, flog, bf)


def _head_masks():
    lane = lax.broadcasted_iota(jnp.int32, (1, LANE), 1)
    return lane < HEAD_DIM


def _split_heads(x2, is_a):
    zero = jnp.zeros_like(x2)
    return jnp.where(is_a, x2, zero), jnp.where(is_a, zero, x2)


BIAS_PARTS = 3
ATTN_GROUP = 4
ATTN_FWD_GROUP = 8


def _bias_lanes(h):
    lane = lax.broadcasted_iota(jnp.int32, (1, LANE), 1)
    first = (1 - h) * HEAD_DIM
    return lane, first


def _fold_key_bias(qkv, c):
    T = qkv.shape[0]
    tm = _tile(T, 512)
    npair = N_HEADS // 2

    def body(k_ref, c_ref, o_ref):
        cc = c_ref[...]
        parts, rest = [], cc
        for _ in range(BIAS_PARTS):
            piece = rest.astype(BF16)
            parts.append(piece)
            rest = rest - piece.astype(F32)
        for j in range(npair):
            k2 = k_ref[:, j * LANE:(j + 1) * LANE]
            for h in range(2):
                lane, first = _bias_lanes(h)
                out = k2
                for n, piece in enumerate(parts):
                    col = piece[:, 2 * j + h:2 * j + h + 1]
                    out = jnp.where(lane == first + n, col, out)
                o_ref[:, (2 * j + h) * LANE:(2 * j + h + 1) * LANE] = out

    return pl.pallas_call(
        body, name="fold_key_bias", grid=(T // tm,),
        in_specs=[pl.BlockSpec((tm, D_ATTN), lambda i: (i, 1)), pl.BlockSpec((tm, N_FLOG), lambda i: (i, 0))],
        out_specs=pl.BlockSpec((tm, 2 * D_ATTN), lambda i: (i, 0)),
        out_shape=jax.ShapeDtypeStruct((T, 2 * D_ATTN), BF16),
        compiler_params=_cp(1),
    )(qkv, c)


def _attn_fwd(qkv, vt, kb):
    T = qkv.shape[0]
    tq = _tile(T, 512)
    tk = tq
    nq = T // tq
    npair = N_HEADS // 2

    def body(q_ref, ka_ref, kb_ref, vt_ref, o_ref, al_ref, m_s, l_s, acc_s):
        i = pl.program_id(1)
        qs = []
        for h, qh in enumerate(_split_heads(q_ref[...], _head_masks())):
            lane, first = _bias_lanes(h)
            qs.append(jnp.where((lane >= first) & (lane < first + BIAS_PARTS), -1.0, qh).astype(BF16))
        k_refs = (ka_ref, kb_ref)
        m_s[...] = jnp.full_like(m_s, NEG_INF)
        l_s[...] = jnp.zeros_like(l_s)
        acc_s[...] = jnp.zeros_like(acc_s)

        def scores_at(kk):
            k0 = pl.multiple_of(kk * tk, tk)
            return tuple(_nt(k_refs[h][pl.ds(k0, tk), :], qs[h]) for h in range(2))

        def consume(kk, scores, masked):
            k0 = pl.multiple_of(kk * tk, tk)
            v2t = vt_ref[:, pl.ds(k0, tk)]
            for h in range(2):
                zt = scores[h]
                if masked:
                    rr = lax.broadcasted_iota(jnp.int32, (tk, tq), 0)
                    cc = lax.broadcasted_iota(jnp.int32, (tk, tq), 1)
                    zt = jnp.where(cc >= rr, zt, NEG_INF)
                m_old = m_s[h]
                m_new = jnp.maximum(m_old, jnp.max(zt, axis=0, keepdims=True))
                p = jnp.exp2(zt - m_new)
                a = jnp.exp2(m_old - m_new)
                l_s[h] = a * l_s[h] + jnp.sum(p, axis=0, keepdims=True)
                acc_s[h] = a * acc_s[h] + _nn(v2t, p.astype(BF16))
                m_s[h] = m_new

        def group(kk, n, last_masked):
            scores = [scores_at(kk + u) for u in range(n)]
            for u in range(n):
                consume(kk + u, scores[u], last_masked and u == n - 1)

        def loop_body(t, carry):
            group(ATTN_FWD_GROUP * t, ATTN_FWD_GROUP, False)
            return carry

        lax.fori_loop(0, i // ATTN_FWD_GROUP, loop_body, 0)
        for left in range(ATTN_FWD_GROUP):
            @pl.when(i % ATTN_FWD_GROUP == left)
            def _():
                group(i - left, left + 1, True)

        outs = []
        for h in range(2):
            l = l_s[h]
            outs.append(acc_s[h] * (1.0 / l))
            al_ref[0, h:h + 1, :] = -(m_s[h] + jnp.log2(l))
        al_ref[0, 2:8, :] = jnp.zeros((6, tq), F32)
        dim = lax.broadcasted_iota(jnp.int32, (LANE, 1), 0)
        o_ref[...] = jnp.where(dim < HEAD_DIM, outs[0], outs[1]).T

    rowl = pl.BlockSpec((1, 8, tq), lambda j, i: (j, 0, i))
    return pl.pallas_call(
        body, name="attn_fwd", grid=(npair, nq),
        in_specs=[pl.BlockSpec((tq, LANE), lambda j, i: (i, j)),
                  pl.BlockSpec((T, LANE), lambda j, i: (0, 2 * j), pipeline_mode=pl.Buffered(1)),
                  pl.BlockSpec((T, LANE), lambda j, i: (0, 2 * j + 1), pipeline_mode=pl.Buffered(1)),
                  pl.BlockSpec((LANE, T), lambda j, i: (2 * npair + j, 0), pipeline_mode=pl.Buffered(1))],
        out_specs=[pl.BlockSpec((tq, LANE), lambda j, i: (i, j)), rowl],
        out_shape=[jax.ShapeDtypeStruct((T, D_ATTN), F32), jax.ShapeDtypeStruct((npair, 8, T), F32)],
        scratch_shapes=[pltpu.VMEM((2, 1, tq), F32), pltpu.VMEM((2, 1, tq), F32), pltpu.VMEM((2, LANE, tq), F32)],
        compiler_params=_cp(2),
    )(qkv, kb, kb, vt)


def _attn_bwd(qkv, qkvt, dob, dobt, cb, alrow, dlrow):
    T = qkv.shape[0]
    tq = _tile(T, 512)
    tk = tq
    nq = T // tq
    npair = N_HEADS // 2

    def body(q_ref, qt_ref, k_ref, kt_ref, v_ref, do_ref, dot_ref, cb_ref, al_ref, dl_ref,
             dq_ref, dk_ref, dv_ref, dc_ref, dcq_ref, dk_s, dv_s, dc_s):
        kj = pl.program_id(1)
        is_a = _head_masks()
        ks = _split_heads(k_ref[...], is_a)
        vs = _split_heads(v_ref[...], is_a)
        dim_a = lax.broadcasted_iota(jnp.int32, (LANE, 1), 0) < HEAD_DIM
        kts = _split_heads(kt_ref[...], dim_a)
        head_lane = lax.broadcasted_iota(jnp.int32, (1, LANE), 1) - 2 * pl.program_id(0)
        cs = tuple(jnp.sum(jnp.where(head_lane == h, cb_ref[...], 0.0), axis=-1, keepdims=True) for h in range(2))

        @pl.when(kj == 0)
        def _():
            dq_ref[...] = jnp.zeros_like(dq_ref)
            dcq_ref[...] = jnp.zeros_like(dcq_ref)

        dk_s[...] = jnp.zeros_like(dk_s)
        dv_s[...] = jnp.zeros_like(dv_s)
        dc_s[...] = jnp.zeros_like(dc_s)

        def step(qi, masked):
            q0 = pl.multiple_of(qi * tq, tq)
            q2 = q_ref[pl.ds(q0, tq), :]
            do2 = do_ref[pl.ds(q0, tq), :]
            qt2 = qt_ref[:, pl.ds(q0, tq)]
            dot2 = dot_ref[:, pl.ds(q0, tq)]
            for h in range(2):
                alr = al_ref[0, h:h + 1, pl.ds(q0, tq)]
                dlr = dl_ref[0, h:h + 1, pl.ds(q0, tq)]
                zt = _nt(ks[h], q2) + (alr - cs[h])
                if masked:
                    rr = lax.broadcasted_iota(jnp.int32, (tk, tq), 0)
                    cc = lax.broadcasted_iota(jnp.int32, (tk, tq), 1)
                    zt = jnp.where(cc >= rr, zt, NEG_INF)
                pt = jnp.exp2(zt)
                dst = pt * (_nt(vs[h], do2) - dlr)
                pb = pt.astype(BF16)
                dsb = dst.astype(BF16)
                dv_s[h] += _nt(dot2, pb)
                dk_s[h] += _nt(qt2, dsb)
                dc_s[h] += jnp.sum(dst, axis=-1, keepdims=True)
                dcq_ref[0, h:h + 1, pl.ds(q0, tq)] += jnp.sum(dst, axis=0, keepdims=True)
                dq_ref[:, pl.ds(q0, tq)] += _nn(kts[h], dsb)

        rest = nq - 1 - kj
        for left in range(ATTN_GROUP):
            @pl.when(rest % ATTN_GROUP == left)
            def _():
                for u in range(left + 1):
                    step(kj + u, u == 0)

        def loop_body(t, carry):
            for u in range(ATTN_GROUP):
                step(kj + 1 + rest % ATTN_GROUP + ATTN_GROUP * t + u, False)
            return carry

        lax.fori_loop(0, rest // ATTN_GROUP, loop_body, 0)
        dk_ref[...] = (jnp.where(dim_a, dk_s[0], dk_s[1]) * (1.0 / LOG2E)).astype(BF16)
        dv_ref[...] = jnp.where(dim_a, dv_s[0], dv_s[1]).astype(BF16)
        lane = lax.broadcasted_iota(jnp.int32, (1, LANE), 1)
        head = 2 * pl.program_id(0)
        dc_ref[...] = jnp.where(lane == head, -dc_s[0], jnp.where(lane == head + 1, -dc_s[1], 0.0))

        @pl.when(kj == nq - 1)
        def _():
            dq_ref[...] = dq_ref[...] * Q_SCALE

    full = lambda col: pl.BlockSpec((T, LANE), lambda j, kj: (0, col(j)), pipeline_mode=pl.Buffered(1))
    fullt = lambda row: pl.BlockSpec((LANE, T), lambda j, kj: (row(j), 0), pipeline_mode=pl.Buffered(1))
    tile = lambda col: pl.BlockSpec((tk, LANE), lambda j, kj: (kj, col(j)))
    tilet = lambda row: pl.BlockSpec((LANE, tk), lambda j, kj: (row(j), kj))
    rowl = pl.BlockSpec((1, 8, T), lambda j, kj: (j, 0, 0))
    return pl.pallas_call(
        body, name="attn_bwd", grid=(npair, nq),
        in_specs=[full(lambda j: j), fullt(lambda j: j), tile(lambda j: npair + j), tilet(lambda j: npair + j),
                  tile(lambda j: 2 * npair + j), full(lambda j: j), fullt(lambda j: j), tile(lambda j: 0), rowl, rowl],
        out_specs=[pl.BlockSpec((LANE, T), lambda j, kj: (j, 0)), tilet(lambda j: j), tilet(lambda j: j),
                   pl.BlockSpec((None, tk, LANE), lambda j, kj: (j, kj, 0)), rowl],
        out_shape=[jax.ShapeDtypeStruct((D_ATTN, T), F32), jax.ShapeDtypeStruct((D_ATTN, T), BF16),
                   jax.ShapeDtypeStruct((D_ATTN, T), BF16), jax.ShapeDtypeStruct((npair, T, LANE), F32),
                   jax.ShapeDtypeStruct((npair, 8, T), F32)],
        scratch_shapes=[pltpu.VMEM((2, LANE, tk), F32), pltpu.VMEM((2, LANE, tk), F32), pltpu.VMEM((2, tk, 1), F32)],
        compiler_params=_cp(2),
    )(qkv, qkvt, qkv, qkvt, qkv, dob, dobt, cb, alrow, dlrow)


HALO = 8


def _shift_rows(cur, other, k, tm, down):
    row = lax.broadcasted_iota(jnp.int32, (tm, 1), 0)
    reps = tm // HALO
    if down:
        rolled = pltpu.roll(cur, k, 0)
        fill = jnp.tile(pltpu.roll(other, k, 0), (reps, 1))
        return jnp.where(row < k, fill, rolled)
    rolled = pltpu.roll(cur, tm - k, 0)
    fill = jnp.tile(pltpu.roll(other, HALO - k, 0), (reps, 1))
    return jnp.where(row >= tm - k, fill, rolled)


def _conv_fwd(c, hh, c_prev, hh_prev, w_ref, first, tm):
    u = c * hh
    u_prev = jnp.where(first, 0.0, c_prev * hh_prev)
    u1 = _shift_rows(u, u_prev, 1, tm, True)
    u2 = _shift_rows(u, u_prev, 2, tm, True)
    y = w_ref[0:1, :] * u2 + w_ref[1:2, :] * u1 + w_ref[2:3, :] * u
    return u, u1, u2, y


def _rms(x, g):
    rs = lax.rsqrt(jnp.mean(x * x, axis=-1, keepdims=True) + RMS_EPS)
    return x * rs * g, rs


def _mixer_tail_fwd(o, bchf, conv_w, g_attn, g_conv, w_mo, x1, lg, lb):
    T = o.shape[0]
    tm = _tile(T, 512)
    hb = tm // HALO

    def body(o_ref, b_ref, c_ref, h_ref, cp_ref, hp_ref, w_ref, ga_ref, gc_ref, wmo_ref, x1_ref, lg_ref, lb_ref,
             x2_ref, r2_ref, mg_ref):
        first = pl.program_id(0) == 0
        _, _, _, y = _conv_fwd(c_ref[...], h_ref[...], cp_ref[...], hp_ref[...], w_ref, first, tm)
        na, _ = _rms(o_ref[...], ga_ref[...])
        nc, _ = _rms(b_ref[...] * y, gc_ref[...])
        nab = na.astype(BF16)
        ncb = nc.astype(BF16)
        mg_ref[:, 0:D_ATTN] = nab
        mg_ref[:, D_ATTN:] = ncb
        r2 = ALPHA * x1_ref[...] + _nn(nab, wmo_ref[0:D_ATTN, :]) + _nn(ncb, wmo_ref[D_ATTN:, :])
        r2_ref[...] = r2
        xhat, _ = _ln_stats(r2)
        x2_ref[...] = xhat * lg_ref[...] + lb_ref[...]

    row = lambda n, col=0: pl.BlockSpec((tm, n), lambda i: (i, col))
    prev = lambda col: pl.BlockSpec((HALO, D_CONV), lambda i: (jnp.maximum(i * hb - 1, 0), col))
    return pl.pallas_call(
        body, name="mixer_tail_fwd", grid=(T // tm,),
        in_specs=[row(D_ATTN), row(D_CONV, 0), row(D_CONV, 1), row(D_CONV, 2), prev(1), prev(2),
                  _resident((3, D_CONV)), _resident((1, D_ATTN)), _resident((1, D_CONV)),
                  _resident((D_MODEL, D_MODEL)), row(D_MODEL), _resident((1, D_MODEL)), _resident((1, D_MODEL))],
        out_specs=[row(D_MODEL), row(D_MODEL), row(D_MODEL)],
        out_shape=[jax.ShapeDtypeStruct((T, D_MODEL), F32), jax.ShapeDtypeStruct((T, D_MODEL), F32),
                   jax.ShapeDtypeStruct((T, D_MODEL), BF16)],
        compiler_params=_cp(1),
    )(o, bchf, bchf, bchf, bchf, bchf, conv_w, g_attn, g_conv, w_mo, x1, lg, lb)


def _head_sum_rows():
    row = lax.broadcasted_iota(jnp.int32, (4 * 8, D_ATTN), 0)
    head = lax.broadcasted_iota(jnp.int32, (4 * 8, D_ATTN), 1) // HEAD_DIM
    return jnp.where((row % 8 < 2) & (2 * (row // 8) + row % 8 == head), 1.0, 0.0).astype(F32)


def _mixer_tail_bwd(dx2, r2, lg, w_mo, o, bchf, conv_w, g_attn, g_conv, swap=()):
    T = o.shape[0]
    tm = _tile(T, 256)
    hb = tm // HALO
    ns = len(swap)
    last = T // tm - 1

    def body(dx2_ref, r2_ref, lg_ref, wmo_ref, o_ref, b_ref, c_ref, h_ref, cp_ref, hp_ref, w_ref, ga_ref, gc_ref,
             *rest):
        comm_in = rest[:ns]
        dx1_ref, dr_ref, do_ref, dot_ref, dl_ref, dco_ref, dlg_ref, dlb_ref, dga_ref, dgc_ref = rest[ns:ns + 10]
        comm_out, sems = rest[ns + 10:2 * ns + 10], rest[2 * ns + 10:]
        i = pl.program_id(0)
        if ns:
            @pl.when(i == 0)
            def _():
                _swap_start(comm_in, comm_out, *sems)

            @pl.when(i == last)
            def _():
                _swap_finish(comm_in, comm_out, *sems)

        @pl.when(i == 0)
        def _():
            for ref in (dlg_ref, dlb_ref, dga_ref, dgc_ref):
                ref[...] = jnp.zeros_like(ref)

        dy = dx2_ref[...]
        xhat, rstd = _ln_stats(r2_ref[...])
        dr = _ln_bwd(dy, xhat, rstd, lg_ref[...])
        dlg_ref[...] += _rowsum(dy * xhat)
        dlb_ref[...] += _rowsum(dy)
        dx1_ref[...] = ALPHA * dr
        drb = dr.astype(BF16)
        dr_ref[...] = drb
        dna = _nt(drb, wmo_ref[0:D_ATTN, :])
        dnc = _nt(drb, wmo_ref[D_ATTN:, :])

        def rms_bwd(x, g, dn):
            rs = lax.rsqrt(jnp.mean(x * x, axis=-1, keepdims=True) + RMS_EPS)
            dng = dn * g
            dx = rs * dng - x * (rs * rs * rs) * jnp.mean(dng * x, axis=-1, keepdims=True)
            return dx, _rowsum(dn * x * rs)

        oo = o_ref[...]
        do, dga = rms_bwd(oo, ga_ref[...], dna)
        dga_ref[...] += dga
        do_ref[...] = do.astype(BF16)
        dot_ref[...] = do.T.astype(BF16)
        dl_ref[...] = lax.dot_general(_head_sum_rows(), do * oo, (((1,), (1,)), ((), ())),
                                      preferred_element_type=F32, precision=lax.Precision.HIGHEST)
        _, _, _, y = _conv_fwd(c_ref[...], h_ref[...], cp_ref[...], hp_ref[...], w_ref, i == 0, tm)
        dco, dgc = rms_bwd(b_ref[...] * y, gc_ref[...], dnc)
        dgc_ref[...] += dgc
        dco_ref[...] = dco

    row = lambda n, col=0: pl.BlockSpec((tm, n), lambda i: (i, col))
    prev = lambda col: pl.BlockSpec((HALO, D_CONV), lambda i: (jnp.maximum(i * hb - 1, 0), col))
    vec = lambda n: _resident((1, n))
    return pl.pallas_call(
        body, name="mixer_tail_bwd", grid=(T // tm,),
        in_specs=[row(D_MODEL), row(D_MODEL), vec(D_MODEL), _resident((D_MODEL, D_MODEL)), row(D_ATTN),
                  row(D_CONV, 0), row(D_CONV, 1), row(D_CONV, 2), prev(1), prev(2), _resident((3, D_CONV)),
                  vec(D_ATTN), vec(D_CONV)] + [ANY_SPEC] * ns,
        out_specs=[row(D_MODEL), row(D_MODEL), row(D_ATTN), pl.BlockSpec((D_ATTN, tm), lambda i: (0, i)),
                   pl.BlockSpec((4 * 8, tm), lambda i: (0, i)), row(D_CONV),
                   vec(D_MODEL), vec(D_MODEL), vec(D_ATTN), vec(D_CONV)] + [ANY_SPEC] * ns,
        out_shape=[jax.ShapeDtypeStruct((T, D_MODEL), F32), jax.ShapeDtypeStruct((T, D_MODEL), BF16),
                   jax.ShapeDtypeStruct((T, D_ATTN), BF16), jax.ShapeDtypeStruct((D_ATTN, T), BF16),
                   jax.ShapeDtypeStruct((4 * 8, T), F32),
                   jax.ShapeDtypeStruct((T, D_CONV), F32), jax.ShapeDtypeStruct((1, D_MODEL), F32),
                   jax.ShapeDtypeStruct((1, D_MODEL), F32), jax.ShapeDtypeStruct((1, D_ATTN), F32),
                   jax.ShapeDtypeStruct((1, D_CONV), F32)] + _swap_shapes(swap),
        scratch_shapes=_swap_sems(ns) if ns else [],
        compiler_params=_cp(1),
    )(dx2, r2, lg, w_mo, o, bchf, bchf, bchf, bchf, bchf, conv_w, g_attn, g_conv, *swap)


def _conv_bwd(dco, bchf, conv_w):
    T = dco.shape[0]
    tm = _tile(T, 512)
    hb = tm // HALO
    nt = T // tm

    def body(dco_ref, dcon_ref, b_ref, bn_ref, c_ref, h_ref, cp_ref, hp_ref, w_ref, dbch_ref, dw_ref):
        i = pl.program_id(0)

        @pl.when(i == 0)
        def _():
            dw_ref[...] = jnp.zeros_like(dw_ref)

        cc = c_ref[...]
        hh = h_ref[...]
        u, u1, u2, y = _conv_fwd(cc, hh, cp_ref[...], hp_ref[...], w_ref, i == 0, tm)
        dco = dco_ref[...]
        bb = b_ref[...]
        dyc = dco * bb
        dy_next = jnp.where(i == nt - 1, 0.0, dcon_ref[...] * bn_ref[...])
        d1 = _shift_rows(dyc, dy_next, 1, tm, False)
        d2 = _shift_rows(dyc, dy_next, 2, tm, False)
        du = w_ref[2:3, :] * dyc + w_ref[1:2, :] * d1 + w_ref[0:1, :] * d2
        dbch_ref[:, 0:D_CONV] = (dco * y).astype(BF16)
        dbch_ref[:, D_CONV:2 * D_CONV] = (du * hh).astype(BF16)
        dbch_ref[:, 2 * D_CONV:] = (du * cc).astype(BF16)
        dw_ref[0:1, :] += _rowsum(dyc * u2)
        dw_ref[1:2, :] += _rowsum(dyc * u1)
        dw_ref[2:3, :] += _rowsum(dyc * u)

    row = lambda n, col=0: pl.BlockSpec((tm, n), lambda i: (i, col))
    prev = lambda col: pl.BlockSpec((HALO, D_CONV), lambda i: (jnp.maximum(i * hb - 1, 0), col))
    nxt = lambda col: pl.BlockSpec((HALO, D_CONV), lambda i: (jnp.minimum((i + 1) * hb, T // HALO - 1), col))
    return pl.pallas_call(
        body, name="conv_bwd", grid=(nt,),
        in_specs=[row(D_CONV), nxt(0), row(D_CONV, 0), nxt(0), row(D_CONV, 1), row(D_CONV, 2), prev(1), prev(2),
                  _resident((3, D_CONV))],
        out_specs=[row(3 * D_CONV), _resident((8, D_CONV))],
        out_shape=[jax.ShapeDtypeStruct((T, 3 * D_CONV), BF16), jax.ShapeDtypeStruct((8, D_CONV), F32)],
        compiler_params=_cp(1),
    )(dco, dco, bchf, bchf, bchf, bchf, bchf, bchf, conv_w)


def _mixer_in_bwd(dx1a, dqt, dkt, dvt, dbch, dfl, w_qkvt, w_bch, w_f):
    T = dx1a.shape[0]
    tm = _tile(T, 512)

    def body(a_ref, dq_ref, dk_ref, dv_ref, db_ref, df_ref, wq_ref, wb_ref, wf_ref, o_ref):
        acc = a_ref[...] + _nt(db_ref[...], wb_ref[...]) + _nt(df_ref[...], wf_ref[...])
        for n, ref in enumerate((dq_ref, dk_ref, dv_ref)):
            acc = acc + _tn(ref[...].astype(BF16), wq_ref[n * D_ATTN:(n + 1) * D_ATTN, :])
        o_ref[...] = acc

    row = lambda n: pl.BlockSpec((tm, n), lambda i: (i, 0))
    col = pl.BlockSpec((D_ATTN, tm), lambda i: (0, i))
    return pl.pallas_call(
        body, name="mixer_in_bwd", grid=(T // tm,),
        in_specs=[row(D_MODEL), col, col, col, row(3 * D_CONV), row(N_FLOG),
                  _resident((3 * D_ATTN, D_MODEL)), _resident((D_MODEL, 3 * D_CONV)), _resident((D_MODEL, N_FLOG))],
        out_specs=row(D_MODEL),
        out_shape=jax.ShapeDtypeStruct((T, D_MODEL), F32),
        compiler_params=_cp(1),
    )(dx1a, dqt, dkt, dvt, dbch, dfl, w_qkvt, w_bch, w_f)


def _ple_loss(x3, p, tgt, w_g, w_p, b_g, lg, lb):
    T = x3.shape[0]
    tm = _tile(T, 512)

    def body(x_ref, p_ref, t_ref, wg_ref, wp_ref, bg_ref, lg_ref, lb_ref,
             dx_ref, de_ref, dz_ref, loss_ref, dlg_ref, dlb_ref, dbg_ref):
        @pl.when(pl.program_id(0) == 0)
        def _():
            for ref in (loss_ref, dlg_ref, dlb_ref, dbg_ref):
                ref[...] = jnp.zeros_like(ref)

        xf = x_ref[...]
        gate = _sigmoid(_nn(xf.astype(BF16), wg_ref[...]) + bg_ref[...])
        e = _nn(p_ref[...].astype(BF16), wp_ref[...])
        xhat, rstd = _ln_stats(ALPHA * xf + gate * e)
        err = xhat * lg_ref[...] + lb_ref[...] - t_ref[...]
        sq = jnp.sum(_rowsum(err * err), axis=-1, keepdims=True)
        loss_ref[...] += jnp.broadcast_to(sq * (0.5 / D_MODEL), loss_ref.shape)
        dy = err * (1.0 / D_MODEL)
        dr = _ln_bwd(dy, xhat, rstd, lg_ref[...])
        dlg_ref[...] += _rowsum(dy * xhat)
        dlb_ref[...] += _rowsum(dy)
        de_ref[...] = (dr * gate).astype(BF16)
        dz = dr * e * gate * (1.0 - gate)
        dbg_ref[...] += _rowsum(dz)
        dzb = dz.astype(BF16)
        dz_ref[...] = dzb
        dx_ref[...] = ALPHA * dr + _nt(dzb, wg_ref[...])

    row = lambda n: pl.BlockSpec((tm, n), lambda i: (i, 0))
    vec = lambda n: _resident((1, n))
    return pl.pallas_call(
        body, name="ple_loss", grid=(T // tm,),
        in_specs=[row(D_MODEL), row(PLE_DIM), row(D_MODEL), _resident((D_MODEL, D_MODEL)),
                  _resident((PLE_DIM, D_MODEL)), vec(D_MODEL), vec(D_MODEL), vec(D_MODEL)],
        out_specs=[row(D_MODEL), row(D_MODEL), row(D_MODEL), vec(LANE), vec(D_MODEL), vec(D_MODEL), vec(D_MODEL)],
        out_shape=[jax.ShapeDtypeStruct((T, D_MODEL), F32), jax.ShapeDtypeStruct((T, D_MODEL), BF16),
                   jax.ShapeDtypeStruct((T, D_MODEL), BF16), jax.ShapeDtypeStruct((1, LANE), F32),
                   jax.ShapeDtypeStruct((1, D_MODEL), F32), jax.ShapeDtypeStruct((1, D_MODEL), F32),
                   jax.ShapeDtypeStruct((1, D_MODEL), F32)],
        compiler_params=_cp(1),
    )(x3, p, tgt, w_g, w_p, b_g, lg, lb)


def _lane_layout(v8):
    return jnp.repeat(v8, HEAD_DIM, axis=1)


def _row_layout(v8):
    t = v8.shape[0]
    return jnp.pad(v8.T.reshape(N_HEADS // 2, 2, t), ((0, 0), (0, 6), (0, 0)))


def _from_lane_layout(vl):
    return vl[:, ::HEAD_DIM]


def _from_row_layout(vr):
    return vr[:, :2, :].reshape(N_HEADS, -1).T


def _local_step(x, p, tgt, w, overlap=None):
    bf = lambda a: a.astype(BF16)
    w1i, w1o = bf(w["ffn1_w_in"]), bf(w["ffn1_w_out"])
    first = _ffn_fwd(x, w1i, w1o, w["ln1_g"], w["ln1_b"], "ffn1_fwd", overlap["gather"] if overlap else ())
    x1, r1, g1, u1, h1 = first[:5]
    if overlap:
        w = {**w, **overlap["weights"](first[5:])}
    w2i, w2o = bf(w["ffn2_w_in"]), bf(w["ffn2_w_out"])
    wmi = w["w_mix_in"]
    o_f = 3 * D_ATTN
    o_b = o_f + N_HEADS
    w_qkv = bf(wmi[:, :o_f])
    w_f = bf(jnp.pad(wmi[:, o_f:o_b], ((0, 0), (0, N_FLOG - N_HEADS))))
    w_bch = bf(wmi[:, o_b:])
    w_bchf = jnp.concatenate([w_bch, w_f], axis=1)
    w_mo, w_g, w_p = bf(w["w_mix_out"]), bf(w["w_ple_gate"]), bf(w["w_ple"])
    b_f = jnp.pad(w["b_forget"], ((0, 0), (0, N_FLOG - N_HEADS)))

    q_scale = jnp.concatenate([jnp.full((1, D_ATTN), Q_SCALE * LOG2E, F32), jnp.ones((1, 2 * D_ATTN), F32)], axis=1)
    qkv, qkvt = _matmul_nn(x1, w_qkv, q_scale, BF16, "proj_qkv", also_transposed=True)
    bchf = _matmul_nn(x1, w_bchf, jnp.ones((1, 3 * D_CONV + N_FLOG), F32), F32, "proj_bchf")
    fcol = 3 * D_CONV // N_FLOG
    c = _forget_cumsum(bchf, fcol, b_f)
    o, alrow = _attn_fwd(qkv, qkvt, _fold_key_bias(qkv, c))
    x2, r2, merged = _mixer_tail_fwd(o, bchf, w["conv_w"], w["g_attn"], w["g_conv"], w_mo, x1, w["ln2_g"], w["ln2_b"])
    x3, r3, g2, u2, h2 = _ffn_fwd(x2, w2i, w2o, w["ln3_g"], w["ln3_b"], "ffn2_fwd")

    grads = {}
    dx3, de, dz, loss, grads["ln4_g"], grads["ln4_b"], grads["b_ple_gate"] = _ple_loss(
        x3, p, tgt, w_g, w_p, w["b_ple_gate"], w["ln4_g"], w["ln4_b"])
    by_chip = overlap is not None
    grads["w_ple"] = _matmul_tn(p, de, "dw_ple", by_chip)
    grads["w_ple_gate"] = _matmul_tn(x3, dz, "dw_ple_gate")

    dx2, dgu2, df2, grads["ln3_g"], grads["ln3_b"] = _ffn_bwd(dx3, r3, g2, u2, w2i, w2o, w["ln3_g"], "ffn2_bwd")
    grads["ffn2_w_in"] = _matmul_tn(x2, dgu2, "dw_ffn2_in", by_chip)
    grads["ffn2_w_out"] = _matmul_tn(h2, df2, "dw_ffn2_out")

    to_swap = overlap["swap"](grads) if overlap else ()
    tail = _mixer_tail_bwd(dx2, r2, w["ln2_g"], w_mo, o, bchf, w["conv_w"], w["g_attn"], w["g_conv"], to_swap)
    (dx1a, dr2, dob, dobt, delta, dco, grads["ln2_g"], grads["ln2_b"], grads["g_attn"], grads["g_conv"]) = tail[:10]
    grads["w_mix_out"] = _matmul_tn(merged, dr2, "dw_mix_out")
    dbch, dcw = _conv_bwd(dco, bchf, w["conv_w"])
    grads["conv_w"] = dcw[:3]
    dqt, dkt, dvt, dck, dcq = _attn_bwd(qkv, qkvt, dob, dobt, c, alrow, delta.reshape(N_HEADS // 2, 8, -1))
    dcq_lanes = jnp.pad(_from_row_layout(dcq), ((0, 0), (0, N_FLOG - N_HEADS)))
    dfl, dbf = _forget_bwd(dck, dcq_lanes, bchf, fcol, b_f)
    grads["b_forget"] = dbf[:, :N_HEADS]
    dx1 = _mixer_in_bwd(dx1a, dqt, dkt, dvt, dbch, dfl, w_qkv.T, w_bch, w_f)
    grads["w_mix_in"] = jnp.concatenate(
        [_matmul_tokens(dqt, x1, "dw_q").T, _matmul_tokens(dkt, x1, "dw_k").T, _matmul_tokens(dvt, x1, "dw_v").T,
         _matmul_tn(x1, dfl, "dw_flog")[:, :N_HEADS], _matmul_tn(x1, dbch, "dw_bch")], axis=1)

    dx0, dgu1, df1, grads["ln1_g"], grads["ln1_b"] = _ffn_bwd(dx1, r1, g1, u1, w1i, w1o, w["ln1_g"], "ffn1_bwd")
    grads["ffn1_w_out"] = _matmul_tn(h1, df1, "dw_ffn1_out")
    if not overlap:
        grads["ffn1_w_in"] = _matmul_tn(x, dgu1, "dw_ffn1_in")
        return loss, dx0, grads
    sums = overlap["chip_sums"](grads, to_swap, tail[10:])
    grads["ffn1_w_in"], *received = _matmul_tn(x, dgu1, "dw_ffn1_in", by_chip, exchange=sums)
    return loss, dx0, grads, sums, received


WEIGHTS = ["ffn1_w_in", "ffn1_w_out", "ln1_g", "ln1_b", "w_mix_in", "b_forget", "conv_w", "g_attn", "g_conv",
           "w_mix_out", "ln2_g", "ln2_b", "ffn2_w_in", "ffn2_w_out", "ln3_g", "ln3_b", "w_ple", "w_ple_gate",
           "b_ple_gate", "ln4_g", "ln4_b"]
LAYOUT = {
    "ffn1_w_in": ((D_MODEL, 2 * D_FF), 1), "ffn1_w_out": ((D_FF, D_MODEL), 0),
    "w_mix_in": ((D_MODEL, 3 * D_ATTN + N_HEADS + 3 * D_CONV), 1), "conv_w": ((3, D_CONV), 1),
    "w_mix_out": ((D_MODEL, D_MODEL), 0), "ffn2_w_in": ((D_MODEL, 2 * D_FF), 1), "ffn2_w_out": ((D_FF, D_MODEL), 0),
    "w_ple": ((PLE_DIM, D_MODEL), 1), "w_ple_gate": ((D_MODEL, D_MODEL), 0),
    "ln1_g": ((1, D_MODEL), None), "ln1_b": ((1, D_MODEL), None), "b_forget": ((1, N_HEADS), None),
    "g_attn": ((1, D_ATTN), None), "g_conv": ((1, D_CONV), None), "ln2_g": ((1, D_MODEL), None),
    "ln2_b": ((1, D_MODEL), None), "ln3_g": ((1, D_MODEL), None), "ln3_b": ((1, D_MODEL), None),
    "b_ple_gate": ((1, D_MODEL), None), "ln4_g": ((1, D_MODEL), None), "ln4_b": ((1, D_MODEL), None),
}
BIG = [n for n in WEIGHTS if LAYOUT[n][1] is not None and n != "conv_w"]
SMALL = [n for n in WEIGHTS if n not in BIG]
ROW = 1024
SMALL_ROWS = 16


def _shard_shape(name):
    shape, axis = LAYOUT[name]
    if axis is None:
        return shape
    return tuple(s // N_CHIPS if a == axis else s for a, s in enumerate(shape))


def _halves(a):
    return a.reshape(a.shape[:-2] + (2, a.shape[-2] // 2, a.shape[-1]))


def _split_chips(name, full):
    shape, axis = LAYOUT[name]
    if axis == 0:
        return full.reshape((N_CHIPS, shape[0] // N_CHIPS) + shape[1:])
    return jnp.moveaxis(full.reshape(shape[:1] + (N_CHIPS, shape[1] // N_CHIPS)), 1, 0)


def _join_chips(name, parts):
    shape, axis = LAYOUT[name]
    if axis == 0:
        return parts.reshape(shape)
    return jnp.moveaxis(parts, 0, 1).reshape(shape)


SMALL_AT = {"ln1_g": (0, 0), "ln1_b": (1, 0), "ln2_g": (2, 0), "ln2_b": (3, 0), "ln3_g": (4, 0), "ln3_b": (5, 0),
            "b_ple_gate": (6, 0), "ln4_g": (7, 0), "ln4_b": (8, 0), "g_attn": (9, 0), "g_conv": (9, D_ATTN),
            "b_forget": (10, 0), "conv_w": (10, LANE)}
CONV_SHARD = D_CONV // N_CHIPS


def _pack_small_grads(grads):
    def body(*refs):
        ins, o_ref = dict(zip(SMALL, refs[:-1])), refs[-1]
        o_ref[...] = jnp.zeros_like(o_ref)
        for s in range(N_CHIPS):
            for n in SMALL:
                r, c0 = SMALL_AT[n]
                if n == "conv_w":
                    for k in range(3):
                        o_ref[s, r:r + 1, c0 + k * CONV_SHARD:c0 + (k + 1) * CONV_SHARD] = (
                            ins[n][k:k + 1, s * CONV_SHARD:(s + 1) * CONV_SHARD])
                else:
                    o_ref[s, r:r + 1, c0:c0 + ins[n].shape[1]] = ins[n][...]

    return pl.pallas_call(
        body, name="pack_small_grads",
        out_shape=jax.ShapeDtypeStruct((N_CHIPS, SMALL_ROWS, ROW), F32),
    )(*[grads[n] for n in SMALL])


def _adamw_math(w, g, m, v):
    c1 = 1.0 - ADAM_B1 ** ADAM_STEP
    c2 = 1.0 - ADAM_B2 ** ADAM_STEP
    m = ADAM_B1 * m + (1.0 - ADAM_B1) * g
    v = ADAM_B2 * v + (1.0 - ADAM_B2) * (g * g)
    return -ADAM_LR * ((m / c1) / (jnp.sqrt(v / c2) + ADAM_EPS) + ADAM_WD * w), m, v


def _adamw_small(g_mine, g_sib, c_idx, w, m, v):
    ns = len(SMALL)

    def body(c_ref, gm_ref, gs_ref, *refs):
        ws, ms, vs = refs[:ns], refs[ns:2 * ns], refs[2 * ns:3 * ns]
        outs = refs[3 * ns:]
        mine_first = c_ref[0] == 0
        top = jnp.where(mine_first, gm_ref[...], gs_ref[...])
        bot = jnp.where(mine_first, gs_ref[...], gm_ref[...])
        for i, n in enumerate(SMALL):
            r, c0 = SMALL_AT[n]
            blk, rr = (top, r) if r < SMALL_ROWS // 2 else (bot, r - SMALL_ROWS // 2)
            rows, width = ws[i].shape
            for k in range(rows):
                g = blk[rr:rr + 1, c0 + k * width:c0 + (k + 1) * width]
                d, mn, vn = _adamw_math(ws[i][k:k + 1, :], g, ms[i][k:k + 1, :], vs[i][k:k + 1, :])
                for q, val in enumerate((g, d, mn, vn)):
                    outs[q * ns + i][k:k + 1, :] = val

    shapes = [jax.ShapeDtypeStruct(a.shape, F32) for a in w]
    vmem = pl.BlockSpec(memory_space=pltpu.VMEM)
    res = pl.pallas_call(
        body, name="adamw_small",
        in_specs=[pl.BlockSpec(memory_space=pltpu.SMEM)] + [vmem] * (2 + 3 * ns),
        out_specs=[vmem] * (4 * ns),
        out_shape=shapes * 4,
    )(c_idx, g_mine, g_sib, *w, *m, *v)
    return [res[q * ns:(q + 1) * ns] for q in range(4)]


def _place():
    x, y, c = lax.axis_index("x"), lax.axis_index("y"), lax.axis_index("c")
    others = [(1 - x, y), (x, 1 - y), (1 - x, 1 - y)]
    return x, y, c, others


ANY_SPEC = pl.BlockSpec(memory_space=pl.ANY)


def _remote(src, dst, send_sems, recv_sems, k, to):
    return pltpu.make_async_remote_copy(src_ref=src, dst_ref=dst, send_sem=send_sems.at[k], recv_sem=recv_sems.at[k],
                                        device_id=to, device_id_type=MESH)


def _all_gather(shards):
    n = len(shards)

    def body(*refs):
        ins, outs, send_sems, recv_sems = refs[:n], refs[n:2 * n], refs[2 * n], refs[2 * n + 1]
        _gather_start(ins, outs, send_sems, recv_sems)
        _gather_finish(ins, outs, send_sems, recv_sems)

    return pl.pallas_call(
        body, name="all_gather_weights",
        out_shape=_gather_shapes(shards), in_specs=[ANY_SPEC] * n, out_specs=[ANY_SPEC] * n,
        scratch_shapes=_gather_sems(n),
    )(*shards)


def _gather_shapes(shards):
    return [jax.ShapeDtypeStruct((N_CHIPS,) + a.shape, a.dtype) for a in shards]


def _gather_sems(n):
    return [pltpu.SemaphoreType.DMA((6 * n,)), pltpu.SemaphoreType.DMA((6 * n,))]


def _gather_sends(ins, outs, send_sems, recv_sems):
    x, y, c, others = _place()
    s = 2 * x + y
    return [_remote(ins[t].at[c], outs[t].at[s, c], send_sems, recv_sems, 6 * t + j, (*chip, c))
            for t in range(len(ins)) for j, chip in enumerate(others)]


def _gather_start(ins, outs, send_sems, recv_sems):
    for cp in _gather_sends(ins, outs, send_sems, recv_sems):
        cp.start()


def _gather_finish(ins, outs, send_sems, recv_sems):
    x, y, c, others = _place()
    slot = lambda t, chip, half: outs[t].at[2 * chip[0] + chip[1], half]
    passed = []
    for t in range(len(ins)):
        for j, chip in enumerate(others):
            landed = slot(t, chip, c)
            _remote(landed, landed, send_sems, recv_sems, 6 * t + j, (x, y, c)).wait_recv()
            passed.append(_remote(landed, landed, send_sems, recv_sems, 6 * t + 3 + j, (x, y, 1 - c)))
            passed[-1].start()
    for t in range(len(ins)):
        for j, chip in enumerate(others):
            landed = slot(t, chip, 1 - c)
            _remote(landed, landed, send_sems, recv_sems, 6 * t + 3 + j, (x, y, c)).wait_recv()
    for cp in _gather_sends(ins, outs, send_sems, recv_sems) + passed:
        cp.wait_send()


def _swap_halves(gs, tag):
    n = len(gs)

    def body(*refs):
        ins, outs, send_sems, recv_sems = refs[:n], refs[n:2 * n], refs[2 * n], refs[2 * n + 1]
        _swap_start(ins, outs, send_sems, recv_sems)
        _swap_finish(ins, outs, send_sems, recv_sems)

    return pl.pallas_call(
        body, name="grad_swap_halves_" + tag,
        out_shape=_swap_shapes(gs), in_specs=[ANY_SPEC] * n, out_specs=[ANY_SPEC] * n,
        scratch_shapes=_swap_sems(n),
    )(*gs)


def _swap_shapes(gs):
    return [jax.ShapeDtypeStruct(g.shape[:1] + g.shape[2:], g.dtype) for g in gs]


def _swap_sems(n):
    return [pltpu.SemaphoreType.DMA((n,)), pltpu.SemaphoreType.DMA((n,))]


def _swap_copies(ins, outs, send_sems, recv_sems):
    x, y, c, _ = _place()
    return [_remote(ins[t].at[:, 1 - c], outs[t], send_sems, recv_sems, t, (x, y, 1 - c)) for t in range(len(ins))]


def _swap_start(ins, outs, send_sems, recv_sems):
    for cp in _swap_copies(ins, outs, send_sems, recv_sems):
        cp.start()


def _swap_finish(ins, outs, send_sems, recv_sems):
    for cp in _swap_copies(ins, outs, send_sems, recv_sems):
        cp.wait()


def _exchange_chips(pps):
    n = len(pps)

    def body(*refs):
        ins, outs, send_sems, recv_sems = refs[:n], refs[n:2 * n], refs[2 * n], refs[2 * n + 1]
        _exchange_start(ins, outs, send_sems, recv_sems)
        _exchange_finish(ins, outs, send_sems, recv_sems)

    return pl.pallas_call(
        body, name="grad_exchange_chips",
        out_shape=_exchange_shapes(pps), in_specs=[ANY_SPEC] * n, out_specs=[ANY_SPEC] * n,
        scratch_shapes=_exchange_sems(n),
    )(*pps)


def _exchange_shapes(pps):
    return [jax.ShapeDtypeStruct(p.shape, p.dtype) for p in pps]


def _exchange_sems(n):
    return [pltpu.SemaphoreType.DMA((3 * n,)), pltpu.SemaphoreType.DMA((3 * n,))]


def _exchange_sends(ins, outs, send_sems, recv_sems):
    x, y, c, others = _place()
    s = 2 * x + y
    return [_remote(ins[t].at[2 * chip[0] + chip[1]], outs[t].at[s], send_sems, recv_sems, 3 * t + j, (*chip, c))
            for t in range(len(ins)) for j, chip in enumerate(others)]


def _exchange_start(ins, outs, send_sems, recv_sems):
    for cp in _exchange_sends(ins, outs, send_sems, recv_sems):
        cp.start()


def _exchange_finish(ins, outs, send_sems, recv_sems):
    x, y, c, others = _place()
    for t in range(len(ins)):
        for j, chip in enumerate(others):
            landed = outs[t].at[2 * chip[0] + chip[1]]
            _remote(landed, landed, send_sems, recv_sems, 3 * t + j, (x, y, c)).wait_recv()
    for cp in _exchange_sends(ins, outs, send_sems, recv_sems):
        cp.wait_send()


def _share_half(rs):
    n = len(rs)

    def body(*refs):
        ins, outs, send_sems, recv_sems = refs[:n], refs[n:2 * n], refs[2 * n], refs[2 * n + 1]
        x, y, c, _ = _place()
        copies = [_remote(ins[t], outs[t], send_sems, recv_sems, t, (x, y, 1 - c)) for t in range(n)]
        for cp in copies:
            cp.start()
        for cp in copies:
            cp.wait()

    return pl.pallas_call(
        body, name="grad_share_half",
        out_shape=[jax.ShapeDtypeStruct(r.shape, r.dtype) for r in rs],
        in_specs=[ANY_SPEC] * n, out_specs=[ANY_SPEC] * n,
        scratch_shapes=[pltpu.SemaphoreType.DMA((n,)), pltpu.SemaphoreType.DMA((n,))],
    )(*rs)


ELEMENTWISE_BLOCK_BYTES = 1 << 20


def _row_tile(rows, cols):
    return _tile(rows, max(8, ELEMENTWISE_BLOCK_BYTES // (4 * cols) // 8 * 8))


def _add_my_half(g, sib, c_idx, name):
    rh, cols = g.shape[2:]
    tr = _row_tile(rh, cols)

    def body(c_ref, g_ref, s_ref, o_ref):
        o_ref[...] = (g_ref[...] + s_ref[...]).astype(BF16)

    return pl.pallas_call(
        body, name="grad_add_halves_" + name,
        grid_spec=pltpu.PrefetchScalarGridSpec(
            num_scalar_prefetch=1, grid=(N_CHIPS, rh // tr),
            in_specs=[pl.BlockSpec((None, None, tr, cols), lambda s, i, c: (s, c[0], i, 0)),
                      pl.BlockSpec((None, tr, cols), lambda s, i, c: (s, i, 0))],
            out_specs=pl.BlockSpec((None, tr, cols), lambda s, i, c: (s, i, 0))),
        out_shape=jax.ShapeDtypeStruct((N_CHIPS, rh, cols), BF16),
        compiler_params=_cp(2),
    )(c_idx, g, sib)


def _sum_chips(parts, pp, s_idx, name):
    rh, cols = parts.shape[1:]
    tr = _row_tile(rh, cols)

    def body(s_ref, p0, p1, p2, p3, mine_ref, o_ref):
        own = mine_ref[...]
        t = [jnp.where(s_ref[0] == k, own, p[...]).astype(F32) for k, p in enumerate((p0, p1, p2, p3))]
        o_ref[...] = ((t[0] + t[1]) + t[2]) + t[3]

    slot = lambda k: pl.BlockSpec((None, tr, cols), lambda i, s: (jnp.where(s[0] == k, (k + 1) % N_CHIPS, k), i, 0))
    return pl.pallas_call(
        body, name="grad_sum_chips_" + name,
        grid_spec=pltpu.PrefetchScalarGridSpec(
            num_scalar_prefetch=1, grid=(rh // tr,),
            in_specs=[slot(0), slot(1), slot(2), slot(3), pl.BlockSpec((None, tr, cols), lambda i, s: (s[0], i, 0))],
            out_specs=pl.BlockSpec((tr, cols), lambda i, s: (i, 0))),
        out_shape=jax.ShapeDtypeStruct((rh, cols), F32),
        compiler_params=_cp(1),
    )(s_idx, parts, parts, parts, parts, pp)


def _adamw(w, g_mine, g_sib, m, v, c_idx, name):
    rows, cols = w.shape
    tr = _row_tile(rows // 2, cols)
    nbh = rows // 2 // tr

    def body(c_ref, w_ref, gm_ref, gs_ref, m_ref, v_ref, g_ref, d_ref, mo_ref, vo_ref):
        g = jnp.where(pl.program_id(0) // nbh == c_ref[0], gm_ref[...], gs_ref[...])
        g_ref[...] = g
        d_ref[...], mo_ref[...], vo_ref[...] = _adamw_math(w_ref[...], g, m_ref[...], v_ref[...])

    spec = pl.BlockSpec((tr, cols), lambda i, c: (i, 0))
    half = pl.BlockSpec((tr, cols), lambda i, c: (i % nbh, 0))
    return pl.pallas_call(
        body, name="adamw_" + name,
        grid_spec=pltpu.PrefetchScalarGridSpec(
            num_scalar_prefetch=1, grid=(rows // tr,),
            in_specs=[spec, half, half, spec, spec], out_specs=[spec] * 4),
        out_shape=[jax.ShapeDtypeStruct(w.shape, F32)] * 4,
        compiler_params=_cp(1),
    )(c_idx, w, g_mine, g_sib, m, v)


def kernel(x, p, ffn1_w_in, ffn1_w_out, ln1_g, ln1_b, w_mix_in, b_forget, conv_w, g_attn, g_conv, w_mix_out, ln2_g, ln2_b, ffn2_w_in, ffn2_w_out, ln3_g, ln3_b, w_ple, w_ple_gate, b_ple_gate, ln4_g, ln4_b, loss_target, m_ffn1_w_in, m_ffn1_w_out, m_ln1_g, m_ln1_b, m_w_mix_in, m_b_forget, m_conv_w, m_g_attn, m_g_conv, m_w_mix_out, m_ln2_g, m_ln2_b, m_ffn2_w_in, m_ffn2_w_out, m_ln3_g, m_ln3_b, m_w_ple, m_w_ple_gate, m_b_ple_gate, m_ln4_g, m_ln4_b, v_ffn1_w_in, v_ffn1_w_out, v_ln1_g, v_ln1_b, v_w_mix_in, v_b_forget, v_conv_w, v_g_attn, v_g_conv, v_w_mix_out, v_ln2_g, v_ln2_b, v_ffn2_w_in, v_ffn2_w_out, v_ln3_g, v_ln3_b, v_w_ple, v_w_ple_gate, v_b_ple_gate, v_ln4_g, v_ln4_b):
    args = dict(locals())
    shard = {n: args[n][0] if LAYOUT[n][1] is not None else args[n] for n in WEIGHTS}
    m_shard = {n: args["m_" + n][0] if LAYOUT[n][1] is not None else args["m_" + n] for n in WEIGHTS}
    v_shard = {n: args["v_" + n][0] if LAYOUT[n][1] is not None else args["v_" + n] for n in WEIGHTS}
    c_idx = lax.axis_index("c").astype(jnp.int32).reshape(1)
    chip = (2 * lax.axis_index("x") + lax.axis_index("y")).astype(jnp.int32)

    conv_rows = SMALL_ROWS - shard["conv_w"].shape[0]
    mine = {n: _halves(shard[n].astype(BF16)) for n in BIG}
    mine["conv_w"] = _halves(jnp.pad(shard["conv_w"], ((0, conv_rows), (0, 0))))
    early_w = ["ffn1_w_in", "ffn1_w_out"]
    late_w = [n for n in BIG if n not in early_w] + ["conv_w"]

    def full_weights(names, gathered):
        out = {}
        for n, theirs in zip(names, gathered):
            g = lax.dynamic_update_slice(theirs, mine[n][None], (chip, 0, 0, 0))
            if n == "conv_w":
                out[n] = _join_chips(n, g.reshape(N_CHIPS, SMALL_ROWS, CONV_SHARD)[:, :3])
            else:
                out[n] = _join_chips(n, g.reshape((N_CHIPS,) + _shard_shape(n)))
        return out

    full = full_weights(early_w, _all_gather([mine[n] for n in early_w]))
    full.update({n: shard[n] for n in SMALL if n != "conv_w"})

    def per_chip(names, grads):
        by_chip = lambda n: grads[n] if grads[n].ndim == 3 else _split_chips(n, grads[n])
        return [_halves(_pack_small_grads(grads) if n == "small" else by_chip(n)) for n in names]

    def add_halves(names, mine_, sibs):
        return [_add_my_half(g, sib, c_idx, n) for n, g, sib in zip(names, mine_, sibs)]

    def chip_sums(names, grads):
        mine_ = per_chip(names, grads)
        return add_halves(names, mine_, _swap_halves(mine_, names[0]))

    ready_a = ["ffn2_w_in", "ffn2_w_out", "w_ple", "w_ple_gate"]
    ready_b = ["w_mix_in", "w_mix_out"]
    early_g = ready_a + ready_b
    late_g = early_w + ["small"]
    loss_acc, grad_x, grads, early_sums, early_parts = _local_step(
        x[0], p[0, 0], loss_target[0], full,
        overlap={"gather": [mine[n] for n in late_w], "weights": lambda gathered: full_weights(late_w, gathered),
                 "swap": lambda grads: per_chip(ready_a, grads),
                 "chip_sums": lambda grads, swapped, received: (add_halves(ready_a, swapped, received)
                                                                + chip_sums(ready_b, grads))})
    loss = lax.psum(loss_acc[0, 0], ("x", "y", "c"))

    late_sums = chip_sums(late_g, grads)
    names = early_g + late_g
    sums = list(early_sums) + late_sums
    parts = list(early_parts) + list(_exchange_chips(late_sums))
    half_of = {n: _sum_chips(pt, own, chip.reshape(1), n) for n, pt, own in zip(names, parts, sums)}
    names = BIG + ["small"]
    my_half = [half_of[n] for n in names]
    sib_half = _share_half(my_half)

    out = {}
    for n, gm, gs in zip(BIG, my_half, sib_half):
        out[n] = [a[None] for a in _adamw(shard[n], gm, gs, m_shard[n], v_shard[n], c_idx, n)]
    small = _adamw_small(my_half[-1], sib_half[-1], c_idx, [shard[n] for n in SMALL], [m_shard[n] for n in SMALL],
                         [v_shard[n] for n in SMALL])
    for i, n in enumerate(SMALL):
        out[n] = [small[q][i][None] if n == "conv_w" else small[q][i] for q in range(4)]
    return (loss, grad_x[None], *[out[n][q] for q in range(4) for n in WEIGHTS])
```

```python
import functools
import math

import jax
import jax.numpy as jnp
from jax import lax
from jax.experimental import pallas as pl
from jax.experimental.pallas import tpu as pltpu

F32 = jnp.float32
BF16 = jnp.bfloat16

D_MODEL = 1024
D_FF = 2816
N_HEADS = 8
HEAD_DIM = 64
D_ATTN = N_HEADS * HEAD_DIM
D_CONV = 512
PLE_DIM = 256
N_FLOG = 128
ALPHA = 2.0 ** 0.25
LN_EPS = 1e-5
RMS_EPS = 1e-6
NEG_INF = -1e30
Q_SCALE = 1.0 / math.sqrt(HEAD_DIM)
LOG2E = math.log2(math.e)

ADAM_LR = 0.001
ADAM_B1 = 0.9
ADAM_B2 = 0.999
ADAM_EPS = 1e-08
ADAM_WD = 0.01
ADAM_STEP = 10

V7X_VMEM_BYTES = 64 << 20
VMEM_LIMIT = V7X_VMEM_BYTES - (8 << 20)
LANE = 128
FF_CHUNK = 256
N_CHIPS = 4
MESH = pl.DeviceIdType.MESH


def _cp(n_axes):
    return pltpu.CompilerParams(dimension_semantics=("arbitrary",) * n_axes, vmem_limit_bytes=VMEM_LIMIT)


def _resident(shape):
    n = len(shape)
    return pl.BlockSpec(shape, lambda *_: (0,) * n, pipeline_mode=pl.Buffered(1))


def _nn(a, b):
    return jnp.dot(a, b, preferred_element_type=F32)


def _nt(a, b):
    return lax.dot_general(a, b, (((1,), (1,)), ((), ())), preferred_element_type=F32)


def _tn(a, b):
    return lax.dot_general(a, b, (((0,), (0,)), ((), ())), preferred_element_type=F32)


def _ln_stats(r):
    mu = jnp.mean(r, axis=-1, keepdims=True)
    xc = r - mu
    var = jnp.mean(xc * xc, axis=-1, keepdims=True)
    rstd = lax.rsqrt(var + LN_EPS)
    return xc * rstd, rstd


def _ln_bwd(dy, xhat, rstd, g):
    dxh = dy * g
    m1 = jnp.mean(dxh, axis=-1, keepdims=True)
    m2 = jnp.mean(dxh * xhat, axis=-1, keepdims=True)
    return rstd * (dxh - m1 - xhat * m2)


def _sigmoid(z):
    return 1.0 / (1.0 + jnp.exp(-z))


def _rowsum(a):
    return jnp.sum(a, axis=0, keepdims=True)


def _tile(total, want):
    if total <= want:
        return total
    for t in range(want - want % 8, 0, -8):
        if total % t == 0:
            return t
    raise ValueError((total, want))


def _ffn_fwd(x, w_in, w_out, lg, lb, name, gather=()):
    T = x.shape[0]
    tm = _tile(T, 512)
    nf = D_FF // FF_CHUNK
    ng = len(gather)
    last = T // tm - 1

    def body(x_ref, wi_ref, wo_ref, lg_ref, lb_ref, *rest):
        comm_in, (xo_ref, r_ref, g_ref, u_ref, h_ref) = rest[:ng], rest[ng:ng + 5]
        comm_out, sems = rest[ng + 5:2 * ng + 5], rest[2 * ng + 5:]
        if ng:
            @pl.when(pl.program_id(0) == 0)
            def _():
                _gather_start(comm_in, comm_out, *sems)

        xf = x_ref[...]
        xb = xf.astype(BF16)
        acc = jnp.zeros((tm, D_MODEL), F32)
        for j in range(nf):
            c0 = j * FF_CHUNK
            g = _nn(xb, wi_ref[:, c0:c0 + FF_CHUNK])
            u = _nn(xb, wi_ref[:, D_FF + c0:D_FF + c0 + FF_CHUNK])
            hb = (g * _sigmoid(g) * u).astype(BF16)
            g_ref[:, c0:c0 + FF_CHUNK] = g.astype(BF16)
            u_ref[:, c0:c0 + FF_CHUNK] = u.astype(BF16)
            h_ref[:, c0:c0 + FF_CHUNK] = hb
            acc = acc + _nn(hb, wo_ref[c0:c0 + FF_CHUNK, :])
        r = ALPHA * xf + 0.5 * acc
        r_ref[...] = r
        xhat, _ = _ln_stats(r)
        xo_ref[...] = xhat * lg_ref[...] + lb_ref[...]
        if ng:
            @pl.when(pl.program_id(0) == last)
            def _():
                _gather_finish(comm_in, comm_out, *sems)

    row = lambda n: pl.BlockSpec((tm, n), lambda i: (i, 0))
    return pl.pallas_call(
        body, name=name, grid=(T // tm,),
        in_specs=[row(D_MODEL), _resident((D_MODEL, 2 * D_FF)), _resident((D_FF, D_MODEL)),
                  _resident((1, D_MODEL)), _resident((1, D_MODEL))] + [ANY_SPEC] * ng,
        out_specs=[row(D_MODEL), row(D_MODEL), row(D_FF), row(D_FF), row(D_FF)] + [ANY_SPEC] * ng,
        out_shape=[jax.ShapeDtypeStruct((T, D_MODEL), F32), jax.ShapeDtypeStruct((T, D_MODEL), F32),
                   jax.ShapeDtypeStruct((T, D_FF), BF16), jax.ShapeDtypeStruct((T, D_FF), BF16),
                   jax.ShapeDtypeStruct((T, D_FF), BF16)] + _gather_shapes(gather),
        scratch_shapes=_gather_sems(ng) if ng else [],
        compiler_params=_cp(1),
    )(x, w_in, w_out, lg, lb, *gather)


def _ffn_bwd(dxo, r, g, u, w_in, w_out, lg, name, exchange=()):
    T = r.shape[0]
    tm = _tile(T, 256)
    nf = D_FF // FF_CHUNK
    ne = len(exchange)
    last = T // tm - 1

    def body(dxo_ref, r_ref, g_ref, u_ref, wi_ref, wo_ref, lg_ref, *rest):
        comm_in, (dx_ref, dgu_ref, df_ref, dlg_ref, dlb_ref) = rest[:ne], rest[ne:ne + 5]
        comm_out, sems = rest[ne + 5:2 * ne + 5], rest[2 * ne + 5:]
        i = pl.program_id(0)
        if ne:
            @pl.when(i == 0)
            def _():
                _exchange_start(comm_in, comm_out, *sems)

        dy = dxo_ref[...]
        xhat, rstd = _ln_stats(r_ref[...])
        dr = _ln_bwd(dy, xhat, rstd, lg_ref[...])

        @pl.when(i == 0)
        def _():
            dlg_ref[...] = jnp.zeros_like(dlg_ref)
            dlb_ref[...] = jnp.zeros_like(dlb_ref)

        dlg_ref[...] += _rowsum(dy * xhat)
        dlb_ref[...] += _rowsum(dy)
        dfb = (0.5 * dr).astype(BF16)
        df_ref[...] = dfb
        acc = jnp.zeros((tm, D_MODEL), F32)
        dh_ahead = _nt(dfb, wo_ref[0:FF_CHUNK, :])
        for j in range(nf):
            c0 = j * FF_CHUNK
            dh = dh_ahead
            if j + 1 < nf:
                dh_ahead = _nt(dfb, wo_ref[c0 + FF_CHUNK:c0 + 2 * FF_CHUNK, :])
            gg = g_ref[:, c0:c0 + FF_CHUNK].astype(F32)
            uu = u_ref[:, c0:c0 + FF_CHUNK].astype(F32)
            s = _sigmoid(gg)
            dgb = (dh * uu * s * (1.0 + gg * (1.0 - s))).astype(BF16)
            dub = (dh * gg * s).astype(BF16)
            dgu_ref[:, c0:c0 + FF_CHUNK] = dgb
            dgu_ref[:, D_FF + c0:D_FF + c0 + FF_CHUNK] = dub
            acc = acc + _nt(dgb, wi_ref[:, c0:c0 + FF_CHUNK]) + _nt(dub, wi_ref[:, D_FF + c0:D_FF + c0 + FF_CHUNK])
        dx_ref[...] = ALPHA * dr + acc
        if ne:
            @pl.when(i == last)
            def _():
                _exchange_finish(comm_in, comm_out, *sems)

    row = lambda n: pl.BlockSpec((tm, n), lambda i: (i, 0))
    return pl.pallas_call(
        body, name=name, grid=(T // tm,),
        in_specs=[row(D_MODEL), row(D_MODEL), row(D_FF), row(D_FF), _resident((D_MODEL, 2 * D_FF)),
                  _resident((D_FF, D_MODEL)), _resident((1, D_MODEL))] + [ANY_SPEC] * ne,
        out_specs=[row(D_MODEL), row(2 * D_FF), row(D_MODEL), _resident((1, D_MODEL)), _resident((1, D_MODEL))]
        + [ANY_SPEC] * ne,
        out_shape=[jax.ShapeDtypeStruct((T, D_MODEL), F32), jax.ShapeDtypeStruct((T, 2 * D_FF), BF16),
                   jax.ShapeDtypeStruct((T, D_MODEL), BF16), jax.ShapeDtypeStruct((1, D_MODEL), F32),
                   jax.ShapeDtypeStruct((1, D_MODEL), F32)] + _exchange_shapes(exchange),
        scratch_shapes=_exchange_sems(ne) if ne else [],
        compiler_params=_cp(1),
    )(dxo, r, g, u, w_in, w_out, lg, *exchange)


def _matmul_tn(a, b, name, by_chip=False, exchange=()):
    T, K = a.shape
    N = b.shape[1]
    tt = _tile(T, 1024)
    tn = N // N_CHIPS if by_chip else N
    while K * tn * 4 > (6 << 20) and tn % 256 == 0 and not by_chip:
        tn //= 2
    assert N % tn == 0
    ne = len(exchange)
    grid = (N // tn, T // tt)

    def body(a_ref, b_ref, *rest):
        comm_in, o_ref, comm_out, sems = rest[:ne], rest[ne], rest[ne + 1:2 * ne + 1], rest[2 * ne + 1:]
        n, t = pl.program_id(0), pl.program_id(1)
        if ne:
            @pl.when((n == 0) & (t == 0))
            def _():
                _exchange_start(comm_in, comm_out, *sems)

        @pl.when(t == 0)
        def _():
            o_ref[...] = jnp.zeros_like(o_ref)

        o_ref[...] += _tn(a_ref[...].astype(BF16), b_ref[...].astype(BF16))
        if ne:
            @pl.when((n == grid[0] - 1) & (t == grid[1] - 1))
            def _():
                _exchange_finish(comm_in, comm_out, *sems)

    res = pl.pallas_call(
        body, name=name, grid=grid,
        in_specs=[pl.BlockSpec((tt, K), lambda n, t: (t, 0)), pl.BlockSpec((tt, tn), lambda n, t: (t, n))]
        + [ANY_SPEC] * ne,
        out_specs=[pl.BlockSpec((None, K, tn), lambda n, t: (n, 0, 0)) if by_chip
                   else pl.BlockSpec((K, tn), lambda n, t: (0, n))] + [ANY_SPEC] * ne,
        out_shape=[jax.ShapeDtypeStruct((N_CHIPS, K, tn) if by_chip else (K, N), F32)] + _exchange_shapes(exchange),
        scratch_shapes=_exchange_sems(ne) if ne else [],
        compiler_params=_cp(2),
    )(a, b, *exchange)
    return res if ne else res[0]


def _matmul_tokens(at, b, name):
    M, T = at.shape
    N = b.shape[1]
    tt = _tile(T, 1024)

    def body(a_ref, b_ref, o_ref):
        @pl.when(pl.program_id(0) == 0)
        def _():
            o_ref[...] = jnp.zeros_like(o_ref)

        o_ref[...] += _nn(a_ref[...].astype(BF16), b_ref[...].astype(BF16))

    return pl.pallas_call(
        body, name=name, grid=(T // tt,),
        in_specs=[pl.BlockSpec((M, tt), lambda t: (0, t)), pl.BlockSpec((tt, N), lambda t: (t, 0))],
        out_specs=pl.BlockSpec((M, N), lambda t: (0, 0)),
        out_shape=jax.ShapeDtypeStruct((M, N), F32),
        compiler_params=_cp(1),
    )(at, b)


def _matmul_nn(x, w, scale, out_dtype, name, also_transposed=False):
    T, K = x.shape
    N = w.shape[1]
    tm = _tile(T, 512)

    def body(x_ref, w_ref, s_ref, o_ref, *ot_ref):
        res = _nn(x_ref[...].astype(BF16), w_ref[...]) * s_ref[...]
        o_ref[...] = res.astype(out_dtype)
        if also_transposed:
            ot_ref[0][...] = res.T.astype(out_dtype)

    res = pl.pallas_call(
        body, name=name, grid=(T // tm,),
        in_specs=[pl.BlockSpec((tm, K), lambda i: (i, 0)), _resident((K, N)), _resident((1, N))],
        out_specs=[pl.BlockSpec((tm, N), lambda i: (i, 0))] + [pl.BlockSpec((N, tm), lambda i: (0, i))] * also_transposed,
        out_shape=[jax.ShapeDtypeStruct((T, N), out_dtype)] + [jax.ShapeDtypeStruct((N, T), out_dtype)] * also_transposed,
        compiler_params=_cp(1),
    )(x, w, scale)
    return res if also_transposed else res[0]


def _log_sigmoid(z):
    return jnp.minimum(z, 0.0) - jnp.log1p(jnp.exp(-jnp.abs(z)))


def _tri(n, lower):
    r = lax.broadcasted_iota(jnp.int32, (n, n), 0)
    c = lax.broadcasted_iota(jnp.int32, (n, n), 1)
    return jnp.where((c <= r) if lower else (c >= r), 1.0, 0.0).astype(F32)


def _f32dot(a, b):
    return jnp.dot(a, b, preferred_element_type=F32, precision=lax.Precision.HIGHEST)


def _forget_cumsum(flog, col, bf):
    T = flog.shape[0]
    bt = _tile(T, 512)

    def body(f_ref, b_ref, c_ref, carry):
        @pl.when(pl.program_id(0) == 0)
        def _():
            carry[...] = jnp.zeros_like(carry)

        lf = _log_sigmoid(f_ref[...] + b_ref[...])
        c = _f32dot(_tri(bt, True), lf) + carry[...]
        c_ref[...] = c * LOG2E
        carry[...] = c[bt - 1:bt, :]

    return pl.pallas_call(
        body, name="forget_cumsum", grid=(T // bt,),
        in_specs=[pl.BlockSpec((bt, N_FLOG), lambda i: (i, col)), _resident((1, N_FLOG))],
        out_specs=pl.BlockSpec((bt, N_FLOG), lambda i: (i, 0)),
        out_shape=jax.ShapeDtypeStruct((T, N_FLOG), F32),
        scratch_shapes=[pltpu.VMEM((1, N_FLOG), F32)],
        compiler_params=_cp(1),
    )(flog, bf)


def _forget_bwd(dck, dcq, flog, col, bf):
    T = dcq.shape[0]
    bt = _tile(T, 512)
    nb = T // bt

    def body(k0_ref, k1_ref, k2_ref, k3_ref, dcq_ref, f_ref, b_ref, dz_ref, db_ref, carry):
        @pl.when(pl.program_id(0) == 0)
        def _():
            carry[...] = jnp.zeros_like(carry)
            db_ref[...] = jnp.zeros_like(db_ref)

        dc = ((k0_ref[...] + k1_ref[...]) + (k2_ref[...] + k3_ref[...])) + dcq_ref[...]
        dlf = _f32dot(_tri(bt, False), dc) + carry[...]
        carry[...] = dlf[0:1, :]
        z = f_ref[...] + b_ref[...]
        dz = dlf * _sigmoid(-z)
        dz_ref[...] = dz.astype(BF16)
        db_ref[...] += _rowsum(dz)

    slab = lambda j: pl.BlockSpec((None, bt, N_FLOG), lambda i: (j, nb - 1 - i, 0))
    return pl.pallas_call(
        body, name="forget_bwd", grid=(nb,),
        in_specs=[slab(0), slab(1), slab(2), slab(3),
                  pl.BlockSpec((bt, N_FLOG), lambda i: (nb - 1 - i, 0)),
                  pl.BlockSpec((bt, N_FLOG), lambda i: (nb - 1 - i, col)), _resident((1, N_FLOG))],
        out_specs=[pl.BlockSpec((bt, N_FLOG), lambda i: (nb - 1 - i, 0)), _resident((1, N_FLOG))],
        out_shape=[jax.ShapeDtypeStruct((T, N_FLOG), BF16), jax.ShapeDtypeStruct((1, N_FLOG), F32)],
        scratch_shapes=[pltpu.VMEM((1, N_FLOG), F32)],
        compiler_params=_cp(1),
    )(dck, dck, dck, dck, dcq, flog, bf)


def _head_masks():
    lane = lax.broadcasted_iota(jnp.int32, (1, LANE), 1)
    return lane < HEAD_DIM


def _split_heads(x2, is_a):
    zero = jnp.zeros_like(x2)
    return jnp.where(is_a, x2, zero), jnp.where(is_a, zero, x2)


BIAS_PARTS = 3
ATTN_GROUP = 4
ATTN_FWD_GROUP = 8


def _bias_lanes(h):
    lane = lax.broadcasted_iota(jnp.int32, (1, LANE), 1)
    first = (1 - h) * HEAD_DIM
    return lane, first


def _fold_key_bias(qkv, c):
    T = qkv.shape[0]
    tm = _tile(T, 512)
    npair = N_HEADS // 2

    def body(k_ref, c_ref, o_ref):
        cc = c_ref[...]
        parts, rest = [], cc
        for _ in range(BIAS_PARTS):
            piece = rest.astype(BF16)
            parts.append(piece)
            rest = rest - piece.astype(F32)
        for j in range(npair):
            k2 = k_ref[:, j * LANE:(j + 1) * LANE]
            for h in range(2):
                lane, first = _bias_lanes(h)
                out = k2
                for n, piece in enumerate(parts):
                    col = piece[:, 2 * j + h:2 * j + h + 1]
                    out = jnp.where(lane == first + n, col, out)
                o_ref[:, (2 * j + h) * LANE:(2 * j + h + 1) * LANE] = out

    return pl.pallas_call(
        body, name="fold_key_bias", grid=(T // tm,),
        in_specs=[pl.BlockSpec((tm, D_ATTN), lambda i: (i, 1)), pl.BlockSpec((tm, N_FLOG), lambda i: (i, 0))],
        out_specs=pl.BlockSpec((tm, 2 * D_ATTN), lambda i: (i, 0)),
        out_shape=jax.ShapeDtypeStruct((T, 2 * D_ATTN), BF16),
        compiler_params=_cp(1),
    )(qkv, c)


def _attn_fwd(qkv, vt, kb):
    T = qkv.shape[0]
    tq = _tile(T, 512)
    tk = tq
    nq = T // tq
    npair = N_HEADS // 2

    def body(q_ref, ka_ref, kb_ref, vt_ref, o_ref, al_ref, m_s, l_s, acc_s):
        i = pl.program_id(1)
        qs = []
        for h, qh in enumerate(_split_heads(q_ref[...], _head_masks())):
            lane, first = _bias_lanes(h)
            qs.append(jnp.where((lane >= first) & (lane < first + BIAS_PARTS), -1.0, qh).astype(BF16))
        k_refs = (ka_ref, kb_ref)
        m_s[...] = jnp.full_like(m_s, NEG_INF)
        l_s[...] = jnp.zeros_like(l_s)
        acc_s[...] = jnp.zeros_like(acc_s)

        def scores_at(kk):
            k0 = pl.multiple_of(kk * tk, tk)
            return tuple(_nt(k_refs[h][pl.ds(k0, tk), :], qs[h]) for h in range(2))

        def consume(kk, scores, masked):
            k0 = pl.multiple_of(kk * tk, tk)
            v2t = vt_ref[:, pl.ds(k0, tk)]
            for h in range(2):
                zt = scores[h]
                if masked:
                    rr = lax.broadcasted_iota(jnp.int32, (tk, tq), 0)
                    cc = lax.broadcasted_iota(jnp.int32, (tk, tq), 1)
                    zt = jnp.where(cc >= rr, zt, NEG_INF)
                m_old = m_s[h]
                m_new = jnp.maximum(m_old, jnp.max(zt, axis=0, keepdims=True))
                p = jnp.exp2(zt - m_new)
                a = jnp.exp2(m_old - m_new)
                l_s[h] = a * l_s[h] + jnp.sum(p, axis=0, keepdims=True)
                acc_s[h] = a * acc_s[h] + _nn(v2t, p.astype(BF16))
                m_s[h] = m_new

        def group(kk, n, last_masked):
            scores = [scores_at(kk + u) for u in range(n)]
            for u in range(n):
                consume(kk + u, scores[u], last_masked and u == n - 1)

        def loop_body(t, carry):
            group(ATTN_FWD_GROUP * t, ATTN_FWD_GROUP, False)
            return carry

        lax.fori_loop(0, i // ATTN_FWD_GROUP, loop_body, 0)
        for left in range(ATTN_FWD_GROUP):
            @pl.when(i % ATTN_FWD_GROUP == left)
            def _():
                group(i - left, left + 1, True)

        outs = []
        for h in range(2):
            l = l_s[h]
            outs.append(acc_s[h] * (1.0 / l))
            al_ref[0, h:h + 1, :] = -(m_s[h] + jnp.log2(l))
        al_ref[0, 2:8, :] = jnp.zeros((6, tq), F32)
        dim = lax.broadcasted_iota(jnp.int32, (LANE, 1), 0)
        o_ref[...] = jnp.where(dim < HEAD_DIM, outs[0], outs[1]).T

    rowl = pl.BlockSpec((1, 8, tq), lambda j, i: (j, 0, i))
    return pl.pallas_call(
        body, name="attn_fwd", grid=(npair, nq),
        in_specs=[pl.BlockSpec((tq, LANE), lambda j, i: (i, j)),
                  pl.BlockSpec((T, LANE), lambda j, i: (0, 2 * j), pipeline_mode=pl.Buffered(1)),
                  pl.BlockSpec((T, LANE), lambda j, i: (0, 2 * j + 1), pipeline_mode=pl.Buffered(1)),
                  pl.BlockSpec((LANE, T), lambda j, i: (2 * npair + j, 0), pipeline_mode=pl.Buffered(1))],
        out_specs=[pl.BlockSpec((tq, LANE), lambda j, i: (i, j)), rowl],
        out_shape=[jax.ShapeDtypeStruct((T, D_ATTN), F32), jax.ShapeDtypeStruct((npair, 8, T), F32)],
        scratch_shapes=[pltpu.VMEM((2, 1, tq), F32), pltpu.VMEM((2, 1, tq), F32), pltpu.VMEM((2, LANE, tq), F32)],
        compiler_params=_cp(2),
    )(qkv, kb, kb, vt)


def _attn_bwd(qkv, qkvt, dob, dobt, cb, alrow, dlrow):
    T = qkv.shape[0]
    tq = _tile(T, 512)
    tk = tq
    nq = T // tq
    npair = N_HEADS // 2

    def body(q_ref, qt_ref, k_ref, kt_ref, v_ref, do_ref, dot_ref, cb_ref, al_ref, dl_ref,
             dq_ref, dk_ref, dv_ref, dc_ref, dcq_ref, dk_s, dv_s, dc_s):
        kj = pl.program_id(1)
        is_a = _head_masks()
        ks = _split_heads(k_ref[...], is_a)
        vs = _split_heads(v_ref[...], is_a)
        dim_a = lax.broadcasted_iota(jnp.int32, (LANE, 1), 0) < HEAD_DIM
        kts = _split_heads(kt_ref[...], dim_a)
        head_lane = lax.broadcasted_iota(jnp.int32, (1, LANE), 1) - 2 * pl.program_id(0)
        cs = tuple(jnp.sum(jnp.where(head_lane == h, cb_ref[...], 0.0), axis=-1, keepdims=True) for h in range(2))

        @pl.when(kj == 0)
        def _():
            dq_ref[...] = jnp.zeros_like(dq_ref)
            dcq_ref[...] = jnp.zeros_like(dcq_ref)

        dk_s[...] = jnp.zeros_like(dk_s)
        dv_s[...] = jnp.zeros_like(dv_s)
        dc_s[...] = jnp.zeros_like(dc_s)

        def step(qi, masked):
            q0 = pl.multiple_of(qi * tq, tq)
            q2 = q_ref[pl.ds(q0, tq), :]
            do2 = do_ref[pl.ds(q0, tq), :]
            qt2 = qt_ref[:, pl.ds(q0, tq)]
            dot2 = dot_ref[:, pl.ds(q0, tq)]
            for h in range(2):
                alr = al_ref[0, h:h + 1, pl.ds(q0, tq)]
                dlr = dl_ref[0, h:h + 1, pl.ds(q0, tq)]
                zt = _nt(ks[h], q2) + (alr - cs[h])
                if masked:
                    rr = lax.broadcasted_iota(jnp.int32, (tk, tq), 0)
                    cc = lax.broadcasted_iota(jnp.int32, (tk, tq), 1)
                    zt = jnp.where(cc >= rr, zt, NEG_INF)
                pt = jnp.exp2(zt)
                dst = pt * (_nt(vs[h], do2) - dlr)
                pb = pt.astype(BF16)
                dsb = dst.astype(BF16)
                dv_s[h] += _nt(dot2, pb)
                dk_s[h] += _nt(qt2, dsb)
                dc_s[h] += jnp.sum(dst, axis=-1, keepdims=True)
                dcq_ref[0, h:h + 1, pl.ds(q0, tq)] += jnp.sum(dst, axis=0, keepdims=True)
                dq_ref[:, pl.ds(q0, tq)] += _nn(kts[h], dsb)

        rest = nq - 1 - kj
        for left in range(ATTN_GROUP):
            @pl.when(rest % ATTN_GROUP == left)
            def _():
                for u in range(left + 1):
                    step(kj + u, u == 0)

        def loop_body(t, carry):
            for u in range(ATTN_GROUP):
                step(kj + 1 + rest % ATTN_GROUP + ATTN_GROUP * t + u, False)
            return carry

        lax.fori_loop(0, rest // ATTN_GROUP, loop_body, 0)
        dk_ref[...] = (jnp.where(dim_a, dk_s[0], dk_s[1]) * (1.0 / LOG2E)).astype(BF16)
        dv_ref[...] = jnp.where(dim_a, dv_s[0], dv_s[1]).astype(BF16)
        lane = lax.broadcasted_iota(jnp.int32, (1, LANE), 1)
        head = 2 * pl.program_id(0)
        dc_ref[...] = jnp.where(lane == head, -dc_s[0], jnp.where(lane == head + 1, -dc_s[1], 0.0))

        @pl.when(kj == nq - 1)
        def _():
            dq_ref[...] = dq_ref[...] * Q_SCALE

    full = lambda col: pl.BlockSpec((T, LANE), lambda j, kj: (0, col(j)), pipeline_mode=pl.Buffered(1))
    fullt = lambda row: pl.BlockSpec((LANE, T), lambda j, kj: (row(j), 0), pipeline_mode=pl.Buffered(1))
    tile = lambda col: pl.BlockSpec((tk, LANE), lambda j, kj: (kj, col(j)))
    tilet = lambda row: pl.BlockSpec((LANE, tk), lambda j, kj: (row(j), kj))
    rowl = pl.BlockSpec((1, 8, T), lambda j, kj: (j, 0, 0))
    return pl.pallas_call(
        body, name="attn_bwd", grid=(npair, nq),
        in_specs=[full(lambda j: j), fullt(lambda j: j), tile(lambda j: npair + j), tilet(lambda j: npair + j),
                  tile(lambda j: 2 * npair + j), full(lambda j: j), fullt(lambda j: j), tile(lambda j: 0), rowl, rowl],
        out_specs=[pl.BlockSpec((LANE, T), lambda j, kj: (j, 0)), tilet(lambda j: j), tilet(lambda j: j),
                   pl.BlockSpec((None, tk, LANE), lambda j, kj: (j, kj, 0)), rowl],
        out_shape=[jax.ShapeDtypeStruct((D_ATTN, T), F32), jax.ShapeDtypeStruct((D_ATTN, T), BF16),
                   jax.ShapeDtypeStruct((D_ATTN, T), BF16), jax.ShapeDtypeStruct((npair, T, LANE), F32),
                   jax.ShapeDtypeStruct((npair, 8, T), F32)],
        scratch_shapes=[pltpu.VMEM((2, LANE, tk), F32), pltpu.VMEM((2, LANE, tk), F32), pltpu.VMEM((2, tk, 1), F32)],
        compiler_params=_cp(2),
    )(qkv, qkvt, qkv, qkvt, qkv, dob, dobt, cb, alrow, dlrow)


HALO = 8


def _shift_rows(cur, other, k, tm, down):
    row = lax.broadcasted_iota(jnp.int32, (tm, 1), 0)
    reps = tm // HALO
    if down:
        rolled = pltpu.roll(cur, k, 0)
        fill = jnp.tile(pltpu.roll(other, k, 0), (reps, 1))
        return jnp.where(row < k, fill, rolled)
    rolled = pltpu.roll(cur, tm - k, 0)
    fill = jnp.tile(pltpu.roll(other, HALO - k, 0), (reps, 1))
    return jnp.where(row >= tm - k, fill, rolled)


def _conv_fwd(c, hh, c_prev, hh_prev, w_ref, first, tm):
    u = c * hh
    u_prev = jnp.where(first, 0.0, c_prev * hh_prev)
    u1 = _shift_rows(u, u_prev, 1, tm, True)
    u2 = _shift_rows(u, u_prev, 2, tm, True)
    y = w_ref[0:1, :] * u2 + w_ref[1:2, :] * u1 + w_ref[2:3, :] * u
    return u, u1, u2, y


def _rms(x, g):
    rs = lax.rsqrt(jnp.mean(x * x, axis=-1, keepdims=True) + RMS_EPS)
    return x * rs * g, rs


def _mixer_tail_fwd(o, bchf, conv_w, g_attn, g_conv, w_mo, x1, lg, lb):
    T = o.shape[0]
    tm = _tile(T, 512)
    hb = tm // HALO

    def body(o_ref, b_ref, c_ref, h_ref, cp_ref, hp_ref, w_ref, ga_ref, gc_ref, wmo_ref, x1_ref, lg_ref, lb_ref,
             x2_ref, r2_ref, mg_ref):
        first = pl.program_id(0) == 0
        _, _, _, y = _conv_fwd(c_ref[...], h_ref[...], cp_ref[...], hp_ref[...], w_ref, first, tm)
        na, _ = _rms(o_ref[...], ga_ref[...])
        nc, _ = _rms(b_ref[...] * y, gc_ref[...])
        nab = na.astype(BF16)
        ncb = nc.astype(BF16)
        mg_ref[:, 0:D_ATTN] = nab
        mg_ref[:, D_ATTN:] = ncb
        r2 = ALPHA * x1_ref[...] + _nn(nab, wmo_ref[0:D_ATTN, :]) + _nn(ncb, wmo_ref[D_ATTN:, :])
        r2_ref[...] = r2
        xhat, _ = _ln_stats(r2)
        x2_ref[...] = xhat * lg_ref[...] + lb_ref[...]

    row = lambda n, col=0: pl.BlockSpec((tm, n), lambda i: (i, col))
    prev = lambda col: pl.BlockSpec((HALO, D_CONV), lambda i: (jnp.maximum(i * hb - 1, 0), col))
    return pl.pallas_call(
        body, name="mixer_tail_fwd", grid=(T // tm,),
        in_specs=[row(D_ATTN), row(D_CONV, 0), row(D_CONV, 1), row(D_CONV, 2), prev(1), prev(2),
                  _resident((3, D_CONV)), _resident((1, D_ATTN)), _resident((1, D_CONV)),
                  _resident((D_MODEL, D_MODEL)), row(D_MODEL), _resident((1, D_MODEL)), _resident((1, D_MODEL))],
        out_specs=[row(D_MODEL), row(D_MODEL), row(D_MODEL)],
        out_shape=[jax.ShapeDtypeStruct((T, D_MODEL), F32), jax.ShapeDtypeStruct((T, D_MODEL), F32),
                   jax.ShapeDtypeStruct((T, D_MODEL), BF16)],
        compiler_params=_cp(1),
    )(o, bchf, bchf, bchf, bchf, bchf, conv_w, g_attn, g_conv, w_mo, x1, lg, lb)


def _head_sum_rows():
    row = lax.broadcasted_iota(jnp.int32, (4 * 8, D_ATTN), 0)
    head = lax.broadcasted_iota(jnp.int32, (4 * 8, D_ATTN), 1) // HEAD_DIM
    return jnp.where((row % 8 < 2) & (2 * (row // 8) + row % 8 == head), 1.0, 0.0).astype(F32)


def _mixer_tail_bwd(dx2, r2, lg, w_mo, o, bchf, conv_w, g_attn, g_conv, swap=()):
    T = o.shape[0]
    tm = _tile(T, 512)
    hb = tm // HALO
    ns = len(swap)
    last = T // tm - 1

    def body(dx2_ref, r2_ref, lg_ref, wmo_ref, o_ref, b_ref, c_ref, h_ref, cp_ref, hp_ref, w_ref, ga_ref, gc_ref,
             *rest):
        comm_in = rest[:ns]
        dx1_ref, dr_ref, do_ref, dot_ref, dl_ref, dco_ref, dlg_ref, dlb_ref, dga_ref, dgc_ref = rest[ns:ns + 10]
        comm_out, sems = rest[ns + 10:2 * ns + 10], rest[2 * ns + 10:]
        i = pl.program_id(0)
        if ns:
            @pl.when(i == 0)
            def _():
                _swap_start(comm_in, comm_out, *sems)

            @pl.when(i == last)
            def _():
                _swap_finish(comm_in, comm_out, *sems)

        @pl.when(i == 0)
        def _():
            for ref in (dlg_ref, dlb_ref, dga_ref, dgc_ref):
                ref[...] = jnp.zeros_like(ref)

        dy = dx2_ref[...]
        xhat, rstd = _ln_stats(r2_ref[...])
        dr = _ln_bwd(dy, xhat, rstd, lg_ref[...])
        dlg_ref[...] += _rowsum(dy * xhat)
        dlb_ref[...] += _rowsum(dy)
        dx1_ref[...] = ALPHA * dr
        drb = dr.astype(BF16)
        dr_ref[...] = drb
        dna = _nt(drb, wmo_ref[0:D_ATTN, :])
        dnc = _nt(drb, wmo_ref[D_ATTN:, :])

        def rms_bwd(x, g, dn):
            rs = lax.rsqrt(jnp.mean(x * x, axis=-1, keepdims=True) + RMS_EPS)
            dng = dn * g
            dx = rs * dng - x * (rs * rs * rs) * jnp.mean(dng * x, axis=-1, keepdims=True)
            return dx, _rowsum(dn * x * rs)

        oo = o_ref[...]
        do, dga = rms_bwd(oo, ga_ref[...], dna)
        dga_ref[...] += dga
        do_ref[...] = do.astype(BF16)
        dot_ref[...] = do.T.astype(BF16)
        dl_ref[...] = lax.dot_general(_head_sum_rows(), do * oo, (((1,), (1,)), ((), ())),
                                      preferred_element_type=F32, precision=lax.Precision.HIGHEST)
        _, _, _, y = _conv_fwd(c_ref[...], h_ref[...], cp_ref[...], hp_ref[...], w_ref, i == 0, tm)
        dco, dgc = rms_bwd(b_ref[...] * y, gc_ref[...], dnc)
        dgc_ref[...] += dgc
        dco_ref[...] = dco

    row = lambda n, col=0: pl.BlockSpec((tm, n), lambda i: (i, col))
    prev = lambda col: pl.BlockSpec((HALO, D_CONV), lambda i: (jnp.maximum(i * hb - 1, 0), col))
    vec = lambda n: _resident((1, n))
    return pl.pallas_call(
        body, name="mixer_tail_bwd", grid=(T // tm,),
        in_specs=[row(D_MODEL), row(D_MODEL), vec(D_MODEL), _resident((D_MODEL, D_MODEL)), row(D_ATTN),
                  row(D_CONV, 0), row(D_CONV, 1), row(D_CONV, 2), prev(1), prev(2), _resident((3, D_CONV)),
                  vec(D_ATTN), vec(D_CONV)] + [ANY_SPEC] * ns,
        out_specs=[row(D_MODEL), row(D_MODEL), row(D_ATTN), pl.BlockSpec((D_ATTN, tm), lambda i: (0, i)),
                   pl.BlockSpec((4 * 8, tm), lambda i: (0, i)), row(D_CONV),
                   vec(D_MODEL), vec(D_MODEL), vec(D_ATTN), vec(D_CONV)] + [ANY_SPEC] * ns,
        out_shape=[jax.ShapeDtypeStruct((T, D_MODEL), F32), jax.ShapeDtypeStruct((T, D_MODEL), BF16),
                   jax.ShapeDtypeStruct((T, D_ATTN), BF16), jax.ShapeDtypeStruct((D_ATTN, T), BF16),
                   jax.ShapeDtypeStruct((4 * 8, T), F32),
                   jax.ShapeDtypeStruct((T, D_CONV), F32), jax.ShapeDtypeStruct((1, D_MODEL), F32),
                   jax.ShapeDtypeStruct((1, D_MODEL), F32), jax.ShapeDtypeStruct((1, D_ATTN), F32),
                   jax.ShapeDtypeStruct((1, D_CONV), F32)] + _swap_shapes(swap),
        scratch_shapes=_swap_sems(ns) if ns else [],
        compiler_params=_cp(1),
    )(dx2, r2, lg, w_mo, o, bchf, bchf, bchf, bchf, bchf, conv_w, g_attn, g_conv, *swap)


def _conv_bwd(dco, bchf, conv_w):
    T = dco.shape[0]
    tm = _tile(T, 512)
    hb = tm // HALO
    nt = T // tm

    def body(dco_ref, dcon_ref, b_ref, bn_ref, c_ref, h_ref, cp_ref, hp_ref, w_ref, dbch_ref, dw_ref):
        i = pl.program_id(0)

        @pl.when(i == 0)
        def _():
            dw_ref[...] = jnp.zeros_like(dw_ref)

        cc = c_ref[...]
        hh = h_ref[...]
        u, u1, u2, y = _conv_fwd(cc, hh, cp_ref[...], hp_ref[...], w_ref, i == 0, tm)
        dco = dco_ref[...]
        bb = b_ref[...]
        dyc = dco * bb
        dy_next = jnp.where(i == nt - 1, 0.0, dcon_ref[...] * bn_ref[...])
        d1 = _shift_rows(dyc, dy_next, 1, tm, False)
        d2 = _shift_rows(dyc, dy_next, 2, tm, False)
        du = w_ref[2:3, :] * dyc + w_ref[1:2, :] * d1 + w_ref[0:1, :] * d2
        dbch_ref[:, 0:D_CONV] = (dco * y).astype(BF16)
        dbch_ref[:, D_CONV:2 * D_CONV] = (du * hh).astype(BF16)
        dbch_ref[:, 2 * D_CONV:] = (du * cc).astype(BF16)
        dw_ref[0:1, :] += _rowsum(dyc * u2)
        dw_ref[1:2, :] += _rowsum(dyc * u1)
        dw_ref[2:3, :] += _rowsum(dyc * u)

    row = lambda n, col=0: pl.BlockSpec((tm, n), lambda i: (i, col))
    prev = lambda col: pl.BlockSpec((HALO, D_CONV), lambda i: (jnp.maximum(i * hb - 1, 0), col))
    nxt = lambda col: pl.BlockSpec((HALO, D_CONV), lambda i: (jnp.minimum((i + 1) * hb, T // HALO - 1), col))
    return pl.pallas_call(
        body, name="conv_bwd", grid=(nt,),
        in_specs=[row(D_CONV), nxt(0), row(D_CONV, 0), nxt(0), row(D_CONV, 1), row(D_CONV, 2), prev(1), prev(2),
                  _resident((3, D_CONV))],
        out_specs=[row(3 * D_CONV), _resident((8, D_CONV))],
        out_shape=[jax.ShapeDtypeStruct((T, 3 * D_CONV), BF16), jax.ShapeDtypeStruct((8, D_CONV), F32)],
        compiler_params=_cp(1),
    )(dco, dco, bchf, bchf, bchf, bchf, bchf, bchf, conv_w)


def _mixer_in_bwd(dx1a, dqt, dkt, dvt, dbch, dfl, w_qkvt, w_bch, w_f):
    T = dx1a.shape[0]
    tm = _tile(T, 512)

    def body(a_ref, dq_ref, dk_ref, dv_ref, db_ref, df_ref, wq_ref, wb_ref, wf_ref, o_ref):
        acc = a_ref[...] + _nt(db_ref[...], wb_ref[...]) + _nt(df_ref[...], wf_ref[...])
        for n, ref in enumerate((dq_ref, dk_ref, dv_ref)):
            acc = acc + _tn(ref[...].astype(BF16), wq_ref[n * D_ATTN:(n + 1) * D_ATTN, :])
        o_ref[...] = acc

    row = lambda n: pl.BlockSpec((tm, n), lambda i: (i, 0))
    col = pl.BlockSpec((D_ATTN, tm), lambda i: (0, i))
    return pl.pallas_call(
        body, name="mixer_in_bwd", grid=(T // tm,),
        in_specs=[row(D_MODEL), col, col, col, row(3 * D_CONV), row(N_FLOG),
                  _resident((3 * D_ATTN, D_MODEL)), _resident((D_MODEL, 3 * D_CONV)), _resident((D_MODEL, N_FLOG))],
        out_specs=row(D_MODEL),
        out_shape=jax.ShapeDtypeStruct((T, D_MODEL), F32),
        compiler_params=_cp(1),
    )(dx1a, dqt, dkt, dvt, dbch, dfl, w_qkvt, w_bch, w_f)


def _ple_loss(x3, p, tgt, w_g, w_p, b_g, lg, lb):
    T = x3.shape[0]
    tm = _tile(T, 512)

    def body(x_ref, p_ref, t_ref, wg_ref, wp_ref, bg_ref, lg_ref, lb_ref,
             dx_ref, de_ref, dz_ref, loss_ref, dlg_ref, dlb_ref, dbg_ref):
        @pl.when(pl.program_id(0) == 0)
        def _():
            for ref in (loss_ref, dlg_ref, dlb_ref, dbg_ref):
                ref[...] = jnp.zeros_like(ref)

        xf = x_ref[...]
        gate = _sigmoid(_nn(xf.astype(BF16), wg_ref[...]) + bg_ref[...])
        e = _nn(p_ref[...].astype(BF16), wp_ref[...])
        xhat, rstd = _ln_stats(ALPHA * xf + gate * e)
        err = xhat * lg_ref[...] + lb_ref[...] - t_ref[...]
        sq = jnp.sum(_rowsum(err * err), axis=-1, keepdims=True)
        loss_ref[...] += jnp.broadcast_to(sq * (0.5 / D_MODEL), loss_ref.shape)
        dy = err * (1.0 / D_MODEL)
        dr = _ln_bwd(dy, xhat, rstd, lg_ref[...])
        dlg_ref[...] += _rowsum(dy * xhat)
        dlb_ref[...] += _rowsum(dy)
        de_ref[...] = (dr * gate).astype(BF16)
        dz = dr * e * gate * (1.0 - gate)
        dbg_ref[...] += _rowsum(dz)
        dzb = dz.astype(BF16)
        dz_ref[...] = dzb
        dx_ref[...] = ALPHA * dr + _nt(dzb, wg_ref[...])

    row = lambda n: pl.BlockSpec((tm, n), lambda i: (i, 0))
    vec = lambda n: _resident((1, n))
    return pl.pallas_call(
        body, name="ple_loss", grid=(T // tm,),
        in_specs=[row(D_MODEL), row(PLE_DIM), row(D_MODEL), _resident((D_MODEL, D_MODEL)),
                  _resident((PLE_DIM, D_MODEL)), vec(D_MODEL), vec(D_MODEL), vec(D_MODEL)],
        out_specs=[row(D_MODEL), row(D_MODEL), row(D_MODEL), vec(LANE), vec(D_MODEL), vec(D_MODEL), vec(D_MODEL)],
        out_shape=[jax.ShapeDtypeStruct((T, D_MODEL), F32), jax.ShapeDtypeStruct((T, D_MODEL), BF16),
                   jax.ShapeDtypeStruct((T, D_MODEL), BF16), jax.ShapeDtypeStruct((1, LANE), F32),
                   jax.ShapeDtypeStruct((1, D_MODEL), F32), jax.ShapeDtypeStruct((1, D_MODEL), F32),
                   jax.ShapeDtypeStruct((1, D_MODEL), F32)],
        compiler_params=_cp(1),
    )(x3, p, tgt, w_g, w_p, b_g, lg, lb)


def _lane_layout(v8):
    return jnp.repeat(v8, HEAD_DIM, axis=1)


def _row_layout(v8):
    t = v8.shape[0]
    return jnp.pad(v8.T.reshape(N_HEADS // 2, 2, t), ((0, 0), (0, 6), (0, 0)))


def _from_lane_layout(vl):
    return vl[:, ::HEAD_DIM]


def _from_row_layout(vr):
    return vr[:, :2, :].reshape(N_HEADS, -1).T


def _local_step(x, p, tgt, w, overlap=None):
    bf = lambda a: a.astype(BF16)
    w1i, w1o = bf(w["ffn1_w_in"]), bf(w["ffn1_w_out"])
    first = _ffn_fwd(x, w1i, w1o, w["ln1_g"], w["ln1_b"], "ffn1_fwd", overlap["gather"] if overlap else ())
    x1, r1, g1, u1, h1 = first[:5]
    if overlap:
        w = {**w, **overlap["weights"](first[5:])}
    w2i, w2o = bf(w["ffn2_w_in"]), bf(w["ffn2_w_out"])
    wmi = w["w_mix_in"]
    o_f = 3 * D_ATTN
    o_b = o_f + N_HEADS
    w_qkv = bf(wmi[:, :o_f])
    w_f = bf(jnp.pad(wmi[:, o_f:o_b], ((0, 0), (0, N_FLOG - N_HEADS))))
    w_bch = bf(wmi[:, o_b:])
    w_bchf = jnp.concatenate([w_bch, w_f], axis=1)
    w_mo, w_g, w_p = bf(w["w_mix_out"]), bf(w["w_ple_gate"]), bf(w["w_ple"])
    b_f = jnp.pad(w["b_forget"], ((0, 0), (0, N_FLOG - N_HEADS)))

    q_scale = jnp.concatenate([jnp.full((1, D_ATTN), Q_SCALE * LOG2E, F32), jnp.ones((1, 2 * D_ATTN), F32)], axis=1)
    qkv, qkvt = _matmul_nn(x1, w_qkv, q_scale, BF16, "proj_qkv", also_transposed=True)
    bchf = _matmul_nn(x1, w_bchf, jnp.ones((1, 3 * D_CONV + N_FLOG), F32), F32, "proj_bchf")
    fcol = 3 * D_CONV // N_FLOG
    c = _forget_cumsum(bchf, fcol, b_f)
    o, alrow = _attn_fwd(qkv, qkvt, _fold_key_bias(qkv, c))
    x2, r2, merged = _mixer_tail_fwd(o, bchf, w["conv_w"], w["g_attn"], w["g_conv"], w_mo, x1, w["ln2_g"], w["ln2_b"])
    x3, r3, g2, u2, h2 = _ffn_fwd(x2, w2i, w2o, w["ln3_g"], w["ln3_b"], "ffn2_fwd")

    grads = {}
    dx3, de, dz, loss, grads["ln4_g"], grads["ln4_b"], grads["b_ple_gate"] = _ple_loss(
        x3, p, tgt, w_g, w_p, w["b_ple_gate"], w["ln4_g"], w["ln4_b"])
    by_chip = overlap is not None
    grads["w_ple"] = _matmul_tn(p, de, "dw_ple")
    grads["w_ple_gate"] = _matmul_tn(x3, dz, "dw_ple_gate")

    dx2, dgu2, df2, grads["ln3_g"], grads["ln3_b"] = _ffn_bwd(dx3, r3, g2, u2, w2i, w2o, w["ln3_g"], "ffn2_bwd")
    grads["ffn2_w_in"] = _matmul_tn(x2, dgu2, "dw_ffn2_in", by_chip)
    grads["ffn2_w_out"] = _matmul_tn(h2, df2, "dw_ffn2_out")

    to_swap = overlap["swap"](grads) if overlap else ()
    tail = _mixer_tail_bwd(dx2, r2, w["ln2_g"], w_mo, o, bchf, w["conv_w"], w["g_attn"], w["g_conv"], to_swap)
    (dx1a, dr2, dob, dobt, delta, dco, grads["ln2_g"], grads["ln2_b"], grads["g_attn"], grads["g_conv"]) = tail[:10]
    grads["w_mix_out"] = _matmul_tn(merged, dr2, "dw_mix_out")
    dbch, dcw = _conv_bwd(dco, bchf, w["conv_w"])
    grads["conv_w"] = dcw[:3]
    dqt, dkt, dvt, dck, dcq = _attn_bwd(qkv, qkvt, dob, dobt, c, alrow, delta.reshape(N_HEADS // 2, 8, -1))
    dcq_lanes = jnp.pad(_from_row_layout(dcq), ((0, 0), (0, N_FLOG - N_HEADS)))
    dfl, dbf = _forget_bwd(dck, dcq_lanes, bchf, fcol, b_f)
    grads["b_forget"] = dbf[:, :N_HEADS]
    dx1 = _mixer_in_bwd(dx1a, dqt, dkt, dvt, dbch, dfl, w_qkv.T, w_bch, w_f)
    grads["w_mix_in"] = jnp.concatenate(
        [_matmul_tokens(dqt, x1, "dw_q").T, _matmul_tokens(dkt, x1, "dw_k").T, _matmul_tokens(dvt, x1, "dw_v").T,
         _matmul_tn(x1, dfl, "dw_flog")[:, :N_HEADS], _matmul_tn(x1, dbch, "dw_bch")], axis=1)

    dx0, dgu1, df1, grads["ln1_g"], grads["ln1_b"] = _ffn_bwd(dx1, r1, g1, u1, w1i, w1o, w["ln1_g"], "ffn1_bwd")
    grads["ffn1_w_out"] = _matmul_tn(h1, df1, "dw_ffn1_out")
    if not overlap:
        grads["ffn1_w_in"] = _matmul_tn(x, dgu1, "dw_ffn1_in")
        return loss, dx0, grads
    sums = overlap["chip_sums"](grads, to_swap, tail[10:])
    grads["ffn1_w_in"], *received = _matmul_tn(x, dgu1, "dw_ffn1_in", by_chip, exchange=sums)
    return loss, dx0, grads, sums, received


WEIGHTS = ["ffn1_w_in", "ffn1_w_out", "ln1_g", "ln1_b", "w_mix_in", "b_forget", "conv_w", "g_attn", "g_conv",
           "w_mix_out", "ln2_g", "ln2_b", "ffn2_w_in", "ffn2_w_out", "ln3_g", "ln3_b", "w_ple", "w_ple_gate",
           "b_ple_gate", "ln4_g", "ln4_b"]
LAYOUT = {
    "ffn1_w_in": ((D_MODEL, 2 * D_FF), 1), "ffn1_w_out": ((D_FF, D_MODEL), 0),
    "w_mix_in": ((D_MODEL, 3 * D_ATTN + N_HEADS + 3 * D_CONV), 1), "conv_w": ((3, D_CONV), 1),
    "w_mix_out": ((D_MODEL, D_MODEL), 0), "ffn2_w_in": ((D_MODEL, 2 * D_FF), 1), "ffn2_w_out": ((D_FF, D_MODEL), 0),
    "w_ple": ((PLE_DIM, D_MODEL), 1), "w_ple_gate": ((D_MODEL, D_MODEL), 0),
    "ln1_g": ((1, D_MODEL), None), "ln1_b": ((1, D_MODEL), None), "b_forget": ((1, N_HEADS), None),
    "g_attn": ((1, D_ATTN), None), "g_conv": ((1, D_CONV), None), "ln2_g": ((1, D_MODEL), None),
    "ln2_b": ((1, D_MODEL), None), "ln3_g": ((1, D_MODEL), None), "ln3_b": ((1, D_MODEL), None),
    "b_ple_gate": ((1, D_MODEL), None), "ln4_g": ((1, D_MODEL), None), "ln4_b": ((1, D_MODEL), None),
}
BIG = [n for n in WEIGHTS if LAYOUT[n][1] is not None and n != "conv_w"]
SMALL = [n for n in WEIGHTS if n not in BIG]
ROW = 1024
SMALL_ROWS = 16


def _shard_shape(name):
    shape, axis = LAYOUT[name]
    if axis is None:
        return shape
    return tuple(s // N_CHIPS if a == axis else s for a, s in enumerate(shape))


def _halves(a):
    return a.reshape(a.shape[:-2] + (2, a.shape[-2] // 2, a.shape[-1]))


def _split_chips(name, full):
    shape, axis = LAYOUT[name]
    if axis == 0:
        return full.reshape((N_CHIPS, shape[0] // N_CHIPS) + shape[1:])
    return jnp.moveaxis(full.reshape(shape[:1] + (N_CHIPS, shape[1] // N_CHIPS)), 1, 0)


def _join_chips(name, parts):
    shape, axis = LAYOUT[name]
    if axis == 0:
        return parts.reshape(shape)
    return jnp.moveaxis(parts, 0, 1).reshape(shape)


SMALL_AT = {"ln1_g": (0, 0), "ln1_b": (1, 0), "ln2_g": (2, 0), "ln2_b": (3, 0), "ln3_g": (4, 0), "ln3_b": (5, 0),
            "b_ple_gate": (6, 0), "ln4_g": (7, 0), "ln4_b": (8, 0), "g_attn": (9, 0), "g_conv": (9, D_ATTN),
            "b_forget": (10, 0), "conv_w": (10, LANE)}
CONV_SHARD = D_CONV // N_CHIPS


def _pack_small_grads(grads):
    def body(*refs):
        ins, o_ref = dict(zip(SMALL, refs[:-1])), refs[-1]
        o_ref[...] = jnp.zeros_like(o_ref)
        for s in range(N_CHIPS):
            for n in SMALL:
                r, c0 = SMALL_AT[n]
                if n == "conv_w":
                    for k in range(3):
                        o_ref[s, r:r + 1, c0 + k * CONV_SHARD:c0 + (k + 1) * CONV_SHARD] = (
                            ins[n][k:k + 1, s * CONV_SHARD:(s + 1) * CONV_SHARD])
                else:
                    o_ref[s, r:r + 1, c0:c0 + ins[n].shape[1]] = ins[n][...]

    return pl.pallas_call(
        body, name="pack_small_grads",
        out_shape=jax.ShapeDtypeStruct((N_CHIPS, SMALL_ROWS, ROW), F32),
    )(*[grads[n] for n in SMALL])


def _adamw_math(w, g, m, v):
    c1 = 1.0 - ADAM_B1 ** ADAM_STEP
    c2 = 1.0 - ADAM_B2 ** ADAM_STEP
    m = ADAM_B1 * m + (1.0 - ADAM_B1) * g
    v = ADAM_B2 * v + (1.0 - ADAM_B2) * (g * g)
    return -ADAM_LR * ((m / c1) / (jnp.sqrt(v / c2) + ADAM_EPS) + ADAM_WD * w), m, v


def _adamw_small(g_mine, g_sib, c_idx, w, m, v):
    ns = len(SMALL)

    def body(c_ref, gm_ref, gs_ref, *refs):
        ws, ms, vs = refs[:ns], refs[ns:2 * ns], refs[2 * ns:3 * ns]
        outs = refs[3 * ns:]
        mine_first = c_ref[0] == 0
        top = jnp.where(mine_first, gm_ref[...], gs_ref[...])
        bot = jnp.where(mine_first, gs_ref[...], gm_ref[...])
        for i, n in enumerate(SMALL):
            r, c0 = SMALL_AT[n]
            blk, rr = (top, r) if r < SMALL_ROWS // 2 else (bot, r - SMALL_ROWS // 2)
            rows, width = ws[i].shape
            for k in range(rows):
                g = blk[rr:rr + 1, c0 + k * width:c0 + (k + 1) * width]
                d, mn, vn = _adamw_math(ws[i][k:k + 1, :], g, ms[i][k:k + 1, :], vs[i][k:k + 1, :])
                for q, val in enumerate((g, d, mn, vn)):
                    outs[q * ns + i][k:k + 1, :] = val

    shapes = [jax.ShapeDtypeStruct(a.shape, F32) for a in w]
    vmem = pl.BlockSpec(memory_space=pltpu.VMEM)
    res = pl.pallas_call(
        body, name="adamw_small",
        in_specs=[pl.BlockSpec(memory_space=pltpu.SMEM)] + [vmem] * (2 + 3 * ns),
        out_specs=[vmem] * (4 * ns),
        out_shape=shapes * 4,
    )(c_idx, g_mine, g_sib, *w, *m, *v)
    return [res[q * ns:(q + 1) * ns] for q in range(4)]


def _place():
    x, y, c = lax.axis_index("x"), lax.axis_index("y"), lax.axis_index("c")
    others = [(1 - x, y), (x, 1 - y), (1 - x, 1 - y)]
    return x, y, c, others


ANY_SPEC = pl.BlockSpec(memory_space=pl.ANY)


def _remote(src, dst, send_sems, recv_sems, k, to):
    return pltpu.make_async_remote_copy(src_ref=src, dst_ref=dst, send_sem=send_sems.at[k], recv_sem=recv_sems.at[k],
                                        device_id=to, device_id_type=MESH)


def _all_gather(shards):
    n = len(shards)

    def body(*refs):
        ins, outs, send_sems, recv_sems = refs[:n], refs[n:2 * n], refs[2 * n], refs[2 * n + 1]
        _gather_start(ins, outs, send_sems, recv_sems)
        _gather_finish(ins, outs, send_sems, recv_sems)

    return pl.pallas_call(
        body, name="all_gather_weights",
        out_shape=_gather_shapes(shards), in_specs=[ANY_SPEC] * n, out_specs=[ANY_SPEC] * n,
        scratch_shapes=_gather_sems(n),
    )(*shards)


def _gather_shapes(shards):
    return [jax.ShapeDtypeStruct((N_CHIPS,) + a.shape, a.dtype) for a in shards]


def _gather_sems(n):
    return [pltpu.SemaphoreType.DMA((6 * n,)), pltpu.SemaphoreType.DMA((6 * n,))]


def _gather_sends(ins, outs, send_sems, recv_sems):
    x, y, c, others = _place()
    s = 2 * x + y
    return [_remote(ins[t].at[c], outs[t].at[s, c], send_sems, recv_sems, 6 * t + j, (*chip, c))
            for t in range(len(ins)) for j, chip in enumerate(others)]


def _gather_start(ins, outs, send_sems, recv_sems):
    for cp in _gather_sends(ins, outs, send_sems, recv_sems):
        cp.start()


def _gather_finish(ins, outs, send_sems, recv_sems):
    x, y, c, others = _place()
    slot = lambda t, chip, half: outs[t].at[2 * chip[0] + chip[1], half]
    passed = []
    for t in range(len(ins)):
        for j, chip in enumerate(others):
            landed = slot(t, chip, c)
            _remote(landed, landed, send_sems, recv_sems, 6 * t + j, (x, y, c)).wait_recv()
            passed.append(_remote(landed, landed, send_sems, recv_sems, 6 * t + 3 + j, (x, y, 1 - c)))
            passed[-1].start()
    for t in range(len(ins)):
        for j, chip in enumerate(others):
            landed = slot(t, chip, 1 - c)
            _remote(landed, landed, send_sems, recv_sems, 6 * t + 3 + j, (x, y, c)).wait_recv()
    for cp in _gather_sends(ins, outs, send_sems, recv_sems) + passed:
        cp.wait_send()


def _swap_halves(gs, tag):
    n = len(gs)

    def body(*refs):
        ins, outs, send_sems, recv_sems = refs[:n], refs[n:2 * n], refs[2 * n], refs[2 * n + 1]
        _swap_start(ins, outs, send_sems, recv_sems)
        _swap_finish(ins, outs, send_sems, recv_sems)

    return pl.pallas_call(
        body, name="grad_swap_halves_" + tag,
        out_shape=_swap_shapes(gs), in_specs=[ANY_SPEC] * n, out_specs=[ANY_SPEC] * n,
        scratch_shapes=_swap_sems(n),
    )(*gs)


def _swap_shapes(gs):
    return [jax.ShapeDtypeStruct(g.shape[:1] + g.shape[2:], g.dtype) for g in gs]


def _swap_sems(n):
    return [pltpu.SemaphoreType.DMA((n,)), pltpu.SemaphoreType.DMA((n,))]


def _swap_copies(ins, outs, send_sems, recv_sems):
    x, y, c, _ = _place()
    return [_remote(ins[t].at[:, 1 - c], outs[t], send_sems, recv_sems, t, (x, y, 1 - c)) for t in range(len(ins))]


def _swap_start(ins, outs, send_sems, recv_sems):
    for cp in _swap_copies(ins, outs, send_sems, recv_sems):
        cp.start()


def _swap_finish(ins, outs, send_sems, recv_sems):
    for cp in _swap_copies(ins, outs, send_sems, recv_sems):
        cp.wait()


def _exchange_chips(pps):
    n = len(pps)

    def body(*refs):
        ins, outs, send_sems, recv_sems = refs[:n], refs[n:2 * n], refs[2 * n], refs[2 * n + 1]
        _exchange_start(ins, outs, send_sems, recv_sems)
        _exchange_finish(ins, outs, send_sems, recv_sems)

    return pl.pallas_call(
        body, name="grad_exchange_chips",
        out_shape=_exchange_shapes(pps), in_specs=[ANY_SPEC] * n, out_specs=[ANY_SPEC] * n,
        scratch_shapes=_exchange_sems(n),
    )(*pps)


def _exchange_shapes(pps):
    return [jax.ShapeDtypeStruct(p.shape, p.dtype) for p in pps]


def _exchange_sems(n):
    return [pltpu.SemaphoreType.DMA((3 * n,)), pltpu.SemaphoreType.DMA((3 * n,))]


def _exchange_sends(ins, outs, send_sems, recv_sems):
    x, y, c, others = _place()
    s = 2 * x + y
    return [_remote(ins[t].at[2 * chip[0] + chip[1]], outs[t].at[s], send_sems, recv_sems, 3 * t + j, (*chip, c))
            for t in range(len(ins)) for j, chip in enumerate(others)]


def _exchange_start(ins, outs, send_sems, recv_sems):
    for cp in _exchange_sends(ins, outs, send_sems, recv_sems):
        cp.start()


def _exchange_finish(ins, outs, send_sems, recv_sems):
    x, y, c, others = _place()
    for t in range(len(ins)):
        for j, chip in enumerate(others):
            landed = outs[t].at[2 * chip[0] + chip[1]]
            _remote(landed, landed, send_sems, recv_sems, 3 * t + j, (x, y, c)).wait_recv()
    for cp in _exchange_sends(ins, outs, send_sems, recv_sems):
        cp.wait_send()


def _share_half(rs):
    n = len(rs)

    def body(*refs):
        ins, outs, send_sems, recv_sems = refs[:n], refs[n:2 * n], refs[2 * n], refs[2 * n + 1]
        x, y, c, _ = _place()
        copies = [_remote(ins[t], outs[t], send_sems, recv_sems, t, (x, y, 1 - c)) for t in range(n)]
        for cp in copies:
            cp.start()
        for cp in copies:
            cp.wait()

    return pl.pallas_call(
        body, name="grad_share_half",
        out_shape=[jax.ShapeDtypeStruct(r.shape, r.dtype) for r in rs],
        in_specs=[ANY_SPEC] * n, out_specs=[ANY_SPEC] * n,
        scratch_shapes=[pltpu.SemaphoreType.DMA((n,)), pltpu.SemaphoreType.DMA((n,))],
    )(*rs)


ELEMENTWISE_BLOCK_BYTES = 1 << 20


def _row_tile(rows, cols):
    return _tile(rows, max(8, ELEMENTWISE_BLOCK_BYTES // (4 * cols) // 8 * 8))


def _add_my_half(g, sib, c_idx, name):
    rh, cols = g.shape[2:]
    tr = _row_tile(rh, cols)

    def body(c_ref, g_ref, s_ref, o_ref):
        o_ref[...] = (g_ref[...] + s_ref[...]).astype(BF16)

    return pl.pallas_call(
        body, name="grad_add_halves_" + name,
        grid_spec=pltpu.PrefetchScalarGridSpec(
            num_scalar_prefetch=1, grid=(N_CHIPS, rh // tr),
            in_specs=[pl.BlockSpec((None, None, tr, cols), lambda s, i, c: (s, c[0], i, 0)),
                      pl.BlockSpec((None, tr, cols), lambda s, i, c: (s, i, 0))],
            out_specs=pl.BlockSpec((None, tr, cols), lambda s, i, c: (s, i, 0))),
        out_shape=jax.ShapeDtypeStruct((N_CHIPS, rh, cols), BF16),
        compiler_params=_cp(2),
    )(c_idx, g, sib)


def _sum_chips(parts, pp, s_idx, name):
    rh, cols = parts.shape[1:]
    tr = _row_tile(rh, cols)

    def body(s_ref, p0, p1, p2, p3, mine_ref, o_ref):
        own = mine_ref[...]
        t = [jnp.where(s_ref[0] == k, own, p[...]).astype(F32) for k, p in enumerate((p0, p1, p2, p3))]
        o_ref[...] = ((t[0] + t[1]) + t[2]) + t[3]

    slot = lambda k: pl.BlockSpec((None, tr, cols), lambda i, s: (jnp.where(s[0] == k, (k + 1) % N_CHIPS, k), i, 0))
    return pl.pallas_call(
        body, name="grad_sum_chips_" + name,
        grid_spec=pltpu.PrefetchScalarGridSpec(
            num_scalar_prefetch=1, grid=(rh // tr,),
            in_specs=[slot(0), slot(1), slot(2), slot(3), pl.BlockSpec((None, tr, cols), lambda i, s: (s[0], i, 0))],
            out_specs=pl.BlockSpec((tr, cols), lambda i, s: (i, 0))),
        out_shape=jax.ShapeDtypeStruct((rh, cols), F32),
        compiler_params=_cp(1),
    )(s_idx, parts, parts, parts, parts, pp)


def _adamw(w, g_mine, g_sib, m, v, c_idx, name):
    rows, cols = w.shape
    tr = _row_tile(rows // 2, cols)
    nbh = rows // 2 // tr

    def body(c_ref, w_ref, gm_ref, gs_ref, m_ref, v_ref, g_ref, d_ref, mo_ref, vo_ref):
        g = jnp.where(pl.program_id(0) // nbh == c_ref[0], gm_ref[...], gs_ref[...])
        g_ref[...] = g
        d_ref[...], mo_ref[...], vo_ref[...] = _adamw_math(w_ref[...], g, m_ref[...], v_ref[...])

    spec = pl.BlockSpec((tr, cols), lambda i, c: (i, 0))
    half = pl.BlockSpec((tr, cols), lambda i, c: (i % nbh, 0))
    return pl.pallas_call(
        body, name="adamw_" + name,
        grid_spec=pltpu.PrefetchScalarGridSpec(
            num_scalar_prefetch=1, grid=(rows // tr,),
            in_specs=[spec, half, half, spec, spec], out_specs=[spec] * 4),
        out_shape=[jax.ShapeDtypeStruct(w.shape, F32)] * 4,
        compiler_params=_cp(1),
    )(c_idx, w, g_mine, g_sib, m, v)


def kernel(x, p, ffn1_w_in, ffn1_w_out, ln1_g, ln1_b, w_mix_in, b_forget, conv_w, g_attn, g_conv, w_mix_out, ln2_g, ln2_b, ffn2_w_in, ffn2_w_out, ln3_g, ln3_b, w_ple, w_ple_gate, b_ple_gate, ln4_g, ln4_b, loss_target, m_ffn1_w_in, m_ffn1_w_out, m_ln1_g, m_ln1_b, m_w_mix_in, m_b_forget, m_conv_w, m_g_attn, m_g_conv, m_w_mix_out, m_ln2_g, m_ln2_b, m_ffn2_w_in, m_ffn2_w_out, m_ln3_g, m_ln3_b, m_w_ple, m_w_ple_gate, m_b_ple_gate, m_ln4_g, m_ln4_b, v_ffn1_w_in, v_ffn1_w_out, v_ln1_g, v_ln1_b, v_w_mix_in, v_b_forget, v_conv_w, v_g_attn, v_g_conv, v_w_mix_out, v_ln2_g, v_ln2_b, v_ffn2_w_in, v_ffn2_w_out, v_ln3_g, v_ln3_b, v_w_ple, v_w_ple_gate, v_b_ple_gate, v_ln4_g, v_ln4_b):
    args = dict(locals())
    shard = {n: args[n][0] if LAYOUT[n][1] is not None else args[n] for n in WEIGHTS}
    m_shard = {n: args["m_" + n][0] if LAYOUT[n][1] is not None else args["m_" + n] for n in WEIGHTS}
    v_shard = {n: args["v_" + n][0] if LAYOUT[n][1] is not None else args["v_" + n] for n in WEIGHTS}
    c_idx = lax.axis_index("c").astype(jnp.int32).reshape(1)
    chip = (2 * lax.axis_index("x") + lax.axis_index("y")).astype(jnp.int32)

    conv_rows = SMALL_ROWS - shard["conv_w"].shape[0]
    mine = {n: _halves(shard[n].astype(BF16)) for n in BIG}
    mine["conv_w"] = _halves(jnp.pad(shard["conv_w"], ((0, conv_rows), (0, 0))))
    early_w = ["ffn1_w_in", "ffn1_w_out"]
    late_w = [n for n in BIG if n not in early_w] + ["conv_w"]

    def full_weights(names, gathered):
        out = {}
        for n, theirs in zip(names, gathered):
            g = lax.dynamic_update_slice(theirs, mine[n][None], (chip, 0, 0, 0))
            if n == "conv_w":
                out[n] = _join_chips(n, g.reshape(N_CHIPS, SMALL_ROWS, CONV_SHARD)[:, :3])
            else:
                out[n] = _join_chips(n, g.reshape((N_CHIPS,) + _shard_shape(n)))
        return out

    full = full_weights(early_w, _all_gather([mine[n] for n in early_w]))
    full.update({n: shard[n] for n in SMALL if n != "conv_w"})

    def per_chip(names, grads):
        by_chip = lambda n: grads[n] if grads[n].ndim == 3 else _split_chips(n, grads[n])
        return [_halves(_pack_small_grads(grads) if n == "small" else by_chip(n)) for n in names]

    def add_halves(names, mine_, sibs):
        return [_add_my_half(g, sib, c_idx, n) for n, g, sib in zip(names, mine_, sibs)]

    def chip_sums(names, grads):
        mine_ = per_chip(names, grads)
        return add_halves(names, mine_, _swap_halves(mine_, names[0]))

    ready_a = ["ffn2_w_in", "ffn2_w_out", "w_ple", "w_ple_gate"]
    ready_b = ["w_mix_in", "w_mix_out"]
    early_g = ready_a + ready_b
    late_g = early_w + ["small"]
    loss_acc, grad_x, grads, early_sums, early_parts = _local_step(
        x[0], p[0, 0], loss_target[0], full,
        overlap={"gather": [mine[n] for n in late_w], "weights": lambda gathered: full_weights(late_w, gathered),
                 "swap": lambda grads: per_chip(ready_a, grads),
                 "chip_sums": lambda grads, swapped, received: (add_halves(ready_a, swapped, received)
                                                                + chip_sums(ready_b, grads))})
    loss = lax.psum(loss_acc[0, 0], ("x", "y", "c"))

    late_sums = chip_sums(late_g, grads)
    names = early_g + late_g
    sums = list(early_sums) + late_sums
    parts = list(early_parts) + list(_exchange_chips(late_sums))
    half_of = {n: _sum_chips(pt, own, chip.reshape(1), n) for n, pt, own in zip(names, parts, sums)}
    names = BIG + ["small"]
    my_half = [half_of[n] for n in names]
    sib_half = _share_half(my_half)

    out = {}
    for n, gm, gs in zip(BIG, my_half, sib_half):
        out[n] = [a[None] for a in _adamw(shard[n], gm, gs, m_shard[n], v_shard[n], c_idx, n)]
    small = _adamw_small(my_half[-1], sib_half[-1], c_idx, [shard[n] for n in SMALL], [m_shard[n] for n in SMALL],
                         [v_shard[n] for n in SMALL])
    for i, n in enumerate(SMALL):
        out[n] = [small[q][i][None] if n == "conv_w" else small[q][i] for q in range(4)]
    return (loss, grad_x[None], *[out[n][q] for q in range(4) for n in WEIGHTS])
```

```python
import functools
import math

import jax
import jax.numpy as jnp
from jax import lax
from jax.experimental import pallas as pl
from jax.experimental.pallas import tpu as pltpu

F32 = jnp.float32
BF16 = jnp.bfloat16

D_MODEL = 1024
D_FF = 2816
N_HEADS = 8
HEAD_DIM = 64
D_ATTN = N_HEADS * HEAD_DIM
D_CONV = 512
PLE_DIM = 256
N_FLOG = 128
ALPHA = 2.0 ** 0.25
LN_EPS = 1e-5
RMS_EPS = 1e-6
NEG_INF = -1e30
Q_SCALE = 1.0 / math.sqrt(HEAD_DIM)
LOG2E = math.log2(math.e)

ADAM_LR = 0.001
ADAM_B1 = 0.9
ADAM_B2 = 0.999
ADAM_EPS = 1e-08
ADAM_WD = 0.01
ADAM_STEP = 10

V7X_VMEM_BYTES = 64 << 20
VMEM_LIMIT = V7X_VMEM_BYTES - (8 << 20)
LANE = 128
FF_CHUNK = 256
N_CHIPS = 4
MESH = pl.DeviceIdType.MESH


def _cp(n_axes):
    return pltpu.CompilerParams(dimension_semantics=("arbitrary",) * n_axes, vmem_limit_bytes=VMEM_LIMIT)


def _resident(shape):
    n = len(shape)
    return pl.BlockSpec(shape, lambda *_: (0,) * n, pipeline_mode=pl.Buffered(1))


def _nn(a, b):
    return jnp.dot(a, b, preferred_element_type=F32)


def _nt(a, b):
    return lax.dot_general(a, b, (((1,), (1,)), ((), ())), preferred_element_type=F32)


def _tn(a, b):
    return lax.dot_general(a, b, (((0,), (0,)), ((), ())), preferred_element_type=F32)


def _ln_stats(r):
    mu = jnp.mean(r, axis=-1, keepdims=True)
    xc = r - mu
    var = jnp.mean(xc * xc, axis=-1, keepdims=True)
    rstd = lax.rsqrt(var + LN_EPS)
    return xc * rstd, rstd


def _ln_bwd(dy, xhat, rstd, g):
    dxh = dy * g
    m1 = jnp.mean(dxh, axis=-1, keepdims=True)
    m2 = jnp.mean(dxh * xhat, axis=-1, keepdims=True)
    return rstd * (dxh - m1 - xhat * m2)


def _sigmoid(z):
    return 1.0 / (1.0 + jnp.exp(-z))


def _rowsum(a):
    return jnp.sum(a, axis=0, keepdims=True)


def _tile(total, want):
    if total <= want:
        return total
    for t in range(want - want % 8, 0, -8):
        if total % t == 0:
            return t
    raise ValueError((total, want))


def _ffn_fwd(x, w_in, w_out, lg, lb, name, gather=()):
    T = x.shape[0]
    tm = _tile(T, 512)
    nf = D_FF // FF_CHUNK
    ng = len(gather)
    last = T // tm - 1

    def body(x_ref, wi_ref, wo_ref, lg_ref, lb_ref, *rest):
        comm_in, (xo_ref, r_ref, g_ref, u_ref, h_ref) = rest[:ng], rest[ng:ng + 5]
        comm_out, sems = rest[ng + 5:2 * ng + 5], rest[2 * ng + 5:]
        if ng:
            @pl.when(pl.program_id(0) == 0)
            def _():
                _gather_start(comm_in, comm_out, *sems)

        xf = x_ref[...]
        xb = xf.astype(BF16)
        acc = jnp.zeros((tm, D_MODEL), F32)
        for j in range(nf):
            c0 = j * FF_CHUNK
            g = _nn(xb, wi_ref[:, c0:c0 + FF_CHUNK])
            u = _nn(xb, wi_ref[:, D_FF + c0:D_FF + c0 + FF_CHUNK])
            hb = (g * _sigmoid(g) * u).astype(BF16)
            g_ref[:, c0:c0 + FF_CHUNK] = g.astype(BF16)
            u_ref[:, c0:c0 + FF_CHUNK] = u.astype(BF16)
            h_ref[:, c0:c0 + FF_CHUNK] = hb
            acc = acc + _nn(hb, wo_ref[c0:c0 + FF_CHUNK, :])
        r = ALPHA * xf + 0.5 * acc
        r_ref[...] = r
        xhat, _ = _ln_stats(r)
        xo_ref[...] = xhat * lg_ref[...] + lb_ref[...]
        if ng:
            @pl.when(pl.program_id(0) == last)
            def _():
                _gather_finish(comm_in, comm_out, *sems)

    row = lambda n: pl.BlockSpec((tm, n), lambda i: (i, 0))
    return pl.pallas_call(
        body, name=name, grid=(T // tm,),
        in_specs=[row(D_MODEL), _resident((D_MODEL, 2 * D_FF)), _resident((D_FF, D_MODEL)),
                  _resident((1, D_MODEL)), _resident((1, D_MODEL))] + [ANY_SPEC] * ng,
        out_specs=[row(D_MODEL), row(D_MODEL), row(D_FF), row(D_FF), row(D_FF)] + [ANY_SPEC] * ng,
        out_shape=[jax.ShapeDtypeStruct((T, D_MODEL), F32), jax.ShapeDtypeStruct((T, D_MODEL), F32),
                   jax.ShapeDtypeStruct((T, D_FF), BF16), jax.ShapeDtypeStruct((T, D_FF), BF16),
                   jax.ShapeDtypeStruct((T, D_FF), BF16)] + _gather_shapes(gather),
        scratch_shapes=_gather_sems(ng) if ng else [],
        compiler_params=_cp(1),
    )(x, w_in, w_out, lg, lb, *gather)


def _ffn_bwd(dxo, r, g, u, w_in, w_out, lg, name, exchange=()):
    T = r.shape[0]
    tm = _tile(T, 256)
    nf = D_FF // FF_CHUNK
    ne = len(exchange)
    last = T // tm - 1

    def body(dxo_ref, r_ref, g_ref, u_ref, wi_ref, wo_ref, lg_ref, *rest):
        comm_in, (dx_ref, dgu_ref, df_ref, dlg_ref, dlb_ref) = rest[:ne], rest[ne:ne + 5]
        comm_out, sems = rest[ne + 5:2 * ne + 5], rest[2 * ne + 5:]
        i = pl.program_id(0)
        if ne:
            @pl.when(i == 0)
            def _():
                _exchange_start(comm_in, comm_out, *sems)

        dy = dxo_ref[...]
        xhat, rstd = _ln_stats(r_ref[...])
        dr = _ln_bwd(dy, xhat, rstd, lg_ref[...])

        @pl.when(i == 0)
        def _():
            dlg_ref[...] = jnp.zeros_like(dlg_ref)
            dlb_ref[...] = jnp.zeros_like(dlb_ref)

        dlg_ref[...] += _rowsum(dy * xhat)
        dlb_ref[...] += _rowsum(dy)
        dfb = (0.5 * dr).astype(BF16)
        df_ref[...] = dfb
        acc = jnp.zeros((tm, D_MODEL), F32)
        dh_ahead = _nt(dfb, wo_ref[0:FF_CHUNK, :])
        for j in range(nf):
            c0 = j * FF_CHUNK
            dh = dh_ahead
            if j + 1 < nf:
                dh_ahead = _nt(dfb, wo_ref[c0 + FF_CHUNK:c0 + 2 * FF_CHUNK, :])
            gg = g_ref[:, c0:c0 + FF_CHUNK].astype(F32)
            uu = u_ref[:, c0:c0 + FF_CHUNK].astype(F32)
            s = _sigmoid(gg)
            dgb = (dh * uu * s * (1.0 + gg * (1.0 - s))).astype(BF16)
            dub = (dh * gg * s).astype(BF16)
            dgu_ref[:, c0:c0 + FF_CHUNK] = dgb
            dgu_ref[:, D_FF + c0:D_FF + c0 + FF_CHUNK] = dub
            acc = acc + _nt(dgb, wi_ref[:, c0:c0 + FF_CHUNK]) + _nt(dub, wi_ref[:, D_FF + c0:D_FF + c0 + FF_CHUNK])
        dx_ref[...] = ALPHA * dr + acc
        if ne:
            @pl.when(i == last)
            def _():
                _exchange_finish(comm_in, comm_out, *sems)

    row = lambda n: pl.BlockSpec((tm, n), lambda i: (i, 0))
    return pl.pallas_call(
        body, name=name, grid=(T // tm,),
        in_specs=[row(D_MODEL), row(D_MODEL), row(D_FF), row(D_FF), _resident((D_MODEL, 2 * D_FF)),
                  _resident((D_FF, D_MODEL)), _resident((1, D_MODEL))] + [ANY_SPEC] * ne,
        out_specs=[row(D_MODEL), row(2 * D_FF), row(D_MODEL), _resident((1, D_MODEL)), _resident((1, D_MODEL))]
        + [ANY_SPEC] * ne,
        out_shape=[jax.ShapeDtypeStruct((T, D_MODEL), F32), jax.ShapeDtypeStruct((T, 2 * D_FF), BF16),
                   jax.ShapeDtypeStruct((T, D_MODEL), BF16), jax.ShapeDtypeStruct((1, D_MODEL), F32),
                   jax.ShapeDtypeStruct((1, D_MODEL), F32)] + _exchange_shapes(exchange),
        scratch_shapes=_exchange_sems(ne) if ne else [],
        compiler_params=_cp(1),
    )(dxo, r, g, u, w_in, w_out, lg, *exchange)


def _matmul_tn(a, b, name, by_chip=False, exchange=()):
    T, K = a.shape
    N = b.shape[1]
    tt = _tile(T, 2048)
    tn = N // N_CHIPS if by_chip else N
    while K * tn * 4 > (6 << 20) and tn % 256 == 0 and not by_chip:
        tn //= 2
    assert N % tn == 0
    ne = len(exchange)
    grid = (N // tn, T // tt)

    def body(a_ref, b_ref, *rest):
        comm_in, o_ref, comm_out, sems = rest[:ne], rest[ne], rest[ne + 1:2 * ne + 1], rest[2 * ne + 1:]
        n, t = pl.program_id(0), pl.program_id(1)
        if ne:
            @pl.when((n == 0) & (t == 0))
            def _():
                _exchange_start(comm_in, comm_out, *sems)

        @pl.when(t == 0)
        def _():
            o_ref[...] = jnp.zeros_like(o_ref)

        o_ref[...] += _tn(a_ref[...].astype(BF16), b_ref[...].astype(BF16))
        if ne:
            @pl.when((n == grid[0] - 1) & (t == grid[1] - 1))
            def _():
                _exchange_finish(comm_in, comm_out, *sems)

    res = pl.pallas_call(
        body, name=name, grid=grid,
        in_specs=[pl.BlockSpec((tt, K), lambda n, t: (t, 0)), pl.BlockSpec((tt, tn), lambda n, t: (t, n))]
        + [ANY_SPEC] * ne,
        out_specs=[pl.BlockSpec((None, K, tn), lambda n, t: (n, 0, 0)) if by_chip
                   else pl.BlockSpec((K, tn), lambda n, t: (0, n))] + [ANY_SPEC] * ne,
        out_shape=[jax.ShapeDtypeStruct((N_CHIPS, K, tn) if by_chip else (K, N), F32)] + _exchange_shapes(exchange),
        scratch_shapes=_exchange_sems(ne) if ne else [],
        compiler_params=_cp(2),
    )(a, b, *exchange)
    return res if ne else res[0]


def _matmul_tokens(at, b, name):
    M, T = at.shape
    N = b.shape[1]
    tt = _tile(T, 1024)

    def body(a_ref, b_ref, o_ref):
        @pl.when(pl.program_id(0) == 0)
        def _():
            o_ref[...] = jnp.zeros_like(o_ref)

        o_ref[...] += _nn(a_ref[...].astype(BF16), b_ref[...].astype(BF16))

    return pl.pallas_call(
        body, name=name, grid=(T // tt,),
        in_specs=[pl.BlockSpec((M, tt), lambda t: (0, t)), pl.BlockSpec((tt, N), lambda t: (t, 0))],
        out_specs=pl.BlockSpec((M, N), lambda t: (0, 0)),
        out_shape=jax.ShapeDtypeStruct((M, N), F32),
        compiler_params=_cp(1),
    )(at, b)


def _matmul_nn(x, w, scale, out_dtype, name, also_transposed=False):
    T, K = x.shape
    N = w.shape[1]
    tm = _tile(T, 512)

    def body(x_ref, w_ref, s_ref, o_ref, *ot_ref):
        res = _nn(x_ref[...].astype(BF16), w_ref[...]) * s_ref[...]
        o_ref[...] = res.astype(out_dtype)
        if also_transposed:
            ot_ref[0][...] = res.T.astype(out_dtype)

    res = pl.pallas_call(
        body, name=name, grid=(T // tm,),
        in_specs=[pl.BlockSpec((tm, K), lambda i: (i, 0)), _resident((K, N)), _resident((1, N))],
        out_specs=[pl.BlockSpec((tm, N), lambda i: (i, 0))] + [pl.BlockSpec((N, tm), lambda i: (0, i))] * also_transposed,
        out_shape=[jax.ShapeDtypeStruct((T, N), out_dtype)] + [jax.ShapeDtypeStruct((N, T), out_dtype)] * also_transposed,
        compiler_params=_cp(1),
    )(x, w, scale)
    return res if also_transposed else res[0]


def _log_sigmoid(z):
    return jnp.minimum(z, 0.0) - jnp.log1p(jnp.exp(-jnp.abs(z)))


def _tri(n, lower):
    r = lax.broadcasted_iota(jnp.int32, (n, n), 0)
    c = lax.broadcasted_iota(jnp.int32, (n, n), 1)
    return jnp.where((c <= r) if lower else (c >= r), 1.0, 0.0).astype(F32)


def _f32dot(a, b):
    return jnp.dot(a, b, preferred_element_type=F32, precision=lax.Precision.HIGHEST)


def _forget_cumsum(flog, col, bf):
    T = flog.shape[0]
    bt = _tile(T, 512)

    def body(f_ref, b_ref, c_ref, carry):
        @pl.when(pl.program_id(0) == 0)
        def _():
            carry[...] = jnp.zeros_like(carry)

        lf = _log_sigmoid(f_ref[...] + b_ref[...])
        c = _f32dot(_tri(bt, True), lf) + carry[...]
        c_ref[...] = c * LOG2E
        carry[...] = c[bt - 1:bt, :]

    return pl.pallas_call(
        body, name="forget_cumsum", grid=(T // bt,),
        in_specs=[pl.BlockSpec((bt, N_FLOG), lambda i: (i, col)), _resident((1, N_FLOG))],
        out_specs=pl.BlockSpec((bt, N_FLOG), lambda i: (i, 0)),
        out_shape=jax.ShapeDtypeStruct((T, N_FLOG), F32),
        scratch_shapes=[pltpu.VMEM((1, N_FLOG), F32)],
        compiler_params=_cp(1),
    )(flog, bf)


def _forget_bwd(dck, dcq, flog, col, bf):
    T = dcq.shape[0]
    bt = _tile(T, 512)
    nb = T // bt

    def body(k0_ref, k1_ref, k2_ref, k3_ref, dcq_ref, f_ref, b_ref, dz_ref, db_ref, carry):
        @pl.when(pl.program_id(0) == 0)
        def _():
            carry[...] = jnp.zeros_like(carry)
            db_ref[...] = jnp.zeros_like(db_ref)

        dc = ((k0_ref[...] + k1_ref[...]) + (k2_ref[...] + k3_ref[...])) + dcq_ref[...]
        dlf = _f32dot(_tri(bt, False), dc) + carry[...]
        carry[...] = dlf[0:1, :]
        z = f_ref[...] + b_ref[...]
        dz = dlf * _sigmoid(-z)
        dz_ref[...] = dz.astype(BF16)
        db_ref[...] += _rowsum(dz)

    slab = lambda j: pl.BlockSpec((None, bt, N_FLOG), lambda i: (j, nb - 1 - i, 0))
    return pl.pallas_call(
        body, name="forget_bwd", grid=(nb,),
        in_specs=[slab(0), slab(1), slab(2), slab(3),
                  pl.BlockSpec((bt, N_FLOG), lambda i: (nb - 1 - i, 0)),
                  pl.BlockSpec((bt, N_FLOG), lambda i: (nb - 1 - i, col)), _resident((1, N_FLOG))],
        out_specs=[pl.BlockSpec((bt, N_FLOG), lambda i: (nb - 1 - i, 0)), _resident((1, N_FLOG))],
        out_shape=[jax.ShapeDtypeStruct((T, N_FLOG), BF16), jax.ShapeDtypeStruct((1, N_FLOG), F32)],
        scratch_shapes=[pltpu.VMEM((1, N_FLOG), F32)],
        compiler_params=_cp(1),
    )(dck, dck, dck, dck, dcq, flog, bf)


def _head_masks():
    lane = lax.broadcasted_iota(jnp.int32, (1, LANE), 1)
    return lane < HEAD_DIM


def _split_heads(x2, is_a):
    zero = jnp.zeros_like(x2)
    return jnp.where(is_a, x2, zero), jnp.where(is_a, zero, x2)


BIAS_PARTS = 3
ATTN_GROUP = 4
ATTN_FWD_GROUP = 8


def _bias_lanes(h):
    lane = lax.broadcasted_iota(jnp.int32, (1, LANE), 1)
    first = (1 - h) * HEAD_DIM
    return lane, first


def _fold_key_bias(qkv, c):
    T = qkv.shape[0]
    tm = _tile(T, 512)
    npair = N_HEADS // 2

    def body(k_ref, c_ref, o_ref):
        cc = c_ref[...]
        parts, rest = [], cc
        for _ in range(BIAS_PARTS):
            piece = rest.astype(BF16)
            parts.append(piece)
            rest = rest - piece.astype(F32)
        for j in range(npair):
            k2 = k_ref[:, j * LANE:(j + 1) * LANE]
            for h in range(2):
                lane, first = _bias_lanes(h)
                out = k2
                for n, piece in enumerate(parts):
                    col = piece[:, 2 * j + h:2 * j + h + 1]
                    out = jnp.where(lane == first + n, col, out)
                o_ref[:, (2 * j + h) * LANE:(2 * j + h + 1) * LANE] = out

    return pl.pallas_call(
        body, name="fold_key_bias", grid=(T // tm,),
        in_specs=[pl.BlockSpec((tm, D_ATTN), lambda i: (i, 1)), pl.BlockSpec((tm, N_FLOG), lambda i: (i, 0))],
        out_specs=pl.BlockSpec((tm, 2 * D_ATTN), lambda i: (i, 0)),
        out_shape=jax.ShapeDtypeStruct((T, 2 * D_ATTN), BF16),
        compiler_params=_cp(1),
    )(qkv, c)


def _attn_fwd(qkv, vt, kb):
    T = qkv.shape[0]
    tq = _tile(T, 512)
    tk = tq
    nq = T // tq
    npair = N_HEADS // 2

    def body(q_ref, ka_ref, kb_ref, vt_ref, o_ref, al_ref, m_s, l_s, acc_s):
        i = pl.program_id(1)
        qs = []
        for h, qh in enumerate(_split_heads(q_ref[...], _head_masks())):
            lane, first = _bias_lanes(h)
            qs.append(jnp.where((lane >= first) & (lane < first + BIAS_PARTS), -1.0, qh).astype(BF16))
        k_refs = (ka_ref, kb_ref)
        m_s[...] = jnp.full_like(m_s, NEG_INF)
        l_s[...] = jnp.zeros_like(l_s)
        acc_s[...] = jnp.zeros_like(acc_s)

        def scores_at(kk):
            k0 = pl.multiple_of(kk * tk, tk)
            return tuple(_nt(k_refs[h][pl.ds(k0, tk), :], qs[h]) for h in range(2))

        def consume(kk, scores, masked):
            k0 = pl.multiple_of(kk * tk, tk)
            v2t = vt_ref[:, pl.ds(k0, tk)]
            for h in range(2):
                zt = scores[h]
                if masked:
                    rr = lax.broadcasted_iota(jnp.int32, (tk, tq), 0)
                    cc = lax.broadcasted_iota(jnp.int32, (tk, tq), 1)
                    zt = jnp.where(cc >= rr, zt, NEG_INF)
                m_old = m_s[h]
                m_new = jnp.maximum(m_old, jnp.max(zt, axis=0, keepdims=True))
                p = jnp.exp2(zt - m_new)
                a = jnp.exp2(m_old - m_new)
                l_s[h] = a * l_s[h] + jnp.sum(p, axis=0, keepdims=True)
                acc_s[h] = a * acc_s[h] + _nn(v2t, p.astype(BF16))
                m_s[h] = m_new

        def group(kk, n, last_masked):
            scores = [scores_at(kk + u) for u in range(n)]
            for u in range(n):
                consume(kk + u, scores[u], last_masked and u == n - 1)

        def loop_body(t, carry):
            group(ATTN_FWD_GROUP * t, ATTN_FWD_GROUP, False)
            return carry

        lax.fori_loop(0, i // ATTN_FWD_GROUP, loop_body, 0)
        for left in range(ATTN_FWD_GROUP):
            @pl.when(i % ATTN_FWD_GROUP == left)
            def _():
                group(i - left, left + 1, True)

        outs = []
        for h in range(2):
            l = l_s[h]
            outs.append(acc_s[h] * (1.0 / l))
            al_ref[0, h:h + 1, :] = -(m_s[h] + jnp.log2(l))
        al_ref[0, 2:8, :] = jnp.zeros((6, tq), F32)
        dim = lax.broadcasted_iota(jnp.int32, (LANE, 1), 0)
        o_ref[...] = jnp.where(dim < HEAD_DIM, outs[0], outs[1]).T

    rowl = pl.BlockSpec((1, 8, tq), lambda j, i: (j, 0, i))
    return pl.pallas_call(
        body, name="attn_fwd", grid=(npair, nq),
        in_specs=[pl.BlockSpec((tq, LANE), lambda j, i: (i, j)),
                  pl.BlockSpec((T, LANE), lambda j, i: (0, 2 * j), pipeline_mode=pl.Buffered(1)),
                  pl.BlockSpec((T, LANE), lambda j, i: (0, 2 * j + 1), pipeline_mode=pl.Buffered(1)),
                  pl.BlockSpec((LANE, T), lambda j, i: (2 * npair + j, 0), pipeline_mode=pl.Buffered(1))],
        out_specs=[pl.BlockSpec((tq, LANE), lambda j, i: (i, j)), rowl],
        out_shape=[jax.ShapeDtypeStruct((T, D_ATTN), F32), jax.ShapeDtypeStruct((npair, 8, T), F32)],
        scratch_shapes=[pltpu.VMEM((2, 1, tq), F32), pltpu.VMEM((2, 1, tq), F32), pltpu.VMEM((2, LANE, tq), F32)],
        compiler_params=_cp(2),
    )(qkv, kb, kb, vt)


def _attn_bwd(qkv, qkvt, dob, dobt, cb, alrow, dlrow):
    T = qkv.shape[0]
    tq = _tile(T, 512)
    tk = tq
    nq = T // tq
    npair = N_HEADS // 2

    def body(q_ref, qt_ref, k_ref, kt_ref, v_ref, do_ref, dot_ref, cb_ref, al_ref, dl_ref,
             dq_ref, dk_ref, dv_ref, dc_ref, dcq_ref, dk_s, dv_s, dc_s):
        kj = pl.program_id(1)
        is_a = _head_masks()
        ks = _split_heads(k_ref[...], is_a)
        vs = _split_heads(v_ref[...], is_a)
        dim_a = lax.broadcasted_iota(jnp.int32, (LANE, 1), 0) < HEAD_DIM
        kts = _split_heads(kt_ref[...], dim_a)
        head_lane = lax.broadcasted_iota(jnp.int32, (1, LANE), 1) - 2 * pl.program_id(0)
        cs = tuple(jnp.sum(jnp.where(head_lane == h, cb_ref[...], 0.0), axis=-1, keepdims=True) for h in range(2))

        @pl.when(kj == 0)
        def _():
            dq_ref[...] = jnp.zeros_like(dq_ref)
            dcq_ref[...] = jnp.zeros_like(dcq_ref)

        dk_s[...] = jnp.zeros_like(dk_s)
        dv_s[...] = jnp.zeros_like(dv_s)
        dc_s[...] = jnp.zeros_like(dc_s)

        def step(qi, masked):
            q0 = pl.multiple_of(qi * tq, tq)
            q2 = q_ref[pl.ds(q0, tq), :]
            do2 = do_ref[pl.ds(q0, tq), :]
            qt2 = qt_ref[:, pl.ds(q0, tq)]
            dot2 = dot_ref[:, pl.ds(q0, tq)]
            for h in range(2):
                alr = al_ref[0, h:h + 1, pl.ds(q0, tq)]
                dlr = dl_ref[0, h:h + 1, pl.ds(q0, tq)]
                zt = _nt(ks[h], q2) + (alr - cs[h])
                if masked:
                    rr = lax.broadcasted_iota(jnp.int32, (tk, tq), 0)
                    cc = lax.broadcasted_iota(jnp.int32, (tk, tq), 1)
                    zt = jnp.where(cc >= rr, zt, NEG_INF)
                pt = jnp.exp2(zt)
                dst = pt * (_nt(vs[h], do2) - dlr)
                pb = pt.astype(BF16)
                dsb = dst.astype(BF16)
                dv_s[h] += _nt(dot2, pb)
                dk_s[h] += _nt(qt2, dsb)
                dc_s[h] += jnp.sum(dst, axis=-1, keepdims=True)
                dcq_ref[0, h:h + 1, pl.ds(q0, tq)] += jnp.sum(dst, axis=0, keepdims=True)
                dq_ref[:, pl.ds(q0, tq)] += _nn(kts[h], dsb)

        rest = nq - 1 - kj
        for left in range(ATTN_GROUP):
            @pl.when(rest % ATTN_GROUP == left)
            def _():
                for u in range(left + 1):
                    step(kj + u, u == 0)

        def loop_body(t, carry):
            for u in range(ATTN_GROUP):
                step(kj + 1 + rest % ATTN_GROUP + ATTN_GROUP * t + u, False)
            return carry

        lax.fori_loop(0, rest // ATTN_GROUP, loop_body, 0)
        dk_ref[...] = (jnp.where(dim_a, dk_s[0], dk_s[1]) * (1.0 / LOG2E)).astype(BF16)
        dv_ref[...] = jnp.where(dim_a, dv_s[0], dv_s[1]).astype(BF16)
        lane = lax.broadcasted_iota(jnp.int32, (1, LANE), 1)
        head = 2 * pl.program_id(0)
        dc_ref[...] = jnp.where(lane == head, -dc_s[0], jnp.where(lane == head + 1, -dc_s[1], 0.0))

        @pl.when(kj == nq - 1)
        def _():
            dq_ref[...] = dq_ref[...] * Q_SCALE

    full = lambda col: pl.BlockSpec((T, LANE), lambda j, kj: (0, col(j)), pipeline_mode=pl.Buffered(1))
    fullt = lambda row: pl.BlockSpec((LANE, T), lambda j, kj: (row(j), 0), pipeline_mode=pl.Buffered(1))
    tile = lambda col: pl.BlockSpec((tk, LANE), lambda j, kj: (kj, col(j)))
    tilet = lambda row: pl.BlockSpec((LANE, tk), lambda j, kj: (row(j), kj))
    rowl = pl.BlockSpec((1, 8, T), lambda j, kj: (j, 0, 0))
    return pl.pallas_call(
        body, name="attn_bwd", grid=(npair, nq),
        in_specs=[full(lambda j: j), fullt(lambda j: j), tile(lambda j: npair + j), tilet(lambda j: npair + j),
                  tile(lambda j: 2 * npair + j), full(lambda j: j), fullt(lambda j: j), tile(lambda j: 0), rowl, rowl],
        out_specs=[pl.BlockSpec((LANE, T), lambda j, kj: (j, 0)), tilet(lambda j: j), tilet(lambda j: j),
                   pl.BlockSpec((None, tk, LANE), lambda j, kj: (j, kj, 0)), rowl],
        out_shape=[jax.ShapeDtypeStruct((D_ATTN, T), F32), jax.ShapeDtypeStruct((D_ATTN, T), BF16),
                   jax.ShapeDtypeStruct((D_ATTN, T), BF16), jax.ShapeDtypeStruct((npair, T, LANE), F32),
                   jax.ShapeDtypeStruct((npair, 8, T), F32)],
        scratch_shapes=[pltpu.VMEM((2, LANE, tk), F32), pltpu.VMEM((2, LANE, tk), F32), pltpu.VMEM((2, tk, 1), F32)],
        compiler_params=_cp(2),
    )(qkv, qkvt, qkv, qkvt, qkv, dob, dobt, cb, alrow, dlrow)


HALO = 8


def _shift_rows(cur, other, k, tm, down):
    row = lax.broadcasted_iota(jnp.int32, (tm, 1), 0)
    reps = tm // HALO
    if down:
        rolled = pltpu.roll(cur, k, 0)
        fill = jnp.tile(pltpu.roll(other, k, 0), (reps, 1))
        return jnp.where(row < k, fill, rolled)
    rolled = pltpu.roll(cur, tm - k, 0)
    fill = jnp.tile(pltpu.roll(other, HALO - k, 0), (reps, 1))
    return jnp.where(row >= tm - k, fill, rolled)


def _conv_fwd(c, hh, c_prev, hh_prev, w_ref, first, tm):
    u = c * hh
    u_prev = jnp.where(first, 0.0, c_prev * hh_prev)
    u1 = _shift_rows(u, u_prev, 1, tm, True)
    u2 = _shift_rows(u, u_prev, 2, tm, True)
    y = w_ref[0:1, :] * u2 + w_ref[1:2, :] * u1 + w_ref[2:3, :] * u
    return u, u1, u2, y


def _rms(x, g):
    rs = lax.rsqrt(jnp.mean(x * x, axis=-1, keepdims=True) + RMS_EPS)
    return x * rs * g, rs


def _mixer_tail_fwd(o, bchf, conv_w, g_attn, g_conv, w_mo, x1, lg, lb):
    T = o.shape[0]
    tm = _tile(T, 512)
    hb = tm // HALO

    def body(o_ref, b_ref, c_ref, h_ref, cp_ref, hp_ref, w_ref, ga_ref, gc_ref, wmo_ref, x1_ref, lg_ref, lb_ref,
             x2_ref, r2_ref, mg_ref):
        first = pl.program_id(0) == 0
        _, _, _, y = _conv_fwd(c_ref[...], h_ref[...], cp_ref[...], hp_ref[...], w_ref, first, tm)
        na, _ = _rms(o_ref[...], ga_ref[...])
        nc, _ = _rms(b_ref[...] * y, gc_ref[...])
        nab = na.astype(BF16)
        ncb = nc.astype(BF16)
        mg_ref[:, 0:D_ATTN] = nab
        mg_ref[:, D_ATTN:] = ncb
        r2 = ALPHA * x1_ref[...] + _nn(nab, wmo_ref[0:D_ATTN, :]) + _nn(ncb, wmo_ref[D_ATTN:, :])
        r2_ref[...] = r2
        xhat, _ = _ln_stats(r2)
        x2_ref[...] = xhat * lg_ref[...] + lb_ref[...]

    row = lambda n, col=0: pl.BlockSpec((tm, n), lambda i: (i, col))
    prev = lambda col: pl.BlockSpec((HALO, D_CONV), lambda i: (jnp.maximum(i * hb - 1, 0), col))
    return pl.pallas_call(
        body, name="mixer_tail_fwd", grid=(T // tm,),
        in_specs=[row(D_ATTN), row(D_CONV, 0), row(D_CONV, 1), row(D_CONV, 2), prev(1), prev(2),
                  _resident((3, D_CONV)), _resident((1, D_ATTN)), _resident((1, D_CONV)),
                  _resident((D_MODEL, D_MODEL)), row(D_MODEL), _resident((1, D_MODEL)), _resident((1, D_MODEL))],
        out_specs=[row(D_MODEL), row(D_MODEL), row(D_MODEL)],
        out_shape=[jax.ShapeDtypeStruct((T, D_MODEL), F32), jax.ShapeDtypeStruct((T, D_MODEL), F32),
                   jax.ShapeDtypeStruct((T, D_MODEL), BF16)],
        compiler_params=_cp(1),
    )(o, bchf, bchf, bchf, bchf, bchf, conv_w, g_attn, g_conv, w_mo, x1, lg, lb)


def _head_sum_rows():
    row = lax.broadcasted_iota(jnp.int32, (4 * 8, D_ATTN), 0)
    head = lax.broadcasted_iota(jnp.int32, (4 * 8, D_ATTN), 1) // HEAD_DIM
    return jnp.where((row % 8 < 2) & (2 * (row // 8) + row % 8 == head), 1.0, 0.0).astype(F32)


def _mixer_tail_bwd(dx2, r2, lg, w_mo, o, bchf, conv_w, g_attn, g_conv, swap=()):
    T = o.shape[0]
    tm = _tile(T, 512)
    hb = tm // HALO
    ns = len(swap)
    last = T // tm - 1

    def body(dx2_ref, r2_ref, lg_ref, wmo_ref, o_ref, b_ref, c_ref, h_ref, cp_ref, hp_ref, w_ref, ga_ref, gc_ref,
             *rest):
        comm_in = rest[:ns]
        dx1_ref, dr_ref, do_ref, dot_ref, dl_ref, dco_ref, dlg_ref, dlb_ref, dga_ref, dgc_ref = rest[ns:ns + 10]
        comm_out, sems = rest[ns + 10:2 * ns + 10], rest[2 * ns + 10:]
        i = pl.program_id(0)
        if ns:
            @pl.when(i == 0)
            def _():
                _swap_start(comm_in, comm_out, *sems)

            @pl.when(i == last)
            def _():
                _swap_finish(comm_in, comm_out, *sems)

        @pl.when(i == 0)
        def _():
            for ref in (dlg_ref, dlb_ref, dga_ref, dgc_ref):
                ref[...] = jnp.zeros_like(ref)

        dy = dx2_ref[...]
        xhat, rstd = _ln_stats(r2_ref[...])
        dr = _ln_bwd(dy, xhat, rstd, lg_ref[...])
        dlg_ref[...] += _rowsum(dy * xhat)
        dlb_ref[...] += _rowsum(dy)
        dx1_ref[...] = ALPHA * dr
        drb = dr.astype(BF16)
        dr_ref[...] = drb
        dna = _nt(drb, wmo_ref[0:D_ATTN, :])
        dnc = _nt(drb, wmo_ref[D_ATTN:, :])

        def rms_bwd(x, g, dn):
            rs = lax.rsqrt(jnp.mean(x * x, axis=-1, keepdims=True) + RMS_EPS)
            dng = dn * g
            dx = rs * dng - x * (rs * rs * rs) * jnp.mean(dng * x, axis=-1, keepdims=True)
            return dx, _rowsum(dn * x * rs)

        oo = o_ref[...]
        do, dga = rms_bwd(oo, ga_ref[...], dna)
        dga_ref[...] += dga
        do_ref[...] = do.astype(BF16)
        dot_ref[...] = do.T.astype(BF16)
        dl_ref[...] = lax.dot_general(_head_sum_rows(), do * oo, (((1,), (1,)), ((), ())),
                                      preferred_element_type=F32, precision=lax.Precision.HIGHEST)
        _, _, _, y = _conv_fwd(c_ref[...], h_ref[...], cp_ref[...], hp_ref[...], w_ref, i == 0, tm)
        dco, dgc = rms_bwd(b_ref[...] * y, gc_ref[...], dnc)
        dgc_ref[...] += dgc
        dco_ref[...] = dco

    row = lambda n, col=0: pl.BlockSpec((tm, n), lambda i: (i, col))
    prev = lambda col: pl.BlockSpec((HALO, D_CONV), lambda i: (jnp.maximum(i * hb - 1, 0), col))
    vec = lambda n: _resident((1, n))
    return pl.pallas_call(
        body, name="mixer_tail_bwd", grid=(T // tm,),
        in_specs=[row(D_MODEL), row(D_MODEL), vec(D_MODEL), _resident((D_MODEL, D_MODEL)), row(D_ATTN),
                  row(D_CONV, 0), row(D_CONV, 1), row(D_CONV, 2), prev(1), prev(2), _resident((3, D_CONV)),
                  vec(D_ATTN), vec(D_CONV)] + [ANY_SPEC] * ns,
        out_specs=[row(D_MODEL), row(D_MODEL), row(D_ATTN), pl.BlockSpec((D_ATTN, tm), lambda i: (0, i)),
                   pl.BlockSpec((4 * 8, tm), lambda i: (0, i)), row(D_CONV),
                   vec(D_MODEL), vec(D_MODEL), vec(D_ATTN), vec(D_CONV)] + [ANY_SPEC] * ns,
        out_shape=[jax.ShapeDtypeStruct((T, D_MODEL), F32), jax.ShapeDtypeStruct((T, D_MODEL), BF16),
                   jax.ShapeDtypeStruct((T, D_ATTN), BF16), jax.ShapeDtypeStruct((D_ATTN, T), BF16),
                   jax.ShapeDtypeStruct((4 * 8, T), F32),
                   jax.ShapeDtypeStruct((T, D_CONV), F32), jax.ShapeDtypeStruct((1, D_MODEL), F32),
                   jax.ShapeDtypeStruct((1, D_MODEL), F32), jax.ShapeDtypeStruct((1, D_ATTN), F32),
                   jax.ShapeDtypeStruct((1, D_CONV), F32)] + _swap_shapes(swap),
        scratch_shapes=_swap_sems(ns) if ns else [],
        compiler_params=_cp(1),
    )(dx2, r2, lg, w_mo, o, bchf, bchf, bchf, bchf, bchf, conv_w, g_attn, g_conv, *swap)


def _conv_bwd(dco, bchf, conv_w):
    T = dco.shape[0]
    tm = _tile(T, 512)
    hb = tm // HALO
    nt = T // tm

    def body(dco_ref, dcon_ref, b_ref, bn_ref, c_ref, h_ref, cp_ref, hp_ref, w_ref, dbch_ref, dw_ref):
        i = pl.program_id(0)

        @pl.when(i == 0)
        def _():
            dw_ref[...] = jnp.zeros_like(dw_ref)

        cc = c_ref[...]
        hh = h_ref[...]
        u, u1, u2, y = _conv_fwd(cc, hh, cp_ref[...], hp_ref[...], w_ref, i == 0, tm)
        dco = dco_ref[...]
        bb = b_ref[...]
        dyc = dco * bb
        dy_next = jnp.where(i == nt - 1, 0.0, dcon_ref[...] * bn_ref[...])
        d1 = _shift_rows(dyc, dy_next, 1, tm, False)
        d2 = _shift_rows(dyc, dy_next, 2, tm, False)
        du = w_ref[2:3, :] * dyc + w_ref[1:2, :] * d1 + w_ref[0:1, :] * d2
        dbch_ref[:, 0:D_CONV] = (dco * y).astype(BF16)
        dbch_ref[:, D_CONV:2 * D_CONV] = (du * hh).astype(BF16)
        dbch_ref[:, 2 * D_CONV:] = (du * cc).astype(BF16)
        dw_ref[0:1, :] += _rowsum(dyc * u2)
        dw_ref[1:2, :] += _rowsum(dyc * u1)
        dw_ref[2:3, :] += _rowsum(dyc * u)

    row = lambda n, col=0: pl.BlockSpec((tm, n), lambda i: (i, col))
    prev = lambda col: pl.BlockSpec((HALO, D_CONV), lambda i: (jnp.maximum(i * hb - 1, 0), col))
    nxt = lambda col: pl.BlockSpec((HALO, D_CONV), lambda i: (jnp.minimum((i + 1) * hb, T // HALO - 1), col))
    return pl.pallas_call(
        body, name="conv_bwd", grid=(nt,),
        in_specs=[row(D_CONV), nxt(0), row(D_CONV, 0), nxt(0), row(D_CONV, 1), row(D_CONV, 2), prev(1), prev(2),
                  _resident((3, D_CONV))],
        out_specs=[row(3 * D_CONV), _resident((8, D_CONV))],
        out_shape=[jax.ShapeDtypeStruct((T, 3 * D_CONV), BF16), jax.ShapeDtypeStruct((8, D_CONV), F32)],
        compiler_params=_cp(1),
    )(dco, dco, bchf, bchf, bchf, bchf, bchf, bchf, conv_w)


def _mixer_in_bwd(dx1a, dqt, dkt, dvt, dbch, dfl, w_qkvt, w_bch, w_f):
    T = dx1a.shape[0]
    tm = _tile(T, 512)

    def body(a_ref, dq_ref, dk_ref, dv_ref, db_ref, df_ref, wq_ref, wb_ref, wf_ref, o_ref):
        acc = a_ref[...] + _nt(db_ref[...], wb_ref[...]) + _nt(df_ref[...], wf_ref[...])
        for n, ref in enumerate((dq_ref, dk_ref, dv_ref)):
            acc = acc + _tn(ref[...].astype(BF16), wq_ref[n * D_ATTN:(n + 1) * D_ATTN, :])
        o_ref[...] = acc

    row = lambda n: pl.BlockSpec((tm, n), lambda i: (i, 0))
    col = pl.BlockSpec((D_ATTN, tm), lambda i: (0, i))
    return pl.pallas_call(
        body, name="mixer_in_bwd", grid=(T // tm,),
        in_specs=[row(D_MODEL), col, col, col, row(3 * D_CONV), row(N_FLOG),
                  _resident((3 * D_ATTN, D_MODEL)), _resident((D_MODEL, 3 * D_CONV)), _resident((D_MODEL, N_FLOG))],
        out_specs=row(D_MODEL),
        out_shape=jax.ShapeDtypeStruct((T, D_MODEL), F32),
        compiler_params=_cp(1),
    )(dx1a, dqt, dkt, dvt, dbch, dfl, w_qkvt, w_bch, w_f)


def _ple_loss(x3, p, tgt, w_g, w_p, b_g, lg, lb):
    T = x3.shape[0]
    tm = _tile(T, 512)

    def body(x_ref, p_ref, t_ref, wg_ref, wp_ref, bg_ref, lg_ref, lb_ref,
             dx_ref, de_ref, dz_ref, loss_ref, dlg_ref, dlb_ref, dbg_ref):
        @pl.when(pl.program_id(0) == 0)
        def _():
            for ref in (loss_ref, dlg_ref, dlb_ref, dbg_ref):
                ref[...] = jnp.zeros_like(ref)

        xf = x_ref[...]
        gate = _sigmoid(_nn(xf.astype(BF16), wg_ref[...]) + bg_ref[...])
        e = _nn(p_ref[...].astype(BF16), wp_ref[...])
        xhat, rstd = _ln_stats(ALPHA * xf + gate * e)
        err = xhat * lg_ref[...] + lb_ref[...] - t_ref[...]
        sq = jnp.sum(_rowsum(err * err), axis=-1, keepdims=True)
        loss_ref[...] += jnp.broadcast_to(sq * (0.5 / D_MODEL), loss_ref.shape)
        dy = err * (1.0 / D_MODEL)
        dr = _ln_bwd(dy, xhat, rstd, lg_ref[...])
        dlg_ref[...] += _rowsum(dy * xhat)
        dlb_ref[...] += _rowsum(dy)
        de_ref[...] = (dr * gate).astype(BF16)
        dz = dr * e * gate * (1.0 - gate)
        dbg_ref[...] += _rowsum(dz)
        dzb = dz.astype(BF16)
        dz_ref[...] = dzb
        dx_ref[...] = ALPHA * dr + _nt(dzb, wg_ref[...])

    row = lambda n: pl.BlockSpec((tm, n), lambda i: (i, 0))
    vec = lambda n: _resident((1, n))
    return pl.pallas_call(
        body, name="ple_loss", grid=(T // tm,),
        in_specs=[row(D_MODEL), row(PLE_DIM), row(D_MODEL), _resident((D_MODEL, D_MODEL)),
                  _resident((PLE_DIM, D_MODEL)), vec(D_MODEL), vec(D_MODEL), vec(D_MODEL)],
        out_specs=[row(D_MODEL), row(D_MODEL), row(D_MODEL), vec(LANE), vec(D_MODEL), vec(D_MODEL), vec(D_MODEL)],
        out_shape=[jax.ShapeDtypeStruct((T, D_MODEL), F32), jax.ShapeDtypeStruct((T, D_MODEL), BF16),
                   jax.ShapeDtypeStruct((T, D_MODEL), BF16), jax.ShapeDtypeStruct((1, LANE), F32),
                   jax.ShapeDtypeStruct((1, D_MODEL), F32), jax.ShapeDtypeStruct((1, D_MODEL), F32),
                   jax.ShapeDtypeStruct((1, D_MODEL), F32)],
        compiler_params=_cp(1),
    )(x3, p, tgt, w_g, w_p, b_g, lg, lb)


def _lane_layout(v8):
    return jnp.repeat(v8, HEAD_DIM, axis=1)


def _row_layout(v8):
    t = v8.shape[0]
    return jnp.pad(v8.T.reshape(N_HEADS // 2, 2, t), ((0, 0), (0, 6), (0, 0)))


def _from_lane_layout(vl):
    return vl[:, ::HEAD_DIM]


def _from_row_layout(vr):
    return vr[:, :2, :].reshape(N_HEADS, -1).T


def _local_step(x, p, tgt, w, overlap=None):
    bf = lambda a: a.astype(BF16)
    w1i, w1o = bf(w["ffn1_w_in"]), bf(w["ffn1_w_out"])
    first = _ffn_fwd(x, w1i, w1o, w["ln1_g"], w["ln1_b"], "ffn1_fwd", overlap["gather"] if overlap else ())
    x1, r1, g1, u1, h1 = first[:5]
    if overlap:
        w = {**w, **overlap["weights"](first[5:])}
    w2i, w2o = bf(w["ffn2_w_in"]), bf(w["ffn2_w_out"])
    wmi = w["w_mix_in"]
    o_f = 3 * D_ATTN
    o_b = o_f + N_HEADS
    w_qkv = bf(wmi[:, :o_f])
    w_f = bf(jnp.pad(wmi[:, o_f:o_b], ((0, 0), (0, N_FLOG - N_HEADS))))
    w_bch = bf(wmi[:, o_b:])
    w_bchf = jnp.concatenate([w_bch, w_f], axis=1)
    w_mo, w_g, w_p = bf(w["w_mix_out"]), bf(w["w_ple_gate"]), bf(w["w_ple"])
    b_f = jnp.pad(w["b_forget"], ((0, 0), (0, N_FLOG - N_HEADS)))

    q_scale = jnp.concatenate([jnp.full((1, D_ATTN), Q_SCALE * LOG2E, F32), jnp.ones((1, 2 * D_ATTN), F32)], axis=1)
    qkv, qkvt = _matmul_nn(x1, w_qkv, q_scale, BF16, "proj_qkv", also_transposed=True)
    bchf = _matmul_nn(x1, w_bchf, jnp.ones((1, 3 * D_CONV + N_FLOG), F32), F32, "proj_bchf")
    fcol = 3 * D_CONV // N_FLOG
    c = _forget_cumsum(bchf, fcol, b_f)
    o, alrow = _attn_fwd(qkv, qkvt, _fold_key_bias(qkv, c))
    x2, r2, merged = _mixer_tail_fwd(o, bchf, w["conv_w"], w["g_attn"], w["g_conv"], w_mo, x1, w["ln2_g"], w["ln2_b"])
    x3, r3, g2, u2, h2 = _ffn_fwd(x2, w2i, w2o, w["ln3_g"], w["ln3_b"], "ffn2_fwd")

    grads = {}
    dx3, de, dz, loss, grads["ln4_g"], grads["ln4_b"], grads["b_ple_gate"] = _ple_loss(
        x3, p, tgt, w_g, w_p, w["b_ple_gate"], w["ln4_g"], w["ln4_b"])
    by_chip = overlap is not None
    grads["w_ple"] = _matmul_tn(p, de, "dw_ple")
    grads["w_ple_gate"] = _matmul_tn(x3, dz, "dw_ple_gate")

    dx2, dgu2, df2, grads["ln3_g"], grads["ln3_b"] = _ffn_bwd(dx3, r3, g2, u2, w2i, w2o, w["ln3_g"], "ffn2_bwd")
    grads["ffn2_w_in"] = _matmul_tn(x2, dgu2, "dw_ffn2_in", by_chip)
    grads["ffn2_w_out"] = _matmul_tn(h2, df2, "dw_ffn2_out")

    to_swap = overlap["swap"](grads) if overlap else ()
    tail = _mixer_tail_bwd(dx2, r2, w["ln2_g"], w_mo, o, bchf, w["conv_w"], w["g_attn"], w["g_conv"], to_swap)
    (dx1a, dr2, dob, dobt, delta, dco, grads["ln2_g"], grads["ln2_b"], grads["g_attn"], grads["g_conv"]) = tail[:10]
    grads["w_mix_out"] = _matmul_tn(merged, dr2, "dw_mix_out")
    dbch, dcw = _conv_bwd(dco, bchf, w["conv_w"])
    grads["conv_w"] = dcw[:3]
    dqt, dkt, dvt, dck, dcq = _attn_bwd(qkv, qkvt, dob, dobt, c, alrow, delta.reshape(N_HEADS // 2, 8, -1))
    dcq_lanes = jnp.pad(_from_row_layout(dcq), ((0, 0), (0, N_FLOG - N_HEADS)))
    dfl, dbf = _forget_bwd(dck, dcq_lanes, bchf, fcol, b_f)
    grads["b_forget"] = dbf[:, :N_HEADS]
    dx1 = _mixer_in_bwd(dx1a, dqt, dkt, dvt, dbch, dfl, w_qkv.T, w_bch, w_f)
    grads["w_mix_in"] = jnp.concatenate(
        [_matmul_tokens(dqt, x1, "dw_q").T, _matmul_tokens(dkt, x1, "dw_k").T, _matmul_tokens(dvt, x1, "dw_v").T,
         _matmul_tn(x1, dfl, "dw_flog")[:, :N_HEADS], _matmul_tn(x1, dbch, "dw_bch")], axis=1)

    dx0, dgu1, df1, grads["ln1_g"], grads["ln1_b"] = _ffn_bwd(dx1, r1, g1, u1, w1i, w1o, w["ln1_g"], "ffn1_bwd")
    grads["ffn1_w_out"] = _matmul_tn(h1, df1, "dw_ffn1_out")
    if not overlap:
        grads["ffn1_w_in"] = _matmul_tn(x, dgu1, "dw_ffn1_in")
        return loss, dx0, grads
    sums = overlap["chip_sums"](grads, to_swap, tail[10:])
    grads["ffn1_w_in"], *received = _matmul_tn(x, dgu1, "dw_ffn1_in", by_chip, exchange=sums)
    return loss, dx0, grads, sums, received


WEIGHTS = ["ffn1_w_in", "ffn1_w_out", "ln1_g", "ln1_b", "w_mix_in", "b_forget", "conv_w", "g_attn", "g_conv",
           "w_mix_out", "ln2_g", "ln2_b", "ffn2_w_in", "ffn2_w_out", "ln3_g", "ln3_b", "w_ple", "w_ple_gate",
           "b_ple_gate", "ln4_g", "ln4_b"]
LAYOUT = {
    "ffn1_w_in": ((D_MODEL, 2 * D_FF), 1), "ffn1_w_out": ((D_FF, D_MODEL), 0),
    "w_mix_in": ((D_MODEL, 3 * D_ATTN + N_HEADS + 3 * D_CONV), 1), "conv_w": ((3, D_CONV), 1),
    "w_mix_out": ((D_MODEL, D_MODEL), 0), "ffn2_w_in": ((D_MODEL, 2 * D_FF), 1), "ffn2_w_out": ((D_FF, D_MODEL), 0),
    "w_ple": ((PLE_DIM, D_MODEL), 1), "w_ple_gate": ((D_MODEL, D_MODEL), 0),
    "ln1_g": ((1, D_MODEL), None), "ln1_b": ((1, D_MODEL), None), "b_forget": ((1, N_HEADS), None),
    "g_attn": ((1, D_ATTN), None), "g_conv": ((1, D_CONV), None), "ln2_g": ((1, D_MODEL), None),
    "ln2_b": ((1, D_MODEL), None), "ln3_g": ((1, D_MODEL), None), "ln3_b": ((1, D_MODEL), None),
    "b_ple_gate": ((1, D_MODEL), None), "ln4_g": ((1, D_MODEL), None), "ln4_b": ((1, D_MODEL), None),
}
BIG = [n for n in WEIGHTS if LAYOUT[n][1] is not None and n != "conv_w"]
SMALL = [n for n in WEIGHTS if n not in BIG]
ROW = 1024
SMALL_ROWS = 16


def _shard_shape(name):
    shape, axis = LAYOUT[name]
    if axis is None:
        return shape
    return tuple(s // N_CHIPS if a == axis else s for a, s in enumerate(shape))


def _halves(a):
    return a.reshape(a.shape[:-2] + (2, a.shape[-2] // 2, a.shape[-1]))


def _split_chips(name, full):
    shape, axis = LAYOUT[name]
    if axis == 0:
        return full.reshape((N_CHIPS, shape[0] // N_CHIPS) + shape[1:])
    return jnp.moveaxis(full.reshape(shape[:1] + (N_CHIPS, shape[1] // N_CHIPS)), 1, 0)


def _join_chips(name, parts):
    shape, axis = LAYOUT[name]
    if axis == 0:
        return parts.reshape(shape)
    return jnp.moveaxis(parts, 0, 1).reshape(shape)


SMALL_AT = {"ln1_g": (0, 0), "ln1_b": (1, 0), "ln2_g": (2, 0), "ln2_b": (3, 0), "ln3_g": (4, 0), "ln3_b": (5, 0),
            "b_ple_gate": (6, 0), "ln4_g": (7, 0), "ln4_b": (8, 0), "g_attn": (9, 0), "g_conv": (9, D_ATTN),
            "b_forget": (10, 0), "conv_w": (10, LANE)}
CONV_SHARD = D_CONV // N_CHIPS


def _pack_small_grads(grads):
    def body(*refs):
        ins, o_ref = dict(zip(SMALL, refs[:-1])), refs[-1]
        o_ref[...] = jnp.zeros_like(o_ref)
        for s in range(N_CHIPS):
            for n in SMALL:
                r, c0 = SMALL_AT[n]
                if n == "conv_w":
                    for k in range(3):
                        o_ref[s, r:r + 1, c0 + k * CONV_SHARD:c0 + (k + 1) * CONV_SHARD] = (
                            ins[n][k:k + 1, s * CONV_SHARD:(s + 1) * CONV_SHARD])
                else:
                    o_ref[s, r:r + 1, c0:c0 + ins[n].shape[1]] = ins[n][...]

    return pl.pallas_call(
        body, name="pack_small_grads",
        out_shape=jax.ShapeDtypeStruct((N_CHIPS, SMALL_ROWS, ROW), F32),
    )(*[grads[n] for n in SMALL])


def _adamw_math(w, g, m, v):
    c1 = 1.0 - ADAM_B1 ** ADAM_STEP
    c2 = 1.0 - ADAM_B2 ** ADAM_STEP
    m = ADAM_B1 * m + (1.0 - ADAM_B1) * g
    v = ADAM_B2 * v + (1.0 - ADAM_B2) * (g * g)
    return -ADAM_LR * ((m / c1) / (jnp.sqrt(v / c2) + ADAM_EPS) + ADAM_WD * w), m, v


def _adamw_small(g_mine, g_sib, c_idx, w, m, v):
    ns = len(SMALL)

    def body(c_ref, gm_ref, gs_ref, *refs):
        ws, ms, vs = refs[:ns], refs[ns:2 * ns], refs[2 * ns:3 * ns]
        outs = refs[3 * ns:]
        mine_first = c_ref[0] == 0
        top = jnp.where(mine_first, gm_ref[...], gs_ref[...])
        bot = jnp.where(mine_first, gs_ref[...], gm_ref[...])
        for i, n in enumerate(SMALL):
            r, c0 = SMALL_AT[n]
            blk, rr = (top, r) if r < SMALL_ROWS // 2 else (bot, r - SMALL_ROWS // 2)
            rows, width = ws[i].shape
            for k in range(rows):
                g = blk[rr:rr + 1, c0 + k * width:c0 + (k + 1) * width]
                d, mn, vn = _adamw_math(ws[i][k:k + 1, :], g, ms[i][k:k + 1, :], vs[i][k:k + 1, :])
                for q, val in enumerate((g, d, mn, vn)):
                    outs[q * ns + i][k:k + 1, :] = val

    shapes = [jax.ShapeDtypeStruct(a.shape, F32) for a in w]
    vmem = pl.BlockSpec(memory_space=pltpu.VMEM)
    res = pl.pallas_call(
        body, name="adamw_small",
        in_specs=[pl.BlockSpec(memory_space=pltpu.SMEM)] + [vmem] * (2 + 3 * ns),
        out_specs=[vmem] * (4 * ns),
        out_shape=shapes * 4,
    )(c_idx, g_mine, g_sib, *w, *m, *v)
    return [res[q * ns:(q + 1) * ns] for q in range(4)]


def _place():
    x, y, c = lax.axis_index("x"), lax.axis_index("y"), lax.axis_index("c")
    others = [(1 - x, y), (x, 1 - y), (1 - x, 1 - y)]
    return x, y, c, others


ANY_SPEC = pl.BlockSpec(memory_space=pl.ANY)


def _remote(src, dst, send_sems, recv_sems, k, to):
    return pltpu.make_async_remote_copy(src_ref=src, dst_ref=dst, send_sem=send_sems.at[k], recv_sem=recv_sems.at[k],
                                        device_id=to, device_id_type=MESH)


def _all_gather(shards):
    n = len(shards)

    def body(*refs):
        ins, outs, send_sems, recv_sems = refs[:n], refs[n:2 * n], refs[2 * n], refs[2 * n + 1]
        _gather_start(ins, outs, send_sems, recv_sems)
        _gather_finish(ins, outs, send_sems, recv_sems)

    return pl.pallas_call(
        body, name="all_gather_weights",
        out_shape=_gather_shapes(shards), in_specs=[ANY_SPEC] * n, out_specs=[ANY_SPEC] * n,
        scratch_shapes=_gather_sems(n),
    )(*shards)


def _gather_shapes(shards):
    return [jax.ShapeDtypeStruct((N_CHIPS,) + a.shape, a.dtype) for a in shards]


def _gather_sems(n):
    return [pltpu.SemaphoreType.DMA((6 * n,)), pltpu.SemaphoreType.DMA((6 * n,))]


def _gather_sends(ins, outs, send_sems, recv_sems):
    x, y, c, others = _place()
    s = 2 * x + y
    return [_remote(ins[t].at[c], outs[t].at[s, c], send_sems, recv_sems, 6 * t + j, (*chip, c))
            for t in range(len(ins)) for j, chip in enumerate(others)]


def _gather_start(ins, outs, send_sems, recv_sems):
    for cp in _gather_sends(ins, outs, send_sems, recv_sems):
        cp.start()


def _gather_finish(ins, outs, send_sems, recv_sems):
    x, y, c, others = _place()
    slot = lambda t, chip, half: outs[t].at[2 * chip[0] + chip[1], half]
    passed = []
    for t in range(len(ins)):
        for j, chip in enumerate(others):
            landed = slot(t, chip, c)
            _remote(landed, landed, send_sems, recv_sems, 6 * t + j, (x, y, c)).wait_recv()
            passed.append(_remote(landed, landed, send_sems, recv_sems, 6 * t + 3 + j, (x, y, 1 - c)))
            passed[-1].start()
    for t in range(len(ins)):
        for j, chip in enumerate(others):
            landed = slot(t, chip, 1 - c)
            _remote(landed, landed, send_sems, recv_sems, 6 * t + 3 + j, (x, y, c)).wait_recv()
    for cp in _gather_sends(ins, outs, send_sems, recv_sems) + passed:
        cp.wait_send()


def _swap_halves(gs, tag):
    n = len(gs)

    def body(*refs):
        ins, outs, send_sems, recv_sems = refs[:n], refs[n:2 * n], refs[2 * n], refs[2 * n + 1]
        _swap_start(ins, outs, send_sems, recv_sems)
        _swap_finish(ins, outs, send_sems, recv_sems)

    return pl.pallas_call(
        body, name="grad_swap_halves_" + tag,
        out_shape=_swap_shapes(gs), in_specs=[ANY_SPEC] * n, out_specs=[ANY_SPEC] * n,
        scratch_shapes=_swap_sems(n),
    )(*gs)


def _swap_shapes(gs):
    return [jax.ShapeDtypeStruct(g.shape[:1] + g.shape[2:], g.dtype) for g in gs]


def _swap_sems(n):
    return [pltpu.SemaphoreType.DMA((n,)), pltpu.SemaphoreType.DMA((n,))]


def _swap_copies(ins, outs, send_sems, recv_sems):
    x, y, c, _ = _place()
    return [_remote(ins[t].at[:, 1 - c], outs[t], send_sems, recv_sems, t, (x, y, 1 - c)) for t in range(len(ins))]


def _swap_start(ins, outs, send_sems, recv_sems):
    for cp in _swap_copies(ins, outs, send_sems, recv_sems):
        cp.start()


def _swap_finish(ins, outs, send_sems, recv_sems):
    for cp in _swap_copies(ins, outs, send_sems, recv_sems):
        cp.wait()


def _exchange_chips(pps):
    n = len(pps)

    def body(*refs):
        ins, outs, send_sems, recv_sems = refs[:n], refs[n:2 * n], refs[2 * n], refs[2 * n + 1]
        _exchange_start(ins, outs, send_sems, recv_sems)
        _exchange_finish(ins, outs, send_sems, recv_sems)

    return pl.pallas_call(
        body, name="grad_exchange_chips",
        out_shape=_exchange_shapes(pps), in_specs=[ANY_SPEC] * n, out_specs=[ANY_SPEC] * n,
        scratch_shapes=_exchange_sems(n),
    )(*pps)


def _exchange_shapes(pps):
    return [jax.ShapeDtypeStruct(p.shape, p.dtype) for p in pps]


def _exchange_sems(n):
    return [pltpu.SemaphoreType.DMA((3 * n,)), pltpu.SemaphoreType.DMA((3 * n,))]


def _exchange_sends(ins, outs, send_sems, recv_sems):
    x, y, c, others = _place()
    s = 2 * x + y
    return [_remote(ins[t].at[2 * chip[0] + chip[1]], outs[t].at[s], send_sems, recv_sems, 3 * t + j, (*chip, c))
            for t in range(len(ins)) for j, chip in enumerate(others)]


def _exchange_start(ins, outs, send_sems, recv_sems):
    for cp in _exchange_sends(ins, outs, send_sems, recv_sems):
        cp.start()


def _exchange_finish(ins, outs, send_sems, recv_sems):
    x, y, c, others = _place()
    for t in range(len(ins)):
        for j, chip in enumerate(others):
            landed = outs[t].at[2 * chip[0] + chip[1]]
            _remote(landed, landed, send_sems, recv_sems, 3 * t + j, (x, y, c)).wait_recv()
    for cp in _exchange_sends(ins, outs, send_sems, recv_sems):
        cp.wait_send()


def _share_half(rs):
    n = len(rs)

    def body(*refs):
        ins, outs, send_sems, recv_sems = refs[:n], refs[n:2 * n], refs[2 * n], refs[2 * n + 1]
        x, y, c, _ = _place()
        copies = [_remote(ins[t], outs[t], send_sems, recv_sems, t, (x, y, 1 - c)) for t in range(n)]
        for cp in copies:
            cp.start()
        for cp in copies:
            cp.wait()

    return pl.pallas_call(
        body, name="grad_share_half",
        out_shape=[jax.ShapeDtypeStruct(r.shape, r.dtype) for r in rs],
        in_specs=[ANY_SPEC] * n, out_specs=[ANY_SPEC] * n,
        scratch_shapes=[pltpu.SemaphoreType.DMA((n,)), pltpu.SemaphoreType.DMA((n,))],
    )(*rs)


ELEMENTWISE_BLOCK_BYTES = 1 << 20


def _row_tile(rows, cols):
    return _tile(rows, max(8, ELEMENTWISE_BLOCK_BYTES // (4 * cols) // 8 * 8))


def _add_my_half(g, sib, c_idx, name):
    rh, cols = g.shape[2:]
    tr = _row_tile(rh, cols)

    def body(c_ref, g_ref, s_ref, o_ref):
        o_ref[...] = (g_ref[...] + s_ref[...]).astype(BF16)

    return pl.pallas_call(
        body, name="grad_add_halves_" + name,
        grid_spec=pltpu.PrefetchScalarGridSpec(
            num_scalar_prefetch=1, grid=(N_CHIPS, rh // tr),
            in_specs=[pl.BlockSpec((None, None, tr, cols), lambda s, i, c: (s, c[0], i, 0)),
                      pl.BlockSpec((None, tr, cols), lambda s, i, c: (s, i, 0))],
            out_specs=pl.BlockSpec((None, tr, cols), lambda s, i, c: (s, i, 0))),
        out_shape=jax.ShapeDtypeStruct((N_CHIPS, rh, cols), BF16),
        compiler_params=_cp(2),
    )(c_idx, g, sib)


def _sum_chips(parts, pp, s_idx, name):
    rh, cols = parts.shape[1:]
    tr = _row_tile(rh, cols)

    def body(s_ref, p0, p1, p2, p3, mine_ref, o_ref):
        own = mine_ref[...]
        t = [jnp.where(s_ref[0] == k, own, p[...]).astype(F32) for k, p in enumerate((p0, p1, p2, p3))]
        o_ref[...] = ((t[0] + t[1]) + t[2]) + t[3]

    slot = lambda k: pl.BlockSpec((None, tr, cols), lambda i, s: (jnp.where(s[0] == k, (k + 1) % N_CHIPS, k), i, 0))
    return pl.pallas_call(
        body, name="grad_sum_chips_" + name,
        grid_spec=pltpu.PrefetchScalarGridSpec(
            num_scalar_prefetch=1, grid=(rh // tr,),
            in_specs=[slot(0), slot(1), slot(2), slot(3), pl.BlockSpec((None, tr, cols), lambda i, s: (s[0], i, 0))],
            out_specs=pl.BlockSpec((tr, cols), lambda i, s: (i, 0))),
        out_shape=jax.ShapeDtypeStruct((rh, cols), F32),
        compiler_params=_cp(1),
    )(s_idx, parts, parts, parts, parts, pp)


def _adamw(w, g_mine, g_sib, m, v, c_idx, name):
    rows, cols = w.shape
    tr = _row_tile(rows // 2, cols)
    nbh = rows // 2 // tr

    def body(c_ref, w_ref, gm_ref, gs_ref, m_ref, v_ref, g_ref, d_ref, mo_ref, vo_ref):
        g = jnp.where(pl.program_id(0) // nbh == c_ref[0], gm_ref[...], gs_ref[...])
        g_ref[...] = g
        d_ref[...], mo_ref[...], vo_ref[...] = _adamw_math(w_ref[...], g, m_ref[...], v_ref[...])

    spec = pl.BlockSpec((tr, cols), lambda i, c: (i, 0))
    half = pl.BlockSpec((tr, cols), lambda i, c: (i % nbh, 0))
    return pl.pallas_call(
        body, name="adamw_" + name,
        grid_spec=pltpu.PrefetchScalarGridSpec(
            num_scalar_prefetch=1, grid=(rows // tr,),
            in_specs=[spec, half, half, spec, spec], out_specs=[spec] * 4),
        out_shape=[jax.ShapeDtypeStruct(w.shape, F32)] * 4,
        compiler_params=_cp(1),
    )(c_idx, w, g_mine, g_sib, m, v)


def kernel(x, p, ffn1_w_in, ffn1_w_out, ln1_g, ln1_b, w_mix_in, b_forget, conv_w, g_attn, g_conv, w_mix_out, ln2_g, ln2_b, ffn2_w_in, ffn2_w_out, ln3_g, ln3_b, w_ple, w_ple_gate, b_ple_gate, ln4_g, ln4_b, loss_target, m_ffn1_w_in, m_ffn1_w_out, m_ln1_g, m_ln1_b, m_w_mix_in, m_b_forget, m_conv_w, m_g_attn, m_g_conv, m_w_mix_out, m_ln2_g, m_ln2_b, m_ffn2_w_in, m_ffn2_w_out, m_ln3_g, m_ln3_b, m_w_ple, m_w_ple_gate, m_b_ple_gate, m_ln4_g, m_ln4_b, v_ffn1_w_in, v_ffn1_w_out, v_ln1_g, v_ln1_b, v_w_mix_in, v_b_forget, v_conv_w, v_g_attn, v_g_conv, v_w_mix_out, v_ln2_g, v_ln2_b, v_ffn2_w_in, v_ffn2_w_out, v_ln3_g, v_ln3_b, v_w_ple, v_w_ple_gate, v_b_ple_gate, v_ln4_g, v_ln4_b):
    args = dict(locals())
    shard = {n: args[n][0] if LAYOUT[n][1] is not None else args[n] for n in WEIGHTS}
    m_shard = {n: args["m_" + n][0] if LAYOUT[n][1] is not None else args["m_" + n] for n in WEIGHTS}
    v_shard = {n: args["v_" + n][0] if LAYOUT[n][1] is not None else args["v_" + n] for n in WEIGHTS}
    c_idx = lax.axis_index("c").astype(jnp.int32).reshape(1)
    chip = (2 * lax.axis_index("x") + lax.axis_index("y")).astype(jnp.int32)

    conv_rows = SMALL_ROWS - shard["conv_w"].shape[0]
    mine = {n: _halves(shard[n].astype(BF16)) for n in BIG}
    mine["conv_w"] = _halves(jnp.pad(shard["conv_w"], ((0, conv_rows), (0, 0))))
    early_w = ["ffn1_w_in", "ffn1_w_out"]
    late_w = [n for n in BIG if n not in early_w] + ["conv_w"]

    def full_weights(names, gathered):
        out = {}
        for n, theirs in zip(names, gathered):
            g = lax.dynamic_update_slice(theirs, mine[n][None], (chip, 0, 0, 0))
            if n == "conv_w":
                out[n] = _join_chips(n, g.reshape(N_CHIPS, SMALL_ROWS, CONV_SHARD)[:, :3])
            else:
                out[n] = _join_chips(n, g.reshape((N_CHIPS,) + _shard_shape(n)))
        return out

    full = full_weights(early_w, _all_gather([mine[n] for n in early_w]))
    full.update({n: shard[n] for n in SMALL if n != "conv_w"})

    def per_chip(names, grads):
        by_chip = lambda n: grads[n] if grads[n].ndim == 3 else _split_chips(n, grads[n])
        return [_halves(_pack_small_grads(grads) if n == "small" else by_chip(n)) for n in names]

    def add_halves(names, mine_, sibs):
        return [_add_my_half(g, sib, c_idx, n) for n, g, sib in zip(names, mine_, sibs)]

    def chip_sums(names, grads):
        mine_ = per_chip(names, grads)
        return add_halves(names, mine_, _swap_halves(mine_, names[0]))

    ready_a = ["ffn2_w_in", "ffn2_w_out", "w_ple", "w_ple_gate"]
    ready_b = ["w_mix_in", "w_mix_out"]
    early_g = ready_a + ready_b
    late_g = early_w + ["small"]
    loss_acc, grad_x, grads, early_sums, early_parts = _local_step(
        x[0], p[0, 0], loss_target[0], full,
        overlap={"gather": [mine[n] for n in late_w], "weights": lambda gathered: full_weights(late_w, gathered),
                 "swap": lambda grads: per_chip(ready_a, grads),
                 "chip_sums": lambda grads, swapped, received: (add_halves(ready_a, swapped, received)
                                                                + chip_sums(ready_b, grads))})
    loss = lax.psum(loss_acc[0, 0], ("x", "y", "c"))

    late_sums = chip_sums(late_g, grads)
    names = early_g + late_g
    sums = list(early_sums) + late_sums
    parts = list(early_parts) + list(_exchange_chips(late_sums))
    half_of = {n: _sum_chips(pt, own, chip.reshape(1), n) for n, pt, own in zip(names, parts, sums)}
    names = BIG + ["small"]
    my_half = [half_of[n] for n in names]
    sib_half = _share_half(my_half)

    out = {}
    for n, gm, gs in zip(BIG, my_half, sib_half):
        out[n] = [a[None] for a in _adamw(shard[n], gm, gs, m_shard[n], v_shard[n], c_idx, n)]
    small = _adamw_small(my_half[-1], sib_half[-1], c_idx, [shard[n] for n in SMALL], [m_shard[n] for n in SMALL],
                         [v_shard[n] for n in SMALL])
    for i, n in enumerate(SMALL):
        out[n] = [small[q][i][None] if n == "conv_w" else small[q][i] for q in range(4)]
    return (loss, grad_x[None], *[out[n][q] for q in range(4) for n in WEIGHTS])
```

```python
import math

import jax
import jax.numpy as jnp
from jax import lax
from jax.experimental import pallas as pl
from jax.experimental.pallas import tpu as pltpu

F32 = jnp.float32
BF16 = jnp.bfloat16

D_MODEL = 1024
D_FF = 2816
N_HEADS = 8
HEAD_DIM = 64
D_ATTN = N_HEADS * HEAD_DIM
D_CONV = 512
PLE_DIM = 256
N_FLOG = 128
ALPHA = 2.0 ** 0.25
LN_EPS = 1e-5
RMS_EPS = 1e-6
NEG_INF = -1e30
Q_SCALE = 1.0 / math.sqrt(HEAD_DIM)
LOG2E = math.log2(math.e)

ADAM_LR = 0.001
ADAM_B1 = 0.9
ADAM_B2 = 0.999
ADAM_EPS = 1e-08
ADAM_WD = 0.01
ADAM_STEP = 10

V7X_VMEM_BYTES = 64 << 20
VMEM_LIMIT = V7X_VMEM_BYTES - (8 << 20)
LANE = 128
FF_CHUNK = 256
N_CHIPS = 4
TOKEN_TILE = 512
FFN_BWD_TILE = 256
ATTN_TILE = 512
SCAN_BLOCK = 512
WGRAD_TOKENS = 2048
WGRAD_T_TOKENS = 1024
MESH = pl.DeviceIdType.MESH


def _cp(n_axes):
    return pltpu.CompilerParams(dimension_semantics=("arbitrary",) * n_axes, vmem_limit_bytes=VMEM_LIMIT)


def _resident(shape):
    n = len(shape)
    return pl.BlockSpec(shape, lambda *_: (0,) * n, pipeline_mode=pl.Buffered(1))


def _nn(a, b):
    return jnp.dot(a, b, preferred_element_type=F32)


def _nt(a, b):
    return lax.dot_general(a, b, (((1,), (1,)), ((), ())), preferred_element_type=F32)


def _tn(a, b):
    return lax.dot_general(a, b, (((0,), (0,)), ((), ())), preferred_element_type=F32)


def _ln_stats(r):
    mu = jnp.mean(r, axis=-1, keepdims=True)
    xc = r - mu
    var = jnp.mean(xc * xc, axis=-1, keepdims=True)
    rstd = lax.rsqrt(var + LN_EPS)
    return xc * rstd, rstd


def _ln_bwd(dy, xhat, rstd, g):
    dxh = dy * g
    m1 = jnp.mean(dxh, axis=-1, keepdims=True)
    m2 = jnp.mean(dxh * xhat, axis=-1, keepdims=True)
    return rstd * (dxh - m1 - xhat * m2)


def _sigmoid(z):
    return 1.0 / (1.0 + jnp.exp(-z))


def _rowsum(a):
    return jnp.sum(a, axis=0, keepdims=True)


def _tile(total, want):
    if total <= want:
        return total
    for t in range(want - want % 8, 0, -8):
        if total % t == 0:
            return t
    raise ValueError((total, want))


def _ffn_fwd(x, w_in, w_out, lg, lb, name, gather=()):
    T = x.shape[0]
    tm = _tile(T, TOKEN_TILE)
    nf = D_FF // FF_CHUNK
    ng = len(gather)
    last = T // tm - 1

    def body(x_ref, wi_ref, wo_ref, lg_ref, lb_ref, *rest):
        comm_in, (xo_ref, r_ref, g_ref, u_ref, h_ref) = rest[:ng], rest[ng:ng + 5]
        comm_out, sems = rest[ng + 5:2 * ng + 5], rest[2 * ng + 5:]
        if ng:
            @pl.when(pl.program_id(0) == 0)
            def _():
                _gather_start(comm_in, comm_out, *sems)

        xf = x_ref[...]
        xb = xf.astype(BF16)
        acc = jnp.zeros((tm, D_MODEL), F32)
        for j in range(nf):
            c0 = j * FF_CHUNK
            g = _nn(xb, wi_ref[:, c0:c0 + FF_CHUNK])
            u = _nn(xb, wi_ref[:, D_FF + c0:D_FF + c0 + FF_CHUNK])
            hb = (g * _sigmoid(g) * u).astype(BF16)
            g_ref[:, c0:c0 + FF_CHUNK] = g.astype(BF16)
            u_ref[:, c0:c0 + FF_CHUNK] = u.astype(BF16)
            h_ref[:, c0:c0 + FF_CHUNK] = hb
            acc = acc + _nn(hb, wo_ref[c0:c0 + FF_CHUNK, :])
        r = ALPHA * xf + 0.5 * acc
        r_ref[...] = r
        xhat, _ = _ln_stats(r)
        xo_ref[...] = xhat * lg_ref[...] + lb_ref[...]
        if ng:
            @pl.when(pl.program_id(0) == last)
            def _():
                _gather_finish(comm_in, comm_out, *sems)

    row = lambda n: pl.BlockSpec((tm, n), lambda i: (i, 0))
    return pl.pallas_call(
        body, name=name, grid=(T // tm,),
        in_specs=[row(D_MODEL), _resident((D_MODEL, 2 * D_FF)), _resident((D_FF, D_MODEL)),
                  _resident((1, D_MODEL)), _resident((1, D_MODEL))] + [ANY_SPEC] * ng,
        out_specs=[row(D_MODEL), row(D_MODEL), row(D_FF), row(D_FF), row(D_FF)] + [ANY_SPEC] * ng,
        out_shape=[jax.ShapeDtypeStruct((T, D_MODEL), F32), jax.ShapeDtypeStruct((T, D_MODEL), F32),
                   jax.ShapeDtypeStruct((T, D_FF), BF16), jax.ShapeDtypeStruct((T, D_FF), BF16),
                   jax.ShapeDtypeStruct((T, D_FF), BF16)] + _gather_shapes(gather),
        scratch_shapes=_gather_sems(ng) if ng else [],
        compiler_params=_cp(1),
    )(x, w_in, w_out, lg, lb, *gather)


def _ffn_bwd(dxo, r, g, u, w_in, w_out, lg, name, exchange=()):
    T = r.shape[0]
    tm = _tile(T, FFN_BWD_TILE)
    nf = D_FF // FF_CHUNK
    ne = len(exchange)
    last = T // tm - 1

    def body(dxo_ref, r_ref, g_ref, u_ref, wi_ref, wo_ref, lg_ref, *rest):
        comm_in, (dx_ref, dgu_ref, df_ref, dlg_ref, dlb_ref) = rest[:ne], rest[ne:ne + 5]
        comm_out, sems = rest[ne + 5:2 * ne + 5], rest[2 * ne + 5:]
        i = pl.program_id(0)
        if ne:
            @pl.when(i == 0)
            def _():
                _exchange_start(comm_in, comm_out, *sems)

        dy = dxo_ref[...]
        xhat, rstd = _ln_stats(r_ref[...])
        dr = _ln_bwd(dy, xhat, rstd, lg_ref[...])

        @pl.when(i == 0)
        def _():
            dlg_ref[...] = jnp.zeros_like(dlg_ref)
            dlb_ref[...] = jnp.zeros_like(dlb_ref)

        dlg_ref[...] += _rowsum(dy * xhat)
        dlb_ref[...] += _rowsum(dy)
        dfb = (0.5 * dr).astype(BF16)
        df_ref[...] = dfb
        acc = jnp.zeros((tm, D_MODEL), F32)
        dh_ahead = _nt(dfb, wo_ref[0:FF_CHUNK, :])
        for j in range(nf):
            c0 = j * FF_CHUNK
            dh = dh_ahead
            if j + 1 < nf:
                dh_ahead = _nt(dfb, wo_ref[c0 + FF_CHUNK:c0 + 2 * FF_CHUNK, :])
            gg = g_ref[:, c0:c0 + FF_CHUNK].astype(F32)
            uu = u_ref[:, c0:c0 + FF_CHUNK].astype(F32)
            s = _sigmoid(gg)
            dgb = (dh * uu * s * (1.0 + gg * (1.0 - s))).astype(BF16)
            dub = (dh * gg * s).astype(BF16)
            dgu_ref[:, c0:c0 + FF_CHUNK] = dgb
            dgu_ref[:, D_FF + c0:D_FF + c0 + FF_CHUNK] = dub
            acc = acc + _nt(dgb, wi_ref[:, c0:c0 + FF_CHUNK]) + _nt(dub, wi_ref[:, D_FF + c0:D_FF + c0 + FF_CHUNK])
        dx_ref[...] = ALPHA * dr + acc
        if ne:
            @pl.when(i == last)
            def _():
                _exchange_finish(comm_in, comm_out, *sems)

    row = lambda n: pl.BlockSpec((tm, n), lambda i: (i, 0))
    return pl.pallas_call(
        body, name=name, grid=(T // tm,),
        in_specs=[row(D_MODEL), row(D_MODEL), row(D_FF), row(D_FF), _resident((D_MODEL, 2 * D_FF)),
                  _resident((D_FF, D_MODEL)), _resident((1, D_MODEL))] + [ANY_SPEC] * ne,
        out_specs=[row(D_MODEL), row(2 * D_FF), row(D_MODEL), _resident((1, D_MODEL)), _resident((1, D_MODEL))]
        + [ANY_SPEC] * ne,
        out_shape=[jax.ShapeDtypeStruct((T, D_MODEL), F32), jax.ShapeDtypeStruct((T, 2 * D_FF), BF16),
                   jax.ShapeDtypeStruct((T, D_MODEL), BF16), jax.ShapeDtypeStruct((1, D_MODEL), F32),
                   jax.ShapeDtypeStruct((1, D_MODEL), F32)] + _exchange_shapes(exchange),
        scratch_shapes=_exchange_sems(ne) if ne else [],
        compiler_params=_cp(1),
    )(dxo, r, g, u, w_in, w_out, lg, *exchange)


def _matmul_tn(a, b, name, by_chip=False, exchange=()):
    T, K = a.shape
    N = b.shape[1]
    tt = _tile(T, WGRAD_TOKENS)
    tn = N // N_CHIPS if by_chip else N
    while K * tn * 4 > (6 << 20) and tn % 256 == 0 and not by_chip:
        tn //= 2
    assert N % tn == 0
    ne = len(exchange)
    grid = (N // tn, T // tt)

    def body(a_ref, b_ref, *rest):
        comm_in, o_ref, comm_out, sems = rest[:ne], rest[ne], rest[ne + 1:2 * ne + 1], rest[2 * ne + 1:]
        n, t = pl.program_id(0), pl.program_id(1)
        if ne:
            @pl.when((n == 0) & (t == 0))
            def _():
                _exchange_start(comm_in, comm_out, *sems)

        @pl.when(t == 0)
        def _():
            o_ref[...] = jnp.zeros_like(o_ref)

        o_ref[...] += _tn(a_ref[...].astype(BF16), b_ref[...].astype(BF16))
        if ne:
            @pl.when((n == grid[0] - 1) & (t == grid[1] - 1))
            def _():
                _exchange_finish(comm_in, comm_out, *sems)

    res = pl.pallas_call(
        body, name=name, grid=grid,
        in_specs=[pl.BlockSpec((tt, K), lambda n, t: (t, 0)), pl.BlockSpec((tt, tn), lambda n, t: (t, n))]
        + [ANY_SPEC] * ne,
        out_specs=[pl.BlockSpec((None, K, tn), lambda n, t: (n, 0, 0)) if by_chip
                   else pl.BlockSpec((K, tn), lambda n, t: (0, n))] + [ANY_SPEC] * ne,
        out_shape=[jax.ShapeDtypeStruct((N_CHIPS, K, tn) if by_chip else (K, N), F32)] + _exchange_shapes(exchange),
        scratch_shapes=_exchange_sems(ne) if ne else [],
        compiler_params=_cp(2),
    )(a, b, *exchange)
    return res if ne else res[0]


def _matmul_tokens(at, b, name):
    M, T = at.shape
    N = b.shape[1]
    tt = _tile(T, WGRAD_T_TOKENS)

    def body(a_ref, b_ref, o_ref):
        @pl.when(pl.program_id(0) == 0)
        def _():
            o_ref[...] = jnp.zeros_like(o_ref)

        o_ref[...] += _nn(a_ref[...].astype(BF16), b_ref[...].astype(BF16))

    return pl.pallas_call(
        body, name=name, grid=(T // tt,),
        in_specs=[pl.BlockSpec((M, tt), lambda t: (0, t)), pl.BlockSpec((tt, N), lambda t: (t, 0))],
        out_specs=pl.BlockSpec((M, N), lambda t: (0, 0)),
        out_shape=jax.ShapeDtypeStruct((M, N), F32),
        compiler_params=_cp(1),
    )(at, b)


def _matmul_nn(x, w, scale, out_dtype, name, also_transposed=False):
    T, K = x.shape
    N = w.shape[1]
    tm = _tile(T, TOKEN_TILE)

    def body(x_ref, w_ref, s_ref, o_ref, *ot_ref):
        res = _nn(x_ref[...].astype(BF16), w_ref[...]) * s_ref[...]
        o_ref[...] = res.astype(out_dtype)
        if also_transposed:
            ot_ref[0][...] = res.T.astype(out_dtype)

    res = pl.pallas_call(
        body, name=name, grid=(T // tm,),
        in_specs=[pl.BlockSpec((tm, K), lambda i: (i, 0)), _resident((K, N)), _resident((1, N))],
        out_specs=[pl.BlockSpec((tm, N), lambda i: (i, 0))] + [pl.BlockSpec((N, tm), lambda i: (0, i))] * also_transposed,
        out_shape=[jax.ShapeDtypeStruct((T, N), out_dtype)] + [jax.ShapeDtypeStruct((N, T), out_dtype)] * also_transposed,
        compiler_params=_cp(1),
    )(x, w, scale)
    return res if also_transposed else res[0]


def _log_sigmoid(z):
    return jnp.minimum(z, 0.0) - jnp.log1p(jnp.exp(-jnp.abs(z)))


def _tri(n, lower):
    r = lax.broadcasted_iota(jnp.int32, (n, n), 0)
    c = lax.broadcasted_iota(jnp.int32, (n, n), 1)
    return jnp.where((c <= r) if lower else (c >= r), 1.0, 0.0).astype(F32)


def _f32dot(a, b):
    return jnp.dot(a, b, preferred_element_type=F32, precision=lax.Precision.HIGHEST)


def _forget_cumsum(flog, col, bf):
    T = flog.shape[0]
    bt = _tile(T, SCAN_BLOCK)

    def body(f_ref, b_ref, c_ref, carry):
        @pl.when(pl.program_id(0) == 0)
        def _():
            carry[...] = jnp.zeros_like(carry)

        lf = _log_sigmoid(f_ref[...] + b_ref[...])
        c = _f32dot(_tri(bt, True), lf) + carry[...]
        c_ref[...] = c * LOG2E
        carry[...] = c[bt - 1:bt, :]

    return pl.pallas_call(
        body, name="forget_cumsum", grid=(T // bt,),
        in_specs=[pl.BlockSpec((bt, N_FLOG), lambda i: (i, col)), _resident((1, N_FLOG))],
        out_specs=pl.BlockSpec((bt, N_FLOG), lambda i: (i, 0)),
        out_shape=jax.ShapeDtypeStruct((T, N_FLOG), F32),
        scratch_shapes=[pltpu.VMEM((1, N_FLOG), F32)],
        compiler_params=_cp(1),
    )(flog, bf)


def _forget_bwd(dck, dcq, flog, col, bf):
    T = dcq.shape[0]
    bt = _tile(T, SCAN_BLOCK)
    nb = T // bt

    def body(k0_ref, k1_ref, k2_ref, k3_ref, dcq_ref, f_ref, b_ref, dz_ref, db_ref, carry):
        @pl.when(pl.program_id(0) == 0)
        def _():
            carry[...] = jnp.zeros_like(carry)
            db_ref[...] = jnp.zeros_like(db_ref)

        dc = ((k0_ref[...] + k1_ref[...]) + (k2_ref[...] + k3_ref[...])) + dcq_ref[...]
        dlf = _f32dot(_tri(bt, False), dc) + carry[...]
        carry[...] = dlf[0:1, :]
        z = f_ref[...] + b_ref[...]
        dz = dlf * _sigmoid(-z)
        dz_ref[...] = dz.astype(BF16)
        db_ref[...] += _rowsum(dz)

    slab = lambda j: pl.BlockSpec((None, bt, N_FLOG), lambda i: (j, nb - 1 - i, 0))
    return pl.pallas_call(
        body, name="forget_bwd", grid=(nb,),
        in_specs=[slab(0), slab(1), slab(2), slab(3),
                  pl.BlockSpec((bt, N_FLOG), lambda i: (nb - 1 - i, 0)),
                  pl.BlockSpec((bt, N_FLOG), lambda i: (nb - 1 - i, col)), _resident((1, N_FLOG))],
        out_specs=[pl.BlockSpec((bt, N_FLOG), lambda i: (nb - 1 - i, 0)), _resident((1, N_FLOG))],
        out_shape=[jax.ShapeDtypeStruct((T, N_FLOG), BF16), jax.ShapeDtypeStruct((1, N_FLOG), F32)],
        scratch_shapes=[pltpu.VMEM((1, N_FLOG), F32)],
        compiler_params=_cp(1),
    )(dck, dck, dck, dck, dcq, flog, bf)


def _head_masks():
    lane = lax.broadcasted_iota(jnp.int32, (1, LANE), 1)
    return lane < HEAD_DIM


def _split_heads(x2, is_a):
    zero = jnp.zeros_like(x2)
    return jnp.where(is_a, x2, zero), jnp.where(is_a, zero, x2)


BIAS_PARTS = 3
ATTN_GROUP = 4
ATTN_FWD_GROUP = 8


def _bias_lanes(h):
    lane = lax.broadcasted_iota(jnp.int32, (1, LANE), 1)
    first = (1 - h) * HEAD_DIM
    return lane, first


def _fold_key_bias(qkv, c):
    T = qkv.shape[0]
    tm = _tile(T, TOKEN_TILE)
    npair = N_HEADS // 2

    def body(k_ref, c_ref, o_ref):
        cc = c_ref[...]
        parts, rest = [], cc
        for _ in range(BIAS_PARTS):
            piece = rest.astype(BF16)
            parts.append(piece)
            rest = rest - piece.astype(F32)
        for j in range(npair):
            k2 = k_ref[:, j * LANE:(j + 1) * LANE]
            for h in range(2):
                lane, first = _bias_lanes(h)
                out = k2
                for n, piece in enumerate(parts):
                    col = piece[:, 2 * j + h:2 * j + h + 1]
                    out = jnp.where(lane == first + n, col, out)
                o_ref[:, (2 * j + h) * LANE:(2 * j + h + 1) * LANE] = out

    return pl.pallas_call(
        body, name="fold_key_bias", grid=(T // tm,),
        in_specs=[pl.BlockSpec((tm, D_ATTN), lambda i: (i, 1)), pl.BlockSpec((tm, N_FLOG), lambda i: (i, 0))],
        out_specs=pl.BlockSpec((tm, 2 * D_ATTN), lambda i: (i, 0)),
        out_shape=jax.ShapeDtypeStruct((T, 2 * D_ATTN), BF16),
        compiler_params=_cp(1),
    )(qkv, c)


def _attn_fwd(qkv, vt, kb):
    T = qkv.shape[0]
    tq = _tile(T, ATTN_TILE)
    tk = tq
    nq = T // tq
    npair = N_HEADS // 2

    def body(qt_ref, ka_ref, kb_ref, vt_ref, o_ref, al_ref, m_s, l_s, acc_s):
        i = pl.program_id(1)
        dim = lax.broadcasted_iota(jnp.int32, (LANE, 1), 0)
        qs = []
        for h, qh in enumerate(_split_heads(qt_ref[...], dim < HEAD_DIM)):
            first = (1 - h) * HEAD_DIM
            qs.append(jnp.where((dim >= first) & (dim < first + BIAS_PARTS), -1.0, qh).astype(BF16))
        k_refs = (ka_ref, kb_ref)
        m_s[...] = jnp.full_like(m_s, NEG_INF)
        l_s[...] = jnp.zeros_like(l_s)
        acc_s[...] = jnp.zeros_like(acc_s)

        def scores_at(kk):
            k0 = pl.multiple_of(kk * tk, tk)
            return tuple(_nn(k_refs[h][pl.ds(k0, tk), :], qs[h]) for h in range(2))

        def consume(kk, scores, masked):
            k0 = pl.multiple_of(kk * tk, tk)
            v2t = vt_ref[:, pl.ds(k0, tk)]
            for h in range(2):
                zt = scores[h]
                if masked:
                    rr = lax.broadcasted_iota(jnp.int32, (tk, tq), 0)
                    cc = lax.broadcasted_iota(jnp.int32, (tk, tq), 1)
                    zt = jnp.where(cc >= rr, zt, NEG_INF)
                m_old = m_s[h]
                m_new = jnp.maximum(m_old, jnp.max(zt, axis=0, keepdims=True))
                p = jnp.exp2(zt - m_new)
                a = jnp.exp2(m_old - m_new)
                l_s[h] = a * l_s[h] + jnp.sum(p, axis=0, keepdims=True)
                acc_s[h] = a * acc_s[h] + _nn(v2t, p.astype(BF16))
                m_s[h] = m_new

        def group(kk, n, last_masked):
            scores = [scores_at(kk + u) for u in range(n)]
            for u in range(n):
                consume(kk + u, scores[u], last_masked and u == n - 1)

        def loop_body(t, carry):
            group(ATTN_FWD_GROUP * t, ATTN_FWD_GROUP, False)
            return carry

        lax.fori_loop(0, i // ATTN_FWD_GROUP, loop_body, 0)
        for left in range(ATTN_FWD_GROUP):
            @pl.when(i % ATTN_FWD_GROUP == left)
            def _():
                group(i - left, left + 1, True)

        outs = []
        for h in range(2):
            l = l_s[h]
            outs.append(acc_s[h] * (1.0 / l))
            al_ref[0, h:h + 1, :] = -(m_s[h] + jnp.log2(l))
        al_ref[0, 2:8, :] = jnp.zeros((6, tq), F32)
        dim = lax.broadcasted_iota(jnp.int32, (LANE, 1), 0)
        o_ref[...] = jnp.where(dim < HEAD_DIM, outs[0], outs[1]).T

    rowl = pl.BlockSpec((1, 8, tq), lambda j, i: (j, 0, i))
    return pl.pallas_call(
        body, name="attn_fwd", grid=(npair, nq),
        in_specs=[pl.BlockSpec((LANE, tq), lambda j, i: (j, i)),
                  pl.BlockSpec((T, LANE), lambda j, i: (0, 2 * j), pipeline_mode=pl.Buffered(1)),
                  pl.BlockSpec((T, LANE), lambda j, i: (0, 2 * j + 1), pipeline_mode=pl.Buffered(1)),
                  pl.BlockSpec((LANE, T), lambda j, i: (2 * npair + j, 0), pipeline_mode=pl.Buffered(1))],
        out_specs=[pl.BlockSpec((tq, LANE), lambda j, i: (i, j)), rowl],
        out_shape=[jax.ShapeDtypeStruct((T, D_ATTN), F32), jax.ShapeDtypeStruct((npair, 8, T), F32)],
        scratch_shapes=[pltpu.VMEM((2, 1, tq), F32), pltpu.VMEM((2, 1, tq), F32), pltpu.VMEM((2, LANE, tq), F32)],
        compiler_params=_cp(2),
    )(vt, kb, kb, vt)


def _attn_bwd(qkv, qkvt, dobt, cb, alrow, dlrow):
    T = qkv.shape[0]
    tq = _tile(T, ATTN_TILE)
    tk = tq
    nq = T // tq
    npair = N_HEADS // 2

    def body(qt_ref, k_ref, kt_ref, v_ref, dot_ref, cb_ref, al_ref, dl_ref,
             dq_ref, dk_ref, dv_ref, dc_ref, dcq_ref, dk_s, dv_s, dc_s):
        kj = pl.program_id(1)
        is_a = _head_masks()
        ks = _split_heads(k_ref[...], is_a)
        vs = _split_heads(v_ref[...], is_a)
        dim_a = lax.broadcasted_iota(jnp.int32, (LANE, 1), 0) < HEAD_DIM
        kts = _split_heads(kt_ref[...], dim_a)
        head_lane = lax.broadcasted_iota(jnp.int32, (1, LANE), 1) - 2 * pl.program_id(0)
        cs = tuple(jnp.sum(jnp.where(head_lane == h, cb_ref[...], 0.0), axis=-1, keepdims=True) for h in range(2))

        @pl.when(kj == 0)
        def _():
            dq_ref[...] = jnp.zeros_like(dq_ref)
            dcq_ref[...] = jnp.zeros_like(dcq_ref)

        dk_s[...] = jnp.zeros_like(dk_s)
        dv_s[...] = jnp.zeros_like(dv_s)
        dc_s[...] = jnp.zeros_like(dc_s)

        def step(qi, masked):
            q0 = pl.multiple_of(qi * tq, tq)
            qt2 = qt_ref[:, pl.ds(q0, tq)]
            dot2 = dot_ref[:, pl.ds(q0, tq)]
            for h in range(2):
                alr = al_ref[0, h:h + 1, pl.ds(q0, tq)]
                dlr = dl_ref[0, h:h + 1, pl.ds(q0, tq)]
                zt = _nn(ks[h], qt2) + (alr - cs[h])
                if masked:
                    rr = lax.broadcasted_iota(jnp.int32, (tk, tq), 0)
                    cc = lax.broadcasted_iota(jnp.int32, (tk, tq), 1)
                    zt = jnp.where(cc >= rr, zt, NEG_INF)
                pt = jnp.exp2(zt)
                dst = pt * (_nn(vs[h], dot2) - dlr)
                pb = pt.astype(BF16)
                dsb = dst.astype(BF16)
                dv_s[h] += _nt(dot2, pb)
                dk_s[h] += _nt(qt2, dsb)
                dc_s[h] += jnp.sum(dst, axis=-1, keepdims=True)
                dcq_ref[0, h:h + 1, pl.ds(q0, tq)] += jnp.sum(dst, axis=0, keepdims=True)
                dq_ref[:, pl.ds(q0, tq)] += _nn(kts[h], dsb)

        rest = nq - 1 - kj
        for left in range(ATTN_GROUP):
            @pl.when(rest % ATTN_GROUP == left)
            def _():
                for u in range(left + 1):
                    step(kj + u, u == 0)

        def loop_body(t, carry):
            for u in range(ATTN_GROUP):
                step(kj + 1 + rest % ATTN_GROUP + ATTN_GROUP * t + u, False)
            return carry

        lax.fori_loop(0, rest // ATTN_GROUP, loop_body, 0)
        dk_ref[...] = (jnp.where(dim_a, dk_s[0], dk_s[1]) * (1.0 / LOG2E)).astype(BF16)
        dv_ref[...] = jnp.where(dim_a, dv_s[0], dv_s[1]).astype(BF16)
        lane = lax.broadcasted_iota(jnp.int32, (1, LANE), 1)
        head = 2 * pl.program_id(0)
        dc_ref[...] = jnp.where(lane == head, -dc_s[0], jnp.where(lane == head + 1, -dc_s[1], 0.0))

        @pl.when(kj == nq - 1)
        def _():
            dq_ref[...] = dq_ref[...] * Q_SCALE

    fullt = lambda row: pl.BlockSpec((LANE, T), lambda j, kj: (row(j), 0), pipeline_mode=pl.Buffered(1))
    tile = lambda col: pl.BlockSpec((tk, LANE), lambda j, kj: (kj, col(j)))
    tilet = lambda row: pl.BlockSpec((LANE, tk), lambda j, kj: (row(j), kj))
    rowl = pl.BlockSpec((1, 8, T), lambda j, kj: (j, 0, 0))
    return pl.pallas_call(
        body, name="attn_bwd", grid=(npair, nq),
        in_specs=[fullt(lambda j: j), tile(lambda j: npair + j), tilet(lambda j: npair + j),
                  tile(lambda j: 2 * npair + j), fullt(lambda j: j), tile(lambda j: 0), rowl, rowl],
        out_specs=[pl.BlockSpec((LANE, T), lambda j, kj: (j, 0)), tilet(lambda j: j), tilet(lambda j: j),
                   pl.BlockSpec((None, tk, LANE), lambda j, kj: (j, kj, 0)), rowl],
        out_shape=[jax.ShapeDtypeStruct((D_ATTN, T), F32), jax.ShapeDtypeStruct((D_ATTN, T), BF16),
                   jax.ShapeDtypeStruct((D_ATTN, T), BF16), jax.ShapeDtypeStruct((npair, T, LANE), F32),
                   jax.ShapeDtypeStruct((npair, 8, T), F32)],
        scratch_shapes=[pltpu.VMEM((2, LANE, tk), F32), pltpu.VMEM((2, LANE, tk), F32), pltpu.VMEM((2, tk, 1), F32)],
        compiler_params=_cp(2),
    )(qkvt, qkv, qkvt, qkv, dobt, cb, alrow, dlrow)


HALO = 8


def _shift_rows(cur, other, k, tm, down):
    row = lax.broadcasted_iota(jnp.int32, (tm, 1), 0)
    reps = tm // HALO
    if down:
        rolled = pltpu.roll(cur, k, 0)
        fill = jnp.tile(pltpu.roll(other, k, 0), (reps, 1))
        return jnp.where(row < k, fill, rolled)
    rolled = pltpu.roll(cur, tm - k, 0)
    fill = jnp.tile(pltpu.roll(other, HALO - k, 0), (reps, 1))
    return jnp.where(row >= tm - k, fill, rolled)


def _conv_fwd(c, hh, c_prev, hh_prev, w_ref, first, tm):
    u = c * hh
    u_prev = jnp.where(first, 0.0, c_prev * hh_prev)
    u1 = _shift_rows(u, u_prev, 1, tm, True)
    u2 = _shift_rows(u, u_prev, 2, tm, True)
    y = w_ref[0:1, :] * u2 + w_ref[1:2, :] * u1 + w_ref[2:3, :] * u
    return u, u1, u2, y


def _rms(x, g):
    rs = lax.rsqrt(jnp.mean(x * x, axis=-1, keepdims=True) + RMS_EPS)
    return x * rs * g, rs


def _mixer_tail_fwd(o, bchf, conv_w, g_attn, g_conv, w_mo, x1, lg, lb):
    T = o.shape[0]
    tm = _tile(T, TOKEN_TILE)
    hb = tm // HALO

    def body(o_ref, b_ref, c_ref, h_ref, cp_ref, hp_ref, w_ref, ga_ref, gc_ref, wmo_ref, x1_ref, lg_ref, lb_ref,
             x2_ref, r2_ref, mg_ref):
        first = pl.program_id(0) == 0
        _, _, _, y = _conv_fwd(c_ref[...], h_ref[...], cp_ref[...], hp_ref[...], w_ref, first, tm)
        na, _ = _rms(o_ref[...], ga_ref[...])
        nc, _ = _rms(b_ref[...] * y, gc_ref[...])
        nab = na.astype(BF16)
        ncb = nc.astype(BF16)
        mg_ref[:, 0:D_ATTN] = nab
        mg_ref[:, D_ATTN:] = ncb
        r2 = ALPHA * x1_ref[...] + _nn(nab, wmo_ref[0:D_ATTN, :]) + _nn(ncb, wmo_ref[D_ATTN:, :])
        r2_ref[...] = r2
        xhat, _ = _ln_stats(r2)
        x2_ref[...] = xhat * lg_ref[...] + lb_ref[...]

    row = lambda n, col=0: pl.BlockSpec((tm, n), lambda i: (i, col))
    prev = lambda col: pl.BlockSpec((HALO, D_CONV), lambda i: (jnp.maximum(i * hb - 1, 0), col))
    return pl.pallas_call(
        body, name="mixer_tail_fwd", grid=(T // tm,),
        in_specs=[row(D_ATTN), row(D_CONV, 0), row(D_CONV, 1), row(D_CONV, 2), prev(1), prev(2),
                  _resident((3, D_CONV)), _resident((1, D_ATTN)), _resident((1, D_CONV)),
                  _resident((D_MODEL, D_MODEL)), row(D_MODEL), _resident((1, D_MODEL)), _resident((1, D_MODEL))],
        out_specs=[row(D_MODEL), row(D_MODEL), row(D_MODEL)],
        out_shape=[jax.ShapeDtypeStruct((T, D_MODEL), F32), jax.ShapeDtypeStruct((T, D_MODEL), F32),
                   jax.ShapeDtypeStruct((T, D_MODEL), BF16)],
        compiler_params=_cp(1),
    )(o, bchf, bchf, bchf, bchf, bchf, conv_w, g_attn, g_conv, w_mo, x1, lg, lb)


def _head_sum_rows():
    row = lax.broadcasted_iota(jnp.int32, (4 * 8, D_ATTN), 0)
    head = lax.broadcasted_iota(jnp.int32, (4 * 8, D_ATTN), 1) // HEAD_DIM
    return jnp.where((row % 8 < 2) & (2 * (row // 8) + row % 8 == head), 1.0, 0.0).astype(F32)


def _mixer_tail_bwd(dx2, r2, lg, w_mo, o, bchf, conv_w, g_attn, g_conv, swap=()):
    T = o.shape[0]
    tm = _tile(T, TOKEN_TILE)
    hb = tm // HALO
    ns = len(swap)
    last = T // tm - 1

    def body(dx2_ref, r2_ref, lg_ref, wmo_ref, o_ref, b_ref, c_ref, h_ref, cp_ref, hp_ref, w_ref, ga_ref, gc_ref,
             *rest):
        comm_in = rest[:ns]
        dx1_ref, dr_ref, dot_ref, dl_ref, dco_ref, dlg_ref, dlb_ref, dga_ref, dgc_ref = rest[ns:ns + 9]
        comm_out, sems = rest[ns + 9:2 * ns + 9], rest[2 * ns + 9:]
        i = pl.program_id(0)
        if ns:
            @pl.when(i == 0)
            def _():
                _swap_start(comm_in, comm_out, *sems)

            @pl.when(i == last)
            def _():
                _swap_finish(comm_in, comm_out, *sems)

        @pl.when(i == 0)
        def _():
            for ref in (dlg_ref, dlb_ref, dga_ref, dgc_ref):
                ref[...] = jnp.zeros_like(ref)

        dy = dx2_ref[...]
        xhat, rstd = _ln_stats(r2_ref[...])
        dr = _ln_bwd(dy, xhat, rstd, lg_ref[...])
        dlg_ref[...] += _rowsum(dy * xhat)
        dlb_ref[...] += _rowsum(dy)
        dx1_ref[...] = ALPHA * dr
        drb = dr.astype(BF16)
        dr_ref[...] = drb
        dna = _nt(drb, wmo_ref[0:D_ATTN, :])
        dnc = _nt(drb, wmo_ref[D_ATTN:, :])

        def rms_bwd(x, g, dn):
            rs = lax.rsqrt(jnp.mean(x * x, axis=-1, keepdims=True) + RMS_EPS)
            dng = dn * g
            dx = rs * dng - x * (rs * rs * rs) * jnp.mean(dng * x, axis=-1, keepdims=True)
            return dx, _rowsum(dn * x * rs)

        oo = o_ref[...]
        do, dga = rms_bwd(oo, ga_ref[...], dna)
        dga_ref[...] += dga
        dot_ref[...] = do.T.astype(BF16)
        dl_ref[...] = lax.dot_general(_head_sum_rows(), do * oo, (((1,), (1,)), ((), ())),
                                      preferred_element_type=F32, precision=lax.Precision.HIGHEST)
        _, _, _, y = _conv_fwd(c_ref[...], h_ref[...], cp_ref[...], hp_ref[...], w_ref, i == 0, tm)
        dco, dgc = rms_bwd(b_ref[...] * y, gc_ref[...], dnc)
        dgc_ref[...] += dgc
        dco_ref[...] = dco

    row = lambda n, col=0: pl.BlockSpec((tm, n), lambda i: (i, col))
    prev = lambda col: pl.BlockSpec((HALO, D_CONV), lambda i: (jnp.maximum(i * hb - 1, 0), col))
    vec = lambda n: _resident((1, n))
    return pl.pallas_call(
        body, name="mixer_tail_bwd", grid=(T // tm,),
        in_specs=[row(D_MODEL), row(D_MODEL), vec(D_MODEL), _resident((D_MODEL, D_MODEL)), row(D_ATTN),
                  row(D_CONV, 0), row(D_CONV, 1), row(D_CONV, 2), prev(1), prev(2), _resident((3, D_CONV)),
                  vec(D_ATTN), vec(D_CONV)] + [ANY_SPEC] * ns,
        out_specs=[row(D_MODEL), row(D_MODEL), pl.BlockSpec((D_ATTN, tm), lambda i: (0, i)),
                   pl.BlockSpec((4 * 8, tm), lambda i: (0, i)), row(D_CONV),
                   vec(D_MODEL), vec(D_MODEL), vec(D_ATTN), vec(D_CONV)] + [ANY_SPEC] * ns,
        out_shape=[jax.ShapeDtypeStruct((T, D_MODEL), F32), jax.ShapeDtypeStruct((T, D_MODEL), BF16),
                   jax.ShapeDtypeStruct((D_ATTN, T), BF16),
                   jax.ShapeDtypeStruct((4 * 8, T), F32),
                   jax.ShapeDtypeStruct((T, D_CONV), F32), jax.ShapeDtypeStruct((1, D_MODEL), F32),
                   jax.ShapeDtypeStruct((1, D_MODEL), F32), jax.ShapeDtypeStruct((1, D_ATTN), F32),
                   jax.ShapeDtypeStruct((1, D_CONV), F32)] + _swap_shapes(swap),
        scratch_shapes=_swap_sems(ns) if ns else [],
        compiler_params=_cp(1),
    )(dx2, r2, lg, w_mo, o, bchf, bchf, bchf, bchf, bchf, conv_w, g_attn, g_conv, *swap)


def _conv_bwd(dco, bchf, conv_w):
    T = dco.shape[0]
    tm = _tile(T, TOKEN_TILE)
    hb = tm // HALO
    nt = T // tm

    def body(dco_ref, dcon_ref, b_ref, bn_ref, c_ref, h_ref, cp_ref, hp_ref, w_ref, dbch_ref, dw_ref):
        i = pl.program_id(0)

        @pl.when(i == 0)
        def _():
            dw_ref[...] = jnp.zeros_like(dw_ref)

        cc = c_ref[...]
        hh = h_ref[...]
        u, u1, u2, y = _conv_fwd(cc, hh, cp_ref[...], hp_ref[...], w_ref, i == 0, tm)
        dco = dco_ref[...]
        bb = b_ref[...]
        dyc = dco * bb
        dy_next = jnp.where(i == nt - 1, 0.0, dcon_ref[...] * bn_ref[...])
        d1 = _shift_rows(dyc, dy_next, 1, tm, False)
        d2 = _shift_rows(dyc, dy_next, 2, tm, False)
        du = w_ref[2:3, :] * dyc + w_ref[1:2, :] * d1 + w_ref[0:1, :] * d2
        dbch_ref[:, 0:D_CONV] = (dco * y).astype(BF16)
        dbch_ref[:, D_CONV:2 * D_CONV] = (du * hh).astype(BF16)
        dbch_ref[:, 2 * D_CONV:] = (du * cc).astype(BF16)
        dw_ref[0:1, :] += _rowsum(dyc * u2)
        dw_ref[1:2, :] += _rowsum(dyc * u1)
        dw_ref[2:3, :] += _rowsum(dyc * u)

    row = lambda n, col=0: pl.BlockSpec((tm, n), lambda i: (i, col))
    prev = lambda col: pl.BlockSpec((HALO, D_CONV), lambda i: (jnp.maximum(i * hb - 1, 0), col))
    nxt = lambda col: pl.BlockSpec((HALO, D_CONV), lambda i: (jnp.minimum((i + 1) * hb, T // HALO - 1), col))
    return pl.pallas_call(
        body, name="conv_bwd", grid=(nt,),
        in_specs=[row(D_CONV), nxt(0), row(D_CONV, 0), nxt(0), row(D_CONV, 1), row(D_CONV, 2), prev(1), prev(2),
                  _resident((3, D_CONV))],
        out_specs=[row(3 * D_CONV), _resident((8, D_CONV))],
        out_shape=[jax.ShapeDtypeStruct((T, 3 * D_CONV), BF16), jax.ShapeDtypeStruct((8, D_CONV), F32)],
        compiler_params=_cp(1),
    )(dco, dco, bchf, bchf, bchf, bchf, bchf, bchf, conv_w)


def _mixer_in_bwd(dx1a, dqt, dkt, dvt, dbch, dfl, w_qkvt, w_bch, w_f):
    T = dx1a.shape[0]
    tm = _tile(T, TOKEN_TILE)

    def body(a_ref, dq_ref, dk_ref, dv_ref, db_ref, df_ref, wq_ref, wb_ref, wf_ref, o_ref):
        acc = a_ref[...] + _nt(db_ref[...], wb_ref[...]) + _nt(df_ref[...], wf_ref[...])
        for n, ref in enumerate((dq_ref, dk_ref, dv_ref)):
            acc = acc + _tn(ref[...].astype(BF16), wq_ref[n * D_ATTN:(n + 1) * D_ATTN, :])
        o_ref[...] = acc

    row = lambda n: pl.BlockSpec((tm, n), lambda i: (i, 0))
    col = pl.BlockSpec((D_ATTN, tm), lambda i: (0, i))
    return pl.pallas_call(
        body, name="mixer_in_bwd", grid=(T // tm,),
        in_specs=[row(D_MODEL), col, col, col, row(3 * D_CONV), row(N_FLOG),
                  _resident((3 * D_ATTN, D_MODEL)), _resident((D_MODEL, 3 * D_CONV)), _resident((D_MODEL, N_FLOG))],
        out_specs=row(D_MODEL),
        out_shape=jax.ShapeDtypeStruct((T, D_MODEL), F32),
        compiler_params=_cp(1),
    )(dx1a, dqt, dkt, dvt, dbch, dfl, w_qkvt, w_bch, w_f)


def _ple_loss(x3, p, tgt, w_g, w_p, b_g, lg, lb):
    T = x3.shape[0]
    tm = _tile(T, TOKEN_TILE)

    def body(x_ref, p_ref, t_ref, wg_ref, wp_ref, bg_ref, lg_ref, lb_ref,
             dx_ref, de_ref, dz_ref, loss_ref, dlg_ref, dlb_ref, dbg_ref):
        @pl.when(pl.program_id(0) == 0)
        def _():
            for ref in (loss_ref, dlg_ref, dlb_ref, dbg_ref):
                ref[...] = jnp.zeros_like(ref)

        xf = x_ref[...]
        gate = _sigmoid(_nn(xf.astype(BF16), wg_ref[...]) + bg_ref[...])
        e = _nn(p_ref[...].astype(BF16), wp_ref[...])
        xhat, rstd = _ln_stats(ALPHA * xf + gate * e)
        err = xhat * lg_ref[...] + lb_ref[...] - t_ref[...]
        sq = jnp.sum(_rowsum(err * err), axis=-1, keepdims=True)
        loss_ref[...] += jnp.broadcast_to(sq * (0.5 / D_MODEL), loss_ref.shape)
        dy = err * (1.0 / D_MODEL)
        dr = _ln_bwd(dy, xhat, rstd, lg_ref[...])
        dlg_ref[...] += _rowsum(dy * xhat)
        dlb_ref[...] += _rowsum(dy)
        de_ref[...] = (dr * gate).astype(BF16)
        dz = dr * e * gate * (1.0 - gate)
        dbg_ref[...] += _rowsum(dz)
        dzb = dz.astype(BF16)
        dz_ref[...] = dzb
        dx_ref[...] = ALPHA * dr + _nt(dzb, wg_ref[...])

    row = lambda n: pl.BlockSpec((tm, n), lambda i: (i, 0))
    vec = lambda n: _resident((1, n))
    return pl.pallas_call(
        body, name="ple_loss", grid=(T // tm,),
        in_specs=[row(D_MODEL), row(PLE_DIM), row(D_MODEL), _resident((D_MODEL, D_MODEL)),
                  _resident((PLE_DIM, D_MODEL)), vec(D_MODEL), vec(D_MODEL), vec(D_MODEL)],
        out_specs=[row(D_MODEL), row(D_MODEL), row(D_MODEL), vec(LANE), vec(D_MODEL), vec(D_MODEL), vec(D_MODEL)],
        out_shape=[jax.ShapeDtypeStruct((T, D_MODEL), F32), jax.ShapeDtypeStruct((T, D_MODEL), BF16),
                   jax.ShapeDtypeStruct((T, D_MODEL), BF16), jax.ShapeDtypeStruct((1, LANE), F32),
                   jax.ShapeDtypeStruct((1, D_MODEL), F32), jax.ShapeDtypeStruct((1, D_MODEL), F32),
                   jax.ShapeDtypeStruct((1, D_MODEL), F32)],
        compiler_params=_cp(1),
    )(x3, p, tgt, w_g, w_p, b_g, lg, lb)


def _from_row_layout(vr):
    return vr[:, :2, :].reshape(N_HEADS, -1).T


def _local_step(x, p, tgt, w, overlap=None):
    bf = lambda a: a.astype(BF16)
    w1i, w1o = bf(w["ffn1_w_in"]), bf(w["ffn1_w_out"])
    first = _ffn_fwd(x, w1i, w1o, w["ln1_g"], w["ln1_b"], "ffn1_fwd", overlap["gather"] if overlap else ())
    x1, r1, g1, u1, h1 = first[:5]
    if overlap:
        w = {**w, **overlap["weights"](first[5:])}
    w2i, w2o = bf(w["ffn2_w_in"]), bf(w["ffn2_w_out"])
    wmi = w["w_mix_in"]
    o_f = 3 * D_ATTN
    o_b = o_f + N_HEADS
    w_qkv = bf(wmi[:, :o_f])
    w_f = bf(jnp.pad(wmi[:, o_f:o_b], ((0, 0), (0, N_FLOG - N_HEADS))))
    w_bch = bf(wmi[:, o_b:])
    w_bchf = jnp.concatenate([w_bch, w_f], axis=1)
    w_mo, w_g, w_p = bf(w["w_mix_out"]), bf(w["w_ple_gate"]), bf(w["w_ple"])
    b_f = jnp.pad(w["b_forget"], ((0, 0), (0, N_FLOG - N_HEADS)))

    q_scale = jnp.concatenate([jnp.full((1, D_ATTN), Q_SCALE * LOG2E, F32), jnp.ones((1, 2 * D_ATTN), F32)], axis=1)
    qkv, qkvt = _matmul_nn(x1, w_qkv, q_scale, BF16, "proj_qkv", also_transposed=True)
    bchf = _matmul_nn(x1, w_bchf, jnp.ones((1, 3 * D_CONV + N_FLOG), F32), F32, "proj_bchf")
    fcol = 3 * D_CONV // N_FLOG
    c = _forget_cumsum(bchf, fcol, b_f)
    o, alrow = _attn_fwd(qkv, qkvt, _fold_key_bias(qkv, c))
    x2, r2, merged = _mixer_tail_fwd(o, bchf, w["conv_w"], w["g_attn"], w["g_conv"], w_mo, x1, w["ln2_g"], w["ln2_b"])
    x3, r3, g2, u2, h2 = _ffn_fwd(x2, w2i, w2o, w["ln3_g"], w["ln3_b"], "ffn2_fwd")

    grads = {}
    dx3, de, dz, loss, grads["ln4_g"], grads["ln4_b"], grads["b_ple_gate"] = _ple_loss(
        x3, p, tgt, w_g, w_p, w["b_ple_gate"], w["ln4_g"], w["ln4_b"])
    by_chip = overlap is not None
    grads["w_ple"] = _matmul_tn(p, de, "dw_ple")
    grads["w_ple_gate"] = _matmul_tn(x3, dz, "dw_ple_gate")

    dx2, dgu2, df2, grads["ln3_g"], grads["ln3_b"] = _ffn_bwd(dx3, r3, g2, u2, w2i, w2o, w["ln3_g"], "ffn2_bwd")
    grads["ffn2_w_in"] = _matmul_tn(x2, dgu2, "dw_ffn2_in", by_chip)
    grads["ffn2_w_out"] = _matmul_tn(h2, df2, "dw_ffn2_out")

    to_swap = overlap["swap"](grads) if overlap else ()
    tail = _mixer_tail_bwd(dx2, r2, w["ln2_g"], w_mo, o, bchf, w["conv_w"], w["g_attn"], w["g_conv"], to_swap)
    (dx1a, dr2, dobt, delta, dco, grads["ln2_g"], grads["ln2_b"], grads["g_attn"], grads["g_conv"]) = tail[:9]
    grads["w_mix_out"] = _matmul_tn(merged, dr2, "dw_mix_out")
    dbch, dcw = _conv_bwd(dco, bchf, w["conv_w"])
    grads["conv_w"] = dcw[:3]
    dqt, dkt, dvt, dck, dcq = _attn_bwd(qkv, qkvt, dobt, c, alrow, delta.reshape(N_HEADS // 2, 8, -1))
    dcq_lanes = jnp.pad(_from_row_layout(dcq), ((0, 0), (0, N_FLOG - N_HEADS)))
    dfl, dbf = _forget_bwd(dck, dcq_lanes, bchf, fcol, b_f)
    grads["b_forget"] = dbf[:, :N_HEADS]
    dx1 = _mixer_in_bwd(dx1a, dqt, dkt, dvt, dbch, dfl, w_qkv.T, w_bch, w_f)
    grads["w_mix_in"] = jnp.concatenate(
        [_matmul_tokens(dqt, x1, "dw_q").T, _matmul_tokens(dkt, x1, "dw_k").T, _matmul_tokens(dvt, x1, "dw_v").T,
         _matmul_tn(x1, dfl, "dw_flog")[:, :N_HEADS], _matmul_tn(x1, dbch, "dw_bch")], axis=1)

    dx0, dgu1, df1, grads["ln1_g"], grads["ln1_b"] = _ffn_bwd(dx1, r1, g1, u1, w1i, w1o, w["ln1_g"], "ffn1_bwd")
    grads["ffn1_w_out"] = _matmul_tn(h1, df1, "dw_ffn1_out")
    if not overlap:
        grads["ffn1_w_in"] = _matmul_tn(x, dgu1, "dw_ffn1_in")
        return loss, dx0, grads
    sums = overlap["chip_sums"](grads, to_swap, tail[9:])
    grads["ffn1_w_in"], *received = _matmul_tn(x, dgu1, "dw_ffn1_in", by_chip, exchange=sums)
    return loss, dx0, grads, sums, received


WEIGHTS = ["ffn1_w_in", "ffn1_w_out", "ln1_g", "ln1_b", "w_mix_in", "b_forget", "conv_w", "g_attn", "g_conv",
           "w_mix_out", "ln2_g", "ln2_b", "ffn2_w_in", "ffn2_w_out", "ln3_g", "ln3_b", "w_ple", "w_ple_gate",
           "b_ple_gate", "ln4_g", "ln4_b"]
LAYOUT = {
    "ffn1_w_in": ((D_MODEL, 2 * D_FF), 1), "ffn1_w_out": ((D_FF, D_MODEL), 0),
    "w_mix_in": ((D_MODEL, 3 * D_ATTN + N_HEADS + 3 * D_CONV), 1), "conv_w": ((3, D_CONV), 1),
    "w_mix_out": ((D_MODEL, D_MODEL), 0), "ffn2_w_in": ((D_MODEL, 2 * D_FF), 1), "ffn2_w_out": ((D_FF, D_MODEL), 0),
    "w_ple": ((PLE_DIM, D_MODEL), 1), "w_ple_gate": ((D_MODEL, D_MODEL), 0),
    "ln1_g": ((1, D_MODEL), None), "ln1_b": ((1, D_MODEL), None), "b_forget": ((1, N_HEADS), None),
    "g_attn": ((1, D_ATTN), None), "g_conv": ((1, D_CONV), None), "ln2_g": ((1, D_MODEL), None),
    "ln2_b": ((1, D_MODEL), None), "ln3_g": ((1, D_MODEL), None), "ln3_b": ((1, D_MODEL), None),
    "b_ple_gate": ((1, D_MODEL), None), "ln4_g": ((1, D_MODEL), None), "ln4_b": ((1, D_MODEL), None),
}
BIG = [n for n in WEIGHTS if LAYOUT[n][1] is not None and n != "conv_w"]
SMALL = [n for n in WEIGHTS if n not in BIG]
ROW = 1024
SMALL_ROWS = 16


def _shard_shape(name):
    shape, axis = LAYOUT[name]
    if axis is None:
        return shape
    return tuple(s // N_CHIPS if a == axis else s for a, s in enumerate(shape))


def _halves(a):
    return a.reshape(a.shape[:-2] + (2, a.shape[-2] // 2, a.shape[-1]))


def _split_chips(name, full):
    shape, axis = LAYOUT[name]
    if axis == 0:
        return full.reshape((N_CHIPS, shape[0] // N_CHIPS) + shape[1:])
    return jnp.moveaxis(full.reshape(shape[:1] + (N_CHIPS, shape[1] // N_CHIPS)), 1, 0)


def _join_chips(name, parts):
    shape, axis = LAYOUT[name]
    if axis == 0:
        return parts.reshape(shape)
    return jnp.moveaxis(parts, 0, 1).reshape(shape)


SMALL_AT = {"ln1_g": (0, 0), "ln1_b": (1, 0), "ln2_g": (2, 0), "ln2_b": (3, 0), "ln3_g": (4, 0), "ln3_b": (5, 0),
            "b_ple_gate": (6, 0), "ln4_g": (7, 0), "ln4_b": (8, 0), "g_attn": (9, 0), "g_conv": (9, D_ATTN),
            "b_forget": (10, 0), "conv_w": (10, LANE)}
CONV_SHARD = D_CONV // N_CHIPS


def _pack_small_grads(grads):
    def body(*refs):
        ins, o_ref = dict(zip(SMALL, refs[:-1])), refs[-1]
        o_ref[...] = jnp.zeros_like(o_ref)
        for s in range(N_CHIPS):
            for n in SMALL:
                r, c0 = SMALL_AT[n]
                if n == "conv_w":
                    for k in range(3):
                        o_ref[s, r:r + 1, c0 + k * CONV_SHARD:c0 + (k + 1) * CONV_SHARD] = (
                            ins[n][k:k + 1, s * CONV_SHARD:(s + 1) * CONV_SHARD])
                else:
                    o_ref[s, r:r + 1, c0:c0 + ins[n].shape[1]] = ins[n][...]

    return pl.pallas_call(
        body, name="pack_small_grads",
        out_shape=jax.ShapeDtypeStruct((N_CHIPS, SMALL_ROWS, ROW), F32),
    )(*[grads[n] for n in SMALL])


def _adamw_math(w, g, m, v):
    c1 = 1.0 - ADAM_B1 ** ADAM_STEP
    c2 = 1.0 - ADAM_B2 ** ADAM_STEP
    m = ADAM_B1 * m + (1.0 - ADAM_B1) * g
    v = ADAM_B2 * v + (1.0 - ADAM_B2) * (g * g)
    return -ADAM_LR * ((m / c1) / (jnp.sqrt(v / c2) + ADAM_EPS) + ADAM_WD * w), m, v


def _adamw_small(g_mine, g_sib, c_idx, w, m, v):
    ns = len(SMALL)

    def body(c_ref, gm_ref, gs_ref, *refs):
        ws, ms, vs = refs[:ns], refs[ns:2 * ns], refs[2 * ns:3 * ns]
        outs = refs[3 * ns:]
        mine_first = c_ref[0] == 0
        top = jnp.where(mine_first, gm_ref[...], gs_ref[...])
        bot = jnp.where(mine_first, gs_ref[...], gm_ref[...])
        for i, n in enumerate(SMALL):
            r, c0 = SMALL_AT[n]
            blk, rr = (top, r) if r < SMALL_ROWS // 2 else (bot, r - SMALL_ROWS // 2)
            rows, width = ws[i].shape
            for k in range(rows):
                g = blk[rr:rr + 1, c0 + k * width:c0 + (k + 1) * width]
                d, mn, vn = _adamw_math(ws[i][k:k + 1, :], g, ms[i][k:k + 1, :], vs[i][k:k + 1, :])
                for q, val in enumerate((g, d, mn, vn)):
                    outs[q * ns + i][k:k + 1, :] = val

    shapes = [jax.ShapeDtypeStruct(a.shape, F32) for a in w]
    vmem = pl.BlockSpec(memory_space=pltpu.VMEM)
    res = pl.pallas_call(
        body, name="adamw_small",
        in_specs=[pl.BlockSpec(memory_space=pltpu.SMEM)] + [vmem] * (2 + 3 * ns),
        out_specs=[vmem] * (4 * ns),
        out_shape=shapes * 4,
    )(c_idx, g_mine, g_sib, *w, *m, *v)
    return [res[q * ns:(q + 1) * ns] for q in range(4)]


def _place():
    x, y, c = lax.axis_index("x"), lax.axis_index("y"), lax.axis_index("c")
    others = [(1 - x, y), (x, 1 - y), (1 - x, 1 - y)]
    return x, y, c, others


ANY_SPEC = pl.BlockSpec(memory_space=pl.ANY)


def _remote(src, dst, send_sems, recv_sems, k, to):
    return pltpu.make_async_remote_copy(src_ref=src, dst_ref=dst, send_sem=send_sems.at[k], recv_sem=recv_sems.at[k],
                                        device_id=to, device_id_type=MESH)


def _all_gather(shards):
    n = len(shards)

    def body(*refs):
        ins, outs, send_sems, recv_sems = refs[:n], refs[n:2 * n], refs[2 * n], refs[2 * n + 1]
        _gather_start(ins, outs, send_sems, recv_sems)
        _gather_finish(ins, outs, send_sems, recv_sems)

    return pl.pallas_call(
        body, name="all_gather_weights",
        out_shape=_gather_shapes(shards), in_specs=[ANY_SPEC] * n, out_specs=[ANY_SPEC] * n,
        scratch_shapes=_gather_sems(n),
    )(*shards)


def _gather_shapes(shards):
    return [jax.ShapeDtypeStruct((N_CHIPS,) + a.shape, a.dtype) for a in shards]


def _gather_sems(n):
    return [pltpu.SemaphoreType.DMA((6 * n,)), pltpu.SemaphoreType.DMA((6 * n,))]


def _gather_sends(ins, outs, send_sems, recv_sems):
    x, y, c, others = _place()
    s = 2 * x + y
    return [_remote(ins[t].at[c], outs[t].at[s, c], send_sems, recv_sems, 6 * t + j, (*chip, c))
            for t in range(len(ins)) for j, chip in enumerate(others)]


def _gather_start(ins, outs, send_sems, recv_sems):
    for cp in _gather_sends(ins, outs, send_sems, recv_sems):
        cp.start()


def _gather_finish(ins, outs, send_sems, recv_sems):
    x, y, c, others = _place()
    slot = lambda t, chip, half: outs[t].at[2 * chip[0] + chip[1], half]
    passed = []
    for t in range(len(ins)):
        for j, chip in enumerate(others):
            landed = slot(t, chip, c)
            _remote(landed, landed, send_sems, recv_sems, 6 * t + j, (x, y, c)).wait_recv()
            passed.append(_remote(landed, landed, send_sems, recv_sems, 6 * t + 3 + j, (x, y, 1 - c)))
            passed[-1].start()
    for t in range(len(ins)):
        for j, chip in enumerate(others):
            landed = slot(t, chip, 1 - c)
            _remote(landed, landed, send_sems, recv_sems, 6 * t + 3 + j, (x, y, c)).wait_recv()
    for cp in _gather_sends(ins, outs, send_sems, recv_sems) + passed:
        cp.wait_send()


def _swap_halves(gs, tag):
    n = len(gs)

    def body(*refs):
        ins, outs, send_sems, recv_sems = refs[:n], refs[n:2 * n], refs[2 * n], refs[2 * n + 1]
        _swap_start(ins, outs, send_sems, recv_sems)
        _swap_finish(ins, outs, send_sems, recv_sems)

    return pl.pallas_call(
        body, name="grad_swap_halves_" + tag,
        out_shape=_swap_shapes(gs), in_specs=[ANY_SPEC] * n, out_specs=[ANY_SPEC] * n,
        scratch_shapes=_swap_sems(n),
    )(*gs)


def _swap_shapes(gs):
    return [jax.ShapeDtypeStruct(g.shape[:1] + g.shape[2:], g.dtype) for g in gs]


def _swap_sems(n):
    return [pltpu.SemaphoreType.DMA((n,)), pltpu.SemaphoreType.DMA((n,))]


def _swap_copies(ins, outs, send_sems, recv_sems):
    x, y, c, _ = _place()
    return [_remote(ins[t].at[:, 1 - c], outs[t], send_sems, recv_sems, t, (x, y, 1 - c)) for t in range(len(ins))]


def _swap_start(ins, outs, send_sems, recv_sems):
    for cp in _swap_copies(ins, outs, send_sems, recv_sems):
        cp.start()


def _swap_finish(ins, outs, send_sems, recv_sems):
    for cp in _swap_copies(ins, outs, send_sems, recv_sems):
        cp.wait()


def _exchange_chips(pps):
    n = len(pps)

    def body(*refs):
        ins, outs, send_sems, recv_sems = refs[:n], refs[n:2 * n], refs[2 * n], refs[2 * n + 1]
        _exchange_start(ins, outs, send_sems, recv_sems)
        _exchange_finish(ins, outs, send_sems, recv_sems)

    return pl.pallas_call(
        body, name="grad_exchange_chips",
        out_shape=_exchange_shapes(pps), in_specs=[ANY_SPEC] * n, out_specs=[ANY_SPEC] * n,
        scratch_shapes=_exchange_sems(n),
    )(*pps)


def _exchange_shapes(pps):
    return [jax.ShapeDtypeStruct(p.shape, p.dtype) for p in pps]


def _exchange_sems(n):
    return [pltpu.SemaphoreType.DMA((3 * n,)), pltpu.SemaphoreType.DMA((3 * n,))]


def _exchange_sends(ins, outs, send_sems, recv_sems):
    x, y, c, others = _place()
    s = 2 * x + y
    return [_remote(ins[t].at[2 * chip[0] + chip[1]], outs[t].at[s], send_sems, recv_sems, 3 * t + j, (*chip, c))
            for t in range(len(ins)) for j, chip in enumerate(others)]


def _exchange_start(ins, outs, send_sems, recv_sems):
    for cp in _exchange_sends(ins, outs, send_sems, recv_sems):
        cp.start()


def _exchange_finish(ins, outs, send_sems, recv_sems):
    x, y, c, others = _place()
    for t in range(len(ins)):
        for j, chip in enumerate(others):
            landed = outs[t].at[2 * chip[0] + chip[1]]
            _remote(landed, landed, send_sems, recv_sems, 3 * t + j, (x, y, c)).wait_recv()
    for cp in _exchange_sends(ins, outs, send_sems, recv_sems):
        cp.wait_send()


def _share_half(rs):
    n = len(rs)

    def body(*refs):
        ins, outs, send_sems, recv_sems = refs[:n], refs[n:2 * n], refs[2 * n], refs[2 * n + 1]
        x, y, c, _ = _place()
        copies = [_remote(ins[t], outs[t], send_sems, recv_sems, t, (x, y, 1 - c)) for t in range(n)]
        for cp in copies:
            cp.start()
        for cp in copies:
            cp.wait()

    return pl.pallas_call(
        body, name="grad_share_half",
        out_shape=[jax.ShapeDtypeStruct(r.shape, r.dtype) for r in rs],
        in_specs=[ANY_SPEC] * n, out_specs=[ANY_SPEC] * n,
        scratch_shapes=[pltpu.SemaphoreType.DMA((n,)), pltpu.SemaphoreType.DMA((n,))],
    )(*rs)


ELEMENTWISE_BLOCK_BYTES = 1 << 20


def _row_tile(rows, cols):
    return _tile(rows, max(8, ELEMENTWISE_BLOCK_BYTES // (4 * cols) // 8 * 8))


def _add_my_half(g, sib, c_idx, name):
    rh, cols = g.shape[2:]
    tr = _row_tile(rh, cols)

    def body(c_ref, g_ref, s_ref, o_ref):
        o_ref[...] = (g_ref[...] + s_ref[...]).astype(BF16)

    return pl.pallas_call(
        body, name="grad_add_halves_" + name,
        grid_spec=pltpu.PrefetchScalarGridSpec(
            num_scalar_prefetch=1, grid=(N_CHIPS, rh // tr),
            in_specs=[pl.BlockSpec((None, None, tr, cols), lambda s, i, c: (s, c[0], i, 0)),
                      pl.BlockSpec((None, tr, cols), lambda s, i, c: (s, i, 0))],
            out_specs=pl.BlockSpec((None, tr, cols), lambda s, i, c: (s, i, 0))),
        out_shape=jax.ShapeDtypeStruct((N_CHIPS, rh, cols), BF16),
        compiler_params=_cp(2),
    )(c_idx, g, sib)


def _sum_chips(parts, pp, s_idx, name):
    rh, cols = parts.shape[1:]
    tr = _row_tile(rh, cols)

    def body(s_ref, p0, p1, p2, p3, mine_ref, o_ref):
        own = mine_ref[...]
        t = [jnp.where(s_ref[0] == k, own, p[...]).astype(F32) for k, p in enumerate((p0, p1, p2, p3))]
        o_ref[...] = ((t[0] + t[1]) + t[2]) + t[3]

    slot = lambda k: pl.BlockSpec((None, tr, cols), lambda i, s: (jnp.where(s[0] == k, (k + 1) % N_CHIPS, k), i, 0))
    return pl.pallas_call(
        body, name="grad_sum_chips_" + name,
        grid_spec=pltpu.PrefetchScalarGridSpec(
            num_scalar_prefetch=1, grid=(rh // tr,),
            in_specs=[slot(0), slot(1), slot(2), slot(3), pl.BlockSpec((None, tr, cols), lambda i, s: (s[0], i, 0))],
            out_specs=pl.BlockSpec((tr, cols), lambda i, s: (i, 0))),
        out_shape=jax.ShapeDtypeStruct((rh, cols), F32),
        compiler_params=_cp(1),
    )(s_idx, parts, parts, parts, parts, pp)


def _adamw(w, g_mine, g_sib, m, v, c_idx, name):
    rows, cols = w.shape
    tr = _row_tile(rows // 2, cols)
    nbh = rows // 2 // tr

    def body(c_ref, w_ref, gm_ref, gs_ref, m_ref, v_ref, g_ref, d_ref, mo_ref, vo_ref):
        g = jnp.where(pl.program_id(0) // nbh == c_ref[0], gm_ref[...], gs_ref[...])
        g_ref[...] = g
        d_ref[...], mo_ref[...], vo_ref[...] = _adamw_math(w_ref[...], g, m_ref[...], v_ref[...])

    spec = pl.BlockSpec((tr, cols), lambda i, c: (i, 0))
    half = pl.BlockSpec((tr, cols), lambda i, c: (i % nbh, 0))
    return pl.pallas_call(
        body, name="adamw_" + name,
        grid_spec=pltpu.PrefetchScalarGridSpec(
            num_scalar_prefetch=1, grid=(rows // tr,),
            in_specs=[spec, half, half, spec, spec], out_specs=[spec] * 4),
        out_shape=[jax.ShapeDtypeStruct(w.shape, F32)] * 4,
        compiler_params=_cp(1),
    )(c_idx, w, g_mine, g_sib, m, v)


def kernel(x, p, ffn1_w_in, ffn1_w_out, ln1_g, ln1_b, w_mix_in, b_forget, conv_w, g_attn, g_conv, w_mix_out, ln2_g, ln2_b, ffn2_w_in, ffn2_w_out, ln3_g, ln3_b, w_ple, w_ple_gate, b_ple_gate, ln4_g, ln4_b, loss_target, m_ffn1_w_in, m_ffn1_w_out, m_ln1_g, m_ln1_b, m_w_mix_in, m_b_forget, m_conv_w, m_g_attn, m_g_conv, m_w_mix_out, m_ln2_g, m_ln2_b, m_ffn2_w_in, m_ffn2_w_out, m_ln3_g, m_ln3_b, m_w_ple, m_w_ple_gate, m_b_ple_gate, m_ln4_g, m_ln4_b, v_ffn1_w_in, v_ffn1_w_out, v_ln1_g, v_ln1_b, v_w_mix_in, v_b_forget, v_conv_w, v_g_attn, v_g_conv, v_w_mix_out, v_ln2_g, v_ln2_b, v_ffn2_w_in, v_ffn2_w_out, v_ln3_g, v_ln3_b, v_w_ple, v_w_ple_gate, v_b_ple_gate, v_ln4_g, v_ln4_b):
    args = dict(locals())
    shard = {n: args[n][0] if LAYOUT[n][1] is not None else args[n] for n in WEIGHTS}
    m_shard = {n: args["m_" + n][0] if LAYOUT[n][1] is not None else args["m_" + n] for n in WEIGHTS}
    v_shard = {n: args["v_" + n][0] if LAYOUT[n][1] is not None else args["v_" + n] for n in WEIGHTS}
    c_idx = lax.axis_index("c").astype(jnp.int32).reshape(1)
    chip = (2 * lax.axis_index("x") + lax.axis_index("y")).astype(jnp.int32)

    conv_rows = SMALL_ROWS - shard["conv_w"].shape[0]
    mine = {n: _halves(shard[n].astype(BF16)) for n in BIG}
    mine["conv_w"] = _halves(jnp.pad(shard["conv_w"], ((0, conv_rows), (0, 0))))
    early_w = ["ffn1_w_in", "ffn1_w_out"]
    late_w = [n for n in BIG if n not in early_w] + ["conv_w"]

    def full_weights(names, gathered):
        out = {}
        for n, theirs in zip(names, gathered):
            g = lax.dynamic_update_slice(theirs, mine[n][None], (chip, 0, 0, 0))
            if n == "conv_w":
                out[n] = _join_chips(n, g.reshape(N_CHIPS, SMALL_ROWS, CONV_SHARD)[:, :3])
            else:
                out[n] = _join_chips(n, g.reshape((N_CHIPS,) + _shard_shape(n)))
        return out

    full = full_weights(early_w, _all_gather([mine[n] for n in early_w]))
    full.update({n: shard[n] for n in SMALL if n != "conv_w"})

    def per_chip(names, grads):
        by_chip = lambda n: grads[n] if grads[n].ndim == 3 else _split_chips(n, grads[n])
        return [_halves(_pack_small_grads(grads) if n == "small" else by_chip(n)) for n in names]

    def add_halves(names, mine_, sibs):
        return [_add_my_half(g, sib, c_idx, n) for n, g, sib in zip(names, mine_, sibs)]

    def chip_sums(names, grads):
        mine_ = per_chip(names, grads)
        return add_halves(names, mine_, _swap_halves(mine_, names[0]))

    ready_a = ["ffn2_w_in", "ffn2_w_out", "w_ple", "w_ple_gate"]
    ready_b = ["w_mix_in", "w_mix_out"]
    early_g = ready_a + ready_b
    late_g = early_w + ["small"]
    loss_acc, grad_x, grads, early_sums, early_parts = _local_step(
        x[0], p[0, 0], loss_target[0], full,
        overlap={"gather": [mine[n] for n in late_w], "weights": lambda gathered: full_weights(late_w, gathered),
                 "swap": lambda grads: per_chip(ready_a, grads),
                 "chip_sums": lambda grads, swapped, received: (add_halves(ready_a, swapped, received)
                                                                + chip_sums(ready_b, grads))})
    loss = lax.psum(loss_acc[0, 0], ("x", "y", "c"))

    late_sums = chip_sums(late_g, grads)
    names = early_g + late_g
    sums = list(early_sums) + late_sums
    parts = list(early_parts) + list(_exchange_chips(late_sums))
    half_of = {n: _sum_chips(pt, own, chip.reshape(1), n) for n, pt, own in zip(names, parts, sums)}
    names = BIG + ["small"]
    my_half = [half_of[n] for n in names]
    sib_half = _share_half(my_half)

    out = {}
    for n, gm, gs in zip(BIG, my_half, sib_half):
        out[n] = [a[None] for a in _adamw(shard[n], gm, gs, m_shard[n], v_shard[n], c_idx, n)]
    small = _adamw_small(my_half[-1], sib_half[-1], c_idx, [shard[n] for n in SMALL], [m_shard[n] for n in SMALL],
                         [v_shard[n] for n in SMALL])
    for i, n in enumerate(SMALL):
        out[n] = [small[q][i][None] if n == "conv_w" else small[q][i] for q in range(4)]
    return (loss, grad_x[None], *[out[n][q] for q in range(4) for n in WEIGHTS])
```

```python
import math

import jax
import jax.numpy as jnp
from jax import lax
from jax.experimental import pallas as pl
from jax.experimental.pallas import tpu as pltpu

F32 = jnp.float32
BF16 = jnp.bfloat16

D_MODEL = 1024
D_FF = 2816
N_HEADS = 8
HEAD_DIM = 64
D_ATTN = N_HEADS * HEAD_DIM
D_CONV = 512
PLE_DIM = 256
N_FLOG = 128
ALPHA = 2.0 ** 0.25
LN_EPS = 1e-5
RMS_EPS = 1e-6
NEG_INF = -1e30
Q_SCALE = 1.0 / math.sqrt(HEAD_DIM)
LOG2E = math.log2(math.e)

ADAM_LR = 0.001
ADAM_B1 = 0.9
ADAM_B2 = 0.999
ADAM_EPS = 1e-08
ADAM_WD = 0.01
ADAM_STEP = 10

V7X_VMEM_BYTES = 64 << 20
VMEM_LIMIT = V7X_VMEM_BYTES - (8 << 20)
LANE = 128
FF_CHUNK = 256
N_CHIPS = 4
TOKEN_TILE = 512
FFN_BWD_TILE = 256
ATTN_TILE = 512
SCAN_BLOCK = 512
WGRAD_TOKENS = 2048
WGRAD_T_TOKENS = 1024
MESH = pl.DeviceIdType.MESH


def _cp(n_axes):
    return pltpu.CompilerParams(dimension_semantics=("arbitrary",) * n_axes, vmem_limit_bytes=VMEM_LIMIT)


def _resident(shape):
    n = len(shape)
    return pl.BlockSpec(shape, lambda *_: (0,) * n, pipeline_mode=pl.Buffered(1))


def _nn(a, b):
    return jnp.dot(a, b, preferred_element_type=F32)


def _nt(a, b):
    return lax.dot_general(a, b, (((1,), (1,)), ((), ())), preferred_element_type=F32)


def _tn(a, b):
    return lax.dot_general(a, b, (((0,), (0,)), ((), ())), preferred_element_type=F32)


def _ln_stats(r):
    mu = jnp.mean(r, axis=-1, keepdims=True)
    xc = r - mu
    var = jnp.mean(xc * xc, axis=-1, keepdims=True)
    rstd = lax.rsqrt(var + LN_EPS)
    return xc * rstd, rstd


def _ln_bwd(dy, xhat, rstd, g):
    dxh = dy * g
    m1 = jnp.mean(dxh, axis=-1, keepdims=True)
    m2 = jnp.mean(dxh * xhat, axis=-1, keepdims=True)
    return rstd * (dxh - m1 - xhat * m2)


def _sigmoid(z):
    return 1.0 / (1.0 + jnp.exp(-z))


def _rowsum(a):
    return jnp.sum(a, axis=0, keepdims=True)


def _tile(total, want):
    if total <= want:
        return total
    for t in range(want - want % 8, 0, -8):
        if total % t == 0:
            return t
    raise ValueError((total, want))


def _ffn_fwd(x, w_in, w_out, lg, lb, name, gather=()):
    T = x.shape[0]
    tm = _tile(T, TOKEN_TILE)
    nf = D_FF // FF_CHUNK
    ng = len(gather)
    last = T // tm - 1

    def body(x_ref, wi_ref, wo_ref, lg_ref, lb_ref, *rest):
        comm_in, (xo_ref, r_ref, g_ref, u_ref, h_ref) = rest[:ng], rest[ng:ng + 5]
        comm_out, sems = rest[ng + 5:2 * ng + 5], rest[2 * ng + 5:]
        if ng:
            @pl.when(pl.program_id(0) == 0)
            def _():
                _gather_start(comm_in, comm_out, *sems)

        xf = x_ref[...]
        xb = xf.astype(BF16)
        acc = jnp.zeros((tm, D_MODEL), F32)
        for j in range(nf):
            c0 = j * FF_CHUNK
            g = _nn(xb, wi_ref[:, c0:c0 + FF_CHUNK])
            u = _nn(xb, wi_ref[:, D_FF + c0:D_FF + c0 + FF_CHUNK])
            hb = (g * _sigmoid(g) * u).astype(BF16)
            g_ref[:, c0:c0 + FF_CHUNK] = g.astype(BF16)
            u_ref[:, c0:c0 + FF_CHUNK] = u.astype(BF16)
            h_ref[:, c0:c0 + FF_CHUNK] = hb
            acc = acc + _nn(hb, wo_ref[c0:c0 + FF_CHUNK, :])
        r = ALPHA * xf + 0.5 * acc
        r_ref[...] = r
        xhat, _ = _ln_stats(r)
        xo_ref[...] = xhat * lg_ref[...] + lb_ref[...]
        if ng:
            @pl.when(pl.program_id(0) == last)
            def _():
                _gather_finish(comm_in, comm_out, *sems)

    row = lambda n: pl.BlockSpec((tm, n), lambda i: (i, 0))
    return pl.pallas_call(
        body, name=name, grid=(T // tm,),
        in_specs=[row(D_MODEL), _resident((D_MODEL, 2 * D_FF)), _resident((D_FF, D_MODEL)),
                  _resident((1, D_MODEL)), _resident((1, D_MODEL))] + [ANY_SPEC] * ng,
        out_specs=[row(D_MODEL), row(D_MODEL), row(D_FF), row(D_FF), row(D_FF)] + [ANY_SPEC] * ng,
        out_shape=[jax.ShapeDtypeStruct((T, D_MODEL), F32), jax.ShapeDtypeStruct((T, D_MODEL), F32),
                   jax.ShapeDtypeStruct((T, D_FF), BF16), jax.ShapeDtypeStruct((T, D_FF), BF16),
                   jax.ShapeDtypeStruct((T, D_FF), BF16)] + _gather_shapes(gather),
        scratch_shapes=_gather_sems(ng) if ng else [],
        compiler_params=_cp(1),
    )(x, w_in, w_out, lg, lb, *gather)


def _ffn_bwd(dxo, r, g, u, w_in, w_out, lg, name, exchange=()):
    T = r.shape[0]
    tm = _tile(T, FFN_BWD_TILE)
    nf = D_FF // FF_CHUNK
    ne = len(exchange)
    last = T // tm - 1

    def body(dxo_ref, r_ref, g_ref, u_ref, wi_ref, wo_ref, lg_ref, *rest):
        comm_in, (dx_ref, dgu_ref, df_ref, dlg_ref, dlb_ref) = rest[:ne], rest[ne:ne + 5]
        comm_out, sems = rest[ne + 5:2 * ne + 5], rest[2 * ne + 5:]
        i = pl.program_id(0)
        if ne:
            @pl.when(i == 0)
            def _():
                _exchange_start(comm_in, comm_out, *sems)

        dy = dxo_ref[...]
        xhat, rstd = _ln_stats(r_ref[...])
        dr = _ln_bwd(dy, xhat, rstd, lg_ref[...])

        @pl.when(i == 0)
        def _():
            dlg_ref[...] = jnp.zeros_like(dlg_ref)
            dlb_ref[...] = jnp.zeros_like(dlb_ref)

        dlg_ref[...] += _rowsum(dy * xhat)
        dlb_ref[...] += _rowsum(dy)
        dfb = (0.5 * dr).astype(BF16)
        df_ref[...] = dfb
        acc = jnp.zeros((tm, D_MODEL), F32)
        dh_ahead = _nt(dfb, wo_ref[0:FF_CHUNK, :])
        for j in range(nf):
            c0 = j * FF_CHUNK
            dh = dh_ahead
            if j + 1 < nf:
                dh_ahead = _nt(dfb, wo_ref[c0 + FF_CHUNK:c0 + 2 * FF_CHUNK, :])
            gg = g_ref[:, c0:c0 + FF_CHUNK].astype(F32)
            uu = u_ref[:, c0:c0 + FF_CHUNK].astype(F32)
            s = _sigmoid(gg)
            dgb = (dh * uu * s * (1.0 + gg * (1.0 - s))).astype(BF16)
            dub = (dh * gg * s).astype(BF16)
            dgu_ref[:, c0:c0 + FF_CHUNK] = dgb
            dgu_ref[:, D_FF + c0:D_FF + c0 + FF_CHUNK] = dub
            acc = acc + _nt(dgb, wi_ref[:, c0:c0 + FF_CHUNK]) + _nt(dub, wi_ref[:, D_FF + c0:D_FF + c0 + FF_CHUNK])
        dx_ref[...] = ALPHA * dr + acc
        if ne:
            @pl.when(i == last)
            def _():
                _exchange_finish(comm_in, comm_out, *sems)

    row = lambda n: pl.BlockSpec((tm, n), lambda i: (i, 0))
    return pl.pallas_call(
        body, name=name, grid=(T // tm,),
        in_specs=[row(D_MODEL), row(D_MODEL), row(D_FF), row(D_FF), _resident((D_MODEL, 2 * D_FF)),
                  _resident((D_FF, D_MODEL)), _resident((1, D_MODEL))] + [ANY_SPEC] * ne,
        out_specs=[row(D_MODEL), row(2 * D_FF), row(D_MODEL), _resident((1, D_MODEL)), _resident((1, D_MODEL))]
        + [ANY_SPEC] * ne,
        out_shape=[jax.ShapeDtypeStruct((T, D_MODEL), F32), jax.ShapeDtypeStruct((T, 2 * D_FF), BF16),
                   jax.ShapeDtypeStruct((T, D_MODEL), BF16), jax.ShapeDtypeStruct((1, D_MODEL), F32),
                   jax.ShapeDtypeStruct((1, D_MODEL), F32)] + _exchange_shapes(exchange),
        scratch_shapes=_exchange_sems(ne) if ne else [],
        compiler_params=_cp(1),
    )(dxo, r, g, u, w_in, w_out, lg, *exchange)


def _matmul_tn(a, b, name, by_chip=False, exchange=()):
    T, K = a.shape
    N = b.shape[1]
    tt = _tile(T, WGRAD_TOKENS)
    tn = N // N_CHIPS if by_chip else N
    while K * tn * 4 > (6 << 20) and tn % 256 == 0 and not by_chip:
        tn //= 2
    assert N % tn == 0
    ne = len(exchange)
    grid = (N // tn, T // tt)

    def body(a_ref, b_ref, *rest):
        comm_in, o_ref, comm_out, sems = rest[:ne], rest[ne], rest[ne + 1:2 * ne + 1], rest[2 * ne + 1:]
        n, t = pl.program_id(0), pl.program_id(1)
        if ne:
            @pl.when((n == 0) & (t == 0))
            def _():
                _exchange_start(comm_in, comm_out, *sems)

        @pl.when(t == 0)
        def _():
            o_ref[...] = jnp.zeros_like(o_ref)

        o_ref[...] += _tn(a_ref[...].astype(BF16), b_ref[...].astype(BF16))
        if ne:
            @pl.when((n == grid[0] - 1) & (t == grid[1] - 1))
            def _():
                _exchange_finish(comm_in, comm_out, *sems)

    res = pl.pallas_call(
        body, name=name, grid=grid,
        in_specs=[pl.BlockSpec((tt, K), lambda n, t: (t, 0)), pl.BlockSpec((tt, tn), lambda n, t: (t, n))]
        + [ANY_SPEC] * ne,
        out_specs=[pl.BlockSpec((None, K, tn), lambda n, t: (n, 0, 0)) if by_chip
                   else pl.BlockSpec((K, tn), lambda n, t: (0, n))] + [ANY_SPEC] * ne,
        out_shape=[jax.ShapeDtypeStruct((N_CHIPS, K, tn) if by_chip else (K, N), F32)] + _exchange_shapes(exchange),
        scratch_shapes=_exchange_sems(ne) if ne else [],
        compiler_params=_cp(2),
    )(a, b, *exchange)
    return res if ne else res[0]


def _matmul_tokens(at, b, name):
    M, T = at.shape
    N = b.shape[1]
    tt = _tile(T, WGRAD_T_TOKENS)

    def body(a_ref, b_ref, o_ref):
        @pl.when(pl.program_id(0) == 0)
        def _():
            o_ref[...] = jnp.zeros_like(o_ref)

        o_ref[...] += _nn(a_ref[...].astype(BF16), b_ref[...].astype(BF16))

    return pl.pallas_call(
        body, name=name, grid=(T // tt,),
        in_specs=[pl.BlockSpec((M, tt), lambda t: (0, t)), pl.BlockSpec((tt, N), lambda t: (t, 0))],
        out_specs=pl.BlockSpec((M, N), lambda t: (0, 0)),
        out_shape=jax.ShapeDtypeStruct((M, N), F32),
        compiler_params=_cp(1),
    )(at, b)


def _matmul_nn(x, w, scale, out_dtype, name, also_transposed=False):
    T, K = x.shape
    N = w.shape[1]
    tm = _tile(T, TOKEN_TILE)

    def body(x_ref, w_ref, s_ref, o_ref, *ot_ref):
        res = _nn(x_ref[...].astype(BF16), w_ref[...]) * s_ref[...]
        o_ref[...] = res.astype(out_dtype)
        if also_transposed:
            ot_ref[0][...] = res.T.astype(out_dtype)

    res = pl.pallas_call(
        body, name=name, grid=(T // tm,),
        in_specs=[pl.BlockSpec((tm, K), lambda i: (i, 0)), _resident((K, N)), _resident((1, N))],
        out_specs=[pl.BlockSpec((tm, N), lambda i: (i, 0))] + [pl.BlockSpec((N, tm), lambda i: (0, i))] * also_transposed,
        out_shape=[jax.ShapeDtypeStruct((T, N), out_dtype)] + [jax.ShapeDtypeStruct((N, T), out_dtype)] * also_transposed,
        compiler_params=_cp(1),
    )(x, w, scale)
    return res if also_transposed else res[0]


def _log_sigmoid(z):
    return jnp.minimum(z, 0.0) - jnp.log1p(jnp.exp(-jnp.abs(z)))


def _tri(n, lower):
    r = lax.broadcasted_iota(jnp.int32, (n, n), 0)
    c = lax.broadcasted_iota(jnp.int32, (n, n), 1)
    return jnp.where((c <= r) if lower else (c >= r), 1.0, 0.0).astype(F32)


def _f32dot(a, b):
    return jnp.dot(a, b, preferred_element_type=F32, precision=lax.Precision.HIGHEST)


def _forget_cumsum(flog, col, bf):
    T = flog.shape[0]
    bt = _tile(T, SCAN_BLOCK)

    def body(f_ref, b_ref, c_ref, carry):
        @pl.when(pl.program_id(0) == 0)
        def _():
            carry[...] = jnp.zeros_like(carry)

        lf = _log_sigmoid(f_ref[...] + b_ref[...])
        c = _f32dot(_tri(bt, True), lf) + carry[...]
        c_ref[...] = c * LOG2E
        carry[...] = c[bt - 1:bt, :]

    return pl.pallas_call(
        body, name="forget_cumsum", grid=(T // bt,),
        in_specs=[pl.BlockSpec((bt, N_FLOG), lambda i: (i, col)), _resident((1, N_FLOG))],
        out_specs=pl.BlockSpec((bt, N_FLOG), lambda i: (i, 0)),
        out_shape=jax.ShapeDtypeStruct((T, N_FLOG), F32),
        scratch_shapes=[pltpu.VMEM((1, N_FLOG), F32)],
        compiler_params=_cp(1),
    )(flog, bf)


def _forget_bwd(dck, dcq, flog, col, bf):
    T = dcq.shape[0]
    bt = _tile(T, SCAN_BLOCK)
    nb = T // bt

    def body(k0_ref, k1_ref, k2_ref, k3_ref, dcq_ref, f_ref, b_ref, dz_ref, db_ref, carry):
        @pl.when(pl.program_id(0) == 0)
        def _():
            carry[...] = jnp.zeros_like(carry)
            db_ref[...] = jnp.zeros_like(db_ref)

        dc = ((k0_ref[...] + k1_ref[...]) + (k2_ref[...] + k3_ref[...])) + dcq_ref[...]
        dlf = _f32dot(_tri(bt, False), dc) + carry[...]
        carry[...] = dlf[0:1, :]
        z = f_ref[...] + b_ref[...]
        dz = dlf * _sigmoid(-z)
        dz_ref[...] = dz.astype(BF16)
        db_ref[...] += _rowsum(dz)

    slab = lambda j: pl.BlockSpec((None, bt, N_FLOG), lambda i: (j, nb - 1 - i, 0))
    return pl.pallas_call(
        body, name="forget_bwd", grid=(nb,),
        in_specs=[slab(0), slab(1), slab(2), slab(3),
                  pl.BlockSpec((bt, N_FLOG), lambda i: (nb - 1 - i, 0)),
                  pl.BlockSpec((bt, N_FLOG), lambda i: (nb - 1 - i, col)), _resident((1, N_FLOG))],
        out_specs=[pl.BlockSpec((bt, N_FLOG), lambda i: (nb - 1 - i, 0)), _resident((1, N_FLOG))],
        out_shape=[jax.ShapeDtypeStruct((T, N_FLOG), BF16), jax.ShapeDtypeStruct((1, N_FLOG), F32)],
        scratch_shapes=[pltpu.VMEM((1, N_FLOG), F32)],
        compiler_params=_cp(1),
    )(dck, dck, dck, dck, dcq, flog, bf)


def _head_masks():
    lane = lax.broadcasted_iota(jnp.int32, (1, LANE), 1)
    return lane < HEAD_DIM


def _split_heads(x2, is_a):
    zero = jnp.zeros_like(x2)
    return jnp.where(is_a, x2, zero), jnp.where(is_a, zero, x2)


BIAS_PARTS = 3
ATTN_GROUP = 4
ATTN_FWD_GROUP = 8


def _bias_lanes(h):
    lane = lax.broadcasted_iota(jnp.int32, (1, LANE), 1)
    first = (1 - h) * HEAD_DIM
    return lane, first


def _fold_key_bias(qkv, c):
    T = qkv.shape[0]
    tm = _tile(T, TOKEN_TILE)
    npair = N_HEADS // 2

    def body(k_ref, c_ref, o_ref):
        cc = c_ref[...]
        parts, rest = [], cc
        for _ in range(BIAS_PARTS):
            piece = rest.astype(BF16)
            parts.append(piece)
            rest = rest - piece.astype(F32)
        for j in range(npair):
            k2 = k_ref[:, j * LANE:(j + 1) * LANE]
            for h in range(2):
                lane, first = _bias_lanes(h)
                out = k2
                for n, piece in enumerate(parts):
                    col = piece[:, 2 * j + h:2 * j + h + 1]
                    out = jnp.where(lane == first + n, col, out)
                o_ref[:, (2 * j + h) * LANE:(2 * j + h + 1) * LANE] = out

    return pl.pallas_call(
        body, name="fold_key_bias", grid=(T // tm,),
        in_specs=[pl.BlockSpec((tm, D_ATTN), lambda i: (i, 1)), pl.BlockSpec((tm, N_FLOG), lambda i: (i, 0))],
        out_specs=pl.BlockSpec((tm, 2 * D_ATTN), lambda i: (i, 0)),
        out_shape=jax.ShapeDtypeStruct((T, 2 * D_ATTN), BF16),
        compiler_params=_cp(1),
    )(qkv, c)


def _attn_fwd(qkv, vt, kb):
    T = qkv.shape[0]
    tq = _tile(T, ATTN_TILE)
    tk = tq
    nq = T // tq
    npair = N_HEADS // 2

    def body(qt_ref, ka_ref, kb_ref, vt_ref, o_ref, al_ref, m_s, l_s, acc_s):
        i = pl.program_id(1)
        dim = lax.broadcasted_iota(jnp.int32, (LANE, 1), 0)
        qs = []
        for h, qh in enumerate(_split_heads(qt_ref[...], dim < HEAD_DIM)):
            first = (1 - h) * HEAD_DIM
            qs.append(jnp.where((dim >= first) & (dim < first + BIAS_PARTS), -1.0, qh).astype(BF16))
        k_refs = (ka_ref, kb_ref)
        m_s[...] = jnp.full_like(m_s, NEG_INF)
        l_s[...] = jnp.zeros_like(l_s)
        acc_s[...] = jnp.zeros_like(acc_s)

        def scores_at(kk):
            k0 = pl.multiple_of(kk * tk, tk)
            return tuple(_nn(k_refs[h][pl.ds(k0, tk), :], qs[h]) for h in range(2))

        def consume(kk, scores, masked):
            k0 = pl.multiple_of(kk * tk, tk)
            v2t = vt_ref[:, pl.ds(k0, tk)]
            for h in range(2):
                zt = scores[h]
                if masked:
                    rr = lax.broadcasted_iota(jnp.int32, (tk, tq), 0)
                    cc = lax.broadcasted_iota(jnp.int32, (tk, tq), 1)
                    zt = jnp.where(cc >= rr, zt, NEG_INF)
                m_old = m_s[h]
                m_new = jnp.maximum(m_old, jnp.max(zt, axis=0, keepdims=True))
                p = jnp.exp2(zt - m_new)
                a = jnp.exp2(m_old - m_new)
                l_s[h] = a * l_s[h] + jnp.sum(p, axis=0, keepdims=True)
                acc_s[h] = a * acc_s[h] + _nn(v2t, p.astype(BF16))
                m_s[h] = m_new

        def group(kk, n, last_masked):
            scores = [scores_at(kk + u) for u in range(n)]
            for u in range(n):
                consume(kk + u, scores[u], last_masked and u == n - 1)

        def loop_body(t, carry):
            group(ATTN_FWD_GROUP * t, ATTN_FWD_GROUP, False)
            return carry

        lax.fori_loop(0, i // ATTN_FWD_GROUP, loop_body, 0)
        for left in range(ATTN_FWD_GROUP):
            @pl.when(i % ATTN_FWD_GROUP == left)
            def _():
                group(i - left, left + 1, True)

        outs = []
        for h in range(2):
            l = l_s[h]
            outs.append(acc_s[h] * (1.0 / l))
            al_ref[0, h:h + 1, :] = -(m_s[h] + jnp.log2(l))
        al_ref[0, 2:8, :] = jnp.zeros((6, tq), F32)
        dim = lax.broadcasted_iota(jnp.int32, (LANE, 1), 0)
        o_ref[...] = jnp.where(dim < HEAD_DIM, outs[0], outs[1]).T

    rowl = pl.BlockSpec((1, 8, tq), lambda j, i: (j, 0, i))
    return pl.pallas_call(
        body, name="attn_fwd", grid=(npair, nq),
        in_specs=[pl.BlockSpec((LANE, tq), lambda j, i: (j, i)),
                  pl.BlockSpec((T, LANE), lambda j, i: (0, 2 * j), pipeline_mode=pl.Buffered(1)),
                  pl.BlockSpec((T, LANE), lambda j, i: (0, 2 * j + 1), pipeline_mode=pl.Buffered(1)),
                  pl.BlockSpec((LANE, T), lambda j, i: (2 * npair + j, 0), pipeline_mode=pl.Buffered(1))],
        out_specs=[pl.BlockSpec((tq, LANE), lambda j, i: (i, j)), rowl],
        out_shape=[jax.ShapeDtypeStruct((T, D_ATTN), F32), jax.ShapeDtypeStruct((npair, 8, T), F32)],
        scratch_shapes=[pltpu.VMEM((2, 1, tq), F32), pltpu.VMEM((2, 1, tq), F32), pltpu.VMEM((2, LANE, tq), F32)],
        compiler_params=_cp(2),
    )(vt, kb, kb, vt)


def _attn_bwd(qkv, qkvt, dobt, cb, alrow, dlrow):
    T = qkv.shape[0]
    tq = _tile(T, ATTN_TILE)
    tk = tq
    nq = T // tq
    npair = N_HEADS // 2

    def body(qt_ref, k_ref, kt_ref, v_ref, dot_ref, cb_ref, al_ref, dl_ref,
             dq_ref, dk_ref, dv_ref, dc_ref, dcq_ref, dk_s, dv_s, dc_s):
        kj = pl.program_id(1)
        is_a = _head_masks()
        ks = _split_heads(k_ref[...], is_a)
        vs = _split_heads(v_ref[...], is_a)
        dim_a = lax.broadcasted_iota(jnp.int32, (LANE, 1), 0) < HEAD_DIM
        kts = _split_heads(kt_ref[...], dim_a)
        head_lane = lax.broadcasted_iota(jnp.int32, (1, LANE), 1) - 2 * pl.program_id(0)
        cs = tuple(jnp.sum(jnp.where(head_lane == h, cb_ref[...], 0.0), axis=-1, keepdims=True) for h in range(2))

        @pl.when(kj == 0)
        def _():
            dq_ref[...] = jnp.zeros_like(dq_ref)
            dcq_ref[...] = jnp.zeros_like(dcq_ref)

        dk_s[...] = jnp.zeros_like(dk_s)
        dv_s[...] = jnp.zeros_like(dv_s)
        dc_s[...] = jnp.zeros_like(dc_s)

        def block(qi, k_off, nk, q_off, nqs, masked):
            q0 = pl.multiple_of(qi * tq + q_off, nqs)
            rows = slice(k_off, k_off + nk)
            qt2 = qt_ref[:, pl.ds(q0, nqs)]
            dot2 = dot_ref[:, pl.ds(q0, nqs)]
            for h in range(2):
                alr = al_ref[0, h:h + 1, pl.ds(q0, nqs)]
                dlr = dl_ref[0, h:h + 1, pl.ds(q0, nqs)]
                zt = _nn(ks[h][rows], qt2) + (alr - cs[h][rows])
                if masked:
                    rr = lax.broadcasted_iota(jnp.int32, (nk, nqs), 0)
                    cc = lax.broadcasted_iota(jnp.int32, (nk, nqs), 1)
                    zt = jnp.where(cc >= rr, zt, NEG_INF)
                pt = jnp.exp2(zt)
                dst = pt * (_nn(vs[h][rows], dot2) - dlr)
                pb = pt.astype(BF16)
                dsb = dst.astype(BF16)
                dv_s[h, :, rows] += _nt(dot2, pb)
                dk_s[h, :, rows] += _nt(qt2, dsb)
                dc_s[h, rows, :] += jnp.sum(dst, axis=-1, keepdims=True)
                dcq_ref[0, h:h + 1, pl.ds(q0, nqs)] += jnp.sum(dst, axis=0, keepdims=True)
                dq_ref[:, pl.ds(q0, nqs)] += _nn(kts[h][:, rows], dsb)

        def step(qi, masked):
            if not masked:
                block(qi, 0, tk, 0, tq, False)
            elif tk % 256:
                block(qi, 0, tk, 0, tq, True)
            else:
                block(qi, 0, tk // 2, 0, tq, True)
                block(qi, tk // 2, tk // 2, tq // 2, tq // 2, True)

        rest = nq - 1 - kj
        for left in range(ATTN_GROUP):
            @pl.when(rest % ATTN_GROUP == left)
            def _():
                for u in range(left + 1):
                    step(kj + u, u == 0)

        def loop_body(t, carry):
            for u in range(ATTN_GROUP):
                step(kj + 1 + rest % ATTN_GROUP + ATTN_GROUP * t + u, False)
            return carry

        lax.fori_loop(0, rest // ATTN_GROUP, loop_body, 0)
        dk_ref[...] = (jnp.where(dim_a, dk_s[0], dk_s[1]) * (1.0 / LOG2E)).astype(BF16)
        dv_ref[...] = jnp.where(dim_a, dv_s[0], dv_s[1]).astype(BF16)
        lane = lax.broadcasted_iota(jnp.int32, (1, LANE), 1)
        head = 2 * pl.program_id(0)
        dc_ref[...] = jnp.where(lane == head, -dc_s[0], jnp.where(lane == head + 1, -dc_s[1], 0.0))

        @pl.when(kj == nq - 1)
        def _():
            dq_ref[...] = dq_ref[...] * Q_SCALE

    fullt = lambda row: pl.BlockSpec((LANE, T), lambda j, kj: (row(j), 0), pipeline_mode=pl.Buffered(1))
    tile = lambda col: pl.BlockSpec((tk, LANE), lambda j, kj: (kj, col(j)))
    tilet = lambda row: pl.BlockSpec((LANE, tk), lambda j, kj: (row(j), kj))
    rowl = pl.BlockSpec((1, 8, T), lambda j, kj: (j, 0, 0))
    return pl.pallas_call(
        body, name="attn_bwd", grid=(npair, nq),
        in_specs=[fullt(lambda j: j), tile(lambda j: npair + j), tilet(lambda j: npair + j),
                  tile(lambda j: 2 * npair + j), fullt(lambda j: j), tile(lambda j: 0), rowl, rowl],
        out_specs=[pl.BlockSpec((LANE, T), lambda j, kj: (j, 0)), tilet(lambda j: j), tilet(lambda j: j),
                   pl.BlockSpec((None, tk, LANE), lambda j, kj: (j, kj, 0)), rowl],
        out_shape=[jax.ShapeDtypeStruct((D_ATTN, T), F32), jax.ShapeDtypeStruct((D_ATTN, T), BF16),
                   jax.ShapeDtypeStruct((D_ATTN, T), BF16), jax.ShapeDtypeStruct((npair, T, LANE), F32),
                   jax.ShapeDtypeStruct((npair, 8, T), F32)],
        scratch_shapes=[pltpu.VMEM((2, LANE, tk), F32), pltpu.VMEM((2, LANE, tk), F32), pltpu.VMEM((2, tk, 1), F32)],
        compiler_params=_cp(2),
    )(qkvt, qkv, qkvt, qkv, dobt, cb, alrow, dlrow)


HALO = 8


def _shift_rows(cur, other, k, tm, down):
    row = lax.broadcasted_iota(jnp.int32, (tm, 1), 0)
    reps = tm // HALO
    if down:
        rolled = pltpu.roll(cur, k, 0)
        fill = jnp.tile(pltpu.roll(other, k, 0), (reps, 1))
        return jnp.where(row < k, fill, rolled)
    rolled = pltpu.roll(cur, tm - k, 0)
    fill = jnp.tile(pltpu.roll(other, HALO - k, 0), (reps, 1))
    return jnp.where(row >= tm - k, fill, rolled)


def _conv_fwd(c, hh, c_prev, hh_prev, w_ref, first, tm):
    u = c * hh
    u_prev = jnp.where(first, 0.0, c_prev * hh_prev)
    u1 = _shift_rows(u, u_prev, 1, tm, True)
    u2 = _shift_rows(u, u_prev, 2, tm, True)
    y = w_ref[0:1, :] * u2 + w_ref[1:2, :] * u1 + w_ref[2:3, :] * u
    return u, u1, u2, y


def _rms(x, g):
    rs = lax.rsqrt(jnp.mean(x * x, axis=-1, keepdims=True) + RMS_EPS)
    return x * rs * g, rs


def _mixer_tail_fwd(o, bchf, conv_w, g_attn, g_conv, w_mo, x1, lg, lb):
    T = o.shape[0]
    tm = _tile(T, TOKEN_TILE)
    hb = tm // HALO

    def body(o_ref, b_ref, c_ref, h_ref, cp_ref, hp_ref, w_ref, ga_ref, gc_ref, wmo_ref, x1_ref, lg_ref, lb_ref,
             x2_ref, r2_ref, mg_ref):
        first = pl.program_id(0) == 0
        _, _, _, y = _conv_fwd(c_ref[...], h_ref[...], cp_ref[...], hp_ref[...], w_ref, first, tm)
        na, _ = _rms(o_ref[...], ga_ref[...])
        nc, _ = _rms(b_ref[...] * y, gc_ref[...])
        nab = na.astype(BF16)
        ncb = nc.astype(BF16)
        mg_ref[:, 0:D_ATTN] = nab
        mg_ref[:, D_ATTN:] = ncb
        r2 = ALPHA * x1_ref[...] + _nn(nab, wmo_ref[0:D_ATTN, :]) + _nn(ncb, wmo_ref[D_ATTN:, :])
        r2_ref[...] = r2
        xhat, _ = _ln_stats(r2)
        x2_ref[...] = xhat * lg_ref[...] + lb_ref[...]

    row = lambda n, col=0: pl.BlockSpec((tm, n), lambda i: (i, col))
    prev = lambda col: pl.BlockSpec((HALO, D_CONV), lambda i: (jnp.maximum(i * hb - 1, 0), col))
    return pl.pallas_call(
        body, name="mixer_tail_fwd", grid=(T // tm,),
        in_specs=[row(D_ATTN), row(D_CONV, 0), row(D_CONV, 1), row(D_CONV, 2), prev(1), prev(2),
                  _resident((3, D_CONV)), _resident((1, D_ATTN)), _resident((1, D_CONV)),
                  _resident((D_MODEL, D_MODEL)), row(D_MODEL), _resident((1, D_MODEL)), _resident((1, D_MODEL))],
        out_specs=[row(D_MODEL), row(D_MODEL), row(D_MODEL)],
        out_shape=[jax.ShapeDtypeStruct((T, D_MODEL), F32), jax.ShapeDtypeStruct((T, D_MODEL), F32),
                   jax.ShapeDtypeStruct((T, D_MODEL), BF16)],
        compiler_params=_cp(1),
    )(o, bchf, bchf, bchf, bchf, bchf, conv_w, g_attn, g_conv, w_mo, x1, lg, lb)


def _head_sum_rows():
    row = lax.broadcasted_iota(jnp.int32, (4 * 8, D_ATTN), 0)
    head = lax.broadcasted_iota(jnp.int32, (4 * 8, D_ATTN), 1) // HEAD_DIM
    return jnp.where((row % 8 < 2) & (2 * (row // 8) + row % 8 == head), 1.0, 0.0).astype(F32)


def _mixer_tail_bwd(dx2, r2, lg, w_mo, o, bchf, conv_w, g_attn, g_conv, swap=()):
    T = o.shape[0]
    tm = _tile(T, TOKEN_TILE)
    hb = tm // HALO
    ns = len(swap)
    last = T // tm - 1

    def body(dx2_ref, r2_ref, lg_ref, wmo_ref, o_ref, b_ref, c_ref, h_ref, cp_ref, hp_ref, w_ref, ga_ref, gc_ref,
             *rest):
        comm_in = rest[:ns]
        dx1_ref, dr_ref, dot_ref, dl_ref, dco_ref, dlg_ref, dlb_ref, dga_ref, dgc_ref = rest[ns:ns + 9]
        comm_out, sems = rest[ns + 9:2 * ns + 9], rest[2 * ns + 9:]
        i = pl.program_id(0)
        if ns:
            @pl.when(i == 0)
            def _():
                _swap_start(comm_in, comm_out, *sems)

            @pl.when(i == last)
            def _():
                _swap_finish(comm_in, comm_out, *sems)

        @pl.when(i == 0)
        def _():
            for ref in (dlg_ref, dlb_ref, dga_ref, dgc_ref):
                ref[...] = jnp.zeros_like(ref)

        dy = dx2_ref[...]
        xhat, rstd = _ln_stats(r2_ref[...])
        dr = _ln_bwd(dy, xhat, rstd, lg_ref[...])
        dlg_ref[...] += _rowsum(dy * xhat)
        dlb_ref[...] += _rowsum(dy)
        dx1_ref[...] = ALPHA * dr
        drb = dr.astype(BF16)
        dr_ref[...] = drb
        dna = _nt(drb, wmo_ref[0:D_ATTN, :])
        dnc = _nt(drb, wmo_ref[D_ATTN:, :])

        def rms_bwd(x, g, dn):
            rs = lax.rsqrt(jnp.mean(x * x, axis=-1, keepdims=True) + RMS_EPS)
            dng = dn * g
            dx = rs * dng - x * (rs * rs * rs) * jnp.mean(dng * x, axis=-1, keepdims=True)
            return dx, _rowsum(dn * x * rs)

        oo = o_ref[...]
        do, dga = rms_bwd(oo, ga_ref[...], dna)
        dga_ref[...] += dga
        dot_ref[...] = do.T.astype(BF16)
        dl_ref[...] = lax.dot_general(_head_sum_rows(), do * oo, (((1,), (1,)), ((), ())),
                                      preferred_element_type=F32, precision=lax.Precision.HIGHEST)
        _, _, _, y = _conv_fwd(c_ref[...], h_ref[...], cp_ref[...], hp_ref[...], w_ref, i == 0, tm)
        dco, dgc = rms_bwd(b_ref[...] * y, gc_ref[...], dnc)
        dgc_ref[...] += dgc
        dco_ref[...] = dco

    row = lambda n, col=0: pl.BlockSpec((tm, n), lambda i: (i, col))
    prev = lambda col: pl.BlockSpec((HALO, D_CONV), lambda i: (jnp.maximum(i * hb - 1, 0), col))
    vec = lambda n: _resident((1, n))
    return pl.pallas_call(
        body, name="mixer_tail_bwd", grid=(T // tm,),
        in_specs=[row(D_MODEL), row(D_MODEL), vec(D_MODEL), _resident((D_MODEL, D_MODEL)), row(D_ATTN),
                  row(D_CONV, 0), row(D_CONV, 1), row(D_CONV, 2), prev(1), prev(2), _resident((3, D_CONV)),
                  vec(D_ATTN), vec(D_CONV)] + [ANY_SPEC] * ns,
        out_specs=[row(D_MODEL), row(D_MODEL), pl.BlockSpec((D_ATTN, tm), lambda i: (0, i)),
                   pl.BlockSpec((4 * 8, tm), lambda i: (0, i)), row(D_CONV),
                   vec(D_MODEL), vec(D_MODEL), vec(D_ATTN), vec(D_CONV)] + [ANY_SPEC] * ns,
        out_shape=[jax.ShapeDtypeStruct((T, D_MODEL), F32), jax.ShapeDtypeStruct((T, D_MODEL), BF16),
                   jax.ShapeDtypeStruct((D_ATTN, T), BF16),
                   jax.ShapeDtypeStruct((4 * 8, T), F32),
                   jax.ShapeDtypeStruct((T, D_CONV), F32), jax.ShapeDtypeStruct((1, D_MODEL), F32),
                   jax.ShapeDtypeStruct((1, D_MODEL), F32), jax.ShapeDtypeStruct((1, D_ATTN), F32),
                   jax.ShapeDtypeStruct((1, D_CONV), F32)] + _swap_shapes(swap),
        scratch_shapes=_swap_sems(ns) if ns else [],
        compiler_params=_cp(1),
    )(dx2, r2, lg, w_mo, o, bchf, bchf, bchf, bchf, bchf, conv_w, g_attn, g_conv, *swap)


def _conv_bwd(dco, bchf, conv_w):
    T = dco.shape[0]
    tm = _tile(T, TOKEN_TILE)
    hb = tm // HALO
    nt = T // tm

    def body(dco_ref, dcon_ref, b_ref, bn_ref, c_ref, h_ref, cp_ref, hp_ref, w_ref, dbch_ref, dw_ref):
        i = pl.program_id(0)

        @pl.when(i == 0)
        def _():
            dw_ref[...] = jnp.zeros_like(dw_ref)

        cc = c_ref[...]
        hh = h_ref[...]
        u, u1, u2, y = _conv_fwd(cc, hh, cp_ref[...], hp_ref[...], w_ref, i == 0, tm)
        dco = dco_ref[...]
        bb = b_ref[...]
        dyc = dco * bb
        dy_next = jnp.where(i == nt - 1, 0.0, dcon_ref[...] * bn_ref[...])
        d1 = _shift_rows(dyc, dy_next, 1, tm, False)
        d2 = _shift_rows(dyc, dy_next, 2, tm, False)
        du = w_ref[2:3, :] * dyc + w_ref[1:2, :] * d1 + w_ref[0:1, :] * d2
        dbch_ref[:, 0:D_CONV] = (dco * y).astype(BF16)
        dbch_ref[:, D_CONV:2 * D_CONV] = (du * hh).astype(BF16)
        dbch_ref[:, 2 * D_CONV:] = (du * cc).astype(BF16)
        dw_ref[0:1, :] += _rowsum(dyc * u2)
        dw_ref[1:2, :] += _rowsum(dyc * u1)
        dw_ref[2:3, :] += _rowsum(dyc * u)

    row = lambda n, col=0: pl.BlockSpec((tm, n), lambda i: (i, col))
    prev = lambda col: pl.BlockSpec((HALO, D_CONV), lambda i: (jnp.maximum(i * hb - 1, 0), col))
    nxt = lambda col: pl.BlockSpec((HALO, D_CONV), lambda i: (jnp.minimum((i + 1) * hb, T // HALO - 1), col))
    return pl.pallas_call(
        body, name="conv_bwd", grid=(nt,),
        in_specs=[row(D_CONV), nxt(0), row(D_CONV, 0), nxt(0), row(D_CONV, 1), row(D_CONV, 2), prev(1), prev(2),
                  _resident((3, D_CONV))],
        out_specs=[row(3 * D_CONV), _resident((8, D_CONV))],
        out_shape=[jax.ShapeDtypeStruct((T, 3 * D_CONV), BF16), jax.ShapeDtypeStruct((8, D_CONV), F32)],
        compiler_params=_cp(1),
    )(dco, dco, bchf, bchf, bchf, bchf, bchf, bchf, conv_w)


def _mixer_in_bwd(dx1a, dqt, dkt, dvt, dbch, dfl, w_qkvt, w_bch, w_f):
    T = dx1a.shape[0]
    tm = _tile(T, TOKEN_TILE)

    def body(a_ref, dq_ref, dk_ref, dv_ref, db_ref, df_ref, wq_ref, wb_ref, wf_ref, o_ref):
        acc = a_ref[...] + _nt(db_ref[...], wb_ref[...]) + _nt(df_ref[...], wf_ref[...])
        for n, ref in enumerate((dq_ref, dk_ref, dv_ref)):
            acc = acc + _tn(ref[...].astype(BF16), wq_ref[n * D_ATTN:(n + 1) * D_ATTN, :])
        o_ref[...] = acc

    row = lambda n: pl.BlockSpec((tm, n), lambda i: (i, 0))
    col = pl.BlockSpec((D_ATTN, tm), lambda i: (0, i))
    return pl.pallas_call(
        body, name="mixer_in_bwd", grid=(T // tm,),
        in_specs=[row(D_MODEL), col, col, col, row(3 * D_CONV), row(N_FLOG),
                  _resident((3 * D_ATTN, D_MODEL)), _resident((D_MODEL, 3 * D_CONV)), _resident((D_MODEL, N_FLOG))],
        out_specs=row(D_MODEL),
        out_shape=jax.ShapeDtypeStruct((T, D_MODEL), F32),
        compiler_params=_cp(1),
    )(dx1a, dqt, dkt, dvt, dbch, dfl, w_qkvt, w_bch, w_f)


def _ple_loss(x3, p, tgt, w_g, w_p, b_g, lg, lb):
    T = x3.shape[0]
    tm = _tile(T, TOKEN_TILE)

    def body(x_ref, p_ref, t_ref, wg_ref, wp_ref, bg_ref, lg_ref, lb_ref,
             dx_ref, de_ref, dz_ref, loss_ref, dlg_ref, dlb_ref, dbg_ref):
        @pl.when(pl.program_id(0) == 0)
        def _():
            for ref in (loss_ref, dlg_ref, dlb_ref, dbg_ref):
                ref[...] = jnp.zeros_like(ref)

        xf = x_ref[...]
        gate = _sigmoid(_nn(xf.astype(BF16), wg_ref[...]) + bg_ref[...])
        e = _nn(p_ref[...].astype(BF16), wp_ref[...])
        xhat, rstd = _ln_stats(ALPHA * xf + gate * e)
        err = xhat * lg_ref[...] + lb_ref[...] - t_ref[...]
        sq = jnp.sum(_rowsum(err * err), axis=-1, keepdims=True)
        loss_ref[...] += jnp.broadcast_to(sq * (0.5 / D_MODEL), loss_ref.shape)
        dy = err * (1.0 / D_MODEL)
        dr = _ln_bwd(dy, xhat, rstd, lg_ref[...])
        dlg_ref[...] += _rowsum(dy * xhat)
        dlb_ref[...] += _rowsum(dy)
        de_ref[...] = (dr * gate).astype(BF16)
        dz = dr * e * gate * (1.0 - gate)
        dbg_ref[...] += _rowsum(dz)
        dzb = dz.astype(BF16)
        dz_ref[...] = dzb
        dx_ref[...] = ALPHA * dr + _nt(dzb, wg_ref[...])

    row = lambda n: pl.BlockSpec((tm, n), lambda i: (i, 0))
    vec = lambda n: _resident((1, n))
    return pl.pallas_call(
        body, name="ple_loss", grid=(T // tm,),
        in_specs=[row(D_MODEL), row(PLE_DIM), row(D_MODEL), _resident((D_MODEL, D_MODEL)),
                  _resident((PLE_DIM, D_MODEL)), vec(D_MODEL), vec(D_MODEL), vec(D_MODEL)],
        out_specs=[row(D_MODEL), row(D_MODEL), row(D_MODEL), vec(LANE), vec(D_MODEL), vec(D_MODEL), vec(D_MODEL)],
        out_shape=[jax.ShapeDtypeStruct((T, D_MODEL), F32), jax.ShapeDtypeStruct((T, D_MODEL), BF16),
                   jax.ShapeDtypeStruct((T, D_MODEL), BF16), jax.ShapeDtypeStruct((1, LANE), F32),
                   jax.ShapeDtypeStruct((1, D_MODEL), F32), jax.ShapeDtypeStruct((1, D_MODEL), F32),
                   jax.ShapeDtypeStruct((1, D_MODEL), F32)],
        compiler_params=_cp(1),
    )(x3, p, tgt, w_g, w_p, b_g, lg, lb)


def _from_row_layout(vr):
    return vr[:, :2, :].reshape(N_HEADS, -1).T


def _local_step(x, p, tgt, w, overlap=None):
    bf = lambda a: a.astype(BF16)
    w1i, w1o = bf(w["ffn1_w_in"]), bf(w["ffn1_w_out"])
    first = _ffn_fwd(x, w1i, w1o, w["ln1_g"], w["ln1_b"], "ffn1_fwd", overlap["gather"] if overlap else ())
    x1, r1, g1, u1, h1 = first[:5]
    if overlap:
        w = {**w, **overlap["weights"](first[5:])}
    w2i, w2o = bf(w["ffn2_w_in"]), bf(w["ffn2_w_out"])
    wmi = w["w_mix_in"]
    o_f = 3 * D_ATTN
    o_b = o_f + N_HEADS
    w_qkv = bf(wmi[:, :o_f])
    w_f = bf(jnp.pad(wmi[:, o_f:o_b], ((0, 0), (0, N_FLOG - N_HEADS))))
    w_bch = bf(wmi[:, o_b:])
    w_bchf = jnp.concatenate([w_bch, w_f], axis=1)
    w_mo, w_g, w_p = bf(w["w_mix_out"]), bf(w["w_ple_gate"]), bf(w["w_ple"])
    b_f = jnp.pad(w["b_forget"], ((0, 0), (0, N_FLOG - N_HEADS)))

    q_scale = jnp.concatenate([jnp.full((1, D_ATTN), Q_SCALE * LOG2E, F32), jnp.ones((1, 2 * D_ATTN), F32)], axis=1)
    qkv, qkvt = _matmul_nn(x1, w_qkv, q_scale, BF16, "proj_qkv", also_transposed=True)
    bchf = _matmul_nn(x1, w_bchf, jnp.ones((1, 3 * D_CONV + N_FLOG), F32), F32, "proj_bchf")
    fcol = 3 * D_CONV // N_FLOG
    c = _forget_cumsum(bchf, fcol, b_f)
    o, alrow = _attn_fwd(qkv, qkvt, _fold_key_bias(qkv, c))
    x2, r2, merged = _mixer_tail_fwd(o, bchf, w["conv_w"], w["g_attn"], w["g_conv"], w_mo, x1, w["ln2_g"], w["ln2_b"])
    x3, r3, g2, u2, h2 = _ffn_fwd(x2, w2i, w2o, w["ln3_g"], w["ln3_b"], "ffn2_fwd")

    grads = {}
    dx3, de, dz, loss, grads["ln4_g"], grads["ln4_b"], grads["b_ple_gate"] = _ple_loss(
        x3, p, tgt, w_g, w_p, w["b_ple_gate"], w["ln4_g"], w["ln4_b"])
    by_chip = overlap is not None
    grads["w_ple"] = _matmul_tn(p, de, "dw_ple")
    grads["w_ple_gate"] = _matmul_tn(x3, dz, "dw_ple_gate")

    dx2, dgu2, df2, grads["ln3_g"], grads["ln3_b"] = _ffn_bwd(dx3, r3, g2, u2, w2i, w2o, w["ln3_g"], "ffn2_bwd")
    grads["ffn2_w_in"] = _matmul_tn(x2, dgu2, "dw_ffn2_in", by_chip)
    grads["ffn2_w_out"] = _matmul_tn(h2, df2, "dw_ffn2_out")

    to_swap = overlap["swap"](grads) if overlap else ()
    tail = _mixer_tail_bwd(dx2, r2, w["ln2_g"], w_mo, o, bchf, w["conv_w"], w["g_attn"], w["g_conv"], to_swap)
    (dx1a, dr2, dobt, delta, dco, grads["ln2_g"], grads["ln2_b"], grads["g_attn"], grads["g_conv"]) = tail[:9]
    grads["w_mix_out"] = _matmul_tn(merged, dr2, "dw_mix_out")
    dbch, dcw = _conv_bwd(dco, bchf, w["conv_w"])
    grads["conv_w"] = dcw[:3]
    dqt, dkt, dvt, dck, dcq = _attn_bwd(qkv, qkvt, dobt, c, alrow, delta.reshape(N_HEADS // 2, 8, -1))
    dcq_lanes = jnp.pad(_from_row_layout(dcq), ((0, 0), (0, N_FLOG - N_HEADS)))
    dfl, dbf = _forget_bwd(dck, dcq_lanes, bchf, fcol, b_f)
    grads["b_forget"] = dbf[:, :N_HEADS]
    dx1 = _mixer_in_bwd(dx1a, dqt, dkt, dvt, dbch, dfl, w_qkv.T, w_bch, w_f)
    grads["w_mix_in"] = jnp.concatenate(
        [_matmul_tokens(dqt, x1, "dw_q").T, _matmul_tokens(dkt, x1, "dw_k").T, _matmul_tokens(dvt, x1, "dw_v").T,
         _matmul_tn(x1, dfl, "dw_flog")[:, :N_HEADS], _matmul_tn(x1, dbch, "dw_bch")], axis=1)

    dx0, dgu1, df1, grads["ln1_g"], grads["ln1_b"] = _ffn_bwd(dx1, r1, g1, u1, w1i, w1o, w["ln1_g"], "ffn1_bwd")
    grads["ffn1_w_out"] = _matmul_tn(h1, df1, "dw_ffn1_out")
    if not overlap:
        grads["ffn1_w_in"] = _matmul_tn(x, dgu1, "dw_ffn1_in")
        return loss, dx0, grads
    sums = overlap["chip_sums"](grads, to_swap, tail[9:])
    grads["ffn1_w_in"], *received = _matmul_tn(x, dgu1, "dw_ffn1_in", by_chip, exchange=sums)
    return loss, dx0, grads, sums, received


WEIGHTS = ["ffn1_w_in", "ffn1_w_out", "ln1_g", "ln1_b", "w_mix_in", "b_forget", "conv_w", "g_attn", "g_conv",
           "w_mix_out", "ln2_g", "ln2_b", "ffn2_w_in", "ffn2_w_out", "ln3_g", "ln3_b", "w_ple", "w_ple_gate",
           "b_ple_gate", "ln4_g", "ln4_b"]
LAYOUT = {
    "ffn1_w_in": ((D_MODEL, 2 * D_FF), 1), "ffn1_w_out": ((D_FF, D_MODEL), 0),
    "w_mix_in": ((D_MODEL, 3 * D_ATTN + N_HEADS + 3 * D_CONV), 1), "conv_w": ((3, D_CONV), 1),
    "w_mix_out": ((D_MODEL, D_MODEL), 0), "ffn2_w_in": ((D_MODEL, 2 * D_FF), 1), "ffn2_w_out": ((D_FF, D_MODEL), 0),
    "w_ple": ((PLE_DIM, D_MODEL), 1), "w_ple_gate": ((D_MODEL, D_MODEL), 0),
    "ln1_g": ((1, D_MODEL), None), "ln1_b": ((1, D_MODEL), None), "b_forget": ((1, N_HEADS), None),
    "g_attn": ((1, D_ATTN), None), "g_conv": ((1, D_CONV), None), "ln2_g": ((1, D_MODEL), None),
    "ln2_b": ((1, D_MODEL), None), "ln3_g": ((1, D_MODEL), None), "ln3_b": ((1, D_MODEL), None),
    "b_ple_gate": ((1, D_MODEL), None), "ln4_g": ((1, D_MODEL), None), "ln4_b": ((1, D_MODEL), None),
}
BIG = [n for n in WEIGHTS if LAYOUT[n][1] is not None and n != "conv_w"]
SMALL = [n for n in WEIGHTS if n not in BIG]
ROW = 1024
SMALL_ROWS = 16


def _shard_shape(name):
    shape, axis = LAYOUT[name]
    if axis is None:
        return shape
    return tuple(s // N_CHIPS if a == axis else s for a, s in enumerate(shape))


def _halves(a):
    return a.reshape(a.shape[:-2] + (2, a.shape[-2] // 2, a.shape[-1]))


def _split_chips(name, full):
    shape, axis = LAYOUT[name]
    if axis == 0:
        return full.reshape((N_CHIPS, shape[0] // N_CHIPS) + shape[1:])
    return jnp.moveaxis(full.reshape(shape[:1] + (N_CHIPS, shape[1] // N_CHIPS)), 1, 0)


def _join_chips(name, parts):
    shape, axis = LAYOUT[name]
    if axis == 0:
        return parts.reshape(shape)
    return jnp.moveaxis(parts, 0, 1).reshape(shape)


SMALL_AT = {"ln1_g": (0, 0), "ln1_b": (1, 0), "ln2_g": (2, 0), "ln2_b": (3, 0), "ln3_g": (4, 0), "ln3_b": (5, 0),
            "b_ple_gate": (6, 0), "ln4_g": (7, 0), "ln4_b": (8, 0), "g_attn": (9, 0), "g_conv": (9, D_ATTN),
            "b_forget": (10, 0), "conv_w": (10, LANE)}
CONV_SHARD = D_CONV // N_CHIPS


def _pack_small_grads(grads):
    def body(*refs):
        ins, o_ref = dict(zip(SMALL, refs[:-1])), refs[-1]
        o_ref[...] = jnp.zeros_like(o_ref)
        for s in range(N_CHIPS):
            for n in SMALL:
                r, c0 = SMALL_AT[n]
                if n == "conv_w":
                    for k in range(3):
                        o_ref[s, r:r + 1, c0 + k * CONV_SHARD:c0 + (k + 1) * CONV_SHARD] = (
                            ins[n][k:k + 1, s * CONV_SHARD:(s + 1) * CONV_SHARD])
                else:
                    o_ref[s, r:r + 1, c0:c0 + ins[n].shape[1]] = ins[n][...]

    return pl.pallas_call(
        body, name="pack_small_grads",
        out_shape=jax.ShapeDtypeStruct((N_CHIPS, SMALL_ROWS, ROW), F32),
    )(*[grads[n] for n in SMALL])


def _adamw_math(w, g, m, v):
    c1 = 1.0 - ADAM_B1 ** ADAM_STEP
    c2 = 1.0 - ADAM_B2 ** ADAM_STEP
    m = ADAM_B1 * m + (1.0 - ADAM_B1) * g
    v = ADAM_B2 * v + (1.0 - ADAM_B2) * (g * g)
    return -ADAM_LR * ((m / c1) / (jnp.sqrt(v / c2) + ADAM_EPS) + ADAM_WD * w), m, v


def _adamw_small(g_mine, g_sib, c_idx, w, m, v):
    ns = len(SMALL)

    def body(c_ref, gm_ref, gs_ref, *refs):
        ws, ms, vs = refs[:ns], refs[ns:2 * ns], refs[2 * ns:3 * ns]
        outs = refs[3 * ns:]
        mine_first = c_ref[0] == 0
        top = jnp.where(mine_first, gm_ref[...], gs_ref[...])
        bot = jnp.where(mine_first, gs_ref[...], gm_ref[...])
        for i, n in enumerate(SMALL):
            r, c0 = SMALL_AT[n]
            blk, rr = (top, r) if r < SMALL_ROWS // 2 else (bot, r - SMALL_ROWS // 2)
            rows, width = ws[i].shape
            for k in range(rows):
                g = blk[rr:rr + 1, c0 + k * width:c0 + (k + 1) * width]
                d, mn, vn = _adamw_math(ws[i][k:k + 1, :], g, ms[i][k:k + 1, :], vs[i][k:k + 1, :])
                for q, val in enumerate((g, d, mn, vn)):
                    outs[q * ns + i][k:k + 1, :] = val

    shapes = [jax.ShapeDtypeStruct(a.shape, F32) for a in w]
    vmem = pl.BlockSpec(memory_space=pltpu.VMEM)
    res = pl.pallas_call(
        body, name="adamw_small",
        in_specs=[pl.BlockSpec(memory_space=pltpu.SMEM)] + [vmem] * (2 + 3 * ns),
        out_specs=[vmem] * (4 * ns),
        out_shape=shapes * 4,
    )(c_idx, g_mine, g_sib, *w, *m, *v)
    return [res[q * ns:(q + 1) * ns] for q in range(4)]


def _place():
    x, y, c = lax.axis_index("x"), lax.axis_index("y"), lax.axis_index("c")
    others = [(1 - x, y), (x, 1 - y), (1 - x, 1 - y)]
    return x, y, c, others


ANY_SPEC = pl.BlockSpec(memory_space=pl.ANY)


def _remote(src, dst, send_sems, recv_sems, k, to):
    return pltpu.make_async_remote_copy(src_ref=src, dst_ref=dst, send_sem=send_sems.at[k], recv_sem=recv_sems.at[k],
                                        device_id=to, device_id_type=MESH)


def _all_gather(shards):
    n = len(shards)

    def body(*refs):
        ins, outs, send_sems, recv_sems = refs[:n], refs[n:2 * n], refs[2 * n], refs[2 * n + 1]
        _gather_start(ins, outs, send_sems, recv_sems)
        _gather_finish(ins, outs, send_sems, recv_sems)

    return pl.pallas_call(
        body, name="all_gather_weights",
        out_shape=_gather_shapes(shards), in_specs=[ANY_SPEC] * n, out_specs=[ANY_SPEC] * n,
        scratch_shapes=_gather_sems(n),
    )(*shards)


def _gather_shapes(shards):
    return [jax.ShapeDtypeStruct((N_CHIPS,) + a.shape, a.dtype) for a in shards]


def _gather_sems(n):
    return [pltpu.SemaphoreType.DMA((6 * n,)), pltpu.SemaphoreType.DMA((6 * n,))]


def _gather_sends(ins, outs, send_sems, recv_sems):
    x, y, c, others = _place()
    s = 2 * x + y
    return [_remote(ins[t].at[c], outs[t].at[s, c], send_sems, recv_sems, 6 * t + j, (*chip, c))
            for t in range(len(ins)) for j, chip in enumerate(others)]


def _gather_start(ins, outs, send_sems, recv_sems):
    for cp in _gather_sends(ins, outs, send_sems, recv_sems):
        cp.start()


def _gather_finish(ins, outs, send_sems, recv_sems):
    x, y, c, others = _place()
    slot = lambda t, chip, half: outs[t].at[2 * chip[0] + chip[1], half]
    passed = []
    for t in range(len(ins)):
        for j, chip in enumerate(others):
            landed = slot(t, chip, c)
            _remote(landed, landed, send_sems, recv_sems, 6 * t + j, (x, y, c)).wait_recv()
            passed.append(_remote(landed, landed, send_sems, recv_sems, 6 * t + 3 + j, (x, y, 1 - c)))
            passed[-1].start()
    for t in range(len(ins)):
        for j, chip in enumerate(others):
            landed = slot(t, chip, 1 - c)
            _remote(landed, landed, send_sems, recv_sems, 6 * t + 3 + j, (x, y, c)).wait_recv()
    for cp in _gather_sends(ins, outs, send_sems, recv_sems) + passed:
        cp.wait_send()


def _swap_halves(gs, tag):
    n = len(gs)

    def body(*refs):
        ins, outs, send_sems, recv_sems = refs[:n], refs[n:2 * n], refs[2 * n], refs[2 * n + 1]
        _swap_start(ins, outs, send_sems, recv_sems)
        _swap_finish(ins, outs, send_sems, recv_sems)

    return pl.pallas_call(
        body, name="grad_swap_halves_" + tag,
        out_shape=_swap_shapes(gs), in_specs=[ANY_SPEC] * n, out_specs=[ANY_SPEC] * n,
        scratch_shapes=_swap_sems(n),
    )(*gs)


def _swap_shapes(gs):
    return [jax.ShapeDtypeStruct(g.shape[:1] + g.shape[2:], g.dtype) for g in gs]


def _swap_sems(n):
    return [pltpu.SemaphoreType.DMA((n,)), pltpu.SemaphoreType.DMA((n,))]


def _swap_copies(ins, outs, send_sems, recv_sems):
    x, y, c, _ = _place()
    return [_remote(ins[t].at[:, 1 - c], outs[t], send_sems, recv_sems, t, (x, y, 1 - c)) for t in range(len(ins))]


def _swap_start(ins, outs, send_sems, recv_sems):
    for cp in _swap_copies(ins, outs, send_sems, recv_sems):
        cp.start()


def _swap_finish(ins, outs, send_sems, recv_sems):
    for cp in _swap_copies(ins, outs, send_sems, recv_sems):
        cp.wait()


def _exchange_chips(pps):
    n = len(pps)

    def body(*refs):
        ins, outs, send_sems, recv_sems = refs[:n], refs[n:2 * n], refs[2 * n], refs[2 * n + 1]
        _exchange_start(ins, outs, send_sems, recv_sems)
        _exchange_finish(ins, outs, send_sems, recv_sems)

    return pl.pallas_call(
        body, name="grad_exchange_chips",
        out_shape=_exchange_shapes(pps), in_specs=[ANY_SPEC] * n, out_specs=[ANY_SPEC] * n,
        scratch_shapes=_exchange_sems(n),
    )(*pps)


def _exchange_shapes(pps):
    return [jax.ShapeDtypeStruct(p.shape, p.dtype) for p in pps]


def _exchange_sems(n):
    return [pltpu.SemaphoreType.DMA((3 * n,)), pltpu.SemaphoreType.DMA((3 * n,))]


def _exchange_sends(ins, outs, send_sems, recv_sems):
    x, y, c, others = _place()
    s = 2 * x + y
    return [_remote(ins[t].at[2 * chip[0] + chip[1]], outs[t].at[s], send_sems, recv_sems, 3 * t + j, (*chip, c))
            for t in range(len(ins)) for j, chip in enumerate(others)]


def _exchange_start(ins, outs, send_sems, recv_sems):
    for cp in _exchange_sends(ins, outs, send_sems, recv_sems):
        cp.start()


def _exchange_finish(ins, outs, send_sems, recv_sems):
    x, y, c, others = _place()
    for t in range(len(ins)):
        for j, chip in enumerate(others):
            landed = outs[t].at[2 * chip[0] + chip[1]]
            _remote(landed, landed, send_sems, recv_sems, 3 * t + j, (x, y, c)).wait_recv()
    for cp in _exchange_sends(ins, outs, send_sems, recv_sems):
        cp.wait_send()


def _share_half(rs):
    n = len(rs)

    def body(*refs):
        ins, outs, send_sems, recv_sems = refs[:n], refs[n:2 * n], refs[2 * n], refs[2 * n + 1]
        x, y, c, _ = _place()
        copies = [_remote(ins[t], outs[t], send_sems, recv_sems, t, (x, y, 1 - c)) for t in range(n)]
        for cp in copies:
            cp.start()
        for cp in copies:
            cp.wait()

    return pl.pallas_call(
        body, name="grad_share_half",
        out_shape=[jax.ShapeDtypeStruct(r.shape, r.dtype) for r in rs],
        in_specs=[ANY_SPEC] * n, out_specs=[ANY_SPEC] * n,
        scratch_shapes=[pltpu.SemaphoreType.DMA((n,)), pltpu.SemaphoreType.DMA((n,))],
    )(*rs)


ELEMENTWISE_BLOCK_BYTES = 1 << 20


def _row_tile(rows, cols):
    return _tile(rows, max(8, ELEMENTWISE_BLOCK_BYTES // (4 * cols) // 8 * 8))


def _add_my_half(g, sib, c_idx, name):
    rh, cols = g.shape[2:]
    tr = _row_tile(rh, cols)

    def body(c_ref, g_ref, s_ref, o_ref):
        o_ref[...] = (g_ref[...] + s_ref[...]).astype(BF16)

    return pl.pallas_call(
        body, name="grad_add_halves_" + name,
        grid_spec=pltpu.PrefetchScalarGridSpec(
            num_scalar_prefetch=1, grid=(N_CHIPS, rh // tr),
            in_specs=[pl.BlockSpec((None, None, tr, cols), lambda s, i, c: (s, c[0], i, 0)),
                      pl.BlockSpec((None, tr, cols), lambda s, i, c: (s, i, 0))],
            out_specs=pl.BlockSpec((None, tr, cols), lambda s, i, c: (s, i, 0))),
        out_shape=jax.ShapeDtypeStruct((N_CHIPS, rh, cols), BF16),
        compiler_params=_cp(2),
    )(c_idx, g, sib)


def _sum_chips(parts, pp, s_idx, name):
    rh, cols = parts.shape[1:]
    tr = _row_tile(rh, cols)

    def body(s_ref, p0, p1, p2, p3, mine_ref, o_ref):
        own = mine_ref[...]
        t = [jnp.where(s_ref[0] == k, own, p[...]).astype(F32) for k, p in enumerate((p0, p1, p2, p3))]
        o_ref[...] = ((t[0] + t[1]) + t[2]) + t[3]

    slot = lambda k: pl.BlockSpec((None, tr, cols), lambda i, s: (jnp.where(s[0] == k, (k + 1) % N_CHIPS, k), i, 0))
    return pl.pallas_call(
        body, name="grad_sum_chips_" + name,
        grid_spec=pltpu.PrefetchScalarGridSpec(
            num_scalar_prefetch=1, grid=(rh // tr,),
            in_specs=[slot(0), slot(1), slot(2), slot(3), pl.BlockSpec((None, tr, cols), lambda i, s: (s[0], i, 0))],
            out_specs=pl.BlockSpec((tr, cols), lambda i, s: (i, 0))),
        out_shape=jax.ShapeDtypeStruct((rh, cols), F32),
        compiler_params=_cp(1),
    )(s_idx, parts, parts, parts, parts, pp)


def _adamw(w, g_mine, g_sib, m, v, c_idx, name):
    rows, cols = w.shape
    tr = _row_tile(rows // 2, cols)
    nbh = rows // 2 // tr

    def body(c_ref, w_ref, gm_ref, gs_ref, m_ref, v_ref, g_ref, d_ref, mo_ref, vo_ref):
        g = jnp.where(pl.program_id(0) // nbh == c_ref[0], gm_ref[...], gs_ref[...])
        g_ref[...] = g
        d_ref[...], mo_ref[...], vo_ref[...] = _adamw_math(w_ref[...], g, m_ref[...], v_ref[...])

    spec = pl.BlockSpec((tr, cols), lambda i, c: (i, 0))
    half = pl.BlockSpec((tr, cols), lambda i, c: (i % nbh, 0))
    return pl.pallas_call(
        body, name="adamw_" + name,
        grid_spec=pltpu.PrefetchScalarGridSpec(
            num_scalar_prefetch=1, grid=(rows // tr,),
            in_specs=[spec, half, half, spec, spec], out_specs=[spec] * 4),
        out_shape=[jax.ShapeDtypeStruct(w.shape, F32)] * 4,
        compiler_params=_cp(1),
    )(c_idx, w, g_mine, g_sib, m, v)


def kernel(x, p, ffn1_w_in, ffn1_w_out, ln1_g, ln1_b, w_mix_in, b_forget, conv_w, g_attn, g_conv, w_mix_out, ln2_g, ln2_b, ffn2_w_in, ffn2_w_out, ln3_g, ln3_b, w_ple, w_ple_gate, b_ple_gate, ln4_g, ln4_b, loss_target, m_ffn1_w_in, m_ffn1_w_out, m_ln1_g, m_ln1_b, m_w_mix_in, m_b_forget, m_conv_w, m_g_attn, m_g_conv, m_w_mix_out, m_ln2_g, m_ln2_b, m_ffn2_w_in, m_ffn2_w_out, m_ln3_g, m_ln3_b, m_w_ple, m_w_ple_gate, m_b_ple_gate, m_ln4_g, m_ln4_b, v_ffn1_w_in, v_ffn1_w_out, v_ln1_g, v_ln1_b, v_w_mix_in, v_b_forget, v_conv_w, v_g_attn, v_g_conv, v_w_mix_out, v_ln2_g, v_ln2_b, v_ffn2_w_in, v_ffn2_w_out, v_ln3_g, v_ln3_b, v_w_ple, v_w_ple_gate, v_b_ple_gate, v_ln4_g, v_ln4_b):
    args = dict(locals())
    shard = {n: args[n][0] if LAYOUT[n][1] is not None else args[n] for n in WEIGHTS}
    m_shard = {n: args["m_" + n][0] if LAYOUT[n][1] is not None else args["m_" + n] for n in WEIGHTS}
    v_shard = {n: args["v_" + n][0] if LAYOUT[n][1] is not None else args["v_" + n] for n in WEIGHTS}
    c_idx = lax.axis_index("c").astype(jnp.int32).reshape(1)
    chip = (2 * lax.axis_index("x") + lax.axis_index("y")).astype(jnp.int32)

    conv_rows = SMALL_ROWS - shard["conv_w"].shape[0]
    mine = {n: _halves(shard[n].astype(BF16)) for n in BIG}
    mine["conv_w"] = _halves(jnp.pad(shard["conv_w"], ((0, conv_rows), (0, 0))))
    early_w = ["ffn1_w_in", "ffn1_w_out"]
    late_w = [n for n in BIG if n not in early_w] + ["conv_w"]

    def full_weights(names, gathered):
        out = {}
        for n, theirs in zip(names, gathered):
            g = lax.dynamic_update_slice(theirs, mine[n][None], (chip, 0, 0, 0))
            if n == "conv_w":
                out[n] = _join_chips(n, g.reshape(N_CHIPS, SMALL_ROWS, CONV_SHARD)[:, :3])
            else:
                out[n] = _join_chips(n, g.reshape((N_CHIPS,) + _shard_shape(n)))
        return out

    full = full_weights(early_w, _all_gather([mine[n] for n in early_w]))
    full.update({n: shard[n] for n in SMALL if n != "conv_w"})

    def per_chip(names, grads):
        by_chip = lambda n: grads[n] if grads[n].ndim == 3 else _split_chips(n, grads[n])
        return [_halves(_pack_small_grads(grads) if n == "small" else by_chip(n)) for n in names]

    def add_halves(names, mine_, sibs):
        return [_add_my_half(g, sib, c_idx, n) for n, g, sib in zip(names, mine_, sibs)]

    def chip_sums(names, grads):
        mine_ = per_chip(names, grads)
        return add_halves(names, mine_, _swap_halves(mine_, names[0]))

    ready_a = ["ffn2_w_in", "ffn2_w_out", "w_ple", "w_ple_gate"]
    ready_b = ["w_mix_in", "w_mix_out"]
    early_g = ready_a + ready_b
    late_g = early_w + ["small"]
    loss_acc, grad_x, grads, early_sums, early_parts = _local_step(
        x[0], p[0, 0], loss_target[0], full,
        overlap={"gather": [mine[n] for n in late_w], "weights": lambda gathered: full_weights(late_w, gathered),
                 "swap": lambda grads: per_chip(ready_a, grads),
                 "chip_sums": lambda grads, swapped, received: (add_halves(ready_a, swapped, received)
                                                                + chip_sums(ready_b, grads))})
    loss = lax.psum(loss_acc[0, 0], ("x", "y", "c"))

    late_sums = chip_sums(late_g, grads)
    names = early_g + late_g
    sums = list(early_sums) + late_sums
    parts = list(early_parts) + list(_exchange_chips(late_sums))
    half_of = {n: _sum_chips(pt, own, chip.reshape(1), n) for n, pt, own in zip(names, parts, sums)}
    names = BIG + ["small"]
    my_half = [half_of[n] for n in names]
    sib_half = _share_half(my_half)

    out = {}
    for n, gm, gs in zip(BIG, my_half, sib_half):
        out[n] = [a[None] for a in _adamw(shard[n], gm, gs, m_shard[n], v_shard[n], c_idx, n)]
    small = _adamw_small(my_half[-1], sib_half[-1], c_idx, [shard[n] for n in SMALL], [m_shard[n] for n in SMALL],
                         [v_shard[n] for n in SMALL])
    for i, n in enumerate(SMALL):
        out[n] = [small[q][i][None] if n == "conv_w" else small[q][i] for q in range(4)]
    return (loss, grad_x[None], *[out[n][q] for q in range(4) for n in WEIGHTS])
```

```python
import math

import jax
import jax.numpy as jnp
from jax import lax
from jax.experimental import pallas as pl
from jax.experimental.pallas import tpu as pltpu

F32 = jnp.float32
BF16 = jnp.bfloat16

D_MODEL = 1024
D_FF = 2816
N_HEADS = 8
HEAD_DIM = 64
D_ATTN = N_HEADS * HEAD_DIM
D_CONV = 512
PLE_DIM = 256
N_FLOG = 128
ALPHA = 2.0 ** 0.25
LN_EPS = 1e-5
RMS_EPS = 1e-6
NEG_INF = -1e30
Q_SCALE = 1.0 / math.sqrt(HEAD_DIM)
LOG2E = math.log2(math.e)

ADAM_LR = 0.001
ADAM_B1 = 0.9
ADAM_B2 = 0.999
ADAM_EPS = 1e-08
ADAM_WD = 0.01
ADAM_STEP = 10

V7X_VMEM_BYTES = 64 << 20
VMEM_LIMIT = V7X_VMEM_BYTES - (8 << 20)
LANE = 128
FF_CHUNK = 256
N_CHIPS = 4
TOKEN_TILE = 512
FFN_BWD_TILE = 256
ATTN_TILE = 512
SCAN_BLOCK = 512
WGRAD_TOKENS = 2048
WGRAD_T_TOKENS = 1024
MESH = pl.DeviceIdType.MESH


def _cp(n_axes):
    return pltpu.CompilerParams(dimension_semantics=("arbitrary",) * n_axes, vmem_limit_bytes=VMEM_LIMIT)


def _resident(shape):
    n = len(shape)
    return pl.BlockSpec(shape, lambda *_: (0,) * n, pipeline_mode=pl.Buffered(1))


def _nn(a, b):
    return jnp.dot(a, b, preferred_element_type=F32)


def _nt(a, b):
    return lax.dot_general(a, b, (((1,), (1,)), ((), ())), preferred_element_type=F32)


def _tn(a, b):
    return lax.dot_general(a, b, (((0,), (0,)), ((), ())), preferred_element_type=F32)


def _ln_stats(r):
    mu = jnp.mean(r, axis=-1, keepdims=True)
    xc = r - mu
    var = jnp.mean(xc * xc, axis=-1, keepdims=True)
    rstd = lax.rsqrt(var + LN_EPS)
    return xc * rstd, rstd


def _ln_bwd(dy, xhat, rstd, g):
    dxh = dy * g
    m1 = jnp.mean(dxh, axis=-1, keepdims=True)
    m2 = jnp.mean(dxh * xhat, axis=-1, keepdims=True)
    return rstd * (dxh - m1 - xhat * m2)


def _sigmoid(z):
    return 1.0 / (1.0 + jnp.exp(-z))


def _rowsum(a):
    return jnp.sum(a, axis=0, keepdims=True)


def _tile(total, want):
    if total <= want:
        return total
    for t in range(want - want % 8, 0, -8):
        if total % t == 0:
            return t
    raise ValueError((total, want))


def _ffn_fwd(x, w_in, w_out, lg, lb, name, gather=()):
    T = x.shape[0]
    tm = _tile(T, TOKEN_TILE)
    nf = D_FF // FF_CHUNK
    ng = len(gather)
    last = T // tm - 1

    def body(x_ref, wi_ref, wo_ref, lg_ref, lb_ref, *rest):
        comm_in, (xo_ref, r_ref, g_ref, u_ref, h_ref) = rest[:ng], rest[ng:ng + 5]
        comm_out, sems = rest[ng + 5:2 * ng + 5], rest[2 * ng + 5:]
        if ng:
            @pl.when(pl.program_id(0) == 0)
            def _():
                _gather_start(comm_in, comm_out, *sems)

        xf = x_ref[...]
        xb = xf.astype(BF16)
        acc = jnp.zeros((tm, D_MODEL), F32)
        for j in range(nf):
            c0 = j * FF_CHUNK
            g = _nn(xb, wi_ref[:, c0:c0 + FF_CHUNK])
            u = _nn(xb, wi_ref[:, D_FF + c0:D_FF + c0 + FF_CHUNK])
            hb = (g * _sigmoid(g) * u).astype(BF16)
            g_ref[:, c0:c0 + FF_CHUNK] = g.astype(BF16)
            u_ref[:, c0:c0 + FF_CHUNK] = u.astype(BF16)
            h_ref[:, c0:c0 + FF_CHUNK] = hb
            acc = acc + _nn(hb, wo_ref[c0:c0 + FF_CHUNK, :])
        r = ALPHA * xf + 0.5 * acc
        r_ref[...] = r
        xhat, _ = _ln_stats(r)
        xo_ref[...] = xhat * lg_ref[...] + lb_ref[...]
        if ng:
            @pl.when(pl.program_id(0) == last)
            def _():
                _gather_finish(comm_in, comm_out, *sems)

    row = lambda n: pl.BlockSpec((tm, n), lambda i: (i, 0))
    return pl.pallas_call(
        body, name=name, grid=(T // tm,),
        in_specs=[row(D_MODEL), _resident((D_MODEL, 2 * D_FF)), _resident((D_FF, D_MODEL)),
                  _resident((1, D_MODEL)), _resident((1, D_MODEL))] + [ANY_SPEC] * ng,
        out_specs=[row(D_MODEL), row(D_MODEL), row(D_FF), row(D_FF), row(D_FF)] + [ANY_SPEC] * ng,
        out_shape=[jax.ShapeDtypeStruct((T, D_MODEL), F32), jax.ShapeDtypeStruct((T, D_MODEL), F32),
                   jax.ShapeDtypeStruct((T, D_FF), BF16), jax.ShapeDtypeStruct((T, D_FF), BF16),
                   jax.ShapeDtypeStruct((T, D_FF), BF16)] + _gather_shapes(gather),
        scratch_shapes=_gather_sems(ng) if ng else [],
        compiler_params=_cp(1),
    )(x, w_in, w_out, lg, lb, *gather)


def _ffn_bwd(dxo, r, g, u, w_in, w_out, lg, name, exchange=()):
    T = r.shape[0]
    tm = _tile(T, FFN_BWD_TILE)
    nf = D_FF // FF_CHUNK
    ne = len(exchange)
    last = T // tm - 1

    def body(dxo_ref, r_ref, g_ref, u_ref, wi_ref, wo_ref, lg_ref, *rest):
        comm_in, (dx_ref, dgu_ref, df_ref, dlg_ref, dlb_ref) = rest[:ne], rest[ne:ne + 5]
        comm_out, sems = rest[ne + 5:2 * ne + 5], rest[2 * ne + 5:]
        i = pl.program_id(0)
        if ne:
            @pl.when(i == 0)
            def _():
                _exchange_start(comm_in, comm_out, *sems)

        dy = dxo_ref[...]
        xhat, rstd = _ln_stats(r_ref[...])
        dr = _ln_bwd(dy, xhat, rstd, lg_ref[...])

        @pl.when(i == 0)
        def _():
            dlg_ref[...] = jnp.zeros_like(dlg_ref)
            dlb_ref[...] = jnp.zeros_like(dlb_ref)

        dlg_ref[...] += _rowsum(dy * xhat)
        dlb_ref[...] += _rowsum(dy)
        dfb = (0.5 * dr).astype(BF16)
        df_ref[...] = dfb
        acc = jnp.zeros((tm, D_MODEL), F32)
        dh_ahead = _nt(dfb, wo_ref[0:FF_CHUNK, :])
        for j in range(nf):
            c0 = j * FF_CHUNK
            dh = dh_ahead
            if j + 1 < nf:
                dh_ahead = _nt(dfb, wo_ref[c0 + FF_CHUNK:c0 + 2 * FF_CHUNK, :])
            gg = g_ref[:, c0:c0 + FF_CHUNK].astype(F32)
            uu = u_ref[:, c0:c0 + FF_CHUNK].astype(F32)
            s = _sigmoid(gg)
            dgb = (dh * uu * s * (1.0 + gg * (1.0 - s))).astype(BF16)
            dub = (dh * gg * s).astype(BF16)
            dgu_ref[:, c0:c0 + FF_CHUNK] = dgb
            dgu_ref[:, D_FF + c0:D_FF + c0 + FF_CHUNK] = dub
            acc = acc + _nt(dgb, wi_ref[:, c0:c0 + FF_CHUNK]) + _nt(dub, wi_ref[:, D_FF + c0:D_FF + c0 + FF_CHUNK])
        dx_ref[...] = ALPHA * dr + acc
        if ne:
            @pl.when(i == last)
            def _():
                _exchange_finish(comm_in, comm_out, *sems)

    row = lambda n: pl.BlockSpec((tm, n), lambda i: (i, 0))
    return pl.pallas_call(
        body, name=name, grid=(T // tm,),
        in_specs=[row(D_MODEL), row(D_MODEL), row(D_FF), row(D_FF), _resident((D_MODEL, 2 * D_FF)),
                  _resident((D_FF, D_MODEL)), _resident((1, D_MODEL))] + [ANY_SPEC] * ne,
        out_specs=[row(D_MODEL), row(2 * D_FF), row(D_MODEL), _resident((1, D_MODEL)), _resident((1, D_MODEL))]
        + [ANY_SPEC] * ne,
        out_shape=[jax.ShapeDtypeStruct((T, D_MODEL), F32), jax.ShapeDtypeStruct((T, 2 * D_FF), BF16),
                   jax.ShapeDtypeStruct((T, D_MODEL), BF16), jax.ShapeDtypeStruct((1, D_MODEL), F32),
                   jax.ShapeDtypeStruct((1, D_MODEL), F32)] + _exchange_shapes(exchange),
        scratch_shapes=_exchange_sems(ne) if ne else [],
        compiler_params=_cp(1),
    )(dxo, r, g, u, w_in, w_out, lg, *exchange)


def _matmul_tn(a, b, name, by_chip=False, exchange=()):
    T, K = a.shape
    N = b.shape[1]
    tt = _tile(T, WGRAD_TOKENS)
    tn = N // N_CHIPS if by_chip else N
    while K * tn * 4 > (6 << 20) and tn % 256 == 0 and not by_chip:
        tn //= 2
    assert N % tn == 0
    ne = len(exchange)
    grid = (N // tn, T // tt)

    def body(a_ref, b_ref, *rest):
        comm_in, o_ref, comm_out, sems = rest[:ne], rest[ne], rest[ne + 1:2 * ne + 1], rest[2 * ne + 1:]
        n, t = pl.program_id(0), pl.program_id(1)
        if ne:
            @pl.when((n == 0) & (t == 0))
            def _():
                _exchange_start(comm_in, comm_out, *sems)

        @pl.when(t == 0)
        def _():
            o_ref[...] = jnp.zeros_like(o_ref)

        o_ref[...] += _tn(a_ref[...].astype(BF16), b_ref[...].astype(BF16))
        if ne:
            @pl.when((n == grid[0] - 1) & (t == grid[1] - 1))
            def _():
                _exchange_finish(comm_in, comm_out, *sems)

    res = pl.pallas_call(
        body, name=name, grid=grid,
        in_specs=[pl.BlockSpec((tt, K), lambda n, t: (t, 0)), pl.BlockSpec((tt, tn), lambda n, t: (t, n))]
        + [ANY_SPEC] * ne,
        out_specs=[pl.BlockSpec((None, K, tn), lambda n, t: (n, 0, 0)) if by_chip
                   else pl.BlockSpec((K, tn), lambda n, t: (0, n))] + [ANY_SPEC] * ne,
        out_shape=[jax.ShapeDtypeStruct((N_CHIPS, K, tn) if by_chip else (K, N), F32)] + _exchange_shapes(exchange),
        scratch_shapes=_exchange_sems(ne) if ne else [],
        compiler_params=_cp(2),
    )(a, b, *exchange)
    return res if ne else res[0]


def _matmul_tokens(at, b, name):
    M, T = at.shape
    N = b.shape[1]
    tt = _tile(T, WGRAD_T_TOKENS)

    def body(a_ref, b_ref, o_ref):
        @pl.when(pl.program_id(0) == 0)
        def _():
            o_ref[...] = jnp.zeros_like(o_ref)

        o_ref[...] += _nn(a_ref[...].astype(BF16), b_ref[...].astype(BF16))

    return pl.pallas_call(
        body, name=name, grid=(T // tt,),
        in_specs=[pl.BlockSpec((M, tt), lambda t: (0, t)), pl.BlockSpec((tt, N), lambda t: (t, 0))],
        out_specs=pl.BlockSpec((M, N), lambda t: (0, 0)),
        out_shape=jax.ShapeDtypeStruct((M, N), F32),
        compiler_params=_cp(1),
    )(at, b)


def _matmul_nn(x, w, scale, out_dtype, name, also_transposed=False):
    T, K = x.shape
    N = w.shape[1]
    tm = _tile(T, TOKEN_TILE)

    def body(x_ref, w_ref, s_ref, o_ref, *ot_ref):
        res = _nn(x_ref[...].astype(BF16), w_ref[...]) * s_ref[...]
        o_ref[...] = res.astype(out_dtype)
        if also_transposed:
            ot_ref[0][...] = res.T.astype(out_dtype)

    res = pl.pallas_call(
        body, name=name, grid=(T // tm,),
        in_specs=[pl.BlockSpec((tm, K), lambda i: (i, 0)), _resident((K, N)), _resident((1, N))],
        out_specs=[pl.BlockSpec((tm, N), lambda i: (i, 0))] + [pl.BlockSpec((N, tm), lambda i: (0, i))] * also_transposed,
        out_shape=[jax.ShapeDtypeStruct((T, N), out_dtype)] + [jax.ShapeDtypeStruct((N, T), out_dtype)] * also_transposed,
        compiler_params=_cp(1),
    )(x, w, scale)
    return res if also_transposed else res[0]


def _log_sigmoid(z):
    return jnp.minimum(z, 0.0) - jnp.log1p(jnp.exp(-jnp.abs(z)))


def _tri(n, lower):
    r = lax.broadcasted_iota(jnp.int32, (n, n), 0)
    c = lax.broadcasted_iota(jnp.int32, (n, n), 1)
    return jnp.where((c <= r) if lower else (c >= r), 1.0, 0.0).astype(F32)


def _f32dot(a, b):
    return jnp.dot(a, b, preferred_element_type=F32, precision=lax.Precision.HIGHEST)


def _forget_cumsum(flog, col, bf):
    T = flog.shape[0]
    bt = _tile(T, SCAN_BLOCK)

    def body(f_ref, b_ref, c_ref, carry):
        @pl.when(pl.program_id(0) == 0)
        def _():
            carry[...] = jnp.zeros_like(carry)

        lf = _log_sigmoid(f_ref[...] + b_ref[...])
        c = _f32dot(_tri(bt, True), lf) + carry[...]
        c_ref[...] = c * LOG2E
        carry[...] = c[bt - 1:bt, :]

    return pl.pallas_call(
        body, name="forget_cumsum", grid=(T // bt,),
        in_specs=[pl.BlockSpec((bt, N_FLOG), lambda i: (i, col)), _resident((1, N_FLOG))],
        out_specs=pl.BlockSpec((bt, N_FLOG), lambda i: (i, 0)),
        out_shape=jax.ShapeDtypeStruct((T, N_FLOG), F32),
        scratch_shapes=[pltpu.VMEM((1, N_FLOG), F32)],
        compiler_params=_cp(1),
    )(flog, bf)


def _forget_bwd(dck, dcq, flog, col, bf):
    T = dcq.shape[0]
    bt = _tile(T, SCAN_BLOCK)
    nb = T // bt

    def body(k0_ref, k1_ref, k2_ref, k3_ref, dcq_ref, f_ref, b_ref, dz_ref, db_ref, carry):
        @pl.when(pl.program_id(0) == 0)
        def _():
            carry[...] = jnp.zeros_like(carry)
            db_ref[...] = jnp.zeros_like(db_ref)

        dc = ((k0_ref[...] + k1_ref[...]) + (k2_ref[...] + k3_ref[...])) + dcq_ref[...]
        dlf = _f32dot(_tri(bt, False), dc) + carry[...]
        carry[...] = dlf[0:1, :]
        z = f_ref[...] + b_ref[...]
        dz = dlf * _sigmoid(-z)
        dz_ref[...] = dz.astype(BF16)
        db_ref[...] += _rowsum(dz)

    slab = lambda j: pl.BlockSpec((None, bt, N_FLOG), lambda i: (j, nb - 1 - i, 0))
    return pl.pallas_call(
        body, name="forget_bwd", grid=(nb,),
        in_specs=[slab(0), slab(1), slab(2), slab(3),
                  pl.BlockSpec((bt, N_FLOG), lambda i: (nb - 1 - i, 0)),
                  pl.BlockSpec((bt, N_FLOG), lambda i: (nb - 1 - i, col)), _resident((1, N_FLOG))],
        out_specs=[pl.BlockSpec((bt, N_FLOG), lambda i: (nb - 1 - i, 0)), _resident((1, N_FLOG))],
        out_shape=[jax.ShapeDtypeStruct((T, N_FLOG), BF16), jax.ShapeDtypeStruct((1, N_FLOG), F32)],
        scratch_shapes=[pltpu.VMEM((1, N_FLOG), F32)],
        compiler_params=_cp(1),
    )(dck, dck, dck, dck, dcq, flog, bf)


def _head_masks():
    lane = lax.broadcasted_iota(jnp.int32, (1, LANE), 1)
    return lane < HEAD_DIM


def _split_heads(x2, is_a):
    zero = jnp.zeros_like(x2)
    return jnp.where(is_a, x2, zero), jnp.where(is_a, zero, x2)


BIAS_PARTS = 3
ATTN_GROUP = 4
ATTN_FWD_GROUP = 8


def _bias_lanes(h):
    lane = lax.broadcasted_iota(jnp.int32, (1, LANE), 1)
    first = (1 - h) * HEAD_DIM
    return lane, first


def _fold_key_bias(qkv, c):
    T = qkv.shape[0]
    tm = _tile(T, TOKEN_TILE)
    npair = N_HEADS // 2

    def body(k_ref, c_ref, o_ref):
        cc = c_ref[...]
        parts, rest = [], cc
        for _ in range(BIAS_PARTS):
            piece = rest.astype(BF16)
            parts.append(piece)
            rest = rest - piece.astype(F32)
        for j in range(npair):
            k2 = k_ref[:, j * LANE:(j + 1) * LANE]
            for h in range(2):
                lane, first = _bias_lanes(h)
                out = k2
                for n, piece in enumerate(parts):
                    col = piece[:, 2 * j + h:2 * j + h + 1]
                    out = jnp.where(lane == first + n, col, out)
                o_ref[:, (2 * j + h) * LANE:(2 * j + h + 1) * LANE] = out

    return pl.pallas_call(
        body, name="fold_key_bias", grid=(T // tm,),
        in_specs=[pl.BlockSpec((tm, D_ATTN), lambda i: (i, 1)), pl.BlockSpec((tm, N_FLOG), lambda i: (i, 0))],
        out_specs=pl.BlockSpec((tm, 2 * D_ATTN), lambda i: (i, 0)),
        out_shape=jax.ShapeDtypeStruct((T, 2 * D_ATTN), BF16),
        compiler_params=_cp(1),
    )(qkv, c)


def _attn_fwd(qkv, vt, kb):
    T = qkv.shape[0]
    tq = _tile(T, ATTN_TILE)
    tk = tq
    nq = T // tq
    npair = N_HEADS // 2

    def body(qt_ref, ka_ref, kb_ref, vt_ref, o_ref, al_ref, m_s, l_s, acc_s):
        i = pl.program_id(1)
        dim = lax.broadcasted_iota(jnp.int32, (LANE, 1), 0)
        qs = []
        for h, qh in enumerate(_split_heads(qt_ref[...], dim < HEAD_DIM)):
            first = (1 - h) * HEAD_DIM
            qs.append(jnp.where((dim >= first) & (dim < first + BIAS_PARTS), -1.0, qh).astype(BF16))
        k_refs = (ka_ref, kb_ref)
        m_s[...] = jnp.full_like(m_s, NEG_INF)
        l_s[...] = jnp.zeros_like(l_s)
        acc_s[...] = jnp.zeros_like(acc_s)

        def scores_at(kk):
            k0 = pl.multiple_of(kk * tk, tk)
            return tuple(_nn(k_refs[h][pl.ds(k0, tk), :], qs[h]) for h in range(2))

        def consume(kk, scores, masked):
            k0 = pl.multiple_of(kk * tk, tk)
            v2t = vt_ref[:, pl.ds(k0, tk)]
            for h in range(2):
                zt = scores[h]
                if masked:
                    rr = lax.broadcasted_iota(jnp.int32, (tk, tq), 0)
                    cc = lax.broadcasted_iota(jnp.int32, (tk, tq), 1)
                    zt = jnp.where(cc >= rr, zt, NEG_INF)
                m_old = m_s[h]
                m_new = jnp.maximum(m_old, jnp.max(zt, axis=0, keepdims=True))
                p = jnp.exp2(zt - m_new)
                a = jnp.exp2(m_old - m_new)
                l_s[h] = a * l_s[h] + jnp.sum(p, axis=0, keepdims=True)
                acc_s[h] = a * acc_s[h] + _nn(v2t, p.astype(BF16))
                m_s[h] = m_new

        def group(kk, n, last_masked):
            scores = [scores_at(kk + u) for u in range(n)]
            for u in range(n):
                consume(kk + u, scores[u], last_masked and u == n - 1)

        def loop_body(t, carry):
            group(ATTN_FWD_GROUP * t, ATTN_FWD_GROUP, False)
            return carry

        lax.fori_loop(0, i // ATTN_FWD_GROUP, loop_body, 0)
        for left in range(ATTN_FWD_GROUP):
            @pl.when(i % ATTN_FWD_GROUP == left)
            def _():
                group(i - left, left + 1, True)

        outs = []
        for h in range(2):
            l = l_s[h]
            outs.append(acc_s[h] * (1.0 / l))
            al_ref[0, h:h + 1, :] = -(m_s[h] + jnp.log2(l))
        al_ref[0, 2:8, :] = jnp.zeros((6, tq), F32)
        dim = lax.broadcasted_iota(jnp.int32, (LANE, 1), 0)
        o_ref[...] = jnp.where(dim < HEAD_DIM, outs[0], outs[1]).T

    rowl = pl.BlockSpec((1, 8, tq), lambda j, i: (j, 0, i))
    return pl.pallas_call(
        body, name="attn_fwd", grid=(npair, nq),
        in_specs=[pl.BlockSpec((LANE, tq), lambda j, i: (j, i)),
                  pl.BlockSpec((T, LANE), lambda j, i: (0, 2 * j), pipeline_mode=pl.Buffered(1)),
                  pl.BlockSpec((T, LANE), lambda j, i: (0, 2 * j + 1), pipeline_mode=pl.Buffered(1)),
                  pl.BlockSpec((LANE, T), lambda j, i: (2 * npair + j, 0), pipeline_mode=pl.Buffered(1))],
        out_specs=[pl.BlockSpec((tq, LANE), lambda j, i: (i, j)), rowl],
        out_shape=[jax.ShapeDtypeStruct((T, D_ATTN), F32), jax.ShapeDtypeStruct((npair, 8, T), F32)],
        scratch_shapes=[pltpu.VMEM((2, 1, tq), F32), pltpu.VMEM((2, 1, tq), F32), pltpu.VMEM((2, LANE, tq), F32)],
        compiler_params=_cp(2),
    )(vt, kb, kb, vt)


def _attn_bwd(qkv, qkvt, dobt, cb, alrow, dlrow):
    T = qkv.shape[0]
    tq = _tile(T, ATTN_TILE)
    tk = tq
    nq = T // tq
    npair = N_HEADS // 2

    def body(qt_ref, k_ref, kt_ref, v_ref, dot_ref, cb_ref, al_ref, dl_ref,
             dq_ref, dk_ref, dv_ref, dc_ref, dcq_ref, dk_s, dv_s, dc_s):
        kj = pl.program_id(1)
        is_a = _head_masks()
        ks = _split_heads(k_ref[...], is_a)
        vs = _split_heads(v_ref[...], is_a)
        dim_a = lax.broadcasted_iota(jnp.int32, (LANE, 1), 0) < HEAD_DIM
        kts = _split_heads(kt_ref[...], dim_a)
        head_lane = lax.broadcasted_iota(jnp.int32, (1, LANE), 1) - 2 * pl.program_id(0)
        cs = tuple(jnp.sum(jnp.where(head_lane == h, cb_ref[...], 0.0), axis=-1, keepdims=True) for h in range(2))

        @pl.when(kj == 0)
        def _():
            dq_ref[...] = jnp.zeros_like(dq_ref)
            dcq_ref[...] = jnp.zeros_like(dcq_ref)

        dk_s[...] = jnp.zeros_like(dk_s)
        dv_s[...] = jnp.zeros_like(dv_s)
        dc_s[...] = jnp.zeros_like(dc_s)

        def block(qi, k_off, nk, q_off, nqs, masked):
            q0 = pl.multiple_of(qi * tq + q_off, nqs)
            rows = slice(k_off, k_off + nk)
            qt2 = qt_ref[:, pl.ds(q0, nqs)]
            dot2 = dot_ref[:, pl.ds(q0, nqs)]
            for h in range(2):
                alr = al_ref[0, h:h + 1, pl.ds(q0, nqs)]
                dlr = dl_ref[0, h:h + 1, pl.ds(q0, nqs)]
                zt = _nn(ks[h][rows], qt2) + (alr - cs[h][rows])
                if masked:
                    rr = lax.broadcasted_iota(jnp.int32, (nk, nqs), 0)
                    cc = lax.broadcasted_iota(jnp.int32, (nk, nqs), 1)
                    zt = jnp.where(cc >= rr, zt, NEG_INF)
                pt = jnp.exp2(zt)
                dst = pt * (_nn(vs[h][rows], dot2) - dlr)
                pb = pt.astype(BF16)
                dsb = dst.astype(BF16)
                dv_s[h, :, rows] += _nt(dot2, pb)
                dk_s[h, :, rows] += _nt(qt2, dsb)
                dc_s[h, rows, :] += jnp.sum(dst, axis=-1, keepdims=True)
                dcq_ref[0, h:h + 1, pl.ds(q0, nqs)] += jnp.sum(dst, axis=0, keepdims=True)
                dq_ref[:, pl.ds(q0, nqs)] += _nn(kts[h][:, rows], dsb)

        def step(qi, masked):
            if not masked:
                block(qi, 0, tk, 0, tq, False)
            elif tk % 256:
                block(qi, 0, tk, 0, tq, True)
            else:
                block(qi, 0, tk // 2, 0, tq, True)
                block(qi, tk // 2, tk // 2, tq // 2, tq // 2, True)

        rest = nq - 1 - kj
        for left in range(ATTN_GROUP):
            @pl.when(rest % ATTN_GROUP == left)
            def _():
                for u in range(left + 1):
                    step(kj + u, u == 0)

        def loop_body(t, carry):
            for u in range(ATTN_GROUP):
                step(kj + 1 + rest % ATTN_GROUP + ATTN_GROUP * t + u, False)
            return carry

        lax.fori_loop(0, rest // ATTN_GROUP, loop_body, 0)
        dk_ref[...] = (jnp.where(dim_a, dk_s[0], dk_s[1]) * (1.0 / LOG2E)).astype(BF16)
        dv_ref[...] = jnp.where(dim_a, dv_s[0], dv_s[1]).astype(BF16)
        lane = lax.broadcasted_iota(jnp.int32, (1, LANE), 1)
        head = 2 * pl.program_id(0)
        dc_ref[...] = jnp.where(lane == head, -dc_s[0], jnp.where(lane == head + 1, -dc_s[1], 0.0))

        @pl.when(kj == nq - 1)
        def _():
            dq_ref[...] = dq_ref[...] * Q_SCALE

    fullt = lambda row: pl.BlockSpec((LANE, T), lambda j, kj: (row(j), 0), pipeline_mode=pl.Buffered(1))
    tile = lambda col: pl.BlockSpec((tk, LANE), lambda j, kj: (kj, col(j)))
    tilet = lambda row: pl.BlockSpec((LANE, tk), lambda j, kj: (row(j), kj))
    rowl = pl.BlockSpec((1, 8, T), lambda j, kj: (j, 0, 0))
    return pl.pallas_call(
        body, name="attn_bwd", grid=(npair, nq),
        in_specs=[fullt(lambda j: j), tile(lambda j: npair + j), tilet(lambda j: npair + j),
                  tile(lambda j: 2 * npair + j), fullt(lambda j: j), tile(lambda j: 0), rowl, rowl],
        out_specs=[pl.BlockSpec((LANE, T), lambda j, kj: (j, 0)), tilet(lambda j: j), tilet(lambda j: j),
                   pl.BlockSpec((None, tk, LANE), lambda j, kj: (j, kj, 0)), rowl],
        out_shape=[jax.ShapeDtypeStruct((D_ATTN, T), F32), jax.ShapeDtypeStruct((D_ATTN, T), BF16),
                   jax.ShapeDtypeStruct((D_ATTN, T), BF16), jax.ShapeDtypeStruct((npair, T, LANE), F32),
                   jax.ShapeDtypeStruct((npair, 8, T), F32)],
        scratch_shapes=[pltpu.VMEM((2, LANE, tk), F32), pltpu.VMEM((2, LANE, tk), F32), pltpu.VMEM((2, tk, 1), F32)],
        compiler_params=_cp(2),
    )(qkvt, qkv, qkvt, qkv, dobt, cb, alrow, dlrow)


HALO = 8


def _shift_rows(cur, other, k, tm, down):
    row = lax.broadcasted_iota(jnp.int32, (tm, 1), 0)
    reps = tm // HALO
    if down:
        rolled = pltpu.roll(cur, k, 0)
        fill = jnp.tile(pltpu.roll(other, k, 0), (reps, 1))
        return jnp.where(row < k, fill, rolled)
    rolled = pltpu.roll(cur, tm - k, 0)
    fill = jnp.tile(pltpu.roll(other, HALO - k, 0), (reps, 1))
    return jnp.where(row >= tm - k, fill, rolled)


def _conv_fwd(c, hh, c_prev, hh_prev, w_ref, first, tm):
    u = c * hh
    u_prev = jnp.where(first, 0.0, c_prev * hh_prev)
    u1 = _shift_rows(u, u_prev, 1, tm, True)
    u2 = _shift_rows(u, u_prev, 2, tm, True)
    y = w_ref[0:1, :] * u2 + w_ref[1:2, :] * u1 + w_ref[2:3, :] * u
    return u, u1, u2, y


def _rms(x, g):
    rs = lax.rsqrt(jnp.mean(x * x, axis=-1, keepdims=True) + RMS_EPS)
    return x * rs * g, rs


def _mixer_tail_fwd(o, bchf, conv_w, g_attn, g_conv, w_mo, x1, lg, lb):
    T = o.shape[0]
    tm = _tile(T, TOKEN_TILE)
    hb = tm // HALO

    def body(o_ref, b_ref, c_ref, h_ref, cp_ref, hp_ref, w_ref, ga_ref, gc_ref, wmo_ref, x1_ref, lg_ref, lb_ref,
             x2_ref, r2_ref, mg_ref):
        first = pl.program_id(0) == 0
        _, _, _, y = _conv_fwd(c_ref[...], h_ref[...], cp_ref[...], hp_ref[...], w_ref, first, tm)
        na, _ = _rms(o_ref[...], ga_ref[...])
        nc, _ = _rms(b_ref[...] * y, gc_ref[...])
        nab = na.astype(BF16)
        ncb = nc.astype(BF16)
        mg_ref[:, 0:D_ATTN] = nab
        mg_ref[:, D_ATTN:] = ncb
        r2 = ALPHA * x1_ref[...] + _nn(nab, wmo_ref[0:D_ATTN, :]) + _nn(ncb, wmo_ref[D_ATTN:, :])
        r2_ref[...] = r2
        xhat, _ = _ln_stats(r2)
        x2_ref[...] = xhat * lg_ref[...] + lb_ref[...]

    row = lambda n, col=0: pl.BlockSpec((tm, n), lambda i: (i, col))
    prev = lambda col: pl.BlockSpec((HALO, D_CONV), lambda i: (jnp.maximum(i * hb - 1, 0), col))
    return pl.pallas_call(
        body, name="mixer_tail_fwd", grid=(T // tm,),
        in_specs=[row(D_ATTN), row(D_CONV, 0), row(D_CONV, 1), row(D_CONV, 2), prev(1), prev(2),
                  _resident((3, D_CONV)), _resident((1, D_ATTN)), _resident((1, D_CONV)),
                  _resident((D_MODEL, D_MODEL)), row(D_MODEL), _resident((1, D_MODEL)), _resident((1, D_MODEL))],
        out_specs=[row(D_MODEL), row(D_MODEL), row(D_MODEL)],
        out_shape=[jax.ShapeDtypeStruct((T, D_MODEL), F32), jax.ShapeDtypeStruct((T, D_MODEL), F32),
                   jax.ShapeDtypeStruct((T, D_MODEL), BF16)],
        compiler_params=_cp(1),
    )(o, bchf, bchf, bchf, bchf, bchf, conv_w, g_attn, g_conv, w_mo, x1, lg, lb)


def _head_sum_rows():
    row = lax.broadcasted_iota(jnp.int32, (4 * 8, D_ATTN), 0)
    head = lax.broadcasted_iota(jnp.int32, (4 * 8, D_ATTN), 1) // HEAD_DIM
    return jnp.where((row % 8 < 2) & (2 * (row // 8) + row % 8 == head), 1.0, 0.0).astype(F32)


def _mixer_tail_bwd(dx2, r2, lg, w_mo, o, bchf, conv_w, g_attn, g_conv, swap=()):
    T = o.shape[0]
    tm = _tile(T, TOKEN_TILE)
    hb = tm // HALO
    ns = len(swap)
    last = T // tm - 1

    def body(dx2_ref, r2_ref, lg_ref, wmo_ref, o_ref, b_ref, c_ref, h_ref, cp_ref, hp_ref, w_ref, ga_ref, gc_ref,
             *rest):
        comm_in = rest[:ns]
        dx1_ref, dr_ref, dot_ref, dl_ref, dco_ref, dlg_ref, dlb_ref, dga_ref, dgc_ref = rest[ns:ns + 9]
        comm_out, sems = rest[ns + 9:2 * ns + 9], rest[2 * ns + 9:]
        i = pl.program_id(0)
        if ns:
            @pl.when(i == 0)
            def _():
                _swap_start(comm_in, comm_out, *sems)

            @pl.when(i == last)
            def _():
                _swap_finish(comm_in, comm_out, *sems)

        @pl.when(i == 0)
        def _():
            for ref in (dlg_ref, dlb_ref, dga_ref, dgc_ref):
                ref[...] = jnp.zeros_like(ref)

        dy = dx2_ref[...]
        xhat, rstd = _ln_stats(r2_ref[...])
        dr = _ln_bwd(dy, xhat, rstd, lg_ref[...])
        dlg_ref[...] += _rowsum(dy * xhat)
        dlb_ref[...] += _rowsum(dy)
        dx1_ref[...] = ALPHA * dr
        drb = dr.astype(BF16)
        dr_ref[...] = drb
        dna = _nt(drb, wmo_ref[0:D_ATTN, :])
        dnc = _nt(drb, wmo_ref[D_ATTN:, :])

        def rms_bwd(x, g, dn):
            rs = lax.rsqrt(jnp.mean(x * x, axis=-1, keepdims=True) + RMS_EPS)
            dng = dn * g
            dx = rs * dng - x * (rs * rs * rs) * jnp.mean(dng * x, axis=-1, keepdims=True)
            return dx, _rowsum(dn * x * rs)

        oo = o_ref[...]
        do, dga = rms_bwd(oo, ga_ref[...], dna)
        dga_ref[...] += dga
        dot_ref[...] = do.T.astype(BF16)
        dl_ref[...] = lax.dot_general(_head_sum_rows(), do * oo, (((1,), (1,)), ((), ())),
                                      preferred_element_type=F32, precision=lax.Precision.HIGHEST)
        _, _, _, y = _conv_fwd(c_ref[...], h_ref[...], cp_ref[...], hp_ref[...], w_ref, i == 0, tm)
        dco, dgc = rms_bwd(b_ref[...] * y, gc_ref[...], dnc)
        dgc_ref[...] += dgc
        dco_ref[...] = dco

    row = lambda n, col=0: pl.BlockSpec((tm, n), lambda i: (i, col))
    prev = lambda col: pl.BlockSpec((HALO, D_CONV), lambda i: (jnp.maximum(i * hb - 1, 0), col))
    vec = lambda n: _resident((1, n))
    return pl.pallas_call(
        body, name="mixer_tail_bwd", grid=(T // tm,),
        in_specs=[row(D_MODEL), row(D_MODEL), vec(D_MODEL), _resident((D_MODEL, D_MODEL)), row(D_ATTN),
                  row(D_CONV, 0), row(D_CONV, 1), row(D_CONV, 2), prev(1), prev(2), _resident((3, D_CONV)),
                  vec(D_ATTN), vec(D_CONV)] + [ANY_SPEC] * ns,
        out_specs=[row(D_MODEL), row(D_MODEL), pl.BlockSpec((D_ATTN, tm), lambda i: (0, i)),
                   pl.BlockSpec((4 * 8, tm), lambda i: (0, i)), row(D_CONV),
                   vec(D_MODEL), vec(D_MODEL), vec(D_ATTN), vec(D_CONV)] + [ANY_SPEC] * ns,
        out_shape=[jax.ShapeDtypeStruct((T, D_MODEL), F32), jax.ShapeDtypeStruct((T, D_MODEL), BF16),
                   jax.ShapeDtypeStruct((D_ATTN, T), BF16),
                   jax.ShapeDtypeStruct((4 * 8, T), F32),
                   jax.ShapeDtypeStruct((T, D_CONV), F32), jax.ShapeDtypeStruct((1, D_MODEL), F32),
                   jax.ShapeDtypeStruct((1, D_MODEL), F32), jax.ShapeDtypeStruct((1, D_ATTN), F32),
                   jax.ShapeDtypeStruct((1, D_CONV), F32)] + _swap_shapes(swap),
        scratch_shapes=_swap_sems(ns) if ns else [],
        compiler_params=_cp(1),
    )(dx2, r2, lg, w_mo, o, bchf, bchf, bchf, bchf, bchf, conv_w, g_attn, g_conv, *swap)


def _conv_bwd(dco, bchf, conv_w):
    T = dco.shape[0]
    tm = _tile(T, TOKEN_TILE)
    hb = tm // HALO
    nt = T // tm

    def body(dco_ref, dcon_ref, b_ref, bn_ref, c_ref, h_ref, cp_ref, hp_ref, w_ref, dbch_ref, dw_ref):
        i = pl.program_id(0)

        @pl.when(i == 0)
        def _():
            dw_ref[...] = jnp.zeros_like(dw_ref)

        cc = c_ref[...]
        hh = h_ref[...]
        u, u1, u2, y = _conv_fwd(cc, hh, cp_ref[...], hp_ref[...], w_ref, i == 0, tm)
        dco = dco_ref[...]
        bb = b_ref[...]
        dyc = dco * bb
        dy_next = jnp.where(i == nt - 1, 0.0, dcon_ref[...] * bn_ref[...])
        d1 = _shift_rows(dyc, dy_next, 1, tm, False)
        d2 = _shift_rows(dyc, dy_next, 2, tm, False)
        du = w_ref[2:3, :] * dyc + w_ref[1:2, :] * d1 + w_ref[0:1, :] * d2
        dbch_ref[:, 0:D_CONV] = (dco * y).astype(BF16)
        dbch_ref[:, D_CONV:2 * D_CONV] = (du * hh).astype(BF16)
        dbch_ref[:, 2 * D_CONV:] = (du * cc).astype(BF16)
        dw_ref[0:1, :] += _rowsum(dyc * u2)
        dw_ref[1:2, :] += _rowsum(dyc * u1)
        dw_ref[2:3, :] += _rowsum(dyc * u)

    row = lambda n, col=0: pl.BlockSpec((tm, n), lambda i: (i, col))
    prev = lambda col: pl.BlockSpec((HALO, D_CONV), lambda i: (jnp.maximum(i * hb - 1, 0), col))
    nxt = lambda col: pl.BlockSpec((HALO, D_CONV), lambda i: (jnp.minimum((i + 1) * hb, T // HALO - 1), col))
    return pl.pallas_call(
        body, name="conv_bwd", grid=(nt,),
        in_specs=[row(D_CONV), nxt(0), row(D_CONV, 0), nxt(0), row(D_CONV, 1), row(D_CONV, 2), prev(1), prev(2),
                  _resident((3, D_CONV))],
        out_specs=[row(3 * D_CONV), _resident((8, D_CONV))],
        out_shape=[jax.ShapeDtypeStruct((T, 3 * D_CONV), BF16), jax.ShapeDtypeStruct((8, D_CONV), F32)],
        compiler_params=_cp(1),
    )(dco, dco, bchf, bchf, bchf, bchf, bchf, bchf, conv_w)


def _mixer_in_bwd(dx1a, dqt, dkt, dvt, dbch, dfl, w_qkvt, w_bch, w_f):
    T = dx1a.shape[0]
    tm = _tile(T, TOKEN_TILE)

    def body(a_ref, dq_ref, dk_ref, dv_ref, db_ref, df_ref, wq_ref, wb_ref, wf_ref, o_ref):
        acc = a_ref[...] + _nt(db_ref[...], wb_ref[...]) + _nt(df_ref[...], wf_ref[...])
        for n, ref in enumerate((dq_ref, dk_ref, dv_ref)):
            acc = acc + _tn(ref[...].astype(BF16), wq_ref[n * D_ATTN:(n + 1) * D_ATTN, :])
        o_ref[...] = acc

    row = lambda n: pl.BlockSpec((tm, n), lambda i: (i, 0))
    col = pl.BlockSpec((D_ATTN, tm), lambda i: (0, i))
    return pl.pallas_call(
        body, name="mixer_in_bwd", grid=(T // tm,),
        in_specs=[row(D_MODEL), col, col, col, row(3 * D_CONV), row(N_FLOG),
                  _resident((3 * D_ATTN, D_MODEL)), _resident((D_MODEL, 3 * D_CONV)), _resident((D_MODEL, N_FLOG))],
        out_specs=row(D_MODEL),
        out_shape=jax.ShapeDtypeStruct((T, D_MODEL), F32),
        compiler_params=_cp(1),
    )(dx1a, dqt, dkt, dvt, dbch, dfl, w_qkvt, w_bch, w_f)


def _ple_loss(x3, p, tgt, w_g, w_p, b_g, lg, lb):
    T = x3.shape[0]
    tm = _tile(T, TOKEN_TILE)

    def body(x_ref, p_ref, t_ref, wg_ref, wp_ref, bg_ref, lg_ref, lb_ref,
             dx_ref, de_ref, dz_ref, loss_ref, dlg_ref, dlb_ref, dbg_ref):
        @pl.when(pl.program_id(0) == 0)
        def _():
            for ref in (loss_ref, dlg_ref, dlb_ref, dbg_ref):
                ref[...] = jnp.zeros_like(ref)

        xf = x_ref[...]
        gate = _sigmoid(_nn(xf.astype(BF16), wg_ref[...]) + bg_ref[...])
        e = _nn(p_ref[...].astype(BF16), wp_ref[...])
        xhat, rstd = _ln_stats(ALPHA * xf + gate * e)
        err = xhat * lg_ref[...] + lb_ref[...] - t_ref[...]
        sq = jnp.sum(_rowsum(err * err), axis=-1, keepdims=True)
        loss_ref[...] += jnp.broadcast_to(sq * (0.5 / D_MODEL), loss_ref.shape)
        dy = err * (1.0 / D_MODEL)
        dr = _ln_bwd(dy, xhat, rstd, lg_ref[...])
        dlg_ref[...] += _rowsum(dy * xhat)
        dlb_ref[...] += _rowsum(dy)
        de_ref[...] = (dr * gate).astype(BF16)
        dz = dr * e * gate * (1.0 - gate)
        dbg_ref[...] += _rowsum(dz)
        dzb = dz.astype(BF16)
        dz_ref[...] = dzb
        dx_ref[...] = ALPHA * dr + _nt(dzb, wg_ref[...])

    row = lambda n: pl.BlockSpec((tm, n), lambda i: (i, 0))
    vec = lambda n: _resident((1, n))
    return pl.pallas_call(
        body, name="ple_loss", grid=(T // tm,),
        in_specs=[row(D_MODEL), row(PLE_DIM), row(D_MODEL), _resident((D_MODEL, D_MODEL)),
                  _resident((PLE_DIM, D_MODEL)), vec(D_MODEL), vec(D_MODEL), vec(D_MODEL)],
        out_specs=[row(D_MODEL), row(D_MODEL), row(D_MODEL), vec(LANE), vec(D_MODEL), vec(D_MODEL), vec(D_MODEL)],
        out_shape=[jax.ShapeDtypeStruct((T, D_MODEL), F32), jax.ShapeDtypeStruct((T, D_MODEL), BF16),
                   jax.ShapeDtypeStruct((T, D_MODEL), BF16), jax.ShapeDtypeStruct((1, LANE), F32),
                   jax.ShapeDtypeStruct((1, D_MODEL), F32), jax.ShapeDtypeStruct((1, D_MODEL), F32),
                   jax.ShapeDtypeStruct((1, D_MODEL), F32)],
        compiler_params=_cp(1),
    )(x3, p, tgt, w_g, w_p, b_g, lg, lb)


def _from_row_layout(vr):
    return vr[:, :2, :].reshape(N_HEADS, -1).T


def _local_step(x, p, tgt, w, overlap=None):
    bf = lambda a: a.astype(BF16)
    w1i, w1o = bf(w["ffn1_w_in"]), bf(w["ffn1_w_out"])
    first = _ffn_fwd(x, w1i, w1o, w["ln1_g"], w["ln1_b"], "ffn1_fwd", overlap["gather"] if overlap else ())
    x1, r1, g1, u1, h1 = first[:5]
    if overlap:
        w = {**w, **overlap["weights"](first[5:])}
    w2i, w2o = bf(w["ffn2_w_in"]), bf(w["ffn2_w_out"])
    wmi = w["w_mix_in"]
    o_f = 3 * D_ATTN
    o_b = o_f + N_HEADS
    w_qkv = bf(wmi[:, :o_f])
    w_f = bf(jnp.pad(wmi[:, o_f:o_b], ((0, 0), (0, N_FLOG - N_HEADS))))
    w_bch = bf(wmi[:, o_b:])
    w_bchf = jnp.concatenate([w_bch, w_f], axis=1)
    w_mo, w_g, w_p = bf(w["w_mix_out"]), bf(w["w_ple_gate"]), bf(w["w_ple"])
    b_f = jnp.pad(w["b_forget"], ((0, 0), (0, N_FLOG - N_HEADS)))

    q_scale = jnp.concatenate([jnp.full((1, D_ATTN), Q_SCALE * LOG2E, F32), jnp.ones((1, 2 * D_ATTN), F32)], axis=1)
    qkv, qkvt = _matmul_nn(x1, w_qkv, q_scale, BF16, "proj_qkv", also_transposed=True)
    bchf = _matmul_nn(x1, w_bchf, jnp.ones((1, 3 * D_CONV + N_FLOG), F32), F32, "proj_bchf")
    fcol = 3 * D_CONV // N_FLOG
    c = _forget_cumsum(bchf, fcol, b_f)
    o, alrow = _attn_fwd(qkv, qkvt, _fold_key_bias(qkv, c))
    x2, r2, merged = _mixer_tail_fwd(o, bchf, w["conv_w"], w["g_attn"], w["g_conv"], w_mo, x1, w["ln2_g"], w["ln2_b"])
    x3, r3, g2, u2, h2 = _ffn_fwd(x2, w2i, w2o, w["ln3_g"], w["ln3_b"], "ffn2_fwd")

    grads = {}
    dx3, de, dz, loss, grads["ln4_g"], grads["ln4_b"], grads["b_ple_gate"] = _ple_loss(
        x3, p, tgt, w_g, w_p, w["b_ple_gate"], w["ln4_g"], w["ln4_b"])
    by_chip = overlap is not None
    grads["w_ple"] = _matmul_tn(p, de, "dw_ple")
    grads["w_ple_gate"] = _matmul_tn(x3, dz, "dw_ple_gate")

    dx2, dgu2, df2, grads["ln3_g"], grads["ln3_b"] = _ffn_bwd(dx3, r3, g2, u2, w2i, w2o, w["ln3_g"], "ffn2_bwd")
    grads["ffn2_w_in"] = _matmul_tn(x2, dgu2, "dw_ffn2_in", by_chip)
    grads["ffn2_w_out"] = _matmul_tn(h2, df2, "dw_ffn2_out")

    to_swap = overlap["swap"](grads) if overlap else ()
    tail = _mixer_tail_bwd(dx2, r2, w["ln2_g"], w_mo, o, bchf, w["conv_w"], w["g_attn"], w["g_conv"], to_swap)
    (dx1a, dr2, dobt, delta, dco, grads["ln2_g"], grads["ln2_b"], grads["g_attn"], grads["g_conv"]) = tail[:9]
    grads["w_mix_out"] = _matmul_tn(merged, dr2, "dw_mix_out")
    dbch, dcw = _conv_bwd(dco, bchf, w["conv_w"])
    grads["conv_w"] = dcw[:3]
    dqt, dkt, dvt, dck, dcq = _attn_bwd(qkv, qkvt, dobt, c, alrow, delta.reshape(N_HEADS // 2, 8, -1))
    dcq_lanes = jnp.pad(_from_row_layout(dcq), ((0, 0), (0, N_FLOG - N_HEADS)))
    dfl, dbf = _forget_bwd(dck, dcq_lanes, bchf, fcol, b_f)
    grads["b_forget"] = dbf[:, :N_HEADS]
    dx1 = _mixer_in_bwd(dx1a, dqt, dkt, dvt, dbch, dfl, w_qkv.T, w_bch, w_f)
    grads["w_mix_in"] = jnp.concatenate(
        [_matmul_tokens(dqt, x1, "dw_q").T, _matmul_tokens(dkt, x1, "dw_k").T, _matmul_tokens(dvt, x1, "dw_v").T,
         _matmul_tn(x1, dfl, "dw_flog")[:, :N_HEADS], _matmul_tn(x1, dbch, "dw_bch")], axis=1)

    dx0, dgu1, df1, grads["ln1_g"], grads["ln1_b"] = _ffn_bwd(dx1, r1, g1, u1, w1i, w1o, w["ln1_g"], "ffn1_bwd")
    grads["ffn1_w_out"] = _matmul_tn(h1, df1, "dw_ffn1_out")
    if not overlap:
        grads["ffn1_w_in"] = _matmul_tn(x, dgu1, "dw_ffn1_in")
        return loss, dx0, grads
    sums = overlap["chip_sums"](grads, to_swap, tail[9:])
    grads["ffn1_w_in"], *received = _matmul_tn(x, dgu1, "dw_ffn1_in", by_chip, exchange=sums)
    return loss, dx0, grads, sums, received


WEIGHTS = ["ffn1_w_in", "ffn1_w_out", "ln1_g", "ln1_b", "w_mix_in", "b_forget", "conv_w", "g_attn", "g_conv",
           "w_mix_out", "ln2_g", "ln2_b", "ffn2_w_in", "ffn2_w_out", "ln3_g", "ln3_b", "w_ple", "w_ple_gate",
           "b_ple_gate", "ln4_g", "ln4_b"]
LAYOUT = {
    "ffn1_w_in": ((D_MODEL, 2 * D_FF), 1), "ffn1_w_out": ((D_FF, D_MODEL), 0),
    "w_mix_in": ((D_MODEL, 3 * D_ATTN + N_HEADS + 3 * D_CONV), 1), "conv_w": ((3, D_CONV), 1),
    "w_mix_out": ((D_MODEL, D_MODEL), 0), "ffn2_w_in": ((D_MODEL, 2 * D_FF), 1), "ffn2_w_out": ((D_FF, D_MODEL), 0),
    "w_ple": ((PLE_DIM, D_MODEL), 1), "w_ple_gate": ((D_MODEL, D_MODEL), 0),
    "ln1_g": ((1, D_MODEL), None), "ln1_b": ((1, D_MODEL), None), "b_forget": ((1, N_HEADS), None),
    "g_attn": ((1, D_ATTN), None), "g_conv": ((1, D_CONV), None), "ln2_g": ((1, D_MODEL), None),
    "ln2_b": ((1, D_MODEL), None), "ln3_g": ((1, D_MODEL), None), "ln3_b": ((1, D_MODEL), None),
    "b_ple_gate": ((1, D_MODEL), None), "ln4_g": ((1, D_MODEL), None), "ln4_b": ((1, D_MODEL), None),
}
BIG = [n for n in WEIGHTS if LAYOUT[n][1] is not None and n != "conv_w"]
SMALL = [n for n in WEIGHTS if n not in BIG]
ROW = 1024
SMALL_ROWS = 16


def _shard_shape(name):
    shape, axis = LAYOUT[name]
    if axis is None:
        return shape
    return tuple(s // N_CHIPS if a == axis else s for a, s in enumerate(shape))


def _halves(a):
    return a.reshape(a.shape[:-2] + (2, a.shape[-2] // 2, a.shape[-1]))


def _split_chips(name, full):
    shape, axis = LAYOUT[name]
    if axis == 0:
        return full.reshape((N_CHIPS, shape[0] // N_CHIPS) + shape[1:])
    return jnp.moveaxis(full.reshape(shape[:1] + (N_CHIPS, shape[1] // N_CHIPS)), 1, 0)


def _join_chips(name, parts):
    shape, axis = LAYOUT[name]
    if axis == 0:
        return parts.reshape(shape)
    return jnp.moveaxis(parts, 0, 1).reshape(shape)


SMALL_AT = {"ln1_g": (0, 0), "ln1_b": (1, 0), "ln2_g": (2, 0), "ln2_b": (3, 0), "ln3_g": (4, 0), "ln3_b": (5, 0),
            "b_ple_gate": (6, 0), "ln4_g": (7, 0), "ln4_b": (8, 0), "g_attn": (9, 0), "g_conv": (9, D_ATTN),
            "b_forget": (10, 0), "conv_w": (10, LANE)}
CONV_SHARD = D_CONV // N_CHIPS


def _pack_small_grads(grads):
    def body(*refs):
        ins, o_ref = dict(zip(SMALL, refs[:-1])), refs[-1]
        o_ref[...] = jnp.zeros_like(o_ref)
        for s in range(N_CHIPS):
            for n in SMALL:
                r, c0 = SMALL_AT[n]
                if n == "conv_w":
                    for k in range(3):
                        o_ref[s, r:r + 1, c0 + k * CONV_SHARD:c0 + (k + 1) * CONV_SHARD] = (
                            ins[n][k:k + 1, s * CONV_SHARD:(s + 1) * CONV_SHARD])
                else:
                    o_ref[s, r:r + 1, c0:c0 + ins[n].shape[1]] = ins[n][...]

    return pl.pallas_call(
        body, name="pack_small_grads",
        out_shape=jax.ShapeDtypeStruct((N_CHIPS, SMALL_ROWS, ROW), F32),
    )(*[grads[n] for n in SMALL])


def _adamw_math(w, g, m, v):
    c1 = 1.0 - ADAM_B1 ** ADAM_STEP
    c2 = 1.0 - ADAM_B2 ** ADAM_STEP
    m = ADAM_B1 * m + (1.0 - ADAM_B1) * g
    v = ADAM_B2 * v + (1.0 - ADAM_B2) * (g * g)
    return -ADAM_LR * ((m / c1) / (jnp.sqrt(v / c2) + ADAM_EPS) + ADAM_WD * w), m, v


def _adamw_small(g_mine, g_sib, c_idx, w, m, v):
    ns = len(SMALL)

    def body(c_ref, gm_ref, gs_ref, *refs):
        ws, ms, vs = refs[:ns], refs[ns:2 * ns], refs[2 * ns:3 * ns]
        outs = refs[3 * ns:]
        mine_first = c_ref[0] == 0
        top = jnp.where(mine_first, gm_ref[...], gs_ref[...])
        bot = jnp.where(mine_first, gs_ref[...], gm_ref[...])
        for i, n in enumerate(SMALL):
            r, c0 = SMALL_AT[n]
            blk, rr = (top, r) if r < SMALL_ROWS // 2 else (bot, r - SMALL_ROWS // 2)
            rows, width = ws[i].shape
            for k in range(rows):
                g = blk[rr:rr + 1, c0 + k * width:c0 + (k + 1) * width]
                d, mn, vn = _adamw_math(ws[i][k:k + 1, :], g, ms[i][k:k + 1, :], vs[i][k:k + 1, :])
                for q, val in enumerate((g, d, mn, vn)):
                    outs[q * ns + i][k:k + 1, :] = val

    shapes = [jax.ShapeDtypeStruct(a.shape, F32) for a in w]
    vmem = pl.BlockSpec(memory_space=pltpu.VMEM)
    res = pl.pallas_call(
        body, name="adamw_small",
        in_specs=[pl.BlockSpec(memory_space=pltpu.SMEM)] + [vmem] * (2 + 3 * ns),
        out_specs=[vmem] * (4 * ns),
        out_shape=shapes * 4,
    )(c_idx, g_mine, g_sib, *w, *m, *v)
    return [res[q * ns:(q + 1) * ns] for q in range(4)]


def _place():
    x, y, c = lax.axis_index("x"), lax.axis_index("y"), lax.axis_index("c")
    others = [(1 - x, y), (x, 1 - y), (1 - x, 1 - y)]
    return x, y, c, others


ANY_SPEC = pl.BlockSpec(memory_space=pl.ANY)


def _remote(src, dst, send_sems, recv_sems, k, to):
    return pltpu.make_async_remote_copy(src_ref=src, dst_ref=dst, send_sem=send_sems.at[k], recv_sem=recv_sems.at[k],
                                        device_id=to, device_id_type=MESH)


def _all_gather(shards):
    n = len(shards)

    def body(*refs):
        ins, outs, send_sems, recv_sems = refs[:n], refs[n:2 * n], refs[2 * n], refs[2 * n + 1]
        _gather_start(ins, outs, send_sems, recv_sems)
        _gather_finish(ins, outs, send_sems, recv_sems)

    return pl.pallas_call(
        body, name="all_gather_weights",
        out_shape=_gather_shapes(shards), in_specs=[ANY_SPEC] * n, out_specs=[ANY_SPEC] * n,
        scratch_shapes=_gather_sems(n),
    )(*shards)


def _gather_shapes(shards):
    return [jax.ShapeDtypeStruct((N_CHIPS,) + a.shape, a.dtype) for a in shards]


def _gather_sems(n):
    return [pltpu.SemaphoreType.DMA((6 * n,)), pltpu.SemaphoreType.DMA((6 * n,))]


def _gather_sends(ins, outs, send_sems, recv_sems):
    x, y, c, others = _place()
    s = 2 * x + y
    return [_remote(ins[t].at[c], outs[t].at[s, c], send_sems, recv_sems, 6 * t + j, (*chip, c))
            for t in range(len(ins)) for j, chip in enumerate(others)]


def _gather_start(ins, outs, send_sems, recv_sems):
    for cp in _gather_sends(ins, outs, send_sems, recv_sems):
        cp.start()


def _gather_finish(ins, outs, send_sems, recv_sems):
    x, y, c, others = _place()
    slot = lambda t, chip, half: outs[t].at[2 * chip[0] + chip[1], half]
    passed = []
    for t in range(len(ins)):
        for j, chip in enumerate(others):
            landed = slot(t, chip, c)
            _remote(landed, landed, send_sems, recv_sems, 6 * t + j, (x, y, c)).wait_recv()
            passed.append(_remote(landed, landed, send_sems, recv_sems, 6 * t + 3 + j, (x, y, 1 - c)))
            passed[-1].start()
    for t in range(len(ins)):
        for j, chip in enumerate(others):
            landed = slot(t, chip, 1 - c)
            _remote(landed, landed, send_sems, recv_sems, 6 * t + 3 + j, (x, y, c)).wait_recv()
    for cp in _gather_sends(ins, outs, send_sems, recv_sems) + passed:
        cp.wait_send()


def _swap_halves(gs, tag):
    n = len(gs)

    def body(*refs):
        ins, outs, send_sems, recv_sems = refs[:n], refs[n:2 * n], refs[2 * n], refs[2 * n + 1]
        _swap_start(ins, outs, send_sems, recv_sems)
        _swap_finish(ins, outs, send_sems, recv_sems)

    return pl.pallas_call(
        body, name="grad_swap_halves_" + tag,
        out_shape=_swap_shapes(gs), in_specs=[ANY_SPEC] * n, out_specs=[ANY_SPEC] * n,
        scratch_shapes=_swap_sems(n),
    )(*gs)


def _swap_shapes(gs):
    return [jax.ShapeDtypeStruct(g.shape[:1] + g.shape[2:], g.dtype) for g in gs]


def _swap_sems(n):
    return [pltpu.SemaphoreType.DMA((n,)), pltpu.SemaphoreType.DMA((n,))]


def _swap_copies(ins, outs, send_sems, recv_sems):
    x, y, c, _ = _place()
    return [_remote(ins[t].at[:, 1 - c], outs[t], send_sems, recv_sems, t, (x, y, 1 - c)) for t in range(len(ins))]


def _swap_start(ins, outs, send_sems, recv_sems):
    for cp in _swap_copies(ins, outs, send_sems, recv_sems):
        cp.start()


def _swap_finish(ins, outs, send_sems, recv_sems):
    for cp in _swap_copies(ins, outs, send_sems, recv_sems):
        cp.wait()


def _exchange_chips(pps):
    n = len(pps)

    def body(*refs):
        ins, outs, send_sems, recv_sems = refs[:n], refs[n:2 * n], refs[2 * n], refs[2 * n + 1]
        _exchange_start(ins, outs, send_sems, recv_sems)
        _exchange_finish(ins, outs, send_sems, recv_sems)

    return pl.pallas_call(
        body, name="grad_exchange_chips",
        out_shape=_exchange_shapes(pps), in_specs=[ANY_SPEC] * n, out_specs=[ANY_SPEC] * n,
        scratch_shapes=_exchange_sems(n),
    )(*pps)


def _exchange_shapes(pps):
    return [jax.ShapeDtypeStruct(p.shape, p.dtype) for p in pps]


def _exchange_sems(n):
    return [pltpu.SemaphoreType.DMA((3 * n,)), pltpu.SemaphoreType.DMA((3 * n,))]


def _exchange_sends(ins, outs, send_sems, recv_sems):
    x, y, c, others = _place()
    s = 2 * x + y
    return [_remote(ins[t].at[2 * chip[0] + chip[1]], outs[t].at[s], send_sems, recv_sems, 3 * t + j, (*chip, c))
            for t in range(len(ins)) for j, chip in enumerate(others)]


def _exchange_start(ins, outs, send_sems, recv_sems):
    for cp in _exchange_sends(ins, outs, send_sems, recv_sems):
        cp.start()


def _exchange_finish(ins, outs, send_sems, recv_sems):
    x, y, c, others = _place()
    for t in range(len(ins)):
        for j, chip in enumerate(others):
            landed = outs[t].at[2 * chip[0] + chip[1]]
            _remote(landed, landed, send_sems, recv_sems, 3 * t + j, (x, y, c)).wait_recv()
    for cp in _exchange_sends(ins, outs, send_sems, recv_sems):
        cp.wait_send()


def _share_half(rs):
    n = len(rs)

    def body(*refs):
        ins, outs, send_sems, recv_sems = refs[:n], refs[n:2 * n], refs[2 * n], refs[2 * n + 1]
        x, y, c, _ = _place()
        copies = [_remote(ins[t], outs[t], send_sems, recv_sems, t, (x, y, 1 - c)) for t in range(n)]
        for cp in copies:
            cp.start()
        for cp in copies:
            cp.wait()

    return pl.pallas_call(
        body, name="grad_share_half",
        out_shape=[jax.ShapeDtypeStruct(r.shape, r.dtype) for r in rs],
        in_specs=[ANY_SPEC] * n, out_specs=[ANY_SPEC] * n,
        scratch_shapes=[pltpu.SemaphoreType.DMA((n,)), pltpu.SemaphoreType.DMA((n,))],
    )(*rs)


ELEMENTWISE_BLOCK_BYTES = 1 << 20


def _row_tile(rows, cols):
    return _tile(rows, max(8, ELEMENTWISE_BLOCK_BYTES // (4 * cols) // 8 * 8))


def _add_my_half(g, sib, c_idx, name):
    rh, cols = g.shape[2:]
    tr = _row_tile(rh, cols)

    def body(c_ref, g_ref, s_ref, o_ref):
        o_ref[...] = (g_ref[...] + s_ref[...]).astype(BF16)

    return pl.pallas_call(
        body, name="grad_add_halves_" + name,
        grid_spec=pltpu.PrefetchScalarGridSpec(
            num_scalar_prefetch=1, grid=(N_CHIPS, rh // tr),
            in_specs=[pl.BlockSpec((None, None, tr, cols), lambda s, i, c: (s, c[0], i, 0)),
                      pl.BlockSpec((None, tr, cols), lambda s, i, c: (s, i, 0))],
            out_specs=pl.BlockSpec((None, tr, cols), lambda s, i, c: (s, i, 0))),
        out_shape=jax.ShapeDtypeStruct((N_CHIPS, rh, cols), BF16),
        compiler_params=_cp(2),
    )(c_idx, g, sib)


def _sum_chips(parts, pp, s_idx, name):
    rh, cols = parts.shape[1:]
    tr = _row_tile(rh, cols)

    def body(s_ref, p0, p1, p2, p3, mine_ref, o_ref):
        own = mine_ref[...]
        t = [jnp.where(s_ref[0] == k, own, p[...]).astype(F32) for k, p in enumerate((p0, p1, p2, p3))]
        o_ref[...] = ((t[0] + t[1]) + t[2]) + t[3]

    slot = lambda k: pl.BlockSpec((None, tr, cols), lambda i, s: (jnp.where(s[0] == k, (k + 1) % N_CHIPS, k), i, 0))
    return pl.pallas_call(
        body, name="grad_sum_chips_" + name,
        grid_spec=pltpu.PrefetchScalarGridSpec(
            num_scalar_prefetch=1, grid=(rh // tr,),
            in_specs=[slot(0), slot(1), slot(2), slot(3), pl.BlockSpec((None, tr, cols), lambda i, s: (s[0], i, 0))],
            out_specs=pl.BlockSpec((tr, cols), lambda i, s: (i, 0))),
        out_shape=jax.ShapeDtypeStruct((rh, cols), F32),
        compiler_params=_cp(1),
    )(s_idx, parts, parts, parts, parts, pp)


def _adamw(w, g_mine, g_sib, m, v, c_idx, name):
    rows, cols = w.shape
    tr = _row_tile(rows // 2, cols)
    nbh = rows // 2 // tr

    def body(c_ref, w_ref, gm_ref, gs_ref, m_ref, v_ref, g_ref, d_ref, mo_ref, vo_ref):
        g = jnp.where(pl.program_id(0) // nbh == c_ref[0], gm_ref[...], gs_ref[...])
        g_ref[...] = g
        d_ref[...], mo_ref[...], vo_ref[...] = _adamw_math(w_ref[...], g, m_ref[...], v_ref[...])

    spec = pl.BlockSpec((tr, cols), lambda i, c: (i, 0))
    half = pl.BlockSpec((tr, cols), lambda i, c: (i % nbh, 0))
    return pl.pallas_call(
        body, name="adamw_" + name,
        grid_spec=pltpu.PrefetchScalarGridSpec(
            num_scalar_prefetch=1, grid=(rows // tr,),
            in_specs=[spec, half, half, spec, spec], out_specs=[spec] * 4),
        out_shape=[jax.ShapeDtypeStruct(w.shape, F32)] * 4,
        compiler_params=_cp(1),
    )(c_idx, w, g_mine, g_sib, m, v)


def kernel(x, p, ffn1_w_in, ffn1_w_out, ln1_g, ln1_b, w_mix_in, b_forget, conv_w, g_attn, g_conv, w_mix_out, ln2_g, ln2_b, ffn2_w_in, ffn2_w_out, ln3_g, ln3_b, w_ple, w_ple_gate, b_ple_gate, ln4_g, ln4_b, loss_target, m_ffn1_w_in, m_ffn1_w_out, m_ln1_g, m_ln1_b, m_w_mix_in, m_b_forget, m_conv_w, m_g_attn, m_g_conv, m_w_mix_out, m_ln2_g, m_ln2_b, m_ffn2_w_in, m_ffn2_w_out, m_ln3_g, m_ln3_b, m_w_ple, m_w_ple_gate, m_b_ple_gate, m_ln4_g, m_ln4_b, v_ffn1_w_in, v_ffn1_w_out, v_ln1_g, v_ln1_b, v_w_mix_in, v_b_forget, v_conv_w, v_g_attn, v_g_conv, v_w_mix_out, v_ln2_g, v_ln2_b, v_ffn2_w_in, v_ffn2_w_out, v_ln3_g, v_ln3_b, v_w_ple, v_w_ple_gate, v_b_ple_gate, v_ln4_g, v_ln4_b):
    args = dict(locals())
    shard = {n: args[n][0] if LAYOUT[n][1] is not None else args[n] for n in WEIGHTS}
    m_shard = {n: args["m_" + n][0] if LAYOUT[n][1] is not None else args["m_" + n] for n in WEIGHTS}
    v_shard = {n: args["v_" + n][0] if LAYOUT[n][1] is not None else args["v_" + n] for n in WEIGHTS}
    c_idx = lax.axis_index("c").astype(jnp.int32).reshape(1)
    chip = (2 * lax.axis_index("x") + lax.axis_index("y")).astype(jnp.int32)

    conv_rows = SMALL_ROWS - shard["conv_w"].shape[0]
    mine = {n: _halves(shard[n].astype(BF16)) for n in BIG}
    mine["conv_w"] = _halves(jnp.pad(shard["conv_w"], ((0, conv_rows), (0, 0))))
    early_w = ["ffn1_w_in", "ffn1_w_out"]
    late_w = [n for n in BIG if n not in early_w] + ["conv_w"]

    def full_weights(names, gathered):
        out = {}
        for n, theirs in zip(names, gathered):
            g = lax.dynamic_update_slice(theirs, mine[n][None], (chip, 0, 0, 0))
            if n == "conv_w":
                out[n] = _join_chips(n, g.reshape(N_CHIPS, SMALL_ROWS, CONV_SHARD)[:, :3])
            else:
                out[n] = _join_chips(n, g.reshape((N_CHIPS,) + _shard_shape(n)))
        return out

    full = full_weights(early_w, _all_gather([mine[n] for n in early_w]))
    full.update({n: shard[n] for n in SMALL if n != "conv_w"})

    def per_chip(names, grads):
        by_chip = lambda n: grads[n] if grads[n].ndim == 3 else _split_chips(n, grads[n])
        return [_halves(_pack_small_grads(grads) if n == "small" else by_chip(n)) for n in names]

    def add_halves(names, mine_, sibs):
        return [_add_my_half(g, sib, c_idx, n) for n, g, sib in zip(names, mine_, sibs)]

    def chip_sums(names, grads):
        mine_ = per_chip(names, grads)
        return add_halves(names, mine_, _swap_halves(mine_, names[0]))

    ready_a = ["ffn2_w_in", "ffn2_w_out", "w_ple", "w_ple_gate"]
    ready_b = ["w_mix_in", "w_mix_out", "ffn1_w_out"]
    early_g = ready_a + ready_b
    late_g = ["ffn1_w_in", "small"]
    loss_acc, grad_x, grads, early_sums, early_parts = _local_step(
        x[0], p[0, 0], loss_target[0], full,
        overlap={"gather": [mine[n] for n in late_w], "weights": lambda gathered: full_weights(late_w, gathered),
                 "swap": lambda grads: per_chip(ready_a, grads),
                 "chip_sums": lambda grads, swapped, received: (add_halves(ready_a, swapped, received)
                                                                + chip_sums(ready_b, grads))})
    loss = lax.psum(loss_acc[0, 0], ("x", "y", "c"))

    late_sums = chip_sums(late_g, grads)
    names = early_g + late_g
    sums = list(early_sums) + late_sums
    parts = list(early_parts) + list(_exchange_chips(late_sums))
    half_of = {n: _sum_chips(pt, own, chip.reshape(1), n) for n, pt, own in zip(names, parts, sums)}
    names = BIG + ["small"]
    my_half = [half_of[n] for n in names]
    sib_half = _share_half(my_half)

    out = {}
    for n, gm, gs in zip(BIG, my_half, sib_half):
        out[n] = [a[None] for a in _adamw(shard[n], gm, gs, m_shard[n], v_shard[n], c_idx, n)]
    small = _adamw_small(my_half[-1], sib_half[-1], c_idx, [shard[n] for n in SMALL], [m_shard[n] for n in SMALL],
                         [v_shard[n] for n in SMALL])
    for i, n in enumerate(SMALL):
        out[n] = [small[q][i][None] if n == "conv_w" else small[q][i] for q in range(4)]
    return (loss, grad_x[None], *[out[n][q] for q in range(4) for n in WEIGHTS])
```

```python
import math

import jax
import jax.numpy as jnp
from jax import lax
from jax.experimental import pallas as pl
from jax.experimental.pallas import tpu as pltpu

F32 = jnp.float32
BF16 = jnp.bfloat16

D_MODEL = 1024
D_FF = 2816
N_HEADS = 8
HEAD_DIM = 64
D_ATTN = N_HEADS * HEAD_DIM
D_CONV = 512
PLE_DIM = 256
N_FLOG = 128
ALPHA = 2.0 ** 0.25
LN_EPS = 1e-5
RMS_EPS = 1e-6
NEG_INF = -1e30
Q_SCALE = 1.0 / math.sqrt(HEAD_DIM)
LOG2E = math.log2(math.e)

ADAM_LR = 0.001
ADAM_B1 = 0.9
ADAM_B2 = 0.999
ADAM_EPS = 1e-08
ADAM_WD = 0.01
ADAM_STEP = 10

V7X_VMEM_BYTES = 64 << 20
VMEM_LIMIT = V7X_VMEM_BYTES - (8 << 20)
LANE = 128
FF_CHUNK = 256
N_CHIPS = 4
TOKEN_TILE = 512
FFN_BWD_TILE = 256
ATTN_TILE = 512
SCAN_BLOCK = 512
WGRAD_TOKENS = 2048
WGRAD_T_TOKENS = 1024
MESH = pl.DeviceIdType.MESH


def _cp(n_axes):
    return pltpu.CompilerParams(dimension_semantics=("arbitrary",) * n_axes, vmem_limit_bytes=VMEM_LIMIT)


def _resident(shape):
    n = len(shape)
    return pl.BlockSpec(shape, lambda *_: (0,) * n, pipeline_mode=pl.Buffered(1))


def _nn(a, b):
    return jnp.dot(a, b, preferred_element_type=F32)


def _nt(a, b):
    return lax.dot_general(a, b, (((1,), (1,)), ((), ())), preferred_element_type=F32)


def _tn(a, b):
    return lax.dot_general(a, b, (((0,), (0,)), ((), ())), preferred_element_type=F32)


def _ln_stats(r):
    mu = jnp.mean(r, axis=-1, keepdims=True)
    xc = r - mu
    var = jnp.mean(xc * xc, axis=-1, keepdims=True)
    rstd = lax.rsqrt(var + LN_EPS)
    return xc * rstd, rstd


def _ln_bwd(dy, xhat, rstd, g):
    dxh = dy * g
    m1 = jnp.mean(dxh, axis=-1, keepdims=True)
    m2 = jnp.mean(dxh * xhat, axis=-1, keepdims=True)
    return rstd * (dxh - m1 - xhat * m2)


def _sigmoid(z):
    return 1.0 / (1.0 + jnp.exp(-z))


def _rowsum(a):
    return jnp.sum(a, axis=0, keepdims=True)


def _tile(total, want):
    if total <= want:
        return total
    for t in range(want - want % 8, 0, -8):
        if total % t == 0:
            return t
    raise ValueError((total, want))


def _ffn_fwd(x, w_in, w_out, lg, lb, name, gather=()):
    T = x.shape[0]
    tm = _tile(T, TOKEN_TILE)
    nf = D_FF // FF_CHUNK
    ng = len(gather)
    last = T // tm - 1

    def body(x_ref, wi_ref, wo_ref, lg_ref, lb_ref, *rest):
        comm_in, (xo_ref, r_ref, g_ref, u_ref, h_ref) = rest[:ng], rest[ng:ng + 5]
        comm_out, sems = rest[ng + 5:2 * ng + 5], rest[2 * ng + 5:]
        if ng:
            @pl.when(pl.program_id(0) == 0)
            def _():
                _gather_start(comm_in, comm_out, *sems)

        xf = x_ref[...]
        xb = xf.astype(BF16)
        acc = jnp.zeros((tm, D_MODEL), F32)
        for j in range(nf):
            c0 = j * FF_CHUNK
            g = _nn(xb, wi_ref[:, c0:c0 + FF_CHUNK])
            u = _nn(xb, wi_ref[:, D_FF + c0:D_FF + c0 + FF_CHUNK])
            hb = (g * _sigmoid(g) * u).astype(BF16)
            g_ref[:, c0:c0 + FF_CHUNK] = g.astype(BF16)
            u_ref[:, c0:c0 + FF_CHUNK] = u.astype(BF16)
            h_ref[:, c0:c0 + FF_CHUNK] = hb
            acc = acc + _nn(hb, wo_ref[c0:c0 + FF_CHUNK, :])
        r = ALPHA * xf + 0.5 * acc
        r_ref[...] = r
        xhat, _ = _ln_stats(r)
        xo_ref[...] = xhat * lg_ref[...] + lb_ref[...]
        if ng:
            @pl.when(pl.program_id(0) == max(last - 4, 0))
            def _():
                _gather_forward(comm_in, comm_out, *sems)

            @pl.when(pl.program_id(0) == last)
            def _():
                _gather_drain(comm_in, comm_out, *sems)

    row = lambda n: pl.BlockSpec((tm, n), lambda i: (i, 0))
    return pl.pallas_call(
        body, name=name, grid=(T // tm,),
        in_specs=[row(D_MODEL), _resident((D_MODEL, 2 * D_FF)), _resident((D_FF, D_MODEL)),
                  _resident((1, D_MODEL)), _resident((1, D_MODEL))] + [ANY_SPEC] * ng,
        out_specs=[row(D_MODEL), row(D_MODEL), row(D_FF), row(D_FF), row(D_FF)] + [ANY_SPEC] * ng,
        out_shape=[jax.ShapeDtypeStruct((T, D_MODEL), F32), jax.ShapeDtypeStruct((T, D_MODEL), F32),
                   jax.ShapeDtypeStruct((T, D_FF), BF16), jax.ShapeDtypeStruct((T, D_FF), BF16),
                   jax.ShapeDtypeStruct((T, D_FF), BF16)] + _gather_shapes(gather),
        scratch_shapes=_gather_sems(ng) if ng else [],
        compiler_params=_cp(1),
    )(x, w_in, w_out, lg, lb, *gather)


def _ffn_bwd(dxo, r, g, u, w_in, w_out, lg, name, exchange=()):
    T = r.shape[0]
    tm = _tile(T, FFN_BWD_TILE)
    nf = D_FF // FF_CHUNK
    ne = len(exchange)
    last = T // tm - 1

    def body(dxo_ref, r_ref, g_ref, u_ref, wi_ref, wo_ref, lg_ref, *rest):
        comm_in, (dx_ref, dgu_ref, df_ref, dlg_ref, dlb_ref) = rest[:ne], rest[ne:ne + 5]
        comm_out, sems = rest[ne + 5:2 * ne + 5], rest[2 * ne + 5:]
        i = pl.program_id(0)
        if ne:
            @pl.when(i == 0)
            def _():
                _exchange_start(comm_in, comm_out, *sems)

        dy = dxo_ref[...]
        xhat, rstd = _ln_stats(r_ref[...])
        dr = _ln_bwd(dy, xhat, rstd, lg_ref[...])

        @pl.when(i == 0)
        def _():
            dlg_ref[...] = jnp.zeros_like(dlg_ref)
            dlb_ref[...] = jnp.zeros_like(dlb_ref)

        dlg_ref[...] += _rowsum(dy * xhat)
        dlb_ref[...] += _rowsum(dy)
        dfb = (0.5 * dr).astype(BF16)
        df_ref[...] = dfb
        acc = jnp.zeros((tm, D_MODEL), F32)
        dh_ahead = _nt(dfb, wo_ref[0:FF_CHUNK, :])
        for j in range(nf):
            c0 = j * FF_CHUNK
            dh = dh_ahead
            if j + 1 < nf:
                dh_ahead = _nt(dfb, wo_ref[c0 + FF_CHUNK:c0 + 2 * FF_CHUNK, :])
            gg = g_ref[:, c0:c0 + FF_CHUNK].astype(F32)
            uu = u_ref[:, c0:c0 + FF_CHUNK].astype(F32)
            s = _sigmoid(gg)
            dgb = (dh * uu * s * (1.0 + gg * (1.0 - s))).astype(BF16)
            dub = (dh * gg * s).astype(BF16)
            dgu_ref[:, c0:c0 + FF_CHUNK] = dgb
            dgu_ref[:, D_FF + c0:D_FF + c0 + FF_CHUNK] = dub
            acc = acc + _nt(dgb, wi_ref[:, c0:c0 + FF_CHUNK]) + _nt(dub, wi_ref[:, D_FF + c0:D_FF + c0 + FF_CHUNK])
        dx_ref[...] = ALPHA * dr + acc
        if ne:
            @pl.when(i == last)
            def _():
                _exchange_finish(comm_in, comm_out, *sems)

    row = lambda n: pl.BlockSpec((tm, n), lambda i: (i, 0))
    return pl.pallas_call(
        body, name=name, grid=(T // tm,),
        in_specs=[row(D_MODEL), row(D_MODEL), row(D_FF), row(D_FF), _resident((D_MODEL, 2 * D_FF)),
                  _resident((D_FF, D_MODEL)), _resident((1, D_MODEL))] + [ANY_SPEC] * ne,
        out_specs=[row(D_MODEL), row(2 * D_FF), row(D_MODEL), _resident((1, D_MODEL)), _resident((1, D_MODEL))]
        + [ANY_SPEC] * ne,
        out_shape=[jax.ShapeDtypeStruct((T, D_MODEL), F32), jax.ShapeDtypeStruct((T, 2 * D_FF), BF16),
                   jax.ShapeDtypeStruct((T, D_MODEL), BF16), jax.ShapeDtypeStruct((1, D_MODEL), F32),
                   jax.ShapeDtypeStruct((1, D_MODEL), F32)] + _exchange_shapes(exchange),
        scratch_shapes=_exchange_sems(ne) if ne else [],
        compiler_params=_cp(1),
    )(dxo, r, g, u, w_in, w_out, lg, *exchange)


def _matmul_tn(a, b, name, by_chip=False, exchange=()):
    T, K = a.shape
    N = b.shape[1]
    tt = _tile(T, WGRAD_TOKENS)
    tn = N // N_CHIPS if by_chip else N
    while K * tn * 4 > (6 << 20) and tn % 256 == 0 and not by_chip:
        tn //= 2
    assert N % tn == 0
    ne = len(exchange)
    grid = (N // tn, T // tt)

    def body(a_ref, b_ref, *rest):
        comm_in, o_ref, comm_out, sems = rest[:ne], rest[ne], rest[ne + 1:2 * ne + 1], rest[2 * ne + 1:]
        n, t = pl.program_id(0), pl.program_id(1)
        if ne:
            @pl.when((n == 0) & (t == 0))
            def _():
                _exchange_start(comm_in, comm_out, *sems)

        @pl.when(t == 0)
        def _():
            o_ref[...] = jnp.zeros_like(o_ref)

        o_ref[...] += _tn(a_ref[...].astype(BF16), b_ref[...].astype(BF16))
        if ne:
            @pl.when((n == grid[0] - 1) & (t == grid[1] - 1))
            def _():
                _exchange_finish(comm_in, comm_out, *sems)

    res = pl.pallas_call(
        body, name=name, grid=grid,
        in_specs=[pl.BlockSpec((tt, K), lambda n, t: (t, 0)), pl.BlockSpec((tt, tn), lambda n, t: (t, n))]
        + [ANY_SPEC] * ne,
        out_specs=[pl.BlockSpec((None, K, tn), lambda n, t: (n, 0, 0)) if by_chip
                   else pl.BlockSpec((K, tn), lambda n, t: (0, n))] + [ANY_SPEC] * ne,
        out_shape=[jax.ShapeDtypeStruct((N_CHIPS, K, tn) if by_chip else (K, N), F32)] + _exchange_shapes(exchange),
        scratch_shapes=_exchange_sems(ne) if ne else [],
        compiler_params=_cp(2),
    )(a, b, *exchange)
    return res if ne else res[0]


def _matmul_tokens(at, b, name):
    M, T = at.shape
    N = b.shape[1]
    tt = _tile(T, WGRAD_T_TOKENS)

    def body(a_ref, b_ref, o_ref):
        @pl.when(pl.program_id(0) == 0)
        def _():
            o_ref[...] = jnp.zeros_like(o_ref)

        o_ref[...] += _nn(a_ref[...].astype(BF16), b_ref[...].astype(BF16))

    return pl.pallas_call(
        body, name=name, grid=(T // tt,),
        in_specs=[pl.BlockSpec((M, tt), lambda t: (0, t)), pl.BlockSpec((tt, N), lambda t: (t, 0))],
        out_specs=pl.BlockSpec((M, N), lambda t: (0, 0)),
        out_shape=jax.ShapeDtypeStruct((M, N), F32),
        compiler_params=_cp(1),
    )(at, b)


def _matmul_nn(x, w, scale, out_dtype, name, also_transposed=False):
    T, K = x.shape
    N = w.shape[1]
    tm = _tile(T, TOKEN_TILE)

    def body(x_ref, w_ref, s_ref, o_ref, *ot_ref):
        res = _nn(x_ref[...].astype(BF16), w_ref[...]) * s_ref[...]
        o_ref[...] = res.astype(out_dtype)
        if also_transposed:
            ot_ref[0][...] = res.T.astype(out_dtype)

    res = pl.pallas_call(
        body, name=name, grid=(T // tm,),
        in_specs=[pl.BlockSpec((tm, K), lambda i: (i, 0)), _resident((K, N)), _resident((1, N))],
        out_specs=[pl.BlockSpec((tm, N), lambda i: (i, 0))] + [pl.BlockSpec((N, tm), lambda i: (0, i))] * also_transposed,
        out_shape=[jax.ShapeDtypeStruct((T, N), out_dtype)] + [jax.ShapeDtypeStruct((N, T), out_dtype)] * also_transposed,
        compiler_params=_cp(1),
    )(x, w, scale)
    return res if also_transposed else res[0]


def _log_sigmoid(z):
    return jnp.minimum(z, 0.0) - jnp.log1p(jnp.exp(-jnp.abs(z)))


def _tri(n, lower):
    r = lax.broadcasted_iota(jnp.int32, (n, n), 0)
    c = lax.broadcasted_iota(jnp.int32, (n, n), 1)
    return jnp.where((c <= r) if lower else (c >= r), 1.0, 0.0).astype(F32)


def _f32dot(a, b):
    return jnp.dot(a, b, preferred_element_type=F32, precision=lax.Precision.HIGHEST)


def _forget_cumsum(flog, col, bf):
    T = flog.shape[0]
    bt = _tile(T, SCAN_BLOCK)

    def body(f_ref, b_ref, c_ref, carry):
        @pl.when(pl.program_id(0) == 0)
        def _():
            carry[...] = jnp.zeros_like(carry)

        lf = _log_sigmoid(f_ref[...] + b_ref[...])
        c = _f32dot(_tri(bt, True), lf) + carry[...]
        c_ref[...] = c * LOG2E
        carry[...] = c[bt - 1:bt, :]

    return pl.pallas_call(
        body, name="forget_cumsum", grid=(T // bt,),
        in_specs=[pl.BlockSpec((bt, N_FLOG), lambda i: (i, col)), _resident((1, N_FLOG))],
        out_specs=pl.BlockSpec((bt, N_FLOG), lambda i: (i, 0)),
        out_shape=jax.ShapeDtypeStruct((T, N_FLOG), F32),
        scratch_shapes=[pltpu.VMEM((1, N_FLOG), F32)],
        compiler_params=_cp(1),
    )(flog, bf)


def _forget_bwd(dck, dcq, flog, col, bf):
    T = dcq.shape[0]
    bt = _tile(T, SCAN_BLOCK)
    nb = T // bt

    def body(k0_ref, k1_ref, k2_ref, k3_ref, dcq_ref, f_ref, b_ref, dz_ref, db_ref, carry):
        @pl.when(pl.program_id(0) == 0)
        def _():
            carry[...] = jnp.zeros_like(carry)
            db_ref[...] = jnp.zeros_like(db_ref)

        dc = ((k0_ref[...] + k1_ref[...]) + (k2_ref[...] + k3_ref[...])) + dcq_ref[...]
        dlf = _f32dot(_tri(bt, False), dc) + carry[...]
        carry[...] = dlf[0:1, :]
        z = f_ref[...] + b_ref[...]
        dz = dlf * _sigmoid(-z)
        dz_ref[...] = dz.astype(BF16)
        db_ref[...] += _rowsum(dz)

    slab = lambda j: pl.BlockSpec((None, bt, N_FLOG), lambda i: (j, nb - 1 - i, 0))
    return pl.pallas_call(
        body, name="forget_bwd", grid=(nb,),
        in_specs=[slab(0), slab(1), slab(2), slab(3),
                  pl.BlockSpec((bt, N_FLOG), lambda i: (nb - 1 - i, 0)),
                  pl.BlockSpec((bt, N_FLOG), lambda i: (nb - 1 - i, col)), _resident((1, N_FLOG))],
        out_specs=[pl.BlockSpec((bt, N_FLOG), lambda i: (nb - 1 - i, 0)), _resident((1, N_FLOG))],
        out_shape=[jax.ShapeDtypeStruct((T, N_FLOG), BF16), jax.ShapeDtypeStruct((1, N_FLOG), F32)],
        scratch_shapes=[pltpu.VMEM((1, N_FLOG), F32)],
        compiler_params=_cp(1),
    )(dck, dck, dck, dck, dcq, flog, bf)


def _head_masks():
    lane = lax.broadcasted_iota(jnp.int32, (1, LANE), 1)
    return lane < HEAD_DIM


def _split_heads(x2, is_a):
    zero = jnp.zeros_like(x2)
    return jnp.where(is_a, x2, zero), jnp.where(is_a, zero, x2)


BIAS_PARTS = 3
ATTN_GROUP = 4
ATTN_FWD_GROUP = 8


def _bias_lanes(h):
    lane = lax.broadcasted_iota(jnp.int32, (1, LANE), 1)
    first = (1 - h) * HEAD_DIM
    return lane, first


def _fold_key_bias(qkv, c):
    T = qkv.shape[0]
    tm = _tile(T, TOKEN_TILE)
    npair = N_HEADS // 2

    def body(k_ref, c_ref, o_ref):
        cc = c_ref[...]
        parts, rest = [], cc
        for _ in range(BIAS_PARTS):
            piece = rest.astype(BF16)
            parts.append(piece)
            rest = rest - piece.astype(F32)
        for j in range(npair):
            k2 = k_ref[:, j * LANE:(j + 1) * LANE]
            for h in range(2):
                lane, first = _bias_lanes(h)
                out = k2
                for n, piece in enumerate(parts):
                    col = piece[:, 2 * j + h:2 * j + h + 1]
                    out = jnp.where(lane == first + n, col, out)
                o_ref[:, (2 * j + h) * LANE:(2 * j + h + 1) * LANE] = out

    return pl.pallas_call(
        body, name="fold_key_bias", grid=(T // tm,),
        in_specs=[pl.BlockSpec((tm, D_ATTN), lambda i: (i, 1)), pl.BlockSpec((tm, N_FLOG), lambda i: (i, 0))],
        out_specs=pl.BlockSpec((tm, 2 * D_ATTN), lambda i: (i, 0)),
        out_shape=jax.ShapeDtypeStruct((T, 2 * D_ATTN), BF16),
        compiler_params=_cp(1),
    )(qkv, c)


def _attn_fwd(qkv, vt, kb):
    T = qkv.shape[0]
    tq = _tile(T, ATTN_TILE)
    tk = tq
    nq = T // tq
    npair = N_HEADS // 2

    def body(qt_ref, ka_ref, kb_ref, vt_ref, o_ref, al_ref, m_s, l_s, acc_s):
        i = pl.program_id(1)
        dim = lax.broadcasted_iota(jnp.int32, (LANE, 1), 0)
        qs = []
        for h, qh in enumerate(_split_heads(qt_ref[...], dim < HEAD_DIM)):
            first = (1 - h) * HEAD_DIM
            qs.append(jnp.where((dim >= first) & (dim < first + BIAS_PARTS), -1.0, qh).astype(BF16))
        k_refs = (ka_ref, kb_ref)
        m_s[...] = jnp.full_like(m_s, NEG_INF)
        l_s[...] = jnp.zeros_like(l_s)
        acc_s[...] = jnp.zeros_like(acc_s)

        def scores_at(kk):
            k0 = pl.multiple_of(kk * tk, tk)
            return tuple(_nn(k_refs[h][pl.ds(k0, tk), :], qs[h]) for h in range(2))

        def consume(kk, scores, masked):
            k0 = pl.multiple_of(kk * tk, tk)
            v2t = vt_ref[:, pl.ds(k0, tk)]
            for h in range(2):
                zt = scores[h]
                if masked:
                    rr = lax.broadcasted_iota(jnp.int32, (tk, tq), 0)
                    cc = lax.broadcasted_iota(jnp.int32, (tk, tq), 1)
                    zt = jnp.where(cc >= rr, zt, NEG_INF)
                m_old = m_s[h]
                m_new = jnp.maximum(m_old, jnp.max(zt, axis=0, keepdims=True))
                p = jnp.exp2(zt - m_new)
                a = jnp.exp2(m_old - m_new)
                l_s[h] = a * l_s[h] + jnp.sum(p, axis=0, keepdims=True)
                acc_s[h] = a * acc_s[h] + _nn(v2t, p.astype(BF16))
                m_s[h] = m_new

        def group(kk, n, last_masked):
            scores = [scores_at(kk + u) for u in range(n)]
            for u in range(n):
                consume(kk + u, scores[u], last_masked and u == n - 1)

        def loop_body(t, carry):
            group(ATTN_FWD_GROUP * t, ATTN_FWD_GROUP, False)
            return carry

        lax.fori_loop(0, i // ATTN_FWD_GROUP, loop_body, 0)
        for left in range(ATTN_FWD_GROUP):
            @pl.when(i % ATTN_FWD_GROUP == left)
            def _():
                group(i - left, left + 1, True)

        outs = []
        for h in range(2):
            l = l_s[h]
            outs.append(acc_s[h] * (1.0 / l))
            al_ref[0, h:h + 1, :] = -(m_s[h] + jnp.log2(l))
        al_ref[0, 2:8, :] = jnp.zeros((6, tq), F32)
        dim = lax.broadcasted_iota(jnp.int32, (LANE, 1), 0)
        o_ref[...] = jnp.where(dim < HEAD_DIM, outs[0], outs[1]).T

    rowl = pl.BlockSpec((1, 8, tq), lambda j, i: (j, 0, i))
    return pl.pallas_call(
        body, name="attn_fwd", grid=(npair, nq),
        in_specs=[pl.BlockSpec((LANE, tq), lambda j, i: (j, i)),
                  pl.BlockSpec((T, LANE), lambda j, i: (0, 2 * j), pipeline_mode=pl.Buffered(1)),
                  pl.BlockSpec((T, LANE), lambda j, i: (0, 2 * j + 1), pipeline_mode=pl.Buffered(1)),
                  pl.BlockSpec((LANE, T), lambda j, i: (2 * npair + j, 0), pipeline_mode=pl.Buffered(1))],
        out_specs=[pl.BlockSpec((tq, LANE), lambda j, i: (i, j)), rowl],
        out_shape=[jax.ShapeDtypeStruct((T, D_ATTN), F32), jax.ShapeDtypeStruct((npair, 8, T), F32)],
        scratch_shapes=[pltpu.VMEM((2, 1, tq), F32), pltpu.VMEM((2, 1, tq), F32), pltpu.VMEM((2, LANE, tq), F32)],
        compiler_params=_cp(2),
    )(vt, kb, kb, vt)


def _attn_bwd(qkv, qkvt, dobt, cb, alrow, dlrow):
    T = qkv.shape[0]
    tq = _tile(T, ATTN_TILE)
    tk = tq
    nq = T // tq
    npair = N_HEADS // 2

    def body(qt_ref, k_ref, kt_ref, v_ref, dot_ref, cb_ref, al_ref, dl_ref,
             dq_ref, dk_ref, dv_ref, dc_ref, dcq_ref, dk_s, dv_s, dc_s):
        kj = pl.program_id(1)
        is_a = _head_masks()
        ks = _split_heads(k_ref[...], is_a)
        vs = _split_heads(v_ref[...], is_a)
        dim_a = lax.broadcasted_iota(jnp.int32, (LANE, 1), 0) < HEAD_DIM
        kts = _split_heads(kt_ref[...], dim_a)
        head_lane = lax.broadcasted_iota(jnp.int32, (1, LANE), 1) - 2 * pl.program_id(0)
        cs = tuple(jnp.sum(jnp.where(head_lane == h, cb_ref[...], 0.0), axis=-1, keepdims=True) for h in range(2))

        @pl.when(kj == 0)
        def _():
            dq_ref[...] = jnp.zeros_like(dq_ref)
            dcq_ref[...] = jnp.zeros_like(dcq_ref)

        dk_s[...] = jnp.zeros_like(dk_s)
        dv_s[...] = jnp.zeros_like(dv_s)
        dc_s[...] = jnp.zeros_like(dc_s)

        def block(qi, k_off, nk, q_off, nqs, masked):
            q0 = pl.multiple_of(qi * tq + q_off, nqs)
            rows = slice(k_off, k_off + nk)
            qt2 = qt_ref[:, pl.ds(q0, nqs)]
            dot2 = dot_ref[:, pl.ds(q0, nqs)]
            for h in range(2):
                alr = al_ref[0, h:h + 1, pl.ds(q0, nqs)]
                dlr = dl_ref[0, h:h + 1, pl.ds(q0, nqs)]
                zt = _nn(ks[h][rows], qt2) + (alr - cs[h][rows])
                if masked:
                    rr = lax.broadcasted_iota(jnp.int32, (nk, nqs), 0)
                    cc = lax.broadcasted_iota(jnp.int32, (nk, nqs), 1)
                    zt = jnp.where(cc >= rr, zt, NEG_INF)
                pt = jnp.exp2(zt)
                dst = pt * (_nn(vs[h][rows], dot2) - dlr)
                pb = pt.astype(BF16)
                dsb = dst.astype(BF16)
                dv_s[h, :, rows] += _nt(dot2, pb)
                dk_s[h, :, rows] += _nt(qt2, dsb)
                dc_s[h, rows, :] += jnp.sum(dst, axis=-1, keepdims=True)
                dcq_ref[0, h:h + 1, pl.ds(q0, nqs)] += jnp.sum(dst, axis=0, keepdims=True)
                dq_ref[:, pl.ds(q0, nqs)] += _nn(kts[h][:, rows], dsb)

        def step(qi, masked):
            if not masked:
                block(qi, 0, tk, 0, tq, False)
            elif tk % 256:
                block(qi, 0, tk, 0, tq, True)
            else:
                block(qi, 0, tk // 2, 0, tq, True)
                block(qi, tk // 2, tk // 2, tq // 2, tq // 2, True)

        rest = nq - 1 - kj
        for left in range(ATTN_GROUP):
            @pl.when(rest % ATTN_GROUP == left)
            def _():
                for u in range(left + 1):
                    step(kj + u, u == 0)

        def loop_body(t, carry):
            for u in range(ATTN_GROUP):
                step(kj + 1 + rest % ATTN_GROUP + ATTN_GROUP * t + u, False)
            return carry

        lax.fori_loop(0, rest // ATTN_GROUP, loop_body, 0)
        dk_ref[...] = (jnp.where(dim_a, dk_s[0], dk_s[1]) * (1.0 / LOG2E)).astype(BF16)
        dv_ref[...] = jnp.where(dim_a, dv_s[0], dv_s[1]).astype(BF16)
        lane = lax.broadcasted_iota(jnp.int32, (1, LANE), 1)
        head = 2 * pl.program_id(0)
        dc_ref[...] = jnp.where(lane == head, -dc_s[0], jnp.where(lane == head + 1, -dc_s[1], 0.0))

        @pl.when(kj == nq - 1)
        def _():
            dq_ref[...] = dq_ref[...] * Q_SCALE

    fullt = lambda row: pl.BlockSpec((LANE, T), lambda j, kj: (row(j), 0), pipeline_mode=pl.Buffered(1))
    tile = lambda col: pl.BlockSpec((tk, LANE), lambda j, kj: (kj, col(j)))
    tilet = lambda row: pl.BlockSpec((LANE, tk), lambda j, kj: (row(j), kj))
    rowl = pl.BlockSpec((1, 8, T), lambda j, kj: (j, 0, 0))
    return pl.pallas_call(
        body, name="attn_bwd", grid=(npair, nq),
        in_specs=[fullt(lambda j: j), tile(lambda j: npair + j), tilet(lambda j: npair + j),
                  tile(lambda j: 2 * npair + j), fullt(lambda j: j), tile(lambda j: 0), rowl, rowl],
        out_specs=[pl.BlockSpec((LANE, T), lambda j, kj: (j, 0)), tilet(lambda j: j), tilet(lambda j: j),
                   pl.BlockSpec((None, tk, LANE), lambda j, kj: (j, kj, 0)), rowl],
        out_shape=[jax.ShapeDtypeStruct((D_ATTN, T), F32), jax.ShapeDtypeStruct((D_ATTN, T), BF16),
                   jax.ShapeDtypeStruct((D_ATTN, T), BF16), jax.ShapeDtypeStruct((npair, T, LANE), F32),
                   jax.ShapeDtypeStruct((npair, 8, T), F32)],
        scratch_shapes=[pltpu.VMEM((2, LANE, tk), F32), pltpu.VMEM((2, LANE, tk), F32), pltpu.VMEM((2, tk, 1), F32)],
        compiler_params=_cp(2),
    )(qkvt, qkv, qkvt, qkv, dobt, cb, alrow, dlrow)


HALO = 8


def _shift_rows(cur, other, k, tm, down):
    row = lax.broadcasted_iota(jnp.int32, (tm, 1), 0)
    reps = tm // HALO
    if down:
        rolled = pltpu.roll(cur, k, 0)
        fill = jnp.tile(pltpu.roll(other, k, 0), (reps, 1))
        return jnp.where(row < k, fill, rolled)
    rolled = pltpu.roll(cur, tm - k, 0)
    fill = jnp.tile(pltpu.roll(other, HALO - k, 0), (reps, 1))
    return jnp.where(row >= tm - k, fill, rolled)


def _conv_fwd(c, hh, c_prev, hh_prev, w_ref, first, tm):
    u = c * hh
    u_prev = jnp.where(first, 0.0, c_prev * hh_prev)
    u1 = _shift_rows(u, u_prev, 1, tm, True)
    u2 = _shift_rows(u, u_prev, 2, tm, True)
    y = w_ref[0:1, :] * u2 + w_ref[1:2, :] * u1 + w_ref[2:3, :] * u
    return u, u1, u2, y


def _rms(x, g):
    rs = lax.rsqrt(jnp.mean(x * x, axis=-1, keepdims=True) + RMS_EPS)
    return x * rs * g, rs


def _mixer_tail_fwd(o, bchf, conv_w, g_attn, g_conv, w_mo, x1, lg, lb):
    T = o.shape[0]
    tm = _tile(T, TOKEN_TILE)
    hb = tm // HALO

    def body(o_ref, b_ref, c_ref, h_ref, cp_ref, hp_ref, w_ref, ga_ref, gc_ref, wmo_ref, x1_ref, lg_ref, lb_ref,
             x2_ref, r2_ref, mg_ref):
        first = pl.program_id(0) == 0
        _, _, _, y = _conv_fwd(c_ref[...], h_ref[...], cp_ref[...], hp_ref[...], w_ref, first, tm)
        na, _ = _rms(o_ref[...], ga_ref[...])
        nc, _ = _rms(b_ref[...] * y, gc_ref[...])
        nab = na.astype(BF16)
        ncb = nc.astype(BF16)
        mg_ref[:, 0:D_ATTN] = nab
        mg_ref[:, D_ATTN:] = ncb
        r2 = ALPHA * x1_ref[...] + _nn(nab, wmo_ref[0:D_ATTN, :]) + _nn(ncb, wmo_ref[D_ATTN:, :])
        r2_ref[...] = r2
        xhat, _ = _ln_stats(r2)
        x2_ref[...] = xhat * lg_ref[...] + lb_ref[...]

    row = lambda n, col=0: pl.BlockSpec((tm, n), lambda i: (i, col))
    prev = lambda col: pl.BlockSpec((HALO, D_CONV), lambda i: (jnp.maximum(i * hb - 1, 0), col))
    return pl.pallas_call(
        body, name="mixer_tail_fwd", grid=(T // tm,),
        in_specs=[row(D_ATTN), row(D_CONV, 0), row(D_CONV, 1), row(D_CONV, 2), prev(1), prev(2),
                  _resident((3, D_CONV)), _resident((1, D_ATTN)), _resident((1, D_CONV)),
                  _resident((D_MODEL, D_MODEL)), row(D_MODEL), _resident((1, D_MODEL)), _resident((1, D_MODEL))],
        out_specs=[row(D_MODEL), row(D_MODEL), row(D_MODEL)],
        out_shape=[jax.ShapeDtypeStruct((T, D_MODEL), F32), jax.ShapeDtypeStruct((T, D_MODEL), F32),
                   jax.ShapeDtypeStruct((T, D_MODEL), BF16)],
        compiler_params=_cp(1),
    )(o, bchf, bchf, bchf, bchf, bchf, conv_w, g_attn, g_conv, w_mo, x1, lg, lb)


def _head_sum_rows():
    row = lax.broadcasted_iota(jnp.int32, (4 * 8, D_ATTN), 0)
    head = lax.broadcasted_iota(jnp.int32, (4 * 8, D_ATTN), 1) // HEAD_DIM
    return jnp.where((row % 8 < 2) & (2 * (row // 8) + row % 8 == head), 1.0, 0.0).astype(F32)


def _mixer_tail_bwd(dx2, r2, lg, w_mo, o, bchf, conv_w, g_attn, g_conv, swap=()):
    T = o.shape[0]
    tm = _tile(T, TOKEN_TILE)
    hb = tm // HALO
    ns = len(swap)
    last = T // tm - 1

    def body(dx2_ref, r2_ref, lg_ref, wmo_ref, o_ref, b_ref, c_ref, h_ref, cp_ref, hp_ref, w_ref, ga_ref, gc_ref,
             *rest):
        comm_in = rest[:ns]
        dx1_ref, dr_ref, dot_ref, dl_ref, dco_ref, dlg_ref, dlb_ref, dga_ref, dgc_ref = rest[ns:ns + 9]
        comm_out, sems = rest[ns + 9:2 * ns + 9], rest[2 * ns + 9:]
        i = pl.program_id(0)
        if ns:
            @pl.when(i == 0)
            def _():
                _swap_start(comm_in, comm_out, *sems)

            @pl.when(i == last)
            def _():
                _swap_finish(comm_in, comm_out, *sems)

        @pl.when(i == 0)
        def _():
            for ref in (dlg_ref, dlb_ref, dga_ref, dgc_ref):
                ref[...] = jnp.zeros_like(ref)

        dy = dx2_ref[...]
        xhat, rstd = _ln_stats(r2_ref[...])
        dr = _ln_bwd(dy, xhat, rstd, lg_ref[...])
        dlg_ref[...] += _rowsum(dy * xhat)
        dlb_ref[...] += _rowsum(dy)
        dx1_ref[...] = ALPHA * dr
        drb = dr.astype(BF16)
        dr_ref[...] = drb
        dna = _nt(drb, wmo_ref[0:D_ATTN, :])
        dnc = _nt(drb, wmo_ref[D_ATTN:, :])

        def rms_bwd(x, g, dn):
            rs = lax.rsqrt(jnp.mean(x * x, axis=-1, keepdims=True) + RMS_EPS)
            dng = dn * g
            dx = rs * dng - x * (rs * rs * rs) * jnp.mean(dng * x, axis=-1, keepdims=True)
            return dx, _rowsum(dn * x * rs)

        oo = o_ref[...]
        do, dga = rms_bwd(oo, ga_ref[...], dna)
        dga_ref[...] += dga
        dot_ref[...] = do.T.astype(BF16)
        dl_ref[...] = lax.dot_general(_head_sum_rows(), do * oo, (((1,), (1,)), ((), ())),
                                      preferred_element_type=F32, precision=lax.Precision.HIGHEST)
        _, _, _, y = _conv_fwd(c_ref[...], h_ref[...], cp_ref[...], hp_ref[...], w_ref, i == 0, tm)
        dco, dgc = rms_bwd(b_ref[...] * y, gc_ref[...], dnc)
        dgc_ref[...] += dgc
        dco_ref[...] = dco

    row = lambda n, col=0: pl.BlockSpec((tm, n), lambda i: (i, col))
    prev = lambda col: pl.BlockSpec((HALO, D_CONV), lambda i: (jnp.maximum(i * hb - 1, 0), col))
    vec = lambda n: _resident((1, n))
    return pl.pallas_call(
        body, name="mixer_tail_bwd", grid=(T // tm,),
        in_specs=[row(D_MODEL), row(D_MODEL), vec(D_MODEL), _resident((D_MODEL, D_MODEL)), row(D_ATTN),
                  row(D_CONV, 0), row(D_CONV, 1), row(D_CONV, 2), prev(1), prev(2), _resident((3, D_CONV)),
                  vec(D_ATTN), vec(D_CONV)] + [ANY_SPEC] * ns,
        out_specs=[row(D_MODEL), row(D_MODEL), pl.BlockSpec((D_ATTN, tm), lambda i: (0, i)),
                   pl.BlockSpec((4 * 8, tm), lambda i: (0, i)), row(D_CONV),
                   vec(D_MODEL), vec(D_MODEL), vec(D_ATTN), vec(D_CONV)] + [ANY_SPEC] * ns,
        out_shape=[jax.ShapeDtypeStruct((T, D_MODEL), F32), jax.ShapeDtypeStruct((T, D_MODEL), BF16),
                   jax.ShapeDtypeStruct((D_ATTN, T), BF16),
                   jax.ShapeDtypeStruct((4 * 8, T), F32),
                   jax.ShapeDtypeStruct((T, D_CONV), F32), jax.ShapeDtypeStruct((1, D_MODEL), F32),
                   jax.ShapeDtypeStruct((1, D_MODEL), F32), jax.ShapeDtypeStruct((1, D_ATTN), F32),
                   jax.ShapeDtypeStruct((1, D_CONV), F32)] + _swap_shapes(swap),
        scratch_shapes=_swap_sems(ns) if ns else [],
        compiler_params=_cp(1),
    )(dx2, r2, lg, w_mo, o, bchf, bchf, bchf, bchf, bchf, conv_w, g_attn, g_conv, *swap)


def _conv_bwd(dco, bchf, conv_w):
    T = dco.shape[0]
    tm = _tile(T, TOKEN_TILE)
    hb = tm // HALO
    nt = T // tm

    def body(dco_ref, dcon_ref, b_ref, bn_ref, c_ref, h_ref, cp_ref, hp_ref, w_ref, dbch_ref, dw_ref):
        i = pl.program_id(0)

        @pl.when(i == 0)
        def _():
            dw_ref[...] = jnp.zeros_like(dw_ref)

        cc = c_ref[...]
        hh = h_ref[...]
        u, u1, u2, y = _conv_fwd(cc, hh, cp_ref[...], hp_ref[...], w_ref, i == 0, tm)
        dco = dco_ref[...]
        bb = b_ref[...]
        dyc = dco * bb
        dy_next = jnp.where(i == nt - 1, 0.0, dcon_ref[...] * bn_ref[...])
        d1 = _shift_rows(dyc, dy_next, 1, tm, False)
        d2 = _shift_rows(dyc, dy_next, 2, tm, False)
        du = w_ref[2:3, :] * dyc + w_ref[1:2, :] * d1 + w_ref[0:1, :] * d2
        dbch_ref[:, 0:D_CONV] = (dco * y).astype(BF16)
        dbch_ref[:, D_CONV:2 * D_CONV] = (du * hh).astype(BF16)
        dbch_ref[:, 2 * D_CONV:] = (du * cc).astype(BF16)
        dw_ref[0:1, :] += _rowsum(dyc * u2)
        dw_ref[1:2, :] += _rowsum(dyc * u1)
        dw_ref[2:3, :] += _rowsum(dyc * u)

    row = lambda n, col=0: pl.BlockSpec((tm, n), lambda i: (i, col))
    prev = lambda col: pl.BlockSpec((HALO, D_CONV), lambda i: (jnp.maximum(i * hb - 1, 0), col))
    nxt = lambda col: pl.BlockSpec((HALO, D_CONV), lambda i: (jnp.minimum((i + 1) * hb, T // HALO - 1), col))
    return pl.pallas_call(
        body, name="conv_bwd", grid=(nt,),
        in_specs=[row(D_CONV), nxt(0), row(D_CONV, 0), nxt(0), row(D_CONV, 1), row(D_CONV, 2), prev(1), prev(2),
                  _resident((3, D_CONV))],
        out_specs=[row(3 * D_CONV), _resident((8, D_CONV))],
        out_shape=[jax.ShapeDtypeStruct((T, 3 * D_CONV), BF16), jax.ShapeDtypeStruct((8, D_CONV), F32)],
        compiler_params=_cp(1),
    )(dco, dco, bchf, bchf, bchf, bchf, bchf, bchf, conv_w)


def _mixer_in_bwd(dx1a, dqt, dkt, dvt, dbch, dfl, w_qkvt, w_bch, w_f):
    T = dx1a.shape[0]
    tm = _tile(T, TOKEN_TILE)

    def body(a_ref, dq_ref, dk_ref, dv_ref, db_ref, df_ref, wq_ref, wb_ref, wf_ref, o_ref):
        acc = a_ref[...] + _nt(db_ref[...], wb_ref[...]) + _nt(df_ref[...], wf_ref[...])
        for n, ref in enumerate((dq_ref, dk_ref, dv_ref)):
            acc = acc + _tn(ref[...].astype(BF16), wq_ref[n * D_ATTN:(n + 1) * D_ATTN, :])
        o_ref[...] = acc

    row = lambda n: pl.BlockSpec((tm, n), lambda i: (i, 0))
    col = pl.BlockSpec((D_ATTN, tm), lambda i: (0, i))
    return pl.pallas_call(
        body, name="mixer_in_bwd", grid=(T // tm,),
        in_specs=[row(D_MODEL), col, col, col, row(3 * D_CONV), row(N_FLOG),
                  _resident((3 * D_ATTN, D_MODEL)), _resident((D_MODEL, 3 * D_CONV)), _resident((D_MODEL, N_FLOG))],
        out_specs=row(D_MODEL),
        out_shape=jax.ShapeDtypeStruct((T, D_MODEL), F32),
        compiler_params=_cp(1),
    )(dx1a, dqt, dkt, dvt, dbch, dfl, w_qkvt, w_bch, w_f)


def _ple_loss(x3, p, tgt, w_g, w_p, b_g, lg, lb):
    T = x3.shape[0]
    tm = _tile(T, TOKEN_TILE)

    def body(x_ref, p_ref, t_ref, wg_ref, wp_ref, bg_ref, lg_ref, lb_ref,
             dx_ref, de_ref, dz_ref, loss_ref, dlg_ref, dlb_ref, dbg_ref):
        @pl.when(pl.program_id(0) == 0)
        def _():
            for ref in (loss_ref, dlg_ref, dlb_ref, dbg_ref):
                ref[...] = jnp.zeros_like(ref)

        xf = x_ref[...]
        gate = _sigmoid(_nn(xf.astype(BF16), wg_ref[...]) + bg_ref[...])
        e = _nn(p_ref[...].astype(BF16), wp_ref[...])
        xhat, rstd = _ln_stats(ALPHA * xf + gate * e)
        err = xhat * lg_ref[...] + lb_ref[...] - t_ref[...]
        sq = jnp.sum(_rowsum(err * err), axis=-1, keepdims=True)
        loss_ref[...] += jnp.broadcast_to(sq * (0.5 / D_MODEL), loss_ref.shape)
        dy = err * (1.0 / D_MODEL)
        dr = _ln_bwd(dy, xhat, rstd, lg_ref[...])
        dlg_ref[...] += _rowsum(dy * xhat)
        dlb_ref[...] += _rowsum(dy)
        de_ref[...] = (dr * gate).astype(BF16)
        dz = dr * e * gate * (1.0 - gate)
        dbg_ref[...] += _rowsum(dz)
        dzb = dz.astype(BF16)
        dz_ref[...] = dzb
        dx_ref[...] = ALPHA * dr + _nt(dzb, wg_ref[...])

    row = lambda n: pl.BlockSpec((tm, n), lambda i: (i, 0))
    vec = lambda n: _resident((1, n))
    return pl.pallas_call(
        body, name="ple_loss", grid=(T // tm,),
        in_specs=[row(D_MODEL), row(PLE_DIM), row(D_MODEL), _resident((D_MODEL, D_MODEL)),
                  _resident((PLE_DIM, D_MODEL)), vec(D_MODEL), vec(D_MODEL), vec(D_MODEL)],
        out_specs=[row(D_MODEL), row(D_MODEL), row(D_MODEL), vec(LANE), vec(D_MODEL), vec(D_MODEL), vec(D_MODEL)],
        out_shape=[jax.ShapeDtypeStruct((T, D_MODEL), F32), jax.ShapeDtypeStruct((T, D_MODEL), BF16),
                   jax.ShapeDtypeStruct((T, D_MODEL), BF16), jax.ShapeDtypeStruct((1, LANE), F32),
                   jax.ShapeDtypeStruct((1, D_MODEL), F32), jax.ShapeDtypeStruct((1, D_MODEL), F32),
                   jax.ShapeDtypeStruct((1, D_MODEL), F32)],
        compiler_params=_cp(1),
    )(x3, p, tgt, w_g, w_p, b_g, lg, lb)


def _from_row_layout(vr):
    return vr[:, :2, :].reshape(N_HEADS, -1).T


def _local_step(x, p, tgt, w, overlap=None):
    bf = lambda a: a.astype(BF16)
    w1i, w1o = bf(w["ffn1_w_in"]), bf(w["ffn1_w_out"])
    first = _ffn_fwd(x, w1i, w1o, w["ln1_g"], w["ln1_b"], "ffn1_fwd", overlap["gather"] if overlap else ())
    x1, r1, g1, u1, h1 = first[:5]
    if overlap:
        w = {**w, **overlap["weights"](first[5:])}
    w2i, w2o = bf(w["ffn2_w_in"]), bf(w["ffn2_w_out"])
    wmi = w["w_mix_in"]
    o_f = 3 * D_ATTN
    o_b = o_f + N_HEADS
    w_qkv = bf(wmi[:, :o_f])
    w_f = bf(jnp.pad(wmi[:, o_f:o_b], ((0, 0), (0, N_FLOG - N_HEADS))))
    w_bch = bf(wmi[:, o_b:])
    w_bchf = jnp.concatenate([w_bch, w_f], axis=1)
    w_mo, w_g, w_p = bf(w["w_mix_out"]), bf(w["w_ple_gate"]), bf(w["w_ple"])
    b_f = jnp.pad(w["b_forget"], ((0, 0), (0, N_FLOG - N_HEADS)))

    q_scale = jnp.concatenate([jnp.full((1, D_ATTN), Q_SCALE * LOG2E, F32), jnp.ones((1, 2 * D_ATTN), F32)], axis=1)
    qkv, qkvt = _matmul_nn(x1, w_qkv, q_scale, BF16, "proj_qkv", also_transposed=True)
    bchf = _matmul_nn(x1, w_bchf, jnp.ones((1, 3 * D_CONV + N_FLOG), F32), F32, "proj_bchf")
    fcol = 3 * D_CONV // N_FLOG
    c = _forget_cumsum(bchf, fcol, b_f)
    o, alrow = _attn_fwd(qkv, qkvt, _fold_key_bias(qkv, c))
    x2, r2, merged = _mixer_tail_fwd(o, bchf, w["conv_w"], w["g_attn"], w["g_conv"], w_mo, x1, w["ln2_g"], w["ln2_b"])
    x3, r3, g2, u2, h2 = _ffn_fwd(x2, w2i, w2o, w["ln3_g"], w["ln3_b"], "ffn2_fwd")

    grads = {}
    dx3, de, dz, loss, grads["ln4_g"], grads["ln4_b"], grads["b_ple_gate"] = _ple_loss(
        x3, p, tgt, w_g, w_p, w["b_ple_gate"], w["ln4_g"], w["ln4_b"])
    by_chip = overlap is not None
    grads["w_ple"] = _matmul_tn(p, de, "dw_ple")
    grads["w_ple_gate"] = _matmul_tn(x3, dz, "dw_ple_gate")

    dx2, dgu2, df2, grads["ln3_g"], grads["ln3_b"] = _ffn_bwd(dx3, r3, g2, u2, w2i, w2o, w["ln3_g"], "ffn2_bwd")
    grads["ffn2_w_in"] = _matmul_tn(x2, dgu2, "dw_ffn2_in", by_chip)
    grads["ffn2_w_out"] = _matmul_tn(h2, df2, "dw_ffn2_out")

    to_swap = overlap["swap"](grads) if overlap else ()
    tail = _mixer_tail_bwd(dx2, r2, w["ln2_g"], w_mo, o, bchf, w["conv_w"], w["g_attn"], w["g_conv"], to_swap)
    (dx1a, dr2, dobt, delta, dco, grads["ln2_g"], grads["ln2_b"], grads["g_attn"], grads["g_conv"]) = tail[:9]
    grads["w_mix_out"] = _matmul_tn(merged, dr2, "dw_mix_out")
    dbch, dcw = _conv_bwd(dco, bchf, w["conv_w"])
    grads["conv_w"] = dcw[:3]
    dqt, dkt, dvt, dck, dcq = _attn_bwd(qkv, qkvt, dobt, c, alrow, delta.reshape(N_HEADS // 2, 8, -1))
    dcq_lanes = jnp.pad(_from_row_layout(dcq), ((0, 0), (0, N_FLOG - N_HEADS)))
    dfl, dbf = _forget_bwd(dck, dcq_lanes, bchf, fcol, b_f)
    grads["b_forget"] = dbf[:, :N_HEADS]
    dx1 = _mixer_in_bwd(dx1a, dqt, dkt, dvt, dbch, dfl, w_qkv.T, w_bch, w_f)
    grads["w_mix_in"] = jnp.concatenate(
        [_matmul_tokens(dqt, x1, "dw_q").T, _matmul_tokens(dkt, x1, "dw_k").T, _matmul_tokens(dvt, x1, "dw_v").T,
         _matmul_tn(x1, dfl, "dw_flog")[:, :N_HEADS], _matmul_tn(x1, dbch, "dw_bch")], axis=1)

    dx0, dgu1, df1, grads["ln1_g"], grads["ln1_b"] = _ffn_bwd(dx1, r1, g1, u1, w1i, w1o, w["ln1_g"], "ffn1_bwd")
    grads["ffn1_w_out"] = _matmul_tn(h1, df1, "dw_ffn1_out")
    if not overlap:
        grads["ffn1_w_in"] = _matmul_tn(x, dgu1, "dw_ffn1_in")
        return loss, dx0, grads
    sums = overlap["chip_sums"](grads, to_swap, tail[9:])
    grads["ffn1_w_in"], *received = _matmul_tn(x, dgu1, "dw_ffn1_in", by_chip, exchange=sums)
    return loss, dx0, grads, sums, received


WEIGHTS = ["ffn1_w_in", "ffn1_w_out", "ln1_g", "ln1_b", "w_mix_in", "b_forget", "conv_w", "g_attn", "g_conv",
           "w_mix_out", "ln2_g", "ln2_b", "ffn2_w_in", "ffn2_w_out", "ln3_g", "ln3_b", "w_ple", "w_ple_gate",
           "b_ple_gate", "ln4_g", "ln4_b"]
LAYOUT = {
    "ffn1_w_in": ((D_MODEL, 2 * D_FF), 1), "ffn1_w_out": ((D_FF, D_MODEL), 0),
    "w_mix_in": ((D_MODEL, 3 * D_ATTN + N_HEADS + 3 * D_CONV), 1), "conv_w": ((3, D_CONV), 1),
    "w_mix_out": ((D_MODEL, D_MODEL), 0), "ffn2_w_in": ((D_MODEL, 2 * D_FF), 1), "ffn2_w_out": ((D_FF, D_MODEL), 0),
    "w_ple": ((PLE_DIM, D_MODEL), 1), "w_ple_gate": ((D_MODEL, D_MODEL), 0),
    "ln1_g": ((1, D_MODEL), None), "ln1_b": ((1, D_MODEL), None), "b_forget": ((1, N_HEADS), None),
    "g_attn": ((1, D_ATTN), None), "g_conv": ((1, D_CONV), None), "ln2_g": ((1, D_MODEL), None),
    "ln2_b": ((1, D_MODEL), None), "ln3_g": ((1, D_MODEL), None), "ln3_b": ((1, D_MODEL), None),
    "b_ple_gate": ((1, D_MODEL), None), "ln4_g": ((1, D_MODEL), None), "ln4_b": ((1, D_MODEL), None),
}
BIG = [n for n in WEIGHTS if LAYOUT[n][1] is not None and n != "conv_w"]
SMALL = [n for n in WEIGHTS if n not in BIG]
ROW = 1024
SMALL_ROWS = 16


def _shard_shape(name):
    shape, axis = LAYOUT[name]
    if axis is None:
        return shape
    return tuple(s // N_CHIPS if a == axis else s for a, s in enumerate(shape))


def _halves(a):
    return a.reshape(a.shape[:-2] + (2, a.shape[-2] // 2, a.shape[-1]))


def _split_chips(name, full):
    shape, axis = LAYOUT[name]
    if axis == 0:
        return full.reshape((N_CHIPS, shape[0] // N_CHIPS) + shape[1:])
    return jnp.moveaxis(full.reshape(shape[:1] + (N_CHIPS, shape[1] // N_CHIPS)), 1, 0)


def _join_chips(name, parts):
    shape, axis = LAYOUT[name]
    if axis == 0:
        return parts.reshape(shape)
    return jnp.moveaxis(parts, 0, 1).reshape(shape)


SMALL_AT = {"ln1_g": (0, 0), "ln1_b": (1, 0), "ln2_g": (2, 0), "ln2_b": (3, 0), "ln3_g": (4, 0), "ln3_b": (5, 0),
            "b_ple_gate": (6, 0), "ln4_g": (7, 0), "ln4_b": (8, 0), "g_attn": (9, 0), "g_conv": (9, D_ATTN),
            "b_forget": (10, 0), "conv_w": (10, LANE)}
CONV_SHARD = D_CONV // N_CHIPS


def _pack_small_grads(grads):
    def body(*refs):
        ins, o_ref = dict(zip(SMALL, refs[:-1])), refs[-1]
        o_ref[...] = jnp.zeros_like(o_ref)
        for s in range(N_CHIPS):
            for n in SMALL:
                r, c0 = SMALL_AT[n]
                if n == "conv_w":
                    for k in range(3):
                        o_ref[s, r:r + 1, c0 + k * CONV_SHARD:c0 + (k + 1) * CONV_SHARD] = (
                            ins[n][k:k + 1, s * CONV_SHARD:(s + 1) * CONV_SHARD])
                else:
                    o_ref[s, r:r + 1, c0:c0 + ins[n].shape[1]] = ins[n][...]

    return pl.pallas_call(
        body, name="pack_small_grads",
        out_shape=jax.ShapeDtypeStruct((N_CHIPS, SMALL_ROWS, ROW), F32),
    )(*[grads[n] for n in SMALL])


def _adamw_math(w, g, m, v):
    c1 = 1.0 - ADAM_B1 ** ADAM_STEP
    c2 = 1.0 - ADAM_B2 ** ADAM_STEP
    m = ADAM_B1 * m + (1.0 - ADAM_B1) * g
    v = ADAM_B2 * v + (1.0 - ADAM_B2) * (g * g)
    return -ADAM_LR * ((m / c1) / (jnp.sqrt(v / c2) + ADAM_EPS) + ADAM_WD * w), m, v


def _adamw_small(g_mine, g_sib, c_idx, w, m, v):
    ns = len(SMALL)

    def body(c_ref, gm_ref, gs_ref, *refs):
        ws, ms, vs = refs[:ns], refs[ns:2 * ns], refs[2 * ns:3 * ns]
        outs = refs[3 * ns:]
        mine_first = c_ref[0] == 0
        top = jnp.where(mine_first, gm_ref[...], gs_ref[...])
        bot = jnp.where(mine_first, gs_ref[...], gm_ref[...])
        for i, n in enumerate(SMALL):
            r, c0 = SMALL_AT[n]
            blk, rr = (top, r) if r < SMALL_ROWS // 2 else (bot, r - SMALL_ROWS // 2)
            rows, width = ws[i].shape
            for k in range(rows):
                g = blk[rr:rr + 1, c0 + k * width:c0 + (k + 1) * width]
                d, mn, vn = _adamw_math(ws[i][k:k + 1, :], g, ms[i][k:k + 1, :], vs[i][k:k + 1, :])
                for q, val in enumerate((g, d, mn, vn)):
                    outs[q * ns + i][k:k + 1, :] = val

    shapes = [jax.ShapeDtypeStruct(a.shape, F32) for a in w]
    vmem = pl.BlockSpec(memory_space=pltpu.VMEM)
    res = pl.pallas_call(
        body, name="adamw_small",
        in_specs=[pl.BlockSpec(memory_space=pltpu.SMEM)] + [vmem] * (2 + 3 * ns),
        out_specs=[vmem] * (4 * ns),
        out_shape=shapes * 4,
    )(c_idx, g_mine, g_sib, *w, *m, *v)
    return [res[q * ns:(q + 1) * ns] for q in range(4)]


def _place():
    x, y, c = lax.axis_index("x"), lax.axis_index("y"), lax.axis_index("c")
    others = [(1 - x, y), (x, 1 - y), (1 - x, 1 - y)]
    return x, y, c, others


ANY_SPEC = pl.BlockSpec(memory_space=pl.ANY)


def _remote(src, dst, send_sems, recv_sems, k, to):
    return pltpu.make_async_remote_copy(src_ref=src, dst_ref=dst, send_sem=send_sems.at[k], recv_sem=recv_sems.at[k],
                                        device_id=to, device_id_type=MESH)


def _all_gather(shards):
    n = len(shards)

    def body(*refs):
        ins, outs, send_sems, recv_sems = refs[:n], refs[n:2 * n], refs[2 * n], refs[2 * n + 1]
        _gather_start(ins, outs, send_sems, recv_sems)
        _gather_finish(ins, outs, send_sems, recv_sems)

    return pl.pallas_call(
        body, name="all_gather_weights",
        out_shape=_gather_shapes(shards), in_specs=[ANY_SPEC] * n, out_specs=[ANY_SPEC] * n,
        scratch_shapes=_gather_sems(n),
    )(*shards)


def _gather_shapes(shards):
    return [jax.ShapeDtypeStruct((N_CHIPS,) + a.shape, a.dtype) for a in shards]


def _gather_sems(n):
    return [pltpu.SemaphoreType.DMA((6 * n,)), pltpu.SemaphoreType.DMA((6 * n,))]


def _gather_sends(ins, outs, send_sems, recv_sems):
    x, y, c, others = _place()
    s = 2 * x + y
    return [_remote(ins[t].at[c], outs[t].at[s, c], send_sems, recv_sems, 6 * t + j, (*chip, c))
            for t in range(len(ins)) for j, chip in enumerate(others)]


def _gather_start(ins, outs, send_sems, recv_sems):
    for cp in _gather_sends(ins, outs, send_sems, recv_sems):
        cp.start()


def _gather_forwards(ins, outs, send_sems, recv_sems):
    x, y, c, others = _place()
    copies = []
    for t in range(len(ins)):
        for j, chip in enumerate(others):
            landed = outs[t].at[2 * chip[0] + chip[1], c]
            copies.append(_remote(landed, landed, send_sems, recv_sems, 6 * t + 3 + j, (x, y, 1 - c)))
    return copies


def _gather_forward(ins, outs, send_sems, recv_sems):
    x, y, c, others = _place()
    forwards = _gather_forwards(ins, outs, send_sems, recv_sems)
    for t in range(len(ins)):
        for j, chip in enumerate(others):
            landed = outs[t].at[2 * chip[0] + chip[1], c]
            _remote(landed, landed, send_sems, recv_sems, 6 * t + j, (x, y, c)).wait_recv()
            forwards[3 * t + j].start()


def _gather_drain(ins, outs, send_sems, recv_sems):
    x, y, c, others = _place()
    for t in range(len(ins)):
        for j, chip in enumerate(others):
            landed = outs[t].at[2 * chip[0] + chip[1], 1 - c]
            _remote(landed, landed, send_sems, recv_sems, 6 * t + 3 + j, (x, y, c)).wait_recv()
    for cp in _gather_sends(ins, outs, send_sems, recv_sems) + _gather_forwards(ins, outs, send_sems, recv_sems):
        cp.wait_send()


def _gather_finish(ins, outs, send_sems, recv_sems):
    _gather_forward(ins, outs, send_sems, recv_sems)
    _gather_drain(ins, outs, send_sems, recv_sems)


def _swap_halves(gs, tag):
    n = len(gs)

    def body(*refs):
        ins, outs, send_sems, recv_sems = refs[:n], refs[n:2 * n], refs[2 * n], refs[2 * n + 1]
        _swap_start(ins, outs, send_sems, recv_sems)
        _swap_finish(ins, outs, send_sems, recv_sems)

    return pl.pallas_call(
        body, name="grad_swap_halves_" + tag,
        out_shape=_swap_shapes(gs), in_specs=[ANY_SPEC] * n, out_specs=[ANY_SPEC] * n,
        scratch_shapes=_swap_sems(n),
    )(*gs)


def _swap_shapes(gs):
    return [jax.ShapeDtypeStruct(g.shape[:1] + g.shape[2:], g.dtype) for g in gs]


def _swap_sems(n):
    return [pltpu.SemaphoreType.DMA((n,)), pltpu.SemaphoreType.DMA((n,))]


def _swap_copies(ins, outs, send_sems, recv_sems):
    x, y, c, _ = _place()
    return [_remote(ins[t].at[:, 1 - c], outs[t], send_sems, recv_sems, t, (x, y, 1 - c)) for t in range(len(ins))]


def _swap_start(ins, outs, send_sems, recv_sems):
    for cp in _swap_copies(ins, outs, send_sems, recv_sems):
        cp.start()


def _swap_finish(ins, outs, send_sems, recv_sems):
    for cp in _swap_copies(ins, outs, send_sems, recv_sems):
        cp.wait()


def _exchange_chips(pps):
    n = len(pps)

    def body(*refs):
        ins, outs, send_sems, recv_sems = refs[:n], refs[n:2 * n], refs[2 * n], refs[2 * n + 1]
        _exchange_start(ins, outs, send_sems, recv_sems)
        _exchange_finish(ins, outs, send_sems, recv_sems)

    return pl.pallas_call(
        body, name="grad_exchange_chips",
        out_shape=_exchange_shapes(pps), in_specs=[ANY_SPEC] * n, out_specs=[ANY_SPEC] * n,
        scratch_shapes=_exchange_sems(n),
    )(*pps)


def _exchange_shapes(pps):
    return [jax.ShapeDtypeStruct(p.shape, p.dtype) for p in pps]


def _exchange_sems(n):
    return [pltpu.SemaphoreType.DMA((3 * n,)), pltpu.SemaphoreType.DMA((3 * n,))]


def _exchange_sends(ins, outs, send_sems, recv_sems):
    x, y, c, others = _place()
    s = 2 * x + y
    return [_remote(ins[t].at[2 * chip[0] + chip[1]], outs[t].at[s], send_sems, recv_sems, 3 * t + j, (*chip, c))
            for t in range(len(ins)) for j, chip in enumerate(others)]


def _exchange_start(ins, outs, send_sems, recv_sems):
    for cp in _exchange_sends(ins, outs, send_sems, recv_sems):
        cp.start()


def _exchange_finish(ins, outs, send_sems, recv_sems):
    x, y, c, others = _place()
    for t in range(len(ins)):
        for j, chip in enumerate(others):
            landed = outs[t].at[2 * chip[0] + chip[1]]
            _remote(landed, landed, send_sems, recv_sems, 3 * t + j, (x, y, c)).wait_recv()
    for cp in _exchange_sends(ins, outs, send_sems, recv_sems):
        cp.wait_send()


def _share_half(rs):
    n = len(rs)

    def body(*refs):
        ins, outs, send_sems, recv_sems = refs[:n], refs[n:2 * n], refs[2 * n], refs[2 * n + 1]
        x, y, c, _ = _place()
        copies = [_remote(ins[t], outs[t], send_sems, recv_sems, t, (x, y, 1 - c)) for t in range(n)]
        for cp in copies:
            cp.start()
        for cp in copies:
            cp.wait()

    return pl.pallas_call(
        body, name="grad_share_half",
        out_shape=[jax.ShapeDtypeStruct(r.shape, r.dtype) for r in rs],
        in_specs=[ANY_SPEC] * n, out_specs=[ANY_SPEC] * n,
        scratch_shapes=[pltpu.SemaphoreType.DMA((n,)), pltpu.SemaphoreType.DMA((n,))],
    )(*rs)


ELEMENTWISE_BLOCK_BYTES = 1 << 20


def _row_tile(rows, cols):
    return _tile(rows, max(8, ELEMENTWISE_BLOCK_BYTES // (4 * cols) // 8 * 8))


def _add_my_half(g, sib, c_idx, name):
    rh, cols = g.shape[2:]
    tr = _row_tile(rh, cols)

    def body(c_ref, g_ref, s_ref, o_ref):
        o_ref[...] = (g_ref[...] + s_ref[...]).astype(BF16)

    return pl.pallas_call(
        body, name="grad_add_halves_" + name,
        grid_spec=pltpu.PrefetchScalarGridSpec(
            num_scalar_prefetch=1, grid=(N_CHIPS, rh // tr),
            in_specs=[pl.BlockSpec((None, None, tr, cols), lambda s, i, c: (s, c[0], i, 0)),
                      pl.BlockSpec((None, tr, cols), lambda s, i, c: (s, i, 0))],
            out_specs=pl.BlockSpec((None, tr, cols), lambda s, i, c: (s, i, 0))),
        out_shape=jax.ShapeDtypeStruct((N_CHIPS, rh, cols), BF16),
        compiler_params=_cp(2),
    )(c_idx, g, sib)


def _sum_chips(parts, pp, s_idx, name):
    rh, cols = parts.shape[1:]
    tr = _row_tile(rh, cols)

    def body(s_ref, p0, p1, p2, p3, mine_ref, o_ref):
        own = mine_ref[...]
        t = [jnp.where(s_ref[0] == k, own, p[...]).astype(F32) for k, p in enumerate((p0, p1, p2, p3))]
        o_ref[...] = ((t[0] + t[1]) + t[2]) + t[3]

    slot = lambda k: pl.BlockSpec((None, tr, cols), lambda i, s: (jnp.where(s[0] == k, (k + 1) % N_CHIPS, k), i, 0))
    return pl.pallas_call(
        body, name="grad_sum_chips_" + name,
        grid_spec=pltpu.PrefetchScalarGridSpec(
            num_scalar_prefetch=1, grid=(rh // tr,),
            in_specs=[slot(0), slot(1), slot(2), slot(3), pl.BlockSpec((None, tr, cols), lambda i, s: (s[0], i, 0))],
            out_specs=pl.BlockSpec((tr, cols), lambda i, s: (i, 0))),
        out_shape=jax.ShapeDtypeStruct((rh, cols), F32),
        compiler_params=_cp(1),
    )(s_idx, parts, parts, parts, parts, pp)


def _adamw(w, g_mine, g_sib, m, v, c_idx, name):
    rows, cols = w.shape
    tr = _row_tile(rows // 2, cols)
    nbh = rows // 2 // tr

    def body(c_ref, w_ref, gm_ref, gs_ref, m_ref, v_ref, g_ref, d_ref, mo_ref, vo_ref):
        g = jnp.where(pl.program_id(0) // nbh == c_ref[0], gm_ref[...], gs_ref[...])
        g_ref[...] = g
        d_ref[...], mo_ref[...], vo_ref[...] = _adamw_math(w_ref[...], g, m_ref[...], v_ref[...])

    spec = pl.BlockSpec((tr, cols), lambda i, c: (i, 0))
    half = pl.BlockSpec((tr, cols), lambda i, c: (i % nbh, 0))
    return pl.pallas_call(
        body, name="adamw_" + name,
        grid_spec=pltpu.PrefetchScalarGridSpec(
            num_scalar_prefetch=1, grid=(rows // tr,),
            in_specs=[spec, half, half, spec, spec], out_specs=[spec] * 4),
        out_shape=[jax.ShapeDtypeStruct(w.shape, F32)] * 4,
        compiler_params=_cp(1),
    )(c_idx, w, g_mine, g_sib, m, v)


def kernel(x, p, ffn1_w_in, ffn1_w_out, ln1_g, ln1_b, w_mix_in, b_forget, conv_w, g_attn, g_conv, w_mix_out, ln2_g, ln2_b, ffn2_w_in, ffn2_w_out, ln3_g, ln3_b, w_ple, w_ple_gate, b_ple_gate, ln4_g, ln4_b, loss_target, m_ffn1_w_in, m_ffn1_w_out, m_ln1_g, m_ln1_b, m_w_mix_in, m_b_forget, m_conv_w, m_g_attn, m_g_conv, m_w_mix_out, m_ln2_g, m_ln2_b, m_ffn2_w_in, m_ffn2_w_out, m_ln3_g, m_ln3_b, m_w_ple, m_w_ple_gate, m_b_ple_gate, m_ln4_g, m_ln4_b, v_ffn1_w_in, v_ffn1_w_out, v_ln1_g, v_ln1_b, v_w_mix_in, v_b_forget, v_conv_w, v_g_attn, v_g_conv, v_w_mix_out, v_ln2_g, v_ln2_b, v_ffn2_w_in, v_ffn2_w_out, v_ln3_g, v_ln3_b, v_w_ple, v_w_ple_gate, v_b_ple_gate, v_ln4_g, v_ln4_b):
    args = dict(locals())
    shard = {n: args[n][0] if LAYOUT[n][1] is not None else args[n] for n in WEIGHTS}
    m_shard = {n: args["m_" + n][0] if LAYOUT[n][1] is not None else args["m_" + n] for n in WEIGHTS}
    v_shard = {n: args["v_" + n][0] if LAYOUT[n][1] is not None else args["v_" + n] for n in WEIGHTS}
    c_idx = lax.axis_index("c").astype(jnp.int32).reshape(1)
    chip = (2 * lax.axis_index("x") + lax.axis_index("y")).astype(jnp.int32)

    conv_rows = SMALL_ROWS - shard["conv_w"].shape[0]
    mine = {n: _halves(shard[n].astype(BF16)) for n in BIG}
    mine["conv_w"] = _halves(jnp.pad(shard["conv_w"], ((0, conv_rows), (0, 0))))
    early_w = ["ffn1_w_in", "ffn1_w_out"]
    late_w = [n for n in BIG if n not in early_w] + ["conv_w"]

    def full_weights(names, gathered):
        out = {}
        for n, theirs in zip(names, gathered):
            g = lax.dynamic_update_slice(theirs, mine[n][None], (chip, 0, 0, 0))
            if n == "conv_w":
                out[n] = _join_chips(n, g.reshape(N_CHIPS, SMALL_ROWS, CONV_SHARD)[:, :3])
            else:
                out[n] = _join_chips(n, g.reshape((N_CHIPS,) + _shard_shape(n)))
        return out

    full = full_weights(early_w, _all_gather([mine[n] for n in early_w]))
    full.update({n: shard[n] for n in SMALL if n != "conv_w"})

    def per_chip(names, grads):
        by_chip = lambda n: grads[n] if grads[n].ndim == 3 else _split_chips(n, grads[n])
        return [_halves(_pack_small_grads(grads) if n == "small" else by_chip(n)) for n in names]

    def add_halves(names, mine_, sibs):
        return [_add_my_half(g, sib, c_idx, n) for n, g, sib in zip(names, mine_, sibs)]

    def chip_sums(names, grads):
        mine_ = per_chip(names, grads)
        return add_halves(names, mine_, _swap_halves(mine_, names[0]))

    ready_a = ["ffn2_w_in", "ffn2_w_out", "w_ple", "w_ple_gate"]
    ready_b = ["w_mix_in", "w_mix_out", "ffn1_w_out"]
    early_g = ready_a + ready_b
    late_g = ["ffn1_w_in", "small"]
    loss_acc, grad_x, grads, early_sums, early_parts = _local_step(
        x[0], p[0, 0], loss_target[0], full,
        overlap={"gather": [mine[n] for n in late_w], "weights": lambda gathered: full_weights(late_w, gathered),
                 "swap": lambda grads: per_chip(ready_a, grads),
                 "chip_sums": lambda grads, swapped, received: (add_halves(ready_a, swapped, received)
                                                                + chip_sums(ready_b, grads))})
    loss = lax.psum(loss_acc[0, 0], ("x", "y", "c"))

    late_sums = chip_sums(late_g, grads)
    names = early_g + late_g
    sums = list(early_sums) + late_sums
    parts = list(early_parts) + list(_exchange_chips(late_sums))
    half_of = {n: _sum_chips(pt, own, chip.reshape(1), n) for n, pt, own in zip(names, parts, sums)}
    names = BIG + ["small"]
    my_half = [half_of[n] for n in names]
    sib_half = _share_half(my_half)

    out = {}
    for n, gm, gs in zip(BIG, my_half, sib_half):
        out[n] = [a[None] for a in _adamw(shard[n], gm, gs, m_shard[n], v_shard[n], c_idx, n)]
    small = _adamw_small(my_half[-1], sib_half[-1], c_idx, [shard[n] for n in SMALL], [m_shard[n] for n in SMALL],
                         [v_shard[n] for n in SMALL])
    for i, n in enumerate(SMALL):
        out[n] = [small[q][i][None] if n == "conv_w" else small[q][i] for q in range(4)]
    return (loss, grad_x[None], *[out[n][q] for q in range(4) for n in WEIGHTS])
```
